```python
import jax, jax.numpy as jnp
from jax import lax
import numpy as np

D_MODEL = 2048
BATCH = 8
SEQ = 4096
DEPTH = 4

EXPAND = 2
D_INNER = EXPAND * D_MODEL
PLE_DIM = 256
N_MIXERS = 2
POOL_WINDOWS = (2, 4, 8, 16)
N_POOL_GROUPS = len(POOL_WINDOWS)
POOL_GROUP_DIM = D_INNER // N_POOL_GROUPS
LRU_HEADS = 16
LRU_BLOCK = D_INNER // LRU_HEADS
CONV_WIDTH = 4
LRU_C = 8.0
RMS_EPS = 1e-6
N_POOL_LAYERS = (DEPTH + 1) // 2
N_LRU_LAYERS = DEPTH // 2

kernel_name = "hybrid_pool_rglru_sandwich_ple"


def rmsnorm(x, g):
    xf = x.astype(jnp.float32)
    var = jnp.mean(xf * xf, axis=-1, keepdims=True)
    return (xf * lax.rsqrt(var + RMS_EPS) * g.astype(jnp.float32)).astype(x.dtype)


def pool_mixer(u, w_grp, b, scale):
    B, S, E = u.shape
    maxw = POOL_WINDOWS[-1]
    uf = u.astype(jnp.float32)
    cs = jnp.cumsum(uf, axis=1)
    csp = jnp.pad(cs, ((0, 0), (maxw, 0), (0, 0)))
    pos = jnp.arange(1, S + 1, dtype=jnp.int32)
    diffs = []
    for g, w in enumerate(POOL_WINDOWS):
        c0, c1 = g * POOL_GROUP_DIM, (g + 1) * POOL_GROUP_DIM
        win_sum = csp[:, maxw:, c0:c1] - csp[:, maxw - w:maxw - w + S, c0:c1]
        count = jnp.minimum(pos, w).astype(jnp.float32)[None, :, None]
        diffs.append(win_sum / count - uf[..., c0:c1])
    d = jnp.stack(diffs, axis=2).astype(u.dtype)
    y = jnp.einsum('bsgc,gcd->bsgd', d, w_grp).reshape(B, S, E) + b
    return y * scale


def causal_depthwise_conv(x, w, b):
    S = x.shape[1]
    xp = jnp.pad(x, ((0, 0), (CONV_WIDTH - 1, 0), (0, 0)))
    y = b
    for k in range(CONV_WIDTH):
        y = y + xp[:, k:k + S] * w[k]
    return y


def rglru(x, wa, ba, wx, bx, lam):
    B, S, E = x.shape
    xb = x.reshape(B, S, LRU_HEADS, LRU_BLOCK)
    r = jax.nn.sigmoid((jnp.einsum('bshi,hij->bshj', xb, wa).reshape(B, S, E) + ba).astype(jnp.float32))
    ig = jax.nn.sigmoid((jnp.einsum('bshi,hij->bshj', xb, wx).reshape(B, S, E) + bx).astype(jnp.float32))
    log_a = -LRU_C * r * jax.nn.softplus(-lam.astype(jnp.float32))
    a = jnp.exp(log_a)
    mult = jnp.sqrt(jnp.maximum(-jnp.expm1(2.0 * log_a), 0.0))
    bterm = mult * ig * x.astype(jnp.float32)

    def combine(left, right):
        a1, b1 = left
        a2, b2 = right
        return a1 * a2, a2 * b1 + b2

    _, h = lax.associative_scan(combine, (a, bterm), axis=1)
    return h.astype(x.dtype)


def _fwd_setup_inputs(seed: int = 0) -> dict:
    key = jax.random.key(seed)
    ks = jax.random.split(key, 24)
    f32 = jnp.float32

    def nrm(k, shape, fan_in):
        return jax.random.normal(k, shape, f32) * (fan_in ** -0.5)

    def gain(k, shape):
        return 1.0 + 0.05 * jax.random.normal(k, shape, f32)

    def bias(k, shape):
        return 0.01 * jax.random.normal(k, shape, f32)

    a0 = jax.random.uniform(ks[15], (N_LRU_LAYERS, D_INNER), f32, 0.9, 0.999)
    lru_L = jnp.log(a0) - jnp.log1p(-a0)
    return {
        "x": jax.random.normal(ks[0], (BATCH, SEQ, D_MODEL), f32),
        "p": jax.random.normal(ks[1], (DEPTH, BATCH, SEQ, PLE_DIM), f32),
        "w_in": nrm(ks[2], (DEPTH, D_MODEL, 2 * D_INNER), D_MODEL),
        "w_out": nrm(ks[3], (DEPTH, D_INNER, D_MODEL), D_INNER),
        "g_pre": gain(ks[4], (DEPTH, D_MODEL)),
        "g_post": gain(ks[5], (DEPTH, D_MODEL)),
        "pool_w": nrm(ks[6], (N_POOL_LAYERS, N_POOL_GROUPS, POOL_GROUP_DIM, POOL_GROUP_DIM), POOL_GROUP_DIM),
        "pool_b": bias(ks[7], (N_POOL_LAYERS, D_INNER)),
        "pool_scale": gain(ks[8], (N_POOL_LAYERS, D_INNER)),
        "conv_w": nrm(ks[9], (N_LRU_LAYERS, CONV_WIDTH, D_INNER), CONV_WIDTH),
        "conv_b": bias(ks[10], (N_LRU_LAYERS, D_INNER)),
        "lru_wa": nrm(ks[11], (N_LRU_LAYERS, LRU_HEADS, LRU_BLOCK, LRU_BLOCK), LRU_BLOCK),
        "lru_ba": bias(ks[12], (N_LRU_LAYERS, D_INNER)),
        "lru_wx": nrm(ks[13], (N_LRU_LAYERS, LRU_HEADS, LRU_BLOCK, LRU_BLOCK), LRU_BLOCK),
        "lru_bx": bias(ks[14], (N_LRU_LAYERS, D_INNER)),
        "lru_L": lru_L,
        "w_ple": nrm(ks[16], (DEPTH, PLE_DIM, D_MODEL), PLE_DIM),
        "w_ple_gate": nrm(ks[17], (DEPTH, D_MODEL, D_MODEL), D_MODEL),
        "g_ple_in": gain(ks[18], (DEPTH, D_MODEL)),
        "g_ple_out": gain(ks[19], (DEPTH, D_MODEL)),
    }


def _fwd_reference(x, p, w_in, w_out, g_pre, g_post, pool_w, pool_b, pool_scale,
              conv_w, conv_b, lru_wa, lru_ba, lru_wx, lru_bx, lru_L,
              w_ple, w_ple_gate, g_ple_in, g_ple_out):
    for i in range(DEPTH):
        h = rmsnorm(x, g_pre[i])
        uz = jnp.einsum('bsd,de->bse', h, w_in[i])
        u, z = uz[..., :D_INNER], uz[..., D_INNER:]
        j = i // N_MIXERS
        if i % N_MIXERS == 0:
            y = pool_mixer(u, pool_w[j], pool_b[j], pool_scale[j])
        else:
            uc = causal_depthwise_conv(u, conv_w[j], conv_b[j])
            y = rglru(uc, lru_wa[j], lru_ba[j], lru_wx[j], lru_bx[j], lru_L[j])
        y = y * jax.nn.silu(z)
        o = jnp.einsum('bse,ed->bsd', y, w_out[i])
        x = x + rmsnorm(o, g_post[i])
        gate = jax.nn.sigmoid(jnp.einsum('bsd,de->bse', rmsnorm(x, g_ple_in[i]), w_ple_gate[i]))
        e = jnp.einsum('bsk,kd->bsd', p[i], w_ple[i])
        x = x + rmsnorm(e * gate, g_ple_out[i])
    return x


import jax as _jax
import jax.numpy as _jnp

TWIN_FORMAT = 'train_step'
FWD_PARAMS = ['x', 'p', 'w_in', 'w_out', 'g_pre', 'g_post', 'pool_w', 'pool_b', 'pool_scale', 'conv_w', 'conv_b', 'lru_wa', 'lru_ba', 'lru_wx', 'lru_bx', 'lru_L', 'w_ple', 'w_ple_gate', 'g_ple_in', 'g_ple_out']
TWIN_WEIGHTS = ['w_in', 'w_out', 'g_pre', 'g_post', 'pool_w', 'pool_b', 'pool_scale', 'conv_w', 'conv_b', 'lru_wa', 'lru_ba', 'lru_wx', 'lru_bx', 'lru_L', 'w_ple', 'w_ple_gate', 'g_ple_in', 'g_ple_out']
TWIN_DIFF_INPUT = 'x'
TWIN_INPUTS = ['x', 'p', 'w_in', 'w_out', 'g_pre', 'g_post', 'pool_w', 'pool_b', 'pool_scale', 'conv_w', 'conv_b', 'lru_wa', 'lru_ba', 'lru_wx', 'lru_bx', 'lru_L', 'w_ple', 'w_ple_gate', 'g_ple_in', 'g_ple_out', 'loss_target', 'm_w_in', 'm_w_out', 'm_g_pre', 'm_g_post', 'm_pool_w', 'm_pool_b', 'm_pool_scale', 'm_conv_w', 'm_conv_b', 'm_lru_wa', 'm_lru_ba', 'm_lru_wx', 'm_lru_bx', 'm_lru_L', 'm_w_ple', 'm_w_ple_gate', 'm_g_ple_in', 'm_g_ple_out', 'v_w_in', 'v_w_out', 'v_g_pre', 'v_g_post', 'v_pool_w', 'v_pool_b', 'v_pool_scale', 'v_conv_w', 'v_conv_b', 'v_lru_wa', 'v_lru_ba', 'v_lru_wx', 'v_lru_bx', 'v_lru_L', 'v_w_ple', 'v_w_ple_gate', 'v_g_ple_in', 'v_g_ple_out']
TWIN_OUTPUTS = ['loss', 'grad_x', 'grad_w_in', 'grad_w_out', 'grad_g_pre', 'grad_g_post', 'grad_pool_w', 'grad_pool_b', 'grad_pool_scale', 'grad_conv_w', 'grad_conv_b', 'grad_lru_wa', 'grad_lru_ba', 'grad_lru_wx', 'grad_lru_bx', 'grad_lru_L', 'grad_w_ple', 'grad_w_ple_gate', 'grad_g_ple_in', 'grad_g_ple_out', 'delta_w_in', 'delta_w_out', 'delta_g_pre', 'delta_g_post', 'delta_pool_w', 'delta_pool_b', 'delta_pool_scale', 'delta_conv_w', 'delta_conv_b', 'delta_lru_wa', 'delta_lru_ba', 'delta_lru_wx', 'delta_lru_bx', 'delta_lru_L', 'delta_w_ple', 'delta_w_ple_gate', 'delta_g_ple_in', 'delta_g_ple_out', 'new_m_w_in', 'new_m_w_out', 'new_m_g_pre', 'new_m_g_post', 'new_m_pool_w', 'new_m_pool_b', 'new_m_pool_scale', 'new_m_conv_w', 'new_m_conv_b', 'new_m_lru_wa', 'new_m_lru_ba', 'new_m_lru_wx', 'new_m_lru_bx', 'new_m_lru_L', 'new_m_w_ple', 'new_m_w_ple_gate', 'new_m_g_ple_in', 'new_m_g_ple_out', 'new_v_w_in', 'new_v_w_out', 'new_v_g_pre', 'new_v_g_post', 'new_v_pool_w', 'new_v_pool_b', 'new_v_pool_scale', 'new_v_conv_w', 'new_v_conv_b', 'new_v_lru_wa', 'new_v_lru_ba', 'new_v_lru_wx', 'new_v_lru_bx', 'new_v_lru_L', 'new_v_w_ple', 'new_v_w_ple_gate', 'new_v_g_ple_in', 'new_v_g_ple_out']
TWIN_LEAF_KINDS = {'loss': 'loss', 'grad_x': 'grad_x', 'grad_w_in': 'grad_w', 'grad_w_out': 'grad_w', 'grad_g_pre': 'grad_w', 'grad_g_post': 'grad_w', 'grad_pool_w': 'grad_w', 'grad_pool_b': 'grad_w', 'grad_pool_scale': 'grad_w', 'grad_conv_w': 'grad_w', 'grad_conv_b': 'grad_w', 'grad_lru_wa': 'grad_w', 'grad_lru_ba': 'grad_w', 'grad_lru_wx': 'grad_w', 'grad_lru_bx': 'grad_w', 'grad_lru_L': 'grad_w', 'grad_w_ple': 'grad_w', 'grad_w_ple_gate': 'grad_w', 'grad_g_ple_in': 'grad_w', 'grad_g_ple_out': 'grad_w', 'delta_w_in': 'delta_w', 'delta_w_out': 'delta_w', 'delta_g_pre': 'delta_w', 'delta_g_post': 'delta_w', 'delta_pool_w': 'delta_w', 'delta_pool_b': 'delta_w', 'delta_pool_scale': 'delta_w', 'delta_conv_w': 'delta_w', 'delta_conv_b': 'delta_w', 'delta_lru_wa': 'delta_w', 'delta_lru_ba': 'delta_w', 'delta_lru_wx': 'delta_w', 'delta_lru_bx': 'delta_w', 'delta_lru_L': 'delta_w', 'delta_w_ple': 'delta_w', 'delta_w_ple_gate': 'delta_w', 'delta_g_ple_in': 'delta_w', 'delta_g_ple_out': 'delta_w', 'new_m_w_in': 'new_m', 'new_m_w_out': 'new_m', 'new_m_g_pre': 'new_m', 'new_m_g_post': 'new_m', 'new_m_pool_w': 'new_m', 'new_m_pool_b': 'new_m', 'new_m_pool_scale': 'new_m', 'new_m_conv_w': 'new_m', 'new_m_conv_b': 'new_m', 'new_m_lru_wa': 'new_m', 'new_m_lru_ba': 'new_m', 'new_m_lru_wx': 'new_m', 'new_m_lru_bx': 'new_m', 'new_m_lru_L': 'new_m', 'new_m_w_ple': 'new_m', 'new_m_w_ple_gate': 'new_m', 'new_m_g_ple_in': 'new_m', 'new_m_g_ple_out': 'new_m', 'new_v_w_in': 'new_v', 'new_v_w_out': 'new_v', 'new_v_g_pre': 'new_v', 'new_v_g_post': 'new_v', 'new_v_pool_w': 'new_v', 'new_v_pool_b': 'new_v', 'new_v_pool_scale': 'new_v', 'new_v_conv_w': 'new_v', 'new_v_conv_b': 'new_v', 'new_v_lru_wa': 'new_v', 'new_v_lru_ba': 'new_v', 'new_v_lru_wx': 'new_v', 'new_v_lru_bx': 'new_v', 'new_v_lru_L': 'new_v', 'new_v_w_ple': 'new_v', 'new_v_w_ple_gate': 'new_v', 'new_v_g_ple_in': 'new_v', 'new_v_g_ple_out': 'new_v'}


def _forward(args):
    return _fwd_reference(*[args[k] for k in FWD_PARAMS])


def _output_shape():
    def fwd():
        inp = _fwd_setup_inputs(0)
        return _fwd_reference(*[inp[k] for k in FWD_PARAMS])
    out = _jax.eval_shape(fwd)
    return out.shape, out.dtype

N_MICROBATCH = 1
ADAM_LR = 0.001
ADAM_B1 = 0.9
ADAM_B2 = 0.999
ADAM_EPS = 1e-08
ADAM_WD = 0.01
ADAM_STEP = 10
PER_EXAMPLE_BATCH_AXIS = {'x': 0, 'p': 1, 'loss_target': 0}
SHARED_INPUTS = []
_WEIGHT_DTYPES = {'w_in': _jnp.float32, 'w_out': _jnp.float32, 'g_pre': _jnp.float32, 'g_post': _jnp.float32, 'pool_w': _jnp.float32, 'pool_b': _jnp.float32, 'pool_scale': _jnp.float32, 'conv_w': _jnp.float32, 'conv_b': _jnp.float32, 'lru_wa': _jnp.float32, 'lru_ba': _jnp.float32, 'lru_wx': _jnp.float32, 'lru_bx': _jnp.float32, 'lru_L': _jnp.float32, 'w_ple': _jnp.float32, 'w_ple_gate': _jnp.float32, 'g_ple_in': _jnp.float32, 'g_ple_out': _jnp.float32}
MOMENT_SCALE = {'w_in': 2.893981e-01, 'w_out': 4.125266e-01, 'g_pre': 5.702882e-01, 'g_post': 1.601100e+01, 'pool_w': 3.233608e-01, 'pool_b': 2.412510e+00, 'pool_scale': 3.225668e-01, 'conv_w': 2.682490e-01, 'conv_b': 2.770917e+00, 'lru_wa': 6.499604e-02, 'lru_ba': 5.999932e-02, 'lru_wx': 1.145400e-01, 'lru_bx': 9.654226e-02, 'lru_L': 1.205033e-01, 'w_ple': 3.932662e-01, 'w_ple_gate': 1.463452e-01, 'g_ple_in': 1.464975e-01, 'g_ple_out': 1.606725e+01}


def _to_microbatches(a, axis):
    t = _jnp.moveaxis(a, axis, 0)
    t = t.reshape((N_MICROBATCH, t.shape[0] // N_MICROBATCH) + t.shape[1:])
    return _jnp.moveaxis(t, 1, axis + 1)


def setup_inputs(seed: int = 0) -> dict:
    inp = _fwd_setup_inputs(seed)
    key = _jax.random.fold_in(_jax.random.key(seed), 7919)
    shape, _ = _output_shape()
    out = dict(inp)
    out["loss_target"] = _jax.random.normal(_jax.random.fold_in(key, 0), shape, _jnp.float32)
    for i, name in enumerate(TWIN_WEIGHTS):
        w = inp[name].astype(_jnp.float32)
        if MOMENT_SCALE is None:
            s = _jnp.sqrt(_jnp.mean(_jnp.square(w)) + 1e-30)
        else:
            s = MOMENT_SCALE[name]
        km, kv = _jax.random.split(_jax.random.fold_in(key, i + 1))
        out[name] = w
        out["m_" + name] = s * _jax.random.normal(km, w.shape, _jnp.float32)
        out["v_" + name] = (s * s) * _jax.random.uniform(kv, w.shape, _jnp.float32, 0.5, 1.5)
    if N_MICROBATCH > 1:
        for name, axis in PER_EXAMPLE_BATCH_AXIS.items():
            out[name] = _to_microbatches(out[name], axis)
    return {'x': out['x'], 'p': out['p'], 'w_in': out['w_in'], 'w_out': out['w_out'], 'g_pre': out['g_pre'], 'g_post': out['g_post'], 'pool_w': out['pool_w'], 'pool_b': out['pool_b'], 'pool_scale': out['pool_scale'], 'conv_w': out['conv_w'], 'conv_b': out['conv_b'], 'lru_wa': out['lru_wa'], 'lru_ba': out['lru_ba'], 'lru_wx': out['lru_wx'], 'lru_bx': out['lru_bx'], 'lru_L': out['lru_L'], 'w_ple': out['w_ple'], 'w_ple_gate': out['w_ple_gate'], 'g_ple_in': out['g_ple_in'], 'g_ple_out': out['g_ple_out'], 'loss_target': out['loss_target'], 'm_w_in': out['m_w_in'], 'm_w_out': out['m_w_out'], 'm_g_pre': out['m_g_pre'], 'm_g_post': out['m_g_post'], 'm_pool_w': out['m_pool_w'], 'm_pool_b': out['m_pool_b'], 'm_pool_scale': out['m_pool_scale'], 'm_conv_w': out['m_conv_w'], 'm_conv_b': out['m_conv_b'], 'm_lru_wa': out['m_lru_wa'], 'm_lru_ba': out['m_lru_ba'], 'm_lru_wx': out['m_lru_wx'], 'm_lru_bx': out['m_lru_bx'], 'm_lru_L': out['m_lru_L'], 'm_w_ple': out['m_w_ple'], 'm_w_ple_gate': out['m_w_ple_gate'], 'm_g_ple_in': out['m_g_ple_in'], 'm_g_ple_out': out['m_g_ple_out'], 'v_w_in': out['v_w_in'], 'v_w_out': out['v_w_out'], 'v_g_pre': out['v_g_pre'], 'v_g_post': out['v_g_post'], 'v_pool_w': out['v_pool_w'], 'v_pool_b': out['v_pool_b'], 'v_pool_scale': out['v_pool_scale'], 'v_conv_w': out['v_conv_w'], 'v_conv_b': out['v_conv_b'], 'v_lru_wa': out['v_lru_wa'], 'v_lru_ba': out['v_lru_ba'], 'v_lru_wx': out['v_lru_wx'], 'v_lru_bx': out['v_lru_bx'], 'v_lru_L': out['v_lru_L'], 'v_w_ple': out['v_w_ple'], 'v_w_ple_gate': out['v_w_ple_gate'], 'v_g_ple_in': out['v_g_ple_in'], 'v_g_ple_out': out['v_g_ple_out']}


def _loss(weights, diff, rest, loss_target):
    with _jax.named_scope("forward"):
        args = {**rest, TWIN_DIFF_INPUT: diff, **{k: w.astype(_WEIGHT_DTYPES[k]) for k, w in weights.items()}}
        y = _forward(args)
    with _jax.named_scope("loss_head"):
        err = _jnp.square(y.astype(_jnp.float32) - loss_target)
        return 0.5 * _jnp.sum(_jnp.mean(err, axis=-1)) if err.ndim else 0.5 * err


def _adamw(w, g, m, v):
    m = ADAM_B1 * m + (1.0 - ADAM_B1) * g
    v = ADAM_B2 * v + (1.0 - ADAM_B2) * _jnp.square(g)
    m_hat = m / (1.0 - ADAM_B1 ** ADAM_STEP)
    v_hat = v / (1.0 - ADAM_B2 ** ADAM_STEP)
    delta = -ADAM_LR * (m_hat / (_jnp.sqrt(v_hat) + ADAM_EPS) + ADAM_WD * w)
    return delta, m, v


def reference(x, p, w_in, w_out, g_pre, g_post, pool_w, pool_b, pool_scale, conv_w, conv_b, lru_wa, lru_ba, lru_wx, lru_bx, lru_L, w_ple, w_ple_gate, g_ple_in, g_ple_out, loss_target, m_w_in, m_w_out, m_g_pre, m_g_post, m_pool_w, m_pool_b, m_pool_scale, m_conv_w, m_conv_b, m_lru_wa, m_lru_ba, m_lru_wx, m_lru_bx, m_lru_L, m_w_ple, m_w_ple_gate, m_g_ple_in, m_g_ple_out, v_w_in, v_w_out, v_g_pre, v_g_post, v_pool_w, v_pool_b, v_pool_scale, v_conv_w, v_conv_b, v_lru_wa, v_lru_ba, v_lru_wx, v_lru_bx, v_lru_L, v_w_ple, v_w_ple_gate, v_g_ple_in, v_g_ple_out):
    given = dict(x=x, p=p, w_in=w_in, w_out=w_out, g_pre=g_pre, g_post=g_post, pool_w=pool_w, pool_b=pool_b, pool_scale=pool_scale, conv_w=conv_w, conv_b=conv_b, lru_wa=lru_wa, lru_ba=lru_ba, lru_wx=lru_wx, lru_bx=lru_bx, lru_L=lru_L, w_ple=w_ple, w_ple_gate=w_ple_gate, g_ple_in=g_ple_in, g_ple_out=g_ple_out, loss_target=loss_target, m_w_in=m_w_in, m_w_out=m_w_out, m_g_pre=m_g_pre, m_g_post=m_g_post, m_pool_w=m_pool_w, m_pool_b=m_pool_b, m_pool_scale=m_pool_scale, m_conv_w=m_conv_w, m_conv_b=m_conv_b, m_lru_wa=m_lru_wa, m_lru_ba=m_lru_ba, m_lru_wx=m_lru_wx, m_lru_bx=m_lru_bx, m_lru_L=m_lru_L, m_w_ple=m_w_ple, m_w_ple_gate=m_w_ple_gate, m_g_ple_in=m_g_ple_in, m_g_ple_out=m_g_ple_out, v_w_in=v_w_in, v_w_out=v_w_out, v_g_pre=v_g_pre, v_g_post=v_g_post, v_pool_w=v_pool_w, v_pool_b=v_pool_b, v_pool_scale=v_pool_scale, v_conv_w=v_conv_w, v_conv_b=v_conv_b, v_lru_wa=v_lru_wa, v_lru_ba=v_lru_ba, v_lru_wx=v_lru_wx, v_lru_bx=v_lru_bx, v_lru_L=v_lru_L, v_w_ple=v_w_ple, v_w_ple_gate=v_w_ple_gate, v_g_ple_in=v_g_ple_in, v_g_ple_out=v_g_ple_out)
    weights = {n: given[n] for n in TWIN_WEIGHTS}
    shared = {n: given[n] for n in SHARED_INPUTS}
    per_example = {n: given[n] for n in ['x', 'p']}
    grad_fn = _jax.value_and_grad(_loss, argnums=(0, 1))

    def one_microbatch(ex, loss_target):
        ex = dict(ex)
        diff = ex.pop(TWIN_DIFF_INPUT)
        return grad_fn(weights, diff, {**shared, **ex}, loss_target)

    if N_MICROBATCH == 1:
        loss, (grad_w, grad_x) = one_microbatch(per_example, given["loss_target"])
    else:
        def body(carry, xs):
            loss_sum, grad_sum = carry
            l_k, (gw_k, gx_k) = one_microbatch(xs[0], xs[1])
            with _jax.named_scope("update"):
                return (loss_sum + l_k, _jax.tree.map(_jnp.add, grad_sum, gw_k)), gx_k

        init = (_jnp.zeros((), _jnp.float32), _jax.tree.map(_jnp.zeros_like, weights))
        (loss, grad_w), grad_x = _jax.lax.scan(body, init, (per_example, given["loss_target"]))
    with _jax.named_scope("update"):
        delta_w, new_m, new_v = {}, {}, {}
        for n in TWIN_WEIGHTS:
            delta_w[n], new_m[n], new_v[n] = _adamw(weights[n], grad_w[n], given["m_" + n], given["v_" + n])
    return (loss, grad_x, *[grad_w[n] for n in TWIN_WEIGHTS], *[delta_w[n] for n in TWIN_WEIGHTS],
            *[new_m[n] for n in TWIN_WEIGHTS], *[new_v[n] for n in TWIN_WEIGHTS])
```

```python
import functools

import jax
import jax.numpy as jnp
from jax import lax
from jax.experimental import pallas as pl
from jax.experimental.pallas import tpu as pltpu

F32 = jnp.float32
MXU = jnp.bfloat16
WIRE = jnp.bfloat16
VMEM_LIMIT = 56 * 1024 * 1024
RMS_EPS = 1e-6
LRU_C = 8.0
POOL_WINDOWS = (2, 4, 8, 16)
MAXW = 16
CONV_W = 4
LRU_HEADS = 16
ADAM_LR, ADAM_B1, ADAM_B2, ADAM_EPS, ADAM_WD, ADAM_STEP = 0.001, 0.9, 0.999, 1e-08, 0.01, 10
MESHID = pl.DeviceIdType.MESH
ANY = pl.BlockSpec(memory_space=pl.ANY)


def _cp(sem=None):
    return pltpu.CompilerParams(dimension_semantics=sem, vmem_limit_bytes=VMEM_LIMIT)


def _sig(v):
    return 1.0 / (1.0 + jnp.exp(-v))


def _tile(n, pref):
    return pref if n % pref == 0 else n


_DN = {"nn": (((1,), (0,)), ((), ())), "nt": (((1,), (1,)), ((), ())), "tn": (((0,), (0,)), ((), ()))}


def _matmul(name, mode, grid, a, a_spec, b, b_spec, out_shape, out_spec, acc_shape, into=None):
    nk = grid[2]

    def body(a_ref, b_ref, *rest):
        o_ref, acc_ref = rest[-2], rest[-1]
        kk = pl.program_id(2)
        prod = lax.dot_general(a_ref[...].astype(MXU), b_ref[...].astype(MXU), _DN[mode], preferred_element_type=F32)
        if nk == 1:
            o_ref[...] = prod.astype(o_ref.dtype)
        else:
            @pl.when(kk == 0)
            def _():
                acc_ref[...] = prod

            @pl.when(kk > 0)
            def _():
                acc_ref[...] += prod

            @pl.when(kk == nk - 1)
            def _():
                o_ref[...] = acc_ref[...].astype(o_ref.dtype)

    in_specs = [a_spec, b_spec]
    args = [a, b]
    aliases = {}
    if into is not None:
        in_specs.append(ANY)
        args.append(into)
        aliases = {2: 0}
    return pl.pallas_call(
        body, name=name, grid=grid, in_specs=in_specs, out_specs=out_spec, out_shape=out_shape,
        scratch_shapes=[pltpu.VMEM(acc_shape, F32)], input_output_aliases=aliases,
        compiler_params=_cp(("parallel", "parallel", "arbitrary")),
    )(*args)


def _rows_call(name, body, ins, in_rows, outs, out_rows, n_rows, tr):
    def spec(shape, tiled):
        if tiled:
            return pl.BlockSpec((tr, shape[1]), lambda i: (i, 0))
        return pl.BlockSpec(shape, lambda i: (0, 0))

    return pl.pallas_call(
        body, name=name, grid=(n_rows // tr,),
        in_specs=[spec(a.shape, t) for a, t in zip(ins, in_rows)],
        out_specs=[spec(o.shape, t) for o, t in zip(outs, out_rows)],
        out_shape=outs, compiler_params=_cp(("arbitrary",)),
    )(*ins)


def _rstd(v):
    return lax.rsqrt(jnp.mean(v * v, axis=-1, keepdims=True) + RMS_EPS)


def _norm_bwd(v, g, dy):
    r = _rstd(v)
    n = v * r
    dn = dy * g
    dv = r * (dn - n * jnp.mean(dn * n, axis=-1, keepdims=True))
    return dv, jnp.sum(dy * n, axis=0, keepdims=True)


def _acc_rows(ref, val):
    @pl.when(pl.program_id(0) == 0)
    def _():
        ref[...] = val

    @pl.when(pl.program_id(0) > 0)
    def _():
        ref[...] += val


def rms_fwd(x, g, tr):
    def body(x_ref, g_ref, o_ref):
        v = x_ref[...]
        o_ref[...] = (v * _rstd(v) * g_ref[...]).astype(o_ref.dtype)

    return _rows_call("rms_fwd", body, [x, g], [True, False], [jax.ShapeDtypeStruct(x.shape, MXU)], [True], x.shape[0], tr)[0]


def res_rms_fwd(x, o, g, tr):
    def body(x_ref, o_ref, g_ref, y_ref):
        v = o_ref[...]
        y_ref[...] = x_ref[...] + v * _rstd(v) * g_ref[...]

    return _rows_call("res_rms_fwd", body, [x, o, g], [True, True, False], [jax.ShapeDtypeStruct(x.shape, F32)], [True], x.shape[0], tr)[0]


def ple_fwd(x1, gpre, e, g, tr):
    def body(x_ref, gp_ref, e_ref, g_ref, y_ref):
        v = e_ref[...] * _sig(gp_ref[...])
        y_ref[...] = x_ref[...] + v * _rstd(v) * g_ref[...]

    return _rows_call("ple_fwd", body, [x1, gpre, e, g], [True, True, True, False], [jax.ShapeDtypeStruct(x1.shape, F32)], [True], x1.shape[0], tr)[0]


def loss_bwd(y, target, tr):
    d = y.shape[1]

    def body(y_ref, t_ref, dy_ref, sq_ref):
        diff = y_ref[...] - t_ref[...]
        dy_ref[...] = diff * (1.0 / d)
        _acc_rows(sq_ref, jnp.sum(diff * diff, axis=0, keepdims=True))

    return _rows_call("loss_bwd", body, [y, target], [True, True],
                      [jax.ShapeDtypeStruct(y.shape, F32), jax.ShapeDtypeStruct((1, d), F32)], [True, False], y.shape[0], tr)


def ple_bwd(dx2, gpre, e, g, tr):
    d = dx2.shape[1]

    def body(dx_ref, gp_ref, e_ref, g_ref, de_ref, dgp_ref, dg_ref):
        gate = _sig(gp_ref[...])
        ev = e_ref[...]
        dv, dg = _norm_bwd(ev * gate, g_ref[...], dx_ref[...])
        de_ref[...] = (dv * gate).astype(de_ref.dtype)
        dgp_ref[...] = (dv * ev * gate * (1.0 - gate)).astype(dgp_ref.dtype)
        _acc_rows(dg_ref, dg)

    return _rows_call("ple_bwd", body, [dx2, gpre, e, g], [True, True, True, False],
                      [jax.ShapeDtypeStruct(dx2.shape, MXU), jax.ShapeDtypeStruct(dx2.shape, MXU), jax.ShapeDtypeStruct((1, d), F32)],
                      [True, True, False], dx2.shape[0], tr)


def rms_bwd_res(dres, dh, x, g, tr):
    d = x.shape[1]

    def body(dr_ref, dh_ref, x_ref, g_ref, dx_ref, dg_ref):
        dv, dg = _norm_bwd(x_ref[...], g_ref[...], dh_ref[...])
        dx_ref[...] = dr_ref[...] + dv
        _acc_rows(dg_ref, dg)

    return _rows_call("rms_bwd_res", body, [dres, dh, x, g], [True, True, True, False],
                      [jax.ShapeDtypeStruct(x.shape, F32), jax.ShapeDtypeStruct((1, d), F32)], [True, False], x.shape[0], tr)


def rms_bwd(dy, o, g, tr):
    d = o.shape[1]

    def body(dy_ref, o_ref, g_ref, do_ref, dg_ref):
        dv, dg = _norm_bwd(o_ref[...], g_ref[...], dy_ref[...])
        do_ref[...] = dv.astype(do_ref.dtype)
        _acc_rows(dg_ref, dg)

    return _rows_call("rms_bwd", body, [dy, o, g], [True, True, False],
                      [jax.ShapeDtypeStruct(o.shape, MXU), jax.ShapeDtypeStruct((1, d), F32)], [True, False], o.shape[0], tr)


def _trailing_sum(ext, w):
    s, k = ext, 1
    while k < w:
        s = s + pltpu.roll(s, k, 0)
        k *= 2
    return s


def _leading_sum(ext, w):
    n = ext.shape[0]
    s, k = ext, 1
    while k < w:
        s = s + pltpu.roll(s, n - k, 0)
        k *= 2
    return s


def _pool_inv_count(rb, tr, w, c):
    t = rb * tr + lax.broadcasted_iota(jnp.int32, (tr, c), 0)
    return 1.0 / jnp.minimum(t + 1, w).astype(F32)


def _pool_d(u_ref, up_ref, rb, tr, w):
    cur = u_ref[...]
    prev = jnp.where(rb > 0, up_ref[...], 0.0)
    ext = jnp.concatenate([prev, cur], axis=0)
    win = _trailing_sum(ext, w)[MAXW:]
    return win * _pool_inv_count(rb, tr, w, cur.shape[1]) - cur


def pool_fwd(uz, w_g, bias, scale, j, tr):
    _, s, e = uz.shape
    ng = len(POOL_WINDOWS)
    cg = e // ng
    nb = s // tr
    hb = tr // MAXW

    def body(u_ref, up_ref, z_ref, w_ref, b_ref, sc_ref, o_ref):
        g, rb = pl.program_id(0), pl.program_id(1)
        wmat = w_ref[...].reshape(cg, cg)
        for gg, win in enumerate(POOL_WINDOWS):
            @pl.when(g == gg)
            def _(win=win):
                d = _pool_d(u_ref, up_ref, rb, tr, win)
                y = (jnp.dot(d.astype(MXU), wmat, preferred_element_type=F32) + b_ref[...]) * sc_ref[...]
                z = z_ref[...]
                o_ref[...] = (y * z * _sig(z)).astype(o_ref.dtype)

    return pl.pallas_call(
        body, name="pool_fwd", grid=(ng, nb),
        in_specs=[
            pl.BlockSpec((None, tr, cg), lambda g, r: (0, r, g)),
            pl.BlockSpec((None, MAXW, cg), lambda g, r: (0, jnp.maximum(r * hb - 1, 0), g)),
            pl.BlockSpec((None, tr, cg), lambda g, r: (1, r, g)),
            pl.BlockSpec((4, None, None, cg // 4, cg), lambda g, r: (0, j, g, 0, 0)),
            pl.BlockSpec((1, cg), lambda g, r: (0, g)),
            pl.BlockSpec((1, cg), lambda g, r: (0, g)),
        ],
        out_specs=pl.BlockSpec((tr, cg), lambda g, r: (r, g)),
        out_shape=jax.ShapeDtypeStruct((s, e), MXU),
        compiler_params=_cp(("parallel", "arbitrary")),
    )(uz, uz, uz, w_g, bias, scale)


def pool_bwd(uz, dy2, w_g, bias, scale, gw_buf, j, tr):
    _, s, e = uz.shape
    ng = len(POOL_WINDOWS)
    cg = e // ng
    nb = s // tr
    hb = tr // MAXW

    def body(u_ref, up_ref, z_ref, dy_ref, w_ref, b_ref, sc_ref, _buf, duz_ref, gw_ref, db_ref, dsc_ref, acc_ref, carry_ref):
        g, step = pl.program_id(0), pl.program_id(1)
        rb = nb - 1 - step
        wmat = w_ref[...].reshape(cg, cg)
        for gg, win in enumerate(POOL_WINDOWS):
            @pl.when(g == gg)
            def _(win=win):
                d = _pool_d(u_ref, up_ref, rb, tr, win).astype(MXU)
                ypre = jnp.dot(d, wmat, preferred_element_type=F32) + b_ref[...]
                z = z_ref[...]
                sg = _sig(z)
                dy2v = dy_ref[...]
                dyv = dy2v * z * sg
                duz_ref[1] = (dy2v * ypre * sc_ref[...] * sg * (1.0 + z * (1.0 - sg))).astype(duz_ref.dtype)
                dypre = dyv * sc_ref[...]
                dsc = jnp.sum(dyv * ypre, axis=0, keepdims=True)
                dbv = jnp.sum(dypre, axis=0, keepdims=True)
                dypre_b = dypre.astype(MXU)
                dd = lax.dot_general(dypre_b, wmat, _DN["nt"], preferred_element_type=F32)
                gw = lax.dot_general(d, dypre_b, _DN["tn"], preferred_element_type=F32)
                q = dd * _pool_inv_count(rb, tr, win, cg)
                nxt = jnp.where(step > 0, carry_ref[...], 0.0)
                lead = _leading_sum(jnp.concatenate([q, nxt], axis=0), win)[:tr]
                duz_ref[0] = (lead - dd).astype(duz_ref.dtype)
                carry_ref[...] = q[:MAXW]

                @pl.when(step == 0)
                def _():
                    acc_ref[...] = gw
                    db_ref[...] = dbv
                    dsc_ref[...] = dsc

                @pl.when(step > 0)
                def _():
                    acc_ref[...] += gw
                    db_ref[...] += dbv
                    dsc_ref[...] += dsc

                @pl.when(step == nb - 1)
                def _():
                    gw_ref[...] = acc_ref[...].reshape(4, cg // 4, cg).astype(gw_ref.dtype)

    return pl.pallas_call(
        body, name="pool_bwd", grid=(ng, nb),
        in_specs=[
            pl.BlockSpec((None, tr, cg), lambda g, r: (0, nb - 1 - r, g)),
            pl.BlockSpec((None, MAXW, cg), lambda g, r: (0, jnp.maximum((nb - 1 - r) * hb - 1, 0), g)),
            pl.BlockSpec((None, tr, cg), lambda g, r: (1, nb - 1 - r, g)),
            pl.BlockSpec((tr, cg), lambda g, r: (nb - 1 - r, g)),
            pl.BlockSpec((4, None, None, cg // 4, cg), lambda g, r: (0, j, g, 0, 0)),
            pl.BlockSpec((1, cg), lambda g, r: (0, g)),
            pl.BlockSpec((1, cg), lambda g, r: (0, g)),
            ANY,
        ],
        out_specs=[
            pl.BlockSpec((2, tr, cg), lambda g, r: (0, nb - 1 - r, g)),
            pl.BlockSpec((4, None, None, cg // 4, cg), lambda g, r: (0, j, g, 0, 0)),
            pl.BlockSpec((1, cg), lambda g, r: (0, g)),
            pl.BlockSpec((1, cg), lambda g, r: (0, g)),
        ],
        out_shape=[
            jax.ShapeDtypeStruct((2, s, e), MXU),
            jax.ShapeDtypeStruct(gw_buf.shape, gw_buf.dtype),
            jax.ShapeDtypeStruct((1, e), F32),
            jax.ShapeDtypeStruct((1, e), F32),
        ],
        scratch_shapes=[pltpu.VMEM((cg, cg), F32), pltpu.VMEM((MAXW, cg), F32)],
        input_output_aliases={7: 1},
        compiler_params=_cp(("parallel", "arbitrary")),
    )(uz, uz, uz, dy2, w_g, bias, scale, gw_buf)


HALO = 8


def _neg_expm1(v):
    poly = v * (1.0 + v * 0.5 * (1.0 + v * (1.0 / 3.0) * (1.0 + v * 0.25 * (1.0 + v * 0.2))))
    return -jnp.where(v > -0.1, poly, jnp.exp(v) - 1.0)


def _softplus_neg(lam):
    t = jnp.exp(-jnp.abs(lam))
    log1p = jnp.where(t < 1e-3, t * (1.0 - t * (0.5 - t * (1.0 / 3.0))), jnp.log(1.0 + t))
    return jnp.maximum(-lam, 0.0) + log1p, _sig(-lam)


def _lru_gates(u_ref, up_ref, rb, sm_ref, wa, wx):
    cur = u_ref[...]
    prev = jnp.where(rb > 0, up_ref[...], 0.0)
    ext = jnp.concatenate([prev, cur], axis=0)
    taps = [cur] + [pltpu.roll(ext, k, 0)[HALO:] for k in range(1, CONV_W)]
    uc = sm_ref[CONV_W:CONV_W + 1, :]
    for k in range(CONV_W):
        uc = uc + taps[k] * sm_ref[CONV_W - 1 - k:CONV_W - k, :]
    ucb = uc.astype(MXU)
    r = _sig(jnp.dot(ucb, wa, preferred_element_type=F32) + sm_ref[5:6, :])
    ig = _sig(jnp.dot(ucb, wx, preferred_element_type=F32) + sm_ref[6:7, :])
    sp, sgn = _softplus_neg(sm_ref[7:8, :])
    log_a = -LRU_C * r * sp
    a = jnp.exp(log_a)
    mult = jnp.sqrt(jnp.maximum(_neg_expm1(2.0 * log_a), 0.0))
    return taps, uc, ucb, r, ig, sp, sgn, a, mult


def _group_scan(a, b, reverse):
    n = a.shape[0]
    row = lax.broadcasted_iota(jnp.int32, a.shape, 0) % 8
    for sft in (1, 2, 4):
        if reverse:
            ash, bsh, m = pltpu.roll(a, n - sft, 0), pltpu.roll(b, n - sft, 0), row < 8 - sft
        else:
            ash, bsh, m = pltpu.roll(a, sft, 0), pltpu.roll(b, sft, 0), row >= sft
        b = jnp.where(m, a * bsh + b, b)
        a = jnp.where(m, a * ash, a)
    return a, b


def _carry_scan(a_ref, b_ref, h_ref, start, reverse):
    ng = a_ref.shape[0] // 8
    edge = 0 if reverse else 7

    def step(i, hin):
        g = ng - 1 - i if reverse else i
        rows = pl.ds(pl.multiple_of(g * 8, 8), 8)
        hv = a_ref[rows, :] * hin + b_ref[rows, :]
        h_ref[rows, :] = hv
        return jnp.broadcast_to(hv[edge:edge + 1, :], hv.shape)

    return lax.fori_loop(0, ng, step, start, unroll=4)


def lru_fwd(uz, wa_g, wx_g, small, j, tr):
    _, s, e = uz.shape
    cb = e // LRU_HEADS
    nb = s // tr
    hb = tr // HALO

    def body(u_ref, up_ref, z_ref, wa_ref, wx_ref, sm_ref, o_ref, h_ref, a_s, b_s, carry_ref):
        rb = pl.program_id(1)
        wa = wa_ref[...].reshape(cb, cb)
        wx = wx_ref[...].reshape(cb, cb)
        _, uc, _, _, ig, _, _, a, mult = _lru_gates(u_ref, up_ref, rb, sm_ref, wa, wx)
        ga, gb = _group_scan(a, mult * ig * uc, False)
        a_s[...] = ga
        b_s[...] = gb
        start = jnp.where(rb > 0, carry_ref[...], 0.0)
        carry_ref[...] = _carry_scan(a_s, b_s, h_ref, start, False)
        z = z_ref[...]
        o_ref[...] = (h_ref[...] * z * _sig(z)).astype(o_ref.dtype)

    wspec = pl.BlockSpec((4, None, None, cb // 4, cb), lambda h, r: (0, j, h, 0, 0))
    return pl.pallas_call(
        body, name="lru_fwd", grid=(LRU_HEADS, nb),
        in_specs=[
            pl.BlockSpec((None, tr, cb), lambda h, r: (0, r, h)),
            pl.BlockSpec((None, HALO, cb), lambda h, r: (0, jnp.maximum(r * hb - 1, 0), h)),
            pl.BlockSpec((None, tr, cb), lambda h, r: (1, r, h)),
            wspec, wspec,
            pl.BlockSpec((8, cb), lambda h, r: (0, h)),
        ],
        out_specs=[pl.BlockSpec((tr, cb), lambda h, r: (r, h)), pl.BlockSpec((tr, cb), lambda h, r: (r, h))],
        out_shape=[jax.ShapeDtypeStruct((s, e), MXU), jax.ShapeDtypeStruct((s, e), F32)],
        scratch_shapes=[pltpu.VMEM((tr, cb), F32), pltpu.VMEM((tr, cb), F32), pltpu.VMEM((8, cb), F32)],
        compiler_params=_cp(("parallel", "arbitrary")),
    )(uz, uz, uz, wa_g, wx_g, small)


def lru_bwd(uz, hst, dy2, wa_g, wx_g, small, gwa_buf, gwx_buf, j, tr):
    _, s, e = uz.shape
    cb = e // LRU_HEADS
    nb = s // tr
    hb = tr // HALO

    def body(u_ref, up_ref, z_ref, h_ref, hp_ref, dy_ref, wa_ref, wx_ref, sm_ref, _b0, _b1,
             duz_ref, gwa_ref, gwx_ref, dsm_ref, a_s, b_s, g_s, acc_a, acc_x, gcar, acar, dcar):
        step = pl.program_id(1)
        rb = nb - 1 - step
        wa = wa_ref[...].reshape(cb, cb)
        wx = wx_ref[...].reshape(cb, cb)
        taps, uc, ucb, r, ig, sp, sgn, a, mult = _lru_gates(u_ref, up_ref, rb, sm_ref, wa, wx)
        row = lax.broadcasted_iota(jnp.int32, a.shape, 0)
        z = z_ref[...]
        sg = _sig(z)
        dy2v = dy_ref[...]
        hv = h_ref[...]
        duz_ref[1] = (dy2v * hv * sg * (1.0 + z * (1.0 - sg))).astype(duz_ref.dtype)
        a_next = jnp.where(row == tr - 1, jnp.where(step > 0, acar[0:1, :], 0.0), pltpu.roll(a, tr - 1, 0))
        ga, gb = _group_scan(a_next, dy2v * z * sg, True)
        a_s[...] = ga
        b_s[...] = gb
        gcar[...] = _carry_scan(a_s, b_s, g_s, jnp.where(step > 0, gcar[...], 0.0), True)
        acar[...] = jnp.broadcast_to(a[0:1, :], acar.shape)
        gv = g_s[...]
        h_before = jnp.where(rb > 0, hp_ref[HALO - 1:HALO, :], 0.0)
        h_prev = jnp.where(row == 0, h_before, pltpu.roll(hv, 1, 0))
        da = gv * h_prev
        gu = gv * uc
        dmult = gu * ig
        dig = gu * mult
        dlog_a = da * a - dmult * jnp.where(mult > 0.0, a * a / mult, 0.0)
        dra = dlog_a * (-LRU_C) * sp * r * (1.0 - r)
        dix = dig * ig * (1.0 - ig)
        dl = jnp.sum(dlog_a * r, axis=0, keepdims=True) * (LRU_C * sgn)
        dra_b, dix_b = dra.astype(MXU), dix.astype(MXU)
        duc = (gv * mult * ig + lax.dot_general(dra_b, wa, _DN["nt"], preferred_element_type=F32)
               + lax.dot_general(dix_b, wx, _DN["nt"], preferred_element_type=F32))
        gwa = lax.dot_general(ucb, dra_b, _DN["tn"], preferred_element_type=F32)
        gwx = lax.dot_general(ucb, dix_b, _DN["tn"], preferred_element_type=F32)
        ext = jnp.concatenate([duc, jnp.where(step > 0, dcar[...], 0.0)], axis=0)
        n = ext.shape[0]
        du = duc * sm_ref[CONV_W - 1:CONV_W, :]
        for k in range(1, CONV_W):
            du = du + pltpu.roll(ext, n - k, 0)[:tr] * sm_ref[CONV_W - 1 - k:CONV_W - k, :]
        duz_ref[0] = du.astype(duz_ref.dtype)
        dcar[...] = duc[:HALO]
        rows = [jnp.sum(duc * taps[CONV_W - 1 - k], axis=0, keepdims=True) for k in range(CONV_W)]
        rows += [jnp.sum(duc, axis=0, keepdims=True), jnp.sum(dra, axis=0, keepdims=True),
                 jnp.sum(dix, axis=0, keepdims=True), dl]

        @pl.when(step == 0)
        def _():
            acc_a[...] = gwa
            acc_x[...] = gwx
            for k, rv in enumerate(rows):
                dsm_ref[k:k + 1, :] = rv

        @pl.when(step > 0)
        def _():
            acc_a[...] += gwa
            acc_x[...] += gwx
            for k, rv in enumerate(rows):
                dsm_ref[k:k + 1, :] += rv

        @pl.when(step == nb - 1)
        def _():
            gwa_ref[...] = acc_a[...].reshape(4, cb // 4, cb).astype(gwa_ref.dtype)
            gwx_ref[...] = acc_x[...].reshape(4, cb // 4, cb).astype(gwx_ref.dtype)

    wspec = pl.BlockSpec((4, None, None, cb // 4, cb), lambda h, r: (0, j, h, 0, 0))
    blk = pl.BlockSpec((tr, cb), lambda h, r: (nb - 1 - r, h))
    return pl.pallas_call(
        body, name="lru_bwd", grid=(LRU_HEADS, nb),
        in_specs=[
            pl.BlockSpec((None, tr, cb), lambda h, r: (0, nb - 1 - r, h)),
            pl.BlockSpec((None, HALO, cb), lambda h, r: (0, jnp.maximum((nb - 1 - r) * hb - 1, 0), h)),
            pl.BlockSpec((None, tr, cb), lambda h, r: (1, nb - 1 - r, h)),
            blk,
            pl.BlockSpec((HALO, cb), lambda h, r: (jnp.maximum((nb - 1 - r) * hb - 1, 0), h)),
            blk,
            wspec, wspec,
            pl.BlockSpec((8, cb), lambda h, r: (0, h)),
            ANY, ANY,
        ],
        out_specs=[
            pl.BlockSpec((2, tr, cb), lambda h, r: (0, nb - 1 - r, h)),
            wspec, wspec,
            pl.BlockSpec((8, cb), lambda h, r: (0, h)),
        ],
        out_shape=[
            jax.ShapeDtypeStruct((2, s, e), MXU),
            jax.ShapeDtypeStruct(gwa_buf.shape, gwa_buf.dtype),
            jax.ShapeDtypeStruct(gwx_buf.shape, gwx_buf.dtype),
            jax.ShapeDtypeStruct((8, e), F32),
        ],
        scratch_shapes=[pltpu.VMEM((tr, cb), F32)] * 3 + [pltpu.VMEM((cb, cb), F32)] * 2 + [pltpu.VMEM((8, cb), F32)] * 3,
        input_output_aliases={9: 1, 10: 2},
        compiler_params=_cp(("parallel", "arbitrary")),
    )(uz, uz, uz, hst, hst, dy2, wa_g, wx_g, small, gwa_buf, gwx_buf)


def _place():
    x, y, c = lax.axis_index("x"), lax.axis_index("y"), lax.axis_index("c")
    chips = [(1 - x, y), (x, 1 - y), (1 - x, 1 - y)]
    return x, y, c, chips


def _rcopy(src, dst, send_sems, recv_sems, k, to):
    return pltpu.make_async_remote_copy(src_ref=src, dst_ref=dst, send_sem=send_sems.at[k], recv_sem=recv_sems.at[k],
                                        device_id=to, device_id_type=MESHID)


def gather_weights(shards):
    n = len(shards)

    def body(*refs):
        ins, outs = refs[:n], refs[n:2 * n]
        send_sems, recv_sems, loc_sems = refs[2 * n:]
        x, y, c, chips = _place()
        me = 2 * x + y
        sib = (x, y, 1 - c)
        locs, sends = [], []
        for a in range(n):
            loc = pltpu.make_async_copy(ins[a], outs[a].at[me], loc_sems.at[a])
            loc.start()
            locs.append(loc)
            for jj, ch in enumerate(chips):
                cp = _rcopy(ins[a].at[:, c], outs[a].at[me, :, c], send_sems, recv_sems, 6 * a + jj, (*ch, c))
                cp.start()
                sends.append(cp)
        for a in range(n):
            for jj, ch in enumerate(chips):
                blk = outs[a].at[2 * ch[0] + ch[1], :, c]
                _rcopy(blk, blk, send_sems, recv_sems, 6 * a + jj, (*ch, c)).wait_recv()
                fw = _rcopy(blk, blk, send_sems, recv_sems, 6 * a + 3 + jj, sib)
                fw.start()
                sends.append(fw)
        for a in range(n):
            for jj, ch in enumerate(chips):
                blk = outs[a].at[2 * ch[0] + ch[1], :, 1 - c]
                _rcopy(blk, blk, send_sems, recv_sems, 6 * a + 3 + jj, sib).wait_recv()
        for cp in sends:
            cp.wait_send()
        for cp in locs:
            cp.wait()

    return pl.pallas_call(
        body, name="gather_weights",
        in_specs=[ANY] * n, out_specs=[ANY] * n,
        out_shape=[jax.ShapeDtypeStruct((4,) + a.shape, a.dtype) for a in shards],
        scratch_shapes=[pltpu.SemaphoreType.DMA((6 * n,)), pltpu.SemaphoreType.DMA((6 * n,)), pltpu.SemaphoreType.DMA((n,))],
        compiler_params=pltpu.CompilerParams(has_side_effects=True),
    )(*shards)


def all_gather_small(v, name):
    m_per, n = v.shape

    def body(x_ref, out_ref, send_sems, recv_sems, local_sem):
        x, y, c, chips = _place()
        me, sibling = (x, y, c), (x, y, 1 - c)

        def rows(px, py, pc):
            return out_ref.at[pl.ds((4 * px + 2 * py + pc) * m_per, m_per), :]

        def copy(k, block, to, src=None):
            return _rcopy(rows(*block) if src is None else src, rows(*block), send_sems, recv_sems, k, to)

        mine = pltpu.make_async_copy(x_ref, rows(*me), local_sem)
        mine.start()
        first = [copy(0, me, sibling, src=x_ref)]
        first += [copy(1 + jj, me, (*chip, c), src=x_ref) for jj, chip in enumerate(chips)]
        for cp in first:
            cp.start()
        passed = [copy(4 + jj, (*chip, c), sibling) for jj, chip in enumerate(chips)]
        for jj, chip in enumerate(chips):
            copy(1 + jj, (*chip, c), me).wait_recv()
            passed[jj].start()
        copy(0, sibling, me).wait_recv()
        for jj, chip in enumerate(chips):
            copy(4 + jj, (*chip, 1 - c), me).wait_recv()
        for cp in first + passed:
            cp.wait_send()
        mine.wait()

    return pl.pallas_call(
        body, name=name,
        out_shape=jax.ShapeDtypeStruct((8 * m_per, n), v.dtype),
        in_specs=[pl.BlockSpec(memory_space=pltpu.VMEM)],
        out_specs=pl.BlockSpec(memory_space=pltpu.VMEM),
        scratch_shapes=[pltpu.SemaphoreType.DMA((7,)), pltpu.SemaphoreType.DMA((7,)), pltpu.SemaphoreType.DMA],
        compiler_params=pltpu.CompilerParams(vmem_limit_bytes=VMEM_LIMIT),
    )(v)


def sum_devices(g):
    def body(g_ref, o_ref):
        acc = g_ref[0]
        for d in range(1, 8):
            acc = acc + g_ref[d]
        o_ref[...] = acc

    return pl.pallas_call(body, name="sum_devices", out_shape=jax.ShapeDtypeStruct(g.shape[1:], g.dtype),
                          compiler_params=pltpu.CompilerParams(vmem_limit_bytes=VMEM_LIMIT))(g)


def send_other_half(grads):
    n = len(grads)

    def body(*refs):
        ins, outs = refs[:n], refs[n:2 * n]
        send_sems, recv_sems = refs[2 * n:]
        x, y, c, _ = _place()
        sib = (x, y, 1 - c)
        cps = [_rcopy(ins[a].at[:, :, 1 - c], outs[a], send_sems, recv_sems, a, sib) for a in range(n)]
        for cp in cps:
            cp.start()
        for cp in cps:
            cp.wait()

    return pl.pallas_call(
        body, name="send_other_half", in_specs=[ANY] * n, out_specs=[ANY] * n,
        out_shape=[jax.ShapeDtypeStruct(g.shape[:2] + g.shape[3:], g.dtype) for g in grads],
        scratch_shapes=[pltpu.SemaphoreType.DMA((n,)), pltpu.SemaphoreType.DMA((n,))],
        compiler_params=pltpu.CompilerParams(has_side_effects=True),
    )(*grads)


def scatter_to_chips(parts):
    n = len(parts)

    def body(*refs):
        ins, outs = refs[:n], refs[n:2 * n]
        send_sems, recv_sems = refs[2 * n:]
        x, y, c, chips = _place()
        cps = []
        for a in range(n):
            for jj, ch in enumerate(chips):
                cps.append(_rcopy(ins[a].at[2 * ch[0] + ch[1]], outs[a].at[jj], send_sems, recv_sems, 3 * a + jj, (*ch, c)))
        for cp in cps:
            cp.start()
        for cp in cps:
            cp.wait()

    return pl.pallas_call(
        body, name="scatter_to_chips", in_specs=[ANY] * n, out_specs=[ANY] * n,
        out_shape=[jax.ShapeDtypeStruct((3,) + p.shape[1:], p.dtype) for p in parts],
        scratch_shapes=[pltpu.SemaphoreType.DMA((3 * n,)), pltpu.SemaphoreType.DMA((3 * n,))],
        compiler_params=pltpu.CompilerParams(has_side_effects=True),
    )(*parts)


def share_halves(halves):
    n = len(halves)

    def body(*refs):
        ins, outs = refs[:n], refs[n:2 * n]
        send_sems, recv_sems, loc_sems = refs[2 * n:]
        x, y, c, _ = _place()
        sib = (x, y, 1 - c)
        locs, cps = [], []
        for a in range(n):
            locs.append(pltpu.make_async_copy(ins[a], outs[a].at[:, c], loc_sems.at[a]))
            cps.append(_rcopy(ins[a], outs[a].at[:, c], send_sems, recv_sems, a, sib))
        for cp in locs + cps:
            cp.start()
        for a in range(n):
            blk = outs[a].at[:, 1 - c]
            _rcopy(blk, blk, send_sems, recv_sems, a, sib).wait_recv()
        for cp in cps:
            cp.wait_send()
        for cp in locs:
            cp.wait()

    return pl.pallas_call(
        body, name="share_halves", in_specs=[ANY] * n, out_specs=[ANY] * n,
        out_shape=[jax.ShapeDtypeStruct((h.shape[0], 2) + h.shape[1:], h.dtype) for h in halves],
        scratch_shapes=[pltpu.SemaphoreType.DMA((n,)), pltpu.SemaphoreType.DMA((n,)), pltpu.SemaphoreType.DMA((n,))],
        compiler_params=pltpu.CompilerParams(has_side_effects=True),
    )(*halves)


def _block_rows(r, c, itemsize, budget=1 << 20):
    tr = r
    while tr * c * itemsize > budget and tr % 16 == 0:
        tr //= 2
    return tr


def add_halves(g, got, c_idx):
    k4, l, _, rh, cc = g.shape
    tr = _block_rows(rh, cc, 4)

    def body(c_ref, g_ref, r_ref, o_ref):
        o_ref[...] = (g_ref[...].astype(F32) + r_ref[...].astype(F32)).astype(o_ref.dtype)

    return pl.pallas_call(
        body, name="add_halves",
        grid_spec=pltpu.PrefetchScalarGridSpec(
            num_scalar_prefetch=1, grid=(k4, l, rh // tr),
            in_specs=[pl.BlockSpec((None, None, None, tr, cc), lambda k, i, b, c_ref: (k, i, c_ref[0], b, 0)),
                      pl.BlockSpec((None, None, tr, cc), lambda k, i, b, c_ref: (k, i, b, 0))],
            out_specs=pl.BlockSpec((None, None, tr, cc), lambda k, i, b, c_ref: (k, i, b, 0))),
        out_shape=jax.ShapeDtypeStruct(got.shape, WIRE),
        compiler_params=_cp(("parallel", "parallel", "parallel")),
    )(c_idx, g, got)


def sum_chips(own, got, k_idx):
    _, l, rh, cc = own.shape
    tr = _block_rows(rh, cc, 4)

    def body(k_ref, o_ref, r_ref, s_ref):
        s_ref[...] = ((o_ref[...].astype(F32) + r_ref[0].astype(F32)) + r_ref[1].astype(F32)) + r_ref[2].astype(F32)

    return pl.pallas_call(
        body, name="sum_chips",
        grid_spec=pltpu.PrefetchScalarGridSpec(
            num_scalar_prefetch=1, grid=(l, rh // tr),
            in_specs=[pl.BlockSpec((None, None, tr, cc), lambda i, b, k_ref: (k_ref[0], i, b, 0)),
                      pl.BlockSpec((3, None, tr, cc), lambda i, b, k_ref: (0, i, b, 0))],
            out_specs=pl.BlockSpec((None, tr, cc), lambda i, b, k_ref: (i, b, 0))),
        out_shape=jax.ShapeDtypeStruct(own.shape[1:], F32),
        compiler_params=_cp(("parallel", "parallel")),
    )(k_idx, own, got)


def _adam_math(w, g, m, v):
    m = ADAM_B1 * m + (1.0 - ADAM_B1) * g
    v = ADAM_B2 * v + (1.0 - ADAM_B2) * (g * g)
    m_hat = m / (1.0 - ADAM_B1 ** ADAM_STEP)
    v_hat = v / (1.0 - ADAM_B2 ** ADAM_STEP)
    delta = -ADAM_LR * (m_hat / (jnp.sqrt(v_hat) + ADAM_EPS) + ADAM_WD * w)
    return delta, m, v


def adamw(w, g, m, v):
    l, r, c = w.shape
    tr = _block_rows(r, c, 4)

    def body(w_ref, g_ref, m_ref, v_ref, go_ref, d_ref, mo_ref, vo_ref):
        gv = g_ref[...]
        go_ref[...] = gv
        d_ref[...], mo_ref[...], vo_ref[...] = _adam_math(w_ref[...], gv, m_ref[...], v_ref[...])

    spec = pl.BlockSpec((None, tr, c), lambda i, b: (i, b, 0))
    return pl.pallas_call(
        body, name="adamw", grid=(l, r // tr), in_specs=[spec] * 4, out_specs=[spec] * 4,
        out_shape=[jax.ShapeDtypeStruct(w.shape, F32)] * 4, compiler_params=_cp(("parallel", "parallel")),
    )(w, g, m, v)


def adamw_small(w, g, m, v):
    def body(w_ref, g_ref, m_ref, v_ref, d_ref, mo_ref, vo_ref):
        d_ref[...], mo_ref[...], vo_ref[...] = _adam_math(w_ref[...], g_ref[...], m_ref[...], v_ref[...])

    return pl.pallas_call(body, name="adamw_small", out_shape=[jax.ShapeDtypeStruct(w.shape, F32)] * 3)(w, g, m, v)


def kernel(x, p, w_in, w_out, g_pre, g_post, pool_w, pool_b, pool_scale, conv_w, conv_b, lru_wa, lru_ba, lru_wx, lru_bx, lru_L, w_ple, w_ple_gate, g_ple_in, g_ple_out, loss_target, m_w_in, m_w_out, m_g_pre, m_g_post, m_pool_w, m_pool_b, m_pool_scale, m_conv_w, m_conv_b, m_lru_wa, m_lru_ba, m_lru_wx, m_lru_bx, m_lru_L, m_w_ple, m_w_ple_gate, m_g_ple_in, m_g_ple_out, v_w_in, v_w_out, v_g_pre, v_g_post, v_pool_w, v_pool_b, v_pool_scale, v_conv_w, v_conv_b, v_lru_wa, v_lru_ba, v_lru_wx, v_lru_bx, v_lru_L, v_w_ple, v_w_ple_gate, v_g_ple_in, v_g_ple_out):
    depth = w_in.shape[0]
    _, s, d = x.shape
    e = 2 * d
    kp = p.shape[-1]
    nmix = pool_w.shape[0]
    ngrp = pool_w.shape[1]
    cg = e // ngrp
    cb = e // LRU_HEADS
    xi, yi, ci = lax.axis_index("x"), lax.axis_index("y"), lax.axis_index("c")
    me = 2 * xi + yi
    c_idx = jnp.reshape(ci, (1,)).astype(jnp.int32)
    k_idx = jnp.reshape(me, (1,)).astype(jnp.int32)
    tr_row = _tile(s, 256)
    tr_mix = _tile(s, 512)
    tm = _tile(s, 1024)

    def halves(a):
        return a.reshape(a.shape[0], 2, a.shape[1] // 2, a.shape[2])

    big = {
        "w_in": w_in, "w_out": w_out, "gate": w_ple_gate, "ple": w_ple,
        "pool": pool_w.reshape(nmix * ngrp, cg // 4, cg),
        "wa": lru_wa.reshape(nmix * LRU_HEADS, cb // 4, cb), "wx": lru_wx.reshape(nmix * LRU_HEADS, cb // 4, cb),
    }
    names = list(big)
    gathered = gather_weights([halves(big[n].astype(WIRE)) for n in names])
    gw = {n: g.reshape((4,) + big[n].shape) for n, g in zip(names, gathered)}
    w_in_g, w_out_g, gate_g, ple_g = gw["w_in"], gw["w_out"], gw["gate"], gw["ple"]
    pool_g = gw["pool"].reshape(4, nmix, ngrp, cg // 4, cg)
    wa_g = gw["wa"].reshape(4, nmix, LRU_HEADS, cb // 4, cb)
    wx_g = gw["wx"].reshape(4, nmix, LRU_HEADS, cb // 4, cb)

    ec = e // 4
    small_loc = jnp.concatenate([conv_w, conv_b[:, None], lru_ba[:, None], lru_bx[:, None], lru_L[:, None]], axis=1)
    sm_all = all_gather_small(small_loc.reshape(nmix * 8, ec), "gather_small").reshape(4, 2, nmix, 8, ec)
    lru_small = jnp.transpose(sm_all[:, 0], (1, 2, 0, 3)).reshape(nmix, 8, e)

    xs = x[0]
    saved = []
    for i in range(depth):
        j = i // 2
        h = rms_fwd(xs, g_pre[i][None], tr_row)
        nj = (2 * e) // 1024 if (2 * e) % 1024 == 0 else 4
        tn = (2 * e) // nj
        per = e // tn
        perk = (e // 2) // tn
        uz = _matmul(
            "mm_in", "nn", (s // tm, nj, 1), h, pl.BlockSpec((tm, d), lambda a, b, k: (a, 0)),
            w_in_g, pl.BlockSpec((None, None, d, tn), lambda a, b, k, i=i, perk=perk: (b // perk, i, 0, b % perk)),
            jax.ShapeDtypeStruct((2, s, e), F32), pl.BlockSpec((None, tm, tn), lambda a, b, k, per=per: (b // per, a, b % per)),
            (8, 128))
        if i % 2 == 0:
            y2 = pool_fwd(uz, pool_g, pool_b[j][None], pool_scale[j][None], j, tr_mix)
            hst = None
        else:
            y2, hst = lru_fwd(uz, wa_g, wx_g, lru_small[j], j, tr_mix)
        tn_o = _tile(d, 1024)
        o = _matmul(
            "mm_out", "nn", (s // tm, d // tn_o, 4), y2, pl.BlockSpec((tm, e // 4), lambda a, b, k: (a, k)),
            w_out_g, pl.BlockSpec((None, None, e // 4, tn_o), lambda a, b, k, i=i: (k, i, 0, b)),
            jax.ShapeDtypeStruct((s, d), F32), pl.BlockSpec((tm, tn_o), lambda a, b, k: (a, b)), (tm, tn_o))
        x1 = res_rms_fwd(xs, o, g_post[i][None], tr_row)
        hn = rms_fwd(x1, g_ple_in[i][None], tr_row)
        gpre = _matmul(
            "mm_gate", "nn", (s // tm, d // tn_o, 4), hn, pl.BlockSpec((tm, d // 4), lambda a, b, k: (a, k)),
            gate_g, pl.BlockSpec((None, None, d // 4, tn_o), lambda a, b, k, i=i: (k, i, 0, b)),
            jax.ShapeDtypeStruct((s, d), F32), pl.BlockSpec((tm, tn_o), lambda a, b, k: (a, b)), (tm, tn_o))
        pe = p[i, 0]
        ev = _matmul(
            "mm_ple", "nn", (s // tm, 4, 1), pe, pl.BlockSpec((tm, kp), lambda a, b, k: (a, 0)),
            ple_g, pl.BlockSpec((None, None, kp, d // 4), lambda a, b, k, i=i: (b, i, 0, 0)),
            jax.ShapeDtypeStruct((s, d), F32), pl.BlockSpec((tm, d // 4), lambda a, b, k: (a, b)), (8, 128))
        x2 = ple_fwd(x1, gpre, ev, g_ple_out[i][None], tr_row)
        saved.append((xs, h, uz, y2, hst, o, x1, hn, gpre, ev))
        xs = x2

    dx, sq = loss_bwd(xs, loss_target[0], tr_row)
    gbuf = {n: lax.empty((4,) + big[n].shape, WIRE) for n in names}
    gbuf["pool"] = gbuf["pool"].reshape(4, nmix, ngrp, cg // 4, cg)
    gbuf["wa"] = gbuf["wa"].reshape(4, nmix, LRU_HEADS, cb // 4, cb)
    gbuf["wx"] = gbuf["wx"].reshape(4, nmix, LRU_HEADS, cb // 4, cb)
    d_gpre, d_gpost, d_gin, d_gout = [None] * depth, [None] * depth, [None] * depth, [None] * depth
    d_pool_b, d_pool_sc, d_lru_small = [None] * nmix, [None] * nmix, [None] * nmix
    ts = _tile(s, 1024)
    for i in reversed(range(depth)):
        j = i // 2
        x0, h, uz, y2, hst, o, x1, hn, gpre, ev = saved[i]
        pe = p[i, 0]
        de, dgp, d_gout[i] = ple_bwd(dx, gpre, ev, g_ple_out[i][None], tr_row)
        gbuf["ple"] = _matmul(
            "mm_dple", "tn", (1, 4, s // ts), pe, pl.BlockSpec((ts, kp), lambda a, b, k: (k, 0)),
            de, pl.BlockSpec((ts, d // 4), lambda a, b, k: (k, b)),
            jax.ShapeDtypeStruct(gbuf["ple"].shape, WIRE), pl.BlockSpec((None, None, kp, d // 4), lambda a, b, k, i=i: (b, i, 0, 0)),
            (kp, d // 4), into=gbuf["ple"])
        gbuf["gate"] = _matmul(
            "mm_dgate", "tn", (4, 1, s // ts), hn, pl.BlockSpec((ts, d // 4), lambda a, b, k: (k, a)),
            dgp, pl.BlockSpec((ts, d), lambda a, b, k: (k, 0)),
            jax.ShapeDtypeStruct(gbuf["gate"].shape, WIRE), pl.BlockSpec((None, None, d // 4, d), lambda a, b, k, i=i: (a, i, 0, 0)),
            (d // 4, d), into=gbuf["gate"])
        dhn = _matmul(
            "mm_dhn", "nt", (s // tm, 4, 1), dgp, pl.BlockSpec((tm, d), lambda a, b, k: (a, 0)),
            gate_g, pl.BlockSpec((None, None, d // 4, d), lambda a, b, k, i=i: (b, i, 0, 0)),
            jax.ShapeDtypeStruct((s, d), F32), pl.BlockSpec((tm, d // 4), lambda a, b, k: (a, b)), (8, 128))
        dx1, d_gin[i] = rms_bwd_res(dx, dhn, x1, g_ple_in[i][None], tr_row)
        do, d_gpost[i] = rms_bwd(dx1, o, g_post[i][None], tr_row)
        tn_o = _tile(d, 1024)
        gbuf["w_out"] = _matmul(
            "mm_dwout", "tn", (4, d // tn_o, s // ts), y2, pl.BlockSpec((ts, e // 4), lambda a, b, k: (k, a)),
            do, pl.BlockSpec((ts, tn_o), lambda a, b, k: (k, b)),
            jax.ShapeDtypeStruct(gbuf["w_out"].shape, WIRE), pl.BlockSpec((None, None, e // 4, tn_o), lambda a, b, k, i=i: (a, i, 0, b)),
            (e // 4, tn_o), into=gbuf["w_out"])
        dy2 = _matmul(
            "mm_dy2", "nt", (s // tm, 4, 1), do, pl.BlockSpec((tm, d), lambda a, b, k: (a, 0)),
            w_out_g, pl.BlockSpec((None, None, e // 4, d), lambda a, b, k, i=i: (b, i, 0, 0)),
            jax.ShapeDtypeStruct((s, e), F32), pl.BlockSpec((tm, e // 4), lambda a, b, k: (a, b)), (8, 128))
        if i % 2 == 0:
            duz, gbuf["pool"], d_pool_b[j], d_pool_sc[j] = pool_bwd(uz, dy2, pool_g, pool_b[j][None], pool_scale[j][None], gbuf["pool"], j, tr_mix)
        else:
            duz, gbuf["wa"], gbuf["wx"], d_lru_small[j] = lru_bwd(uz, hst, dy2, wa_g, wx_g, lru_small[j], gbuf["wa"], gbuf["wx"], j, tr_mix)
        tmi = _tile(d, 1024)
        tni = _tile(e // 2, 1024)
        nslab = (e // 2) // tni
        gbuf["w_in"] = _matmul(
            "mm_dwin", "tn", (d // tmi, 4 * nslab, s // ts), h, pl.BlockSpec((ts, tmi), lambda a, b, k: (k, a)),
            duz, pl.BlockSpec((None, ts, tni), lambda a, b, k, nslab=nslab: (b // (2 * nslab), k, b % (2 * nslab))),
            jax.ShapeDtypeStruct(gbuf["w_in"].shape, WIRE),
            pl.BlockSpec((None, None, tmi, tni), lambda a, b, k, i=i, nslab=nslab: (b // nslab, i, a, b % nslab)),
            (tmi, tni), into=gbuf["w_in"])
        tnd = _tile(d, 1024)
        dh = _matmul(
            "mm_dh", "nt", (s // tm, d // tnd, 4), duz, pl.BlockSpec((None, tm, e // 2), lambda a, b, k: (k // 2, a, k % 2)),
            w_in_g, pl.BlockSpec((None, None, tnd, e // 2), lambda a, b, k, i=i: (k, i, b, 0)),
            jax.ShapeDtypeStruct((s, d), F32), pl.BlockSpec((tm, tnd), lambda a, b, k: (a, b)), (tm, tnd))
        dx, d_gpre[i] = rms_bwd_res(dx1, dh, x0, g_pre[i][None], tr_row)
    grad_x = dx[None]

    gparts = [gbuf[n].reshape((4,) + halves(big[n]).shape) for n in names]
    from_sib = send_other_half(gparts)
    chip_sums = [add_halves(g, r, c_idx) for g, r in zip(gparts, from_sib)]
    from_chips = scatter_to_chips(chip_sums)
    half_sums = [sum_chips(o, r, k_idx) for o, r in zip(chip_sums, from_chips)]
    full = share_halves(half_sums)
    grads = {n: f.reshape(big[n].shape) for n, f in zip(names, full)}

    def rows_e(a):
        return jnp.stack(a).reshape(-1, e) if isinstance(a, list) else a.reshape(-1, e)

    pack = [rows_e([g[0] for g in d_gpre]), rows_e([g[0] for g in d_gpost]), rows_e([g[0] for g in d_gin]), rows_e([g[0] for g in d_gout]),
            jnp.concatenate(d_pool_b, axis=0), jnp.concatenate(d_pool_sc, axis=0), jnp.concatenate(d_lru_small, axis=0),
            jnp.pad(sq, ((0, 0), (0, e - d)))]
    sizes = [a.shape[0] for a in pack]
    packed = jnp.concatenate(pack, axis=0)
    nrow = packed.shape[0]
    nrow_p = -(-nrow // 8) * 8
    packed = jnp.pad(packed, ((0, nrow_p - nrow), (0, 0)))
    total = sum_devices(all_gather_small(packed, "gather_grads").reshape(8, nrow_p, e))
    parts, off = [], 0
    for n_ in sizes:
        parts.append(total[off:off + n_])
        off += n_
    t_gpre, t_gpost, t_gin, t_gout, t_pb, t_psc, t_lru, t_sq = parts
    loss = 0.5 * jnp.sum(t_sq) / d
    t_lru = t_lru.reshape(nmix, 8, e)
    t_lru_loc = lax.dynamic_slice_in_dim(t_lru, me * ec, ec, axis=2)

    def big_update(name, w, m, v):
        shp = big[name].shape
        g, dl, nm, nv = adamw(w.reshape(shp), grads[name], m.reshape(shp), v.reshape(shp))
        return [a.reshape(w.shape) for a in (g, dl, nm, nv)]

    def small_update(w, g, m, v):
        shp = w.shape
        w2 = w.reshape(-1, shp[-1])
        dl, nm, nv = adamw_small(w2, g.reshape(w2.shape), m.reshape(w2.shape), v.reshape(w2.shape))
        return [g.reshape(shp), dl.reshape(shp), nm.reshape(shp), nv.reshape(shp)]

    res = {
        "w_in": big_update("w_in", w_in, m_w_in, v_w_in),
        "w_out": big_update("w_out", w_out, m_w_out, v_w_out),
        "g_pre": small_update(g_pre, t_gpre.reshape(depth, d), m_g_pre, v_g_pre),
        "g_post": small_update(g_post, t_gpost.reshape(depth, d), m_g_post, v_g_post),
        "pool_w": big_update("pool", pool_w, m_pool_w, v_pool_w),
        "pool_b": small_update(pool_b, t_pb, m_pool_b, v_pool_b),
        "pool_scale": small_update(pool_scale, t_psc, m_pool_scale, v_pool_scale),
        "conv_w": small_update(conv_w, t_lru_loc[:, :CONV_W], m_conv_w, v_conv_w),
        "conv_b": small_update(conv_b, t_lru_loc[:, 4], m_conv_b, v_conv_b),
        "lru_wa": big_update("wa", lru_wa, m_lru_wa, v_lru_wa),
        "lru_ba": small_update(lru_ba, t_lru_loc[:, 5], m_lru_ba, v_lru_ba),
        "lru_wx": big_update("wx", lru_wx, m_lru_wx, v_lru_wx),
        "lru_bx": small_update(lru_bx, t_lru_loc[:, 6], m_lru_bx, v_lru_bx),
        "lru_L": small_update(lru_L, t_lru_loc[:, 7], m_lru_L, v_lru_L),
        "w_ple": big_update("ple", w_ple, m_w_ple, v_w_ple),
        "w_ple_gate": big_update("gate", w_ple_gate, m_w_ple_gate, v_w_ple_gate),
        "g_ple_in": small_update(g_ple_in, t_gin.reshape(depth, d), m_g_ple_in, v_g_ple_in),
        "g_ple_out": small_update(g_ple_out, t_gout.reshape(depth, d), m_g_ple_out, v_g_ple_out),
    }
    order = ["w_in", "w_out", "g_pre", "g_post", "pool_w", "pool_b", "pool_scale", "conv_w", "conv_b", "lru_wa", "lru_ba",
             "lru_wx", "lru_bx", "lru_L", "w_ple", "w_ple_gate", "g_ple_in", "g_ple_out"]
    out = [loss, grad_x]
    for slot in range(4):
        out += [res[n][slot] for n in order]
    return tuple(out)
```

```python
import functools

import jax
import jax.numpy as jnp
from jax import lax
from jax.experimental import pallas as pl
from jax.experimental.pallas import tpu as pltpu

F32 = jnp.float32
MXU = jnp.bfloat16
WIRE = jnp.bfloat16
VMEM_LIMIT = 56 * 1024 * 1024
RMS_EPS = 1e-6
LRU_C = 8.0
POOL_WINDOWS = (2, 4, 8, 16)
MAXW = 16
CONV_W = 4
LRU_HEADS = 16
ADAM_LR, ADAM_B1, ADAM_B2, ADAM_EPS, ADAM_WD, ADAM_STEP = 0.001, 0.9, 0.999, 1e-08, 0.01, 10
MESHID = pl.DeviceIdType.MESH
ANY = pl.BlockSpec(memory_space=pl.ANY)


def _cp(sem=None):
    return pltpu.CompilerParams(dimension_semantics=sem, vmem_limit_bytes=VMEM_LIMIT)


def _sig(v):
    return 0.5 * jnp.tanh(0.5 * v) + 0.5


def _tile(n, pref):
    return pref if n % pref == 0 else n


_DN = {"nn": (((1,), (0,)), ((), ())), "nt": (((1,), (1,)), ((), ())), "tn": (((0,), (0,)), ((), ()))}


def _matmul(name, mode, grid, a, a_spec, b, b_spec, out_shape, out_spec, acc_shape, into=None):
    nk = grid[2]

    def body(a_ref, b_ref, *rest):
        o_ref, acc_ref = rest[-2], rest[-1]
        kk = pl.program_id(2)
        prod = lax.dot_general(a_ref[...].astype(MXU), b_ref[...].astype(MXU), _DN[mode], preferred_element_type=F32)
        if nk == 1:
            o_ref[...] = prod.astype(o_ref.dtype)
        else:
            @pl.when(kk == 0)
            def _():
                acc_ref[...] = prod

            @pl.when(kk > 0)
            def _():
                acc_ref[...] += prod

            @pl.when(kk == nk - 1)
            def _():
                o_ref[...] = acc_ref[...].astype(o_ref.dtype)

    in_specs = [a_spec, b_spec]
    args = [a, b]
    aliases = {}
    if into is not None:
        in_specs.append(ANY)
        args.append(into)
        aliases = {2: 0}
    return pl.pallas_call(
        body, name=name, grid=grid, in_specs=in_specs, out_specs=out_spec, out_shape=out_shape,
        scratch_shapes=[pltpu.VMEM(acc_shape, F32)], input_output_aliases=aliases,
        compiler_params=_cp(("parallel", "parallel", "arbitrary")),
    )(*args)


def _rows_call(name, body, ins, in_rows, outs, out_rows, n_rows, tr):
    def spec(shape, tiled):
        if tiled:
            return pl.BlockSpec((tr, shape[1]), lambda i: (i, 0))
        return pl.BlockSpec(shape, lambda i: (0, 0))

    return pl.pallas_call(
        body, name=name, grid=(n_rows // tr,),
        in_specs=[spec(a.shape, t) for a, t in zip(ins, in_rows)],
        out_specs=[spec(o.shape, t) for o, t in zip(outs, out_rows)],
        out_shape=outs, compiler_params=_cp(("arbitrary",)),
    )(*ins)


def _rstd(v):
    return lax.rsqrt(jnp.mean(v * v, axis=-1, keepdims=True) + RMS_EPS)


def _norm_bwd(v, g, dy):
    r = _rstd(v)
    n = v * r
    dn = dy * g
    dv = r * (dn - n * jnp.mean(dn * n, axis=-1, keepdims=True))
    return dv, jnp.sum(dy * n, axis=0, keepdims=True)


def _acc_rows(ref, val):
    @pl.when(pl.program_id(0) == 0)
    def _():
        ref[...] = val

    @pl.when(pl.program_id(0) > 0)
    def _():
        ref[...] += val


def rms_fwd(x, g, tr):
    def body(x_ref, g_ref, o_ref):
        v = x_ref[...]
        o_ref[...] = (v * _rstd(v) * g_ref[...]).astype(o_ref.dtype)

    return _rows_call("rms_fwd", body, [x, g], [True, False], [jax.ShapeDtypeStruct(x.shape, MXU)], [True], x.shape[0], tr)[0]


def res_rms_fwd(x, o, g, tr):
    def body(x_ref, o_ref, g_ref, y_ref):
        v = o_ref[...]
        y_ref[...] = x_ref[...] + v * _rstd(v) * g_ref[...]

    return _rows_call("res_rms_fwd", body, [x, o, g], [True, True, False], [jax.ShapeDtypeStruct(x.shape, F32)], [True], x.shape[0], tr)[0]


def ple_fwd(x1, gpre, e, g, tr):
    def body(x_ref, gp_ref, e_ref, g_ref, y_ref):
        v = e_ref[...] * _sig(gp_ref[...])
        y_ref[...] = x_ref[...] + v * _rstd(v) * g_ref[...]

    return _rows_call("ple_fwd", body, [x1, gpre, e, g], [True, True, True, False], [jax.ShapeDtypeStruct(x1.shape, F32)], [True], x1.shape[0], tr)[0]


def loss_bwd(y, target, tr):
    d = y.shape[1]

    def body(y_ref, t_ref, dy_ref, sq_ref):
        diff = y_ref[...] - t_ref[...]
        dy_ref[...] = diff * (1.0 / d)
        _acc_rows(sq_ref, jnp.sum(diff * diff, axis=0, keepdims=True))

    return _rows_call("loss_bwd", body, [y, target], [True, True],
                      [jax.ShapeDtypeStruct(y.shape, F32), jax.ShapeDtypeStruct((1, d), F32)], [True, False], y.shape[0], tr)


def ple_bwd(dx2, gpre, e, g, tr):
    d = dx2.shape[1]

    def body(dx_ref, gp_ref, e_ref, g_ref, de_ref, dgp_ref, dg_ref):
        gate = _sig(gp_ref[...])
        ev = e_ref[...]
        dv, dg = _norm_bwd(ev * gate, g_ref[...], dx_ref[...])
        de_ref[...] = (dv * gate).astype(de_ref.dtype)
        dgp_ref[...] = (dv * ev * gate * (1.0 - gate)).astype(dgp_ref.dtype)
        _acc_rows(dg_ref, dg)

    return _rows_call("ple_bwd", body, [dx2, gpre, e, g], [True, True, True, False],
                      [jax.ShapeDtypeStruct(dx2.shape, MXU), jax.ShapeDtypeStruct(dx2.shape, MXU), jax.ShapeDtypeStruct((1, d), F32)],
                      [True, True, False], dx2.shape[0], tr)


def rms_bwd_res(dres, dh, x, g, tr):
    d = x.shape[1]

    def body(dr_ref, dh_ref, x_ref, g_ref, dx_ref, dg_ref):
        dv, dg = _norm_bwd(x_ref[...], g_ref[...], dh_ref[...])
        dx_ref[...] = dr_ref[...] + dv
        _acc_rows(dg_ref, dg)

    return _rows_call("rms_bwd_res", body, [dres, dh, x, g], [True, True, True, False],
                      [jax.ShapeDtypeStruct(x.shape, F32), jax.ShapeDtypeStruct((1, d), F32)], [True, False], x.shape[0], tr)


def rms_bwd(dy, o, g, tr):
    d = o.shape[1]

    def body(dy_ref, o_ref, g_ref, do_ref, dg_ref):
        dv, dg = _norm_bwd(o_ref[...], g_ref[...], dy_ref[...])
        do_ref[...] = dv.astype(do_ref.dtype)
        _acc_rows(dg_ref, dg)

    return _rows_call("rms_bwd", body, [dy, o, g], [True, True, False],
                      [jax.ShapeDtypeStruct(o.shape, MXU), jax.ShapeDtypeStruct((1, d), F32)], [True, False], o.shape[0], tr)


def _trailing_sum(ext, w):
    s, k = ext, 1
    while k < w:
        s = s + pltpu.roll(s, k, 0)
        k *= 2
    return s


def _leading_sum(ext, w):
    n = ext.shape[0]
    s, k = ext, 1
    while k < w:
        s = s + pltpu.roll(s, n - k, 0)
        k *= 2
    return s


def _pool_inv_count(rb, tr, w, c):
    t = rb * tr + lax.broadcasted_iota(jnp.int32, (tr, c), 0)
    return 1.0 / jnp.minimum(t + 1, w).astype(F32)


def _pool_d(u_ref, up_ref, rb, tr, w):
    cur = u_ref[...]
    prev = jnp.where(rb > 0, up_ref[...], 0.0)
    ext = jnp.concatenate([prev, cur], axis=0)
    win = _trailing_sum(ext, w)[MAXW:]
    return win * _pool_inv_count(rb, tr, w, cur.shape[1]) - cur


def pool_fwd(uz, w_g, bias, scale, j, tr):
    _, s, e = uz.shape
    ng = len(POOL_WINDOWS)
    cg = e // ng
    nb = s // tr
    hb = tr // MAXW

    def body(u_ref, up_ref, z_ref, w_ref, b_ref, sc_ref, o_ref):
        g, rb = pl.program_id(0), pl.program_id(1)
        wmat = w_ref[...].reshape(cg, cg)
        for gg, win in enumerate(POOL_WINDOWS):
            @pl.when(g == gg)
            def _(win=win):
                d = _pool_d(u_ref, up_ref, rb, tr, win)
                y = (jnp.dot(d.astype(MXU), wmat, preferred_element_type=F32) + b_ref[...]) * sc_ref[...]
                z = z_ref[...]
                o_ref[...] = (y * z * _sig(z)).astype(o_ref.dtype)

    return pl.pallas_call(
        body, name="pool_fwd", grid=(ng, nb),
        in_specs=[
            pl.BlockSpec((None, tr, cg), lambda g, r: (0, r, g)),
            pl.BlockSpec((None, MAXW, cg), lambda g, r: (0, jnp.maximum(r * hb - 1, 0), g)),
            pl.BlockSpec((None, tr, cg), lambda g, r: (1, r, g)),
            pl.BlockSpec((4, None, None, cg // 4, cg), lambda g, r: (0, j, g, 0, 0)),
            pl.BlockSpec((1, cg), lambda g, r: (0, g)),
            pl.BlockSpec((1, cg), lambda g, r: (0, g)),
        ],
        out_specs=pl.BlockSpec((tr, cg), lambda g, r: (r, g)),
        out_shape=jax.ShapeDtypeStruct((s, e), MXU),
        compiler_params=_cp(("parallel", "arbitrary")),
    )(uz, uz, uz, w_g, bias, scale)


def pool_bwd(uz, dy2, w_g, bias, scale, gw_buf, j, tr):
    _, s, e = uz.shape
    ng = len(POOL_WINDOWS)
    cg = e // ng
    nb = s // tr
    hb = tr // MAXW

    def body(u_ref, up_ref, z_ref, dy_ref, w_ref, b_ref, sc_ref, _buf, duz_ref, gw_ref, db_ref, dsc_ref, acc_ref, carry_ref):
        g, step = pl.program_id(0), pl.program_id(1)
        rb = nb - 1 - step
        wmat = w_ref[...].reshape(cg, cg)
        for gg, win in enumerate(POOL_WINDOWS):
            @pl.when(g == gg)
            def _(win=win):
                d = _pool_d(u_ref, up_ref, rb, tr, win).astype(MXU)
                ypre = jnp.dot(d, wmat, preferred_element_type=F32) + b_ref[...]
                z = z_ref[...]
                sg = _sig(z)
                dy2v = dy_ref[...]
                dyv = dy2v * z * sg
                duz_ref[1] = (dy2v * ypre * sc_ref[...] * sg * (1.0 + z * (1.0 - sg))).astype(duz_ref.dtype)
                dypre = dyv * sc_ref[...]
                dsc = jnp.sum(dyv * ypre, axis=0, keepdims=True)
                dbv = jnp.sum(dypre, axis=0, keepdims=True)
                dypre_b = dypre.astype(MXU)
                dd = lax.dot_general(dypre_b, wmat, _DN["nt"], preferred_element_type=F32)
                gw = lax.dot_general(d, dypre_b, _DN["tn"], preferred_element_type=F32)
                q = dd * _pool_inv_count(rb, tr, win, cg)
                nxt = jnp.where(step > 0, carry_ref[...], 0.0)
                lead = _leading_sum(jnp.concatenate([q, nxt], axis=0), win)[:tr]
                duz_ref[0] = (lead - dd).astype(duz_ref.dtype)
                carry_ref[...] = q[:MAXW]

                @pl.when(step == 0)
                def _():
                    acc_ref[...] = gw
                    db_ref[...] = dbv
                    dsc_ref[...] = dsc

                @pl.when(step > 0)
                def _():
                    acc_ref[...] += gw
                    db_ref[...] += dbv
                    dsc_ref[...] += dsc

                @pl.when(step == nb - 1)
                def _():
                    gw_ref[...] = acc_ref[...].reshape(4, cg // 4, cg).astype(gw_ref.dtype)

    return pl.pallas_call(
        body, name="pool_bwd", grid=(ng, nb),
        in_specs=[
            pl.BlockSpec((None, tr, cg), lambda g, r: (0, nb - 1 - r, g)),
            pl.BlockSpec((None, MAXW, cg), lambda g, r: (0, jnp.maximum((nb - 1 - r) * hb - 1, 0), g)),
            pl.BlockSpec((None, tr, cg), lambda g, r: (1, nb - 1 - r, g)),
            pl.BlockSpec((tr, cg), lambda g, r: (nb - 1 - r, g)),
            pl.BlockSpec((4, None, None, cg // 4, cg), lambda g, r: (0, j, g, 0, 0)),
            pl.BlockSpec((1, cg), lambda g, r: (0, g)),
            pl.BlockSpec((1, cg), lambda g, r: (0, g)),
            ANY,
        ],
        out_specs=[
            pl.BlockSpec((2, tr, cg), lambda g, r: (0, nb - 1 - r, g)),
            pl.BlockSpec((4, None, None, cg // 4, cg), lambda g, r: (0, j, g, 0, 0)),
            pl.BlockSpec((1, cg), lambda g, r: (0, g)),
            pl.BlockSpec((1, cg), lambda g, r: (0, g)),
        ],
        out_shape=[
            jax.ShapeDtypeStruct((2, s, e), MXU),
            jax.ShapeDtypeStruct(gw_buf.shape, gw_buf.dtype),
            jax.ShapeDtypeStruct((1, e), F32),
            jax.ShapeDtypeStruct((1, e), F32),
        ],
        scratch_shapes=[pltpu.VMEM((cg, cg), F32), pltpu.VMEM((MAXW, cg), F32)],
        input_output_aliases={7: 1},
        compiler_params=_cp(("parallel", "arbitrary")),
    )(uz, uz, uz, dy2, w_g, bias, scale, gw_buf)


HALO = 8


def _one_minus_sq(log_a, a):
    poly = (-2.0 * log_a) * (1.0 + log_a * (1.0 + log_a * (2.0 / 3.0)))
    return jnp.where(log_a > -0.01, poly, 1.0 - a * a)


def _softplus_neg(lam):
    t = jnp.exp(-jnp.abs(lam))
    log1p = jnp.where(t < 1e-3, t * (1.0 - t * (0.5 - t * (1.0 / 3.0))), jnp.log(1.0 + t))
    return jnp.maximum(-lam, 0.0) + log1p, _sig(-lam)


def _lru_gates(u_ref, up_ref, rb, sm_ref, wa, wx):
    cur = u_ref[...]
    prev = jnp.where(rb > 0, up_ref[...], 0.0)
    ext = jnp.concatenate([prev, cur], axis=0)
    taps = [cur] + [pltpu.roll(ext, k, 0)[HALO:] for k in range(1, CONV_W)]
    uc = sm_ref[CONV_W:CONV_W + 1, :]
    for k in range(CONV_W):
        uc = uc + taps[k] * sm_ref[CONV_W - 1 - k:CONV_W - k, :]
    ucb = uc.astype(MXU)
    r = _sig(jnp.dot(ucb, wa, preferred_element_type=F32) + sm_ref[5:6, :])
    ig = _sig(jnp.dot(ucb, wx, preferred_element_type=F32) + sm_ref[6:7, :])
    sp, sgn = _softplus_neg(sm_ref[7:8, :])
    log_a = r * (-LRU_C * sp)
    a = jnp.exp(log_a)
    mult = jnp.sqrt(jnp.maximum(_one_minus_sq(log_a, a), 0.0))
    return taps, uc, ucb, r, ig, sp, sgn, a, mult


LANES = 128


def _seg_scan(a, b, out_ref, scr, state, reverse):
    a_s, b_s, h_s, p_s = scr
    tr, c = a.shape
    seg = tr // 8
    nl = c // LANES
    for l in range(nl):
        a_s[l] = a[:, l * LANES:(l + 1) * LANES]
        b_s[l] = b[:, l * LANES:(l + 1) * LANES]
    h = [jnp.zeros((8, LANES), F32)] * nl
    pp = [jnp.ones((8, LANES), F32)] * nl
    for i in (range(seg - 1, -1, -1) if reverse else range(seg)):
        rows = pl.ds(i, 8, stride=seg)
        for l in range(nl):
            av = a_s[l, rows, :]
            h[l] = av * h[l] + b_s[l, rows, :]
            pp[l] = av * pp[l]
            h_s[l, rows, :] = h[l]
            p_s[l, rows, :] = pp[l]
    leaving = []
    for l in range(nl):
        st = state[:, l * LANES:(l + 1) * LANES]
        for sgm in (range(7, -1, -1) if reverse else range(8)):
            rows = pl.ds(sgm * seg, seg)
            out_ref[rows, l * LANES:(l + 1) * LANES] = h_s[l, rows, :] + p_s[l, rows, :] * st
            st = h[l][sgm:sgm + 1, :] + pp[l][sgm:sgm + 1, :] * st
        leaving.append(st)
    return jnp.concatenate(leaving, axis=1)


def lru_fwd(uz, wa_g, wx_g, small, j, tr):
    _, s, e = uz.shape
    cb = e // LRU_HEADS
    nb = s // tr
    hb = tr // HALO

    def body(u_ref, up_ref, z_ref, wa_ref, wx_ref, sm_ref, o_ref, h_ref, s0, s1, s2, s3, carry_ref):
        rb = pl.program_id(1)
        wa = wa_ref[...].reshape(cb, cb)
        wx = wx_ref[...].reshape(cb, cb)
        _, uc, _, _, ig, _, _, a, mult = _lru_gates(u_ref, up_ref, rb, sm_ref, wa, wx)
        start = jnp.where(rb > 0, carry_ref[0:1, :], 0.0)
        last = _seg_scan(a, mult * ig * uc, h_ref, (s0, s1, s2, s3), start, False)
        carry_ref[...] = jnp.broadcast_to(last, carry_ref.shape)
        z = z_ref[...]
        o_ref[...] = (h_ref[...] * z * _sig(z)).astype(o_ref.dtype)

    wspec = pl.BlockSpec((4, None, None, cb // 4, cb), lambda h, r: (0, j, h, 0, 0))
    return pl.pallas_call(
        body, name="lru_fwd", grid=(LRU_HEADS, nb),
        in_specs=[
            pl.BlockSpec((None, tr, cb), lambda h, r: (0, r, h)),
            pl.BlockSpec((None, HALO, cb), lambda h, r: (0, jnp.maximum(r * hb - 1, 0), h)),
            pl.BlockSpec((None, tr, cb), lambda h, r: (1, r, h)),
            wspec, wspec,
            pl.BlockSpec((8, cb), lambda h, r: (0, h)),
        ],
        out_specs=[pl.BlockSpec((tr, cb), lambda h, r: (r, h)), pl.BlockSpec((tr, cb), lambda h, r: (r, h))],
        out_shape=[jax.ShapeDtypeStruct((s, e), MXU), jax.ShapeDtypeStruct((s, e), F32)],
        scratch_shapes=[pltpu.VMEM((cb // LANES, tr, LANES), F32)] * 4 + [pltpu.VMEM((8, cb), F32)],
        compiler_params=_cp(("parallel", "arbitrary")),
    )(uz, uz, uz, wa_g, wx_g, small)


def lru_bwd(uz, hst, dy2, wa_g, wx_g, small, gwa_buf, gwx_buf, j, tr):
    _, s, e = uz.shape
    cb = e // LRU_HEADS
    nb = s // tr
    hb = tr // HALO

    def body(u_ref, up_ref, z_ref, h_ref, hp_ref, dy_ref, wa_ref, wx_ref, sm_ref, _b0, _b1,
             duz_ref, gwa_ref, gwx_ref, dsm_ref, s0, s1, s2, s3, g_s, acc_a, acc_x, gcar, acar, dcar):
        step = pl.program_id(1)
        rb = nb - 1 - step
        wa = wa_ref[...].reshape(cb, cb)
        wx = wx_ref[...].reshape(cb, cb)
        taps, uc, ucb, r, ig, sp, sgn, a, mult = _lru_gates(u_ref, up_ref, rb, sm_ref, wa, wx)
        row = lax.broadcasted_iota(jnp.int32, a.shape, 0)
        z = z_ref[...]
        sg = _sig(z)
        dy2v = dy_ref[...]
        hv = h_ref[...]
        duz_ref[1] = (dy2v * hv * sg * (1.0 + z * (1.0 - sg))).astype(duz_ref.dtype)
        a_next = jnp.where(row == tr - 1, jnp.where(step > 0, acar[0:1, :], 0.0), pltpu.roll(a, tr - 1, 0))
        g_first = _seg_scan(a_next, dy2v * z * sg, g_s, (s0, s1, s2, s3), jnp.where(step > 0, gcar[0:1, :], 0.0), True)
        gcar[...] = jnp.broadcast_to(g_first, gcar.shape)
        acar[...] = jnp.broadcast_to(a[0:1, :], acar.shape)
        gv = g_s[...]
        h_before = jnp.where(rb > 0, hp_ref[HALO - 1:HALO, :], 0.0)
        h_prev = jnp.where(row == 0, h_before, pltpu.roll(hv, 1, 0))
        da = gv * h_prev
        gu = gv * uc
        dmult = gu * ig
        dig = gu * mult
        dlog_a = da * a - dmult * jnp.where(mult > 0.0, a * a / mult, 0.0)
        dra = dlog_a * (-LRU_C) * sp * r * (1.0 - r)
        dix = dig * ig * (1.0 - ig)
        dl = jnp.sum(dlog_a * r, axis=0, keepdims=True) * (LRU_C * sgn)
        dra_b, dix_b = dra.astype(MXU), dix.astype(MXU)
        duc = (gv * mult * ig + lax.dot_general(dra_b, wa, _DN["nt"], preferred_element_type=F32)
               + lax.dot_general(dix_b, wx, _DN["nt"], preferred_element_type=F32))
        gwa = lax.dot_general(ucb, dra_b, _DN["tn"], preferred_element_type=F32)
        gwx = lax.dot_general(ucb, dix_b, _DN["tn"], preferred_element_type=F32)
        ext = jnp.concatenate([duc, jnp.where(step > 0, dcar[...], 0.0)], axis=0)
        n = ext.shape[0]
        du = duc * sm_ref[CONV_W - 1:CONV_W, :]
        for k in range(1, CONV_W):
            du = du + pltpu.roll(ext, n - k, 0)[:tr] * sm_ref[CONV_W - 1 - k:CONV_W - k, :]
        duz_ref[0] = du.astype(duz_ref.dtype)
        dcar[...] = duc[:HALO]
        rows = [jnp.sum(duc * taps[CONV_W - 1 - k], axis=0, keepdims=True) for k in range(CONV_W)]
        rows += [jnp.sum(duc, axis=0, keepdims=True), jnp.sum(dra, axis=0, keepdims=True),
                 jnp.sum(dix, axis=0, keepdims=True), dl]

        @pl.when(step == 0)
        def _():
            acc_a[...] = gwa
            acc_x[...] = gwx
            for k, rv in enumerate(rows):
                dsm_ref[k:k + 1, :] = rv

        @pl.when(step > 0)
        def _():
            acc_a[...] += gwa
            acc_x[...] += gwx
            for k, rv in enumerate(rows):
                dsm_ref[k:k + 1, :] += rv

        @pl.when(step == nb - 1)
        def _():
            gwa_ref[...] = acc_a[...].reshape(4, cb // 4, cb).astype(gwa_ref.dtype)
            gwx_ref[...] = acc_x[...].reshape(4, cb // 4, cb).astype(gwx_ref.dtype)

    wspec = pl.BlockSpec((4, None, None, cb // 4, cb), lambda h, r: (0, j, h, 0, 0))
    blk = pl.BlockSpec((tr, cb), lambda h, r: (nb - 1 - r, h))
    return pl.pallas_call(
        body, name="lru_bwd", grid=(LRU_HEADS, nb),
        in_specs=[
            pl.BlockSpec((None, tr, cb), lambda h, r: (0, nb - 1 - r, h)),
            pl.BlockSpec((None, HALO, cb), lambda h, r: (0, jnp.maximum((nb - 1 - r) * hb - 1, 0), h)),
            pl.BlockSpec((None, tr, cb), lambda h, r: (1, nb - 1 - r, h)),
            blk,
            pl.BlockSpec((HALO, cb), lambda h, r: (jnp.maximum((nb - 1 - r) * hb - 1, 0), h)),
            blk,
            wspec, wspec,
            pl.BlockSpec((8, cb), lambda h, r: (0, h)),
            ANY, ANY,
        ],
        out_specs=[
            pl.BlockSpec((2, tr, cb), lambda h, r: (0, nb - 1 - r, h)),
            wspec, wspec,
            pl.BlockSpec((8, cb), lambda h, r: (0, h)),
        ],
        out_shape=[
            jax.ShapeDtypeStruct((2, s, e), MXU),
            jax.ShapeDtypeStruct(gwa_buf.shape, gwa_buf.dtype),
            jax.ShapeDtypeStruct(gwx_buf.shape, gwx_buf.dtype),
            jax.ShapeDtypeStruct((8, e), F32),
        ],
        scratch_shapes=([pltpu.VMEM((cb // LANES, tr, LANES), F32)] * 4 + [pltpu.VMEM((tr, cb), F32)]
                        + [pltpu.VMEM((cb, cb), F32)] * 2 + [pltpu.VMEM((8, cb), F32)] * 3),
        input_output_aliases={9: 1, 10: 2},
        compiler_params=_cp(("parallel", "arbitrary")),
    )(uz, uz, uz, hst, hst, dy2, wa_g, wx_g, small, gwa_buf, gwx_buf)


def _place():
    x, y, c = lax.axis_index("x"), lax.axis_index("y"), lax.axis_index("c")
    chips = [(1 - x, y), (x, 1 - y), (1 - x, 1 - y)]
    return x, y, c, chips


def _rcopy(src, dst, send_sems, recv_sems, k, to):
    return pltpu.make_async_remote_copy(src_ref=src, dst_ref=dst, send_sem=send_sems.at[k], recv_sem=recv_sems.at[k],
                                        device_id=to, device_id_type=MESHID)


def cast_into_slab(w, k_idx):
    l, r, c = w.shape
    rh = r // 2
    tr = _block_rows(rh, c, 4)
    nbh = rh // tr

    def body(k_ref, w_ref, o_ref):
        o_ref[...] = w_ref[...].astype(o_ref.dtype)

    return pl.pallas_call(
        body, name="cast_into_slab",
        grid_spec=pltpu.PrefetchScalarGridSpec(
            num_scalar_prefetch=1, grid=(l, 2, nbh),
            in_specs=[pl.BlockSpec((None, tr, c), lambda i, h, b, k_ref: (i, h * nbh + b, 0))],
            out_specs=pl.BlockSpec((None, None, None, tr, c), lambda i, h, b, k_ref: (k_ref[0], i, h, b, 0))),
        out_shape=jax.ShapeDtypeStruct((4, l, 2, rh, c), WIRE),
        compiler_params=_cp(("parallel", "parallel", "parallel")),
    )(k_idx, w)


def gather_weights(bufs):
    n = len(bufs)

    def body(*refs):
        outs = refs[n:2 * n]
        send_sems, recv_sems = refs[2 * n:]
        x, y, c, chips = _place()
        me = 2 * x + y
        sib = (x, y, 1 - c)
        sends = []
        for a in range(n):
            for jj, ch in enumerate(chips):
                mine = outs[a].at[me, :, c]
                cp = _rcopy(mine, mine, send_sems, recv_sems, 6 * a + jj, (*ch, c))
                cp.start()
                sends.append(cp)
        for a in range(n):
            for jj, ch in enumerate(chips):
                blk = outs[a].at[2 * ch[0] + ch[1], :, c]
                _rcopy(blk, blk, send_sems, recv_sems, 6 * a + jj, (*ch, c)).wait_recv()
                fw = _rcopy(blk, blk, send_sems, recv_sems, 6 * a + 3 + jj, sib)
                fw.start()
                sends.append(fw)
        for a in range(n):
            for jj, ch in enumerate(chips):
                blk = outs[a].at[2 * ch[0] + ch[1], :, 1 - c]
                _rcopy(blk, blk, send_sems, recv_sems, 6 * a + 3 + jj, sib).wait_recv()
        for cp in sends:
            cp.wait_send()

    return pl.pallas_call(
        body, name="gather_weights",
        in_specs=[ANY] * n, out_specs=[ANY] * n,
        out_shape=[jax.ShapeDtypeStruct(a.shape, a.dtype) for a in bufs],
        scratch_shapes=[pltpu.SemaphoreType.DMA((6 * n,)), pltpu.SemaphoreType.DMA((6 * n,))],
        input_output_aliases={a: a for a in range(n)},
        compiler_params=pltpu.CompilerParams(has_side_effects=True),
    )(*bufs)


def all_gather_small(v, name):
    m_per, n = v.shape

    def body(x_ref, out_ref, send_sems, recv_sems, local_sem):
        x, y, c, chips = _place()
        me, sibling = (x, y, c), (x, y, 1 - c)

        def rows(px, py, pc):
            return out_ref.at[pl.ds((4 * px + 2 * py + pc) * m_per, m_per), :]

        def copy(k, block, to, src=None):
            return _rcopy(rows(*block) if src is None else src, rows(*block), send_sems, recv_sems, k, to)

        mine = pltpu.make_async_copy(x_ref, rows(*me), local_sem)
        mine.start()
        first = [copy(0, me, sibling, src=x_ref)]
        first += [copy(1 + jj, me, (*chip, c), src=x_ref) for jj, chip in enumerate(chips)]
        for cp in first:
            cp.start()
        passed = [copy(4 + jj, (*chip, c), sibling) for jj, chip in enumerate(chips)]
        for jj, chip in enumerate(chips):
            copy(1 + jj, (*chip, c), me).wait_recv()
            passed[jj].start()
        copy(0, sibling, me).wait_recv()
        for jj, chip in enumerate(chips):
            copy(4 + jj, (*chip, 1 - c), me).wait_recv()
        for cp in first + passed:
            cp.wait_send()
        mine.wait()

    return pl.pallas_call(
        body, name=name,
        out_shape=jax.ShapeDtypeStruct((8 * m_per, n), v.dtype),
        in_specs=[pl.BlockSpec(memory_space=pltpu.VMEM)],
        out_specs=pl.BlockSpec(memory_space=pltpu.VMEM),
        scratch_shapes=[pltpu.SemaphoreType.DMA((7,)), pltpu.SemaphoreType.DMA((7,)), pltpu.SemaphoreType.DMA],
        compiler_params=pltpu.CompilerParams(vmem_limit_bytes=VMEM_LIMIT),
    )(v)


def sum_devices(g):
    def body(g_ref, o_ref):
        acc = g_ref[0]
        for d in range(1, 8):
            acc = acc + g_ref[d]
        o_ref[...] = acc

    return pl.pallas_call(body, name="sum_devices", out_shape=jax.ShapeDtypeStruct(g.shape[1:], g.dtype),
                          compiler_params=pltpu.CompilerParams(vmem_limit_bytes=VMEM_LIMIT))(g)


def send_other_half(grads):
    n = len(grads)

    def body(*refs):
        ins, outs = refs[:n], refs[n:2 * n]
        send_sems, recv_sems = refs[2 * n:]
        x, y, c, _ = _place()
        sib = (x, y, 1 - c)
        cps = [_rcopy(ins[a].at[:, :, 1 - c], outs[a], send_sems, recv_sems, a, sib) for a in range(n)]
        for cp in cps:
            cp.start()
        for cp in cps:
            cp.wait()

    return pl.pallas_call(
        body, name="send_other_half", in_specs=[ANY] * n, out_specs=[ANY] * n,
        out_shape=[jax.ShapeDtypeStruct(g.shape[:2] + g.shape[3:], g.dtype) for g in grads],
        scratch_shapes=[pltpu.SemaphoreType.DMA((n,)), pltpu.SemaphoreType.DMA((n,))],
        compiler_params=pltpu.CompilerParams(has_side_effects=True),
    )(*grads)


def scatter_to_chips(parts):
    n = len(parts)

    def body(*refs):
        ins, outs = refs[:n], refs[n:2 * n]
        send_sems, recv_sems = refs[2 * n:]
        x, y, c, chips = _place()
        cps = []
        for a in range(n):
            for jj, ch in enumerate(chips):
                cps.append(_rcopy(ins[a].at[2 * ch[0] + ch[1]], outs[a].at[jj], send_sems, recv_sems, 3 * a + jj, (*ch, c)))
        for cp in cps:
            cp.start()
        for cp in cps:
            cp.wait()

    return pl.pallas_call(
        body, name="scatter_to_chips", in_specs=[ANY] * n, out_specs=[ANY] * n,
        out_shape=[jax.ShapeDtypeStruct((3,) + p.shape[1:], p.dtype) for p in parts],
        scratch_shapes=[pltpu.SemaphoreType.DMA((3 * n,)), pltpu.SemaphoreType.DMA((3 * n,))],
        compiler_params=pltpu.CompilerParams(has_side_effects=True),
    )(*parts)


def share_halves(bufs):
    n = len(bufs)

    def body(*refs):
        outs = refs[n:2 * n]
        send_sems, recv_sems = refs[2 * n:]
        x, y, c, _ = _place()
        sib = (x, y, 1 - c)
        cps = [_rcopy(outs[a].at[:, c], outs[a].at[:, c], send_sems, recv_sems, a, sib) for a in range(n)]
        for cp in cps:
            cp.start()
        for a in range(n):
            blk = outs[a].at[:, 1 - c]
            _rcopy(blk, blk, send_sems, recv_sems, a, sib).wait_recv()
        for cp in cps:
            cp.wait_send()

    return pl.pallas_call(
        body, name="share_halves", in_specs=[ANY] * n, out_specs=[ANY] * n,
        out_shape=[jax.ShapeDtypeStruct(b.shape, b.dtype) for b in bufs],
        scratch_shapes=[pltpu.SemaphoreType.DMA((n,)), pltpu.SemaphoreType.DMA((n,))],
        input_output_aliases={a: a for a in range(n)},
        compiler_params=pltpu.CompilerParams(has_side_effects=True),
    )(*bufs)


def _block_rows(r, c, itemsize, budget=1 << 20):
    tr = r
    while tr * c * itemsize > budget and tr % 16 == 0:
        tr //= 2
    return tr


def add_halves(g, got, c_idx):
    k4, l, _, rh, cc = g.shape
    tr = _block_rows(rh, cc, 4)

    def body(c_ref, g_ref, r_ref, o_ref):
        o_ref[...] = (g_ref[...].astype(F32) + r_ref[...].astype(F32)).astype(o_ref.dtype)

    return pl.pallas_call(
        body, name="add_halves",
        grid_spec=pltpu.PrefetchScalarGridSpec(
            num_scalar_prefetch=1, grid=(k4, l, rh // tr),
            in_specs=[pl.BlockSpec((None, None, None, tr, cc), lambda k, i, b, c_ref: (k, i, c_ref[0], b, 0)),
                      pl.BlockSpec((None, None, tr, cc), lambda k, i, b, c_ref: (k, i, b, 0))],
            out_specs=pl.BlockSpec((None, None, tr, cc), lambda k, i, b, c_ref: (k, i, b, 0))),
        out_shape=jax.ShapeDtypeStruct(got.shape, WIRE),
        compiler_params=_cp(("parallel", "parallel", "parallel")),
    )(c_idx, g, got)


def sum_chips(own, got, kc_idx):
    _, l, rh, cc = own.shape
    tr = _block_rows(rh, cc, 4)

    def body(k_ref, o_ref, r_ref, s_ref):
        s_ref[...] = ((o_ref[...].astype(F32) + r_ref[0].astype(F32)) + r_ref[1].astype(F32)) + r_ref[2].astype(F32)

    return pl.pallas_call(
        body, name="sum_chips",
        grid_spec=pltpu.PrefetchScalarGridSpec(
            num_scalar_prefetch=1, grid=(l, rh // tr),
            in_specs=[pl.BlockSpec((None, None, tr, cc), lambda i, b, k_ref: (k_ref[0], i, b, 0)),
                      pl.BlockSpec((3, None, tr, cc), lambda i, b, k_ref: (0, i, b, 0))],
            out_specs=pl.BlockSpec((None, None, tr, cc), lambda i, b, k_ref: (i, k_ref[1], b, 0))),
        out_shape=jax.ShapeDtypeStruct((l, 2, rh, cc), F32),
        compiler_params=_cp(("parallel", "parallel")),
    )(kc_idx, own, got)


def _adam_math(w, g, m, v):
    m = ADAM_B1 * m + (1.0 - ADAM_B1) * g
    v = ADAM_B2 * v + (1.0 - ADAM_B2) * (g * g)
    m_hat = m / (1.0 - ADAM_B1 ** ADAM_STEP)
    v_hat = v / (1.0 - ADAM_B2 ** ADAM_STEP)
    delta = -ADAM_LR * (m_hat / (jnp.sqrt(v_hat) + ADAM_EPS) + ADAM_WD * w)
    return delta, m, v


def adamw(w, g, m, v):
    l, r, c = w.shape
    tr = _block_rows(r, c, 4)

    def body(w_ref, g_ref, m_ref, v_ref, go_ref, d_ref, mo_ref, vo_ref):
        gv = g_ref[...]
        go_ref[...] = gv
        d_ref[...], mo_ref[...], vo_ref[...] = _adam_math(w_ref[...], gv, m_ref[...], v_ref[...])

    spec = pl.BlockSpec((None, tr, c), lambda i, b: (i, b, 0))
    return pl.pallas_call(
        body, name="adamw", grid=(l, r // tr), in_specs=[spec] * 4, out_specs=[spec] * 4,
        out_shape=[jax.ShapeDtypeStruct(w.shape, F32)] * 4, compiler_params=_cp(("parallel", "parallel")),
    )(w, g, m, v)


def adamw_small(w, g, m, v):
    def body(w_ref, g_ref, m_ref, v_ref, d_ref, mo_ref, vo_ref):
        d_ref[...], mo_ref[...], vo_ref[...] = _adam_math(w_ref[...], g_ref[...], m_ref[...], v_ref[...])

    return pl.pallas_call(body, name="adamw_small", out_shape=[jax.ShapeDtypeStruct(w.shape, F32)] * 3)(w, g, m, v)


def kernel(x, p, w_in, w_out, g_pre, g_post, pool_w, pool_b, pool_scale, conv_w, conv_b, lru_wa, lru_ba, lru_wx, lru_bx, lru_L, w_ple, w_ple_gate, g_ple_in, g_ple_out, loss_target, m_w_in, m_w_out, m_g_pre, m_g_post, m_pool_w, m_pool_b, m_pool_scale, m_conv_w, m_conv_b, m_lru_wa, m_lru_ba, m_lru_wx, m_lru_bx, m_lru_L, m_w_ple, m_w_ple_gate, m_g_ple_in, m_g_ple_out, v_w_in, v_w_out, v_g_pre, v_g_post, v_pool_w, v_pool_b, v_pool_scale, v_conv_w, v_conv_b, v_lru_wa, v_lru_ba, v_lru_wx, v_lru_bx, v_lru_L, v_w_ple, v_w_ple_gate, v_g_ple_in, v_g_ple_out):
    depth = w_in.shape[0]
    _, s, d = x.shape
    e = 2 * d
    kp = p.shape[-1]
    nmix = pool_w.shape[0]
    ngrp = pool_w.shape[1]
    cg = e // ngrp
    cb = e // LRU_HEADS
    xi, yi, ci = lax.axis_index("x"), lax.axis_index("y"), lax.axis_index("c")
    me = 2 * xi + yi
    c_idx = jnp.reshape(ci, (1,)).astype(jnp.int32)
    k_idx = jnp.reshape(me, (1,)).astype(jnp.int32)
    tr_row = _tile(s, 256)
    tr_mix = _tile(s, 512)
    tm = _tile(s, 1024)

    def halves(a):
        return a.reshape(a.shape[0], 2, a.shape[1] // 2, a.shape[2])

    big = {
        "w_in": w_in, "w_out": w_out, "gate": w_ple_gate, "ple": w_ple,
        "pool": pool_w.reshape(nmix * ngrp, cg // 4, cg),
        "wa": lru_wa.reshape(nmix * LRU_HEADS, cb // 4, cb), "wx": lru_wx.reshape(nmix * LRU_HEADS, cb // 4, cb),
    }
    names = list(big)
    gathered = gather_weights([cast_into_slab(big[n], k_idx) for n in names])
    gw = {n: g.reshape((4,) + big[n].shape) for n, g in zip(names, gathered)}
    w_in_g, w_out_g, gate_g, ple_g = gw["w_in"], gw["w_out"], gw["gate"], gw["ple"]
    pool_g = gw["pool"].reshape(4, nmix, ngrp, cg // 4, cg)
    wa_g = gw["wa"].reshape(4, nmix, LRU_HEADS, cb // 4, cb)
    wx_g = gw["wx"].reshape(4, nmix, LRU_HEADS, cb // 4, cb)

    ec = e // 4
    small_loc = jnp.concatenate([conv_w, conv_b[:, None], lru_ba[:, None], lru_bx[:, None], lru_L[:, None]], axis=1)
    sm_all = all_gather_small(small_loc.reshape(nmix * 8, ec), "gather_small").reshape(4, 2, nmix, 8, ec)
    lru_small = jnp.transpose(sm_all[:, 0], (1, 2, 0, 3)).reshape(nmix, 8, e)

    xs = x[0]
    saved = []
    for i in range(depth):
        j = i // 2
        h = rms_fwd(xs, g_pre[i][None], tr_row)
        nj = (2 * e) // 1024 if (2 * e) % 1024 == 0 else 4
        tn = (2 * e) // nj
        per = e // tn
        perk = (e // 2) // tn
        uz = _matmul(
            "mm_in", "nn", (s // tm, nj, 1), h, pl.BlockSpec((tm, d), lambda a, b, k: (a, 0)),
            w_in_g, pl.BlockSpec((None, None, d, tn), lambda a, b, k, i=i, perk=perk: (b // perk, i, 0, b % perk)),
            jax.ShapeDtypeStruct((2, s, e), F32), pl.BlockSpec((None, tm, tn), lambda a, b, k, per=per: (b // per, a, b % per)),
            (8, 128))
        if i % 2 == 0:
            y2 = pool_fwd(uz, pool_g, pool_b[j][None], pool_scale[j][None], j, tr_mix)
            hst = None
        else:
            y2, hst = lru_fwd(uz, wa_g, wx_g, lru_small[j], j, tr_mix)
        tn_o = _tile(d, 1024)
        o = _matmul(
            "mm_out", "nn", (s // tm, d // tn_o, 4), y2, pl.BlockSpec((tm, e // 4), lambda a, b, k: (a, k)),
            w_out_g, pl.BlockSpec((None, None, e // 4, tn_o), lambda a, b, k, i=i: (k, i, 0, b)),
            jax.ShapeDtypeStruct((s, d), F32), pl.BlockSpec((tm, tn_o), lambda a, b, k: (a, b)), (tm, tn_o))
        x1 = res_rms_fwd(xs, o, g_post[i][None], tr_row)
        hn = rms_fwd(x1, g_ple_in[i][None], tr_row)
        gpre = _matmul(
            "mm_gate", "nn", (s // tm, d // tn_o, 4), hn, pl.BlockSpec((tm, d // 4), lambda a, b, k: (a, k)),
            gate_g, pl.BlockSpec((None, None, d // 4, tn_o), lambda a, b, k, i=i: (k, i, 0, b)),
            jax.ShapeDtypeStruct((s, d), F32), pl.BlockSpec((tm, tn_o), lambda a, b, k: (a, b)), (tm, tn_o))
        pe = p[i, 0]
        ev = _matmul(
            "mm_ple", "nn", (s // tm, 4, 1), pe, pl.BlockSpec((tm, kp), lambda a, b, k: (a, 0)),
            ple_g, pl.BlockSpec((None, None, kp, d // 4), lambda a, b, k, i=i: (b, i, 0, 0)),
            jax.ShapeDtypeStruct((s, d), F32), pl.BlockSpec((tm, d // 4), lambda a, b, k: (a, b)), (8, 128))
        x2 = ple_fwd(x1, gpre, ev, g_ple_out[i][None], tr_row)
        saved.append((xs, h, uz, y2, hst, o, x1, hn, gpre, ev))
        xs = x2

    dx, sq = loss_bwd(xs, loss_target[0], tr_row)
    gbuf = {n: lax.empty((4,) + big[n].shape, WIRE) for n in names}
    gbuf["pool"] = gbuf["pool"].reshape(4, nmix, ngrp, cg // 4, cg)
    gbuf["wa"] = gbuf["wa"].reshape(4, nmix, LRU_HEADS, cb // 4, cb)
    gbuf["wx"] = gbuf["wx"].reshape(4, nmix, LRU_HEADS, cb // 4, cb)
    d_gpre, d_gpost, d_gin, d_gout = [None] * depth, [None] * depth, [None] * depth, [None] * depth
    d_pool_b, d_pool_sc, d_lru_small = [None] * nmix, [None] * nmix, [None] * nmix
    ts = _tile(s, 1024)
    for i in reversed(range(depth)):
        j = i // 2
        x0, h, uz, y2, hst, o, x1, hn, gpre, ev = saved[i]
        pe = p[i, 0]
        de, dgp, d_gout[i] = ple_bwd(dx, gpre, ev, g_ple_out[i][None], tr_row)
        gbuf["ple"] = _matmul(
            "mm_dple", "tn", (1, 4, s // ts), pe, pl.BlockSpec((ts, kp), lambda a, b, k: (k, 0)),
            de, pl.BlockSpec((ts, d // 4), lambda a, b, k: (k, b)),
            jax.ShapeDtypeStruct(gbuf["ple"].shape, WIRE), pl.BlockSpec((None, None, kp, d // 4), lambda a, b, k, i=i: (b, i, 0, 0)),
            (kp, d // 4), into=gbuf["ple"])
        gbuf["gate"] = _matmul(
            "mm_dgate", "tn", (4, 1, s // ts), hn, pl.BlockSpec((ts, d // 4), lambda a, b, k: (k, a)),
            dgp, pl.BlockSpec((ts, d), lambda a, b, k: (k, 0)),
            jax.ShapeDtypeStruct(gbuf["gate"].shape, WIRE), pl.BlockSpec((None, None, d // 4, d), lambda a, b, k, i=i: (a, i, 0, 0)),
            (d // 4, d), into=gbuf["gate"])
        dhn = _matmul(
            "mm_dhn", "nt", (s // tm, 4, 1), dgp, pl.BlockSpec((tm, d), lambda a, b, k: (a, 0)),
            gate_g, pl.BlockSpec((None, None, d // 4, d), lambda a, b, k, i=i: (b, i, 0, 0)),
            jax.ShapeDtypeStruct((s, d), F32), pl.BlockSpec((tm, d // 4), lambda a, b, k: (a, b)), (8, 128))
        dx1, d_gin[i] = rms_bwd_res(dx, dhn, x1, g_ple_in[i][None], tr_row)
        do, d_gpost[i] = rms_bwd(dx1, o, g_post[i][None], tr_row)
        tn_o = _tile(d, 1024)
        gbuf["w_out"] = _matmul(
            "mm_dwout", "tn", (4, d // tn_o, s // ts), y2, pl.BlockSpec((ts, e // 4), lambda a, b, k: (k, a)),
            do, pl.BlockSpec((ts, tn_o), lambda a, b, k: (k, b)),
            jax.ShapeDtypeStruct(gbuf["w_out"].shape, WIRE), pl.BlockSpec((None, None, e // 4, tn_o), lambda a, b, k, i=i: (a, i, 0, b)),
            (e // 4, tn_o), into=gbuf["w_out"])
        dy2 = _matmul(
            "mm_dy2", "nt", (s // tm, 4, 1), do, pl.BlockSpec((tm, d), lambda a, b, k: (a, 0)),
            w_out_g, pl.BlockSpec((None, None, e // 4, d), lambda a, b, k, i=i: (b, i, 0, 0)),
            jax.ShapeDtypeStruct((s, e), F32), pl.BlockSpec((tm, e // 4), lambda a, b, k: (a, b)), (8, 128))
        if i % 2 == 0:
            duz, gbuf["pool"], d_pool_b[j], d_pool_sc[j] = pool_bwd(uz, dy2, pool_g, pool_b[j][None], pool_scale[j][None], gbuf["pool"], j, tr_mix)
        else:
            duz, gbuf["wa"], gbuf["wx"], d_lru_small[j] = lru_bwd(uz, hst, dy2, wa_g, wx_g, lru_small[j], gbuf["wa"], gbuf["wx"], j, tr_mix)
        tmi = _tile(d, 1024)
        tni = _tile(e // 2, 1024)
        nslab = (e // 2) // tni
        gbuf["w_in"] = _matmul(
            "mm_dwin", "tn", (d // tmi, 4 * nslab, s // ts), h, pl.BlockSpec((ts, tmi), lambda a, b, k: (k, a)),
            duz, pl.BlockSpec((None, ts, tni), lambda a, b, k, nslab=nslab: (b // (2 * nslab), k, b % (2 * nslab))),
            jax.ShapeDtypeStruct(gbuf["w_in"].shape, WIRE),
            pl.BlockSpec((None, None, tmi, tni), lambda a, b, k, i=i, nslab=nslab: (b // nslab, i, a, b % nslab)),
            (tmi, tni), into=gbuf["w_in"])
        tnd = _tile(d, 1024)
        dh = _matmul(
            "mm_dh", "nt", (s // tm, d // tnd, 4), duz, pl.BlockSpec((None, tm, e // 2), lambda a, b, k: (k // 2, a, k % 2)),
            w_in_g, pl.BlockSpec((None, None, tnd, e // 2), lambda a, b, k, i=i: (k, i, b, 0)),
            jax.ShapeDtypeStruct((s, d), F32), pl.BlockSpec((tm, tnd), lambda a, b, k: (a, b)), (tm, tnd))
        dx, d_gpre[i] = rms_bwd_res(dx1, dh, x0, g_pre[i][None], tr_row)
    grad_x = dx[None]

    gparts = [gbuf[n].reshape((4,) + halves(big[n]).shape) for n in names]
    from_sib = send_other_half(gparts)
    chip_sums = [add_halves(g, r, c_idx) for g, r in zip(gparts, from_sib)]
    from_chips = scatter_to_chips(chip_sums)
    kc_idx = jnp.stack([me, ci]).astype(jnp.int32)
    full = share_halves([sum_chips(o, r, kc_idx) for o, r in zip(chip_sums, from_chips)])
    grads = {n: f.reshape(big[n].shape) for n, f in zip(names, full)}

    def rows_e(a):
        return jnp.stack(a).reshape(-1, e) if isinstance(a, list) else a.reshape(-1, e)

    pack = [rows_e([g[0] for g in d_gpre]), rows_e([g[0] for g in d_gpost]), rows_e([g[0] for g in d_gin]), rows_e([g[0] for g in d_gout]),
            jnp.concatenate(d_pool_b, axis=0), jnp.concatenate(d_pool_sc, axis=0), jnp.concatenate(d_lru_small, axis=0),
            jnp.pad(sq, ((0, 0), (0, e - d)))]
    sizes = [a.shape[0] for a in pack]
    packed = jnp.concatenate(pack, axis=0)
    nrow = packed.shape[0]
    nrow_p = -(-nrow // 8) * 8
    packed = jnp.pad(packed, ((0, nrow_p - nrow), (0, 0)))
    total = sum_devices(all_gather_small(packed, "gather_grads").reshape(8, nrow_p, e))
    parts, off = [], 0
    for n_ in sizes:
        parts.append(total[off:off + n_])
        off += n_
    t_gpre, t_gpost, t_gin, t_gout, t_pb, t_psc, t_lru, t_sq = parts
    loss = 0.5 * jnp.sum(t_sq) / d
    t_lru = t_lru.reshape(nmix, 8, e)
    t_lru_loc = lax.dynamic_slice_in_dim(t_lru, me * ec, ec, axis=2)

    def big_update(name, w, m, v):
        shp = big[name].shape
        g, dl, nm, nv = adamw(w.reshape(shp), grads[name], m.reshape(shp), v.reshape(shp))
        return [a.reshape(w.shape) for a in (g, dl, nm, nv)]

    def small_update(w, g, m, v):
        shp = w.shape
        w2 = w.reshape(-1, shp[-1])
        dl, nm, nv = adamw_small(w2, g.reshape(w2.shape), m.reshape(w2.shape), v.reshape(w2.shape))
        return [g.reshape(shp), dl.reshape(shp), nm.reshape(shp), nv.reshape(shp)]

    res = {
        "w_in": big_update("w_in", w_in, m_w_in, v_w_in),
        "w_out": big_update("w_out", w_out, m_w_out, v_w_out),
        "g_pre": small_update(g_pre, t_gpre.reshape(depth, d), m_g_pre, v_g_pre),
        "g_post": small_update(g_post, t_gpost.reshape(depth, d), m_g_post, v_g_post),
        "pool_w": big_update("pool", pool_w, m_pool_w, v_pool_w),
        "pool_b": small_update(pool_b, t_pb, m_pool_b, v_pool_b),
        "pool_scale": small_update(pool_scale, t_psc, m_pool_scale, v_pool_scale),
        "conv_w": small_update(conv_w, t_lru_loc[:, :CONV_W], m_conv_w, v_conv_w),
        "conv_b": small_update(conv_b, t_lru_loc[:, 4], m_conv_b, v_conv_b),
        "lru_wa": big_update("wa", lru_wa, m_lru_wa, v_lru_wa),
        "lru_ba": small_update(lru_ba, t_lru_loc[:, 5], m_lru_ba, v_lru_ba),
        "lru_wx": big_update("wx", lru_wx, m_lru_wx, v_lru_wx),
        "lru_bx": small_update(lru_bx, t_lru_loc[:, 6], m_lru_bx, v_lru_bx),
        "lru_L": small_update(lru_L, t_lru_loc[:, 7], m_lru_L, v_lru_L),
        "w_ple": big_update("ple", w_ple, m_w_ple, v_w_ple),
        "w_ple_gate": big_update("gate", w_ple_gate, m_w_ple_gate, v_w_ple_gate),
        "g_ple_in": small_update(g_ple_in, t_gin.reshape(depth, d), m_g_ple_in, v_g_ple_in),
        "g_ple_out": small_update(g_ple_out, t_gout.reshape(depth, d), m_g_ple_out, v_g_ple_out),
    }
    order = ["w_in", "w_out", "g_pre", "g_post", "pool_w", "pool_b", "pool_scale", "conv_w", "conv_b", "lru_wa", "lru_ba",
             "lru_wx", "lru_bx", "lru_L", "w_ple", "w_ple_gate", "g_ple_in", "g_ple_out"]
    out = [loss, grad_x]
    for slot in range(4):
        out += [res[n][slot] for n in order]
    return tuple(out)
```

```python
import functools

import jax
import jax.numpy as jnp
from jax import lax
from jax.experimental import pallas as pl
from jax.experimental.pallas import tpu as pltpu

F32 = jnp.float32
MXU = jnp.bfloat16
WIRE = jnp.bfloat16
VMEM_LIMIT = 56 * 1024 * 1024
RMS_EPS = 1e-6
LRU_C = 8.0
POOL_WINDOWS = (2, 4, 8, 16)
MAXW = 16
CONV_W = 4
LRU_HEADS = 16
ADAM_LR, ADAM_B1, ADAM_B2, ADAM_EPS, ADAM_WD, ADAM_STEP = 0.001, 0.9, 0.999, 1e-08, 0.01, 10
MESHID = pl.DeviceIdType.MESH
ANY = pl.BlockSpec(memory_space=pl.ANY)


def _cp(sem=None):
    return pltpu.CompilerParams(dimension_semantics=sem, vmem_limit_bytes=VMEM_LIMIT)


def _sig(v):
    return 0.5 * jnp.tanh(0.5 * v) + 0.5


def _tile(n, pref):
    return pref if n % pref == 0 else n


_DN = {"nn": (((1,), (0,)), ((), ())), "nt": (((1,), (1,)), ((), ())), "tn": (((0,), (0,)), ((), ()))}


def _matmul(name, mode, grid, a, a_spec, b, b_spec, out_shape, out_spec, acc_shape, hosted=()):
    nk = grid[2]
    ro = [r for st in hosted for r in st["ro"]]
    bufs = [r for st in hosted for r in st["bufs"]]
    nro, nbuf = len(ro), len(bufs)

    def body(a_ref, b_ref, *rest):
        ro_refs = rest[:nro]
        o_ref = rest[nro + nbuf]
        buf_refs = rest[nro + nbuf + 1:nro + 2 * nbuf + 1]
        acc_ref = rest[nro + 2 * nbuf + 1]
        ids = [pl.program_id(t) for t in range(3)]
        kk = ids[2]

        def run_stages(what):
            r0 = b0 = s0 = 0
            for st in hosted:
                st[what](ro_refs[r0:r0 + len(st["ro"])], buf_refs[b0:b0 + len(st["bufs"])], rest[-2], rest[-1], s0)
                r0, b0, s0 = r0 + len(st["ro"]), b0 + len(st["bufs"]), s0 + st["nsem"]

        if hosted:
            @pl.when((ids[0] == 0) & (ids[1] == 0) & (ids[2] == 0))
            def _():
                run_stages("start")

        prod = lax.dot_general(a_ref[...].astype(MXU), b_ref[...].astype(MXU), _DN[mode], preferred_element_type=F32)
        if nk == 1:
            o_ref[...] = prod.astype(o_ref.dtype)
        else:
            @pl.when(kk == 0)
            def _():
                acc_ref[...] = prod

            @pl.when(kk > 0)
            def _():
                acc_ref[...] += prod

            @pl.when(kk == nk - 1)
            def _():
                o_ref[...] = acc_ref[...].astype(o_ref.dtype)

        if hosted:
            @pl.when((ids[0] == grid[0] - 1) & (ids[1] == grid[1] - 1) & (ids[2] == grid[2] - 1))
            def _():
                run_stages("finish")

    scratch = [pltpu.VMEM(acc_shape, F32)]
    if not hosted:
        return pl.pallas_call(
            body, name=name, grid=grid, in_specs=[a_spec, b_spec], out_specs=out_spec, out_shape=out_shape,
            scratch_shapes=scratch, compiler_params=_cp(("parallel", "parallel", "arbitrary")),
        )(a, b)
    nsem = sum(st["nsem"] for st in hosted)
    res = pl.pallas_call(
        body, name=name, grid=grid,
        in_specs=[a_spec, b_spec] + [ANY] * (nro + nbuf), out_specs=[out_spec] + [ANY] * nbuf,
        out_shape=[out_shape] + [jax.ShapeDtypeStruct(x.shape, x.dtype) for x in bufs],
        scratch_shapes=scratch + [pltpu.SemaphoreType.DMA((nsem,)), pltpu.SemaphoreType.DMA((nsem,))],
        input_output_aliases={2 + nro + t: 1 + t for t in range(nbuf)},
        compiler_params=_cp(("arbitrary", "arbitrary", "arbitrary")),
    )(a, b, *ro, *bufs)
    return res[0], list(res[1:])


def _rows_call(name, body, ins, in_rows, outs, out_rows, n_rows, tr):
    def spec(shape, tiled):
        if tiled:
            return pl.BlockSpec((tr, shape[1]), lambda i: (i, 0))
        return pl.BlockSpec(shape, lambda i: (0, 0))

    return pl.pallas_call(
        body, name=name, grid=(n_rows // tr,),
        in_specs=[spec(a.shape, t) for a, t in zip(ins, in_rows)],
        out_specs=[spec(o.shape, t) for o, t in zip(outs, out_rows)],
        out_shape=outs, compiler_params=_cp(("arbitrary",)),
    )(*ins)


def _rstd(v):
    return lax.rsqrt(jnp.mean(v * v, axis=-1, keepdims=True) + RMS_EPS)


def _norm_bwd(v, g, dy):
    r = _rstd(v)
    n = v * r
    dn = dy * g
    dv = r * (dn - n * jnp.mean(dn * n, axis=-1, keepdims=True))
    return dv, jnp.sum(dy * n, axis=0, keepdims=True)


def _acc_rows(ref, val):
    @pl.when(pl.program_id(0) == 0)
    def _():
        ref[...] = val

    @pl.when(pl.program_id(0) > 0)
    def _():
        ref[...] += val


def rms_fwd(x, g, tr):
    def body(x_ref, g_ref, o_ref):
        v = x_ref[...]
        o_ref[...] = (v * _rstd(v) * g_ref[...]).astype(o_ref.dtype)

    return _rows_call("rms_fwd", body, [x, g], [True, False], [jax.ShapeDtypeStruct(x.shape, MXU)], [True], x.shape[0], tr)[0]


def res_rms_fwd(x, o, g, tr):
    def body(x_ref, o_ref, g_ref, y_ref):
        v = o_ref[...]
        y_ref[...] = x_ref[...] + v * _rstd(v) * g_ref[...]

    return _rows_call("res_rms_fwd", body, [x, o, g], [True, True, False], [jax.ShapeDtypeStruct(x.shape, F32)], [True], x.shape[0], tr)[0]


def ple_fwd(x1, gpre, e, g, tr):
    def body(x_ref, gp_ref, e_ref, g_ref, y_ref):
        v = e_ref[...] * _sig(gp_ref[...])
        y_ref[...] = x_ref[...] + v * _rstd(v) * g_ref[...]

    return _rows_call("ple_fwd", body, [x1, gpre, e, g], [True, True, True, False], [jax.ShapeDtypeStruct(x1.shape, F32)], [True], x1.shape[0], tr)[0]


def loss_bwd(y, target, tr):
    d = y.shape[1]

    def body(y_ref, t_ref, dy_ref, sq_ref):
        diff = y_ref[...] - t_ref[...]
        dy_ref[...] = diff * (1.0 / d)
        _acc_rows(sq_ref, jnp.sum(diff * diff, axis=0, keepdims=True))

    return _rows_call("loss_bwd", body, [y, target], [True, True],
                      [jax.ShapeDtypeStruct(y.shape, F32), jax.ShapeDtypeStruct((1, d), F32)], [True, False], y.shape[0], tr)


def ple_bwd(dx2, gpre, e, g, tr):
    d = dx2.shape[1]

    def body(dx_ref, gp_ref, e_ref, g_ref, de_ref, dgp_ref, dg_ref):
        gate = _sig(gp_ref[...])
        ev = e_ref[...]
        dv, dg = _norm_bwd(ev * gate, g_ref[...], dx_ref[...])
        de_ref[...] = (dv * gate).astype(de_ref.dtype)
        dgp_ref[...] = (dv * ev * gate * (1.0 - gate)).astype(dgp_ref.dtype)
        _acc_rows(dg_ref, dg)

    return _rows_call("ple_bwd", body, [dx2, gpre, e, g], [True, True, True, False],
                      [jax.ShapeDtypeStruct(dx2.shape, MXU), jax.ShapeDtypeStruct(dx2.shape, MXU), jax.ShapeDtypeStruct((1, d), F32)],
                      [True, True, False], dx2.shape[0], tr)


def rms_bwd_res(dres, dh, x, g, tr):
    d = x.shape[1]

    def body(dr_ref, dh_ref, x_ref, g_ref, dx_ref, dg_ref):
        dv, dg = _norm_bwd(x_ref[...], g_ref[...], dh_ref[...])
        dx_ref[...] = dr_ref[...] + dv
        _acc_rows(dg_ref, dg)

    return _rows_call("rms_bwd_res", body, [dres, dh, x, g], [True, True, True, False],
                      [jax.ShapeDtypeStruct(x.shape, F32), jax.ShapeDtypeStruct((1, d), F32)], [True, False], x.shape[0], tr)


def rms_bwd(dy, o, g, tr):
    d = o.shape[1]

    def body(dy_ref, o_ref, g_ref, do_ref, dg_ref):
        dv, dg = _norm_bwd(o_ref[...], g_ref[...], dy_ref[...])
        do_ref[...] = dv.astype(do_ref.dtype)
        _acc_rows(dg_ref, dg)

    return _rows_call("rms_bwd", body, [dy, o, g], [True, True, False],
                      [jax.ShapeDtypeStruct(o.shape, MXU), jax.ShapeDtypeStruct((1, d), F32)], [True, False], o.shape[0], tr)


def _trailing_sum(ext, w):
    s, k = ext, 1
    while k < w:
        s = s + pltpu.roll(s, k, 0)
        k *= 2
    return s


def _leading_sum(ext, w):
    n = ext.shape[0]
    s, k = ext, 1
    while k < w:
        s = s + pltpu.roll(s, n - k, 0)
        k *= 2
    return s


def _pool_inv_count(rb, tr, w, c):
    t = rb * tr + lax.broadcasted_iota(jnp.int32, (tr, c), 0)
    return 1.0 / jnp.minimum(t + 1, w).astype(F32)


def _pool_d(u_ref, up_ref, rb, tr, w):
    cur = u_ref[...]
    prev = jnp.where(rb > 0, up_ref[...], 0.0)
    ext = jnp.concatenate([prev, cur], axis=0)
    win = _trailing_sum(ext, w)[MAXW:]
    return win * _pool_inv_count(rb, tr, w, cur.shape[1]) - cur


def pool_fwd(uz, w_g, bias, scale, tr):
    _, s, e = uz.shape
    ng = len(POOL_WINDOWS)
    cg = e // ng
    nb = s // tr
    hb = tr // MAXW

    def body(u_ref, up_ref, z_ref, w_ref, b_ref, sc_ref, o_ref):
        g, rb = pl.program_id(0), pl.program_id(1)
        wmat = w_ref[...].reshape(cg, cg)
        for gg, win in enumerate(POOL_WINDOWS):
            @pl.when(g == gg)
            def _(win=win):
                d = _pool_d(u_ref, up_ref, rb, tr, win)
                y = (jnp.dot(d.astype(MXU), wmat, preferred_element_type=F32) + b_ref[...]) * sc_ref[...]
                z = z_ref[...]
                o_ref[...] = (y * z * _sig(z)).astype(o_ref.dtype)

    return pl.pallas_call(
        body, name="pool_fwd", grid=(ng, nb),
        in_specs=[
            pl.BlockSpec((None, tr, cg), lambda g, r: (0, r, g)),
            pl.BlockSpec((None, MAXW, cg), lambda g, r: (0, jnp.maximum(r * hb - 1, 0), g)),
            pl.BlockSpec((None, tr, cg), lambda g, r: (1, r, g)),
            pl.BlockSpec((4, None, cg // 4, cg), lambda g, r: (0, g, 0, 0)),
            pl.BlockSpec((1, cg), lambda g, r: (0, g)),
            pl.BlockSpec((1, cg), lambda g, r: (0, g)),
        ],
        out_specs=pl.BlockSpec((tr, cg), lambda g, r: (r, g)),
        out_shape=jax.ShapeDtypeStruct((s, e), MXU),
        compiler_params=_cp(("parallel", "arbitrary")),
    )(uz, uz, uz, w_g, bias, scale)


def pool_bwd(uz, dy2, w_g, bias, scale, tr):
    _, s, e = uz.shape
    ng = len(POOL_WINDOWS)
    cg = e // ng
    nb = s // tr
    hb = tr // MAXW

    def body(u_ref, up_ref, z_ref, dy_ref, w_ref, b_ref, sc_ref, duz_ref, gw_ref, db_ref, dsc_ref, acc_ref, carry_ref):
        g, step = pl.program_id(0), pl.program_id(1)
        rb = nb - 1 - step
        wmat = w_ref[...].reshape(cg, cg)
        for gg, win in enumerate(POOL_WINDOWS):
            @pl.when(g == gg)
            def _(win=win):
                d = _pool_d(u_ref, up_ref, rb, tr, win).astype(MXU)
                ypre = jnp.dot(d, wmat, preferred_element_type=F32) + b_ref[...]
                z = z_ref[...]
                sg = _sig(z)
                dy2v = dy_ref[...]
                dyv = dy2v * z * sg
                duz_ref[1] = (dy2v * ypre * sc_ref[...] * sg * (1.0 + z * (1.0 - sg))).astype(duz_ref.dtype)
                dypre = dyv * sc_ref[...]
                dsc = jnp.sum(dyv * ypre, axis=0, keepdims=True)
                dbv = jnp.sum(dypre, axis=0, keepdims=True)
                dypre_b = dypre.astype(MXU)
                dd = lax.dot_general(dypre_b, wmat, _DN["nt"], preferred_element_type=F32)
                gw = lax.dot_general(d, dypre_b, _DN["tn"], preferred_element_type=F32)
                q = dd * _pool_inv_count(rb, tr, win, cg)
                nxt = jnp.where(step > 0, carry_ref[...], 0.0)
                lead = _leading_sum(jnp.concatenate([q, nxt], axis=0), win)[:tr]
                duz_ref[0] = (lead - dd).astype(duz_ref.dtype)
                carry_ref[...] = q[:MAXW]

                @pl.when(step == 0)
                def _():
                    acc_ref[...] = gw
                    db_ref[...] = dbv
                    dsc_ref[...] = dsc

                @pl.when(step > 0)
                def _():
                    acc_ref[...] += gw
                    db_ref[...] += dbv
                    dsc_ref[...] += dsc

                @pl.when(step == nb - 1)
                def _():
                    gw_ref[...] = acc_ref[...].reshape(4, cg // 4, cg).astype(gw_ref.dtype)

    return pl.pallas_call(
        body, name="pool_bwd", grid=(ng, nb),
        in_specs=[
            pl.BlockSpec((None, tr, cg), lambda g, r: (0, nb - 1 - r, g)),
            pl.BlockSpec((None, MAXW, cg), lambda g, r: (0, jnp.maximum((nb - 1 - r) * hb - 1, 0), g)),
            pl.BlockSpec((None, tr, cg), lambda g, r: (1, nb - 1 - r, g)),
            pl.BlockSpec((tr, cg), lambda g, r: (nb - 1 - r, g)),
            pl.BlockSpec((4, None, cg // 4, cg), lambda g, r: (0, g, 0, 0)),
            pl.BlockSpec((1, cg), lambda g, r: (0, g)),
            pl.BlockSpec((1, cg), lambda g, r: (0, g)),
        ],
        out_specs=[
            pl.BlockSpec((2, tr, cg), lambda g, r: (0, nb - 1 - r, g)),
            pl.BlockSpec((4, None, cg // 4, cg), lambda g, r: (0, g, 0, 0)),
            pl.BlockSpec((1, cg), lambda g, r: (0, g)),
            pl.BlockSpec((1, cg), lambda g, r: (0, g)),
        ],
        out_shape=[
            jax.ShapeDtypeStruct((2, s, e), MXU),
            jax.ShapeDtypeStruct(w_g.shape, WIRE),
            jax.ShapeDtypeStruct((1, e), F32),
            jax.ShapeDtypeStruct((1, e), F32),
        ],
        scratch_shapes=[pltpu.VMEM((cg, cg), F32), pltpu.VMEM((MAXW, cg), F32)],
        compiler_params=_cp(("parallel", "arbitrary")),
    )(uz, uz, uz, dy2, w_g, bias, scale)


HALO = 8


def _one_minus_sq(log_a, a):
    poly = (-2.0 * log_a) * (1.0 + log_a * (1.0 + log_a * (2.0 / 3.0)))
    return jnp.where(log_a > -0.01, poly, 1.0 - a * a)


def _softplus_neg(lam):
    t = jnp.exp(-jnp.abs(lam))
    log1p = jnp.where(t < 1e-3, t * (1.0 - t * (0.5 - t * (1.0 / 3.0))), jnp.log(1.0 + t))
    return jnp.maximum(-lam, 0.0) + log1p, _sig(-lam)


def _lru_gates(u_ref, up_ref, rb, sm_ref, wa, wx):
    cur = u_ref[...]
    prev = jnp.where(rb > 0, up_ref[...], 0.0)
    ext = jnp.concatenate([prev, cur], axis=0)
    taps = [cur] + [pltpu.roll(ext, k, 0)[HALO:] for k in range(1, CONV_W)]
    uc = sm_ref[CONV_W:CONV_W + 1, :]
    for k in range(CONV_W):
        uc = uc + taps[k] * sm_ref[CONV_W - 1 - k:CONV_W - k, :]
    ucb = uc.astype(MXU)
    r = _sig(jnp.dot(ucb, wa, preferred_element_type=F32) + sm_ref[5:6, :])
    ig = _sig(jnp.dot(ucb, wx, preferred_element_type=F32) + sm_ref[6:7, :])
    sp, sgn = _softplus_neg(sm_ref[7:8, :])
    log_a = r * (-LRU_C * sp)
    a = jnp.exp(log_a)
    mult = jnp.sqrt(jnp.maximum(_one_minus_sq(log_a, a), 0.0))
    return taps, uc, ucb, r, ig, sp, sgn, a, mult


LANES = 128


def _seg_scan(a, b, out_ref, scr, state, reverse):
    a_s, b_s, h_s, p_s = scr
    tr, c = a.shape
    seg = tr // 8
    nl = c // LANES
    for l in range(nl):
        a_s[l] = a[:, l * LANES:(l + 1) * LANES]
        b_s[l] = b[:, l * LANES:(l + 1) * LANES]
    h = [jnp.zeros((8, LANES), F32)] * nl
    pp = [jnp.ones((8, LANES), F32)] * nl
    for i in (range(seg - 1, -1, -1) if reverse else range(seg)):
        rows = pl.ds(i, 8, stride=seg)
        for l in range(nl):
            av = a_s[l, rows, :]
            h[l] = av * h[l] + b_s[l, rows, :]
            pp[l] = av * pp[l]
            h_s[l, rows, :] = h[l]
            p_s[l, rows, :] = pp[l]
    leaving = []
    for l in range(nl):
        st = state[:, l * LANES:(l + 1) * LANES]
        for sgm in (range(7, -1, -1) if reverse else range(8)):
            rows = pl.ds(sgm * seg, seg)
            out_ref[rows, l * LANES:(l + 1) * LANES] = h_s[l, rows, :] + p_s[l, rows, :] * st
            st = h[l][sgm:sgm + 1, :] + pp[l][sgm:sgm + 1, :] * st
        leaving.append(st)
    return jnp.concatenate(leaving, axis=1)


def lru_fwd(uz, wa_g, wx_g, small, tr):
    _, s, e = uz.shape
    cb = e // LRU_HEADS
    nb = s // tr
    hb = tr // HALO

    def body(u_ref, up_ref, z_ref, wa_ref, wx_ref, sm_ref, o_ref, h_ref, s0, s1, s2, s3, carry_ref):
        rb = pl.program_id(1)
        wa = wa_ref[...].reshape(cb, cb)
        wx = wx_ref[...].reshape(cb, cb)
        _, uc, _, _, ig, _, _, a, mult = _lru_gates(u_ref, up_ref, rb, sm_ref, wa, wx)
        start = jnp.where(rb > 0, carry_ref[0:1, :], 0.0)
        last = _seg_scan(a, mult * ig * uc, h_ref, (s0, s1, s2, s3), start, False)
        carry_ref[...] = jnp.broadcast_to(last, carry_ref.shape)
        z = z_ref[...]
        o_ref[...] = (h_ref[...] * z * _sig(z)).astype(o_ref.dtype)

    wspec = pl.BlockSpec((4, None, cb // 4, cb), lambda h, r: (0, h, 0, 0))
    return pl.pallas_call(
        body, name="lru_fwd", grid=(LRU_HEADS, nb),
        in_specs=[
            pl.BlockSpec((None, tr, cb), lambda h, r: (0, r, h)),
            pl.BlockSpec((None, HALO, cb), lambda h, r: (0, jnp.maximum(r * hb - 1, 0), h)),
            pl.BlockSpec((None, tr, cb), lambda h, r: (1, r, h)),
            wspec, wspec,
            pl.BlockSpec((8, cb), lambda h, r: (0, h)),
        ],
        out_specs=[pl.BlockSpec((tr, cb), lambda h, r: (r, h)), pl.BlockSpec((tr, cb), lambda h, r: (r, h))],
        out_shape=[jax.ShapeDtypeStruct((s, e), MXU), jax.ShapeDtypeStruct((s, e), F32)],
        scratch_shapes=[pltpu.VMEM((cb // LANES, tr, LANES), F32)] * 4 + [pltpu.VMEM((8, cb), F32)],
        compiler_params=_cp(("parallel", "arbitrary")),
    )(uz, uz, uz, wa_g, wx_g, small)


def lru_bwd(uz, hst, dy2, wa_g, wx_g, small, tr):
    _, s, e = uz.shape
    cb = e // LRU_HEADS
    nb = s // tr
    hb = tr // HALO

    def body(u_ref, up_ref, z_ref, h_ref, hp_ref, dy_ref, wa_ref, wx_ref, sm_ref,
             duz_ref, gwa_ref, gwx_ref, dsm_ref, s0, s1, s2, s3, g_s, acc_a, acc_x, gcar, acar, dcar):
        step = pl.program_id(1)
        rb = nb - 1 - step
        wa = wa_ref[...].reshape(cb, cb)
        wx = wx_ref[...].reshape(cb, cb)
        taps, uc, ucb, r, ig, sp, sgn, a, mult = _lru_gates(u_ref, up_ref, rb, sm_ref, wa, wx)
        row = lax.broadcasted_iota(jnp.int32, a.shape, 0)
        z = z_ref[...]
        sg = _sig(z)
        dy2v = dy_ref[...]
        hv = h_ref[...]
        duz_ref[1] = (dy2v * hv * sg * (1.0 + z * (1.0 - sg))).astype(duz_ref.dtype)
        a_next = jnp.where(row == tr - 1, jnp.where(step > 0, acar[0:1, :], 0.0), pltpu.roll(a, tr - 1, 0))
        g_first = _seg_scan(a_next, dy2v * z * sg, g_s, (s0, s1, s2, s3), jnp.where(step > 0, gcar[0:1, :], 0.0), True)
        gcar[...] = jnp.broadcast_to(g_first, gcar.shape)
        acar[...] = jnp.broadcast_to(a[0:1, :], acar.shape)
        gv = g_s[...]
        h_before = jnp.where(rb > 0, hp_ref[HALO - 1:HALO, :], 0.0)
        h_prev = jnp.where(row == 0, h_before, pltpu.roll(hv, 1, 0))
        da = gv * h_prev
        gu = gv * uc
        dmult = gu * ig
        dig = gu * mult
        dlog_a = da * a - dmult * jnp.where(mult > 0.0, a * a / mult, 0.0)
        dra = dlog_a * (-LRU_C) * sp * r * (1.0 - r)
        dix = dig * ig * (1.0 - ig)
        dl = jnp.sum(dlog_a * r, axis=0, keepdims=True) * (LRU_C * sgn)
        dra_b, dix_b = dra.astype(MXU), dix.astype(MXU)
        duc = (gv * mult * ig + lax.dot_general(dra_b, wa, _DN["nt"], preferred_element_type=F32)
               + lax.dot_general(dix_b, wx, _DN["nt"], preferred_element_type=F32))
        gwa = lax.dot_general(ucb, dra_b, _DN["tn"], preferred_element_type=F32)
        gwx = lax.dot_general(ucb, dix_b, _DN["tn"], preferred_element_type=F32)
        ext = jnp.concatenate([duc, jnp.where(step > 0, dcar[...], 0.0)], axis=0)
        n = ext.shape[0]
        du = duc * sm_ref[CONV_W - 1:CONV_W, :]
        for k in range(1, CONV_W):
            du = du + pltpu.roll(ext, n - k, 0)[:tr] * sm_ref[CONV_W - 1 - k:CONV_W - k, :]
        duz_ref[0] = du.astype(duz_ref.dtype)
        dcar[...] = duc[:HALO]
        rows = [jnp.sum(duc * taps[CONV_W - 1 - k], axis=0, keepdims=True) for k in range(CONV_W)]
        rows += [jnp.sum(duc, axis=0, keepdims=True), jnp.sum(dra, axis=0, keepdims=True),
                 jnp.sum(dix, axis=0, keepdims=True), dl]

        @pl.when(step == 0)
        def _():
            acc_a[...] = gwa
            acc_x[...] = gwx
            for k, rv in enumerate(rows):
                dsm_ref[k:k + 1, :] = rv

        @pl.when(step > 0)
        def _():
            acc_a[...] += gwa
            acc_x[...] += gwx
            for k, rv in enumerate(rows):
                dsm_ref[k:k + 1, :] += rv

        @pl.when(step == nb - 1)
        def _():
            gwa_ref[...] = acc_a[...].reshape(4, cb // 4, cb).astype(gwa_ref.dtype)
            gwx_ref[...] = acc_x[...].reshape(4, cb // 4, cb).astype(gwx_ref.dtype)

    wspec = pl.BlockSpec((4, None, cb // 4, cb), lambda h, r: (0, h, 0, 0))
    blk = pl.BlockSpec((tr, cb), lambda h, r: (nb - 1 - r, h))
    return pl.pallas_call(
        body, name="lru_bwd", grid=(LRU_HEADS, nb),
        in_specs=[
            pl.BlockSpec((None, tr, cb), lambda h, r: (0, nb - 1 - r, h)),
            pl.BlockSpec((None, HALO, cb), lambda h, r: (0, jnp.maximum((nb - 1 - r) * hb - 1, 0), h)),
            pl.BlockSpec((None, tr, cb), lambda h, r: (1, nb - 1 - r, h)),
            blk,
            pl.BlockSpec((HALO, cb), lambda h, r: (jnp.maximum((nb - 1 - r) * hb - 1, 0), h)),
            blk,
            wspec, wspec,
            pl.BlockSpec((8, cb), lambda h, r: (0, h)),
        ],
        out_specs=[
            pl.BlockSpec((2, tr, cb), lambda h, r: (0, nb - 1 - r, h)),
            wspec, wspec,
            pl.BlockSpec((8, cb), lambda h, r: (0, h)),
        ],
        out_shape=[
            jax.ShapeDtypeStruct((2, s, e), MXU),
            jax.ShapeDtypeStruct(wa_g.shape, WIRE),
            jax.ShapeDtypeStruct(wx_g.shape, WIRE),
            jax.ShapeDtypeStruct((8, e), F32),
        ],
        scratch_shapes=([pltpu.VMEM((cb // LANES, tr, LANES), F32)] * 4 + [pltpu.VMEM((tr, cb), F32)]
                        + [pltpu.VMEM((cb, cb), F32)] * 2 + [pltpu.VMEM((8, cb), F32)] * 3),
        compiler_params=_cp(("parallel", "arbitrary")),
    )(uz, uz, uz, hst, hst, dy2, wa_g, wx_g, small)


def _place():
    x, y, c = lax.axis_index("x"), lax.axis_index("y"), lax.axis_index("c")
    chips = [(1 - x, y), (x, 1 - y), (1 - x, 1 - y)]
    return x, y, c, chips


def _rcopy(src, dst, send_sems, recv_sems, k, to):
    return pltpu.make_async_remote_copy(src_ref=src, dst_ref=dst, send_sem=send_sems.at[k], recv_sem=recv_sems.at[k],
                                        device_id=to, device_id_type=MESHID)


def _stage_gather_ici(bufs):
    n = len(bufs)

    def copies(refs, ss, rs, off, own):
        x, y, c, chips = _place()
        out = []
        for a in range(n):
            for jj, ch in enumerate(chips):
                blk = refs[a].at[2 * x + y if own else 2 * ch[0] + ch[1], :, c]
                out.append(_rcopy(blk, blk, ss, rs, off + 3 * a + jj, (*ch, c)))
        return out

    def start(ro, refs, ss, rs, off):
        for cp in copies(refs, ss, rs, off, True):
            cp.start()

    def finish(ro, refs, ss, rs, off):
        for cp in copies(refs, ss, rs, off, False):
            cp.wait_recv()
        for cp in copies(refs, ss, rs, off, True):
            cp.wait_send()

    return dict(ro=[], bufs=list(bufs), nsem=3 * n, start=start, finish=finish)


def _stage_gather_d2d(bufs):
    n = len(bufs)

    def copies(refs, ss, rs, off, sending):
        x, y, c, chips = _place()
        out = []
        for a in range(n):
            for jj, ch in enumerate(chips):
                blk = refs[a].at[2 * ch[0] + ch[1], :, c if sending else 1 - c]
                out.append(_rcopy(blk, blk, ss, rs, off + 3 * a + jj, (x, y, 1 - c)))
        return out

    def start(ro, refs, ss, rs, off):
        for cp in copies(refs, ss, rs, off, True):
            cp.start()

    def finish(ro, refs, ss, rs, off):
        for cp in copies(refs, ss, rs, off, False):
            cp.wait_recv()
        for cp in copies(refs, ss, rs, off, True):
            cp.wait_send()

    return dict(ro=[], bufs=list(bufs), nsem=3 * n, start=start, finish=finish)


def _stage_scatter_ici(parts, gots):
    n = len(parts)

    def copies(ro, refs, ss, rs, off):
        x, y, c, chips = _place()
        return [_rcopy(ro[a].at[2 * ch[0] + ch[1]], refs[a].at[jj], ss, rs, off + 3 * a + jj, (*ch, c))
                for a in range(n) for jj, ch in enumerate(chips)]

    def start(ro, refs, ss, rs, off):
        for cp in copies(ro, refs, ss, rs, off):
            cp.start()

    def finish(ro, refs, ss, rs, off):
        for cp in copies(ro, refs, ss, rs, off):
            cp.wait()

    return dict(ro=list(parts), bufs=list(gots), nsem=3 * n, start=start, finish=finish)


def cast_into_slab(w, k_idx):
    l, r, c = w.shape
    rh = r // 2
    tr = _block_rows(rh, c, 4)
    nbh = rh // tr

    def body(k_ref, w_ref, o_ref):
        o_ref[...] = w_ref[...].astype(o_ref.dtype)

    return pl.pallas_call(
        body, name="cast_into_slab",
        grid_spec=pltpu.PrefetchScalarGridSpec(
            num_scalar_prefetch=1, grid=(l, 2, nbh),
            in_specs=[pl.BlockSpec((None, tr, c), lambda i, h, b, k_ref: (i, h * nbh + b, 0))],
            out_specs=pl.BlockSpec((None, None, None, tr, c), lambda i, h, b, k_ref: (k_ref[0], i, h, b, 0))),
        out_shape=jax.ShapeDtypeStruct((4, l, 2, rh, c), WIRE),
        compiler_params=_cp(("parallel", "parallel", "parallel")),
    )(k_idx, w)


def gather_weights(bufs):
    n = len(bufs)

    def body(*refs):
        outs = refs[n:2 * n]
        send_sems, recv_sems = refs[2 * n:]
        x, y, c, chips = _place()
        me = 2 * x + y
        sib = (x, y, 1 - c)
        sends = []
        for a in range(n):
            for jj, ch in enumerate(chips):
                mine = outs[a].at[me, :, c]
                cp = _rcopy(mine, mine, send_sems, recv_sems, 6 * a + jj, (*ch, c))
                cp.start()
                sends.append(cp)
        for a in range(n):
            for jj, ch in enumerate(chips):
                blk = outs[a].at[2 * ch[0] + ch[1], :, c]
                _rcopy(blk, blk, send_sems, recv_sems, 6 * a + jj, (*ch, c)).wait_recv()
                fw = _rcopy(blk, blk, send_sems, recv_sems, 6 * a + 3 + jj, sib)
                fw.start()
                sends.append(fw)
        for a in range(n):
            for jj, ch in enumerate(chips):
                blk = outs[a].at[2 * ch[0] + ch[1], :, 1 - c]
                _rcopy(blk, blk, send_sems, recv_sems, 6 * a + 3 + jj, sib).wait_recv()
        for cp in sends:
            cp.wait_send()

    return pl.pallas_call(
        body, name="gather_weights",
        in_specs=[ANY] * n, out_specs=[ANY] * n,
        out_shape=[jax.ShapeDtypeStruct(a.shape, a.dtype) for a in bufs],
        scratch_shapes=[pltpu.SemaphoreType.DMA((6 * n,)), pltpu.SemaphoreType.DMA((6 * n,))],
        input_output_aliases={a: a for a in range(n)},
        compiler_params=pltpu.CompilerParams(has_side_effects=True),
    )(*bufs)


def all_gather_small(v, name):
    m_per, n = v.shape

    def body(x_ref, out_ref, send_sems, recv_sems, local_sem):
        x, y, c, chips = _place()
        me, sibling = (x, y, c), (x, y, 1 - c)

        def rows(px, py, pc):
            return out_ref.at[pl.ds((4 * px + 2 * py + pc) * m_per, m_per), :]

        def copy(k, block, to, src=None):
            return _rcopy(rows(*block) if src is None else src, rows(*block), send_sems, recv_sems, k, to)

        mine = pltpu.make_async_copy(x_ref, rows(*me), local_sem)
        mine.start()
        first = [copy(0, me, sibling, src=x_ref)]
        first += [copy(1 + jj, me, (*chip, c), src=x_ref) for jj, chip in enumerate(chips)]
        for cp in first:
            cp.start()
        passed = [copy(4 + jj, (*chip, c), sibling) for jj, chip in enumerate(chips)]
        for jj, chip in enumerate(chips):
            copy(1 + jj, (*chip, c), me).wait_recv()
            passed[jj].start()
        copy(0, sibling, me).wait_recv()
        for jj, chip in enumerate(chips):
            copy(4 + jj, (*chip, 1 - c), me).wait_recv()
        for cp in first + passed:
            cp.wait_send()
        mine.wait()

    return pl.pallas_call(
        body, name=name,
        out_shape=jax.ShapeDtypeStruct((8 * m_per, n), v.dtype),
        in_specs=[pl.BlockSpec(memory_space=pltpu.VMEM)],
        out_specs=pl.BlockSpec(memory_space=pltpu.VMEM),
        scratch_shapes=[pltpu.SemaphoreType.DMA((7,)), pltpu.SemaphoreType.DMA((7,)), pltpu.SemaphoreType.DMA],
        compiler_params=pltpu.CompilerParams(vmem_limit_bytes=VMEM_LIMIT),
    )(v)


def sum_devices(g):
    def body(g_ref, o_ref):
        acc = g_ref[0]
        for d in range(1, 8):
            acc = acc + g_ref[d]
        o_ref[...] = acc

    return pl.pallas_call(body, name="sum_devices", out_shape=jax.ShapeDtypeStruct(g.shape[1:], g.dtype),
                          compiler_params=pltpu.CompilerParams(vmem_limit_bytes=VMEM_LIMIT))(g)


def send_other_half(grads):
    n = len(grads)

    def body(*refs):
        ins, outs = refs[:n], refs[n:2 * n]
        send_sems, recv_sems = refs[2 * n:]
        x, y, c, _ = _place()
        sib = (x, y, 1 - c)
        cps = [_rcopy(ins[a].at[:, :, 1 - c], outs[a], send_sems, recv_sems, a, sib) for a in range(n)]
        for cp in cps:
            cp.start()
        for cp in cps:
            cp.wait()

    return pl.pallas_call(
        body, name="send_other_half", in_specs=[ANY] * n, out_specs=[ANY] * n,
        out_shape=[jax.ShapeDtypeStruct(g.shape[:2] + g.shape[3:], g.dtype) for g in grads],
        scratch_shapes=[pltpu.SemaphoreType.DMA((n,)), pltpu.SemaphoreType.DMA((n,))],
        compiler_params=pltpu.CompilerParams(has_side_effects=True),
    )(*grads)


def scatter_to_chips(parts):
    n = len(parts)

    def body(*refs):
        ins, outs = refs[:n], refs[n:2 * n]
        send_sems, recv_sems = refs[2 * n:]
        x, y, c, chips = _place()
        cps = []
        for a in range(n):
            for jj, ch in enumerate(chips):
                cps.append(_rcopy(ins[a].at[2 * ch[0] + ch[1]], outs[a].at[jj], send_sems, recv_sems, 3 * a + jj, (*ch, c)))
        for cp in cps:
            cp.start()
        for cp in cps:
            cp.wait()

    return pl.pallas_call(
        body, name="scatter_to_chips", in_specs=[ANY] * n, out_specs=[ANY] * n,
        out_shape=[jax.ShapeDtypeStruct((3,) + p.shape[1:], p.dtype) for p in parts],
        scratch_shapes=[pltpu.SemaphoreType.DMA((3 * n,)), pltpu.SemaphoreType.DMA((3 * n,))],
        compiler_params=pltpu.CompilerParams(has_side_effects=True),
    )(*parts)


def share_halves(bufs, spans):
    n = len(bufs)

    def body(*refs):
        outs = refs[n:2 * n]
        send_sems, recv_sems = refs[2 * n:]
        x, y, c, _ = _place()
        sib = (x, y, 1 - c)

        def blk(a, half):
            return outs[a].at[pl.ds(spans[a][0], spans[a][1]), half]

        cps = [_rcopy(blk(a, c), blk(a, c), send_sems, recv_sems, a, sib) for a in range(n)]
        for cp in cps:
            cp.start()
        for a in range(n):
            _rcopy(blk(a, 1 - c), blk(a, 1 - c), send_sems, recv_sems, a, sib).wait_recv()
        for cp in cps:
            cp.wait_send()

    return pl.pallas_call(
        body, name="share_halves", in_specs=[ANY] * n, out_specs=[ANY] * n,
        out_shape=[jax.ShapeDtypeStruct(b.shape, b.dtype) for b in bufs],
        scratch_shapes=[pltpu.SemaphoreType.DMA((n,)), pltpu.SemaphoreType.DMA((n,))],
        input_output_aliases={a: a for a in range(n)},
        compiler_params=pltpu.CompilerParams(has_side_effects=True),
    )(*bufs)


def _block_rows(r, c, itemsize, budget=1 << 20):
    tr = r
    while tr * c * itemsize > budget and tr % 16 == 0:
        tr //= 2
    return tr


def add_halves(g, got, c_idx):
    k4, l, _, rh, cc = g.shape
    tr = _block_rows(rh, cc, 4)

    def body(c_ref, g_ref, r_ref, o_ref):
        o_ref[...] = (g_ref[...].astype(F32) + r_ref[...].astype(F32)).astype(o_ref.dtype)

    return pl.pallas_call(
        body, name="add_halves",
        grid_spec=pltpu.PrefetchScalarGridSpec(
            num_scalar_prefetch=1, grid=(k4, l, rh // tr),
            in_specs=[pl.BlockSpec((None, None, None, tr, cc), lambda k, i, b, c_ref: (k, i, c_ref[0], b, 0)),
                      pl.BlockSpec((None, None, tr, cc), lambda k, i, b, c_ref: (k, i, b, 0))],
            out_specs=pl.BlockSpec((None, None, tr, cc), lambda k, i, b, c_ref: (k, i, b, 0))),
        out_shape=jax.ShapeDtypeStruct(got.shape, WIRE),
        compiler_params=_cp(("parallel", "parallel", "parallel")),
    )(c_idx, g, got)


def sum_chips(own, got, kc_idx, full, first):
    _, l, rh, cc = own.shape
    tr = _block_rows(rh, cc, 4)

    def body(k_ref, o_ref, r_ref, _full, s_ref):
        s_ref[...] = ((o_ref[...].astype(F32) + r_ref[0].astype(F32)) + r_ref[1].astype(F32)) + r_ref[2].astype(F32)

    return pl.pallas_call(
        body, name="sum_chips",
        grid_spec=pltpu.PrefetchScalarGridSpec(
            num_scalar_prefetch=1, grid=(l, rh // tr),
            in_specs=[pl.BlockSpec((None, None, tr, cc), lambda i, b, k_ref: (k_ref[0], i, b, 0)),
                      pl.BlockSpec((3, None, tr, cc), lambda i, b, k_ref: (0, i, b, 0)),
                      ANY],
            out_specs=pl.BlockSpec((None, None, tr, cc), lambda i, b, k_ref: (first + i, k_ref[1], b, 0))),
        out_shape=jax.ShapeDtypeStruct(full.shape, F32),
        input_output_aliases={3: 0},
        compiler_params=_cp(("parallel", "parallel")),
    )(kc_idx, own, got, full)


def _adam_math(w, g, m, v):
    m = ADAM_B1 * m + (1.0 - ADAM_B1) * g
    v = ADAM_B2 * v + (1.0 - ADAM_B2) * (g * g)
    m_hat = m / (1.0 - ADAM_B1 ** ADAM_STEP)
    v_hat = v / (1.0 - ADAM_B2 ** ADAM_STEP)
    delta = -ADAM_LR * (m_hat / (jnp.sqrt(v_hat) + ADAM_EPS) + ADAM_WD * w)
    return delta, m, v


def adamw(w, g, m, v):
    l, r, c = w.shape
    tr = _block_rows(r, c, 4)

    def body(w_ref, g_ref, m_ref, v_ref, go_ref, d_ref, mo_ref, vo_ref):
        gv = g_ref[...]
        go_ref[...] = gv
        d_ref[...], mo_ref[...], vo_ref[...] = _adam_math(w_ref[...], gv, m_ref[...], v_ref[...])

    spec = pl.BlockSpec((None, tr, c), lambda i, b: (i, b, 0))
    return pl.pallas_call(
        body, name="adamw", grid=(l, r // tr), in_specs=[spec] * 4, out_specs=[spec] * 4,
        out_shape=[jax.ShapeDtypeStruct(w.shape, F32)] * 4, compiler_params=_cp(("parallel", "parallel")),
    )(w, g, m, v)


def adamw_small(w, g, m, v):
    def body(w_ref, g_ref, m_ref, v_ref, d_ref, mo_ref, vo_ref):
        d_ref[...], mo_ref[...], vo_ref[...] = _adam_math(w_ref[...], g_ref[...], m_ref[...], v_ref[...])

    return pl.pallas_call(body, name="adamw_small", out_shape=[jax.ShapeDtypeStruct(w.shape, F32)] * 3)(w, g, m, v)


def kernel(x, p, w_in, w_out, g_pre, g_post, pool_w, pool_b, pool_scale, conv_w, conv_b, lru_wa, lru_ba, lru_wx, lru_bx, lru_L, w_ple, w_ple_gate, g_ple_in, g_ple_out, loss_target, m_w_in, m_w_out, m_g_pre, m_g_post, m_pool_w, m_pool_b, m_pool_scale, m_conv_w, m_conv_b, m_lru_wa, m_lru_ba, m_lru_wx, m_lru_bx, m_lru_L, m_w_ple, m_w_ple_gate, m_g_ple_in, m_g_ple_out, v_w_in, v_w_out, v_g_pre, v_g_post, v_pool_w, v_pool_b, v_pool_scale, v_conv_w, v_conv_b, v_lru_wa, v_lru_ba, v_lru_wx, v_lru_bx, v_lru_L, v_w_ple, v_w_ple_gate, v_g_ple_in, v_g_ple_out):
    depth = w_in.shape[0]
    _, s, d = x.shape
    e = 2 * d
    kp = p.shape[-1]
    nmix = pool_w.shape[0]
    ngrp = pool_w.shape[1]
    cg = e // ngrp
    cb = e // LRU_HEADS
    xi, yi, ci = lax.axis_index("x"), lax.axis_index("y"), lax.axis_index("c")
    me = 2 * xi + yi
    c_idx = jnp.reshape(ci, (1,)).astype(jnp.int32)
    k_idx = jnp.reshape(me, (1,)).astype(jnp.int32)
    tr_row = _tile(s, 256)
    tr_mix = _tile(s, 512)
    tm = _tile(s, 1024)

    def halves(a):
        return a.reshape(a.shape[0], 2, a.shape[1] // 2, a.shape[2])

    big = {
        "w_in": w_in, "w_out": w_out, "gate": w_ple_gate, "ple": w_ple,
        "pool": pool_w.reshape(nmix * ngrp, cg // 4, cg),
        "wa": lru_wa.reshape(nmix * LRU_HEADS, cb // 4, cb), "wx": lru_wx.reshape(nmix * LRU_HEADS, cb // 4, cb),
    }
    names = list(big)

    def layer_shards(i):
        sh = {"w_in": w_in[i][None], "w_out": w_out[i][None], "gate": w_ple_gate[i][None], "ple": w_ple[i][None]}
        if i % 2 == 0:
            sh["pool"] = pool_w[i // 2]
        else:
            sh["wa"], sh["wx"] = lru_wa[i // 2], lru_wx[i // 2]
        return sh

    def mixer_names(i):
        return ["pool"] if i % 2 == 0 else ["wa", "wx"]

    wbuf = [{n: cast_into_slab(w, k_idx) for n, w in layer_shards(i).items()} for i in range(depth)]
    first = list(wbuf[0])
    wbuf[0] = dict(zip(first, gather_weights([wbuf[0][n] for n in first])))

    def full_w(i, n):
        b = wbuf[i][n]
        return b.reshape(b.shape[0], b.shape[1], 2 * b.shape[3], b.shape[4])

    def run_mm(stages, *args):
        if not stages:
            return _matmul(*args)
        out, new = _matmul(*args, hosted=[mk([wbuf[l][n] for n in nms]) for mk, l, nms in stages])
        slots = [(l, n) for _, l, nms in stages for n in nms]
        for (l, n), b in zip(slots, new):
            wbuf[l][n] = b
        return out

    ec = e // 4
    small_loc = jnp.concatenate([conv_w, conv_b[:, None], lru_ba[:, None], lru_bx[:, None], lru_L[:, None]], axis=1)
    sm_all = all_gather_small(small_loc.reshape(nmix * 8, ec), "gather_small").reshape(4, 2, nmix, 8, ec)
    lru_small = jnp.transpose(sm_all[:, 0], (1, 2, 0, 3)).reshape(nmix, 8, e)

    xs = x[0]
    saved = []
    for i in range(depth):
        j = i // 2
        h = rms_fwd(xs, g_pre[i][None], tr_row)
        nj = (2 * e) // 1024 if (2 * e) % 1024 == 0 else 4
        tn = (2 * e) // nj
        per = e // tn
        perk = (e // 2) // tn
        nxt = i + 1 if i + 1 < depth else None
        stages = [(_stage_gather_d2d, i, ["gate", "ple"])] if i > 0 else []
        if nxt is not None:
            stages.append((_stage_gather_ici, nxt, ["w_in"]))
        uz = run_mm(
            stages, "mm_in", "nn", (s // tm, nj, 1), h, pl.BlockSpec((tm, d), lambda a, b, k: (a, 0)),
            full_w(i, "w_in"), pl.BlockSpec((None, None, d, tn), lambda a, b, k, perk=perk: (b // perk, 0, 0, b % perk)),
            jax.ShapeDtypeStruct((2, s, e), F32), pl.BlockSpec((None, tm, tn), lambda a, b, k, per=per: (b // per, a, b % per)),
            (8, 128))
        if i % 2 == 0:
            y2 = pool_fwd(uz, full_w(i, "pool"), pool_b[j][None], pool_scale[j][None], tr_mix)
            hst = None
        else:
            y2, hst = lru_fwd(uz, full_w(i, "wa"), full_w(i, "wx"), lru_small[j], tr_mix)
        tn_o = _tile(d, 1024)
        stages = [(_stage_gather_ici, nxt, ["w_out"] + mixer_names(nxt))] if nxt is not None else []
        o = run_mm(
            stages, "mm_out", "nn", (s // tm, d // tn_o, 4), y2, pl.BlockSpec((tm, e // 4), lambda a, b, k: (a, k)),
            full_w(i, "w_out"), pl.BlockSpec((None, None, e // 4, tn_o), lambda a, b, k: (k, 0, 0, b)),
            jax.ShapeDtypeStruct((s, d), F32), pl.BlockSpec((tm, tn_o), lambda a, b, k: (a, b)), (tm, tn_o))
        x1 = res_rms_fwd(xs, o, g_post[i][None], tr_row)
        hn = rms_fwd(x1, g_ple_in[i][None], tr_row)
        stages = []
        if nxt is not None:
            stages = [(_stage_gather_ici, nxt, ["gate", "ple"]), (_stage_gather_d2d, nxt, ["w_in", "w_out"] + mixer_names(nxt))]
        gpre = run_mm(
            stages, "mm_gate", "nn", (s // tm, d // tn_o, 4), hn, pl.BlockSpec((tm, d // 4), lambda a, b, k: (a, k)),
            full_w(i, "gate"), pl.BlockSpec((None, None, d // 4, tn_o), lambda a, b, k: (k, 0, 0, b)),
            jax.ShapeDtypeStruct((s, d), F32), pl.BlockSpec((tm, tn_o), lambda a, b, k: (a, b)), (tm, tn_o))
        pe = p[i, 0]
        ev = _matmul(
            "mm_ple", "nn", (s // tm, 4, 1), pe, pl.BlockSpec((tm, kp), lambda a, b, k: (a, 0)),
            full_w(i, "ple"), pl.BlockSpec((None, None, kp, d // 4), lambda a, b, k: (b, 0, 0, 0)),
            jax.ShapeDtypeStruct((s, d), F32), pl.BlockSpec((tm, d // 4), lambda a, b, k: (a, b)), (8, 128))
        x2 = ple_fwd(x1, gpre, ev, g_ple_out[i][None], tr_row)
        saved.append((xs, h, uz, y2, hst, o, x1, hn, gpre, ev))
        xs = x2

    dx, sq = loss_bwd(xs, loss_target[0], tr_row)
    d_gpre, d_gpost, d_gin, d_gout = [None] * depth, [None] * depth, [None] * depth, [None] * depth
    d_pool_b, d_pool_sc, d_lru_small = [None] * nmix, [None] * nmix, [None] * nmix
    ts = _tile(s, 1024)
    kc_idx = jnp.stack([me, ci]).astype(jnp.int32)
    full = {n: lax.empty(halves(big[n]).shape, F32) for n in names}
    sums, gots = [None] * depth, [None] * depth

    def scatter_stage(l, nms):
        return _stage_scatter_ici([sums[l][n] for n in nms], [gots[l][n] for n in nms]), [(l, n) for n in nms]

    def run_mm_scatter(specs, *args):
        if not specs:
            return _matmul(*args)
        out, new = _matmul(*args, hosted=[st for st, _ in specs])
        for (l, n), b in zip([slot for _, sl in specs for slot in sl], new):
            gots[l][n] = b
        return out

    def first_row(i, n):
        return i if n in ("w_in", "w_out", "gate", "ple") else (i // 2) * (ngrp if n == "pool" else LRU_HEADS)

    for i in reversed(range(depth)):
        j = i // 2
        prev = i + 1 if i + 1 < depth else None
        x0, h, uz, y2, hst, o, x1, hn, gpre, ev = saved[i]
        pe = p[i, 0]
        gl = {}
        de, dgp, d_gout[i] = ple_bwd(dx, gpre, ev, g_ple_out[i][None], tr_row)
        gl["ple"] = _matmul(
            "mm_dple", "tn", (1, 4, s // ts), pe, pl.BlockSpec((ts, kp), lambda a, b, k: (k, 0)),
            de, pl.BlockSpec((ts, d // 4), lambda a, b, k: (k, b)),
            jax.ShapeDtypeStruct((4, 1, kp, d // 4), WIRE), pl.BlockSpec((None, None, kp, d // 4), lambda a, b, k: (b, 0, 0, 0)),
            (kp, d // 4))
        gl["gate"] = _matmul(
            "mm_dgate", "tn", (4, 1, s // ts), hn, pl.BlockSpec((ts, d // 4), lambda a, b, k: (k, a)),
            dgp, pl.BlockSpec((ts, d), lambda a, b, k: (k, 0)),
            jax.ShapeDtypeStruct((4, 1, d // 4, d), WIRE), pl.BlockSpec((None, None, d // 4, d), lambda a, b, k: (a, 0, 0, 0)),
            (d // 4, d))
        dhn = _matmul(
            "mm_dhn", "nt", (s // tm, 4, 1), dgp, pl.BlockSpec((tm, d), lambda a, b, k: (a, 0)),
            full_w(i, "gate"), pl.BlockSpec((None, None, d // 4, d), lambda a, b, k: (b, 0, 0, 0)),
            jax.ShapeDtypeStruct((s, d), F32), pl.BlockSpec((tm, d // 4), lambda a, b, k: (a, b)), (8, 128))
        dx1, d_gin[i] = rms_bwd_res(dx, dhn, x1, g_ple_in[i][None], tr_row)
        do, d_gpost[i] = rms_bwd(dx1, o, g_post[i][None], tr_row)
        tn_o = _tile(d, 1024)
        gl["w_out"] = _matmul(
            "mm_dwout", "tn", (4, d // tn_o, s // ts), y2, pl.BlockSpec((ts, e // 4), lambda a, b, k: (k, a)),
            do, pl.BlockSpec((ts, tn_o), lambda a, b, k: (k, b)),
            jax.ShapeDtypeStruct((4, 1, e // 4, d), WIRE), pl.BlockSpec((None, None, e // 4, tn_o), lambda a, b, k: (a, 0, 0, b)),
            (e // 4, tn_o))
        dy2 = _matmul(
            "mm_dy2", "nt", (s // tm, 4, 1), do, pl.BlockSpec((tm, d), lambda a, b, k: (a, 0)),
            full_w(i, "w_out"), pl.BlockSpec((None, None, e // 4, d), lambda a, b, k: (b, 0, 0, 0)),
            jax.ShapeDtypeStruct((s, e), F32), pl.BlockSpec((tm, e // 4), lambda a, b, k: (a, b)), (8, 128))
        if i % 2 == 0:
            duz, gl["pool"], d_pool_b[j], d_pool_sc[j] = pool_bwd(uz, dy2, full_w(i, "pool"), pool_b[j][None], pool_scale[j][None], tr_mix)
        else:
            duz, gl["wa"], gl["wx"], d_lru_small[j] = lru_bwd(uz, hst, dy2, full_w(i, "wa"), full_w(i, "wx"), lru_small[j], tr_mix)
        tmi = _tile(d, 1024)
        tni = _tile(e // 2, 1024)
        nslab = (e // 2) // tni
        gl["w_in"] = run_mm_scatter(
            [scatter_stage(prev, ["w_in"])] if prev is not None else [],
            "mm_dwin", "tn", (d // tmi, 4 * nslab, s // ts), h, pl.BlockSpec((ts, tmi), lambda a, b, k: (k, a)),
            duz, pl.BlockSpec((None, ts, tni), lambda a, b, k, nslab=nslab: (b // (2 * nslab), k, b % (2 * nslab))),
            jax.ShapeDtypeStruct((4, 1, d, e // 2), WIRE),
            pl.BlockSpec((None, None, tmi, tni), lambda a, b, k, nslab=nslab: (b // nslab, 0, a, b % nslab)),
            (tmi, tni))
        tnd = _tile(d, 1024)
        dh = run_mm_scatter(
            [scatter_stage(prev, ["w_out", "gate", "ple"] + mixer_names(prev))] if prev is not None else [],
            "mm_dh", "nt", (s // tm, d // tnd, 4), duz, pl.BlockSpec((None, tm, e // 2), lambda a, b, k: (k // 2, a, k % 2)),
            full_w(i, "w_in"), pl.BlockSpec((None, None, tnd, e // 2), lambda a, b, k: (k, 0, b, 0)),
            jax.ShapeDtypeStruct((s, d), F32), pl.BlockSpec((tm, tnd), lambda a, b, k: (a, b)), (tm, tnd))
        dx, d_gpre[i] = rms_bwd_res(dx1, dh, x0, g_pre[i][None], tr_row)
        if prev is not None:
            for n in gots[prev]:
                full[n] = sum_chips(sums[prev][n], gots[prev][n], kc_idx, full[n], first_row(prev, n))
        lnames = list(gl)
        gparts = [gl[n].reshape(4, gl[n].shape[1], 2, gl[n].shape[2] // 2, gl[n].shape[3]) for n in lnames]
        from_sib = send_other_half(gparts)
        sums[i] = {n: add_halves(g, r, c_idx) for n, g, r in zip(lnames, gparts, from_sib)}
        gots[i] = {n: lax.empty((3,) + sums[i][n].shape[1:], WIRE) for n in lnames}
    grad_x = dx[None]

    lnames = list(gots[0])
    for n, b in zip(lnames, scatter_to_chips([sums[0][n] for n in lnames])):
        full[n] = sum_chips(sums[0][n], b, kc_idx, full[n], first_row(0, n))
    shared = share_halves([full[n] for n in names], [(0, full[n].shape[0]) for n in names])
    grads = {n: f.reshape(big[n].shape) for n, f in zip(names, shared)}

    def rows_e(a):
        return jnp.stack(a).reshape(-1, e) if isinstance(a, list) else a.reshape(-1, e)

    pack = [rows_e([g[0] for g in d_gpre]), rows_e([g[0] for g in d_gpost]), rows_e([g[0] for g in d_gin]), rows_e([g[0] for g in d_gout]),
            jnp.concatenate(d_pool_b, axis=0), jnp.concatenate(d_pool_sc, axis=0), jnp.concatenate(d_lru_small, axis=0),
            jnp.pad(sq, ((0, 0), (0, e - d)))]
    sizes = [a.shape[0] for a in pack]
    packed = jnp.concatenate(pack, axis=0)
    nrow = packed.shape[0]
    nrow_p = -(-nrow // 8) * 8
    packed = jnp.pad(packed, ((0, nrow_p - nrow), (0, 0)))
    total = sum_devices(all_gather_small(packed, "gather_grads").reshape(8, nrow_p, e))
    parts, off = [], 0
    for n_ in sizes:
        parts.append(total[off:off + n_])
        off += n_
    t_gpre, t_gpost, t_gin, t_gout, t_pb, t_psc, t_lru, t_sq = parts
    loss = 0.5 * jnp.sum(t_sq) / d
    t_lru = t_lru.reshape(nmix, 8, e)
    t_lru_loc = lax.dynamic_slice_in_dim(t_lru, me * ec, ec, axis=2)

    def big_update(name, w, m, v):
        shp = big[name].shape
        g, dl, nm, nv = adamw(w.reshape(shp), grads[name], m.reshape(shp), v.reshape(shp))
        return [a.reshape(w.shape) for a in (g, dl, nm, nv)]

    def small_update(w, g, m, v):
        shp = w.shape
        w2 = w.reshape(-1, shp[-1])
        dl, nm, nv = adamw_small(w2, g.reshape(w2.shape), m.reshape(w2.shape), v.reshape(w2.shape))
        return [g.reshape(shp), dl.reshape(shp), nm.reshape(shp), nv.reshape(shp)]

    res = {
        "w_in": big_update("w_in", w_in, m_w_in, v_w_in),
        "w_out": big_update("w_out", w_out, m_w_out, v_w_out),
        "g_pre": small_update(g_pre, t_gpre.reshape(depth, d), m_g_pre, v_g_pre),
        "g_post": small_update(g_post, t_gpost.reshape(depth, d), m_g_post, v_g_post),
        "pool_w": big_update("pool", pool_w, m_pool_w, v_pool_w),
        "pool_b": small_update(pool_b, t_pb, m_pool_b, v_pool_b),
        "pool_scale": small_update(pool_scale, t_psc, m_pool_scale, v_pool_scale),
        "conv_w": small_update(conv_w, t_lru_loc[:, :CONV_W], m_conv_w, v_conv_w),
        "conv_b": small_update(conv_b, t_lru_loc[:, 4], m_conv_b, v_conv_b),
        "lru_wa": big_update("wa", lru_wa, m_lru_wa, v_lru_wa),
        "lru_ba": small_update(lru_ba, t_lru_loc[:, 5], m_lru_ba, v_lru_ba),
        "lru_wx": big_update("wx", lru_wx, m_lru_wx, v_lru_wx),
        "lru_bx": small_update(lru_bx, t_lru_loc[:, 6], m_lru_bx, v_lru_bx),
        "lru_L": small_update(lru_L, t_lru_loc[:, 7], m_lru_L, v_lru_L),
        "w_ple": big_update("ple", w_ple, m_w_ple, v_w_ple),
        "w_ple_gate": big_update("gate", w_ple_gate, m_w_ple_gate, v_w_ple_gate),
        "g_ple_in": small_update(g_ple_in, t_gin.reshape(depth, d), m_g_ple_in, v_g_ple_in),
        "g_ple_out": small_update(g_ple_out, t_gout.reshape(depth, d), m_g_ple_out, v_g_ple_out),
    }
    order = ["w_in", "w_out", "g_pre", "g_post", "pool_w", "pool_b", "pool_scale", "conv_w", "conv_b", "lru_wa", "lru_ba",
             "lru_wx", "lru_bx", "lru_L", "w_ple", "w_ple_gate", "g_ple_in", "g_ple_out"]
    out = [loss, grad_x]
    for slot in range(4):
        out += [res[n][slot] for n in order]
    return tuple(out)
```

```python
import functools

import jax
import jax.numpy as jnp
from jax import lax
from jax.experimental import pallas as pl
from jax.experimental.pallas import tpu as pltpu

F32 = jnp.float32
MXU = jnp.bfloat16
WIRE = jnp.bfloat16
VMEM_LIMIT = 56 * 1024 * 1024
RMS_EPS = 1e-6
LRU_C = 8.0
POOL_WINDOWS = (2, 4, 8, 16)
MAXW = 16
CONV_W = 4
LRU_HEADS = 16
ADAM_LR, ADAM_B1, ADAM_B2, ADAM_EPS, ADAM_WD, ADAM_STEP = 0.001, 0.9, 0.999, 1e-08, 0.01, 10
MESHID = pl.DeviceIdType.MESH
ANY = pl.BlockSpec(memory_space=pl.ANY)


def _cp(sem=None):
    return pltpu.CompilerParams(dimension_semantics=sem, vmem_limit_bytes=VMEM_LIMIT)


def _sig(v):
    return 0.5 * jnp.tanh(0.5 * v) + 0.5


def _tile(n, pref):
    return pref if n % pref == 0 else n


_DN = {"nn": (((1,), (0,)), ((), ())), "nt": (((1,), (1,)), ((), ())), "tn": (((0,), (0,)), ((), ()))}


def _matmul(name, mode, grid, a, a_spec, b, b_spec, out_shape, out_spec, acc_shape, hosted=()):
    nk = grid[2]
    ro = [r for st in hosted for r in st["ro"]]
    bufs = [r for st in hosted for r in st["bufs"]]
    nro, nbuf = len(ro), len(bufs)

    def body(a_ref, b_ref, *rest):
        ro_refs = rest[:nro]
        o_ref = rest[nro + nbuf]
        buf_refs = rest[nro + nbuf + 1:nro + 2 * nbuf + 1]
        acc_ref = rest[nro + 2 * nbuf + 1]
        ids = [pl.program_id(t) for t in range(3)]
        kk = ids[2]

        def run_stages(what):
            r0 = b0 = s0 = 0
            for st in hosted:
                st[what](ro_refs[r0:r0 + len(st["ro"])], buf_refs[b0:b0 + len(st["bufs"])], rest[-2], rest[-1], s0)
                r0, b0, s0 = r0 + len(st["ro"]), b0 + len(st["bufs"]), s0 + st["nsem"]

        if hosted:
            @pl.when((ids[0] == 0) & (ids[1] == 0) & (ids[2] == 0))
            def _():
                run_stages("start")

        bv = b_ref[...]
        if bv.ndim == 3:
            bv = bv.reshape(bv.shape[0] * bv.shape[1], bv.shape[2])
        prod = lax.dot_general(a_ref[...].astype(MXU), bv.astype(MXU), _DN[mode], preferred_element_type=F32)
        if nk == 1:
            o_ref[...] = prod.astype(o_ref.dtype)
        else:
            @pl.when(kk == 0)
            def _():
                acc_ref[...] = prod

            @pl.when(kk > 0)
            def _():
                acc_ref[...] += prod

            @pl.when(kk == nk - 1)
            def _():
                o_ref[...] = acc_ref[...].astype(o_ref.dtype)

        if hosted:
            @pl.when((ids[0] == grid[0] - 1) & (ids[1] == grid[1] - 1) & (ids[2] == grid[2] - 1))
            def _():
                run_stages("finish")

    scratch = [pltpu.VMEM(acc_shape, F32)]
    if not hosted:
        return pl.pallas_call(
            body, name=name, grid=grid, in_specs=[a_spec, b_spec], out_specs=out_spec, out_shape=out_shape,
            scratch_shapes=scratch, compiler_params=_cp(("parallel", "parallel", "arbitrary")),
        )(a, b)
    nsem = sum(st["nsem"] for st in hosted)
    res = pl.pallas_call(
        body, name=name, grid=grid,
        in_specs=[a_spec, b_spec] + [ANY] * (nro + nbuf), out_specs=[out_spec] + [ANY] * nbuf,
        out_shape=[out_shape] + [jax.ShapeDtypeStruct(x.shape, x.dtype) for x in bufs],
        scratch_shapes=scratch + [pltpu.SemaphoreType.DMA((nsem,)), pltpu.SemaphoreType.DMA((nsem,))],
        input_output_aliases={2 + nro + t: 1 + t for t in range(nbuf)},
        compiler_params=_cp(("arbitrary", "arbitrary", "arbitrary")),
    )(a, b, *ro, *bufs)
    return res[0], list(res[1:])


def _rows_call(name, body, ins, in_rows, outs, out_rows, n_rows, tr):
    def spec(shape, tiled):
        if tiled:
            return pl.BlockSpec((tr, shape[1]), lambda i: (i, 0))
        return pl.BlockSpec(shape, lambda i: (0, 0))

    return pl.pallas_call(
        body, name=name, grid=(n_rows // tr,),
        in_specs=[spec(a.shape, t) for a, t in zip(ins, in_rows)],
        out_specs=[spec(o.shape, t) for o, t in zip(outs, out_rows)],
        out_shape=outs, compiler_params=_cp(("arbitrary",)),
    )(*ins)


def _rstd(v):
    return lax.rsqrt(jnp.mean(v * v, axis=-1, keepdims=True) + RMS_EPS)


def _norm_bwd(v, g, dy):
    r = _rstd(v)
    n = v * r
    dn = dy * g
    dv = r * (dn - n * jnp.mean(dn * n, axis=-1, keepdims=True))
    return dv, jnp.sum(dy * n, axis=0, keepdims=True)


def _acc_rows(ref, val):
    @pl.when(pl.program_id(0) == 0)
    def _():
        ref[...] = val

    @pl.when(pl.program_id(0) > 0)
    def _():
        ref[...] += val


def rms_fwd(x, g, tr):
    def body(x_ref, g_ref, o_ref):
        v = x_ref[...]
        o_ref[...] = (v * _rstd(v) * g_ref[...]).astype(o_ref.dtype)

    return _rows_call("rms_fwd", body, [x, g], [True, False], [jax.ShapeDtypeStruct(x.shape, MXU)], [True], x.shape[0], tr)[0]


def res_rms_fwd(x, o, g, tr):
    def body(x_ref, o_ref, g_ref, y_ref):
        v = o_ref[...]
        y_ref[...] = x_ref[...] + v * _rstd(v) * g_ref[...]

    return _rows_call("res_rms_fwd", body, [x, o, g], [True, True, False], [jax.ShapeDtypeStruct(x.shape, F32)], [True], x.shape[0], tr)[0]


def ple_fwd(x1, gpre, e, g, tr):
    def body(x_ref, gp_ref, e_ref, g_ref, y_ref):
        v = e_ref[...] * _sig(gp_ref[...])
        y_ref[...] = x_ref[...] + v * _rstd(v) * g_ref[...]

    return _rows_call("ple_fwd", body, [x1, gpre, e, g], [True, True, True, False], [jax.ShapeDtypeStruct(x1.shape, F32)], [True], x1.shape[0], tr)[0]


def loss_bwd(y, target, tr):
    d = y.shape[1]

    def body(y_ref, t_ref, dy_ref, sq_ref):
        diff = y_ref[...] - t_ref[...]
        dy_ref[...] = diff * (1.0 / d)
        _acc_rows(sq_ref, jnp.sum(diff * diff, axis=0, keepdims=True))

    return _rows_call("loss_bwd", body, [y, target], [True, True],
                      [jax.ShapeDtypeStruct(y.shape, F32), jax.ShapeDtypeStruct((1, d), F32)], [True, False], y.shape[0], tr)


def ple_bwd(dx2, gpre, e, g, tr):
    d = dx2.shape[1]

    def body(dx_ref, gp_ref, e_ref, g_ref, de_ref, dgp_ref, dg_ref):
        gate = _sig(gp_ref[...])
        ev = e_ref[...]
        dv, dg = _norm_bwd(ev * gate, g_ref[...], dx_ref[...])
        de_ref[...] = (dv * gate).astype(de_ref.dtype)
        dgp_ref[...] = (dv * ev * gate * (1.0 - gate)).astype(dgp_ref.dtype)
        _acc_rows(dg_ref, dg)

    return _rows_call("ple_bwd", body, [dx2, gpre, e, g], [True, True, True, False],
                      [jax.ShapeDtypeStruct(dx2.shape, MXU), jax.ShapeDtypeStruct(dx2.shape, MXU), jax.ShapeDtypeStruct((1, d), F32)],
                      [True, True, False], dx2.shape[0], tr)


def rms_bwd_res(dres, dh, x, g, tr):
    d = x.shape[1]

    def body(dr_ref, dh_ref, x_ref, g_ref, dx_ref, dg_ref):
        dv, dg = _norm_bwd(x_ref[...], g_ref[...], dh_ref[...])
        dx_ref[...] = dr_ref[...] + dv
        _acc_rows(dg_ref, dg)

    return _rows_call("rms_bwd_res", body, [dres, dh, x, g], [True, True, True, False],
                      [jax.ShapeDtypeStruct(x.shape, F32), jax.ShapeDtypeStruct((1, d), F32)], [True, False], x.shape[0], tr)


def rms_bwd(dy, o, g, tr):
    d = o.shape[1]

    def body(dy_ref, o_ref, g_ref, do_ref, dg_ref):
        dv, dg = _norm_bwd(o_ref[...], g_ref[...], dy_ref[...])
        do_ref[...] = dv.astype(do_ref.dtype)
        _acc_rows(dg_ref, dg)

    return _rows_call("rms_bwd", body, [dy, o, g], [True, True, False],
                      [jax.ShapeDtypeStruct(o.shape, MXU), jax.ShapeDtypeStruct((1, d), F32)], [True, False], o.shape[0], tr)


def _trailing_sum(ext, w):
    s, k = ext, 1
    while k < w:
        s = s + pltpu.roll(s, k, 0)
        k *= 2
    return s


def _leading_sum(ext, w):
    n = ext.shape[0]
    s, k = ext, 1
    while k < w:
        s = s + pltpu.roll(s, n - k, 0)
        k *= 2
    return s


def _pool_inv_count(rb, tr, w, c):
    t = rb * tr + lax.broadcasted_iota(jnp.int32, (tr, c), 0)
    return 1.0 / jnp.minimum(t + 1, w).astype(F32)


def _pool_d(u_ref, up_ref, rb, tr, w):
    cur = u_ref[...].astype(F32)
    prev = jnp.where(rb > 0, up_ref[...].astype(F32), 0.0)
    ext = jnp.concatenate([prev, cur], axis=0)
    win = _trailing_sum(ext, w)[MAXW:]
    return win * _pool_inv_count(rb, tr, w, cur.shape[1]) - cur


def pool_fwd(uz, w_g, bias, scale, tr):
    _, s, e = uz.shape
    ng = len(POOL_WINDOWS)
    cg = e // ng
    nb = s // tr
    hb = tr // MAXW

    def body(u_ref, up_ref, z_ref, w_ref, b_ref, sc_ref, o_ref):
        g, rb = pl.program_id(0), pl.program_id(1)
        wmat = w_ref[...].reshape(cg, cg)
        for gg, win in enumerate(POOL_WINDOWS):
            @pl.when(g == gg)
            def _(win=win):
                d = _pool_d(u_ref, up_ref, rb, tr, win)
                y = (jnp.dot(d.astype(MXU), wmat, preferred_element_type=F32) + b_ref[...]) * sc_ref[...]
                z = z_ref[...].astype(F32)
                o_ref[...] = (y * z * _sig(z)).astype(o_ref.dtype)

    return pl.pallas_call(
        body, name="pool_fwd", grid=(ng, nb),
        in_specs=[
            pl.BlockSpec((None, tr, cg), lambda g, r: (0, r, g)),
            pl.BlockSpec((None, MAXW, cg), lambda g, r: (0, jnp.maximum(r * hb - 1, 0), g)),
            pl.BlockSpec((None, tr, cg), lambda g, r: (1, r, g)),
            pl.BlockSpec((4, None, cg // 4, cg), lambda g, r: (0, g, 0, 0)),
            pl.BlockSpec((1, cg), lambda g, r: (0, g)),
            pl.BlockSpec((1, cg), lambda g, r: (0, g)),
        ],
        out_specs=pl.BlockSpec((tr, cg), lambda g, r: (r, g)),
        out_shape=jax.ShapeDtypeStruct((s, e), MXU),
        compiler_params=_cp(("parallel", "arbitrary")),
    )(uz, uz, uz, w_g, bias, scale)


def pool_bwd(uz, dy2, w_g, bias, scale, tr):
    _, s, e = uz.shape
    ng = len(POOL_WINDOWS)
    cg = e // ng
    nb = s // tr
    hb = tr // MAXW

    def body(u_ref, up_ref, z_ref, dy_ref, w_ref, b_ref, sc_ref, duz_ref, gw_ref, db_ref, dsc_ref, acc_ref, carry_ref):
        g, step = pl.program_id(0), pl.program_id(1)
        rb = nb - 1 - step
        wmat = w_ref[...].reshape(cg, cg)
        for gg, win in enumerate(POOL_WINDOWS):
            @pl.when(g == gg)
            def _(win=win):
                d = _pool_d(u_ref, up_ref, rb, tr, win).astype(MXU)
                ypre = jnp.dot(d, wmat, preferred_element_type=F32) + b_ref[...]
                z = z_ref[...].astype(F32)
                sg = _sig(z)
                dy2v = dy_ref[...].astype(F32)
                dyv = dy2v * z * sg
                duz_ref[1] = (dy2v * ypre * sc_ref[...] * sg * (1.0 + z * (1.0 - sg))).astype(duz_ref.dtype)
                dypre = dyv * sc_ref[...]
                dsc = jnp.sum(dyv * ypre, axis=0, keepdims=True)
                dbv = jnp.sum(dypre, axis=0, keepdims=True)
                dypre_b = dypre.astype(MXU)
                dd = lax.dot_general(dypre_b, wmat, _DN["nt"], preferred_element_type=F32)
                gw = lax.dot_general(d, dypre_b, _DN["tn"], preferred_element_type=F32)
                q = dd * _pool_inv_count(rb, tr, win, cg)
                nxt = jnp.where(step > 0, carry_ref[...], 0.0)
                lead = _leading_sum(jnp.concatenate([q, nxt], axis=0), win)[:tr]
                duz_ref[0] = (lead - dd).astype(duz_ref.dtype)
                carry_ref[...] = q[:MAXW]

                @pl.when(step == 0)
                def _():
                    acc_ref[...] = gw
                    db_ref[...] = dbv
                    dsc_ref[...] = dsc

                @pl.when(step > 0)
                def _():
                    acc_ref[...] += gw
                    db_ref[...] += dbv
                    dsc_ref[...] += dsc

                @pl.when(step == nb - 1)
                def _():
                    gw_ref[...] = acc_ref[...].reshape(4, cg // 4, cg).astype(gw_ref.dtype)

    return pl.pallas_call(
        body, name="pool_bwd", grid=(ng, nb),
        in_specs=[
            pl.BlockSpec((None, tr, cg), lambda g, r: (0, nb - 1 - r, g)),
            pl.BlockSpec((None, MAXW, cg), lambda g, r: (0, jnp.maximum((nb - 1 - r) * hb - 1, 0), g)),
            pl.BlockSpec((None, tr, cg), lambda g, r: (1, nb - 1 - r, g)),
            pl.BlockSpec((tr, cg), lambda g, r: (nb - 1 - r, g)),
            pl.BlockSpec((4, None, cg // 4, cg), lambda g, r: (0, g, 0, 0)),
            pl.BlockSpec((1, cg), lambda g, r: (0, g)),
            pl.BlockSpec((1, cg), lambda g, r: (0, g)),
        ],
        out_specs=[
            pl.BlockSpec((2, tr, cg), lambda g, r: (0, nb - 1 - r, g)),
            pl.BlockSpec((4, None, cg // 4, cg), lambda g, r: (0, g, 0, 0)),
            pl.BlockSpec((1, cg), lambda g, r: (0, g)),
            pl.BlockSpec((1, cg), lambda g, r: (0, g)),
        ],
        out_shape=[
            jax.ShapeDtypeStruct((2, s, e), MXU),
            jax.ShapeDtypeStruct(w_g.shape, WIRE),
            jax.ShapeDtypeStruct((1, e), F32),
            jax.ShapeDtypeStruct((1, e), F32),
        ],
        scratch_shapes=[pltpu.VMEM((cg, cg), F32), pltpu.VMEM((MAXW, cg), F32)],
        compiler_params=_cp(("parallel", "arbitrary")),
    )(uz, uz, uz, dy2, w_g, bias, scale)


HALO = 16


def _one_minus_sq(log_a, a):
    poly = (-2.0 * log_a) * (1.0 + log_a * (1.0 + log_a * (2.0 / 3.0)))
    return jnp.where(log_a > -0.01, poly, 1.0 - a * a)


def _softplus_neg(lam):
    t = jnp.exp(-jnp.abs(lam))
    log1p = jnp.where(t < 1e-3, t * (1.0 - t * (0.5 - t * (1.0 / 3.0))), jnp.log(1.0 + t))
    return jnp.maximum(-lam, 0.0) + log1p, _sig(-lam)


def _lru_gates(u_ref, up_ref, rb, sm_ref, wa, wx):
    cur = u_ref[...].astype(F32)
    prev = jnp.where(rb > 0, up_ref[...].astype(F32), 0.0)
    ext = jnp.concatenate([prev, cur], axis=0)
    taps = [cur] + [pltpu.roll(ext, k, 0)[HALO:] for k in range(1, CONV_W)]
    uc = sm_ref[CONV_W:CONV_W + 1, :]
    for k in range(CONV_W):
        uc = uc + taps[k] * sm_ref[CONV_W - 1 - k:CONV_W - k, :]
    ucb = uc.astype(MXU)
    r = _sig(jnp.dot(ucb, wa, preferred_element_type=F32) + sm_ref[5:6, :])
    ig = _sig(jnp.dot(ucb, wx, preferred_element_type=F32) + sm_ref[6:7, :])
    sp, sgn = _softplus_neg(sm_ref[7:8, :])
    log_a = r * (-LRU_C * sp)
    a = jnp.exp(log_a)
    mult = jnp.sqrt(jnp.maximum(_one_minus_sq(log_a, a), 0.0))
    return taps, uc, ucb, r, ig, sp, sgn, a, mult


LANES = 128


def _seg_scan(a, b, out_ref, scr, state, reverse):
    a_s, b_s, h_s, p_s = scr
    tr, c = a.shape
    seg = tr // 8
    nl = c // LANES
    for l in range(nl):
        a_s[l] = a[:, l * LANES:(l + 1) * LANES]
        b_s[l] = b[:, l * LANES:(l + 1) * LANES]
    h = [jnp.zeros((8, LANES), F32)] * nl
    pp = [jnp.ones((8, LANES), F32)] * nl
    for i in (range(seg - 1, -1, -1) if reverse else range(seg)):
        rows = pl.ds(i, 8, stride=seg)
        for l in range(nl):
            av = a_s[l, rows, :]
            h[l] = av * h[l] + b_s[l, rows, :]
            pp[l] = av * pp[l]
            h_s[l, rows, :] = h[l]
            p_s[l, rows, :] = pp[l]
    leaving = []
    for l in range(nl):
        st = state[:, l * LANES:(l + 1) * LANES]
        for sgm in (range(7, -1, -1) if reverse else range(8)):
            rows = pl.ds(sgm * seg, seg)
            out_ref[rows, l * LANES:(l + 1) * LANES] = h_s[l, rows, :] + p_s[l, rows, :] * st
            st = h[l][sgm:sgm + 1, :] + pp[l][sgm:sgm + 1, :] * st
        leaving.append(st)
    return jnp.concatenate(leaving, axis=1)


def lru_fwd(uz, wa_g, wx_g, small, tr):
    _, s, e = uz.shape
    cb = e // LRU_HEADS
    nb = s // tr
    hb = tr // HALO

    def body(u_ref, up_ref, z_ref, wa_ref, wx_ref, sm_ref, o_ref, h_ref, s0, s1, s2, s3, carry_ref):
        rb = pl.program_id(1)
        wa = wa_ref[...].reshape(cb, cb)
        wx = wx_ref[...].reshape(cb, cb)
        _, uc, _, _, ig, _, _, a, mult = _lru_gates(u_ref, up_ref, rb, sm_ref, wa, wx)
        start = jnp.where(rb > 0, carry_ref[0:1, :], 0.0)
        last = _seg_scan(a, mult * ig * uc, h_ref, (s0, s1, s2, s3), start, False)
        carry_ref[...] = jnp.broadcast_to(last, carry_ref.shape)
        z = z_ref[...].astype(F32)
        o_ref[...] = (h_ref[...] * z * _sig(z)).astype(o_ref.dtype)

    wspec = pl.BlockSpec((4, None, cb // 4, cb), lambda h, r: (0, h, 0, 0))
    return pl.pallas_call(
        body, name="lru_fwd", grid=(LRU_HEADS, nb),
        in_specs=[
            pl.BlockSpec((None, tr, cb), lambda h, r: (0, r, h)),
            pl.BlockSpec((None, HALO, cb), lambda h, r: (0, jnp.maximum(r * hb - 1, 0), h)),
            pl.BlockSpec((None, tr, cb), lambda h, r: (1, r, h)),
            wspec, wspec,
            pl.BlockSpec((8, cb), lambda h, r: (0, h)),
        ],
        out_specs=[pl.BlockSpec((tr, cb), lambda h, r: (r, h)), pl.BlockSpec((tr, cb), lambda h, r: (r, h))],
        out_shape=[jax.ShapeDtypeStruct((s, e), MXU), jax.ShapeDtypeStruct((s, e), F32)],
        scratch_shapes=[pltpu.VMEM((cb // LANES, tr, LANES), F32)] * 4 + [pltpu.VMEM((8, cb), F32)],
        compiler_params=_cp(("parallel", "arbitrary")),
    )(uz, uz, uz, wa_g, wx_g, small)


def lru_bwd(uz, hst, dy2, wa_g, wx_g, small, tr):
    _, s, e = uz.shape
    cb = e // LRU_HEADS
    nb = s // tr
    hb = tr // HALO

    def body(u_ref, up_ref, z_ref, h_ref, hp_ref, dy_ref, wa_ref, wx_ref, sm_ref,
             duz_ref, gwa_ref, gwx_ref, dsm_ref, s0, s1, s2, s3, g_s, acc_a, acc_x, gcar, acar, dcar):
        step = pl.program_id(1)
        rb = nb - 1 - step
        wa = wa_ref[...].reshape(cb, cb)
        wx = wx_ref[...].reshape(cb, cb)
        taps, uc, ucb, r, ig, sp, sgn, a, mult = _lru_gates(u_ref, up_ref, rb, sm_ref, wa, wx)
        row = lax.broadcasted_iota(jnp.int32, a.shape, 0)
        z = z_ref[...].astype(F32)
        sg = _sig(z)
        dy2v = dy_ref[...].astype(F32)
        hv = h_ref[...]
        duz_ref[1] = (dy2v * hv * sg * (1.0 + z * (1.0 - sg))).astype(duz_ref.dtype)
        a_next = jnp.where(row == tr - 1, jnp.where(step > 0, acar[0:1, :], 0.0), pltpu.roll(a, tr - 1, 0))
        g_first = _seg_scan(a_next, dy2v * z * sg, g_s, (s0, s1, s2, s3), jnp.where(step > 0, gcar[0:1, :], 0.0), True)
        gcar[...] = jnp.broadcast_to(g_first, gcar.shape)
        acar[...] = jnp.broadcast_to(a[0:1, :], acar.shape)
        gv = g_s[...]
        h_before = jnp.where(rb > 0, hp_ref[HALO - 1:HALO, :], 0.0)
        h_prev = jnp.where(row == 0, h_before, pltpu.roll(hv, 1, 0))
        da = gv * h_prev
        gu = gv * uc
        dmult = gu * ig
        dig = gu * mult
        dlog_a = da * a - dmult * jnp.where(mult > 0.0, a * a / mult, 0.0)
        dra = dlog_a * (-LRU_C) * sp * r * (1.0 - r)
        dix = dig * ig * (1.0 - ig)
        dl = jnp.sum(dlog_a * r, axis=0, keepdims=True) * (LRU_C * sgn)
        dra_b, dix_b = dra.astype(MXU), dix.astype(MXU)
        duc = (gv * mult * ig + lax.dot_general(dra_b, wa, _DN["nt"], preferred_element_type=F32)
               + lax.dot_general(dix_b, wx, _DN["nt"], preferred_element_type=F32))
        gwa = lax.dot_general(ucb, dra_b, _DN["tn"], preferred_element_type=F32)
        gwx = lax.dot_general(ucb, dix_b, _DN["tn"], preferred_element_type=F32)
        ext = jnp.concatenate([duc, jnp.where(step > 0, dcar[...], 0.0)], axis=0)
        n = ext.shape[0]
        du = duc * sm_ref[CONV_W - 1:CONV_W, :]
        for k in range(1, CONV_W):
            du = du + pltpu.roll(ext, n - k, 0)[:tr] * sm_ref[CONV_W - 1 - k:CONV_W - k, :]
        duz_ref[0] = du.astype(duz_ref.dtype)
        dcar[...] = duc[:HALO]
        rows = [jnp.sum(duc * taps[CONV_W - 1 - k], axis=0, keepdims=True) for k in range(CONV_W)]
        rows += [jnp.sum(duc, axis=0, keepdims=True), jnp.sum(dra, axis=0, keepdims=True),
                 jnp.sum(dix, axis=0, keepdims=True), dl]

        @pl.when(step == 0)
        def _():
            acc_a[...] = gwa
            acc_x[...] = gwx
            for k, rv in enumerate(rows):
                dsm_ref[k:k + 1, :] = rv

        @pl.when(step > 0)
        def _():
            acc_a[...] += gwa
            acc_x[...] += gwx
            for k, rv in enumerate(rows):
                dsm_ref[k:k + 1, :] += rv

        @pl.when(step == nb - 1)
        def _():
            gwa_ref[...] = acc_a[...].reshape(4, cb // 4, cb).astype(gwa_ref.dtype)
            gwx_ref[...] = acc_x[...].reshape(4, cb // 4, cb).astype(gwx_ref.dtype)

    wspec = pl.BlockSpec((4, None, cb // 4, cb), lambda h, r: (0, h, 0, 0))
    blk = pl.BlockSpec((tr, cb), lambda h, r: (nb - 1 - r, h))
    return pl.pallas_call(
        body, name="lru_bwd", grid=(LRU_HEADS, nb),
        in_specs=[
            pl.BlockSpec((None, tr, cb), lambda h, r: (0, nb - 1 - r, h)),
            pl.BlockSpec((None, HALO, cb), lambda h, r: (0, jnp.maximum((nb - 1 - r) * hb - 1, 0), h)),
            pl.BlockSpec((None, tr, cb), lambda h, r: (1, nb - 1 - r, h)),
            blk,
            pl.BlockSpec((HALO, cb), lambda h, r: (jnp.maximum((nb - 1 - r) * hb - 1, 0), h)),
            blk,
            wspec, wspec,
            pl.BlockSpec((8, cb), lambda h, r: (0, h)),
        ],
        out_specs=[
            pl.BlockSpec((2, tr, cb), lambda h, r: (0, nb - 1 - r, h)),
            wspec, wspec,
            pl.BlockSpec((8, cb), lambda h, r: (0, h)),
        ],
        out_shape=[
            jax.ShapeDtypeStruct((2, s, e), MXU),
            jax.ShapeDtypeStruct(wa_g.shape, WIRE),
            jax.ShapeDtypeStruct(wx_g.shape, WIRE),
            jax.ShapeDtypeStruct((8, e), F32),
        ],
        scratch_shapes=([pltpu.VMEM((cb // LANES, tr, LANES), F32)] * 4 + [pltpu.VMEM((tr, cb), F32)]
                        + [pltpu.VMEM((cb, cb), F32)] * 2 + [pltpu.VMEM((8, cb), F32)] * 2 + [pltpu.VMEM((HALO, cb), F32)]),
        compiler_params=_cp(("parallel", "arbitrary")),
    )(uz, uz, uz, hst, hst, dy2, wa_g, wx_g, small)


def _place():
    x, y, c = lax.axis_index("x"), lax.axis_index("y"), lax.axis_index("c")
    chips = [(1 - x, y), (x, 1 - y), (1 - x, 1 - y)]
    return x, y, c, chips


def _rcopy(src, dst, send_sems, recv_sems, k, to):
    return pltpu.make_async_remote_copy(src_ref=src, dst_ref=dst, send_sem=send_sems.at[k], recv_sem=recv_sems.at[k],
                                        device_id=to, device_id_type=MESHID)


def _stage_gather_ici(bufs):
    n = len(bufs)

    def copies(refs, ss, rs, off, own):
        x, y, c, chips = _place()
        out = []
        for a in range(n):
            for jj, ch in enumerate(chips):
                blk = refs[a].at[2 * x + y if own else 2 * ch[0] + ch[1], :, c]
                out.append(_rcopy(blk, blk, ss, rs, off + 3 * a + jj, (*ch, c)))
        return out

    def start(ro, refs, ss, rs, off):
        for cp in copies(refs, ss, rs, off, True):
            cp.start()

    def finish(ro, refs, ss, rs, off):
        for cp in copies(refs, ss, rs, off, False):
            cp.wait_recv()
        for cp in copies(refs, ss, rs, off, True):
            cp.wait_send()

    return dict(ro=[], bufs=list(bufs), nsem=3 * n, start=start, finish=finish)


def _stage_gather_d2d(bufs):
    n = len(bufs)

    def copies(refs, ss, rs, off, sending):
        x, y, c, chips = _place()
        out = []
        for a in range(n):
            for jj, ch in enumerate(chips):
                blk = refs[a].at[2 * ch[0] + ch[1], :, c if sending else 1 - c]
                out.append(_rcopy(blk, blk, ss, rs, off + 3 * a + jj, (x, y, 1 - c)))
        return out

    def start(ro, refs, ss, rs, off):
        for cp in copies(refs, ss, rs, off, True):
            cp.start()

    def finish(ro, refs, ss, rs, off):
        for cp in copies(refs, ss, rs, off, False):
            cp.wait_recv()
        for cp in copies(refs, ss, rs, off, True):
            cp.wait_send()

    return dict(ro=[], bufs=list(bufs), nsem=3 * n, start=start, finish=finish)


def _stage_scatter_ici(parts, gots):
    n = len(parts)

    def copies(ro, refs, ss, rs, off):
        x, y, c, chips = _place()
        return [_rcopy(ro[a].at[2 * ch[0] + ch[1]], refs[a].at[jj], ss, rs, off + 3 * a + jj, (*ch, c))
                for a in range(n) for jj, ch in enumerate(chips)]

    def start(ro, refs, ss, rs, off):
        for cp in copies(ro, refs, ss, rs, off):
            cp.start()

    def finish(ro, refs, ss, rs, off):
        for cp in copies(ro, refs, ss, rs, off):
            cp.wait()

    return dict(ro=list(parts), bufs=list(gots), nsem=3 * n, start=start, finish=finish)


def cast_into_slab(w, k_idx):
    l, r, c = w.shape
    rh = r // 2
    tr = _block_rows(rh, c, 4, 4 << 20)
    nbh = rh // tr

    def body(k_ref, w_ref, o_ref):
        o_ref[...] = w_ref[...].astype(o_ref.dtype)

    return pl.pallas_call(
        body, name="cast_into_slab",
        grid_spec=pltpu.PrefetchScalarGridSpec(
            num_scalar_prefetch=1, grid=(l, 2, nbh),
            in_specs=[pl.BlockSpec((None, tr, c), lambda i, h, b, k_ref: (i, h * nbh + b, 0))],
            out_specs=pl.BlockSpec((None, None, None, tr, c), lambda i, h, b, k_ref: (k_ref[0], i, h, b, 0))),
        out_shape=jax.ShapeDtypeStruct((4, l, 2, rh, c), WIRE),
        compiler_params=_cp(("parallel", "parallel", "parallel")),
    )(k_idx, w)


def gather_weights(bufs):
    n = len(bufs)

    def body(*refs):
        outs = refs[n:2 * n]
        send_sems, recv_sems = refs[2 * n:]
        x, y, c, chips = _place()
        me = 2 * x + y
        sib = (x, y, 1 - c)
        sends = []
        for a in range(n):
            for jj, ch in enumerate(chips):
                mine = outs[a].at[me, :, c]
                cp = _rcopy(mine, mine, send_sems, recv_sems, 6 * a + jj, (*ch, c))
                cp.start()
                sends.append(cp)
        for a in range(n):
            for jj, ch in enumerate(chips):
                blk = outs[a].at[2 * ch[0] + ch[1], :, c]
                _rcopy(blk, blk, send_sems, recv_sems, 6 * a + jj, (*ch, c)).wait_recv()
                fw = _rcopy(blk, blk, send_sems, recv_sems, 6 * a + 3 + jj, sib)
                fw.start()
                sends.append(fw)
        for a in range(n):
            for jj, ch in enumerate(chips):
                blk = outs[a].at[2 * ch[0] + ch[1], :, 1 - c]
                _rcopy(blk, blk, send_sems, recv_sems, 6 * a + 3 + jj, sib).wait_recv()
        for cp in sends:
            cp.wait_send()

    return pl.pallas_call(
        body, name="gather_weights",
        in_specs=[ANY] * n, out_specs=[ANY] * n,
        out_shape=[jax.ShapeDtypeStruct(a.shape, a.dtype) for a in bufs],
        scratch_shapes=[pltpu.SemaphoreType.DMA((6 * n,)), pltpu.SemaphoreType.DMA((6 * n,))],
        input_output_aliases={a: a for a in range(n)},
        compiler_params=pltpu.CompilerParams(has_side_effects=True),
    )(*bufs)


def all_gather_small(v, name):
    m_per, n = v.shape

    def body(x_ref, out_ref, send_sems, recv_sems, local_sem):
        x, y, c, chips = _place()
        me, sibling = (x, y, c), (x, y, 1 - c)

        def rows(px, py, pc):
            return out_ref.at[pl.ds((4 * px + 2 * py + pc) * m_per, m_per), :]

        def copy(k, block, to, src=None):
            return _rcopy(rows(*block) if src is None else src, rows(*block), send_sems, recv_sems, k, to)

        mine = pltpu.make_async_copy(x_ref, rows(*me), local_sem)
        mine.start()
        first = [copy(0, me, sibling, src=x_ref)]
        first += [copy(1 + jj, me, (*chip, c), src=x_ref) for jj, chip in enumerate(chips)]
        for cp in first:
            cp.start()
        passed = [copy(4 + jj, (*chip, c), sibling) for jj, chip in enumerate(chips)]
        for jj, chip in enumerate(chips):
            copy(1 + jj, (*chip, c), me).wait_recv()
            passed[jj].start()
        copy(0, sibling, me).wait_recv()
        for jj, chip in enumerate(chips):
            copy(4 + jj, (*chip, 1 - c), me).wait_recv()
        for cp in first + passed:
            cp.wait_send()
        mine.wait()

    return pl.pallas_call(
        body, name=name,
        out_shape=jax.ShapeDtypeStruct((8 * m_per, n), v.dtype),
        in_specs=[pl.BlockSpec(memory_space=pltpu.VMEM)],
        out_specs=pl.BlockSpec(memory_space=pltpu.VMEM),
        scratch_shapes=[pltpu.SemaphoreType.DMA((7,)), pltpu.SemaphoreType.DMA((7,)), pltpu.SemaphoreType.DMA],
        compiler_params=pltpu.CompilerParams(vmem_limit_bytes=VMEM_LIMIT),
    )(v)


def sum_devices(g):
    def body(g_ref, o_ref):
        acc = g_ref[0]
        for d in range(1, 8):
            acc = acc + g_ref[d]
        o_ref[...] = acc

    return pl.pallas_call(body, name="sum_devices", out_shape=jax.ShapeDtypeStruct(g.shape[1:], g.dtype),
                          compiler_params=pltpu.CompilerParams(vmem_limit_bytes=VMEM_LIMIT))(g)


def send_other_half(grads):
    n = len(grads)

    def body(*refs):
        ins, outs = refs[:n], refs[n:2 * n]
        send_sems, recv_sems = refs[2 * n:]
        x, y, c, _ = _place()
        sib = (x, y, 1 - c)
        cps = [_rcopy(ins[a].at[:, :, 1 - c], outs[a], send_sems, recv_sems, a, sib) for a in range(n)]
        for cp in cps:
            cp.start()
        for cp in cps:
            cp.wait()

    return pl.pallas_call(
        body, name="send_other_half", in_specs=[ANY] * n, out_specs=[ANY] * n,
        out_shape=[jax.ShapeDtypeStruct(g.shape[:2] + g.shape[3:], g.dtype) for g in grads],
        scratch_shapes=[pltpu.SemaphoreType.DMA((n,)), pltpu.SemaphoreType.DMA((n,))],
        compiler_params=pltpu.CompilerParams(has_side_effects=True),
    )(*grads)


def scatter_to_chips(parts):
    n = len(parts)

    def body(*refs):
        ins, outs = refs[:n], refs[n:2 * n]
        send_sems, recv_sems = refs[2 * n:]
        x, y, c, chips = _place()
        cps = []
        for a in range(n):
            for jj, ch in enumerate(chips):
                cps.append(_rcopy(ins[a].at[2 * ch[0] + ch[1]], outs[a].at[jj], send_sems, recv_sems, 3 * a + jj, (*ch, c)))
        for cp in cps:
            cp.start()
        for cp in cps:
            cp.wait()

    return pl.pallas_call(
        body, name="scatter_to_chips", in_specs=[ANY] * n, out_specs=[ANY] * n,
        out_shape=[jax.ShapeDtypeStruct((3,) + p.shape[1:], p.dtype) for p in parts],
        scratch_shapes=[pltpu.SemaphoreType.DMA((3 * n,)), pltpu.SemaphoreType.DMA((3 * n,))],
        compiler_params=pltpu.CompilerParams(has_side_effects=True),
    )(*parts)


def share_halves(bufs, spans):
    n = len(bufs)

    def body(*refs):
        outs = refs[n:2 * n]
        send_sems, recv_sems = refs[2 * n:]
        x, y, c, _ = _place()
        sib = (x, y, 1 - c)

        def blk(a, half):
            return outs[a].at[pl.ds(spans[a][0], spans[a][1]), half]

        cps = [_rcopy(blk(a, c), blk(a, c), send_sems, recv_sems, a, sib) for a in range(n)]
        for cp in cps:
            cp.start()
        for a in range(n):
            _rcopy(blk(a, 1 - c), blk(a, 1 - c), send_sems, recv_sems, a, sib).wait_recv()
        for cp in cps:
            cp.wait_send()

    return pl.pallas_call(
        body, name="share_halves", in_specs=[ANY] * n, out_specs=[ANY] * n,
        out_shape=[jax.ShapeDtypeStruct(b.shape, b.dtype) for b in bufs],
        scratch_shapes=[pltpu.SemaphoreType.DMA((n,)), pltpu.SemaphoreType.DMA((n,))],
        input_output_aliases={a: a for a in range(n)},
        compiler_params=pltpu.CompilerParams(has_side_effects=True),
    )(*bufs)


def _block_rows(r, c, itemsize, budget=1 << 20):
    tr = r
    while tr * c * itemsize > budget and tr % 16 == 0:
        tr //= 2
    return tr


def add_halves(g, got, c_idx):
    k4, l, _, rh, cc = g.shape
    tr = _block_rows(rh, cc, 4, 4 << 20)

    def body(c_ref, g_ref, r_ref, o_ref):
        o_ref[...] = (g_ref[...].astype(F32) + r_ref[...].astype(F32)).astype(o_ref.dtype)

    return pl.pallas_call(
        body, name="add_halves",
        grid_spec=pltpu.PrefetchScalarGridSpec(
            num_scalar_prefetch=1, grid=(k4, l, rh // tr),
            in_specs=[pl.BlockSpec((None, None, None, tr, cc), lambda k, i, b, c_ref: (k, i, c_ref[0], b, 0)),
                      pl.BlockSpec((None, None, tr, cc), lambda k, i, b, c_ref: (k, i, b, 0))],
            out_specs=pl.BlockSpec((None, None, tr, cc), lambda k, i, b, c_ref: (k, i, b, 0))),
        out_shape=jax.ShapeDtypeStruct(got.shape, WIRE),
        compiler_params=_cp(("parallel", "parallel", "parallel")),
    )(c_idx, g, got)


def sum_chips(own, got, kc_idx, full, first):
    _, l, rh, cc = own.shape
    tr = _block_rows(rh, cc, 4, 4 << 20)

    def body(k_ref, o_ref, r_ref, _full, s_ref):
        s_ref[...] = ((o_ref[...].astype(F32) + r_ref[0].astype(F32)) + r_ref[1].astype(F32)) + r_ref[2].astype(F32)

    return pl.pallas_call(
        body, name="sum_chips",
        grid_spec=pltpu.PrefetchScalarGridSpec(
            num_scalar_prefetch=1, grid=(l, rh // tr),
            in_specs=[pl.BlockSpec((None, None, tr, cc), lambda i, b, k_ref: (k_ref[0], i, b, 0)),
                      pl.BlockSpec((3, None, tr, cc), lambda i, b, k_ref: (0, i, b, 0)),
                      ANY],
            out_specs=pl.BlockSpec((None, None, tr, cc), lambda i, b, k_ref: (first + i, k_ref[1], b, 0))),
        out_shape=jax.ShapeDtypeStruct(full.shape, F32),
        input_output_aliases={3: 0},
        compiler_params=_cp(("parallel", "parallel")),
    )(kc_idx, own, got, full)


def _adam_math(w, g, m, v):
    m = ADAM_B1 * m + (1.0 - ADAM_B1) * g
    v = ADAM_B2 * v + (1.0 - ADAM_B2) * (g * g)
    m_hat = m / (1.0 - ADAM_B1 ** ADAM_STEP)
    v_hat = v / (1.0 - ADAM_B2 ** ADAM_STEP)
    delta = -ADAM_LR * (m_hat / (jnp.sqrt(v_hat) + ADAM_EPS) + ADAM_WD * w)
    return delta, m, v


def adamw(w, g, m, v):
    l, r, c = w.shape
    tr = _block_rows(r, c, 4, 2 << 20)

    def body(w_ref, g_ref, m_ref, v_ref, go_ref, d_ref, mo_ref, vo_ref):
        gv = g_ref[...]
        go_ref[...] = gv
        d_ref[...], mo_ref[...], vo_ref[...] = _adam_math(w_ref[...], gv, m_ref[...], v_ref[...])

    spec = pl.BlockSpec((None, tr, c), lambda i, b: (i, b, 0))
    return pl.pallas_call(
        body, name="adamw", grid=(l, r // tr), in_specs=[spec] * 4, out_specs=[spec] * 4,
        out_shape=[jax.ShapeDtypeStruct(w.shape, F32)] * 4, compiler_params=_cp(("parallel", "parallel")),
    )(w, g, m, v)


def adamw_small(w, g, m, v):
    def body(w_ref, g_ref, m_ref, v_ref, d_ref, mo_ref, vo_ref):
        d_ref[...], mo_ref[...], vo_ref[...] = _adam_math(w_ref[...], g_ref[...], m_ref[...], v_ref[...])

    return pl.pallas_call(body, name="adamw_small", out_shape=[jax.ShapeDtypeStruct(w.shape, F32)] * 3)(w, g, m, v)


def kernel(x, p, w_in, w_out, g_pre, g_post, pool_w, pool_b, pool_scale, conv_w, conv_b, lru_wa, lru_ba, lru_wx, lru_bx, lru_L, w_ple, w_ple_gate, g_ple_in, g_ple_out, loss_target, m_w_in, m_w_out, m_g_pre, m_g_post, m_pool_w, m_pool_b, m_pool_scale, m_conv_w, m_conv_b, m_lru_wa, m_lru_ba, m_lru_wx, m_lru_bx, m_lru_L, m_w_ple, m_w_ple_gate, m_g_ple_in, m_g_ple_out, v_w_in, v_w_out, v_g_pre, v_g_post, v_pool_w, v_pool_b, v_pool_scale, v_conv_w, v_conv_b, v_lru_wa, v_lru_ba, v_lru_wx, v_lru_bx, v_lru_L, v_w_ple, v_w_ple_gate, v_g_ple_in, v_g_ple_out):
    depth = w_in.shape[0]
    _, s, d = x.shape
    e = 2 * d
    kp = p.shape[-1]
    nmix = pool_w.shape[0]
    ngrp = pool_w.shape[1]
    cg = e // ngrp
    cb = e // LRU_HEADS
    xi, yi, ci = lax.axis_index("x"), lax.axis_index("y"), lax.axis_index("c")
    me = 2 * xi + yi
    c_idx = jnp.reshape(ci, (1,)).astype(jnp.int32)
    k_idx = jnp.reshape(me, (1,)).astype(jnp.int32)
    tr_row = _tile(s, 256)
    tr_mix = _tile(s, 512)
    tm = _tile(s, 1024)
    tm2 = _tile(s, 2048)

    def halves(a):
        return a.reshape(a.shape[0], 2, a.shape[1] // 2, a.shape[2])

    big = {
        "w_in": w_in, "w_out": w_out, "gate": w_ple_gate, "ple": w_ple,
        "pool": pool_w.reshape(nmix * ngrp, cg // 4, cg),
        "wa": lru_wa.reshape(nmix * LRU_HEADS, cb // 4, cb), "wx": lru_wx.reshape(nmix * LRU_HEADS, cb // 4, cb),
    }
    names = list(big)

    def layer_shards(i):
        sh = {"w_in": w_in[i][None], "w_out": w_out[i][None], "gate": w_ple_gate[i][None], "ple": w_ple[i][None]}
        if i % 2 == 0:
            sh["pool"] = pool_w[i // 2]
        else:
            sh["wa"], sh["wx"] = lru_wa[i // 2], lru_wx[i // 2]
        return sh

    def mixer_names(i):
        return ["pool"] if i % 2 == 0 else ["wa", "wx"]

    wbuf = [{n: cast_into_slab(w, k_idx) for n, w in layer_shards(i).items()} for i in range(depth)]
    first = list(wbuf[0])
    wbuf[0] = dict(zip(first, gather_weights([wbuf[0][n] for n in first])))

    def full_w(i, n):
        b = wbuf[i][n]
        return b.reshape(b.shape[0], b.shape[1], 2 * b.shape[3], b.shape[4])

    def run_mm(stages, *args):
        if not stages:
            return _matmul(*args)
        out, new = _matmul(*args, hosted=[mk([wbuf[l][n] for n in nms]) for mk, l, nms in stages])
        slots = [(l, n) for _, l, nms in stages for n in nms]
        for (l, n), b in zip(slots, new):
            wbuf[l][n] = b
        return out

    ec = e // 4
    small_loc = jnp.concatenate([conv_w, conv_b[:, None], lru_ba[:, None], lru_bx[:, None], lru_L[:, None]], axis=1)
    sm_all = all_gather_small(small_loc.reshape(nmix * 8, ec), "gather_small").reshape(4, 2, nmix, 8, ec)
    lru_small = jnp.transpose(sm_all[:, 0], (1, 2, 0, 3)).reshape(nmix, 8, e)

    xs = x[0]
    saved = []
    for i in range(depth):
        j = i // 2
        h = rms_fwd(xs, g_pre[i][None], tr_row)
        nj = (2 * e) // 1024 if (2 * e) % 1024 == 0 else 4
        tn = (2 * e) // nj
        per = e // tn
        perk = (e // 2) // tn
        nxt = i + 1 if i + 1 < depth else None
        stages = [(_stage_gather_d2d, i, ["gate", "ple"])] if i > 0 else []
        if nxt is not None:
            stages.append((_stage_gather_ici, nxt, ["w_in"]))
        uz = run_mm(
            stages, "mm_in", "nn", (s // tm2, nj, 1), h, pl.BlockSpec((tm2, d), lambda a, b, k: (a, 0)),
            full_w(i, "w_in"), pl.BlockSpec((None, None, d, tn), lambda a, b, k, perk=perk: (b // perk, 0, 0, b % perk)),
            jax.ShapeDtypeStruct((2, s, e), MXU), pl.BlockSpec((None, tm2, tn), lambda a, b, k, per=per: (b // per, a, b % per)),
            (8, 128))
        if i % 2 == 0:
            y2 = pool_fwd(uz, full_w(i, "pool"), pool_b[j][None], pool_scale[j][None], tr_mix)
            hst = None
        else:
            y2, hst = lru_fwd(uz, full_w(i, "wa"), full_w(i, "wx"), lru_small[j], tr_mix)
        tn_o = _tile(d, 1024)
        stages = [(_stage_gather_ici, nxt, ["w_out"] + mixer_names(nxt))] if nxt is not None else []
        o = run_mm(
            stages, "mm_out", "nn", (s // tm, d // tn_o, 1), y2, pl.BlockSpec((tm, e), lambda a, b, k: (a, 0)),
            full_w(i, "w_out"), pl.BlockSpec((4, None, e // 4, tn_o), lambda a, b, k: (0, 0, 0, b)),
            jax.ShapeDtypeStruct((s, d), F32), pl.BlockSpec((tm, tn_o), lambda a, b, k: (a, b)), (8, 128))
        x1 = res_rms_fwd(xs, o, g_post[i][None], tr_row)
        hn = rms_fwd(x1, g_ple_in[i][None], tr_row)
        stages = []
        if nxt is not None:
            stages = [(_stage_gather_ici, nxt, ["gate", "ple"]), (_stage_gather_d2d, nxt, ["w_in", "w_out"] + mixer_names(nxt))]
        gpre = run_mm(
            stages, "mm_gate", "nn", (s // tm2, d // tn_o, 1), hn, pl.BlockSpec((tm2, d), lambda a, b, k: (a, 0)),
            full_w(i, "gate"), pl.BlockSpec((4, None, d // 4, tn_o), lambda a, b, k: (0, 0, 0, b)),
            jax.ShapeDtypeStruct((s, d), F32), pl.BlockSpec((tm2, tn_o), lambda a, b, k: (a, b)), (8, 128))
        pe = p[i, 0]
        ev = _matmul(
            "mm_ple", "nn", (s // tm, 4, 1), pe, pl.BlockSpec((tm, kp), lambda a, b, k: (a, 0)),
            full_w(i, "ple"), pl.BlockSpec((None, None, kp, d // 4), lambda a, b, k: (b, 0, 0, 0)),
            jax.ShapeDtypeStruct((s, d), F32), pl.BlockSpec((tm, d // 4), lambda a, b, k: (a, b)), (8, 128))
        x2 = ple_fwd(x1, gpre, ev, g_ple_out[i][None], tr_row)
        saved.append((xs, h, uz, y2, hst, o, x1, hn, gpre, ev))
        xs = x2

    dx, sq = loss_bwd(xs, loss_target[0], tr_row)
    d_gpre, d_gpost, d_gin, d_gout = [None] * depth, [None] * depth, [None] * depth, [None] * depth
    d_pool_b, d_pool_sc, d_lru_small = [None] * nmix, [None] * nmix, [None] * nmix
    ts = _tile(s, 1024)
    kc_idx = jnp.stack([me, ci]).astype(jnp.int32)
    full = {n: lax.empty(halves(big[n]).shape, F32) for n in names}
    sums, gots = [None] * depth, [None] * depth

    def scatter_stage(l, nms):
        return _stage_scatter_ici([sums[l][n] for n in nms], [gots[l][n] for n in nms]), [(l, n) for n in nms]

    def run_mm_scatter(specs, *args):
        if not specs:
            return _matmul(*args)
        out, new = _matmul(*args, hosted=[st for st, _ in specs])
        for (l, n), b in zip([slot for _, sl in specs for slot in sl], new):
            gots[l][n] = b
        return out

    def first_row(i, n):
        return i if n in ("w_in", "w_out", "gate", "ple") else (i // 2) * (ngrp if n == "pool" else LRU_HEADS)

    for i in reversed(range(depth)):
        j = i // 2
        prev = i + 1 if i + 1 < depth else None
        x0, h, uz, y2, hst, o, x1, hn, gpre, ev = saved[i]
        pe = p[i, 0]
        gl = {}
        de, dgp, d_gout[i] = ple_bwd(dx, gpre, ev, g_ple_out[i][None], tr_row)
        gl["ple"] = _matmul(
            "mm_dple", "tn", (1, 4, s // ts), pe, pl.BlockSpec((ts, kp), lambda a, b, k: (k, 0)),
            de, pl.BlockSpec((ts, d // 4), lambda a, b, k: (k, b)),
            jax.ShapeDtypeStruct((4, 1, kp, d // 4), WIRE), pl.BlockSpec((None, None, kp, d // 4), lambda a, b, k: (b, 0, 0, 0)),
            (kp, d // 4))
        tn_o = _tile(d, 1024)
        gl["gate"] = _matmul(
            "mm_dgate", "tn", (4, d // tn_o, 1), hn, pl.BlockSpec((s, d // 4), lambda a, b, k: (0, a)),
            dgp, pl.BlockSpec((s, tn_o), lambda a, b, k: (0, b)),
            jax.ShapeDtypeStruct((4, 1, d // 4, d), WIRE), pl.BlockSpec((None, None, d // 4, tn_o), lambda a, b, k: (a, 0, 0, b)),
            (8, 128))
        dhn = _matmul(
            "mm_dhn", "nt", (s // tm2, 4, 1), dgp, pl.BlockSpec((tm2, d), lambda a, b, k: (a, 0)),
            full_w(i, "gate"), pl.BlockSpec((None, None, d // 4, d), lambda a, b, k: (b, 0, 0, 0)),
            jax.ShapeDtypeStruct((s, d), F32), pl.BlockSpec((tm2, d // 4), lambda a, b, k: (a, b)), (8, 128))
        dx1, d_gin[i] = rms_bwd_res(dx, dhn, x1, g_ple_in[i][None], tr_row)
        do, d_gpost[i] = rms_bwd(dx1, o, g_post[i][None], tr_row)
        gl["w_out"] = _matmul(
            "mm_dwout", "tn", (4, d // tn_o, 1), y2, pl.BlockSpec((s, e // 4), lambda a, b, k: (0, a)),
            do, pl.BlockSpec((s, tn_o), lambda a, b, k: (0, b)),
            jax.ShapeDtypeStruct((4, 1, e // 4, d), WIRE), pl.BlockSpec((None, None, e // 4, tn_o), lambda a, b, k: (a, 0, 0, b)),
            (8, 128))
        dy2 = _matmul(
            "mm_dy2", "nt", (s // tm2, 4, 1), do, pl.BlockSpec((tm2, d), lambda a, b, k: (a, 0)),
            full_w(i, "w_out"), pl.BlockSpec((None, None, e // 4, d), lambda a, b, k: (b, 0, 0, 0)),
            jax.ShapeDtypeStruct((s, e), MXU), pl.BlockSpec((tm2, e // 4), lambda a, b, k: (a, b)), (8, 128))
        if i % 2 == 0:
            duz, gl["pool"], d_pool_b[j], d_pool_sc[j] = pool_bwd(uz, dy2, full_w(i, "pool"), pool_b[j][None], pool_scale[j][None], tr_mix)
        else:
            duz, gl["wa"], gl["wx"], d_lru_small[j] = lru_bwd(uz, hst, dy2, full_w(i, "wa"), full_w(i, "wx"), lru_small[j], tr_mix)
        tmi = _tile(d, 1024)
        tni = _tile(e // 2, 1024)
        nslab = (e // 2) // tni
        gl["w_in"] = run_mm_scatter(
            [scatter_stage(prev, ["w_in"])] if prev is not None else [],
            "mm_dwin", "tn", (d // tmi, 4 * nslab, 1), h, pl.BlockSpec((s, tmi), lambda a, b, k: (0, a)),
            duz, pl.BlockSpec((None, s, tni), lambda a, b, k, nslab=nslab: (b // (2 * nslab), 0, b % (2 * nslab))),
            jax.ShapeDtypeStruct((4, 1, d, e // 2), WIRE),
            pl.BlockSpec((None, None, tmi, tni), lambda a, b, k, nslab=nslab: (b // nslab, 0, a, b % nslab)),
            (8, 128))
        tnd = _tile(d, 1024)
        dh = run_mm_scatter(
            [scatter_stage(prev, ["w_out", "gate", "ple"] + mixer_names(prev))] if prev is not None else [],
            "mm_dh", "nt", (s // tm, d // tnd, 4), duz, pl.BlockSpec((None, tm, e // 2), lambda a, b, k: (k // 2, a, k % 2)),
            full_w(i, "w_in"), pl.BlockSpec((None, None, tnd, e // 2), lambda a, b, k: (k, 0, b, 0)),
            jax.ShapeDtypeStruct((s, d), F32), pl.BlockSpec((tm, tnd), lambda a, b, k: (a, b)), (tm, tnd))
        dx, d_gpre[i] = rms_bwd_res(dx1, dh, x0, g_pre[i][None], tr_row)
        if prev is not None:
            for n in gots[prev]:
                full[n] = sum_chips(sums[prev][n], gots[prev][n], kc_idx, full[n], first_row(prev, n))
        lnames = list(gl)
        gparts = [gl[n].reshape(4, gl[n].shape[1], 2, gl[n].shape[2] // 2, gl[n].shape[3]) for n in lnames]
        from_sib = send_other_half(gparts)
        sums[i] = {n: add_halves(g, r, c_idx) for n, g, r in zip(lnames, gparts, from_sib)}
        gots[i] = {n: lax.empty((3,) + sums[i][n].shape[1:], WIRE) for n in lnames}
    grad_x = dx[None]

    lnames = list(gots[0])
    for n, b in zip(lnames, scatter_to_chips([sums[0][n] for n in lnames])):
        full[n] = sum_chips(sums[0][n], b, kc_idx, full[n], first_row(0, n))
    shared = share_halves([full[n] for n in names], [(0, full[n].shape[0]) for n in names])
    grads = {n: f.reshape(big[n].shape) for n, f in zip(names, shared)}

    def rows_e(a):
        return jnp.stack(a).reshape(-1, e) if isinstance(a, list) else a.reshape(-1, e)

    pack = [rows_e([g[0] for g in d_gpre]), rows_e([g[0] for g in d_gpost]), rows_e([g[0] for g in d_gin]), rows_e([g[0] for g in d_gout]),
            jnp.concatenate(d_pool_b, axis=0), jnp.concatenate(d_pool_sc, axis=0), jnp.concatenate(d_lru_small, axis=0),
            jnp.pad(sq, ((0, 0), (0, e - d)))]
    sizes = [a.shape[0] for a in pack]
    packed = jnp.concatenate(pack, axis=0)
    nrow = packed.shape[0]
    nrow_p = -(-nrow // 8) * 8
    packed = jnp.pad(packed, ((0, nrow_p - nrow), (0, 0)))
    total = sum_devices(all_gather_small(packed, "gather_grads").reshape(8, nrow_p, e))
    parts, off = [], 0
    for n_ in sizes:
        parts.append(total[off:off + n_])
        off += n_
    t_gpre, t_gpost, t_gin, t_gout, t_pb, t_psc, t_lru, t_sq = parts
    loss = 0.5 * jnp.sum(t_sq) / d
    t_lru = t_lru.reshape(nmix, 8, e)
    t_lru_loc = lax.dynamic_slice_in_dim(t_lru, me * ec, ec, axis=2)

    def big_update(name, w, m, v):
        shp = big[name].shape
        g, dl, nm, nv = adamw(w.reshape(shp), grads[name], m.reshape(shp), v.reshape(shp))
        return [a.reshape(w.shape) for a in (g, dl, nm, nv)]

    def small_update(w, g, m, v):
        shp = w.shape
        w2 = w.reshape(-1, shp[-1])
        dl, nm, nv = adamw_small(w2, g.reshape(w2.shape), m.reshape(w2.shape), v.reshape(w2.shape))
        return [g.reshape(shp), dl.reshape(shp), nm.reshape(shp), nv.reshape(shp)]

    res = {
        "w_in": big_update("w_in", w_in, m_w_in, v_w_in),
        "w_out": big_update("w_out", w_out, m_w_out, v_w_out),
        "g_pre": small_update(g_pre, t_gpre.reshape(depth, d), m_g_pre, v_g_pre),
        "g_post": small_update(g_post, t_gpost.reshape(depth, d), m_g_post, v_g_post),
        "pool_w": big_update("pool", pool_w, m_pool_w, v_pool_w),
        "pool_b": small_update(pool_b, t_pb, m_pool_b, v_pool_b),
        "pool_scale": small_update(pool_scale, t_psc, m_pool_scale, v_pool_scale),
        "conv_w": small_update(conv_w, t_lru_loc[:, :CONV_W], m_conv_w, v_conv_w),
        "conv_b": small_update(conv_b, t_lru_loc[:, 4], m_conv_b, v_conv_b),
        "lru_wa": big_update("wa", lru_wa, m_lru_wa, v_lru_wa),
        "lru_ba": small_update(lru_ba, t_lru_loc[:, 5], m_lru_ba, v_lru_ba),
        "lru_wx": big_update("wx", lru_wx, m_lru_wx, v_lru_wx),
        "lru_bx": small_update(lru_bx, t_lru_loc[:, 6], m_lru_bx, v_lru_bx),
        "lru_L": small_update(lru_L, t_lru_loc[:, 7], m_lru_L, v_lru_L),
        "w_ple": big_update("ple", w_ple, m_w_ple, v_w_ple),
        "w_ple_gate": big_update("gate", w_ple_gate, m_w_ple_gate, v_w_ple_gate),
        "g_ple_in": small_update(g_ple_in, t_gin.reshape(depth, d), m_g_ple_in, v_g_ple_in),
        "g_ple_out": small_update(g_ple_out, t_gout.reshape(depth, d), m_g_ple_out, v_g_ple_out),
    }
    order = ["w_in", "w_out", "g_pre", "g_post", "pool_w", "pool_b", "pool_scale", "conv_w", "conv_b", "lru_wa", "lru_ba",
             "lru_wx", "lru_bx", "lru_L", "w_ple", "w_ple_gate", "g_ple_in", "g_ple_out"]
    out = [loss, grad_x]
    for slot in range(4):
        out += [res[n][slot] for n in order]
    return tuple(out)
```

```python
import functools

import jax
import jax.numpy as jnp
from jax import lax
from jax.experimental import pallas as pl
from jax.experimental.pallas import tpu as pltpu

F32 = jnp.float32
MXU = jnp.bfloat16
WIRE = jnp.bfloat16
VMEM_LIMIT = 56 * 1024 * 1024
RMS_EPS = 1e-6
LRU_C = 8.0
POOL_WINDOWS = (2, 4, 8, 16)
MAXW = 16
CONV_W = 4
LRU_HEADS = 16
ADAM_LR, ADAM_B1, ADAM_B2, ADAM_EPS, ADAM_WD, ADAM_STEP = 0.001, 0.9, 0.999, 1e-08, 0.01, 10
MESHID = pl.DeviceIdType.MESH
ANY = pl.BlockSpec(memory_space=pl.ANY)


def _cp(sem=None):
    return pltpu.CompilerParams(dimension_semantics=sem, vmem_limit_bytes=VMEM_LIMIT)


def _sig(v):
    return 0.5 * jnp.tanh(0.5 * v) + 0.5


def _tile(n, pref):
    return pref if n % pref == 0 else n


_DN = {"nn": (((1,), (0,)), ((), ())), "nt": (((1,), (1,)), ((), ())), "tn": (((0,), (0,)), ((), ()))}


def _matmul(name, mode, grid, a, a_spec, b, b_spec, out_shape, out_spec, acc_shape, hosted=()):
    nk = grid[2]
    ro = [r for st in hosted for r in st["ro"]]
    bufs = [r for st in hosted for r in st["bufs"]]
    nro, nbuf = len(ro), len(bufs)

    def body(a_ref, b_ref, *rest):
        ro_refs = rest[:nro]
        o_ref = rest[nro + nbuf]
        buf_refs = rest[nro + nbuf + 1:nro + 2 * nbuf + 1]
        acc_ref = rest[nro + 2 * nbuf + 1]
        ids = [pl.program_id(t) for t in range(3)]
        kk = ids[2]

        def run_stages(what):
            r0 = b0 = s0 = 0
            for st in hosted:
                st[what](ro_refs[r0:r0 + len(st["ro"])], buf_refs[b0:b0 + len(st["bufs"])], rest[-2], rest[-1], s0)
                r0, b0, s0 = r0 + len(st["ro"]), b0 + len(st["bufs"]), s0 + st["nsem"]

        if hosted:
            @pl.when((ids[0] == 0) & (ids[1] == 0) & (ids[2] == 0))
            def _():
                run_stages("start")

        bv = b_ref[...]
        if bv.ndim == 3:
            bv = bv.reshape(bv.shape[0] * bv.shape[1], bv.shape[2])
        prod = lax.dot_general(a_ref[...].astype(MXU), bv.astype(MXU), _DN[mode], preferred_element_type=F32)
        if nk == 1:
            o_ref[...] = prod.astype(o_ref.dtype)
        else:
            @pl.when(kk == 0)
            def _():
                acc_ref[...] = prod

            @pl.when(kk > 0)
            def _():
                acc_ref[...] += prod

            @pl.when(kk == nk - 1)
            def _():
                o_ref[...] = acc_ref[...].astype(o_ref.dtype)

        if hosted:
            @pl.when((ids[0] == grid[0] - 1) & (ids[1] == grid[1] - 1) & (ids[2] == grid[2] - 1))
            def _():
                run_stages("finish")

    scratch = [pltpu.VMEM(acc_shape, F32)]
    if not hosted:
        return pl.pallas_call(
            body, name=name, grid=grid, in_specs=[a_spec, b_spec], out_specs=out_spec, out_shape=out_shape,
            scratch_shapes=scratch, compiler_params=_cp(("parallel", "parallel", "arbitrary")),
        )(a, b)
    nsem = sum(st["nsem"] for st in hosted)
    res = pl.pallas_call(
        body, name=name, grid=grid,
        in_specs=[a_spec, b_spec] + [ANY] * (nro + nbuf), out_specs=[out_spec] + [ANY] * nbuf,
        out_shape=[out_shape] + [jax.ShapeDtypeStruct(x.shape, x.dtype) for x in bufs],
        scratch_shapes=scratch + [pltpu.SemaphoreType.DMA((nsem,)), pltpu.SemaphoreType.DMA((nsem,))],
        input_output_aliases={2 + nro + t: 1 + t for t in range(nbuf)},
        compiler_params=_cp(("arbitrary", "arbitrary", "arbitrary")),
    )(a, b, *ro, *bufs)
    return res[0], list(res[1:])


def _rows_call(name, body, ins, in_rows, outs, out_rows, n_rows, tr):
    def spec(shape, tiled):
        if tiled:
            return pl.BlockSpec((tr, shape[1]), lambda i: (i, 0))
        return pl.BlockSpec(shape, lambda i: (0, 0))

    return pl.pallas_call(
        body, name=name, grid=(n_rows // tr,),
        in_specs=[spec(a.shape, t) for a, t in zip(ins, in_rows)],
        out_specs=[spec(o.shape, t) for o, t in zip(outs, out_rows)],
        out_shape=outs, compiler_params=_cp(("arbitrary",)),
    )(*ins)


def _rstd(v):
    return lax.rsqrt(jnp.mean(v * v, axis=-1, keepdims=True) + RMS_EPS)


def _norm_bwd(v, g, dy):
    r = _rstd(v)
    n = v * r
    dn = dy * g
    dv = r * (dn - n * jnp.mean(dn * n, axis=-1, keepdims=True))
    return dv, jnp.sum(dy * n, axis=0, keepdims=True)


def _acc_rows(ref, val):
    @pl.when(pl.program_id(0) == 0)
    def _():
        ref[...] = val

    @pl.when(pl.program_id(0) > 0)
    def _():
        ref[...] += val


def rms_fwd(x, g, tr):
    def body(x_ref, g_ref, o_ref):
        v = x_ref[...]
        o_ref[...] = (v * _rstd(v) * g_ref[...]).astype(o_ref.dtype)

    return _rows_call("rms_fwd", body, [x, g], [True, False], [jax.ShapeDtypeStruct(x.shape, MXU)], [True], x.shape[0], tr)[0]


def res_rms_fwd(x, o, g, tr):
    def body(x_ref, o_ref, g_ref, y_ref):
        v = o_ref[...]
        y_ref[...] = x_ref[...] + v * _rstd(v) * g_ref[...]

    return _rows_call("res_rms_fwd", body, [x, o, g], [True, True, False], [jax.ShapeDtypeStruct(x.shape, F32)], [True], x.shape[0], tr)[0]


def ple_fwd(x1, gpre, e, g, tr):
    def body(x_ref, gp_ref, e_ref, g_ref, y_ref):
        v = e_ref[...] * _sig(gp_ref[...])
        y_ref[...] = x_ref[...] + v * _rstd(v) * g_ref[...]

    return _rows_call("ple_fwd", body, [x1, gpre, e, g], [True, True, True, False], [jax.ShapeDtypeStruct(x1.shape, F32)], [True], x1.shape[0], tr)[0]


def loss_bwd(y, target, tr):
    d = y.shape[1]

    def body(y_ref, t_ref, dy_ref, sq_ref):
        diff = y_ref[...] - t_ref[...]
        dy_ref[...] = diff * (1.0 / d)
        _acc_rows(sq_ref, jnp.sum(diff * diff, axis=0, keepdims=True))

    return _rows_call("loss_bwd", body, [y, target], [True, True],
                      [jax.ShapeDtypeStruct(y.shape, F32), jax.ShapeDtypeStruct((1, d), F32)], [True, False], y.shape[0], tr)


def ple_bwd(dx2, gpre, e, g, tr):
    d = dx2.shape[1]

    def body(dx_ref, gp_ref, e_ref, g_ref, de_ref, dgp_ref, dg_ref):
        gate = _sig(gp_ref[...])
        ev = e_ref[...]
        dv, dg = _norm_bwd(ev * gate, g_ref[...], dx_ref[...])
        de_ref[...] = (dv * gate).astype(de_ref.dtype)
        dgp_ref[...] = (dv * ev * gate * (1.0 - gate)).astype(dgp_ref.dtype)
        _acc_rows(dg_ref, dg)

    return _rows_call("ple_bwd", body, [dx2, gpre, e, g], [True, True, True, False],
                      [jax.ShapeDtypeStruct(dx2.shape, MXU), jax.ShapeDtypeStruct(dx2.shape, MXU), jax.ShapeDtypeStruct((1, d), F32)],
                      [True, True, False], dx2.shape[0], tr)


def rms_bwd_res(dres, dh, x, g, tr):
    d = x.shape[1]

    def body(dr_ref, dh_ref, x_ref, g_ref, dx_ref, dg_ref):
        dv, dg = _norm_bwd(x_ref[...], g_ref[...], dh_ref[...])
        dx_ref[...] = dr_ref[...] + dv
        _acc_rows(dg_ref, dg)

    return _rows_call("rms_bwd_res", body, [dres, dh, x, g], [True, True, True, False],
                      [jax.ShapeDtypeStruct(x.shape, F32), jax.ShapeDtypeStruct((1, d), F32)], [True, False], x.shape[0], tr)


def rms_bwd(dy, o, g, tr):
    d = o.shape[1]

    def body(dy_ref, o_ref, g_ref, do_ref, dg_ref):
        dv, dg = _norm_bwd(o_ref[...], g_ref[...], dy_ref[...])
        do_ref[...] = dv.astype(do_ref.dtype)
        _acc_rows(dg_ref, dg)

    return _rows_call("rms_bwd", body, [dy, o, g], [True, True, False],
                      [jax.ShapeDtypeStruct(o.shape, MXU), jax.ShapeDtypeStruct((1, d), F32)], [True, False], o.shape[0], tr)


def _trailing_sum(ext, w):
    s, k = ext, 1
    while k < w:
        s = s + pltpu.roll(s, k, 0)
        k *= 2
    return s


def _leading_sum(ext, w):
    n = ext.shape[0]
    s, k = ext, 1
    while k < w:
        s = s + pltpu.roll(s, n - k, 0)
        k *= 2
    return s


def _pool_inv_count(rb, tr, w, c):
    t = rb * tr + lax.broadcasted_iota(jnp.int32, (tr, c), 0)
    return 1.0 / jnp.minimum(t + 1, w).astype(F32)


def _pool_d(u_ref, up_ref, rb, tr, w):
    cur = u_ref[...].astype(F32)
    prev = jnp.where(rb > 0, up_ref[...].astype(F32), 0.0)
    ext = jnp.concatenate([prev, cur], axis=0)
    win = _trailing_sum(ext, w)[MAXW:]
    return win * _pool_inv_count(rb, tr, w, cur.shape[1]) - cur


def pool_fwd(uz, w_g, bias, scale, tr):
    _, s, e = uz.shape
    ng = len(POOL_WINDOWS)
    cg = e // ng
    nb = s // tr
    hb = tr // MAXW

    def body(u_ref, up_ref, z_ref, w_ref, b_ref, sc_ref, o_ref):
        g, rb = pl.program_id(0), pl.program_id(1)
        wmat = w_ref[...].reshape(cg, cg)
        for gg, win in enumerate(POOL_WINDOWS):
            @pl.when(g == gg)
            def _(win=win):
                d = _pool_d(u_ref, up_ref, rb, tr, win)
                y = (jnp.dot(d.astype(MXU), wmat, preferred_element_type=F32) + b_ref[...]) * sc_ref[...]
                z = z_ref[...].astype(F32)
                o_ref[...] = (y * z * _sig(z)).astype(o_ref.dtype)

    return pl.pallas_call(
        body, name="pool_fwd", grid=(ng, nb),
        in_specs=[
            pl.BlockSpec((None, tr, cg), lambda g, r: (0, r, g)),
            pl.BlockSpec((None, MAXW, cg), lambda g, r: (0, jnp.maximum(r * hb - 1, 0), g)),
            pl.BlockSpec((None, tr, cg), lambda g, r: (1, r, g)),
            pl.BlockSpec((4, None, cg // 4, cg), lambda g, r: (0, g, 0, 0)),
            pl.BlockSpec((1, cg), lambda g, r: (0, g)),
            pl.BlockSpec((1, cg), lambda g, r: (0, g)),
        ],
        out_specs=pl.BlockSpec((tr, cg), lambda g, r: (r, g)),
        out_shape=jax.ShapeDtypeStruct((s, e), MXU),
        compiler_params=_cp(("parallel", "arbitrary")),
    )(uz, uz, uz, w_g, bias, scale)


def pool_bwd(uz, dy2, w_g, bias, scale, tr):
    _, s, e = uz.shape
    ng = len(POOL_WINDOWS)
    cg = e // ng
    nb = s // tr
    hb = tr // MAXW

    def body(u_ref, up_ref, z_ref, dy_ref, w_ref, b_ref, sc_ref, duz_ref, gw_ref, db_ref, dsc_ref, acc_ref, carry_ref):
        g, step = pl.program_id(0), pl.program_id(1)
        rb = nb - 1 - step
        wmat = w_ref[...].reshape(cg, cg)
        for gg, win in enumerate(POOL_WINDOWS):
            @pl.when(g == gg)
            def _(win=win):
                d = _pool_d(u_ref, up_ref, rb, tr, win).astype(MXU)
                ypre = jnp.dot(d, wmat, preferred_element_type=F32) + b_ref[...]
                z = z_ref[...].astype(F32)
                sg = _sig(z)
                dy2v = dy_ref[...].astype(F32)
                dyv = dy2v * z * sg
                duz_ref[1] = (dy2v * ypre * sc_ref[...] * sg * (1.0 + z * (1.0 - sg))).astype(duz_ref.dtype)
                dypre = dyv * sc_ref[...]
                dsc = jnp.sum(dyv * ypre, axis=0, keepdims=True)
                dbv = jnp.sum(dypre, axis=0, keepdims=True)
                dypre_b = dypre.astype(MXU)
                dd = lax.dot_general(dypre_b, wmat, _DN["nt"], preferred_element_type=F32)
                gw = lax.dot_general(d, dypre_b, _DN["tn"], preferred_element_type=F32)
                q = dd * _pool_inv_count(rb, tr, win, cg)
                nxt = jnp.where(step > 0, carry_ref[...], 0.0)
                lead = _leading_sum(jnp.concatenate([q, nxt], axis=0), win)[:tr]
                duz_ref[0] = (lead - dd).astype(duz_ref.dtype)
                carry_ref[...] = q[:MAXW]

                @pl.when(step == 0)
                def _():
                    acc_ref[...] = gw
                    db_ref[...] = dbv
                    dsc_ref[...] = dsc

                @pl.when(step > 0)
                def _():
                    acc_ref[...] += gw
                    db_ref[...] += dbv
                    dsc_ref[...] += dsc

                @pl.when(step == nb - 1)
                def _():
                    gw_ref[...] = acc_ref[...].reshape(4, cg // 4, cg).astype(gw_ref.dtype)

    return pl.pallas_call(
        body, name="pool_bwd", grid=(ng, nb),
        in_specs=[
            pl.BlockSpec((None, tr, cg), lambda g, r: (0, nb - 1 - r, g)),
            pl.BlockSpec((None, MAXW, cg), lambda g, r: (0, jnp.maximum((nb - 1 - r) * hb - 1, 0), g)),
            pl.BlockSpec((None, tr, cg), lambda g, r: (1, nb - 1 - r, g)),
            pl.BlockSpec((tr, cg), lambda g, r: (nb - 1 - r, g)),
            pl.BlockSpec((4, None, cg // 4, cg), lambda g, r: (0, g, 0, 0)),
            pl.BlockSpec((1, cg), lambda g, r: (0, g)),
            pl.BlockSpec((1, cg), lambda g, r: (0, g)),
        ],
        out_specs=[
            pl.BlockSpec((2, tr, cg), lambda g, r: (0, nb - 1 - r, g)),
            pl.BlockSpec((4, None, cg // 4, cg), lambda g, r: (0, g, 0, 0)),
            pl.BlockSpec((1, cg), lambda g, r: (0, g)),
            pl.BlockSpec((1, cg), lambda g, r: (0, g)),
        ],
        out_shape=[
            jax.ShapeDtypeStruct((2, s, e), MXU),
            jax.ShapeDtypeStruct(w_g.shape, WIRE),
            jax.ShapeDtypeStruct((1, e), F32),
            jax.ShapeDtypeStruct((1, e), F32),
        ],
        scratch_shapes=[pltpu.VMEM((cg, cg), F32), pltpu.VMEM((MAXW, cg), F32)],
        compiler_params=_cp(("parallel", "arbitrary")),
    )(uz, uz, uz, dy2, w_g, bias, scale)


HALO = 16


def _one_minus_sq(log_a, a):
    poly = (-2.0 * log_a) * (1.0 + log_a * (1.0 + log_a * (2.0 / 3.0)))
    return jnp.where(log_a > -0.01, poly, 1.0 - a * a)


def _softplus_neg(lam):
    t = jnp.exp(-jnp.abs(lam))
    log1p = jnp.where(t < 1e-3, t * (1.0 - t * (0.5 - t * (1.0 / 3.0))), jnp.log(1.0 + t))
    return jnp.maximum(-lam, 0.0) + log1p, _sig(-lam)


def _lru_gates(u_ref, up_ref, rb, sm_ref, wa, wx):
    cur = u_ref[...].astype(F32)
    prev = jnp.where(rb > 0, up_ref[...].astype(F32), 0.0)
    ext = jnp.concatenate([prev, cur], axis=0)
    taps = [cur] + [pltpu.roll(ext, k, 0)[HALO:] for k in range(1, CONV_W)]
    uc = sm_ref[CONV_W:CONV_W + 1, :]
    for k in range(CONV_W):
        uc = uc + taps[k] * sm_ref[CONV_W - 1 - k:CONV_W - k, :]
    ucb = uc.astype(MXU)
    r = _sig(jnp.dot(ucb, wa, preferred_element_type=F32) + sm_ref[5:6, :])
    ig = _sig(jnp.dot(ucb, wx, preferred_element_type=F32) + sm_ref[6:7, :])
    sp, sgn = _softplus_neg(sm_ref[7:8, :])
    log_a = r * (-LRU_C * sp)
    a = jnp.exp(log_a)
    mult = jnp.sqrt(jnp.maximum(_one_minus_sq(log_a, a), 0.0))
    return taps, uc, ucb, r, ig, sp, sgn, a, mult


LANES = 128


def _seg_scan(a, b, out_ref, scr, state, reverse):
    a_s, b_s, h_s, p_s = scr
    tr, c = a.shape
    seg = tr // 8
    nl = c // LANES
    for l in range(nl):
        a_s[l] = a[:, l * LANES:(l + 1) * LANES]
        b_s[l] = b[:, l * LANES:(l + 1) * LANES]
    h = [jnp.zeros((8, LANES), F32)] * nl
    pp = [jnp.ones((8, LANES), F32)] * nl
    for i in (range(seg - 1, -1, -1) if reverse else range(seg)):
        rows = pl.ds(i, 8, stride=seg)
        for l in range(nl):
            av = a_s[l, rows, :]
            h[l] = av * h[l] + b_s[l, rows, :]
            pp[l] = av * pp[l]
            h_s[l, pl.ds(8 * i, 8), :] = h[l]
            p_s[l, pl.ds(8 * i, 8), :] = pp[l]
    leaving = []
    for l in range(nl):
        lanes = slice(l * LANES, (l + 1) * LANES)
        st = state[:, lanes]
        for sgm in (range(7, -1, -1) if reverse else range(8)):
            for t0 in range(0, seg, 8):
                rows = pl.ds(8 * t0 + sgm, 8, stride=8)
                out_ref[pl.ds(sgm * seg + t0, 8), lanes] = h_s[l, rows, :] + p_s[l, rows, :] * st
            st = h[l][sgm:sgm + 1, :] + pp[l][sgm:sgm + 1, :] * st
        leaving.append(st)
    return jnp.concatenate(leaving, axis=1)


def lru_fwd(uz, wa_g, wx_g, small, tr):
    _, s, e = uz.shape
    cb = e // LRU_HEADS
    nb = s // tr
    hb = tr // HALO

    def body(u_ref, up_ref, z_ref, wa_ref, wx_ref, sm_ref, o_ref, h_ref, s0, s1, s2, s3, carry_ref):
        rb = pl.program_id(1)
        wa = wa_ref[...].reshape(cb, cb)
        wx = wx_ref[...].reshape(cb, cb)
        _, uc, _, _, ig, _, _, a, mult = _lru_gates(u_ref, up_ref, rb, sm_ref, wa, wx)
        start = jnp.where(rb > 0, carry_ref[0:1, :], 0.0)
        last = _seg_scan(a, mult * ig * uc, h_ref, (s0, s1, s2, s3), start, False)
        carry_ref[...] = jnp.broadcast_to(last, carry_ref.shape)
        z = z_ref[...].astype(F32)
        o_ref[...] = (h_ref[...] * z * _sig(z)).astype(o_ref.dtype)

    wspec = pl.BlockSpec((4, None, cb // 4, cb), lambda h, r: (0, h, 0, 0))
    return pl.pallas_call(
        body, name="lru_fwd", grid=(LRU_HEADS, nb),
        in_specs=[
            pl.BlockSpec((None, tr, cb), lambda h, r: (0, r, h)),
            pl.BlockSpec((None, HALO, cb), lambda h, r: (0, jnp.maximum(r * hb - 1, 0), h)),
            pl.BlockSpec((None, tr, cb), lambda h, r: (1, r, h)),
            wspec, wspec,
            pl.BlockSpec((8, cb), lambda h, r: (0, h)),
        ],
        out_specs=[pl.BlockSpec((tr, cb), lambda h, r: (r, h)), pl.BlockSpec((tr, cb), lambda h, r: (r, h))],
        out_shape=[jax.ShapeDtypeStruct((s, e), MXU), jax.ShapeDtypeStruct((s, e), F32)],
        scratch_shapes=[pltpu.VMEM((cb // LANES, tr, LANES), F32)] * 4 + [pltpu.VMEM((8, cb), F32)],
        compiler_params=_cp(("parallel", "arbitrary")),
    )(uz, uz, uz, wa_g, wx_g, small)


def lru_bwd(uz, hst, dy2, wa_g, wx_g, small, tr):
    _, s, e = uz.shape
    cb = e // LRU_HEADS
    nb = s // tr
    hb = tr // HALO

    def body(u_ref, up_ref, z_ref, h_ref, hp_ref, dy_ref, wa_ref, wx_ref, sm_ref,
             duz_ref, gwa_ref, gwx_ref, dsm_ref, s0, s1, s2, s3, g_s, acc_a, acc_x, gcar, acar, dcar):
        step = pl.program_id(1)
        rb = nb - 1 - step
        wa = wa_ref[...].reshape(cb, cb)
        wx = wx_ref[...].reshape(cb, cb)
        taps, uc, ucb, r, ig, sp, sgn, a, mult = _lru_gates(u_ref, up_ref, rb, sm_ref, wa, wx)
        row = lax.broadcasted_iota(jnp.int32, a.shape, 0)
        z = z_ref[...].astype(F32)
        sg = _sig(z)
        dy2v = dy_ref[...].astype(F32)
        hv = h_ref[...]
        duz_ref[1] = (dy2v * hv * sg * (1.0 + z * (1.0 - sg))).astype(duz_ref.dtype)
        a_next = jnp.where(row == tr - 1, jnp.where(step > 0, acar[0:1, :], 0.0), pltpu.roll(a, tr - 1, 0))
        g_first = _seg_scan(a_next, dy2v * z * sg, g_s, (s0, s1, s2, s3), jnp.where(step > 0, gcar[0:1, :], 0.0), True)
        gcar[...] = jnp.broadcast_to(g_first, gcar.shape)
        acar[...] = jnp.broadcast_to(a[0:1, :], acar.shape)
        gv = g_s[...]
        h_before = jnp.where(rb > 0, hp_ref[HALO - 1:HALO, :], 0.0)
        h_prev = jnp.where(row == 0, h_before, pltpu.roll(hv, 1, 0))
        da = gv * h_prev
        gu = gv * uc
        dmult = gu * ig
        dig = gu * mult
        dlog_a = da * a - dmult * jnp.where(mult > 0.0, a * a / mult, 0.0)
        dra = dlog_a * (-LRU_C) * sp * r * (1.0 - r)
        dix = dig * ig * (1.0 - ig)
        dl = jnp.sum(dlog_a * r, axis=0, keepdims=True) * (LRU_C * sgn)
        dra_b, dix_b = dra.astype(MXU), dix.astype(MXU)
        duc = (gv * mult * ig + lax.dot_general(dra_b, wa, _DN["nt"], preferred_element_type=F32)
               + lax.dot_general(dix_b, wx, _DN["nt"], preferred_element_type=F32))
        gwa = lax.dot_general(ucb, dra_b, _DN["tn"], preferred_element_type=F32)
        gwx = lax.dot_general(ucb, dix_b, _DN["tn"], preferred_element_type=F32)
        ext = jnp.concatenate([duc, jnp.where(step > 0, dcar[...], 0.0)], axis=0)
        n = ext.shape[0]
        du = duc * sm_ref[CONV_W - 1:CONV_W, :]
        for k in range(1, CONV_W):
            du = du + pltpu.roll(ext, n - k, 0)[:tr] * sm_ref[CONV_W - 1 - k:CONV_W - k, :]
        duz_ref[0] = du.astype(duz_ref.dtype)
        dcar[...] = duc[:HALO]
        rows = [jnp.sum(duc * taps[CONV_W - 1 - k], axis=0, keepdims=True) for k in range(CONV_W)]
        rows += [jnp.sum(duc, axis=0, keepdims=True), jnp.sum(dra, axis=0, keepdims=True),
                 jnp.sum(dix, axis=0, keepdims=True), dl]

        @pl.when(step == 0)
        def _():
            acc_a[...] = gwa
            acc_x[...] = gwx
            for k, rv in enumerate(rows):
                dsm_ref[k:k + 1, :] = rv

        @pl.when(step > 0)
        def _():
            acc_a[...] += gwa
            acc_x[...] += gwx
            for k, rv in enumerate(rows):
                dsm_ref[k:k + 1, :] += rv

        @pl.when(step == nb - 1)
        def _():
            gwa_ref[...] = acc_a[...].reshape(4, cb // 4, cb).astype(gwa_ref.dtype)
            gwx_ref[...] = acc_x[...].reshape(4, cb // 4, cb).astype(gwx_ref.dtype)

    wspec = pl.BlockSpec((4, None, cb // 4, cb), lambda h, r: (0, h, 0, 0))
    blk = pl.BlockSpec((tr, cb), lambda h, r: (nb - 1 - r, h))
    return pl.pallas_call(
        body, name="lru_bwd", grid=(LRU_HEADS, nb),
        in_specs=[
            pl.BlockSpec((None, tr, cb), lambda h, r: (0, nb - 1 - r, h)),
            pl.BlockSpec((None, HALO, cb), lambda h, r: (0, jnp.maximum((nb - 1 - r) * hb - 1, 0), h)),
            pl.BlockSpec((None, tr, cb), lambda h, r: (1, nb - 1 - r, h)),
            blk,
            pl.BlockSpec((HALO, cb), lambda h, r: (jnp.maximum((nb - 1 - r) * hb - 1, 0), h)),
            blk,
            wspec, wspec,
            pl.BlockSpec((8, cb), lambda h, r: (0, h)),
        ],
        out_specs=[
            pl.BlockSpec((2, tr, cb), lambda h, r: (0, nb - 1 - r, h)),
            wspec, wspec,
            pl.BlockSpec((8, cb), lambda h, r: (0, h)),
        ],
        out_shape=[
            jax.ShapeDtypeStruct((2, s, e), MXU),
            jax.ShapeDtypeStruct(wa_g.shape, WIRE),
            jax.ShapeDtypeStruct(wx_g.shape, WIRE),
            jax.ShapeDtypeStruct((8, e), F32),
        ],
        scratch_shapes=([pltpu.VMEM((cb // LANES, tr, LANES), F32)] * 4 + [pltpu.VMEM((tr, cb), F32)]
                        + [pltpu.VMEM((cb, cb), F32)] * 2 + [pltpu.VMEM((8, cb), F32)] * 2 + [pltpu.VMEM((HALO, cb), F32)]),
        compiler_params=_cp(("parallel", "arbitrary")),
    )(uz, uz, uz, hst, hst, dy2, wa_g, wx_g, small)


def _place():
    x, y, c = lax.axis_index("x"), lax.axis_index("y"), lax.axis_index("c")
    chips = [(1 - x, y), (x, 1 - y), (1 - x, 1 - y)]
    return x, y, c, chips


def _rcopy(src, dst, send_sems, recv_sems, k, to):
    return pltpu.make_async_remote_copy(src_ref=src, dst_ref=dst, send_sem=send_sems.at[k], recv_sem=recv_sems.at[k],
                                        device_id=to, device_id_type=MESHID)


def _stage_gather_ici(bufs):
    n = len(bufs)

    def copies(refs, ss, rs, off, own):
        x, y, c, chips = _place()
        out = []
        for a in range(n):
            for jj, ch in enumerate(chips):
                blk = refs[a].at[2 * x + y if own else 2 * ch[0] + ch[1], :, c]
                out.append(_rcopy(blk, blk, ss, rs, off + 3 * a + jj, (*ch, c)))
        return out

    def start(ro, refs, ss, rs, off):
        for cp in copies(refs, ss, rs, off, True):
            cp.start()

    def finish(ro, refs, ss, rs, off):
        for cp in copies(refs, ss, rs, off, False):
            cp.wait_recv()
        for cp in copies(refs, ss, rs, off, True):
            cp.wait_send()

    return dict(ro=[], bufs=list(bufs), nsem=3 * n, start=start, finish=finish)


def _stage_gather_d2d(bufs):
    n = len(bufs)

    def copies(refs, ss, rs, off, sending):
        x, y, c, chips = _place()
        out = []
        for a in range(n):
            for jj, ch in enumerate(chips):
                blk = refs[a].at[2 * ch[0] + ch[1], :, c if sending else 1 - c]
                out.append(_rcopy(blk, blk, ss, rs, off + 3 * a + jj, (x, y, 1 - c)))
        return out

    def start(ro, refs, ss, rs, off):
        for cp in copies(refs, ss, rs, off, True):
            cp.start()

    def finish(ro, refs, ss, rs, off):
        for cp in copies(refs, ss, rs, off, False):
            cp.wait_recv()
        for cp in copies(refs, ss, rs, off, True):
            cp.wait_send()

    return dict(ro=[], bufs=list(bufs), nsem=3 * n, start=start, finish=finish)


def _stage_scatter_ici(parts, gots):
    n = len(parts)

    def copies(ro, refs, ss, rs, off):
        x, y, c, chips = _place()
        return [_rcopy(ro[a].at[2 * ch[0] + ch[1]], refs[a].at[jj], ss, rs, off + 3 * a + jj, (*ch, c))
                for a in range(n) for jj, ch in enumerate(chips)]

    def start(ro, refs, ss, rs, off):
        for cp in copies(ro, refs, ss, rs, off):
            cp.start()

    def finish(ro, refs, ss, rs, off):
        for cp in copies(ro, refs, ss, rs, off):
            cp.wait()

    return dict(ro=list(parts), bufs=list(gots), nsem=3 * n, start=start, finish=finish)


def _stage_send_half(grads, lands):
    n = len(grads)

    def copies(ro, refs, ss, rs, off):
        x, y, c, _ = _place()
        return [_rcopy(ro[a].at[:, :, 1 - c], refs[a], ss, rs, off + a, (x, y, 1 - c)) for a in range(n)]

    def start(ro, refs, ss, rs, off):
        for cp in copies(ro, refs, ss, rs, off):
            cp.start()

    def finish(ro, refs, ss, rs, off):
        for cp in copies(ro, refs, ss, rs, off):
            cp.wait()

    return dict(ro=list(grads), bufs=list(lands), nsem=n, start=start, finish=finish)


def cast_into_slab(w, k_idx):
    l, r, c = w.shape
    rh = r // 2
    tr = _block_rows(rh, c, 4, 4 << 20)
    nbh = rh // tr

    def body(k_ref, w_ref, o_ref):
        o_ref[...] = w_ref[...].astype(o_ref.dtype)

    return pl.pallas_call(
        body, name="cast_into_slab",
        grid_spec=pltpu.PrefetchScalarGridSpec(
            num_scalar_prefetch=1, grid=(l, 2, nbh),
            in_specs=[pl.BlockSpec((None, tr, c), lambda i, h, b, k_ref: (i, h * nbh + b, 0))],
            out_specs=pl.BlockSpec((None, None, None, tr, c), lambda i, h, b, k_ref: (k_ref[0], i, h, b, 0))),
        out_shape=jax.ShapeDtypeStruct((4, l, 2, rh, c), WIRE),
        compiler_params=_cp(("parallel", "parallel", "parallel")),
    )(k_idx, w)


def gather_weights(bufs):
    n = len(bufs)

    def body(*refs):
        outs = refs[n:2 * n]
        send_sems, recv_sems = refs[2 * n:]
        x, y, c, chips = _place()
        me = 2 * x + y
        sib = (x, y, 1 - c)
        sends = []
        for a in range(n):
            for jj, ch in enumerate(chips):
                mine = outs[a].at[me, :, c]
                cp = _rcopy(mine, mine, send_sems, recv_sems, 6 * a + jj, (*ch, c))
                cp.start()
                sends.append(cp)
        for a in range(n):
            for jj, ch in enumerate(chips):
                blk = outs[a].at[2 * ch[0] + ch[1], :, c]
                _rcopy(blk, blk, send_sems, recv_sems, 6 * a + jj, (*ch, c)).wait_recv()
                fw = _rcopy(blk, blk, send_sems, recv_sems, 6 * a + 3 + jj, sib)
                fw.start()
                sends.append(fw)
        for a in range(n):
            for jj, ch in enumerate(chips):
                blk = outs[a].at[2 * ch[0] + ch[1], :, 1 - c]
                _rcopy(blk, blk, send_sems, recv_sems, 6 * a + 3 + jj, sib).wait_recv()
        for cp in sends:
            cp.wait_send()

    return pl.pallas_call(
        body, name="gather_weights",
        in_specs=[ANY] * n, out_specs=[ANY] * n,
        out_shape=[jax.ShapeDtypeStruct(a.shape, a.dtype) for a in bufs],
        scratch_shapes=[pltpu.SemaphoreType.DMA((6 * n,)), pltpu.SemaphoreType.DMA((6 * n,))],
        input_output_aliases={a: a for a in range(n)},
        compiler_params=pltpu.CompilerParams(has_side_effects=True),
    )(*bufs)


def all_gather_small(v, name):
    m_per, n = v.shape

    def body(x_ref, out_ref, send_sems, recv_sems, local_sem):
        x, y, c, chips = _place()
        me, sibling = (x, y, c), (x, y, 1 - c)

        def rows(px, py, pc):
            return out_ref.at[pl.ds((4 * px + 2 * py + pc) * m_per, m_per), :]

        def copy(k, block, to, src=None):
            return _rcopy(rows(*block) if src is None else src, rows(*block), send_sems, recv_sems, k, to)

        mine = pltpu.make_async_copy(x_ref, rows(*me), local_sem)
        mine.start()
        first = [copy(0, me, sibling, src=x_ref)]
        first += [copy(1 + jj, me, (*chip, c), src=x_ref) for jj, chip in enumerate(chips)]
        for cp in first:
            cp.start()
        passed = [copy(4 + jj, (*chip, c), sibling) for jj, chip in enumerate(chips)]
        for jj, chip in enumerate(chips):
            copy(1 + jj, (*chip, c), me).wait_recv()
            passed[jj].start()
        copy(0, sibling, me).wait_recv()
        for jj, chip in enumerate(chips):
            copy(4 + jj, (*chip, 1 - c), me).wait_recv()
        for cp in first + passed:
            cp.wait_send()
        mine.wait()

    return pl.pallas_call(
        body, name=name,
        out_shape=jax.ShapeDtypeStruct((8 * m_per, n), v.dtype),
        in_specs=[pl.BlockSpec(memory_space=pltpu.VMEM)],
        out_specs=pl.BlockSpec(memory_space=pltpu.VMEM),
        scratch_shapes=[pltpu.SemaphoreType.DMA((7,)), pltpu.SemaphoreType.DMA((7,)), pltpu.SemaphoreType.DMA],
        compiler_params=pltpu.CompilerParams(vmem_limit_bytes=VMEM_LIMIT),
    )(v)


def sum_devices(g):
    def body(g_ref, o_ref):
        acc = g_ref[0]
        for d in range(1, 8):
            acc = acc + g_ref[d]
        o_ref[...] = acc

    return pl.pallas_call(body, name="sum_devices", out_shape=jax.ShapeDtypeStruct(g.shape[1:], g.dtype),
                          compiler_params=pltpu.CompilerParams(vmem_limit_bytes=VMEM_LIMIT))(g)


def send_other_half(grads):
    n = len(grads)

    def body(*refs):
        ins, outs = refs[:n], refs[n:2 * n]
        send_sems, recv_sems = refs[2 * n:]
        x, y, c, _ = _place()
        sib = (x, y, 1 - c)
        cps = [_rcopy(ins[a].at[:, :, 1 - c], outs[a], send_sems, recv_sems, a, sib) for a in range(n)]
        for cp in cps:
            cp.start()
        for cp in cps:
            cp.wait()

    return pl.pallas_call(
        body, name="send_other_half", in_specs=[ANY] * n, out_specs=[ANY] * n,
        out_shape=[jax.ShapeDtypeStruct(g.shape[:2] + g.shape[3:], g.dtype) for g in grads],
        scratch_shapes=[pltpu.SemaphoreType.DMA((n,)), pltpu.SemaphoreType.DMA((n,))],
        compiler_params=pltpu.CompilerParams(has_side_effects=True),
    )(*grads)


def scatter_to_chips(parts):
    n = len(parts)

    def body(*refs):
        ins, outs = refs[:n], refs[n:2 * n]
        send_sems, recv_sems = refs[2 * n:]
        x, y, c, chips = _place()
        cps = []
        for a in range(n):
            for jj, ch in enumerate(chips):
                cps.append(_rcopy(ins[a].at[2 * ch[0] + ch[1]], outs[a].at[jj], send_sems, recv_sems, 3 * a + jj, (*ch, c)))
        for cp in cps:
            cp.start()
        for cp in cps:
            cp.wait()

    return pl.pallas_call(
        body, name="scatter_to_chips", in_specs=[ANY] * n, out_specs=[ANY] * n,
        out_shape=[jax.ShapeDtypeStruct((3,) + p.shape[1:], p.dtype) for p in parts],
        scratch_shapes=[pltpu.SemaphoreType.DMA((3 * n,)), pltpu.SemaphoreType.DMA((3 * n,))],
        compiler_params=pltpu.CompilerParams(has_side_effects=True),
    )(*parts)


def share_halves(bufs, spans):
    n = len(bufs)

    def body(*refs):
        outs = refs[n:2 * n]
        send_sems, recv_sems = refs[2 * n:]
        x, y, c, _ = _place()
        sib = (x, y, 1 - c)

        def blk(a, half):
            return outs[a].at[pl.ds(spans[a][0], spans[a][1]), half]

        cps = [_rcopy(blk(a, c), blk(a, c), send_sems, recv_sems, a, sib) for a in range(n)]
        for cp in cps:
            cp.start()
        for a in range(n):
            _rcopy(blk(a, 1 - c), blk(a, 1 - c), send_sems, recv_sems, a, sib).wait_recv()
        for cp in cps:
            cp.wait_send()

    return pl.pallas_call(
        body, name="share_halves", in_specs=[ANY] * n, out_specs=[ANY] * n,
        out_shape=[jax.ShapeDtypeStruct(b.shape, b.dtype) for b in bufs],
        scratch_shapes=[pltpu.SemaphoreType.DMA((n,)), pltpu.SemaphoreType.DMA((n,))],
        input_output_aliases={a: a for a in range(n)},
        compiler_params=pltpu.CompilerParams(has_side_effects=True),
    )(*bufs)


def _block_rows(r, c, itemsize, budget=1 << 20):
    tr = r
    while tr * c * itemsize > budget and tr % 16 == 0:
        tr //= 2
    return tr


def add_halves(g, got, c_idx):
    k4, l, _, rh, cc = g.shape
    tr = _block_rows(rh, cc, 4, 4 << 20)

    def body(c_ref, g_ref, r_ref, o_ref):
        o_ref[...] = (g_ref[...].astype(F32) + r_ref[...].astype(F32)).astype(o_ref.dtype)

    return pl.pallas_call(
        body, name="add_halves",
        grid_spec=pltpu.PrefetchScalarGridSpec(
            num_scalar_prefetch=1, grid=(k4, l, rh // tr),
            in_specs=[pl.BlockSpec((None, None, None, tr, cc), lambda k, i, b, c_ref: (k, i, c_ref[0], b, 0)),
                      pl.BlockSpec((None, None, tr, cc), lambda k, i, b, c_ref: (k, i, b, 0))],
            out_specs=pl.BlockSpec((None, None, tr, cc), lambda k, i, b, c_ref: (k, i, b, 0))),
        out_shape=jax.ShapeDtypeStruct(got.shape, WIRE),
        compiler_params=_cp(("parallel", "parallel", "parallel")),
    )(c_idx, g, got)


def sum_chips(own, got, kc_idx, full, first):
    _, l, rh, cc = own.shape
    tr = _block_rows(rh, cc, 4, 4 << 20)

    def body(k_ref, o_ref, r_ref, _full, s_ref):
        s_ref[...] = ((o_ref[...].astype(F32) + r_ref[0].astype(F32)) + r_ref[1].astype(F32)) + r_ref[2].astype(F32)

    return pl.pallas_call(
        body, name="sum_chips",
        grid_spec=pltpu.PrefetchScalarGridSpec(
            num_scalar_prefetch=1, grid=(l, rh // tr),
            in_specs=[pl.BlockSpec((None, None, tr, cc), lambda i, b, k_ref: (k_ref[0], i, b, 0)),
                      pl.BlockSpec((3, None, tr, cc), lambda i, b, k_ref: (0, i, b, 0)),
                      ANY],
            out_specs=pl.BlockSpec((None, None, tr, cc), lambda i, b, k_ref: (first + i, k_ref[1], b, 0))),
        out_shape=jax.ShapeDtypeStruct(full.shape, F32),
        input_output_aliases={3: 0},
        compiler_params=_cp(("parallel", "parallel")),
    )(kc_idx, own, got, full)


def _adam_math(w, g, m, v):
    m = ADAM_B1 * m + (1.0 - ADAM_B1) * g
    v = ADAM_B2 * v + (1.0 - ADAM_B2) * (g * g)
    m_hat = m / (1.0 - ADAM_B1 ** ADAM_STEP)
    v_hat = v / (1.0 - ADAM_B2 ** ADAM_STEP)
    delta = -ADAM_LR * (m_hat / (jnp.sqrt(v_hat) + ADAM_EPS) + ADAM_WD * w)
    return delta, m, v


def adamw(w, g, m, v):
    l, r, c = w.shape
    tr = _block_rows(r, c, 4, 2 << 20)

    def body(w_ref, g_ref, m_ref, v_ref, go_ref, d_ref, mo_ref, vo_ref):
        gv = g_ref[...]
        go_ref[...] = gv
        d_ref[...], mo_ref[...], vo_ref[...] = _adam_math(w_ref[...], gv, m_ref[...], v_ref[...])

    spec = pl.BlockSpec((None, tr, c), lambda i, b: (i, b, 0))
    return pl.pallas_call(
        body, name="adamw", grid=(l, r // tr), in_specs=[spec] * 4, out_specs=[spec] * 4,
        out_shape=[jax.ShapeDtypeStruct(w.shape, F32)] * 4, compiler_params=_cp(("parallel", "parallel")),
    )(w, g, m, v)


def adamw_small(w, g, m, v):
    def body(w_ref, g_ref, m_ref, v_ref, d_ref, mo_ref, vo_ref):
        d_ref[...], mo_ref[...], vo_ref[...] = _adam_math(w_ref[...], g_ref[...], m_ref[...], v_ref[...])

    return pl.pallas_call(body, name="adamw_small", out_shape=[jax.ShapeDtypeStruct(w.shape, F32)] * 3)(w, g, m, v)


def kernel(x, p, w_in, w_out, g_pre, g_post, pool_w, pool_b, pool_scale, conv_w, conv_b, lru_wa, lru_ba, lru_wx, lru_bx, lru_L, w_ple, w_ple_gate, g_ple_in, g_ple_out, loss_target, m_w_in, m_w_out, m_g_pre, m_g_post, m_pool_w, m_pool_b, m_pool_scale, m_conv_w, m_conv_b, m_lru_wa, m_lru_ba, m_lru_wx, m_lru_bx, m_lru_L, m_w_ple, m_w_ple_gate, m_g_ple_in, m_g_ple_out, v_w_in, v_w_out, v_g_pre, v_g_post, v_pool_w, v_pool_b, v_pool_scale, v_conv_w, v_conv_b, v_lru_wa, v_lru_ba, v_lru_wx, v_lru_bx, v_lru_L, v_w_ple, v_w_ple_gate, v_g_ple_in, v_g_ple_out):
    depth = w_in.shape[0]
    _, s, d = x.shape
    e = 2 * d
    kp = p.shape[-1]
    nmix = pool_w.shape[0]
    ngrp = pool_w.shape[1]
    cg = e // ngrp
    cb = e // LRU_HEADS
    xi, yi, ci = lax.axis_index("x"), lax.axis_index("y"), lax.axis_index("c")
    me = 2 * xi + yi
    c_idx = jnp.reshape(ci, (1,)).astype(jnp.int32)
    k_idx = jnp.reshape(me, (1,)).astype(jnp.int32)
    tr_row = _tile(s, 256)
    tr_mix = _tile(s, 512)
    tm = _tile(s, 1024)
    tm2 = _tile(s, 2048)

    def halves(a):
        return a.reshape(a.shape[0], 2, a.shape[1] // 2, a.shape[2])

    big = {
        "w_in": w_in, "w_out": w_out, "gate": w_ple_gate, "ple": w_ple,
        "pool": pool_w.reshape(nmix * ngrp, cg // 4, cg),
        "wa": lru_wa.reshape(nmix * LRU_HEADS, cb // 4, cb), "wx": lru_wx.reshape(nmix * LRU_HEADS, cb // 4, cb),
    }
    names = list(big)

    def layer_shards(i):
        sh = {"w_in": w_in[i][None], "w_out": w_out[i][None], "gate": w_ple_gate[i][None], "ple": w_ple[i][None]}
        if i % 2 == 0:
            sh["pool"] = pool_w[i // 2]
        else:
            sh["wa"], sh["wx"] = lru_wa[i // 2], lru_wx[i // 2]
        return sh

    def mixer_names(i):
        return ["pool"] if i % 2 == 0 else ["wa", "wx"]

    wbuf = [{n: cast_into_slab(w, k_idx) for n, w in layer_shards(i).items()} for i in range(depth)]
    first = list(wbuf[0])
    wbuf[0] = dict(zip(first, gather_weights([wbuf[0][n] for n in first])))

    def full_w(i, n):
        b = wbuf[i][n]
        return b.reshape(b.shape[0], b.shape[1], 2 * b.shape[3], b.shape[4])

    def run_mm(stages, *args):
        if not stages:
            return _matmul(*args)
        out, new = _matmul(*args, hosted=[mk([wbuf[l][n] for n in nms]) for mk, l, nms in stages])
        slots = [(l, n) for _, l, nms in stages for n in nms]
        for (l, n), b in zip(slots, new):
            wbuf[l][n] = b
        return out

    ec = e // 4
    small_loc = jnp.concatenate([conv_w, conv_b[:, None], lru_ba[:, None], lru_bx[:, None], lru_L[:, None]], axis=1)
    sm_all = all_gather_small(small_loc.reshape(nmix * 8, ec), "gather_small").reshape(4, 2, nmix, 8, ec)
    lru_small = jnp.transpose(sm_all[:, 0], (1, 2, 0, 3)).reshape(nmix, 8, e)

    xs = x[0]
    saved = []
    for i in range(depth):
        j = i // 2
        h = rms_fwd(xs, g_pre[i][None], tr_row)
        nj = (2 * e) // 1024 if (2 * e) % 1024 == 0 else 4
        tn = (2 * e) // nj
        per = e // tn
        perk = (e // 2) // tn
        nxt = i + 1 if i + 1 < depth else None
        stages = [(_stage_gather_d2d, i, ["gate", "ple"])] if i > 0 else []
        if nxt is not None:
            stages.append((_stage_gather_ici, nxt, ["w_in"]))
        uz = run_mm(
            stages, "mm_in", "nn", (s // tm2, nj, 1), h, pl.BlockSpec((tm2, d), lambda a, b, k: (a, 0)),
            full_w(i, "w_in"), pl.BlockSpec((None, None, d, tn), lambda a, b, k, perk=perk: (b // perk, 0, 0, b % perk)),
            jax.ShapeDtypeStruct((2, s, e), MXU), pl.BlockSpec((None, tm2, tn), lambda a, b, k, per=per: (b // per, a, b % per)),
            (8, 128))
        if i % 2 == 0:
            y2 = pool_fwd(uz, full_w(i, "pool"), pool_b[j][None], pool_scale[j][None], tr_mix)
            hst = None
        else:
            y2, hst = lru_fwd(uz, full_w(i, "wa"), full_w(i, "wx"), lru_small[j], tr_mix)
        tn_o = _tile(d, 1024)
        stages = [(_stage_gather_ici, nxt, ["w_out"] + mixer_names(nxt))] if nxt is not None else []
        o = run_mm(
            stages, "mm_out", "nn", (s // tm, d // tn_o, 1), y2, pl.BlockSpec((tm, e), lambda a, b, k: (a, 0)),
            full_w(i, "w_out"), pl.BlockSpec((4, None, e // 4, tn_o), lambda a, b, k: (0, 0, 0, b)),
            jax.ShapeDtypeStruct((s, d), F32), pl.BlockSpec((tm, tn_o), lambda a, b, k: (a, b)), (8, 128))
        x1 = res_rms_fwd(xs, o, g_post[i][None], tr_row)
        hn = rms_fwd(x1, g_ple_in[i][None], tr_row)
        stages = []
        if nxt is not None:
            stages = [(_stage_gather_ici, nxt, ["gate", "ple"]), (_stage_gather_d2d, nxt, ["w_in", "w_out"] + mixer_names(nxt))]
        gpre = run_mm(
            stages, "mm_gate", "nn", (s // tm2, d // tn_o, 1), hn, pl.BlockSpec((tm2, d), lambda a, b, k: (a, 0)),
            full_w(i, "gate"), pl.BlockSpec((4, None, d // 4, tn_o), lambda a, b, k: (0, 0, 0, b)),
            jax.ShapeDtypeStruct((s, d), F32), pl.BlockSpec((tm2, tn_o), lambda a, b, k: (a, b)), (8, 128))
        pe = p[i, 0]
        ev = _matmul(
            "mm_ple", "nn", (s // tm, 4, 1), pe, pl.BlockSpec((tm, kp), lambda a, b, k: (a, 0)),
            full_w(i, "ple"), pl.BlockSpec((None, None, kp, d // 4), lambda a, b, k: (b, 0, 0, 0)),
            jax.ShapeDtypeStruct((s, d), F32), pl.BlockSpec((tm, d // 4), lambda a, b, k: (a, b)), (8, 128))
        x2 = ple_fwd(x1, gpre, ev, g_ple_out[i][None], tr_row)
        saved.append((xs, h, uz, y2, hst, o, x1, hn, gpre, ev))
        xs = x2

    dx, sq = loss_bwd(xs, loss_target[0], tr_row)
    d_gpre, d_gpost, d_gin, d_gout = [None] * depth, [None] * depth, [None] * depth, [None] * depth
    d_pool_b, d_pool_sc, d_lru_small = [None] * nmix, [None] * nmix, [None] * nmix
    ts = _tile(s, 1024)
    kc_idx = jnp.stack([me, ci]).astype(jnp.int32)
    full = {n: lax.empty(halves(big[n]).shape, F32) for n in names}
    sums, gots = [None] * depth, [None] * depth

    def scatter_stage(l, nms):
        return _stage_scatter_ici([sums[l][n] for n in nms], [gots[l][n] for n in nms]), [(l, n) for n in nms]

    def run_mm_scatter(specs, *args):
        if not specs:
            return _matmul(*args)
        out, new = _matmul(*args, hosted=[st for st, _ in specs])
        for (l, n), b in zip([slot for _, sl in specs for slot in sl], new):
            gots[l][n] = b
        return out

    def first_row(i, n):
        return i if n in ("w_in", "w_out", "gate", "ple") else (i // 2) * (ngrp if n == "pool" else LRU_HEADS)

    for i in reversed(range(depth)):
        j = i // 2
        prev = i + 1 if i + 1 < depth else None
        x0, h, uz, y2, hst, o, x1, hn, gpre, ev = saved[i]
        pe = p[i, 0]
        gl = {}
        de, dgp, d_gout[i] = ple_bwd(dx, gpre, ev, g_ple_out[i][None], tr_row)
        gl["ple"] = _matmul(
            "mm_dple", "tn", (1, 4, s // ts), pe, pl.BlockSpec((ts, kp), lambda a, b, k: (k, 0)),
            de, pl.BlockSpec((ts, d // 4), lambda a, b, k: (k, b)),
            jax.ShapeDtypeStruct((4, 1, kp, d // 4), WIRE), pl.BlockSpec((None, None, kp, d // 4), lambda a, b, k: (b, 0, 0, 0)),
            (kp, d // 4))
        tn_o = _tile(d, 1024)
        gl["gate"] = _matmul(
            "mm_dgate", "tn", (4, d // tn_o, 1), hn, pl.BlockSpec((s, d // 4), lambda a, b, k: (0, a)),
            dgp, pl.BlockSpec((s, tn_o), lambda a, b, k: (0, b)),
            jax.ShapeDtypeStruct((4, 1, d // 4, d), WIRE), pl.BlockSpec((None, None, d // 4, tn_o), lambda a, b, k: (a, 0, 0, b)),
            (8, 128))
        dhn = _matmul(
            "mm_dhn", "nt", (s // tm2, 4, 1), dgp, pl.BlockSpec((tm2, d), lambda a, b, k: (a, 0)),
            full_w(i, "gate"), pl.BlockSpec((None, None, d // 4, d), lambda a, b, k: (b, 0, 0, 0)),
            jax.ShapeDtypeStruct((s, d), F32), pl.BlockSpec((tm2, d // 4), lambda a, b, k: (a, b)), (8, 128))
        dx1, d_gin[i] = rms_bwd_res(dx, dhn, x1, g_ple_in[i][None], tr_row)
        do, d_gpost[i] = rms_bwd(dx1, o, g_post[i][None], tr_row)
        gl["w_out"] = _matmul(
            "mm_dwout", "tn", (4, d // tn_o, 1), y2, pl.BlockSpec((s, e // 4), lambda a, b, k: (0, a)),
            do, pl.BlockSpec((s, tn_o), lambda a, b, k: (0, b)),
            jax.ShapeDtypeStruct((4, 1, e // 4, d), WIRE), pl.BlockSpec((None, None, e // 4, tn_o), lambda a, b, k: (a, 0, 0, b)),
            (8, 128))
        dy2 = _matmul(
            "mm_dy2", "nt", (s // tm2, 4, 1), do, pl.BlockSpec((tm2, d), lambda a, b, k: (a, 0)),
            full_w(i, "w_out"), pl.BlockSpec((None, None, e // 4, d), lambda a, b, k: (b, 0, 0, 0)),
            jax.ShapeDtypeStruct((s, e), MXU), pl.BlockSpec((tm2, e // 4), lambda a, b, k: (a, b)), (8, 128))
        if i % 2 == 0:
            duz, gl["pool"], d_pool_b[j], d_pool_sc[j] = pool_bwd(uz, dy2, full_w(i, "pool"), pool_b[j][None], pool_scale[j][None], tr_mix)
        else:
            duz, gl["wa"], gl["wx"], d_lru_small[j] = lru_bwd(uz, hst, dy2, full_w(i, "wa"), full_w(i, "wx"), lru_small[j], tr_mix)
        tmi = _tile(d, 1024)
        tni = _tile(e // 2, 1024)
        nslab = (e // 2) // tni
        gl["w_in"] = run_mm_scatter(
            [scatter_stage(prev, ["w_in"])] if prev is not None else [],
            "mm_dwin", "tn", (d // tmi, 4 * nslab, 1), h, pl.BlockSpec((s, tmi), lambda a, b, k: (0, a)),
            duz, pl.BlockSpec((None, s, tni), lambda a, b, k, nslab=nslab: (b // (2 * nslab), 0, b % (2 * nslab))),
            jax.ShapeDtypeStruct((4, 1, d, e // 2), WIRE),
            pl.BlockSpec((None, None, tmi, tni), lambda a, b, k, nslab=nslab: (b // nslab, 0, a, b % nslab)),
            (8, 128))
        lnames = list(gl)
        gparts = [gl[n].reshape(4, gl[n].shape[1], 2, gl[n].shape[2] // 2, gl[n].shape[3]) for n in lnames]
        hosted = [_stage_send_half(gparts, [lax.empty(g.shape[:2] + g.shape[3:], WIRE) for g in gparts])]
        rest = ["w_out", "gate", "ple"] + mixer_names(prev) if prev is not None else []
        if prev is not None:
            hosted.append(_stage_scatter_ici([sums[prev][n] for n in rest], [gots[prev][n] for n in rest]))
        tnd = _tile(d, 1024)
        dh, passed = _matmul(
            "mm_dh", "nt", (s // tm, d // tnd, 4), duz, pl.BlockSpec((None, tm, e // 2), lambda a, b, k: (k // 2, a, k % 2)),
            full_w(i, "w_in"), pl.BlockSpec((None, None, tnd, e // 2), lambda a, b, k: (k, 0, b, 0)),
            jax.ShapeDtypeStruct((s, d), F32), pl.BlockSpec((tm, tnd), lambda a, b, k: (a, b)), (tm, tnd), hosted=hosted)
        from_sib = passed[:len(lnames)]
        for n, b in zip(rest, passed[len(lnames):]):
            gots[prev][n] = b
        dx, d_gpre[i] = rms_bwd_res(dx1, dh, x0, g_pre[i][None], tr_row)
        if prev is not None:
            for n in gots[prev]:
                full[n] = sum_chips(sums[prev][n], gots[prev][n], kc_idx, full[n], first_row(prev, n))
        sums[i] = {n: add_halves(g, r, c_idx) for n, g, r in zip(lnames, gparts, from_sib)}
        gots[i] = {n: lax.empty((3,) + sums[i][n].shape[1:], WIRE) for n in lnames}
    grad_x = dx[None]

    lnames = list(gots[0])
    for n, b in zip(lnames, scatter_to_chips([sums[0][n] for n in lnames])):
        full[n] = sum_chips(sums[0][n], b, kc_idx, full[n], first_row(0, n))
    shared = share_halves([full[n] for n in names], [(0, full[n].shape[0]) for n in names])
    grads = {n: f.reshape(big[n].shape) for n, f in zip(names, shared)}

    def rows_e(a):
        return jnp.stack(a).reshape(-1, e) if isinstance(a, list) else a.reshape(-1, e)

    pack = [rows_e([g[0] for g in d_gpre]), rows_e([g[0] for g in d_gpost]), rows_e([g[0] for g in d_gin]), rows_e([g[0] for g in d_gout]),
            jnp.concatenate(d_pool_b, axis=0), jnp.concatenate(d_pool_sc, axis=0), jnp.concatenate(d_lru_small, axis=0),
            jnp.pad(sq, ((0, 0), (0, e - d)))]
    sizes = [a.shape[0] for a in pack]
    packed = jnp.concatenate(pack, axis=0)
    nrow = packed.shape[0]
    nrow_p = -(-nrow // 8) * 8
    packed = jnp.pad(packed, ((0, nrow_p - nrow), (0, 0)))
    total = sum_devices(all_gather_small(packed, "gather_grads").reshape(8, nrow_p, e))
    parts, off = [], 0
    for n_ in sizes:
        parts.append(total[off:off + n_])
        off += n_
    t_gpre, t_gpost, t_gin, t_gout, t_pb, t_psc, t_lru, t_sq = parts
    loss = 0.5 * jnp.sum(t_sq) / d
    t_lru = t_lru.reshape(nmix, 8, e)
    t_lru_loc = lax.dynamic_slice_in_dim(t_lru, me * ec, ec, axis=2)

    def big_update(name, w, m, v):
        shp = big[name].shape
        g, dl, nm, nv = adamw(w.reshape(shp), grads[name], m.reshape(shp), v.reshape(shp))
        return [a.reshape(w.shape) for a in (g, dl, nm, nv)]

    def small_update(w, g, m, v):
        shp = w.shape
        w2 = w.reshape(-1, shp[-1])
        dl, nm, nv = adamw_small(w2, g.reshape(w2.shape), m.reshape(w2.shape), v.reshape(w2.shape))
        return [g.reshape(shp), dl.reshape(shp), nm.reshape(shp), nv.reshape(shp)]

    res = {
        "w_in": big_update("w_in", w_in, m_w_in, v_w_in),
        "w_out": big_update("w_out", w_out, m_w_out, v_w_out),
        "g_pre": small_update(g_pre, t_gpre.reshape(depth, d), m_g_pre, v_g_pre),
        "g_post": small_update(g_post, t_gpost.reshape(depth, d), m_g_post, v_g_post),
        "pool_w": big_update("pool", pool_w, m_pool_w, v_pool_w),
        "pool_b": small_update(pool_b, t_pb, m_pool_b, v_pool_b),
        "pool_scale": small_update(pool_scale, t_psc, m_pool_scale, v_pool_scale),
        "conv_w": small_update(conv_w, t_lru_loc[:, :CONV_W], m_conv_w, v_conv_w),
        "conv_b": small_update(conv_b, t_lru_loc[:, 4], m_conv_b, v_conv_b),
        "lru_wa": big_update("wa", lru_wa, m_lru_wa, v_lru_wa),
        "lru_ba": small_update(lru_ba, t_lru_loc[:, 5], m_lru_ba, v_lru_ba),
        "lru_wx": big_update("wx", lru_wx, m_lru_wx, v_lru_wx),
        "lru_bx": small_update(lru_bx, t_lru_loc[:, 6], m_lru_bx, v_lru_bx),
        "lru_L": small_update(lru_L, t_lru_loc[:, 7], m_lru_L, v_lru_L),
        "w_ple": big_update("ple", w_ple, m_w_ple, v_w_ple),
        "w_ple_gate": big_update("gate", w_ple_gate, m_w_ple_gate, v_w_ple_gate),
        "g_ple_in": small_update(g_ple_in, t_gin.reshape(depth, d), m_g_ple_in, v_g_ple_in),
        "g_ple_out": small_update(g_ple_out, t_gout.reshape(depth, d), m_g_ple_out, v_g_ple_out),
    }
    order = ["w_in", "w_out", "g_pre", "g_post", "pool_w", "pool_b", "pool_scale", "conv_w", "conv_b", "lru_wa", "lru_ba",
             "lru_wx", "lru_bx", "lru_L", "w_ple", "w_ple_gate", "g_ple_in", "g_ple_out"]
    out = [loss, grad_x]
    for slot in range(4):
        out += [res[n][slot] for n in order]
    return tuple(out)
```

```python
import functools

import jax
import jax.numpy as jnp
from jax import lax
from jax.experimental import pallas as pl
from jax.experimental.pallas import tpu as pltpu

F32 = jnp.float32
MXU = jnp.bfloat16
WIRE = jnp.bfloat16
VMEM_LIMIT = 56 * 1024 * 1024
RMS_EPS = 1e-6
LRU_C = 8.0
POOL_WINDOWS = (2, 4, 8, 16)
MAXW = 16
CONV_W = 4
LRU_HEADS = 16
ADAM_LR, ADAM_B1, ADAM_B2, ADAM_EPS, ADAM_WD, ADAM_STEP = 0.001, 0.9, 0.999, 1e-08, 0.01, 10
MESHID = pl.DeviceIdType.MESH
ANY = pl.BlockSpec(memory_space=pl.ANY)


def _cp(sem=None):
    return pltpu.CompilerParams(dimension_semantics=sem, vmem_limit_bytes=VMEM_LIMIT)


def _sig(v):
    return 0.5 * jnp.tanh(0.5 * v) + 0.5


def _tile(n, pref):
    return pref if n % pref == 0 else n


_DN = {"nn": (((1,), (0,)), ((), ())), "nt": (((1,), (1,)), ((), ())), "tn": (((0,), (0,)), ((), ()))}


def _matmul(name, mode, grid, a, a_spec, b, b_spec, out_shape, out_spec, acc_shape, hosted=()):
    nk = grid[2]
    ro = [r for st in hosted for r in st["ro"]]
    bufs = [r for st in hosted for r in st["bufs"]]
    nro, nbuf = len(ro), len(bufs)

    def body(a_ref, b_ref, *rest):
        ro_refs = rest[:nro]
        o_ref = rest[nro + nbuf]
        buf_refs = rest[nro + nbuf + 1:nro + 2 * nbuf + 1]
        acc_ref = rest[nro + 2 * nbuf + 1]
        ids = [pl.program_id(t) for t in range(3)]
        kk = ids[2]

        def run_stages(what):
            r0 = b0 = s0 = 0
            for st in hosted:
                st[what](ro_refs[r0:r0 + len(st["ro"])], buf_refs[b0:b0 + len(st["bufs"])], rest[-2], rest[-1], s0)
                r0, b0, s0 = r0 + len(st["ro"]), b0 + len(st["bufs"]), s0 + st["nsem"]

        if hosted:
            @pl.when((ids[0] == 0) & (ids[1] == 0) & (ids[2] == 0))
            def _():
                run_stages("start")

        bv = b_ref[...]
        if bv.ndim == 3:
            bv = bv.reshape(bv.shape[0] * bv.shape[1], bv.shape[2])
        prod = lax.dot_general(a_ref[...].astype(MXU), bv.astype(MXU), _DN[mode], preferred_element_type=F32)
        if nk == 1:
            o_ref[...] = prod.astype(o_ref.dtype)
        else:
            @pl.when(kk == 0)
            def _():
                acc_ref[...] = prod

            @pl.when(kk > 0)
            def _():
                acc_ref[...] += prod

            @pl.when(kk == nk - 1)
            def _():
                o_ref[...] = acc_ref[...].astype(o_ref.dtype)

        if hosted:
            @pl.when((ids[0] == grid[0] - 1) & (ids[1] == grid[1] - 1) & (ids[2] == grid[2] - 1))
            def _():
                run_stages("finish")

    scratch = [pltpu.VMEM(acc_shape, F32)]
    if not hosted:
        return pl.pallas_call(
            body, name=name, grid=grid, in_specs=[a_spec, b_spec], out_specs=out_spec, out_shape=out_shape,
            scratch_shapes=scratch, compiler_params=_cp(("parallel", "parallel", "arbitrary")),
        )(a, b)
    nsem = sum(st["nsem"] for st in hosted)
    res = pl.pallas_call(
        body, name=name, grid=grid,
        in_specs=[a_spec, b_spec] + [ANY] * (nro + nbuf), out_specs=[out_spec] + [ANY] * nbuf,
        out_shape=[out_shape] + [jax.ShapeDtypeStruct(x.shape, x.dtype) for x in bufs],
        scratch_shapes=scratch + [pltpu.SemaphoreType.DMA((nsem,)), pltpu.SemaphoreType.DMA((nsem,))],
        input_output_aliases={2 + nro + t: 1 + t for t in range(nbuf)},
        compiler_params=_cp(("arbitrary", "arbitrary", "arbitrary")),
    )(a, b, *ro, *bufs)
    return res[0], list(res[1:])


def _rows_call(name, body, ins, in_rows, outs, out_rows, n_rows, tr):
    def spec(shape, tiled):
        if tiled:
            return pl.BlockSpec((tr, shape[1]), lambda i: (i, 0))
        return pl.BlockSpec(shape, lambda i: (0, 0))

    return pl.pallas_call(
        body, name=name, grid=(n_rows // tr,),
        in_specs=[spec(a.shape, t) for a, t in zip(ins, in_rows)],
        out_specs=[spec(o.shape, t) for o, t in zip(outs, out_rows)],
        out_shape=outs, compiler_params=_cp(("arbitrary",)),
    )(*ins)


def _rstd(v):
    return lax.rsqrt(jnp.mean(v * v, axis=-1, keepdims=True) + RMS_EPS)


def _norm_bwd(v, g, dy):
    r = _rstd(v)
    n = v * r
    dn = dy * g
    dv = r * (dn - n * jnp.mean(dn * n, axis=-1, keepdims=True))
    return dv, jnp.sum(dy * n, axis=0, keepdims=True)


def _acc_rows(ref, val):
    @pl.when(pl.program_id(0) == 0)
    def _():
        ref[...] = val

    @pl.when(pl.program_id(0) > 0)
    def _():
        ref[...] += val


def rms_fwd(x, g, tr):
    def body(x_ref, g_ref, o_ref):
        v = x_ref[...]
        o_ref[...] = (v * _rstd(v) * g_ref[...]).astype(o_ref.dtype)

    return _rows_call("rms_fwd", body, [x, g], [True, False], [jax.ShapeDtypeStruct(x.shape, MXU)], [True], x.shape[0], tr)[0]


def res_rms_fwd(x, o, g, g_next, tr):
    def body(x_ref, o_ref, g_ref, gn_ref, y_ref, h_ref):
        v = o_ref[...].astype(F32)
        y = x_ref[...] + v * _rstd(v) * g_ref[...]
        y_ref[...] = y
        h_ref[...] = (y * _rstd(y) * gn_ref[...]).astype(h_ref.dtype)

    return _rows_call("res_rms_fwd", body, [x, o, g, g_next], [True, True, False, False],
                      [jax.ShapeDtypeStruct(x.shape, F32), jax.ShapeDtypeStruct(x.shape, MXU)], [True, True], x.shape[0], tr)


def ple_fwd(x1, gpre, e, g, g_next, tr):
    def body(x_ref, gp_ref, e_ref, g_ref, gn_ref, y_ref, h_ref):
        v = e_ref[...].astype(F32) * _sig(gp_ref[...].astype(F32))
        y = x_ref[...] + v * _rstd(v) * g_ref[...]
        y_ref[...] = y
        h_ref[...] = (y * _rstd(y) * gn_ref[...]).astype(h_ref.dtype)

    return _rows_call("ple_fwd", body, [x1, gpre, e, g, g_next], [True, True, True, False, False],
                      [jax.ShapeDtypeStruct(x1.shape, F32), jax.ShapeDtypeStruct(x1.shape, MXU)], [True, True], x1.shape[0], tr)


def ple_fwd_last(x1, gpre, e, g, tr):
    def body(x_ref, gp_ref, e_ref, g_ref, y_ref):
        v = e_ref[...].astype(F32) * _sig(gp_ref[...].astype(F32))
        y_ref[...] = x_ref[...] + v * _rstd(v) * g_ref[...]

    return _rows_call("ple_fwd_last", body, [x1, gpre, e, g], [True, True, True, False],
                      [jax.ShapeDtypeStruct(x1.shape, F32)], [True], x1.shape[0], tr)[0]


def loss_bwd(y, target, tr):
    d = y.shape[1]

    def body(y_ref, t_ref, dy_ref, sq_ref):
        diff = y_ref[...] - t_ref[...]
        dy_ref[...] = diff * (1.0 / d)
        _acc_rows(sq_ref, jnp.sum(diff * diff, axis=0, keepdims=True))

    return _rows_call("loss_bwd", body, [y, target], [True, True],
                      [jax.ShapeDtypeStruct(y.shape, F32), jax.ShapeDtypeStruct((1, d), F32)], [True, False], y.shape[0], tr)


def ple_bwd(dx2, gpre, e, g, tr):
    d = dx2.shape[1]

    def body(dx_ref, gp_ref, e_ref, g_ref, de_ref, dgp_ref, dg_ref):
        gate = _sig(gp_ref[...].astype(F32))
        ev = e_ref[...].astype(F32)
        dv, dg = _norm_bwd(ev * gate, g_ref[...], dx_ref[...])
        de_ref[...] = (dv * gate).astype(de_ref.dtype)
        dgp_ref[...] = (dv * ev * gate * (1.0 - gate)).astype(dgp_ref.dtype)
        _acc_rows(dg_ref, dg)

    return _rows_call("ple_bwd", body, [dx2, gpre, e, g], [True, True, True, False],
                      [jax.ShapeDtypeStruct(dx2.shape, MXU), jax.ShapeDtypeStruct(dx2.shape, MXU), jax.ShapeDtypeStruct((1, d), F32)],
                      [True, True, False], dx2.shape[0], tr)


def rms_bwd_res(dres, dh, x, g, tr):
    d = x.shape[1]

    def body(dr_ref, dh_ref, x_ref, g_ref, dx_ref, dg_ref):
        dv, dg = _norm_bwd(x_ref[...], g_ref[...], dh_ref[...].astype(F32))
        dx_ref[...] = dr_ref[...] + dv
        _acc_rows(dg_ref, dg)

    return _rows_call("rms_bwd_res", body, [dres, dh, x, g], [True, True, True, False],
                      [jax.ShapeDtypeStruct(x.shape, F32), jax.ShapeDtypeStruct((1, d), F32)], [True, False], x.shape[0], tr)


def post_bwd(dres, dh, x, g, o, g_o, tr):
    d = x.shape[1]

    def body(dr_ref, dh_ref, x_ref, g_ref, o_ref, go_ref, dx_ref, do_ref, dg_ref, dgo_ref):
        dv, dg = _norm_bwd(x_ref[...], g_ref[...], dh_ref[...].astype(F32))
        dxv = dr_ref[...] + dv
        dx_ref[...] = dxv
        dov, dgo = _norm_bwd(o_ref[...].astype(F32), go_ref[...], dxv)
        do_ref[...] = dov.astype(do_ref.dtype)
        _acc_rows(dg_ref, dg)
        _acc_rows(dgo_ref, dgo)

    return _rows_call("post_bwd", body, [dres, dh, x, g, o, g_o], [True, True, True, False, True, False],
                      [jax.ShapeDtypeStruct(x.shape, F32), jax.ShapeDtypeStruct(x.shape, MXU),
                       jax.ShapeDtypeStruct((1, d), F32), jax.ShapeDtypeStruct((1, d), F32)], [True, True, False, False], x.shape[0], tr)


def _trailing_sum(ext, w):
    s, k = ext, 1
    while k < w:
        s = s + pltpu.roll(s, k, 0)
        k *= 2
    return s


def _leading_sum(ext, w):
    n = ext.shape[0]
    s, k = ext, 1
    while k < w:
        s = s + pltpu.roll(s, n - k, 0)
        k *= 2
    return s


def _pool_inv_count(rb, tr, w, c):
    t = rb * tr + lax.broadcasted_iota(jnp.int32, (tr, c), 0)
    return 1.0 / jnp.minimum(t + 1, w).astype(F32)


def _pool_d(u_ref, up_ref, rb, tr, w):
    cur = u_ref[...].astype(F32)
    prev = jnp.where(rb > 0, up_ref[...].astype(F32), 0.0)
    ext = jnp.concatenate([prev, cur], axis=0)
    win = _trailing_sum(ext, w)[MAXW:]
    return win * _pool_inv_count(rb, tr, w, cur.shape[1]) - cur


def pool_fwd(uz, w_g, bias, scale, tr):
    _, s, e = uz.shape
    ng = len(POOL_WINDOWS)
    cg = e // ng
    nb = s // tr
    hb = tr // MAXW

    def body(u_ref, up_ref, z_ref, w_ref, b_ref, sc_ref, o_ref):
        g, rb = pl.program_id(0), pl.program_id(1)
        wmat = w_ref[...].reshape(cg, cg)
        for gg, win in enumerate(POOL_WINDOWS):
            @pl.when(g == gg)
            def _(win=win):
                d = _pool_d(u_ref, up_ref, rb, tr, win)
                y = (jnp.dot(d.astype(MXU), wmat, preferred_element_type=F32) + b_ref[...]) * sc_ref[...]
                z = z_ref[...].astype(F32)
                o_ref[...] = (y * z * _sig(z)).astype(o_ref.dtype)

    return pl.pallas_call(
        body, name="pool_fwd", grid=(ng, nb),
        in_specs=[
            pl.BlockSpec((None, tr, cg), lambda g, r: (0, r, g)),
            pl.BlockSpec((None, MAXW, cg), lambda g, r: (0, jnp.maximum(r * hb - 1, 0), g)),
            pl.BlockSpec((None, tr, cg), lambda g, r: (1, r, g)),
            pl.BlockSpec((4, None, cg // 4, cg), lambda g, r: (0, g, 0, 0)),
            pl.BlockSpec((1, cg), lambda g, r: (0, g)),
            pl.BlockSpec((1, cg), lambda g, r: (0, g)),
        ],
        out_specs=pl.BlockSpec((tr, cg), lambda g, r: (r, g)),
        out_shape=jax.ShapeDtypeStruct((s, e), MXU),
        compiler_params=_cp(("parallel", "arbitrary")),
    )(uz, uz, uz, w_g, bias, scale)


def pool_bwd(uz, dy2, w_g, bias, scale, tr):
    _, s, e = uz.shape
    ng = len(POOL_WINDOWS)
    cg = e // ng
    nb = s // tr
    hb = tr // MAXW

    def body(u_ref, up_ref, z_ref, dy_ref, w_ref, b_ref, sc_ref, duz_ref, gw_ref, db_ref, dsc_ref, acc_ref, carry_ref):
        g, step = pl.program_id(0), pl.program_id(1)
        rb = nb - 1 - step
        wmat = w_ref[...].reshape(cg, cg)
        for gg, win in enumerate(POOL_WINDOWS):
            @pl.when(g == gg)
            def _(win=win):
                d = _pool_d(u_ref, up_ref, rb, tr, win).astype(MXU)
                ypre = jnp.dot(d, wmat, preferred_element_type=F32) + b_ref[...]
                z = z_ref[...].astype(F32)
                sg = _sig(z)
                dy2v = dy_ref[...].astype(F32)
                dyv = dy2v * z * sg
                duz_ref[1] = (dy2v * ypre * sc_ref[...] * sg * (1.0 + z * (1.0 - sg))).astype(duz_ref.dtype)
                dypre = dyv * sc_ref[...]
                dsc = jnp.sum(dyv * ypre, axis=0, keepdims=True)
                dbv = jnp.sum(dypre, axis=0, keepdims=True)
                dypre_b = dypre.astype(MXU)
                dd = lax.dot_general(dypre_b, wmat, _DN["nt"], preferred_element_type=F32)
                gw = lax.dot_general(d, dypre_b, _DN["tn"], preferred_element_type=F32)
                q = dd * _pool_inv_count(rb, tr, win, cg)
                nxt = jnp.where(step > 0, carry_ref[...], 0.0)
                lead = _leading_sum(jnp.concatenate([q, nxt], axis=0), win)[:tr]
                duz_ref[0] = (lead - dd).astype(duz_ref.dtype)
                carry_ref[...] = q[:MAXW]

                @pl.when(step == 0)
                def _():
                    acc_ref[...] = gw
                    db_ref[...] = dbv
                    dsc_ref[...] = dsc

                @pl.when(step > 0)
                def _():
                    acc_ref[...] += gw
                    db_ref[...] += dbv
                    dsc_ref[...] += dsc

                @pl.when(step == nb - 1)
                def _():
                    gw_ref[...] = acc_ref[...].reshape(4, cg // 4, cg).astype(gw_ref.dtype)

    return pl.pallas_call(
        body, name="pool_bwd", grid=(ng, nb),
        in_specs=[
            pl.BlockSpec((None, tr, cg), lambda g, r: (0, nb - 1 - r, g)),
            pl.BlockSpec((None, MAXW, cg), lambda g, r: (0, jnp.maximum((nb - 1 - r) * hb - 1, 0), g)),
            pl.BlockSpec((None, tr, cg), lambda g, r: (1, nb - 1 - r, g)),
            pl.BlockSpec((tr, cg), lambda g, r: (nb - 1 - r, g)),
            pl.BlockSpec((4, None, cg // 4, cg), lambda g, r: (0, g, 0, 0)),
            pl.BlockSpec((1, cg), lambda g, r: (0, g)),
            pl.BlockSpec((1, cg), lambda g, r: (0, g)),
        ],
        out_specs=[
            pl.BlockSpec((2, tr, cg), lambda g, r: (0, nb - 1 - r, g)),
            pl.BlockSpec((4, None, cg // 4, cg), lambda g, r: (0, g, 0, 0)),
            pl.BlockSpec((1, cg), lambda g, r: (0, g)),
            pl.BlockSpec((1, cg), lambda g, r: (0, g)),
        ],
        out_shape=[
            jax.ShapeDtypeStruct((2, s, e), MXU),
            jax.ShapeDtypeStruct(w_g.shape, WIRE),
            jax.ShapeDtypeStruct((1, e), F32),
            jax.ShapeDtypeStruct((1, e), F32),
        ],
        scratch_shapes=[pltpu.VMEM((cg, cg), F32), pltpu.VMEM((MAXW, cg), F32)],
        compiler_params=_cp(("parallel", "arbitrary")),
    )(uz, uz, uz, dy2, w_g, bias, scale)


HALO = 16


def _one_minus_sq(log_a, a):
    poly = (-2.0 * log_a) * (1.0 + log_a * (1.0 + log_a * (2.0 / 3.0)))
    return jnp.where(log_a > -0.01, poly, 1.0 - a * a)


def _softplus_neg(lam):
    t = jnp.exp(-jnp.abs(lam))
    log1p = jnp.where(t < 1e-3, t * (1.0 - t * (0.5 - t * (1.0 / 3.0))), jnp.log(1.0 + t))
    return jnp.maximum(-lam, 0.0) + log1p, _sig(-lam)


def _lru_gates(u_ref, up_ref, rb, sm_ref, wa, wx):
    cur = u_ref[...].astype(F32)
    prev = jnp.where(rb > 0, up_ref[...].astype(F32), 0.0)
    ext = jnp.concatenate([prev, cur], axis=0)
    taps = [cur] + [pltpu.roll(ext, k, 0)[HALO:] for k in range(1, CONV_W)]
    uc = sm_ref[CONV_W:CONV_W + 1, :]
    for k in range(CONV_W):
        uc = uc + taps[k] * sm_ref[CONV_W - 1 - k:CONV_W - k, :]
    ucb = uc.astype(MXU)
    r = _sig(jnp.dot(ucb, wa, preferred_element_type=F32) + sm_ref[5:6, :])
    ig = _sig(jnp.dot(ucb, wx, preferred_element_type=F32) + sm_ref[6:7, :])
    sp, sgn = _softplus_neg(sm_ref[7:8, :])
    log_a = r * (-LRU_C * sp)
    a = jnp.exp(log_a)
    mult = jnp.sqrt(jnp.maximum(_one_minus_sq(log_a, a), 0.0))
    return taps, uc, ucb, r, ig, sp, sgn, a, mult


LANES = 128


def _seg_scan(a, b, out_ref, scr, state, reverse):
    a_s, b_s, h_s, p_s = scr
    tr, c = a.shape
    seg = tr // 8
    nl = c // LANES
    for l in range(nl):
        a_s[l] = a[:, l * LANES:(l + 1) * LANES]
        b_s[l] = b[:, l * LANES:(l + 1) * LANES]
    h = [jnp.zeros((8, LANES), F32)] * nl
    pp = [jnp.ones((8, LANES), F32)] * nl
    for i in (range(seg - 1, -1, -1) if reverse else range(seg)):
        rows = pl.ds(i, 8, stride=seg)
        for l in range(nl):
            av = a_s[l, rows, :]
            h[l] = av * h[l] + b_s[l, rows, :]
            pp[l] = av * pp[l]
            h_s[l, pl.ds(8 * i, 8), :] = h[l]
            p_s[l, pl.ds(8 * i, 8), :] = pp[l]
    leaving = []
    for l in range(nl):
        lanes = slice(l * LANES, (l + 1) * LANES)
        st = state[:, lanes]
        for sgm in (range(7, -1, -1) if reverse else range(8)):
            for t0 in range(0, seg, 8):
                rows = pl.ds(8 * t0 + sgm, 8, stride=8)
                out_ref[pl.ds(sgm * seg + t0, 8), lanes] = h_s[l, rows, :] + p_s[l, rows, :] * st
            st = h[l][sgm:sgm + 1, :] + pp[l][sgm:sgm + 1, :] * st
        leaving.append(st)
    return jnp.concatenate(leaving, axis=1)


def lru_fwd(uz, wa_g, wx_g, small, tr):
    _, s, e = uz.shape
    cb = e // LRU_HEADS
    nb = s // tr
    hb = tr // HALO

    def body(u_ref, up_ref, z_ref, wa_ref, wx_ref, sm_ref, o_ref, h_ref, s0, s1, s2, s3, carry_ref):
        rb = pl.program_id(1)
        wa = wa_ref[...].reshape(cb, cb)
        wx = wx_ref[...].reshape(cb, cb)
        _, uc, _, _, ig, _, _, a, mult = _lru_gates(u_ref, up_ref, rb, sm_ref, wa, wx)
        start = jnp.where(rb > 0, carry_ref[0:1, :], 0.0)
        last = _seg_scan(a, mult * ig * uc, h_ref, (s0, s1, s2, s3), start, False)
        carry_ref[...] = jnp.broadcast_to(last, carry_ref.shape)
        z = z_ref[...].astype(F32)
        o_ref[...] = (h_ref[...] * z * _sig(z)).astype(o_ref.dtype)

    wspec = pl.BlockSpec((4, None, cb // 4, cb), lambda h, r: (0, h, 0, 0))
    return pl.pallas_call(
        body, name="lru_fwd", grid=(LRU_HEADS, nb),
        in_specs=[
            pl.BlockSpec((None, tr, cb), lambda h, r: (0, r, h)),
            pl.BlockSpec((None, HALO, cb), lambda h, r: (0, jnp.maximum(r * hb - 1, 0), h)),
            pl.BlockSpec((None, tr, cb), lambda h, r: (1, r, h)),
            wspec, wspec,
            pl.BlockSpec((8, cb), lambda h, r: (0, h)),
        ],
        out_specs=[pl.BlockSpec((tr, cb), lambda h, r: (r, h)), pl.BlockSpec((tr, cb), lambda h, r: (r, h))],
        out_shape=[jax.ShapeDtypeStruct((s, e), MXU), jax.ShapeDtypeStruct((s, e), F32)],
        scratch_shapes=[pltpu.VMEM((cb // LANES, tr, LANES), F32)] * 4 + [pltpu.VMEM((8, cb), F32)],
        compiler_params=_cp(("parallel", "arbitrary")),
    )(uz, uz, uz, wa_g, wx_g, small)


def lru_bwd(uz, hst, dy2, wa_g, wx_g, small, tr):
    _, s, e = uz.shape
    cb = e // LRU_HEADS
    nb = s // tr
    hb = tr // HALO

    def body(u_ref, up_ref, z_ref, h_ref, hp_ref, dy_ref, wa_ref, wx_ref, sm_ref,
             duz_ref, gwa_ref, gwx_ref, dsm_ref, s0, s1, s2, s3, g_s, acc_a, acc_x, gcar, acar, dcar):
        step = pl.program_id(1)
        rb = nb - 1 - step
        wa = wa_ref[...].reshape(cb, cb)
        wx = wx_ref[...].reshape(cb, cb)
        taps, uc, ucb, r, ig, sp, sgn, a, mult = _lru_gates(u_ref, up_ref, rb, sm_ref, wa, wx)
        row = lax.broadcasted_iota(jnp.int32, a.shape, 0)
        z = z_ref[...].astype(F32)
        sg = _sig(z)
        dy2v = dy_ref[...].astype(F32)
        hv = h_ref[...]
        duz_ref[1] = (dy2v * hv * sg * (1.0 + z * (1.0 - sg))).astype(duz_ref.dtype)
        a_next = jnp.where(row == tr - 1, jnp.where(step > 0, acar[0:1, :], 0.0), pltpu.roll(a, tr - 1, 0))
        g_first = _seg_scan(a_next, dy2v * z * sg, g_s, (s0, s1, s2, s3), jnp.where(step > 0, gcar[0:1, :], 0.0), True)
        gcar[...] = jnp.broadcast_to(g_first, gcar.shape)
        acar[...] = jnp.broadcast_to(a[0:1, :], acar.shape)
        gv = g_s[...]
        h_before = jnp.where(rb > 0, hp_ref[HALO - 1:HALO, :], 0.0)
        h_prev = jnp.where(row == 0, h_before, pltpu.roll(hv, 1, 0))
        da = gv * h_prev
        gu = gv * uc
        dmult = gu * ig
        dig = gu * mult
        dlog_a = da * a - dmult * jnp.where(mult > 0.0, a * a / mult, 0.0)
        dra = dlog_a * (-LRU_C) * sp * r * (1.0 - r)
        dix = dig * ig * (1.0 - ig)
        dl = jnp.sum(dlog_a * r, axis=0, keepdims=True) * (LRU_C * sgn)
        dra_b, dix_b = dra.astype(MXU), dix.astype(MXU)
        duc = (gv * mult * ig + lax.dot_general(dra_b, wa, _DN["nt"], preferred_element_type=F32)
               + lax.dot_general(dix_b, wx, _DN["nt"], preferred_element_type=F32))
        gwa = lax.dot_general(ucb, dra_b, _DN["tn"], preferred_element_type=F32)
        gwx = lax.dot_general(ucb, dix_b, _DN["tn"], preferred_element_type=F32)
        ext = jnp.concatenate([duc, jnp.where(step > 0, dcar[...], 0.0)], axis=0)
        n = ext.shape[0]
        du = duc * sm_ref[CONV_W - 1:CONV_W, :]
        for k in range(1, CONV_W):
            du = du + pltpu.roll(ext, n - k, 0)[:tr] * sm_ref[CONV_W - 1 - k:CONV_W - k, :]
        duz_ref[0] = du.astype(duz_ref.dtype)
        dcar[...] = duc[:HALO]
        rows = [jnp.sum(duc * taps[CONV_W - 1 - k], axis=0, keepdims=True) for k in range(CONV_W)]
        rows += [jnp.sum(duc, axis=0, keepdims=True), jnp.sum(dra, axis=0, keepdims=True),
                 jnp.sum(dix, axis=0, keepdims=True), dl]

        @pl.when(step == 0)
        def _():
            acc_a[...] = gwa
            acc_x[...] = gwx
            for k, rv in enumerate(rows):
                dsm_ref[k:k + 1, :] = rv

        @pl.when(step > 0)
        def _():
            acc_a[...] += gwa
            acc_x[...] += gwx
            for k, rv in enumerate(rows):
                dsm_ref[k:k + 1, :] += rv

        @pl.when(step == nb - 1)
        def _():
            gwa_ref[...] = acc_a[...].reshape(4, cb // 4, cb).astype(gwa_ref.dtype)
            gwx_ref[...] = acc_x[...].reshape(4, cb // 4, cb).astype(gwx_ref.dtype)

    wspec = pl.BlockSpec((4, None, cb // 4, cb), lambda h, r: (0, h, 0, 0))
    blk = pl.BlockSpec((tr, cb), lambda h, r: (nb - 1 - r, h))
    return pl.pallas_call(
        body, name="lru_bwd", grid=(LRU_HEADS, nb),
        in_specs=[
            pl.BlockSpec((None, tr, cb), lambda h, r: (0, nb - 1 - r, h)),
            pl.BlockSpec((None, HALO, cb), lambda h, r: (0, jnp.maximum((nb - 1 - r) * hb - 1, 0), h)),
            pl.BlockSpec((None, tr, cb), lambda h, r: (1, nb - 1 - r, h)),
            blk,
            pl.BlockSpec((HALO, cb), lambda h, r: (jnp.maximum((nb - 1 - r) * hb - 1, 0), h)),
            blk,
            wspec, wspec,
            pl.BlockSpec((8, cb), lambda h, r: (0, h)),
        ],
        out_specs=[
            pl.BlockSpec((2, tr, cb), lambda h, r: (0, nb - 1 - r, h)),
            wspec, wspec,
            pl.BlockSpec((8, cb), lambda h, r: (0, h)),
        ],
        out_shape=[
            jax.ShapeDtypeStruct((2, s, e), MXU),
            jax.ShapeDtypeStruct(wa_g.shape, WIRE),
            jax.ShapeDtypeStruct(wx_g.shape, WIRE),
            jax.ShapeDtypeStruct((8, e), F32),
        ],
        scratch_shapes=([pltpu.VMEM((cb // LANES, tr, LANES), F32)] * 4 + [pltpu.VMEM((tr, cb), F32)]
                        + [pltpu.VMEM((cb, cb), F32)] * 2 + [pltpu.VMEM((8, cb), F32)] * 2 + [pltpu.VMEM((HALO, cb), F32)]),
        compiler_params=_cp(("parallel", "arbitrary")),
    )(uz, uz, uz, hst, hst, dy2, wa_g, wx_g, small)


def _place():
    x, y, c = lax.axis_index("x"), lax.axis_index("y"), lax.axis_index("c")
    chips = [(1 - x, y), (x, 1 - y), (1 - x, 1 - y)]
    return x, y, c, chips


def _rcopy(src, dst, send_sems, recv_sems, k, to):
    return pltpu.make_async_remote_copy(src_ref=src, dst_ref=dst, send_sem=send_sems.at[k], recv_sem=recv_sems.at[k],
                                        device_id=to, device_id_type=MESHID)


def _stage_gather_ici(bufs):
    n = len(bufs)

    def copies(refs, ss, rs, off, own):
        x, y, c, chips = _place()
        out = []
        for a in range(n):
            for jj, ch in enumerate(chips):
                blk = refs[a].at[2 * x + y if own else 2 * ch[0] + ch[1], :, c]
                out.append(_rcopy(blk, blk, ss, rs, off + 3 * a + jj, (*ch, c)))
        return out

    def start(ro, refs, ss, rs, off):
        for cp in copies(refs, ss, rs, off, True):
            cp.start()

    def finish(ro, refs, ss, rs, off):
        for cp in copies(refs, ss, rs, off, False):
            cp.wait_recv()
        for cp in copies(refs, ss, rs, off, True):
            cp.wait_send()

    return dict(ro=[], bufs=list(bufs), nsem=3 * n, start=start, finish=finish)


def _stage_gather_d2d(bufs):
    n = len(bufs)

    def copies(refs, ss, rs, off, sending):
        x, y, c, chips = _place()
        out = []
        for a in range(n):
            for jj, ch in enumerate(chips):
                blk = refs[a].at[2 * ch[0] + ch[1], :, c if sending else 1 - c]
                out.append(_rcopy(blk, blk, ss, rs, off + 3 * a + jj, (x, y, 1 - c)))
        return out

    def start(ro, refs, ss, rs, off):
        for cp in copies(refs, ss, rs, off, True):
            cp.start()

    def finish(ro, refs, ss, rs, off):
        for cp in copies(refs, ss, rs, off, False):
            cp.wait_recv()
        for cp in copies(refs, ss, rs, off, True):
            cp.wait_send()

    return dict(ro=[], bufs=list(bufs), nsem=3 * n, start=start, finish=finish)


def _stage_scatter_ici(parts, gots):
    n = len(parts)

    def copies(ro, refs, ss, rs, off):
        x, y, c, chips = _place()
        return [_rcopy(ro[a].at[2 * ch[0] + ch[1]], refs[a].at[jj], ss, rs, off + 3 * a + jj, (*ch, c))
                for a in range(n) for jj, ch in enumerate(chips)]

    def start(ro, refs, ss, rs, off):
        for cp in copies(ro, refs, ss, rs, off):
            cp.start()

    def finish(ro, refs, ss, rs, off):
        for cp in copies(ro, refs, ss, rs, off):
            cp.wait()

    return dict(ro=list(parts), bufs=list(gots), nsem=3 * n, start=start, finish=finish)


def _stage_send_half(grads, lands):
    n = len(grads)

    def copies(ro, refs, ss, rs, off):
        x, y, c, _ = _place()
        return [_rcopy(ro[a].at[:, :, 1 - c], refs[a], ss, rs, off + a, (x, y, 1 - c)) for a in range(n)]

    def start(ro, refs, ss, rs, off):
        for cp in copies(ro, refs, ss, rs, off):
            cp.start()

    def finish(ro, refs, ss, rs, off):
        for cp in copies(ro, refs, ss, rs, off):
            cp.wait()

    return dict(ro=list(grads), bufs=list(lands), nsem=n, start=start, finish=finish)


def cast_into_slab(w, k_idx):
    l, r, c = w.shape
    rh = r // 2
    lb, tr = _block_lr(l, rh, c)
    nbh = rh // tr

    def body(k_ref, w_ref, o_ref):
        o_ref[...] = w_ref[...].astype(o_ref.dtype)

    return pl.pallas_call(
        body, name="cast_into_slab",
        grid_spec=pltpu.PrefetchScalarGridSpec(
            num_scalar_prefetch=1, grid=(l // lb, 2, nbh),
            in_specs=[pl.BlockSpec((lb, tr, c), lambda i, h, b, k_ref: (i, h * nbh + b, 0))],
            out_specs=pl.BlockSpec((None, lb, None, tr, c), lambda i, h, b, k_ref: (k_ref[0], i, h, b, 0))),
        out_shape=jax.ShapeDtypeStruct((4, l, 2, rh, c), WIRE),
        compiler_params=_cp(("parallel", "parallel", "parallel")),
    )(k_idx, w)


def gather_weights(bufs):
    n = len(bufs)

    def body(*refs):
        outs = refs[n:2 * n]
        send_sems, recv_sems = refs[2 * n:]
        x, y, c, chips = _place()
        me = 2 * x + y
        sib = (x, y, 1 - c)
        sends = []
        for a in range(n):
            for jj, ch in enumerate(chips):
                mine = outs[a].at[me, :, c]
                cp = _rcopy(mine, mine, send_sems, recv_sems, 6 * a + jj, (*ch, c))
                cp.start()
                sends.append(cp)
        for a in range(n):
            for jj, ch in enumerate(chips):
                blk = outs[a].at[2 * ch[0] + ch[1], :, c]
                _rcopy(blk, blk, send_sems, recv_sems, 6 * a + jj, (*ch, c)).wait_recv()
                fw = _rcopy(blk, blk, send_sems, recv_sems, 6 * a + 3 + jj, sib)
                fw.start()
                sends.append(fw)
        for a in range(n):
            for jj, ch in enumerate(chips):
                blk = outs[a].at[2 * ch[0] + ch[1], :, 1 - c]
                _rcopy(blk, blk, send_sems, recv_sems, 6 * a + 3 + jj, sib).wait_recv()
        for cp in sends:
            cp.wait_send()

    return pl.pallas_call(
        body, name="gather_weights",
        in_specs=[ANY] * n, out_specs=[ANY] * n,
        out_shape=[jax.ShapeDtypeStruct(a.shape, a.dtype) for a in bufs],
        scratch_shapes=[pltpu.SemaphoreType.DMA((6 * n,)), pltpu.SemaphoreType.DMA((6 * n,))],
        input_output_aliases={a: a for a in range(n)},
        compiler_params=pltpu.CompilerParams(has_side_effects=True),
    )(*bufs)


def all_gather_small(v, name):
    m_per, n = v.shape

    def body(x_ref, out_ref, send_sems, recv_sems, local_sem):
        x, y, c, chips = _place()
        me, sibling = (x, y, c), (x, y, 1 - c)

        def rows(px, py, pc):
            return out_ref.at[pl.ds((4 * px + 2 * py + pc) * m_per, m_per), :]

        def copy(k, block, to, src=None):
            return _rcopy(rows(*block) if src is None else src, rows(*block), send_sems, recv_sems, k, to)

        mine = pltpu.make_async_copy(x_ref, rows(*me), local_sem)
        mine.start()
        first = [copy(0, me, sibling, src=x_ref)]
        first += [copy(1 + jj, me, (*chip, c), src=x_ref) for jj, chip in enumerate(chips)]
        for cp in first:
            cp.start()
        passed = [copy(4 + jj, (*chip, c), sibling) for jj, chip in enumerate(chips)]
        for jj, chip in enumerate(chips):
            copy(1 + jj, (*chip, c), me).wait_recv()
            passed[jj].start()
        copy(0, sibling, me).wait_recv()
        for jj, chip in enumerate(chips):
            copy(4 + jj, (*chip, 1 - c), me).wait_recv()
        for cp in first + passed:
            cp.wait_send()
        mine.wait()

    return pl.pallas_call(
        body, name=name,
        out_shape=jax.ShapeDtypeStruct((8 * m_per, n), v.dtype),
        in_specs=[pl.BlockSpec(memory_space=pltpu.VMEM)],
        out_specs=pl.BlockSpec(memory_space=pltpu.VMEM),
        scratch_shapes=[pltpu.SemaphoreType.DMA((7,)), pltpu.SemaphoreType.DMA((7,)), pltpu.SemaphoreType.DMA],
        compiler_params=pltpu.CompilerParams(vmem_limit_bytes=VMEM_LIMIT),
    )(v)


def sum_devices(g):
    def body(g_ref, o_ref):
        acc = g_ref[0]
        for d in range(1, 8):
            acc = acc + g_ref[d]
        o_ref[...] = acc

    return pl.pallas_call(body, name="sum_devices", out_shape=jax.ShapeDtypeStruct(g.shape[1:], g.dtype),
                          compiler_params=pltpu.CompilerParams(vmem_limit_bytes=VMEM_LIMIT))(g)


def send_other_half(grads):
    n = len(grads)

    def body(*refs):
        ins, outs = refs[:n], refs[n:2 * n]
        send_sems, recv_sems = refs[2 * n:]
        x, y, c, _ = _place()
        sib = (x, y, 1 - c)
        cps = [_rcopy(ins[a].at[:, :, 1 - c], outs[a], send_sems, recv_sems, a, sib) for a in range(n)]
        for cp in cps:
            cp.start()
        for cp in cps:
            cp.wait()

    return pl.pallas_call(
        body, name="send_other_half", in_specs=[ANY] * n, out_specs=[ANY] * n,
        out_shape=[jax.ShapeDtypeStruct(g.shape[:2] + g.shape[3:], g.dtype) for g in grads],
        scratch_shapes=[pltpu.SemaphoreType.DMA((n,)), pltpu.SemaphoreType.DMA((n,))],
        compiler_params=pltpu.CompilerParams(has_side_effects=True),
    )(*grads)


def scatter_to_chips(parts):
    n = len(parts)

    def body(*refs):
        ins, outs = refs[:n], refs[n:2 * n]
        send_sems, recv_sems = refs[2 * n:]
        x, y, c, chips = _place()
        cps = []
        for a in range(n):
            for jj, ch in enumerate(chips):
                cps.append(_rcopy(ins[a].at[2 * ch[0] + ch[1]], outs[a].at[jj], send_sems, recv_sems, 3 * a + jj, (*ch, c)))
        for cp in cps:
            cp.start()
        for cp in cps:
            cp.wait()

    return pl.pallas_call(
        body, name="scatter_to_chips", in_specs=[ANY] * n, out_specs=[ANY] * n,
        out_shape=[jax.ShapeDtypeStruct((3,) + p.shape[1:], p.dtype) for p in parts],
        scratch_shapes=[pltpu.SemaphoreType.DMA((3 * n,)), pltpu.SemaphoreType.DMA((3 * n,))],
        compiler_params=pltpu.CompilerParams(has_side_effects=True),
    )(*parts)


def share_halves(bufs, spans):
    n = len(bufs)

    def body(*refs):
        outs = refs[n:2 * n]
        send_sems, recv_sems = refs[2 * n:]
        x, y, c, _ = _place()
        sib = (x, y, 1 - c)

        def blk(a, half):
            return outs[a].at[pl.ds(spans[a][0], spans[a][1]), half]

        cps = [_rcopy(blk(a, c), blk(a, c), send_sems, recv_sems, a, sib) for a in range(n)]
        for cp in cps:
            cp.start()
        for a in range(n):
            _rcopy(blk(a, 1 - c), blk(a, 1 - c), send_sems, recv_sems, a, sib).wait_recv()
        for cp in cps:
            cp.wait_send()

    return pl.pallas_call(
        body, name="share_halves", in_specs=[ANY] * n, out_specs=[ANY] * n,
        out_shape=[jax.ShapeDtypeStruct(b.shape, b.dtype) for b in bufs],
        scratch_shapes=[pltpu.SemaphoreType.DMA((n,)), pltpu.SemaphoreType.DMA((n,))],
        input_output_aliases={a: a for a in range(n)},
        compiler_params=pltpu.CompilerParams(has_side_effects=True),
    )(*bufs)


def _block_rows(r, c, itemsize, budget=1 << 20):
    tr = r
    while tr * c * itemsize > budget and tr % 16 == 0:
        tr //= 2
    return tr


def _block_lr(l, r, c, budget=4 << 20):
    tr = _block_rows(r, c, 4, budget)
    lb = 1
    if tr == r:
        while l % (2 * lb) == 0 and 2 * lb * r * c * 4 <= budget:
            lb *= 2
    return lb, tr


def add_halves(g, got, c_idx):
    k4, l, _, rh, cc = g.shape
    lb, tr = _block_lr(l, rh, cc)

    def body(c_ref, g_ref, r_ref, o_ref):
        o_ref[...] = (g_ref[...].astype(F32) + r_ref[...].astype(F32)).astype(o_ref.dtype)

    return pl.pallas_call(
        body, name="add_halves",
        grid_spec=pltpu.PrefetchScalarGridSpec(
            num_scalar_prefetch=1, grid=(k4, l // lb, rh // tr),
            in_specs=[pl.BlockSpec((None, lb, None, tr, cc), lambda k, i, b, c_ref: (k, i, c_ref[0], b, 0)),
                      pl.BlockSpec((None, lb, tr, cc), lambda k, i, b, c_ref: (k, i, b, 0))],
            out_specs=pl.BlockSpec((None, lb, tr, cc), lambda k, i, b, c_ref: (k, i, b, 0))),
        out_shape=jax.ShapeDtypeStruct(got.shape, WIRE),
        compiler_params=_cp(("parallel", "parallel", "parallel")),
    )(c_idx, g, got)


def sum_chips(own, got, kc_idx, full, first):
    _, l, rh, cc = own.shape
    lb, tr = _block_lr(l, rh, cc, 2 << 20)
    assert first % lb == 0

    def body(k_ref, o_ref, r_ref, _full, s_ref):
        s_ref[...] = ((o_ref[...].astype(F32) + r_ref[0].astype(F32)) + r_ref[1].astype(F32)) + r_ref[2].astype(F32)

    return pl.pallas_call(
        body, name="sum_chips",
        grid_spec=pltpu.PrefetchScalarGridSpec(
            num_scalar_prefetch=1, grid=(l // lb, rh // tr),
            in_specs=[pl.BlockSpec((None, lb, tr, cc), lambda i, b, k_ref: (k_ref[0], i, b, 0)),
                      pl.BlockSpec((3, lb, tr, cc), lambda i, b, k_ref: (0, i, b, 0)),
                      ANY],
            out_specs=pl.BlockSpec((lb, None, tr, cc), lambda i, b, k_ref: (first // lb + i, k_ref[1], b, 0))),
        out_shape=jax.ShapeDtypeStruct(full.shape, F32),
        input_output_aliases={3: 0},
        compiler_params=_cp(("parallel", "parallel")),
    )(kc_idx, own, got, full)


def _adam_math(w, g, m, v):
    m = ADAM_B1 * m + (1.0 - ADAM_B1) * g
    v = ADAM_B2 * v + (1.0 - ADAM_B2) * (g * g)
    m_hat = m / (1.0 - ADAM_B1 ** ADAM_STEP)
    v_hat = v / (1.0 - ADAM_B2 ** ADAM_STEP)
    delta = -ADAM_LR * (m_hat / (jnp.sqrt(v_hat) + ADAM_EPS) + ADAM_WD * w)
    return delta, m, v


def adamw(w, g, m, v):
    l, r, c = w.shape
    tr = _block_rows(r, c, 4, 2 << 20)

    def body(w_ref, g_ref, m_ref, v_ref, go_ref, d_ref, mo_ref, vo_ref):
        gv = g_ref[...]
        go_ref[...] = gv
        d_ref[...], mo_ref[...], vo_ref[...] = _adam_math(w_ref[...], gv, m_ref[...], v_ref[...])

    spec = pl.BlockSpec((None, tr, c), lambda i, b: (i, b, 0))
    return pl.pallas_call(
        body, name="adamw", grid=(l, r // tr), in_specs=[spec] * 4, out_specs=[spec] * 4,
        out_shape=[jax.ShapeDtypeStruct(w.shape, F32)] * 4, compiler_params=_cp(("parallel", "parallel")),
    )(w, g, m, v)


def adamw_small(w, g, m, v):
    def body(w_ref, g_ref, m_ref, v_ref, d_ref, mo_ref, vo_ref):
        d_ref[...], mo_ref[...], vo_ref[...] = _adam_math(w_ref[...], g_ref[...], m_ref[...], v_ref[...])

    return pl.pallas_call(body, name="adamw_small", out_shape=[jax.ShapeDtypeStruct(w.shape, F32)] * 3)(w, g, m, v)


def kernel(x, p, w_in, w_out, g_pre, g_post, pool_w, pool_b, pool_scale, conv_w, conv_b, lru_wa, lru_ba, lru_wx, lru_bx, lru_L, w_ple, w_ple_gate, g_ple_in, g_ple_out, loss_target, m_w_in, m_w_out, m_g_pre, m_g_post, m_pool_w, m_pool_b, m_pool_scale, m_conv_w, m_conv_b, m_lru_wa, m_lru_ba, m_lru_wx, m_lru_bx, m_lru_L, m_w_ple, m_w_ple_gate, m_g_ple_in, m_g_ple_out, v_w_in, v_w_out, v_g_pre, v_g_post, v_pool_w, v_pool_b, v_pool_scale, v_conv_w, v_conv_b, v_lru_wa, v_lru_ba, v_lru_wx, v_lru_bx, v_lru_L, v_w_ple, v_w_ple_gate, v_g_ple_in, v_g_ple_out):
    depth = w_in.shape[0]
    _, s, d = x.shape
    e = 2 * d
    kp = p.shape[-1]
    nmix = pool_w.shape[0]
    ngrp = pool_w.shape[1]
    cg = e // ngrp
    cb = e // LRU_HEADS
    xi, yi, ci = lax.axis_index("x"), lax.axis_index("y"), lax.axis_index("c")
    me = 2 * xi + yi
    c_idx = jnp.reshape(ci, (1,)).astype(jnp.int32)
    k_idx = jnp.reshape(me, (1,)).astype(jnp.int32)
    tr_row = _tile(s, 256)
    tr_mix = _tile(s, 512)
    tm = _tile(s, 1024)
    tm2 = _tile(s, 2048)

    def halves(a):
        return a.reshape(a.shape[0], 2, a.shape[1] // 2, a.shape[2])

    big = {
        "w_in": w_in, "w_out": w_out, "gate": w_ple_gate, "ple": w_ple,
        "pool": pool_w.reshape(nmix * ngrp, cg // 4, cg),
        "wa": lru_wa.reshape(nmix * LRU_HEADS, cb // 4, cb), "wx": lru_wx.reshape(nmix * LRU_HEADS, cb // 4, cb),
    }
    names = list(big)

    def layer_shards(i):
        sh = {"w_in": w_in[i][None], "w_out": w_out[i][None], "gate": w_ple_gate[i][None], "ple": w_ple[i][None]}
        if i % 2 == 0:
            sh["pool"] = pool_w[i // 2]
        else:
            sh["wa"], sh["wx"] = lru_wa[i // 2], lru_wx[i // 2]
        return sh

    def mixer_names(i):
        return ["pool"] if i % 2 == 0 else ["wa", "wx"]

    wbuf = [{n: cast_into_slab(w, k_idx) for n, w in layer_shards(i).items()} for i in range(depth)]
    first = list(wbuf[0])
    wbuf[0] = dict(zip(first, gather_weights([wbuf[0][n] for n in first])))

    def full_w(i, n):
        b = wbuf[i][n]
        return b.reshape(b.shape[0], b.shape[1], 2 * b.shape[3], b.shape[4])

    def run_mm(stages, *args):
        if not stages:
            return _matmul(*args)
        out, new = _matmul(*args, hosted=[mk([wbuf[l][n] for n in nms]) for mk, l, nms in stages])
        slots = [(l, n) for _, l, nms in stages for n in nms]
        for (l, n), b in zip(slots, new):
            wbuf[l][n] = b
        return out

    ec = e // 4
    small_loc = jnp.concatenate([conv_w, conv_b[:, None], lru_ba[:, None], lru_bx[:, None], lru_L[:, None]], axis=1)
    sm_all = all_gather_small(small_loc.reshape(nmix * 8, ec), "gather_small").reshape(4, 2, nmix, 8, ec)
    lru_small = jnp.transpose(sm_all[:, 0], (1, 2, 0, 3)).reshape(nmix, 8, e)

    xs = x[0]
    saved = []
    for i in range(depth):
        j = i // 2
        h = rms_fwd(xs, g_pre[i][None], tr_row) if i == 0 else h_next
        nj = (2 * e) // 1024 if (2 * e) % 1024 == 0 else 4
        tn = (2 * e) // nj
        per = e // tn
        perk = (e // 2) // tn
        nxt = i + 1 if i + 1 < depth else None
        stages = [(_stage_gather_d2d, i, ["gate", "ple"])] if i > 0 else []
        if nxt is not None:
            stages.append((_stage_gather_ici, nxt, ["w_in"]))
        uz = run_mm(
            stages, "mm_in", "nn", (s // tm2, nj, 1), h, pl.BlockSpec((tm2, d), lambda a, b, k: (a, 0)),
            full_w(i, "w_in"), pl.BlockSpec((None, None, d, tn), lambda a, b, k, perk=perk: (b // perk, 0, 0, b % perk)),
            jax.ShapeDtypeStruct((2, s, e), MXU), pl.BlockSpec((None, tm2, tn), lambda a, b, k, per=per: (b // per, a, b % per)),
            (8, 128))
        if i % 2 == 0:
            y2 = pool_fwd(uz, full_w(i, "pool"), pool_b[j][None], pool_scale[j][None], tr_mix)
            hst = None
        else:
            y2, hst = lru_fwd(uz, full_w(i, "wa"), full_w(i, "wx"), lru_small[j], tr_mix)
        tn_o = _tile(d, 1024)
        stages = [(_stage_gather_ici, nxt, ["w_out"] + mixer_names(nxt))] if nxt is not None else []
        o = run_mm(
            stages, "mm_out", "nn", (s // tm, d // tn_o, 1), y2, pl.BlockSpec((tm, e), lambda a, b, k: (a, 0)),
            full_w(i, "w_out"), pl.BlockSpec((4, None, e // 4, tn_o), lambda a, b, k: (0, 0, 0, b)),
            jax.ShapeDtypeStruct((s, d), MXU), pl.BlockSpec((tm, tn_o), lambda a, b, k: (a, b)), (8, 128))
        x1, hn = res_rms_fwd(xs, o, g_post[i][None], g_ple_in[i][None], tr_row)
        stages = []
        if nxt is not None:
            stages = [(_stage_gather_ici, nxt, ["gate", "ple"]), (_stage_gather_d2d, nxt, ["w_in", "w_out"] + mixer_names(nxt))]
        gpre = run_mm(
            stages, "mm_gate", "nn", (s // tm2, d // tn_o, 1), hn, pl.BlockSpec((tm2, d), lambda a, b, k: (a, 0)),
            full_w(i, "gate"), pl.BlockSpec((4, None, d // 4, tn_o), lambda a, b, k: (0, 0, 0, b)),
            jax.ShapeDtypeStruct((s, d), MXU), pl.BlockSpec((tm2, tn_o), lambda a, b, k: (a, b)), (8, 128))
        pe = p[i, 0]
        ev = _matmul(
            "mm_ple", "nn", (s // tm, 4, 1), pe, pl.BlockSpec((tm, kp), lambda a, b, k: (a, 0)),
            full_w(i, "ple"), pl.BlockSpec((None, None, kp, d // 4), lambda a, b, k: (b, 0, 0, 0)),
            jax.ShapeDtypeStruct((s, d), MXU), pl.BlockSpec((tm, d // 4), lambda a, b, k: (a, b)), (8, 128))
        if nxt is not None:
            x2, h_next = ple_fwd(x1, gpre, ev, g_ple_out[i][None], g_pre[nxt][None], tr_row)
        else:
            x2 = ple_fwd_last(x1, gpre, ev, g_ple_out[i][None], tr_row)
        saved.append((xs, h, uz, y2, hst, o, x1, hn, gpre, ev))
        xs = x2

    dx, sq = loss_bwd(xs, loss_target[0], tr_row)
    d_gpre, d_gpost, d_gin, d_gout = [None] * depth, [None] * depth, [None] * depth, [None] * depth
    d_pool_b, d_pool_sc, d_lru_small = [None] * nmix, [None] * nmix, [None] * nmix
    ts = _tile(s, 1024)
    kc_idx = jnp.stack([me, ci]).astype(jnp.int32)
    full = {n: lax.empty(halves(big[n]).shape, F32) for n in names}
    sums, gots = [None] * depth, [None] * depth

    def scatter_stage(l, nms):
        return _stage_scatter_ici([sums[l][n] for n in nms], [gots[l][n] for n in nms]), [(l, n) for n in nms]

    def run_mm_scatter(specs, *args):
        if not specs:
            return _matmul(*args)
        out, new = _matmul(*args, hosted=[st for st, _ in specs])
        for (l, n), b in zip([slot for _, sl in specs for slot in sl], new):
            gots[l][n] = b
        return out

    def first_row(i, n):
        return i if n in ("w_in", "w_out", "gate", "ple") else (i // 2) * (ngrp if n == "pool" else LRU_HEADS)

    for i in reversed(range(depth)):
        j = i // 2
        prev = i + 1 if i + 1 < depth else None
        x0, h, uz, y2, hst, o, x1, hn, gpre, ev = saved[i]
        pe = p[i, 0]
        gl = {}
        de, dgp, d_gout[i] = ple_bwd(dx, gpre, ev, g_ple_out[i][None], tr_row)
        gl["ple"] = _matmul(
            "mm_dple", "tn", (1, 4, s // ts), pe, pl.BlockSpec((ts, kp), lambda a, b, k: (k, 0)),
            de, pl.BlockSpec((ts, d // 4), lambda a, b, k: (k, b)),
            jax.ShapeDtypeStruct((4, 1, kp, d // 4), WIRE), pl.BlockSpec((None, None, kp, d // 4), lambda a, b, k: (b, 0, 0, 0)),
            (kp, d // 4))
        tn_o = _tile(d, 1024)
        gl["gate"] = _matmul(
            "mm_dgate", "tn", (4, d // tn_o, 1), hn, pl.BlockSpec((s, d // 4), lambda a, b, k: (0, a)),
            dgp, pl.BlockSpec((s, tn_o), lambda a, b, k: (0, b)),
            jax.ShapeDtypeStruct((4, 1, d // 4, d), WIRE), pl.BlockSpec((None, None, d // 4, tn_o), lambda a, b, k: (a, 0, 0, b)),
            (8, 128))
        dhn = _matmul(
            "mm_dhn", "nt", (s // tm2, 4, 1), dgp, pl.BlockSpec((tm2, d), lambda a, b, k: (a, 0)),
            full_w(i, "gate"), pl.BlockSpec((None, None, d // 4, d), lambda a, b, k: (b, 0, 0, 0)),
            jax.ShapeDtypeStruct((s, d), MXU), pl.BlockSpec((tm2, d // 4), lambda a, b, k: (a, b)), (8, 128))
        dx1, do, d_gin[i], d_gpost[i] = post_bwd(dx, dhn, x1, g_ple_in[i][None], o, g_post[i][None], tr_row)
        gl["w_out"] = _matmul(
            "mm_dwout", "tn", (4, d // tn_o, 1), y2, pl.BlockSpec((s, e // 4), lambda a, b, k: (0, a)),
            do, pl.BlockSpec((s, tn_o), lambda a, b, k: (0, b)),
            jax.ShapeDtypeStruct((4, 1, e // 4, d), WIRE), pl.BlockSpec((None, None, e // 4, tn_o), lambda a, b, k: (a, 0, 0, b)),
            (8, 128))
        dy2 = _matmul(
            "mm_dy2", "nt", (s // tm2, 4, 1), do, pl.BlockSpec((tm2, d), lambda a, b, k: (a, 0)),
            full_w(i, "w_out"), pl.BlockSpec((None, None, e // 4, d), lambda a, b, k: (b, 0, 0, 0)),
            jax.ShapeDtypeStruct((s, e), MXU), pl.BlockSpec((tm2, e // 4), lambda a, b, k: (a, b)), (8, 128))
        if i % 2 == 0:
            duz, gl["pool"], d_pool_b[j], d_pool_sc[j] = pool_bwd(uz, dy2, full_w(i, "pool"), pool_b[j][None], pool_scale[j][None], tr_mix)
        else:
            duz, gl["wa"], gl["wx"], d_lru_small[j] = lru_bwd(uz, hst, dy2, full_w(i, "wa"), full_w(i, "wx"), lru_small[j], tr_mix)
        tmi = _tile(d, 1024)
        tni = _tile(e // 2, 1024)
        nslab = (e // 2) // tni
        gl["w_in"] = run_mm_scatter(
            [scatter_stage(prev, ["w_in"])] if prev is not None else [],
            "mm_dwin", "tn", (d // tmi, 4 * nslab, 1), h, pl.BlockSpec((s, tmi), lambda a, b, k: (0, a)),
            duz, pl.BlockSpec((None, s, tni), lambda a, b, k, nslab=nslab: (b // (2 * nslab), 0, b % (2 * nslab))),
            jax.ShapeDtypeStruct((4, 1, d, e // 2), WIRE),
            pl.BlockSpec((None, None, tmi, tni), lambda a, b, k, nslab=nslab: (b // nslab, 0, a, b % nslab)),
            (8, 128))
        lnames = list(gl)
        gparts = [gl[n].reshape(4, gl[n].shape[1], 2, gl[n].shape[2] // 2, gl[n].shape[3]) for n in lnames]
        hosted = [_stage_send_half(gparts, [lax.empty(g.shape[:2] + g.shape[3:], WIRE) for g in gparts])]
        rest = ["w_out", "gate", "ple"] + mixer_names(prev) if prev is not None else []
        if prev is not None:
            hosted.append(_stage_scatter_ici([sums[prev][n] for n in rest], [gots[prev][n] for n in rest]))
        tnd = _tile(d, 1024)
        dh, passed = _matmul(
            "mm_dh", "nt", (s // tm, d // tnd, 4), duz, pl.BlockSpec((None, tm, e // 2), lambda a, b, k: (k // 2, a, k % 2)),
            full_w(i, "w_in"), pl.BlockSpec((None, None, tnd, e // 2), lambda a, b, k: (k, 0, b, 0)),
            jax.ShapeDtypeStruct((s, d), MXU), pl.BlockSpec((tm, tnd), lambda a, b, k: (a, b)), (tm, tnd), hosted=hosted)
        from_sib = passed[:len(lnames)]
        for n, b in zip(rest, passed[len(lnames):]):
            gots[prev][n] = b
        dx, d_gpre[i] = rms_bwd_res(dx1, dh, x0, g_pre[i][None], tr_row)
        if prev is not None:
            for n in gots[prev]:
                full[n] = sum_chips(sums[prev][n], gots[prev][n], kc_idx, full[n], first_row(prev, n))
        sums[i] = {n: add_halves(g, r, c_idx) for n, g, r in zip(lnames, gparts, from_sib)}
        gots[i] = {n: lax.empty((3,) + sums[i][n].shape[1:], WIRE) for n in lnames}
    grad_x = dx[None]

    lnames = list(gots[0])
    for n, b in zip(lnames, scatter_to_chips([sums[0][n] for n in lnames])):
        full[n] = sum_chips(sums[0][n], b, kc_idx, full[n], first_row(0, n))
    shared = share_halves([full[n] for n in names], [(0, full[n].shape[0]) for n in names])
    grads = {n: f.reshape(big[n].shape) for n, f in zip(names, shared)}

    def rows_e(a):
        return jnp.stack(a).reshape(-1, e) if isinstance(a, list) else a.reshape(-1, e)

    pack = [rows_e([g[0] for g in d_gpre]), rows_e([g[0] for g in d_gpost]), rows_e([g[0] for g in d_gin]), rows_e([g[0] for g in d_gout]),
            jnp.concatenate(d_pool_b, axis=0), jnp.concatenate(d_pool_sc, axis=0), jnp.concatenate(d_lru_small, axis=0),
            jnp.pad(sq, ((0, 0), (0, e - d)))]
    sizes = [a.shape[0] for a in pack]
    packed = jnp.concatenate(pack, axis=0)
    nrow = packed.shape[0]
    nrow_p = -(-nrow // 8) * 8
    packed = jnp.pad(packed, ((0, nrow_p - nrow), (0, 0)))
    total = sum_devices(all_gather_small(packed, "gather_grads").reshape(8, nrow_p, e))
    parts, off = [], 0
    for n_ in sizes:
        parts.append(total[off:off + n_])
        off += n_
    t_gpre, t_gpost, t_gin, t_gout, t_pb, t_psc, t_lru, t_sq = parts
    loss = 0.5 * jnp.sum(t_sq) / d
    t_lru = t_lru.reshape(nmix, 8, e)
    t_lru_loc = lax.dynamic_slice_in_dim(t_lru, me * ec, ec, axis=2)

    def big_update(name, w, m, v):
        shp = big[name].shape
        g, dl, nm, nv = adamw(w.reshape(shp), grads[name], m.reshape(shp), v.reshape(shp))
        return [a.reshape(w.shape) for a in (g, dl, nm, nv)]

    def small_update(w, g, m, v):
        shp = w.shape
        w2 = w.reshape(-1, shp[-1])
        dl, nm, nv = adamw_small(w2, g.reshape(w2.shape), m.reshape(w2.shape), v.reshape(w2.shape))
        return [g.reshape(shp), dl.reshape(shp), nm.reshape(shp), nv.reshape(shp)]

    res = {
        "w_in": big_update("w_in", w_in, m_w_in, v_w_in),
        "w_out": big_update("w_out", w_out, m_w_out, v_w_out),
        "g_pre": small_update(g_pre, t_gpre.reshape(depth, d), m_g_pre, v_g_pre),
        "g_post": small_update(g_post, t_gpost.reshape(depth, d), m_g_post, v_g_post),
        "pool_w": big_update("pool", pool_w, m_pool_w, v_pool_w),
        "pool_b": small_update(pool_b, t_pb, m_pool_b, v_pool_b),
        "pool_scale": small_update(pool_scale, t_psc, m_pool_scale, v_pool_scale),
        "conv_w": small_update(conv_w, t_lru_loc[:, :CONV_W], m_conv_w, v_conv_w),
        "conv_b": small_update(conv_b, t_lru_loc[:, 4], m_conv_b, v_conv_b),
        "lru_wa": big_update("wa", lru_wa, m_lru_wa, v_lru_wa),
        "lru_ba": small_update(lru_ba, t_lru_loc[:, 5], m_lru_ba, v_lru_ba),
        "lru_wx": big_update("wx", lru_wx, m_lru_wx, v_lru_wx),
        "lru_bx": small_update(lru_bx, t_lru_loc[:, 6], m_lru_bx, v_lru_bx),
        "lru_L": small_update(lru_L, t_lru_loc[:, 7], m_lru_L, v_lru_L),
        "w_ple": big_update("ple", w_ple, m_w_ple, v_w_ple),
        "w_ple_gate": big_update("gate", w_ple_gate, m_w_ple_gate, v_w_ple_gate),
        "g_ple_in": small_update(g_ple_in, t_gin.reshape(depth, d), m_g_ple_in, v_g_ple_in),
        "g_ple_out": small_update(g_ple_out, t_gout.reshape(depth, d), m_g_ple_out, v_g_ple_out),
    }
    order = ["w_in", "w_out", "g_pre", "g_post", "pool_w", "pool_b", "pool_scale", "conv_w", "conv_b", "lru_wa", "lru_ba",
             "lru_wx", "lru_bx", "lru_L", "w_ple", "w_ple_gate", "g_ple_in", "g_ple_out"]
    out = [loss, grad_x]
    for slot in range(4):
        out += [res[n][slot] for n in order]
    return tuple(out)
```

```python
import functools

import jax
import jax.numpy as jnp
from jax import lax
from jax.experimental import pallas as pl
from jax.experimental.pallas import tpu as pltpu

F32 = jnp.float32
MXU = jnp.bfloat16
WIRE = jnp.bfloat16
VMEM_LIMIT = 56 * 1024 * 1024
RMS_EPS = 1e-6
LRU_C = 8.0
POOL_WINDOWS = (2, 4, 8, 16)
MAXW = 16
CONV_W = 4
LRU_HEADS = 16
ADAM_LR, ADAM_B1, ADAM_B2, ADAM_EPS, ADAM_WD, ADAM_STEP = 0.001, 0.9, 0.999, 1e-08, 0.01, 10
MESHID = pl.DeviceIdType.MESH
ANY = pl.BlockSpec(memory_space=pl.ANY)


def _cp(sem=None):
    return pltpu.CompilerParams(dimension_semantics=sem, vmem_limit_bytes=VMEM_LIMIT)


def _sig(v):
    return 0.5 * jnp.tanh(0.5 * v) + 0.5


def _tile(n, pref):
    return pref if n % pref == 0 else n


_DN = {"nn": (((1,), (0,)), ((), ())), "nt": (((1,), (1,)), ((), ())), "tn": (((0,), (0,)), ((), ()))}


def _matmul(name, mode, grid, a, a_spec, b, b_spec, out_shape, out_spec, acc_shape, hosted=()):
    nk = grid[2]
    ro = [r for st in hosted for r in st["ro"]]
    bufs = [r for st in hosted for r in st["bufs"]]
    nro, nbuf = len(ro), len(bufs)

    def body(a_ref, b_ref, *rest):
        ro_refs = rest[:nro]
        o_ref = rest[nro + nbuf]
        buf_refs = rest[nro + nbuf + 1:nro + 2 * nbuf + 1]
        acc_ref = rest[nro + 2 * nbuf + 1]
        ids = [pl.program_id(t) for t in range(3)]
        kk = ids[2]

        def run_stages(what):
            r0 = b0 = s0 = 0
            for st in hosted:
                if what in st:
                    st[what](ro_refs[r0:r0 + len(st["ro"])], buf_refs[b0:b0 + len(st["bufs"])], rest[-2], rest[-1], s0)
                r0, b0, s0 = r0 + len(st["ro"]), b0 + len(st["bufs"]), s0 + st["nsem"]

        if hosted:
            step = (ids[0] * grid[1] + ids[1]) * grid[2] + ids[2]

            @pl.when(step == 0)
            def _():
                run_stages("start")

            if any("mid" in st for st in hosted):
                @pl.when(step == (grid[0] * grid[1] * grid[2]) // 2)
                def _():
                    run_stages("mid")

        bv = b_ref[...]
        if bv.ndim == 3:
            bv = bv.reshape(bv.shape[0] * bv.shape[1], bv.shape[2])
        prod = lax.dot_general(a_ref[...].astype(MXU), bv.astype(MXU), _DN[mode], preferred_element_type=F32)
        if nk == 1:
            o_ref[...] = prod.astype(o_ref.dtype)
        else:
            @pl.when(kk == 0)
            def _():
                acc_ref[...] = prod

            @pl.when(kk > 0)
            def _():
                acc_ref[...] += prod

            @pl.when(kk == nk - 1)
            def _():
                o_ref[...] = acc_ref[...].astype(o_ref.dtype)

        if hosted:
            @pl.when((ids[0] == grid[0] - 1) & (ids[1] == grid[1] - 1) & (ids[2] == grid[2] - 1))
            def _():
                run_stages("finish")

    scratch = [pltpu.VMEM(acc_shape, F32)]
    if not hosted:
        return pl.pallas_call(
            body, name=name, grid=grid, in_specs=[a_spec, b_spec], out_specs=out_spec, out_shape=out_shape,
            scratch_shapes=scratch, compiler_params=_cp(("parallel", "parallel", "arbitrary")),
        )(a, b)
    nsem = sum(st["nsem"] for st in hosted)
    res = pl.pallas_call(
        body, name=name, grid=grid,
        in_specs=[a_spec, b_spec] + [ANY] * (nro + nbuf), out_specs=[out_spec] + [ANY] * nbuf,
        out_shape=[out_shape] + [jax.ShapeDtypeStruct(x.shape, x.dtype) for x in bufs],
        scratch_shapes=scratch + [pltpu.SemaphoreType.DMA((nsem,)), pltpu.SemaphoreType.DMA((nsem,))],
        input_output_aliases={2 + nro + t: 1 + t for t in range(nbuf)},
        compiler_params=_cp(("arbitrary", "arbitrary", "arbitrary")),
    )(a, b, *ro, *bufs)
    return res[0], list(res[1:])


def _rows_call(name, body, ins, in_rows, outs, out_rows, n_rows, tr):
    def spec(shape, tiled):
        if tiled:
            return pl.BlockSpec((tr, shape[1]), lambda i: (i, 0))
        return pl.BlockSpec(shape, lambda i: (0, 0))

    return pl.pallas_call(
        body, name=name, grid=(n_rows // tr,),
        in_specs=[spec(a.shape, t) for a, t in zip(ins, in_rows)],
        out_specs=[spec(o.shape, t) for o, t in zip(outs, out_rows)],
        out_shape=outs, compiler_params=_cp(("arbitrary",)),
    )(*ins)


def _rstd(v):
    return lax.rsqrt(jnp.mean(v * v, axis=-1, keepdims=True) + RMS_EPS)


def _norm_bwd(v, g, dy):
    r = _rstd(v)
    n = v * r
    dn = dy * g
    dv = r * (dn - n * jnp.mean(dn * n, axis=-1, keepdims=True))
    return dv, jnp.sum(dy * n, axis=0, keepdims=True)


def _acc_rows(ref, val):
    @pl.when(pl.program_id(0) == 0)
    def _():
        ref[...] = val

    @pl.when(pl.program_id(0) > 0)
    def _():
        ref[...] += val


def rms_fwd(x, g, tr):
    def body(x_ref, g_ref, o_ref):
        v = x_ref[...]
        o_ref[...] = (v * _rstd(v) * g_ref[...]).astype(o_ref.dtype)

    return _rows_call("rms_fwd", body, [x, g], [True, False], [jax.ShapeDtypeStruct(x.shape, MXU)], [True], x.shape[0], tr)[0]


def res_rms_fwd(x, o, g, g_next, tr):
    def body(x_ref, o_ref, g_ref, gn_ref, y_ref, h_ref):
        v = o_ref[...].astype(F32)
        y = x_ref[...] + v * _rstd(v) * g_ref[...]
        y_ref[...] = y
        h_ref[...] = (y * _rstd(y) * gn_ref[...]).astype(h_ref.dtype)

    return _rows_call("res_rms_fwd", body, [x, o, g, g_next], [True, True, False, False],
                      [jax.ShapeDtypeStruct(x.shape, F32), jax.ShapeDtypeStruct(x.shape, MXU)], [True, True], x.shape[0], tr)


def ple_fwd(x1, gpre, e, g, g_next, tr):
    def body(x_ref, gp_ref, e_ref, g_ref, gn_ref, y_ref, h_ref):
        v = e_ref[...].astype(F32) * _sig(gp_ref[...].astype(F32))
        y = x_ref[...] + v * _rstd(v) * g_ref[...]
        y_ref[...] = y
        h_ref[...] = (y * _rstd(y) * gn_ref[...]).astype(h_ref.dtype)

    return _rows_call("ple_fwd", body, [x1, gpre, e, g, g_next], [True, True, True, False, False],
                      [jax.ShapeDtypeStruct(x1.shape, F32), jax.ShapeDtypeStruct(x1.shape, MXU)], [True, True], x1.shape[0], tr)


def ple_fwd_last(x1, gpre, e, g, tr):
    def body(x_ref, gp_ref, e_ref, g_ref, y_ref):
        v = e_ref[...].astype(F32) * _sig(gp_ref[...].astype(F32))
        y_ref[...] = x_ref[...] + v * _rstd(v) * g_ref[...]

    return _rows_call("ple_fwd_last", body, [x1, gpre, e, g], [True, True, True, False],
                      [jax.ShapeDtypeStruct(x1.shape, F32)], [True], x1.shape[0], tr)[0]


def loss_bwd(y, target, tr):
    d = y.shape[1]

    def body(y_ref, t_ref, dy_ref, sq_ref):
        diff = y_ref[...] - t_ref[...]
        dy_ref[...] = diff * (1.0 / d)
        _acc_rows(sq_ref, jnp.sum(diff * diff, axis=0, keepdims=True))

    return _rows_call("loss_bwd", body, [y, target], [True, True],
                      [jax.ShapeDtypeStruct(y.shape, F32), jax.ShapeDtypeStruct((1, d), F32)], [True, False], y.shape[0], tr)


def ple_bwd(dx2, gpre, e, g, tr):
    d = dx2.shape[1]

    def body(dx_ref, gp_ref, e_ref, g_ref, de_ref, dgp_ref, dg_ref):
        gate = _sig(gp_ref[...].astype(F32))
        ev = e_ref[...].astype(F32)
        dv, dg = _norm_bwd(ev * gate, g_ref[...], dx_ref[...])
        de_ref[...] = (dv * gate).astype(de_ref.dtype)
        dgp_ref[...] = (dv * ev * gate * (1.0 - gate)).astype(dgp_ref.dtype)
        _acc_rows(dg_ref, dg)

    return _rows_call("ple_bwd", body, [dx2, gpre, e, g], [True, True, True, False],
                      [jax.ShapeDtypeStruct(dx2.shape, MXU), jax.ShapeDtypeStruct(dx2.shape, MXU), jax.ShapeDtypeStruct((1, d), F32)],
                      [True, True, False], dx2.shape[0], tr)


def rms_bwd_res(dres, dh, x, g, tr):
    d = x.shape[1]

    def body(dr_ref, dh_ref, x_ref, g_ref, dx_ref, dg_ref):
        dv, dg = _norm_bwd(x_ref[...], g_ref[...], dh_ref[...].astype(F32))
        dx_ref[...] = dr_ref[...] + dv
        _acc_rows(dg_ref, dg)

    return _rows_call("rms_bwd_res", body, [dres, dh, x, g], [True, True, True, False],
                      [jax.ShapeDtypeStruct(x.shape, F32), jax.ShapeDtypeStruct((1, d), F32)], [True, False], x.shape[0], tr)


def post_bwd(dres, dh, x, g, o, g_o, tr):
    d = x.shape[1]

    def body(dr_ref, dh_ref, x_ref, g_ref, o_ref, go_ref, dx_ref, do_ref, dg_ref, dgo_ref):
        dv, dg = _norm_bwd(x_ref[...], g_ref[...], dh_ref[...].astype(F32))
        dxv = dr_ref[...] + dv
        dx_ref[...] = dxv
        dov, dgo = _norm_bwd(o_ref[...].astype(F32), go_ref[...], dxv)
        do_ref[...] = dov.astype(do_ref.dtype)
        _acc_rows(dg_ref, dg)
        _acc_rows(dgo_ref, dgo)

    return _rows_call("post_bwd", body, [dres, dh, x, g, o, g_o], [True, True, True, False, True, False],
                      [jax.ShapeDtypeStruct(x.shape, F32), jax.ShapeDtypeStruct(x.shape, MXU),
                       jax.ShapeDtypeStruct((1, d), F32), jax.ShapeDtypeStruct((1, d), F32)], [True, True, False, False], x.shape[0], tr)


def _trailing_sum(ext, w):
    s, k = ext, 1
    while k < w:
        s = s + pltpu.roll(s, k, 0)
        k *= 2
    return s


def _leading_sum(ext, w):
    n = ext.shape[0]
    s, k = ext, 1
    while k < w:
        s = s + pltpu.roll(s, n - k, 0)
        k *= 2
    return s


def _pool_inv_count(rb, tr, w, c):
    t = rb * tr + lax.broadcasted_iota(jnp.int32, (tr, c), 0)
    return 1.0 / jnp.minimum(t + 1, w).astype(F32)


def _pool_d(u_ref, up_ref, rb, tr, w):
    cur = u_ref[...].astype(F32)
    prev = jnp.where(rb > 0, up_ref[...].astype(F32), 0.0)
    ext = jnp.concatenate([prev, cur], axis=0)
    win = _trailing_sum(ext, w)[MAXW:]
    return win * _pool_inv_count(rb, tr, w, cur.shape[1]) - cur


def pool_fwd(uz, w_g, bias, scale, tr):
    _, s, e = uz.shape
    ng = len(POOL_WINDOWS)
    cg = e // ng
    nb = s // tr
    hb = tr // MAXW

    def body(u_ref, up_ref, z_ref, w_ref, b_ref, sc_ref, o_ref):
        g, rb = pl.program_id(0), pl.program_id(1)
        wmat = w_ref[...].reshape(cg, cg)
        for gg, win in enumerate(POOL_WINDOWS):
            @pl.when(g == gg)
            def _(win=win):
                d = _pool_d(u_ref, up_ref, rb, tr, win)
                y = (jnp.dot(d.astype(MXU), wmat, preferred_element_type=F32) + b_ref[...]) * sc_ref[...]
                z = z_ref[...].astype(F32)
                o_ref[...] = (y * z * _sig(z)).astype(o_ref.dtype)

    return pl.pallas_call(
        body, name="pool_fwd", grid=(ng, nb),
        in_specs=[
            pl.BlockSpec((None, tr, cg), lambda g, r: (0, r, g)),
            pl.BlockSpec((None, MAXW, cg), lambda g, r: (0, jnp.maximum(r * hb - 1, 0), g)),
            pl.BlockSpec((None, tr, cg), lambda g, r: (1, r, g)),
            pl.BlockSpec((4, None, cg // 4, cg), lambda g, r: (0, g, 0, 0)),
            pl.BlockSpec((1, cg), lambda g, r: (0, g)),
            pl.BlockSpec((1, cg), lambda g, r: (0, g)),
        ],
        out_specs=pl.BlockSpec((tr, cg), lambda g, r: (r, g)),
        out_shape=jax.ShapeDtypeStruct((s, e), MXU),
        compiler_params=_cp(("parallel", "arbitrary")),
    )(uz, uz, uz, w_g, bias, scale)


def pool_bwd(uz, dy2, w_g, bias, scale, tr):
    _, s, e = uz.shape
    ng = len(POOL_WINDOWS)
    cg = e // ng
    nb = s // tr
    hb = tr // MAXW

    def body(u_ref, up_ref, z_ref, dy_ref, w_ref, b_ref, sc_ref, duz_ref, gw_ref, db_ref, dsc_ref, acc_ref, carry_ref):
        g, step = pl.program_id(0), pl.program_id(1)
        rb = nb - 1 - step
        wmat = w_ref[...].reshape(cg, cg)
        for gg, win in enumerate(POOL_WINDOWS):
            @pl.when(g == gg)
            def _(win=win):
                d = _pool_d(u_ref, up_ref, rb, tr, win).astype(MXU)
                ypre = jnp.dot(d, wmat, preferred_element_type=F32) + b_ref[...]
                z = z_ref[...].astype(F32)
                sg = _sig(z)
                dy2v = dy_ref[...].astype(F32)
                dyv = dy2v * z * sg
                duz_ref[1] = (dy2v * ypre * sc_ref[...] * sg * (1.0 + z * (1.0 - sg))).astype(duz_ref.dtype)
                dypre = dyv * sc_ref[...]
                dsc = jnp.sum(dyv * ypre, axis=0, keepdims=True)
                dbv = jnp.sum(dypre, axis=0, keepdims=True)
                dypre_b = dypre.astype(MXU)
                dd = lax.dot_general(dypre_b, wmat, _DN["nt"], preferred_element_type=F32)
                gw = lax.dot_general(d, dypre_b, _DN["tn"], preferred_element_type=F32)
                q = dd * _pool_inv_count(rb, tr, win, cg)
                nxt = jnp.where(step > 0, carry_ref[...], 0.0)
                lead = _leading_sum(jnp.concatenate([q, nxt], axis=0), win)[:tr]
                duz_ref[0] = (lead - dd).astype(duz_ref.dtype)
                carry_ref[...] = q[:MAXW]

                @pl.when(step == 0)
                def _():
                    acc_ref[...] = gw
                    db_ref[...] = dbv
                    dsc_ref[...] = dsc

                @pl.when(step > 0)
                def _():
                    acc_ref[...] += gw
                    db_ref[...] += dbv
                    dsc_ref[...] += dsc

                @pl.when(step == nb - 1)
                def _():
                    gw_ref[...] = acc_ref[...].reshape(4, cg // 4, cg).astype(gw_ref.dtype)

    return pl.pallas_call(
        body, name="pool_bwd", grid=(ng, nb),
        in_specs=[
            pl.BlockSpec((None, tr, cg), lambda g, r: (0, nb - 1 - r, g)),
            pl.BlockSpec((None, MAXW, cg), lambda g, r: (0, jnp.maximum((nb - 1 - r) * hb - 1, 0), g)),
            pl.BlockSpec((None, tr, cg), lambda g, r: (1, nb - 1 - r, g)),
            pl.BlockSpec((tr, cg), lambda g, r: (nb - 1 - r, g)),
            pl.BlockSpec((4, None, cg // 4, cg), lambda g, r: (0, g, 0, 0)),
            pl.BlockSpec((1, cg), lambda g, r: (0, g)),
            pl.BlockSpec((1, cg), lambda g, r: (0, g)),
        ],
        out_specs=[
            pl.BlockSpec((2, tr, cg), lambda g, r: (0, nb - 1 - r, g)),
            pl.BlockSpec((4, None, cg // 4, cg), lambda g, r: (0, g, 0, 0)),
            pl.BlockSpec((1, cg), lambda g, r: (0, g)),
            pl.BlockSpec((1, cg), lambda g, r: (0, g)),
        ],
        out_shape=[
            jax.ShapeDtypeStruct((2, s, e), MXU),
            jax.ShapeDtypeStruct(w_g.shape, WIRE),
            jax.ShapeDtypeStruct((1, e), F32),
            jax.ShapeDtypeStruct((1, e), F32),
        ],
        scratch_shapes=[pltpu.VMEM((cg, cg), F32), pltpu.VMEM((MAXW, cg), F32)],
        compiler_params=_cp(("parallel", "arbitrary")),
    )(uz, uz, uz, dy2, w_g, bias, scale)


HALO = 16


def _one_minus_sq(log_a, a):
    poly = (-2.0 * log_a) * (1.0 + log_a * (1.0 + log_a * (2.0 / 3.0)))
    return jnp.where(log_a > -0.01, poly, 1.0 - a * a)


def _softplus_neg(lam):
    t = jnp.exp(-jnp.abs(lam))
    log1p = jnp.where(t < 1e-3, t * (1.0 - t * (0.5 - t * (1.0 / 3.0))), jnp.log(1.0 + t))
    return jnp.maximum(-lam, 0.0) + log1p, _sig(-lam)


def _lru_gates(u_ref, up_ref, rb, sm_ref, wa, wx):
    cur = u_ref[...].astype(F32)
    prev = jnp.where(rb > 0, up_ref[...].astype(F32), 0.0)
    ext = jnp.concatenate([prev, cur], axis=0)
    taps = [cur] + [pltpu.roll(ext, k, 0)[HALO:] for k in range(1, CONV_W)]
    uc = sm_ref[CONV_W:CONV_W + 1, :]
    for k in range(CONV_W):
        uc = uc + taps[k] * sm_ref[CONV_W - 1 - k:CONV_W - k, :]
    ucb = uc.astype(MXU)
    r = _sig(jnp.dot(ucb, wa, preferred_element_type=F32) + sm_ref[5:6, :])
    ig = _sig(jnp.dot(ucb, wx, preferred_element_type=F32) + sm_ref[6:7, :])
    sp, sgn = _softplus_neg(sm_ref[7:8, :])
    log_a = r * (-LRU_C * sp)
    a = jnp.exp(log_a)
    mult = jnp.sqrt(jnp.maximum(_one_minus_sq(log_a, a), 0.0))
    return taps, uc, ucb, r, ig, sp, sgn, a, mult


LANES = 128


def _seg_scan(a, b, out_ref, scr, state, reverse):
    a_s, b_s, h_s, p_s = scr
    tr, c = a.shape
    seg = tr // 8
    nl = c // LANES
    for l in range(nl):
        a_s[l] = a[:, l * LANES:(l + 1) * LANES]
        b_s[l] = b[:, l * LANES:(l + 1) * LANES]
    h = [jnp.zeros((8, LANES), F32)] * nl
    pp = [jnp.ones((8, LANES), F32)] * nl
    for i in (range(seg - 1, -1, -1) if reverse else range(seg)):
        rows = pl.ds(i, 8, stride=seg)
        for l in range(nl):
            av = a_s[l, rows, :]
            h[l] = av * h[l] + b_s[l, rows, :]
            pp[l] = av * pp[l]
            h_s[l, pl.ds(8 * i, 8), :] = h[l]
            p_s[l, pl.ds(8 * i, 8), :] = pp[l]
    leaving = []
    for l in range(nl):
        lanes = slice(l * LANES, (l + 1) * LANES)
        st = state[:, lanes]
        for sgm in (range(7, -1, -1) if reverse else range(8)):
            for t0 in range(0, seg, 8):
                rows = pl.ds(8 * t0 + sgm, 8, stride=8)
                out_ref[pl.ds(sgm * seg + t0, 8), lanes] = h_s[l, rows, :] + p_s[l, rows, :] * st
            st = h[l][sgm:sgm + 1, :] + pp[l][sgm:sgm + 1, :] * st
        leaving.append(st)
    return jnp.concatenate(leaving, axis=1)


def lru_fwd(uz, wa_g, wx_g, small, tr):
    _, s, e = uz.shape
    cb = e // LRU_HEADS
    nb = s // tr
    hb = tr // HALO

    def body(u_ref, up_ref, z_ref, wa_ref, wx_ref, sm_ref, o_ref, h_ref, s0, s1, s2, s3, carry_ref):
        rb = pl.program_id(1)
        wa = wa_ref[...].reshape(cb, cb)
        wx = wx_ref[...].reshape(cb, cb)
        _, uc, _, _, ig, _, _, a, mult = _lru_gates(u_ref, up_ref, rb, sm_ref, wa, wx)
        start = jnp.where(rb > 0, carry_ref[0:1, :], 0.0)
        last = _seg_scan(a, mult * ig * uc, h_ref, (s0, s1, s2, s3), start, False)
        carry_ref[...] = jnp.broadcast_to(last, carry_ref.shape)
        z = z_ref[...].astype(F32)
        o_ref[...] = (h_ref[...] * z * _sig(z)).astype(o_ref.dtype)

    wspec = pl.BlockSpec((4, None, cb // 4, cb), lambda h, r: (0, h, 0, 0))
    return pl.pallas_call(
        body, name="lru_fwd", grid=(LRU_HEADS, nb),
        in_specs=[
            pl.BlockSpec((None, tr, cb), lambda h, r: (0, r, h)),
            pl.BlockSpec((None, HALO, cb), lambda h, r: (0, jnp.maximum(r * hb - 1, 0), h)),
            pl.BlockSpec((None, tr, cb), lambda h, r: (1, r, h)),
            wspec, wspec,
            pl.BlockSpec((8, cb), lambda h, r: (0, h)),
        ],
        out_specs=[pl.BlockSpec((tr, cb), lambda h, r: (r, h)), pl.BlockSpec((tr, cb), lambda h, r: (r, h))],
        out_shape=[jax.ShapeDtypeStruct((s, e), MXU), jax.ShapeDtypeStruct((s, e), F32)],
        scratch_shapes=[pltpu.VMEM((cb // LANES, tr, LANES), F32)] * 4 + [pltpu.VMEM((8, cb), F32)],
        compiler_params=_cp(("parallel", "arbitrary")),
    )(uz, uz, uz, wa_g, wx_g, small)


def lru_bwd(uz, hst, dy2, wa_g, wx_g, small, tr):
    _, s, e = uz.shape
    cb = e // LRU_HEADS
    nb = s // tr
    hb = tr // HALO

    def body(u_ref, up_ref, z_ref, h_ref, hp_ref, dy_ref, wa_ref, wx_ref, sm_ref,
             duz_ref, gwa_ref, gwx_ref, dsm_ref, s0, s1, s2, s3, g_s, acc_a, acc_x, gcar, acar, dcar):
        step = pl.program_id(1)
        rb = nb - 1 - step
        wa = wa_ref[...].reshape(cb, cb)
        wx = wx_ref[...].reshape(cb, cb)
        taps, uc, ucb, r, ig, sp, sgn, a, mult = _lru_gates(u_ref, up_ref, rb, sm_ref, wa, wx)
        row = lax.broadcasted_iota(jnp.int32, a.shape, 0)
        z = z_ref[...].astype(F32)
        sg = _sig(z)
        dy2v = dy_ref[...].astype(F32)
        hv = h_ref[...]
        duz_ref[1] = (dy2v * hv * sg * (1.0 + z * (1.0 - sg))).astype(duz_ref.dtype)
        a_next = jnp.where(row == tr - 1, jnp.where(step > 0, acar[0:1, :], 0.0), pltpu.roll(a, tr - 1, 0))
        g_first = _seg_scan(a_next, dy2v * z * sg, g_s, (s0, s1, s2, s3), jnp.where(step > 0, gcar[0:1, :], 0.0), True)
        gcar[...] = jnp.broadcast_to(g_first, gcar.shape)
        acar[...] = jnp.broadcast_to(a[0:1, :], acar.shape)
        gv = g_s[...]
        h_before = jnp.where(rb > 0, hp_ref[HALO - 1:HALO, :], 0.0)
        h_prev = jnp.where(row == 0, h_before, pltpu.roll(hv, 1, 0))
        da = gv * h_prev
        gu = gv * uc
        dmult = gu * ig
        dig = gu * mult
        dlog_a = da * a - dmult * jnp.where(mult > 0.0, a * a / mult, 0.0)
        dra = dlog_a * (-LRU_C) * sp * r * (1.0 - r)
        dix = dig * ig * (1.0 - ig)
        dl = jnp.sum(dlog_a * r, axis=0, keepdims=True) * (LRU_C * sgn)
        dra_b, dix_b = dra.astype(MXU), dix.astype(MXU)
        duc = (gv * mult * ig + lax.dot_general(dra_b, wa, _DN["nt"], preferred_element_type=F32)
               + lax.dot_general(dix_b, wx, _DN["nt"], preferred_element_type=F32))
        gwa = lax.dot_general(ucb, dra_b, _DN["tn"], preferred_element_type=F32)
        gwx = lax.dot_general(ucb, dix_b, _DN["tn"], preferred_element_type=F32)
        ext = jnp.concatenate([duc, jnp.where(step > 0, dcar[...], 0.0)], axis=0)
        n = ext.shape[0]
        du = duc * sm_ref[CONV_W - 1:CONV_W, :]
        for k in range(1, CONV_W):
            du = du + pltpu.roll(ext, n - k, 0)[:tr] * sm_ref[CONV_W - 1 - k:CONV_W - k, :]
        duz_ref[0] = du.astype(duz_ref.dtype)
        dcar[...] = duc[:HALO]
        rows = [jnp.sum(duc * taps[CONV_W - 1 - k], axis=0, keepdims=True) for k in range(CONV_W)]
        rows += [jnp.sum(duc, axis=0, keepdims=True), jnp.sum(dra, axis=0, keepdims=True),
                 jnp.sum(dix, axis=0, keepdims=True), dl]

        @pl.when(step == 0)
        def _():
            acc_a[...] = gwa
            acc_x[...] = gwx
            for k, rv in enumerate(rows):
                dsm_ref[k:k + 1, :] = rv

        @pl.when(step > 0)
        def _():
            acc_a[...] += gwa
            acc_x[...] += gwx
            for k, rv in enumerate(rows):
                dsm_ref[k:k + 1, :] += rv

        @pl.when(step == nb - 1)
        def _():
            gwa_ref[...] = acc_a[...].reshape(4, cb // 4, cb).astype(gwa_ref.dtype)
            gwx_ref[...] = acc_x[...].reshape(4, cb // 4, cb).astype(gwx_ref.dtype)

    wspec = pl.BlockSpec((4, None, cb // 4, cb), lambda h, r: (0, h, 0, 0))
    blk = pl.BlockSpec((tr, cb), lambda h, r: (nb - 1 - r, h))
    return pl.pallas_call(
        body, name="lru_bwd", grid=(LRU_HEADS, nb),
        in_specs=[
            pl.BlockSpec((None, tr, cb), lambda h, r: (0, nb - 1 - r, h)),
            pl.BlockSpec((None, HALO, cb), lambda h, r: (0, jnp.maximum((nb - 1 - r) * hb - 1, 0), h)),
            pl.BlockSpec((None, tr, cb), lambda h, r: (1, nb - 1 - r, h)),
            blk,
            pl.BlockSpec((HALO, cb), lambda h, r: (jnp.maximum((nb - 1 - r) * hb - 1, 0), h)),
            blk,
            wspec, wspec,
            pl.BlockSpec((8, cb), lambda h, r: (0, h)),
        ],
        out_specs=[
            pl.BlockSpec((2, tr, cb), lambda h, r: (0, nb - 1 - r, h)),
            wspec, wspec,
            pl.BlockSpec((8, cb), lambda h, r: (0, h)),
        ],
        out_shape=[
            jax.ShapeDtypeStruct((2, s, e), MXU),
            jax.ShapeDtypeStruct(wa_g.shape, WIRE),
            jax.ShapeDtypeStruct(wx_g.shape, WIRE),
            jax.ShapeDtypeStruct((8, e), F32),
        ],
        scratch_shapes=([pltpu.VMEM((cb // LANES, tr, LANES), F32)] * 4 + [pltpu.VMEM((tr, cb), F32)]
                        + [pltpu.VMEM((cb, cb), F32)] * 2 + [pltpu.VMEM((8, cb), F32)] * 2 + [pltpu.VMEM((HALO, cb), F32)]),
        compiler_params=_cp(("parallel", "arbitrary")),
    )(uz, uz, uz, hst, hst, dy2, wa_g, wx_g, small)


def _place():
    x, y, c = lax.axis_index("x"), lax.axis_index("y"), lax.axis_index("c")
    chips = [(1 - x, y), (x, 1 - y), (1 - x, 1 - y)]
    return x, y, c, chips


def _rcopy(src, dst, send_sems, recv_sems, k, to):
    return pltpu.make_async_remote_copy(src_ref=src, dst_ref=dst, send_sem=send_sems.at[k], recv_sem=recv_sems.at[k],
                                        device_id=to, device_id_type=MESHID)


def _stage_gather_ici(bufs):
    n = len(bufs)

    def quarter(ref, chip, c, q):
        rq = ref.shape[3] // 2
        return ref.at[2 * chip[0] + chip[1], :, c, pl.ds(q * rq, rq)]

    def copies(refs, ss, rs, off, a, sending):
        x, y, c, _ = _place()
        me, xn, yn, dg = (x, y), (1 - x, y), (x, 1 - y), (1 - x, 1 - y)
        plan = [(0, me, xn, 0), (1, me, xn, 1), (3, me, yn, 1), (2, me, yn, 0),
                (4, xn, yn, 0), (5, yn, xn, 1)]
        if not sending:
            plan = [(0, xn, xn, 0), (1, xn, xn, 1), (3, yn, yn, 1), (2, yn, yn, 0), (4, dg, yn, 0), (5, dg, xn, 1)]
        out = []
        for k, owner, to, q in plan:
            blk = quarter(refs[a], owner, c, q)
            out.append(_rcopy(blk, blk, ss, rs, off + 6 * a + k, (*to, c)))
        return out

    def start(ro, refs, ss, rs, off):
        for a in range(n):
            for cp in copies(refs, ss, rs, off, a, True)[:4]:
                cp.start()

    def mid(ro, refs, ss, rs, off):
        for a in range(n):
            got, out = copies(refs, ss, rs, off, a, False), copies(refs, ss, rs, off, a, True)
            got[0].wait_recv()
            out[4].start()
            got[2].wait_recv()
            out[5].start()

    def finish(ro, refs, ss, rs, off):
        for a in range(n):
            got = copies(refs, ss, rs, off, a, False)
            for k in (1, 3, 4, 5):
                got[k].wait_recv()
            for cp in copies(refs, ss, rs, off, a, True):
                cp.wait_send()

    return dict(ro=[], bufs=list(bufs), nsem=6 * n, start=start, mid=mid, finish=finish)


def _stage_gather_d2d(bufs):
    n = len(bufs)

    def copies(refs, ss, rs, off, sending):
        x, y, c, chips = _place()
        out = []
        for a in range(n):
            for jj, ch in enumerate(chips):
                blk = refs[a].at[2 * ch[0] + ch[1], :, c if sending else 1 - c]
                out.append(_rcopy(blk, blk, ss, rs, off + 3 * a + jj, (x, y, 1 - c)))
        return out

    def start(ro, refs, ss, rs, off):
        for cp in copies(refs, ss, rs, off, True):
            cp.start()

    def finish(ro, refs, ss, rs, off):
        for cp in copies(refs, ss, rs, off, False):
            cp.wait_recv()
        for cp in copies(refs, ss, rs, off, True):
            cp.wait_send()

    return dict(ro=[], bufs=list(bufs), nsem=3 * n, start=start, finish=finish)


def _stage_scatter_ici(parts, gots):
    n = len(parts)

    def copies(ro, refs, ss, rs, off):
        x, y, c, chips = _place()
        return [_rcopy(ro[a].at[2 * ch[0] + ch[1]], refs[a].at[jj], ss, rs, off + 3 * a + jj, (*ch, c))
                for a in range(n) for jj, ch in enumerate(chips)]

    def start(ro, refs, ss, rs, off):
        for cp in copies(ro, refs, ss, rs, off):
            cp.start()

    def finish(ro, refs, ss, rs, off):
        for cp in copies(ro, refs, ss, rs, off):
            cp.wait()

    return dict(ro=list(parts), bufs=list(gots), nsem=3 * n, start=start, finish=finish)


def _stage_send_half(grads, lands):
    n = len(grads)

    def copies(ro, refs, ss, rs, off):
        x, y, c, _ = _place()
        return [_rcopy(ro[a].at[:, :, 1 - c], refs[a], ss, rs, off + a, (x, y, 1 - c)) for a in range(n)]

    def start(ro, refs, ss, rs, off):
        for cp in copies(ro, refs, ss, rs, off):
            cp.start()

    def finish(ro, refs, ss, rs, off):
        for cp in copies(ro, refs, ss, rs, off):
            cp.wait()

    return dict(ro=list(grads), bufs=list(lands), nsem=n, start=start, finish=finish)


def cast_into_slab(w, k_idx):
    l, r, c = w.shape
    rh = r // 2
    lb, tr = _block_lr(l, rh, c)
    nbh = rh // tr

    def body(k_ref, w_ref, o_ref):
        o_ref[...] = w_ref[...].astype(o_ref.dtype)

    return pl.pallas_call(
        body, name="cast_into_slab",
        grid_spec=pltpu.PrefetchScalarGridSpec(
            num_scalar_prefetch=1, grid=(l // lb, 2, nbh),
            in_specs=[pl.BlockSpec((lb, tr, c), lambda i, h, b, k_ref: (i, h * nbh + b, 0))],
            out_specs=pl.BlockSpec((None, lb, None, tr, c), lambda i, h, b, k_ref: (k_ref[0], i, h, b, 0))),
        out_shape=jax.ShapeDtypeStruct((4, l, 2, rh, c), WIRE),
        compiler_params=_cp(("parallel", "parallel", "parallel")),
    )(k_idx, w)


def gather_weights(bufs):
    n = len(bufs)
    ici = [_stage_gather_ici([b]) for b in bufs]
    d2d = [_stage_gather_d2d([b]) for b in bufs]
    per = ici[0]["nsem"] + d2d[0]["nsem"]

    def body(*refs):
        outs = refs[n:2 * n]
        ss, rs = refs[2 * n:]
        for what in ("start", "mid"):
            for a in range(n):
                ici[a][what]([], [outs[a]], ss, rs, per * a)
        for a in range(n):
            ici[a]["finish"]([], [outs[a]], ss, rs, per * a)
            d2d[a]["start"]([], [outs[a]], ss, rs, per * a + ici[a]["nsem"])
        for a in range(n):
            d2d[a]["finish"]([], [outs[a]], ss, rs, per * a + ici[a]["nsem"])

    return pl.pallas_call(
        body, name="gather_weights",
        in_specs=[ANY] * n, out_specs=[ANY] * n,
        out_shape=[jax.ShapeDtypeStruct(a.shape, a.dtype) for a in bufs],
        scratch_shapes=[pltpu.SemaphoreType.DMA((per * n,)), pltpu.SemaphoreType.DMA((per * n,))],
        input_output_aliases={a: a for a in range(n)},
        compiler_params=pltpu.CompilerParams(has_side_effects=True),
    )(*bufs)


def all_gather_small(v, name):
    m_per, n = v.shape

    def body(x_ref, out_ref, send_sems, recv_sems, local_sem):
        x, y, c, chips = _place()
        me, sibling = (x, y, c), (x, y, 1 - c)

        def rows(px, py, pc):
            return out_ref.at[pl.ds((4 * px + 2 * py + pc) * m_per, m_per), :]

        def copy(k, block, to, src=None):
            return _rcopy(rows(*block) if src is None else src, rows(*block), send_sems, recv_sems, k, to)

        mine = pltpu.make_async_copy(x_ref, rows(*me), local_sem)
        mine.start()
        first = [copy(0, me, sibling, src=x_ref)]
        first += [copy(1 + jj, me, (*chip, c), src=x_ref) for jj, chip in enumerate(chips)]
        for cp in first:
            cp.start()
        passed = [copy(4 + jj, (*chip, c), sibling) for jj, chip in enumerate(chips)]
        for jj, chip in enumerate(chips):
            copy(1 + jj, (*chip, c), me).wait_recv()
            passed[jj].start()
        copy(0, sibling, me).wait_recv()
        for jj, chip in enumerate(chips):
            copy(4 + jj, (*chip, 1 - c), me).wait_recv()
        for cp in first + passed:
            cp.wait_send()
        mine.wait()

    return pl.pallas_call(
        body, name=name,
        out_shape=jax.ShapeDtypeStruct((8 * m_per, n), v.dtype),
        in_specs=[pl.BlockSpec(memory_space=pltpu.VMEM)],
        out_specs=pl.BlockSpec(memory_space=pltpu.VMEM),
        scratch_shapes=[pltpu.SemaphoreType.DMA((7,)), pltpu.SemaphoreType.DMA((7,)), pltpu.SemaphoreType.DMA],
        compiler_params=pltpu.CompilerParams(vmem_limit_bytes=VMEM_LIMIT),
    )(v)


def sum_devices(g):
    def body(g_ref, o_ref):
        acc = g_ref[0]
        for d in range(1, 8):
            acc = acc + g_ref[d]
        o_ref[...] = acc

    return pl.pallas_call(body, name="sum_devices", out_shape=jax.ShapeDtypeStruct(g.shape[1:], g.dtype),
                          compiler_params=pltpu.CompilerParams(vmem_limit_bytes=VMEM_LIMIT))(g)


def send_other_half(grads):
    n = len(grads)

    def body(*refs):
        ins, outs = refs[:n], refs[n:2 * n]
        send_sems, recv_sems = refs[2 * n:]
        x, y, c, _ = _place()
        sib = (x, y, 1 - c)
        cps = [_rcopy(ins[a].at[:, :, 1 - c], outs[a], send_sems, recv_sems, a, sib) for a in range(n)]
        for cp in cps:
            cp.start()
        for cp in cps:
            cp.wait()

    return pl.pallas_call(
        body, name="send_other_half", in_specs=[ANY] * n, out_specs=[ANY] * n,
        out_shape=[jax.ShapeDtypeStruct(g.shape[:2] + g.shape[3:], g.dtype) for g in grads],
        scratch_shapes=[pltpu.SemaphoreType.DMA((n,)), pltpu.SemaphoreType.DMA((n,))],
        compiler_params=pltpu.CompilerParams(has_side_effects=True),
    )(*grads)


def scatter_to_chips(parts):
    n = len(parts)

    def body(*refs):
        ins, outs = refs[:n], refs[n:2 * n]
        send_sems, recv_sems = refs[2 * n:]
        x, y, c, chips = _place()
        cps = []
        for a in range(n):
            for jj, ch in enumerate(chips):
                cps.append(_rcopy(ins[a].at[2 * ch[0] + ch[1]], outs[a].at[jj], send_sems, recv_sems, 3 * a + jj, (*ch, c)))
        for cp in cps:
            cp.start()
        for cp in cps:
            cp.wait()

    return pl.pallas_call(
        body, name="scatter_to_chips", in_specs=[ANY] * n, out_specs=[ANY] * n,
        out_shape=[jax.ShapeDtypeStruct((3,) + p.shape[1:], p.dtype) for p in parts],
        scratch_shapes=[pltpu.SemaphoreType.DMA((3 * n,)), pltpu.SemaphoreType.DMA((3 * n,))],
        compiler_params=pltpu.CompilerParams(has_side_effects=True),
    )(*parts)


def share_halves(bufs, spans):
    n = len(bufs)

    def body(*refs):
        outs = refs[n:2 * n]
        send_sems, recv_sems = refs[2 * n:]
        x, y, c, _ = _place()
        sib = (x, y, 1 - c)

        def blk(a, half):
            return outs[a].at[pl.ds(spans[a][0], spans[a][1]), half]

        cps = [_rcopy(blk(a, c), blk(a, c), send_sems, recv_sems, a, sib) for a in range(n)]
        for cp in cps:
            cp.start()
        for a in range(n):
            _rcopy(blk(a, 1 - c), blk(a, 1 - c), send_sems, recv_sems, a, sib).wait_recv()
        for cp in cps:
            cp.wait_send()

    return pl.pallas_call(
        body, name="share_halves", in_specs=[ANY] * n, out_specs=[ANY] * n,
        out_shape=[jax.ShapeDtypeStruct(b.shape, b.dtype) for b in bufs],
        scratch_shapes=[pltpu.SemaphoreType.DMA((n,)), pltpu.SemaphoreType.DMA((n,))],
        input_output_aliases={a: a for a in range(n)},
        compiler_params=pltpu.CompilerParams(has_side_effects=True),
    )(*bufs)


def _block_rows(r, c, itemsize, budget=1 << 20):
    tr = r
    while tr * c * itemsize > budget and tr % 16 == 0:
        tr //= 2
    return tr


def _block_lr(l, r, c, budget=4 << 20):
    tr = _block_rows(r, c, 4, budget)
    lb = 1
    if tr == r:
        while l % (2 * lb) == 0 and 2 * lb * r * c * 4 <= budget:
            lb *= 2
    return lb, tr


def add_halves(g, got, c_idx):
    k4, l, _, rh, cc = g.shape
    lb, tr = _block_lr(l, rh, cc)

    def body(c_ref, g_ref, r_ref, o_ref):
        o_ref[...] = (g_ref[...].astype(F32) + r_ref[...].astype(F32)).astype(o_ref.dtype)

    return pl.pallas_call(
        body, name="add_halves",
        grid_spec=pltpu.PrefetchScalarGridSpec(
            num_scalar_prefetch=1, grid=(k4, l // lb, rh // tr),
            in_specs=[pl.BlockSpec((None, lb, None, tr, cc), lambda k, i, b, c_ref: (k, i, c_ref[0], b, 0)),
                      pl.BlockSpec((None, lb, tr, cc), lambda k, i, b, c_ref: (k, i, b, 0))],
            out_specs=pl.BlockSpec((None, lb, tr, cc), lambda k, i, b, c_ref: (k, i, b, 0))),
        out_shape=jax.ShapeDtypeStruct(got.shape, WIRE),
        compiler_params=_cp(("parallel", "parallel", "parallel")),
    )(c_idx, g, got)


def sum_chips(own, got, kc_idx, full, first):
    _, l, rh, cc = own.shape
    lb, tr = _block_lr(l, rh, cc, 2 << 20)
    assert first % lb == 0

    def body(k_ref, o_ref, r_ref, _full, s_ref):
        s_ref[...] = ((o_ref[...].astype(F32) + r_ref[0].astype(F32)) + r_ref[1].astype(F32)) + r_ref[2].astype(F32)

    return pl.pallas_call(
        body, name="sum_chips",
        grid_spec=pltpu.PrefetchScalarGridSpec(
            num_scalar_prefetch=1, grid=(l // lb, rh // tr),
            in_specs=[pl.BlockSpec((None, lb, tr, cc), lambda i, b, k_ref: (k_ref[0], i, b, 0)),
                      pl.BlockSpec((3, lb, tr, cc), lambda i, b, k_ref: (0, i, b, 0)),
                      ANY],
            out_specs=pl.BlockSpec((lb, None, tr, cc), lambda i, b, k_ref: (first // lb + i, k_ref[1], b, 0))),
        out_shape=jax.ShapeDtypeStruct(full.shape, F32),
        input_output_aliases={3: 0},
        compiler_params=_cp(("parallel", "parallel")),
    )(kc_idx, own, got, full)


def _adam_math(w, g, m, v):
    m = ADAM_B1 * m + (1.0 - ADAM_B1) * g
    v = ADAM_B2 * v + (1.0 - ADAM_B2) * (g * g)
    m_hat = m / (1.0 - ADAM_B1 ** ADAM_STEP)
    v_hat = v / (1.0 - ADAM_B2 ** ADAM_STEP)
    delta = -ADAM_LR * (m_hat / (jnp.sqrt(v_hat) + ADAM_EPS) + ADAM_WD * w)
    return delta, m, v


def adamw(w, g, m, v):
    l, r, c = w.shape
    tr = _block_rows(r, c, 4, 2 << 20)

    def body(w_ref, g_ref, m_ref, v_ref, go_ref, d_ref, mo_ref, vo_ref):
        gv = g_ref[...]
        go_ref[...] = gv
        d_ref[...], mo_ref[...], vo_ref[...] = _adam_math(w_ref[...], gv, m_ref[...], v_ref[...])

    spec = pl.BlockSpec((None, tr, c), lambda i, b: (i, b, 0))
    return pl.pallas_call(
        body, name="adamw", grid=(l, r // tr), in_specs=[spec] * 4, out_specs=[spec] * 4,
        out_shape=[jax.ShapeDtypeStruct(w.shape, F32)] * 4, compiler_params=_cp(("parallel", "parallel")),
    )(w, g, m, v)


def adamw_small(w, g, m, v):
    def body(w_ref, g_ref, m_ref, v_ref, d_ref, mo_ref, vo_ref):
        d_ref[...], mo_ref[...], vo_ref[...] = _adam_math(w_ref[...], g_ref[...], m_ref[...], v_ref[...])

    return pl.pallas_call(body, name="adamw_small", out_shape=[jax.ShapeDtypeStruct(w.shape, F32)] * 3)(w, g, m, v)


def kernel(x, p, w_in, w_out, g_pre, g_post, pool_w, pool_b, pool_scale, conv_w, conv_b, lru_wa, lru_ba, lru_wx, lru_bx, lru_L, w_ple, w_ple_gate, g_ple_in, g_ple_out, loss_target, m_w_in, m_w_out, m_g_pre, m_g_post, m_pool_w, m_pool_b, m_pool_scale, m_conv_w, m_conv_b, m_lru_wa, m_lru_ba, m_lru_wx, m_lru_bx, m_lru_L, m_w_ple, m_w_ple_gate, m_g_ple_in, m_g_ple_out, v_w_in, v_w_out, v_g_pre, v_g_post, v_pool_w, v_pool_b, v_pool_scale, v_conv_w, v_conv_b, v_lru_wa, v_lru_ba, v_lru_wx, v_lru_bx, v_lru_L, v_w_ple, v_w_ple_gate, v_g_ple_in, v_g_ple_out):
    depth = w_in.shape[0]
    _, s, d = x.shape
    e = 2 * d
    kp = p.shape[-1]
    nmix = pool_w.shape[0]
    ngrp = pool_w.shape[1]
    cg = e // ngrp
    cb = e // LRU_HEADS
    xi, yi, ci = lax.axis_index("x"), lax.axis_index("y"), lax.axis_index("c")
    me = 2 * xi + yi
    c_idx = jnp.reshape(ci, (1,)).astype(jnp.int32)
    k_idx = jnp.reshape(me, (1,)).astype(jnp.int32)
    tr_row = _tile(s, 256)
    tr_mix = _tile(s, 512)
    tm = _tile(s, 1024)
    tm2 = _tile(s, 2048)

    def halves(a):
        return a.reshape(a.shape[0], 2, a.shape[1] // 2, a.shape[2])

    big = {
        "w_in": w_in, "w_out": w_out, "gate": w_ple_gate, "ple": w_ple,
        "pool": pool_w.reshape(nmix * ngrp, cg // 4, cg),
        "wa": lru_wa.reshape(nmix * LRU_HEADS, cb // 4, cb), "wx": lru_wx.reshape(nmix * LRU_HEADS, cb // 4, cb),
    }
    names = list(big)

    def layer_shards(i):
        sh = {"w_in": w_in[i][None], "w_out": w_out[i][None], "gate": w_ple_gate[i][None], "ple": w_ple[i][None]}
        if i % 2 == 0:
            sh["pool"] = pool_w[i // 2]
        else:
            sh["wa"], sh["wx"] = lru_wa[i // 2], lru_wx[i // 2]
        return sh

    def mixer_names(i):
        return ["pool"] if i % 2 == 0 else ["wa", "wx"]

    wbuf = [{n: cast_into_slab(w, k_idx) for n, w in layer_shards(i).items()} for i in range(depth)]
    first = list(wbuf[0])
    wbuf[0] = dict(zip(first, gather_weights([wbuf[0][n] for n in first])))

    def full_w(i, n):
        b = wbuf[i][n]
        return b.reshape(b.shape[0], b.shape[1], 2 * b.shape[3], b.shape[4])

    def run_mm(stages, *args):
        if not stages:
            return _matmul(*args)
        out, new = _matmul(*args, hosted=[mk([wbuf[l][n] for n in nms]) for mk, l, nms in stages])
        slots = [(l, n) for _, l, nms in stages for n in nms]
        for (l, n), b in zip(slots, new):
            wbuf[l][n] = b
        return out

    ec = e // 4
    small_loc = jnp.concatenate([conv_w, conv_b[:, None], lru_ba[:, None], lru_bx[:, None], lru_L[:, None]], axis=1)
    sm_all = all_gather_small(small_loc.reshape(nmix * 8, ec), "gather_small").reshape(4, 2, nmix, 8, ec)
    lru_small = jnp.transpose(sm_all[:, 0], (1, 2, 0, 3)).reshape(nmix, 8, e)

    xs = x[0]
    saved = []
    for i in range(depth):
        j = i // 2
        h = rms_fwd(xs, g_pre[i][None], tr_row) if i == 0 else h_next
        nj = (2 * e) // 1024 if (2 * e) % 1024 == 0 else 4
        tn = (2 * e) // nj
        per = e // tn
        perk = (e // 2) // tn
        nxt = i + 1 if i + 1 < depth else None
        stages = [(_stage_gather_d2d, i, ["gate", "ple"])] if i > 0 else []
        if nxt is not None:
            stages.append((_stage_gather_ici, nxt, ["w_in"]))
        uz = run_mm(
            stages, "mm_in", "nn", (s // tm2, nj, 1), h, pl.BlockSpec((tm2, d), lambda a, b, k: (a, 0)),
            full_w(i, "w_in"), pl.BlockSpec((None, None, d, tn), lambda a, b, k, perk=perk: (b // perk, 0, 0, b % perk)),
            jax.ShapeDtypeStruct((2, s, e), MXU), pl.BlockSpec((None, tm2, tn), lambda a, b, k, per=per: (b // per, a, b % per)),
            (8, 128))
        if i % 2 == 0:
            y2 = pool_fwd(uz, full_w(i, "pool"), pool_b[j][None], pool_scale[j][None], tr_mix)
            hst = None
        else:
            y2, hst = lru_fwd(uz, full_w(i, "wa"), full_w(i, "wx"), lru_small[j], tr_mix)
        tn_o = _tile(d, 1024)
        stages = [(_stage_gather_ici, nxt, ["w_out"] + mixer_names(nxt))] if nxt is not None else []
        o = run_mm(
            stages, "mm_out", "nn", (s // tm, d // tn_o, 1), y2, pl.BlockSpec((tm, e), lambda a, b, k: (a, 0)),
            full_w(i, "w_out"), pl.BlockSpec((4, None, e // 4, tn_o), lambda a, b, k: (0, 0, 0, b)),
            jax.ShapeDtypeStruct((s, d), MXU), pl.BlockSpec((tm, tn_o), lambda a, b, k: (a, b)), (8, 128))
        x1, hn = res_rms_fwd(xs, o, g_post[i][None], g_ple_in[i][None], tr_row)
        stages = []
        if nxt is not None:
            stages = [(_stage_gather_ici, nxt, ["gate", "ple"]), (_stage_gather_d2d, nxt, ["w_in", "w_out"] + mixer_names(nxt))]
        gpre = run_mm(
            stages, "mm_gate", "nn", (s // tm2, d // tn_o, 1), hn, pl.BlockSpec((tm2, d), lambda a, b, k: (a, 0)),
            full_w(i, "gate"), pl.BlockSpec((4, None, d // 4, tn_o), lambda a, b, k: (0, 0, 0, b)),
            jax.ShapeDtypeStruct((s, d), MXU), pl.BlockSpec((tm2, tn_o), lambda a, b, k: (a, b)), (8, 128))
        pe = p[i, 0]
        ev = _matmul(
            "mm_ple", "nn", (s // tm, 4, 1), pe, pl.BlockSpec((tm, kp), lambda a, b, k: (a, 0)),
            full_w(i, "ple"), pl.BlockSpec((None, None, kp, d // 4), lambda a, b, k: (b, 0, 0, 0)),
            jax.ShapeDtypeStruct((s, d), MXU), pl.BlockSpec((tm, d // 4), lambda a, b, k: (a, b)), (8, 128))
        if nxt is not None:
            x2, h_next = ple_fwd(x1, gpre, ev, g_ple_out[i][None], g_pre[nxt][None], tr_row)
        else:
            x2 = ple_fwd_last(x1, gpre, ev, g_ple_out[i][None], tr_row)
        saved.append((xs, h, uz, y2, hst, o, x1, hn, gpre, ev))
        xs = x2

    dx, sq = loss_bwd(xs, loss_target[0], tr_row)
    d_gpre, d_gpost, d_gin, d_gout = [None] * depth, [None] * depth, [None] * depth, [None] * depth
    d_pool_b, d_pool_sc, d_lru_small = [None] * nmix, [None] * nmix, [None] * nmix
    ts = _tile(s, 1024)
    kc_idx = jnp.stack([me, ci]).astype(jnp.int32)
    full = {n: lax.empty(halves(big[n]).shape, F32) for n in names}
    sums, gots = [None] * depth, [None] * depth

    def scatter_stage(l, nms):
        return _stage_scatter_ici([sums[l][n] for n in nms], [gots[l][n] for n in nms]), [(l, n) for n in nms]

    def run_mm_scatter(specs, *args):
        if not specs:
            return _matmul(*args)
        out, new = _matmul(*args, hosted=[st for st, _ in specs])
        for (l, n), b in zip([slot for _, sl in specs for slot in sl], new):
            gots[l][n] = b
        return out

    def first_row(i, n):
        return i if n in ("w_in", "w_out", "gate", "ple") else (i // 2) * (ngrp if n == "pool" else LRU_HEADS)

    for i in reversed(range(depth)):
        j = i // 2
        prev = i + 1 if i + 1 < depth else None
        x0, h, uz, y2, hst, o, x1, hn, gpre, ev = saved[i]
        pe = p[i, 0]
        gl = {}
        de, dgp, d_gout[i] = ple_bwd(dx, gpre, ev, g_ple_out[i][None], tr_row)
        gl["ple"] = _matmul(
            "mm_dple", "tn", (1, 4, s // ts), pe, pl.BlockSpec((ts, kp), lambda a, b, k: (k, 0)),
            de, pl.BlockSpec((ts, d // 4), lambda a, b, k: (k, b)),
            jax.ShapeDtypeStruct((4, 1, kp, d // 4), WIRE), pl.BlockSpec((None, None, kp, d // 4), lambda a, b, k: (b, 0, 0, 0)),
            (kp, d // 4))
        tn_o = _tile(d, 1024)
        gl["gate"] = _matmul(
            "mm_dgate", "tn", (4, d // tn_o, 1), hn, pl.BlockSpec((s, d // 4), lambda a, b, k: (0, a)),
            dgp, pl.BlockSpec((s, tn_o), lambda a, b, k: (0, b)),
            jax.ShapeDtypeStruct((4, 1, d // 4, d), WIRE), pl.BlockSpec((None, None, d // 4, tn_o), lambda a, b, k: (a, 0, 0, b)),
            (8, 128))
        dhn = _matmul(
            "mm_dhn", "nt", (s // tm2, 4, 1), dgp, pl.BlockSpec((tm2, d), lambda a, b, k: (a, 0)),
            full_w(i, "gate"), pl.BlockSpec((None, None, d // 4, d), lambda a, b, k: (b, 0, 0, 0)),
            jax.ShapeDtypeStruct((s, d), MXU), pl.BlockSpec((tm2, d // 4), lambda a, b, k: (a, b)), (8, 128))
        dx1, do, d_gin[i], d_gpost[i] = post_bwd(dx, dhn, x1, g_ple_in[i][None], o, g_post[i][None], tr_row)
        gl["w_out"] = _matmul(
            "mm_dwout", "tn", (4, d // tn_o, 1), y2, pl.BlockSpec((s, e // 4), lambda a, b, k: (0, a)),
            do, pl.BlockSpec((s, tn_o), lambda a, b, k: (0, b)),
            jax.ShapeDtypeStruct((4, 1, e // 4, d), WIRE), pl.BlockSpec((None, None, e // 4, tn_o), lambda a, b, k: (a, 0, 0, b)),
            (8, 128))
        dy2 = _matmul(
            "mm_dy2", "nt", (s // tm2, 4, 1), do, pl.BlockSpec((tm2, d), lambda a, b, k: (a, 0)),
            full_w(i, "w_out"), pl.BlockSpec((None, None, e // 4, d), lambda a, b, k: (b, 0, 0, 0)),
            jax.ShapeDtypeStruct((s, e), MXU), pl.BlockSpec((tm2, e // 4), lambda a, b, k: (a, b)), (8, 128))
        if i % 2 == 0:
            duz, gl["pool"], d_pool_b[j], d_pool_sc[j] = pool_bwd(uz, dy2, full_w(i, "pool"), pool_b[j][None], pool_scale[j][None], tr_mix)
        else:
            duz, gl["wa"], gl["wx"], d_lru_small[j] = lru_bwd(uz, hst, dy2, full_w(i, "wa"), full_w(i, "wx"), lru_small[j], tr_mix)
        tmi = _tile(d, 1024)
        tni = _tile(e // 2, 1024)
        nslab = (e // 2) // tni
        gl["w_in"] = run_mm_scatter(
            [scatter_stage(prev, ["w_in"])] if prev is not None else [],
            "mm_dwin", "tn", (d // tmi, 4 * nslab, 1), h, pl.BlockSpec((s, tmi), lambda a, b, k: (0, a)),
            duz, pl.BlockSpec((None, s, tni), lambda a, b, k, nslab=nslab: (b // (2 * nslab), 0, b % (2 * nslab))),
            jax.ShapeDtypeStruct((4, 1, d, e // 2), WIRE),
            pl.BlockSpec((None, None, tmi, tni), lambda a, b, k, nslab=nslab: (b // nslab, 0, a, b % nslab)),
            (8, 128))
        lnames = list(gl)
        gparts = [gl[n].reshape(4, gl[n].shape[1], 2, gl[n].shape[2] // 2, gl[n].shape[3]) for n in lnames]
        hosted = [_stage_send_half(gparts, [lax.empty(g.shape[:2] + g.shape[3:], WIRE) for g in gparts])]
        rest = ["w_out", "gate", "ple"] + mixer_names(prev) if prev is not None else []
        if prev is not None:
            hosted.append(_stage_scatter_ici([sums[prev][n] for n in rest], [gots[prev][n] for n in rest]))
        tnd = _tile(d, 1024)
        dh, passed = _matmul(
            "mm_dh", "nt", (s // tm, d // tnd, 4), duz, pl.BlockSpec((None, tm, e // 2), lambda a, b, k: (k // 2, a, k % 2)),
            full_w(i, "w_in"), pl.BlockSpec((None, None, tnd, e // 2), lambda a, b, k: (k, 0, b, 0)),
            jax.ShapeDtypeStruct((s, d), MXU), pl.BlockSpec((tm, tnd), lambda a, b, k: (a, b)), (tm, tnd), hosted=hosted)
        from_sib = passed[:len(lnames)]
        for n, b in zip(rest, passed[len(lnames):]):
            gots[prev][n] = b
        dx, d_gpre[i] = rms_bwd_res(dx1, dh, x0, g_pre[i][None], tr_row)
        if prev is not None:
            for n in gots[prev]:
                full[n] = sum_chips(sums[prev][n], gots[prev][n], kc_idx, full[n], first_row(prev, n))
        sums[i] = {n: add_halves(g, r, c_idx) for n, g, r in zip(lnames, gparts, from_sib)}
        gots[i] = {n: lax.empty((3,) + sums[i][n].shape[1:], WIRE) for n in lnames}
    grad_x = dx[None]

    lnames = list(gots[0])
    for n, b in zip(lnames, scatter_to_chips([sums[0][n] for n in lnames])):
        full[n] = sum_chips(sums[0][n], b, kc_idx, full[n], first_row(0, n))
    shared = share_halves([full[n] for n in names], [(0, full[n].shape[0]) for n in names])
    grads = {n: f.reshape(big[n].shape) for n, f in zip(names, shared)}

    def rows_e(a):
        return jnp.stack(a).reshape(-1, e) if isinstance(a, list) else a.reshape(-1, e)

    pack = [rows_e([g[0] for g in d_gpre]), rows_e([g[0] for g in d_gpost]), rows_e([g[0] for g in d_gin]), rows_e([g[0] for g in d_gout]),
            jnp.concatenate(d_pool_b, axis=0), jnp.concatenate(d_pool_sc, axis=0), jnp.concatenate(d_lru_small, axis=0),
            jnp.pad(sq, ((0, 0), (0, e - d)))]
    sizes = [a.shape[0] for a in pack]
    packed = jnp.concatenate(pack, axis=0)
    nrow = packed.shape[0]
    nrow_p = -(-nrow // 8) * 8
    packed = jnp.pad(packed, ((0, nrow_p - nrow), (0, 0)))
    total = sum_devices(all_gather_small(packed, "gather_grads").reshape(8, nrow_p, e))
    parts, off = [], 0
    for n_ in sizes:
        parts.append(total[off:off + n_])
        off += n_
    t_gpre, t_gpost, t_gin, t_gout, t_pb, t_psc, t_lru, t_sq = parts
    loss = 0.5 * jnp.sum(t_sq) / d
    t_lru = t_lru.reshape(nmix, 8, e)
    t_lru_loc = lax.dynamic_slice_in_dim(t_lru, me * ec, ec, axis=2)

    def big_update(name, w, m, v):
        shp = big[name].shape
        g, dl, nm, nv = adamw(w.reshape(shp), grads[name], m.reshape(shp), v.reshape(shp))
        return [a.reshape(w.shape) for a in (g, dl, nm, nv)]

    def small_update(w, g, m, v):
        shp = w.shape
        w2 = w.reshape(-1, shp[-1])
        dl, nm, nv = adamw_small(w2, g.reshape(w2.shape), m.reshape(w2.shape), v.reshape(w2.shape))
        return [g.reshape(shp), dl.reshape(shp), nm.reshape(shp), nv.reshape(shp)]

    res = {
        "w_in": big_update("w_in", w_in, m_w_in, v_w_in),
        "w_out": big_update("w_out", w_out, m_w_out, v_w_out),
        "g_pre": small_update(g_pre, t_gpre.reshape(depth, d), m_g_pre, v_g_pre),
        "g_post": small_update(g_post, t_gpost.reshape(depth, d), m_g_post, v_g_post),
        "pool_w": big_update("pool", pool_w, m_pool_w, v_pool_w),
        "pool_b": small_update(pool_b, t_pb, m_pool_b, v_pool_b),
        "pool_scale": small_update(pool_scale, t_psc, m_pool_scale, v_pool_scale),
        "conv_w": small_update(conv_w, t_lru_loc[:, :CONV_W], m_conv_w, v_conv_w),
        "conv_b": small_update(conv_b, t_lru_loc[:, 4], m_conv_b, v_conv_b),
        "lru_wa": big_update("wa", lru_wa, m_lru_wa, v_lru_wa),
        "lru_ba": small_update(lru_ba, t_lru_loc[:, 5], m_lru_ba, v_lru_ba),
        "lru_wx": big_update("wx", lru_wx, m_lru_wx, v_lru_wx),
        "lru_bx": small_update(lru_bx, t_lru_loc[:, 6], m_lru_bx, v_lru_bx),
        "lru_L": small_update(lru_L, t_lru_loc[:, 7], m_lru_L, v_lru_L),
        "w_ple": big_update("ple", w_ple, m_w_ple, v_w_ple),
        "w_ple_gate": big_update("gate", w_ple_gate, m_w_ple_gate, v_w_ple_gate),
        "g_ple_in": small_update(g_ple_in, t_gin.reshape(depth, d), m_g_ple_in, v_g_ple_in),
        "g_ple_out": small_update(g_ple_out, t_gout.reshape(depth, d), m_g_ple_out, v_g_ple_out),
    }
    order = ["w_in", "w_out", "g_pre", "g_post", "pool_w", "pool_b", "pool_scale", "conv_w", "conv_b", "lru_wa", "lru_ba",
             "lru_wx", "lru_bx", "lru_L", "w_ple", "w_ple_gate", "g_ple_in", "g_ple_out"]
    out = [loss, grad_x]
    for slot in range(4):
        out += [res[n][slot] for n in order]
    return tuple(out)
```

```python
import functools

import jax
import jax.numpy as jnp
from jax import lax
from jax.experimental import pallas as pl
from jax.experimental.pallas import tpu as pltpu

F32 = jnp.float32
MXU = jnp.bfloat16
WIRE = jnp.bfloat16
VMEM_LIMIT = 56 * 1024 * 1024
RMS_EPS = 1e-6
LRU_C = 8.0
POOL_WINDOWS = (2, 4, 8, 16)
MAXW = 16
CONV_W = 4
LRU_HEADS = 16
ADAM_LR, ADAM_B1, ADAM_B2, ADAM_EPS, ADAM_WD, ADAM_STEP = 0.001, 0.9, 0.999, 1e-08, 0.01, 10
MESHID = pl.DeviceIdType.MESH
ANY = pl.BlockSpec(memory_space=pl.ANY)


def _cp(sem=None):
    return pltpu.CompilerParams(dimension_semantics=sem, vmem_limit_bytes=VMEM_LIMIT)


def _sig(v):
    return 0.5 * jnp.tanh(0.5 * v) + 0.5


def _tile(n, pref):
    return pref if n % pref == 0 else n


_DN = {"nn": (((1,), (0,)), ((), ())), "nt": (((1,), (1,)), ((), ())), "tn": (((0,), (0,)), ((), ()))}


def _hosting_call(body, name, grid, in_specs, out_specs, out_shape, scratch_shapes, args, hosted, sem, aliases=None):
    n_in, n_out = len(args), len(out_shape)
    aliases = dict(aliases or {})
    if not hosted:
        res = pl.pallas_call(body, name=name, grid=grid, in_specs=in_specs, out_specs=out_specs, out_shape=out_shape,
                             scratch_shapes=scratch_shapes, input_output_aliases=aliases, compiler_params=_cp(sem))(*args)
        return list(res), []
    ro = [r for st in hosted for r in st["ro"]]
    bufs = [r for st in hosted for r in st["bufs"]]
    nro, nbuf = len(ro), len(bufs)
    nsem = sum(st["nsem"] for st in hosted)
    total = 1
    for g in grid:
        total *= g

    def wrapped(*refs):
        ins, ro_refs = refs[:n_in], refs[n_in:n_in + nro]
        outs = refs[n_in + nro + nbuf:n_in + nro + nbuf + n_out]
        buf_refs = refs[n_in + nro + nbuf + n_out:n_in + nro + 2 * nbuf + n_out]
        scr, ss, rs = refs[n_in + nro + 2 * nbuf + n_out:-2], refs[-2], refs[-1]
        step = 0
        for t, g in enumerate(grid):
            step = step * g + pl.program_id(t)

        def run_stages(what):
            r0 = b0 = s0 = 0
            for st in hosted:
                if what in st:
                    st[what](ro_refs[r0:r0 + len(st["ro"])], buf_refs[b0:b0 + len(st["bufs"])], ss, rs, s0)
                r0, b0, s0 = r0 + len(st["ro"]), b0 + len(st["bufs"]), s0 + st["nsem"]

        @pl.when(step == 0)
        def _():
            run_stages("start")

        if any("mid" in st for st in hosted):
            @pl.when(step == total // 2)
            def _():
                run_stages("mid")

        body(*ins, *outs, *scr)

        @pl.when(step == total - 1)
        def _():
            run_stages("finish")

    res = pl.pallas_call(
        wrapped, name=name, grid=grid,
        in_specs=list(in_specs) + [ANY] * (nro + nbuf), out_specs=list(out_specs) + [ANY] * nbuf,
        out_shape=list(out_shape) + [jax.ShapeDtypeStruct(x.shape, x.dtype) for x in bufs],
        scratch_shapes=list(scratch_shapes) + [pltpu.SemaphoreType.DMA((nsem,)), pltpu.SemaphoreType.DMA((nsem,))],
        input_output_aliases={**aliases, **{n_in + nro + t: n_out + t for t in range(nbuf)}},
        compiler_params=_cp(("arbitrary",) * len(grid)),
    )(*args, *ro, *bufs)
    return list(res[:n_out]), list(res[n_out:])


def _matmul(name, mode, grid, a, a_spec, b, b_spec, out_shape, out_spec, acc_shape, hosted=()):
    nk = grid[2]

    def body(a_ref, b_ref, o_ref, acc_ref):
        kk = pl.program_id(2)
        bv = b_ref[...]
        if bv.ndim == 3:
            bv = bv.reshape(bv.shape[0] * bv.shape[1], bv.shape[2])
        prod = lax.dot_general(a_ref[...].astype(MXU), bv.astype(MXU), _DN[mode], preferred_element_type=F32)
        if nk == 1:
            o_ref[...] = prod.astype(o_ref.dtype)
        else:
            @pl.when(kk == 0)
            def _():
                acc_ref[...] = prod

            @pl.when(kk > 0)
            def _():
                acc_ref[...] += prod

            @pl.when(kk == nk - 1)
            def _():
                o_ref[...] = acc_ref[...].astype(o_ref.dtype)

    outs, passed = _hosting_call(body, name, grid, [a_spec, b_spec], [out_spec], [out_shape], [pltpu.VMEM(acc_shape, F32)],
                                 [a, b], hosted, ("parallel", "parallel", "arbitrary"))
    return (outs[0], passed) if hosted else outs[0]


def _rows_call(name, body, ins, in_rows, outs, out_rows, n_rows, tr):
    def spec(shape, tiled):
        if tiled:
            return pl.BlockSpec((tr, shape[1]), lambda i: (i, 0))
        return pl.BlockSpec(shape, lambda i: (0, 0))

    return pl.pallas_call(
        body, name=name, grid=(n_rows // tr,),
        in_specs=[spec(a.shape, t) for a, t in zip(ins, in_rows)],
        out_specs=[spec(o.shape, t) for o, t in zip(outs, out_rows)],
        out_shape=outs, compiler_params=_cp(("arbitrary",)),
    )(*ins)


def _rstd(v):
    return lax.rsqrt(jnp.mean(v * v, axis=-1, keepdims=True) + RMS_EPS)


def _norm_bwd(v, g, dy):
    r = _rstd(v)
    n = v * r
    dn = dy * g
    dv = r * (dn - n * jnp.mean(dn * n, axis=-1, keepdims=True))
    return dv, jnp.sum(dy * n, axis=0, keepdims=True)


def _acc_rows(ref, val):
    @pl.when(pl.program_id(0) == 0)
    def _():
        ref[...] = val

    @pl.when(pl.program_id(0) > 0)
    def _():
        ref[...] += val


def rms_fwd(x, g, tr):
    def body(x_ref, g_ref, o_ref):
        v = x_ref[...]
        o_ref[...] = (v * _rstd(v) * g_ref[...]).astype(o_ref.dtype)

    return _rows_call("rms_fwd", body, [x, g], [True, False], [jax.ShapeDtypeStruct(x.shape, MXU)], [True], x.shape[0], tr)[0]


def res_rms_fwd(x, o, g, g_next, tr):
    def body(x_ref, o_ref, g_ref, gn_ref, y_ref, h_ref):
        v = o_ref[...].astype(F32)
        y = x_ref[...] + v * _rstd(v) * g_ref[...]
        y_ref[...] = y
        h_ref[...] = (y * _rstd(y) * gn_ref[...]).astype(h_ref.dtype)

    return _rows_call("res_rms_fwd", body, [x, o, g, g_next], [True, True, False, False],
                      [jax.ShapeDtypeStruct(x.shape, F32), jax.ShapeDtypeStruct(x.shape, MXU)], [True, True], x.shape[0], tr)


def ple_fwd(x1, gpre, e, g, g_next, tr):
    def body(x_ref, gp_ref, e_ref, g_ref, gn_ref, y_ref, h_ref):
        v = e_ref[...].astype(F32) * _sig(gp_ref[...].astype(F32))
        y = x_ref[...] + v * _rstd(v) * g_ref[...]
        y_ref[...] = y
        h_ref[...] = (y * _rstd(y) * gn_ref[...]).astype(h_ref.dtype)

    return _rows_call("ple_fwd", body, [x1, gpre, e, g, g_next], [True, True, True, False, False],
                      [jax.ShapeDtypeStruct(x1.shape, F32), jax.ShapeDtypeStruct(x1.shape, MXU)], [True, True], x1.shape[0], tr)


def ple_fwd_last(x1, gpre, e, g, tr):
    def body(x_ref, gp_ref, e_ref, g_ref, y_ref):
        v = e_ref[...].astype(F32) * _sig(gp_ref[...].astype(F32))
        y_ref[...] = x_ref[...] + v * _rstd(v) * g_ref[...]

    return _rows_call("ple_fwd_last", body, [x1, gpre, e, g], [True, True, True, False],
                      [jax.ShapeDtypeStruct(x1.shape, F32)], [True], x1.shape[0], tr)[0]


def loss_bwd(y, target, tr):
    d = y.shape[1]

    def body(y_ref, t_ref, dy_ref, sq_ref):
        diff = y_ref[...] - t_ref[...]
        dy_ref[...] = diff * (1.0 / d)
        _acc_rows(sq_ref, jnp.sum(diff * diff, axis=0, keepdims=True))

    return _rows_call("loss_bwd", body, [y, target], [True, True],
                      [jax.ShapeDtypeStruct(y.shape, F32), jax.ShapeDtypeStruct((1, d), F32)], [True, False], y.shape[0], tr)


def ple_bwd(dx2, gpre, e, g, tr):
    d = dx2.shape[1]

    def body(dx_ref, gp_ref, e_ref, g_ref, de_ref, dgp_ref, dg_ref):
        gate = _sig(gp_ref[...].astype(F32))
        ev = e_ref[...].astype(F32)
        dv, dg = _norm_bwd(ev * gate, g_ref[...], dx_ref[...])
        de_ref[...] = (dv * gate).astype(de_ref.dtype)
        dgp_ref[...] = (dv * ev * gate * (1.0 - gate)).astype(dgp_ref.dtype)
        _acc_rows(dg_ref, dg)

    return _rows_call("ple_bwd", body, [dx2, gpre, e, g], [True, True, True, False],
                      [jax.ShapeDtypeStruct(dx2.shape, MXU), jax.ShapeDtypeStruct(dx2.shape, MXU), jax.ShapeDtypeStruct((1, d), F32)],
                      [True, True, False], dx2.shape[0], tr)


def rms_bwd_res(dres, dh, x, g, tr):
    d = x.shape[1]

    def body(dr_ref, dh_ref, x_ref, g_ref, dx_ref, dg_ref):
        dv, dg = _norm_bwd(x_ref[...], g_ref[...], dh_ref[...].astype(F32))
        dx_ref[...] = dr_ref[...] + dv
        _acc_rows(dg_ref, dg)

    return _rows_call("rms_bwd_res", body, [dres, dh, x, g], [True, True, True, False],
                      [jax.ShapeDtypeStruct(x.shape, F32), jax.ShapeDtypeStruct((1, d), F32)], [True, False], x.shape[0], tr)


def post_bwd(dres, dh, x, g, o, g_o, tr):
    d = x.shape[1]

    def body(dr_ref, dh_ref, x_ref, g_ref, o_ref, go_ref, dx_ref, do_ref, dg_ref, dgo_ref):
        dv, dg = _norm_bwd(x_ref[...], g_ref[...], dh_ref[...].astype(F32))
        dxv = dr_ref[...] + dv
        dx_ref[...] = dxv
        dov, dgo = _norm_bwd(o_ref[...].astype(F32), go_ref[...], dxv)
        do_ref[...] = dov.astype(do_ref.dtype)
        _acc_rows(dg_ref, dg)
        _acc_rows(dgo_ref, dgo)

    return _rows_call("post_bwd", body, [dres, dh, x, g, o, g_o], [True, True, True, False, True, False],
                      [jax.ShapeDtypeStruct(x.shape, F32), jax.ShapeDtypeStruct(x.shape, MXU),
                       jax.ShapeDtypeStruct((1, d), F32), jax.ShapeDtypeStruct((1, d), F32)], [True, True, False, False], x.shape[0], tr)


def _trailing_sum(ext, w):
    s, k = ext, 1
    while k < w:
        s = s + pltpu.roll(s, k, 0)
        k *= 2
    return s


def _leading_sum(ext, w):
    n = ext.shape[0]
    s, k = ext, 1
    while k < w:
        s = s + pltpu.roll(s, n - k, 0)
        k *= 2
    return s


def _pool_inv_count(rb, tr, w, c):
    t = rb * tr + lax.broadcasted_iota(jnp.int32, (tr, c), 0)
    return 1.0 / jnp.minimum(t + 1, w).astype(F32)


def _pool_d(u_ref, up_ref, rb, tr, w):
    cur = u_ref[...].astype(F32)
    prev = jnp.where(rb > 0, up_ref[...].astype(F32), 0.0)
    ext = jnp.concatenate([prev, cur], axis=0)
    win = _trailing_sum(ext, w)[MAXW:]
    return win * _pool_inv_count(rb, tr, w, cur.shape[1]) - cur


def pool_fwd(uz, w_g, bias, scale, tr):
    _, s, e = uz.shape
    ng = len(POOL_WINDOWS)
    cg = e // ng
    nb = s // tr
    hb = tr // MAXW

    def body(u_ref, up_ref, z_ref, w_ref, b_ref, sc_ref, o_ref):
        g, rb = pl.program_id(0), pl.program_id(1)
        wmat = w_ref[...].reshape(cg, cg)
        for gg, win in enumerate(POOL_WINDOWS):
            @pl.when(g == gg)
            def _(win=win):
                d = _pool_d(u_ref, up_ref, rb, tr, win)
                y = (jnp.dot(d.astype(MXU), wmat, preferred_element_type=F32) + b_ref[...]) * sc_ref[...]
                z = z_ref[...].astype(F32)
                o_ref[...] = (y * z * _sig(z)).astype(o_ref.dtype)

    return pl.pallas_call(
        body, name="pool_fwd", grid=(ng, nb),
        in_specs=[
            pl.BlockSpec((None, tr, cg), lambda g, r: (0, r, g)),
            pl.BlockSpec((None, MAXW, cg), lambda g, r: (0, jnp.maximum(r * hb - 1, 0), g)),
            pl.BlockSpec((None, tr, cg), lambda g, r: (1, r, g)),
            pl.BlockSpec((4, None, cg // 4, cg), lambda g, r: (0, g, 0, 0)),
            pl.BlockSpec((1, cg), lambda g, r: (0, g)),
            pl.BlockSpec((1, cg), lambda g, r: (0, g)),
        ],
        out_specs=pl.BlockSpec((tr, cg), lambda g, r: (r, g)),
        out_shape=jax.ShapeDtypeStruct((s, e), MXU),
        compiler_params=_cp(("parallel", "arbitrary")),
    )(uz, uz, uz, w_g, bias, scale)


def pool_bwd(uz, dy2, w_g, bias, scale, tr, hosted=()):
    _, s, e = uz.shape
    ng = len(POOL_WINDOWS)
    cg = e // ng
    nb = s // tr
    hb = tr // MAXW

    def body(u_ref, up_ref, z_ref, dy_ref, w_ref, b_ref, sc_ref, duz_ref, gw_ref, db_ref, dsc_ref, acc_ref, carry_ref):
        g, step = pl.program_id(0), pl.program_id(1)
        rb = nb - 1 - step
        wmat = w_ref[...].reshape(cg, cg)
        for gg, win in enumerate(POOL_WINDOWS):
            @pl.when(g == gg)
            def _(win=win):
                d = _pool_d(u_ref, up_ref, rb, tr, win).astype(MXU)
                ypre = jnp.dot(d, wmat, preferred_element_type=F32) + b_ref[...]
                z = z_ref[...].astype(F32)
                sg = _sig(z)
                dy2v = dy_ref[...].astype(F32)
                dyv = dy2v * z * sg
                duz_ref[1] = (dy2v * ypre * sc_ref[...] * sg * (1.0 + z * (1.0 - sg))).astype(duz_ref.dtype)
                dypre = dyv * sc_ref[...]
                dsc = jnp.sum(dyv * ypre, axis=0, keepdims=True)
                dbv = jnp.sum(dypre, axis=0, keepdims=True)
                dypre_b = dypre.astype(MXU)
                dd = lax.dot_general(dypre_b, wmat, _DN["nt"], preferred_element_type=F32)
                gw = lax.dot_general(d, dypre_b, _DN["tn"], preferred_element_type=F32)
                q = dd * _pool_inv_count(rb, tr, win, cg)
                nxt = jnp.where(step > 0, carry_ref[...], 0.0)
                lead = _leading_sum(jnp.concatenate([q, nxt], axis=0), win)[:tr]
                duz_ref[0] = (lead - dd).astype(duz_ref.dtype)
                carry_ref[...] = q[:MAXW]

                @pl.when(step == 0)
                def _():
                    acc_ref[...] = gw
                    db_ref[...] = dbv
                    dsc_ref[...] = dsc

                @pl.when(step > 0)
                def _():
                    acc_ref[...] += gw
                    db_ref[...] += dbv
                    dsc_ref[...] += dsc

                @pl.when(step == nb - 1)
                def _():
                    gw_ref[...] = acc_ref[...].reshape(4, cg // 4, cg).astype(gw_ref.dtype)

    return _hosting_call(
        body, "pool_bwd", (ng, nb),
        [
            pl.BlockSpec((None, tr, cg), lambda g, r: (0, nb - 1 - r, g)),
            pl.BlockSpec((None, MAXW, cg), lambda g, r: (0, jnp.maximum((nb - 1 - r) * hb - 1, 0), g)),
            pl.BlockSpec((None, tr, cg), lambda g, r: (1, nb - 1 - r, g)),
            pl.BlockSpec((tr, cg), lambda g, r: (nb - 1 - r, g)),
            pl.BlockSpec((4, None, cg // 4, cg), lambda g, r: (0, g, 0, 0)),
            pl.BlockSpec((1, cg), lambda g, r: (0, g)),
            pl.BlockSpec((1, cg), lambda g, r: (0, g)),
        ],
        [
            pl.BlockSpec((2, tr, cg), lambda g, r: (0, nb - 1 - r, g)),
            pl.BlockSpec((4, None, cg // 4, cg), lambda g, r: (0, g, 0, 0)),
            pl.BlockSpec((1, cg), lambda g, r: (0, g)),
            pl.BlockSpec((1, cg), lambda g, r: (0, g)),
        ],
        [
            jax.ShapeDtypeStruct((2, s, e), MXU),
            jax.ShapeDtypeStruct(w_g.shape, WIRE),
            jax.ShapeDtypeStruct((1, e), F32),
            jax.ShapeDtypeStruct((1, e), F32),
        ],
        [pltpu.VMEM((cg, cg), F32), pltpu.VMEM((MAXW, cg), F32)],
        [uz, uz, uz, dy2, w_g, bias, scale], hosted, ("parallel", "arbitrary"))


HALO = 16


def _one_minus_sq(log_a, a):
    poly = (-2.0 * log_a) * (1.0 + log_a * (1.0 + log_a * (2.0 / 3.0)))
    return jnp.where(log_a > -0.01, poly, 1.0 - a * a)


def _softplus_neg(lam):
    t = jnp.exp(-jnp.abs(lam))
    log1p = jnp.where(t < 1e-3, t * (1.0 - t * (0.5 - t * (1.0 / 3.0))), jnp.log(1.0 + t))
    return jnp.maximum(-lam, 0.0) + log1p, _sig(-lam)


def _lru_gates(u_ref, up_ref, rb, sm_ref, wa, wx):
    cur = u_ref[...].astype(F32)
    prev = jnp.where(rb > 0, up_ref[...].astype(F32), 0.0)
    ext = jnp.concatenate([prev, cur], axis=0)
    taps = [cur] + [pltpu.roll(ext, k, 0)[HALO:] for k in range(1, CONV_W)]
    uc = sm_ref[CONV_W:CONV_W + 1, :]
    for k in range(CONV_W):
        uc = uc + taps[k] * sm_ref[CONV_W - 1 - k:CONV_W - k, :]
    ucb = uc.astype(MXU)
    r = _sig(jnp.dot(ucb, wa, preferred_element_type=F32) + sm_ref[5:6, :])
    ig = _sig(jnp.dot(ucb, wx, preferred_element_type=F32) + sm_ref[6:7, :])
    sp, sgn = _softplus_neg(sm_ref[7:8, :])
    log_a = r * (-LRU_C * sp)
    a = jnp.exp(log_a)
    mult = jnp.sqrt(jnp.maximum(_one_minus_sq(log_a, a), 0.0))
    return taps, uc, ucb, r, ig, sp, sgn, a, mult


LANES = 128


def _seg_scan(a, b, out_ref, scr, state, reverse):
    a_s, b_s, h_s, p_s = scr
    tr, c = a.shape
    seg = tr // 8
    nl = c // LANES
    for l in range(nl):
        a_s[l] = a[:, l * LANES:(l + 1) * LANES]
        b_s[l] = b[:, l * LANES:(l + 1) * LANES]
    h = [jnp.zeros((8, LANES), F32)] * nl
    pp = [jnp.ones((8, LANES), F32)] * nl
    for i in (range(seg - 1, -1, -1) if reverse else range(seg)):
        rows = pl.ds(i, 8, stride=seg)
        for l in range(nl):
            av = a_s[l, rows, :]
            h[l] = av * h[l] + b_s[l, rows, :]
            pp[l] = av * pp[l]
            h_s[l, pl.ds(8 * i, 8), :] = h[l]
            p_s[l, pl.ds(8 * i, 8), :] = pp[l]
    leaving = []
    for l in range(nl):
        lanes = slice(l * LANES, (l + 1) * LANES)
        st = state[:, lanes]
        for sgm in (range(7, -1, -1) if reverse else range(8)):
            for t0 in range(0, seg, 8):
                rows = pl.ds(8 * t0 + sgm, 8, stride=8)
                out_ref[pl.ds(sgm * seg + t0, 8), lanes] = h_s[l, rows, :] + p_s[l, rows, :] * st
            st = h[l][sgm:sgm + 1, :] + pp[l][sgm:sgm + 1, :] * st
        leaving.append(st)
    return jnp.concatenate(leaving, axis=1)


def lru_fwd(uz, wa_g, wx_g, small, tr):
    _, s, e = uz.shape
    cb = e // LRU_HEADS
    nb = s // tr
    hb = tr // HALO

    def body(u_ref, up_ref, z_ref, wa_ref, wx_ref, sm_ref, o_ref, h_ref, s0, s1, s2, s3, carry_ref):
        rb = pl.program_id(1)
        wa = wa_ref[...].reshape(cb, cb)
        wx = wx_ref[...].reshape(cb, cb)
        _, uc, _, _, ig, _, _, a, mult = _lru_gates(u_ref, up_ref, rb, sm_ref, wa, wx)
        start = jnp.where(rb > 0, carry_ref[0:1, :], 0.0)
        last = _seg_scan(a, mult * ig * uc, h_ref, (s0, s1, s2, s3), start, False)
        carry_ref[...] = jnp.broadcast_to(last, carry_ref.shape)
        z = z_ref[...].astype(F32)
        o_ref[...] = (h_ref[...] * z * _sig(z)).astype(o_ref.dtype)

    wspec = pl.BlockSpec((4, None, cb // 4, cb), lambda h, r: (0, h, 0, 0))
    return pl.pallas_call(
        body, name="lru_fwd", grid=(LRU_HEADS, nb),
        in_specs=[
            pl.BlockSpec((None, tr, cb), lambda h, r: (0, r, h)),
            pl.BlockSpec((None, HALO, cb), lambda h, r: (0, jnp.maximum(r * hb - 1, 0), h)),
            pl.BlockSpec((None, tr, cb), lambda h, r: (1, r, h)),
            wspec, wspec,
            pl.BlockSpec((8, cb), lambda h, r: (0, h)),
        ],
        out_specs=[pl.BlockSpec((tr, cb), lambda h, r: (r, h)), pl.BlockSpec((tr, cb), lambda h, r: (r, h))],
        out_shape=[jax.ShapeDtypeStruct((s, e), MXU), jax.ShapeDtypeStruct((s, e), F32)],
        scratch_shapes=[pltpu.VMEM((cb // LANES, tr, LANES), F32)] * 4 + [pltpu.VMEM((8, cb), F32)],
        compiler_params=_cp(("parallel", "arbitrary")),
    )(uz, uz, uz, wa_g, wx_g, small)


def lru_bwd(uz, hst, dy2, wa_g, wx_g, small, tr, hosted=()):
    _, s, e = uz.shape
    cb = e // LRU_HEADS
    nb = s // tr
    hb = tr // HALO

    def body(u_ref, up_ref, z_ref, h_ref, hp_ref, dy_ref, wa_ref, wx_ref, sm_ref,
             duz_ref, gwa_ref, gwx_ref, dsm_ref, s0, s1, s2, s3, g_s, acc_a, acc_x, gcar, acar, dcar):
        step = pl.program_id(1)
        rb = nb - 1 - step
        wa = wa_ref[...].reshape(cb, cb)
        wx = wx_ref[...].reshape(cb, cb)
        taps, uc, ucb, r, ig, sp, sgn, a, mult = _lru_gates(u_ref, up_ref, rb, sm_ref, wa, wx)
        row = lax.broadcasted_iota(jnp.int32, a.shape, 0)
        z = z_ref[...].astype(F32)
        sg = _sig(z)
        dy2v = dy_ref[...].astype(F32)
        hv = h_ref[...]
        duz_ref[1] = (dy2v * hv * sg * (1.0 + z * (1.0 - sg))).astype(duz_ref.dtype)
        a_next = jnp.where(row == tr - 1, jnp.where(step > 0, acar[0:1, :], 0.0), pltpu.roll(a, tr - 1, 0))
        g_first = _seg_scan(a_next, dy2v * z * sg, g_s, (s0, s1, s2, s3), jnp.where(step > 0, gcar[0:1, :], 0.0), True)
        gcar[...] = jnp.broadcast_to(g_first, gcar.shape)
        acar[...] = jnp.broadcast_to(a[0:1, :], acar.shape)
        gv = g_s[...]
        h_before = jnp.where(rb > 0, hp_ref[HALO - 1:HALO, :], 0.0)
        h_prev = jnp.where(row == 0, h_before, pltpu.roll(hv, 1, 0))
        da = gv * h_prev
        gu = gv * uc
        dmult = gu * ig
        dig = gu * mult
        dlog_a = da * a - dmult * jnp.where(mult > 0.0, a * a / mult, 0.0)
        dra = dlog_a * (-LRU_C) * sp * r * (1.0 - r)
        dix = dig * ig * (1.0 - ig)
        dl = jnp.sum(dlog_a * r, axis=0, keepdims=True) * (LRU_C * sgn)
        dra_b, dix_b = dra.astype(MXU), dix.astype(MXU)
        duc = (gv * mult * ig + lax.dot_general(dra_b, wa, _DN["nt"], preferred_element_type=F32)
               + lax.dot_general(dix_b, wx, _DN["nt"], preferred_element_type=F32))
        gwa = lax.dot_general(ucb, dra_b, _DN["tn"], preferred_element_type=F32)
        gwx = lax.dot_general(ucb, dix_b, _DN["tn"], preferred_element_type=F32)
        ext = jnp.concatenate([duc, jnp.where(step > 0, dcar[...], 0.0)], axis=0)
        n = ext.shape[0]
        du = duc * sm_ref[CONV_W - 1:CONV_W, :]
        for k in range(1, CONV_W):
            du = du + pltpu.roll(ext, n - k, 0)[:tr] * sm_ref[CONV_W - 1 - k:CONV_W - k, :]
        duz_ref[0] = du.astype(duz_ref.dtype)
        dcar[...] = duc[:HALO]
        rows = [jnp.sum(duc * taps[CONV_W - 1 - k], axis=0, keepdims=True) for k in range(CONV_W)]
        rows += [jnp.sum(duc, axis=0, keepdims=True), jnp.sum(dra, axis=0, keepdims=True),
                 jnp.sum(dix, axis=0, keepdims=True), dl]

        @pl.when(step == 0)
        def _():
            acc_a[...] = gwa
            acc_x[...] = gwx
            for k, rv in enumerate(rows):
                dsm_ref[k:k + 1, :] = rv

        @pl.when(step > 0)
        def _():
            acc_a[...] += gwa
            acc_x[...] += gwx
            for k, rv in enumerate(rows):
                dsm_ref[k:k + 1, :] += rv

        @pl.when(step == nb - 1)
        def _():
            gwa_ref[...] = acc_a[...].reshape(4, cb // 4, cb).astype(gwa_ref.dtype)
            gwx_ref[...] = acc_x[...].reshape(4, cb // 4, cb).astype(gwx_ref.dtype)

    wspec = pl.BlockSpec((4, None, cb // 4, cb), lambda h, r: (0, h, 0, 0))
    blk = pl.BlockSpec((tr, cb), lambda h, r: (nb - 1 - r, h))
    return _hosting_call(
        body, "lru_bwd", (LRU_HEADS, nb),
        [
            pl.BlockSpec((None, tr, cb), lambda h, r: (0, nb - 1 - r, h)),
            pl.BlockSpec((None, HALO, cb), lambda h, r: (0, jnp.maximum((nb - 1 - r) * hb - 1, 0), h)),
            pl.BlockSpec((None, tr, cb), lambda h, r: (1, nb - 1 - r, h)),
            blk,
            pl.BlockSpec((HALO, cb), lambda h, r: (jnp.maximum((nb - 1 - r) * hb - 1, 0), h)),
            blk,
            wspec, wspec,
            pl.BlockSpec((8, cb), lambda h, r: (0, h)),
        ],
        [
            pl.BlockSpec((2, tr, cb), lambda h, r: (0, nb - 1 - r, h)),
            wspec, wspec,
            pl.BlockSpec((8, cb), lambda h, r: (0, h)),
        ],
        [
            jax.ShapeDtypeStruct((2, s, e), MXU),
            jax.ShapeDtypeStruct(wa_g.shape, WIRE),
            jax.ShapeDtypeStruct(wx_g.shape, WIRE),
            jax.ShapeDtypeStruct((8, e), F32),
        ],
        ([pltpu.VMEM((cb // LANES, tr, LANES), F32)] * 4 + [pltpu.VMEM((tr, cb), F32)]
         + [pltpu.VMEM((cb, cb), F32)] * 2 + [pltpu.VMEM((8, cb), F32)] * 2 + [pltpu.VMEM((HALO, cb), F32)]),
        [uz, uz, uz, hst, hst, dy2, wa_g, wx_g, small], hosted, ("parallel", "arbitrary"))


def _place():
    x, y, c = lax.axis_index("x"), lax.axis_index("y"), lax.axis_index("c")
    chips = [(1 - x, y), (x, 1 - y), (1 - x, 1 - y)]
    return x, y, c, chips


def _rcopy(src, dst, send_sems, recv_sems, k, to):
    return pltpu.make_async_remote_copy(src_ref=src, dst_ref=dst, send_sem=send_sems.at[k], recv_sem=recv_sems.at[k],
                                        device_id=to, device_id_type=MESHID)


def _stage_gather_ici(bufs):
    n = len(bufs)

    def quarter(ref, chip, c, q):
        rq = ref.shape[3] // 2
        return ref.at[2 * chip[0] + chip[1], :, c, pl.ds(q * rq, rq)]

    def copies(refs, ss, rs, off, a, sending):
        x, y, c, _ = _place()
        me, xn, yn, dg = (x, y), (1 - x, y), (x, 1 - y), (1 - x, 1 - y)
        plan = [(0, me, xn, 0), (1, me, xn, 1), (3, me, yn, 1), (2, me, yn, 0),
                (4, xn, yn, 0), (5, yn, xn, 1)]
        if not sending:
            plan = [(0, xn, xn, 0), (1, xn, xn, 1), (3, yn, yn, 1), (2, yn, yn, 0), (4, dg, yn, 0), (5, dg, xn, 1)]
        out = []
        for k, owner, to, q in plan:
            blk = quarter(refs[a], owner, c, q)
            out.append(_rcopy(blk, blk, ss, rs, off + 6 * a + k, (*to, c)))
        return out

    def start(ro, refs, ss, rs, off):
        for a in range(n):
            for cp in copies(refs, ss, rs, off, a, True)[:4]:
                cp.start()

    def mid(ro, refs, ss, rs, off):
        for a in range(n):
            got, out = copies(refs, ss, rs, off, a, False), copies(refs, ss, rs, off, a, True)
            got[0].wait_recv()
            out[4].start()
            got[2].wait_recv()
            out[5].start()

    def finish(ro, refs, ss, rs, off):
        for a in range(n):
            got = copies(refs, ss, rs, off, a, False)
            for k in (1, 3, 4, 5):
                got[k].wait_recv()
            for cp in copies(refs, ss, rs, off, a, True):
                cp.wait_send()

    return dict(ro=[], bufs=list(bufs), nsem=6 * n, start=start, mid=mid, finish=finish)


def _stage_gather_d2d(bufs):
    n = len(bufs)

    def copies(refs, ss, rs, off, sending):
        x, y, c, chips = _place()
        out = []
        for a in range(n):
            for jj, ch in enumerate(chips):
                blk = refs[a].at[2 * ch[0] + ch[1], :, c if sending else 1 - c]
                out.append(_rcopy(blk, blk, ss, rs, off + 3 * a + jj, (x, y, 1 - c)))
        return out

    def start(ro, refs, ss, rs, off):
        for cp in copies(refs, ss, rs, off, True):
            cp.start()

    def finish(ro, refs, ss, rs, off):
        for cp in copies(refs, ss, rs, off, False):
            cp.wait_recv()
        for cp in copies(refs, ss, rs, off, True):
            cp.wait_send()

    return dict(ro=[], bufs=list(bufs), nsem=3 * n, start=start, finish=finish)


def _stage_scatter_ici(parts, gots):
    n = len(parts)

    def copies(ro, refs, ss, rs, off):
        x, y, c, chips = _place()
        return [_rcopy(ro[a].at[2 * ch[0] + ch[1]], refs[a].at[jj], ss, rs, off + 3 * a + jj, (*ch, c))
                for a in range(n) for jj, ch in enumerate(chips)]

    def start(ro, refs, ss, rs, off):
        for cp in copies(ro, refs, ss, rs, off):
            cp.start()

    def finish(ro, refs, ss, rs, off):
        for cp in copies(ro, refs, ss, rs, off):
            cp.wait()

    return dict(ro=list(parts), bufs=list(gots), nsem=3 * n, start=start, finish=finish)


def _stage_send_half(grads, lands):
    n = len(grads)

    def copies(ro, refs, ss, rs, off):
        x, y, c, _ = _place()
        return [_rcopy(ro[a].at[:, :, 1 - c], refs[a], ss, rs, off + a, (x, y, 1 - c)) for a in range(n)]

    def start(ro, refs, ss, rs, off):
        for cp in copies(ro, refs, ss, rs, off):
            cp.start()

    def finish(ro, refs, ss, rs, off):
        for cp in copies(ro, refs, ss, rs, off):
            cp.wait()

    return dict(ro=list(grads), bufs=list(lands), nsem=n, start=start, finish=finish)


def cast_into_slab(w, k_idx):
    l, r, c = w.shape
    rh = r // 2
    lb, tr = _block_lr(l, rh, c)
    nbh = rh // tr

    def body(k_ref, w_ref, o_ref):
        o_ref[...] = w_ref[...].astype(o_ref.dtype)

    return pl.pallas_call(
        body, name="cast_into_slab",
        grid_spec=pltpu.PrefetchScalarGridSpec(
            num_scalar_prefetch=1, grid=(l // lb, 2, nbh),
            in_specs=[pl.BlockSpec((lb, tr, c), lambda i, h, b, k_ref: (i, h * nbh + b, 0))],
            out_specs=pl.BlockSpec((None, lb, None, tr, c), lambda i, h, b, k_ref: (k_ref[0], i, h, b, 0))),
        out_shape=jax.ShapeDtypeStruct((4, l, 2, rh, c), WIRE),
        compiler_params=_cp(("parallel", "parallel", "parallel")),
    )(k_idx, w)


def gather_weights(bufs):
    n = len(bufs)
    ici = [_stage_gather_ici([b]) for b in bufs]
    d2d = [_stage_gather_d2d([b]) for b in bufs]
    per = ici[0]["nsem"] + d2d[0]["nsem"]

    def body(*refs):
        outs = refs[n:2 * n]
        ss, rs = refs[2 * n:]
        for what in ("start", "mid"):
            for a in range(n):
                ici[a][what]([], [outs[a]], ss, rs, per * a)
        for a in range(n):
            ici[a]["finish"]([], [outs[a]], ss, rs, per * a)
            d2d[a]["start"]([], [outs[a]], ss, rs, per * a + ici[a]["nsem"])
        for a in range(n):
            d2d[a]["finish"]([], [outs[a]], ss, rs, per * a + ici[a]["nsem"])

    return pl.pallas_call(
        body, name="gather_weights",
        in_specs=[ANY] * n, out_specs=[ANY] * n,
        out_shape=[jax.ShapeDtypeStruct(a.shape, a.dtype) for a in bufs],
        scratch_shapes=[pltpu.SemaphoreType.DMA((per * n,)), pltpu.SemaphoreType.DMA((per * n,))],
        input_output_aliases={a: a for a in range(n)},
        compiler_params=pltpu.CompilerParams(has_side_effects=True),
    )(*bufs)


def all_gather_small(v, name):
    m_per, n = v.shape

    def body(x_ref, out_ref, send_sems, recv_sems, local_sem):
        x, y, c, chips = _place()
        me, sibling = (x, y, c), (x, y, 1 - c)

        def rows(px, py, pc):
            return out_ref.at[pl.ds((4 * px + 2 * py + pc) * m_per, m_per), :]

        def copy(k, block, to, src=None):
            return _rcopy(rows(*block) if src is None else src, rows(*block), send_sems, recv_sems, k, to)

        mine = pltpu.make_async_copy(x_ref, rows(*me), local_sem)
        mine.start()
        first = [copy(0, me, sibling, src=x_ref)]
        first += [copy(1 + jj, me, (*chip, c), src=x_ref) for jj, chip in enumerate(chips)]
        for cp in first:
            cp.start()
        passed = [copy(4 + jj, (*chip, c), sibling) for jj, chip in enumerate(chips)]
        for jj, chip in enumerate(chips):
            copy(1 + jj, (*chip, c), me).wait_recv()
            passed[jj].start()
        copy(0, sibling, me).wait_recv()
        for jj, chip in enumerate(chips):
            copy(4 + jj, (*chip, 1 - c), me).wait_recv()
        for cp in first + passed:
            cp.wait_send()
        mine.wait()

    return pl.pallas_call(
        body, name=name,
        out_shape=jax.ShapeDtypeStruct((8 * m_per, n), v.dtype),
        in_specs=[pl.BlockSpec(memory_space=pltpu.VMEM)],
        out_specs=pl.BlockSpec(memory_space=pltpu.VMEM),
        scratch_shapes=[pltpu.SemaphoreType.DMA((7,)), pltpu.SemaphoreType.DMA((7,)), pltpu.SemaphoreType.DMA],
        compiler_params=pltpu.CompilerParams(vmem_limit_bytes=VMEM_LIMIT),
    )(v)


def sum_devices(g):
    def body(g_ref, o_ref):
        acc = g_ref[0]
        for d in range(1, 8):
            acc = acc + g_ref[d]
        o_ref[...] = acc

    return pl.pallas_call(body, name="sum_devices", out_shape=jax.ShapeDtypeStruct(g.shape[1:], g.dtype),
                          compiler_params=pltpu.CompilerParams(vmem_limit_bytes=VMEM_LIMIT))(g)


def send_other_half(grads):
    n = len(grads)

    def body(*refs):
        ins, outs = refs[:n], refs[n:2 * n]
        send_sems, recv_sems = refs[2 * n:]
        x, y, c, _ = _place()
        sib = (x, y, 1 - c)
        cps = [_rcopy(ins[a].at[:, :, 1 - c], outs[a], send_sems, recv_sems, a, sib) for a in range(n)]
        for cp in cps:
            cp.start()
        for cp in cps:
            cp.wait()

    return pl.pallas_call(
        body, name="send_other_half", in_specs=[ANY] * n, out_specs=[ANY] * n,
        out_shape=[jax.ShapeDtypeStruct(g.shape[:2] + g.shape[3:], g.dtype) for g in grads],
        scratch_shapes=[pltpu.SemaphoreType.DMA((n,)), pltpu.SemaphoreType.DMA((n,))],
        compiler_params=pltpu.CompilerParams(has_side_effects=True),
    )(*grads)


def scatter_to_chips(parts):
    n = len(parts)

    def body(*refs):
        ins, outs = refs[:n], refs[n:2 * n]
        send_sems, recv_sems = refs[2 * n:]
        x, y, c, chips = _place()
        cps = []
        for a in range(n):
            for jj, ch in enumerate(chips):
                cps.append(_rcopy(ins[a].at[2 * ch[0] + ch[1]], outs[a].at[jj], send_sems, recv_sems, 3 * a + jj, (*ch, c)))
        for cp in cps:
            cp.start()
        for cp in cps:
            cp.wait()

    return pl.pallas_call(
        body, name="scatter_to_chips", in_specs=[ANY] * n, out_specs=[ANY] * n,
        out_shape=[jax.ShapeDtypeStruct((3,) + p.shape[1:], p.dtype) for p in parts],
        scratch_shapes=[pltpu.SemaphoreType.DMA((3 * n,)), pltpu.SemaphoreType.DMA((3 * n,))],
        compiler_params=pltpu.CompilerParams(has_side_effects=True),
    )(*parts)


def share_halves(bufs, spans):
    n = len(bufs)

    def body(*refs):
        outs = refs[n:2 * n]
        send_sems, recv_sems = refs[2 * n:]
        x, y, c, _ = _place()
        sib = (x, y, 1 - c)

        def blk(a, half):
            return outs[a].at[pl.ds(spans[a][0], spans[a][1]), half]

        cps = [_rcopy(blk(a, c), blk(a, c), send_sems, recv_sems, a, sib) for a in range(n)]
        for cp in cps:
            cp.start()
        for a in range(n):
            _rcopy(blk(a, 1 - c), blk(a, 1 - c), send_sems, recv_sems, a, sib).wait_recv()
        for cp in cps:
            cp.wait_send()

    return pl.pallas_call(
        body, name="share_halves", in_specs=[ANY] * n, out_specs=[ANY] * n,
        out_shape=[jax.ShapeDtypeStruct(b.shape, b.dtype) for b in bufs],
        scratch_shapes=[pltpu.SemaphoreType.DMA((n,)), pltpu.SemaphoreType.DMA((n,))],
        input_output_aliases={a: a for a in range(n)},
        compiler_params=pltpu.CompilerParams(has_side_effects=True),
    )(*bufs)


def _block_rows(r, c, itemsize, budget=1 << 20):
    tr = r
    while tr * c * itemsize > budget and tr % 16 == 0:
        tr //= 2
    return tr


def _block_lr(l, r, c, budget=4 << 20):
    tr = _block_rows(r, c, 4, budget)
    lb = 1
    if tr == r:
        while l % (2 * lb) == 0 and 2 * lb * r * c * 4 <= budget:
            lb *= 2
    return lb, tr


def add_halves(g, got, c_idx):
    k4, l, _, rh, cc = g.shape
    lb, tr = _block_lr(l, rh, cc)

    def body(c_ref, g_ref, r_ref, o_ref):
        o_ref[...] = (g_ref[...].astype(F32) + r_ref[...].astype(F32)).astype(o_ref.dtype)

    return pl.pallas_call(
        body, name="add_halves",
        grid_spec=pltpu.PrefetchScalarGridSpec(
            num_scalar_prefetch=1, grid=(k4, l // lb, rh // tr),
            in_specs=[pl.BlockSpec((None, lb, None, tr, cc), lambda k, i, b, c_ref: (k, i, c_ref[0], b, 0)),
                      pl.BlockSpec((None, lb, tr, cc), lambda k, i, b, c_ref: (k, i, b, 0))],
            out_specs=pl.BlockSpec((None, lb, tr, cc), lambda k, i, b, c_ref: (k, i, b, 0))),
        out_shape=jax.ShapeDtypeStruct(got.shape, WIRE),
        compiler_params=_cp(("parallel", "parallel", "parallel")),
    )(c_idx, g, got)


def sum_chips(own, got, kc_idx, full, first):
    _, l, rh, cc = own.shape
    lb, tr = _block_lr(l, rh, cc, 2 << 20)
    assert first % lb == 0

    def body(k_ref, o_ref, r_ref, _full, s_ref):
        s_ref[...] = ((o_ref[...].astype(F32) + r_ref[0].astype(F32)) + r_ref[1].astype(F32)) + r_ref[2].astype(F32)

    return pl.pallas_call(
        body, name="sum_chips",
        grid_spec=pltpu.PrefetchScalarGridSpec(
            num_scalar_prefetch=1, grid=(l // lb, rh // tr),
            in_specs=[pl.BlockSpec((None, lb, tr, cc), lambda i, b, k_ref: (k_ref[0], i, b, 0)),
                      pl.BlockSpec((3, lb, tr, cc), lambda i, b, k_ref: (0, i, b, 0)),
                      ANY],
            out_specs=pl.BlockSpec((lb, None, tr, cc), lambda i, b, k_ref: (first // lb + i, k_ref[1], b, 0))),
        out_shape=jax.ShapeDtypeStruct(full.shape, F32),
        input_output_aliases={3: 0},
        compiler_params=_cp(("parallel", "parallel")),
    )(kc_idx, own, got, full)


def _adam_math(w, g, m, v):
    m = ADAM_B1 * m + (1.0 - ADAM_B1) * g
    v = ADAM_B2 * v + (1.0 - ADAM_B2) * (g * g)
    m_hat = m / (1.0 - ADAM_B1 ** ADAM_STEP)
    v_hat = v / (1.0 - ADAM_B2 ** ADAM_STEP)
    delta = -ADAM_LR * (m_hat / (jnp.sqrt(v_hat) + ADAM_EPS) + ADAM_WD * w)
    return delta, m, v


def adamw(w, g, g_first, m, v, lo, hi, prev=None, hosted=()):
    l, r, c = w.shape
    tr = _block_rows(r, c, 4, 2 << 20)
    prev = list(prev or [])

    def body(w_ref, g_ref, m_ref, v_ref, *rest):
        go_ref, d_ref, mo_ref, vo_ref = rest[len(prev):]
        gv = g_ref[...]
        go_ref[...] = gv
        d_ref[...], mo_ref[...], vo_ref[...] = _adam_math(w_ref[...], gv, m_ref[...], v_ref[...])

    spec = pl.BlockSpec((None, tr, c), lambda i, b: (lo + i, b, 0))
    gspec = pl.BlockSpec((None, tr, c), lambda i, b: (g_first + i, b, 0))
    return _hosting_call(
        body, "adamw", (hi - lo, r // tr), [spec, gspec, spec, spec] + [ANY] * len(prev), [spec] * 4,
        [jax.ShapeDtypeStruct(w.shape, F32)] * 4, [], [w, g, m, v] + prev, hosted, ("parallel", "parallel"),
        aliases={4 + t: t for t in range(len(prev))})


def adamw_small(w, g, m, v):
    def body(w_ref, g_ref, m_ref, v_ref, d_ref, mo_ref, vo_ref):
        d_ref[...], mo_ref[...], vo_ref[...] = _adam_math(w_ref[...], g_ref[...], m_ref[...], v_ref[...])

    return pl.pallas_call(body, name="adamw_small", out_shape=[jax.ShapeDtypeStruct(w.shape, F32)] * 3)(w, g, m, v)


def kernel(x, p, w_in, w_out, g_pre, g_post, pool_w, pool_b, pool_scale, conv_w, conv_b, lru_wa, lru_ba, lru_wx, lru_bx, lru_L, w_ple, w_ple_gate, g_ple_in, g_ple_out, loss_target, m_w_in, m_w_out, m_g_pre, m_g_post, m_pool_w, m_pool_b, m_pool_scale, m_conv_w, m_conv_b, m_lru_wa, m_lru_ba, m_lru_wx, m_lru_bx, m_lru_L, m_w_ple, m_w_ple_gate, m_g_ple_in, m_g_ple_out, v_w_in, v_w_out, v_g_pre, v_g_post, v_pool_w, v_pool_b, v_pool_scale, v_conv_w, v_conv_b, v_lru_wa, v_lru_ba, v_lru_wx, v_lru_bx, v_lru_L, v_w_ple, v_w_ple_gate, v_g_ple_in, v_g_ple_out):
    depth = w_in.shape[0]
    _, s, d = x.shape
    e = 2 * d
    kp = p.shape[-1]
    nmix = pool_w.shape[0]
    ngrp = pool_w.shape[1]
    cg = e // ngrp
    cb = e // LRU_HEADS
    xi, yi, ci = lax.axis_index("x"), lax.axis_index("y"), lax.axis_index("c")
    me = 2 * xi + yi
    c_idx = jnp.reshape(ci, (1,)).astype(jnp.int32)
    k_idx = jnp.reshape(me, (1,)).astype(jnp.int32)
    tr_row = _tile(s, 256)
    tr_mix = _tile(s, 512)
    tm = _tile(s, 1024)
    tm2 = _tile(s, 2048)

    def halves(a):
        return a.reshape(a.shape[0], 2, a.shape[1] // 2, a.shape[2])

    big = {
        "w_in": w_in, "w_out": w_out, "gate": w_ple_gate, "ple": w_ple,
        "pool": pool_w.reshape(nmix * ngrp, cg // 4, cg),
        "wa": lru_wa.reshape(nmix * LRU_HEADS, cb // 4, cb), "wx": lru_wx.reshape(nmix * LRU_HEADS, cb // 4, cb),
    }
    names = list(big)

    def layer_shards(i):
        sh = {"w_in": w_in[i][None], "w_out": w_out[i][None], "gate": w_ple_gate[i][None], "ple": w_ple[i][None]}
        if i % 2 == 0:
            sh["pool"] = pool_w[i // 2]
        else:
            sh["wa"], sh["wx"] = lru_wa[i // 2], lru_wx[i // 2]
        return sh

    def mixer_names(i):
        return ["pool"] if i % 2 == 0 else ["wa", "wx"]

    wbuf = [{n: cast_into_slab(w, k_idx) for n, w in layer_shards(i).items()} for i in range(depth)]
    first = list(wbuf[0])
    wbuf[0] = dict(zip(first, gather_weights([wbuf[0][n] for n in first])))

    def full_w(i, n):
        b = wbuf[i][n]
        return b.reshape(b.shape[0], b.shape[1], 2 * b.shape[3], b.shape[4])

    def run_mm(stages, *args):
        if not stages:
            return _matmul(*args)
        out, new = _matmul(*args, hosted=[mk([wbuf[l][n] for n in nms]) for mk, l, nms in stages])
        slots = [(l, n) for _, l, nms in stages for n in nms]
        for (l, n), b in zip(slots, new):
            wbuf[l][n] = b
        return out

    ec = e // 4
    small_loc = jnp.concatenate([conv_w, conv_b[:, None], lru_ba[:, None], lru_bx[:, None], lru_L[:, None]], axis=1)
    sm_all = all_gather_small(small_loc.reshape(nmix * 8, ec), "gather_small").reshape(4, 2, nmix, 8, ec)
    lru_small = jnp.transpose(sm_all[:, 0], (1, 2, 0, 3)).reshape(nmix, 8, e)

    xs = x[0]
    saved = []
    for i in range(depth):
        j = i // 2
        h = rms_fwd(xs, g_pre[i][None], tr_row) if i == 0 else h_next
        nj = (2 * e) // 1024 if (2 * e) % 1024 == 0 else 4
        tn = (2 * e) // nj
        per = e // tn
        perk = (e // 2) // tn
        nxt = i + 1 if i + 1 < depth else None
        stages = [(_stage_gather_d2d, i, ["gate", "ple"])] if i > 0 else []
        if nxt is not None:
            stages.append((_stage_gather_ici, nxt, ["w_in"]))
        uz = run_mm(
            stages, "mm_in", "nn", (s // tm2, nj, 1), h, pl.BlockSpec((tm2, d), lambda a, b, k: (a, 0)),
            full_w(i, "w_in"), pl.BlockSpec((None, None, d, tn), lambda a, b, k, perk=perk: (b // perk, 0, 0, b % perk)),
            jax.ShapeDtypeStruct((2, s, e), MXU), pl.BlockSpec((None, tm2, tn), lambda a, b, k, per=per: (b // per, a, b % per)),
            (8, 128))
        if i % 2 == 0:
            y2 = pool_fwd(uz, full_w(i, "pool"), pool_b[j][None], pool_scale[j][None], tr_mix)
            hst = None
        else:
            y2, hst = lru_fwd(uz, full_w(i, "wa"), full_w(i, "wx"), lru_small[j], tr_mix)
        tn_o = _tile(d, 1024)
        stages = [(_stage_gather_ici, nxt, ["w_out"] + mixer_names(nxt))] if nxt is not None else []
        o = run_mm(
            stages, "mm_out", "nn", (s // tm, d // tn_o, 1), y2, pl.BlockSpec((tm, e), lambda a, b, k: (a, 0)),
            full_w(i, "w_out"), pl.BlockSpec((4, None, e // 4, tn_o), lambda a, b, k: (0, 0, 0, b)),
            jax.ShapeDtypeStruct((s, d), MXU), pl.BlockSpec((tm, tn_o), lambda a, b, k: (a, b)), (8, 128))
        x1, hn = res_rms_fwd(xs, o, g_post[i][None], g_ple_in[i][None], tr_row)
        stages = []
        if nxt is not None:
            stages = [(_stage_gather_ici, nxt, ["gate", "ple"]), (_stage_gather_d2d, nxt, ["w_in", "w_out"] + mixer_names(nxt))]
        gpre = run_mm(
            stages, "mm_gate", "nn", (s // tm2, d // tn_o, 1), hn, pl.BlockSpec((tm2, d), lambda a, b, k: (a, 0)),
            full_w(i, "gate"), pl.BlockSpec((4, None, d // 4, tn_o), lambda a, b, k: (0, 0, 0, b)),
            jax.ShapeDtypeStruct((s, d), MXU), pl.BlockSpec((tm2, tn_o), lambda a, b, k: (a, b)), (8, 128))
        pe = p[i, 0]
        ev = _matmul(
            "mm_ple", "nn", (s // tm, 4, 1), pe, pl.BlockSpec((tm, kp), lambda a, b, k: (a, 0)),
            full_w(i, "ple"), pl.BlockSpec((None, None, kp, d // 4), lambda a, b, k: (b, 0, 0, 0)),
            jax.ShapeDtypeStruct((s, d), MXU), pl.BlockSpec((tm, d // 4), lambda a, b, k: (a, b)), (8, 128))
        if nxt is not None:
            x2, h_next = ple_fwd(x1, gpre, ev, g_ple_out[i][None], g_pre[nxt][None], tr_row)
        else:
            x2 = ple_fwd_last(x1, gpre, ev, g_ple_out[i][None], tr_row)
        saved.append((xs, h, uz, y2, hst, o, x1, hn, gpre, ev))
        xs = x2

    dx, sq = loss_bwd(xs, loss_target[0], tr_row)
    d_gpre, d_gpost, d_gin, d_gout = [None] * depth, [None] * depth, [None] * depth, [None] * depth
    d_pool_b, d_pool_sc, d_lru_small = [None] * nmix, [None] * nmix, [None] * nmix
    ts = _tile(s, 1024)
    kc_idx = jnp.stack([me, ci]).astype(jnp.int32)
    full = {n: lax.empty(halves(big[n]).shape, F32) for n in names}
    sums, gots = [None] * depth, [None] * depth

    def scatter_stage(l, nms):
        return _stage_scatter_ici([sums[l][n] for n in nms], [gots[l][n] for n in nms]), [(l, n) for n in nms]

    def run_mm_scatter(specs, *args):
        if not specs:
            return _matmul(*args)
        out, new = _matmul(*args, hosted=[st for st, _ in specs])
        for (l, n), b in zip([slot for _, sl in specs for slot in sl], new):
            gots[l][n] = b
        return out

    def first_row(i, n):
        return i if n in ("w_in", "w_out", "gate", "ple") else (i // 2) * (ngrp if n == "pool" else LRU_HEADS)

    for i in reversed(range(depth)):
        j = i // 2
        prev = i + 1 if i + 1 < depth else None
        x0, h, uz, y2, hst, o, x1, hn, gpre, ev = saved[i]
        pe = p[i, 0]
        gl = {}
        de, dgp, d_gout[i] = ple_bwd(dx, gpre, ev, g_ple_out[i][None], tr_row)
        gl["ple"] = _matmul(
            "mm_dple", "tn", (1, 4, s // ts), pe, pl.BlockSpec((ts, kp), lambda a, b, k: (k, 0)),
            de, pl.BlockSpec((ts, d // 4), lambda a, b, k: (k, b)),
            jax.ShapeDtypeStruct((4, 1, kp, d // 4), WIRE), pl.BlockSpec((None, None, kp, d // 4), lambda a, b, k: (b, 0, 0, 0)),
            (kp, d // 4))
        tn_o = _tile(d, 1024)
        gl["gate"] = _matmul(
            "mm_dgate", "tn", (4, d // tn_o, 1), hn, pl.BlockSpec((s, d // 4), lambda a, b, k: (0, a)),
            dgp, pl.BlockSpec((s, tn_o), lambda a, b, k: (0, b)),
            jax.ShapeDtypeStruct((4, 1, d // 4, d), WIRE), pl.BlockSpec((None, None, d // 4, tn_o), lambda a, b, k: (a, 0, 0, b)),
            (8, 128))
        dhn = _matmul(
            "mm_dhn", "nt", (s // tm2, 4, 1), dgp, pl.BlockSpec((tm2, d), lambda a, b, k: (a, 0)),
            full_w(i, "gate"), pl.BlockSpec((None, None, d // 4, d), lambda a, b, k: (b, 0, 0, 0)),
            jax.ShapeDtypeStruct((s, d), MXU), pl.BlockSpec((tm2, d // 4), lambda a, b, k: (a, b)), (8, 128))
        dx1, do, d_gin[i], d_gpost[i] = post_bwd(dx, dhn, x1, g_ple_in[i][None], o, g_post[i][None], tr_row)
        gl["w_out"] = _matmul(
            "mm_dwout", "tn", (4, d // tn_o, 1), y2, pl.BlockSpec((s, e // 4), lambda a, b, k: (0, a)),
            do, pl.BlockSpec((s, tn_o), lambda a, b, k: (0, b)),
            jax.ShapeDtypeStruct((4, 1, e // 4, d), WIRE), pl.BlockSpec((None, None, e // 4, tn_o), lambda a, b, k: (a, 0, 0, b)),
            (8, 128))
        dy2 = _matmul(
            "mm_dy2", "nt", (s // tm2, 4, 1), do, pl.BlockSpec((tm2, d), lambda a, b, k: (a, 0)),
            full_w(i, "w_out"), pl.BlockSpec((None, None, e // 4, d), lambda a, b, k: (b, 0, 0, 0)),
            jax.ShapeDtypeStruct((s, e), MXU), pl.BlockSpec((tm2, e // 4), lambda a, b, k: (a, b)), (8, 128))
        riding = [scatter_stage(prev, ["w_in"])[0]] if prev is not None else []
        if i % 2 == 0:
            (duz, gl["pool"], d_pool_b[j], d_pool_sc[j]), passed = pool_bwd(
                uz, dy2, full_w(i, "pool"), pool_b[j][None], pool_scale[j][None], tr_mix, hosted=riding)
        else:
            (duz, gl["wa"], gl["wx"], d_lru_small[j]), passed = lru_bwd(
                uz, hst, dy2, full_w(i, "wa"), full_w(i, "wx"), lru_small[j], tr_mix, hosted=riding)
        if prev is not None:
            gots[prev]["w_in"] = passed[0]
        tmi = _tile(d, 1024)
        tni = _tile(e // 2, 1024)
        nslab = (e // 2) // tni
        gl["w_in"] = run_mm_scatter(
            [scatter_stage(prev, ["w_out"])] if prev is not None else [],
            "mm_dwin", "tn", (d // tmi, 4 * nslab, 1), h, pl.BlockSpec((s, tmi), lambda a, b, k: (0, a)),
            duz, pl.BlockSpec((None, s, tni), lambda a, b, k, nslab=nslab: (b // (2 * nslab), 0, b % (2 * nslab))),
            jax.ShapeDtypeStruct((4, 1, d, e // 2), WIRE),
            pl.BlockSpec((None, None, tmi, tni), lambda a, b, k, nslab=nslab: (b // nslab, 0, a, b % nslab)),
            (8, 128))
        lnames = list(gl)
        gparts = [gl[n].reshape(4, gl[n].shape[1], 2, gl[n].shape[2] // 2, gl[n].shape[3]) for n in lnames]
        hosted = [_stage_send_half(gparts, [lax.empty(g.shape[:2] + g.shape[3:], WIRE) for g in gparts])]
        rest = ["gate", "ple"] + mixer_names(prev) if prev is not None else []
        if prev is not None:
            hosted.append(_stage_scatter_ici([sums[prev][n] for n in rest], [gots[prev][n] for n in rest]))
        tnd = _tile(d, 1024)
        dh, passed = _matmul(
            "mm_dh", "nt", (s // tm, d // tnd, 4), duz, pl.BlockSpec((None, tm, e // 2), lambda a, b, k: (k // 2, a, k % 2)),
            full_w(i, "w_in"), pl.BlockSpec((None, None, tnd, e // 2), lambda a, b, k: (k, 0, b, 0)),
            jax.ShapeDtypeStruct((s, d), MXU), pl.BlockSpec((tm, tnd), lambda a, b, k: (a, b)), (tm, tnd), hosted=hosted)
        from_sib = passed[:len(lnames)]
        for n, b in zip(rest, passed[len(lnames):]):
            gots[prev][n] = b
        dx, d_gpre[i] = rms_bwd_res(dx1, dh, x0, g_pre[i][None], tr_row)
        if prev is not None:
            for n in gots[prev]:
                full[n] = sum_chips(sums[prev][n], gots[prev][n], kc_idx, full[n], first_row(prev, n))
        sums[i] = {n: add_halves(g, r, c_idx) for n, g, r in zip(lnames, gparts, from_sib)}
        gots[i] = {n: lax.empty((3,) + sums[i][n].shape[1:], WIRE) for n in lnames}
    grad_x = dx[None]

    row0 = {n: (gots[0][n].shape[1] if n in gots[0] else 0) for n in names}
    shared = share_halves([full[n] for n in names], [(row0[n], full[n].shape[0] - row0[n]) for n in names])
    grads = {n: f.reshape(big[n].shape) for n, f in zip(names, shared)}

    def rows_e(a):
        return jnp.stack(a).reshape(-1, e) if isinstance(a, list) else a.reshape(-1, e)

    pack = [rows_e([g[0] for g in d_gpre]), rows_e([g[0] for g in d_gpost]), rows_e([g[0] for g in d_gin]), rows_e([g[0] for g in d_gout]),
            jnp.concatenate(d_pool_b, axis=0), jnp.concatenate(d_pool_sc, axis=0), jnp.concatenate(d_lru_small, axis=0),
            jnp.pad(sq, ((0, 0), (0, e - d)))]
    sizes = [a.shape[0] for a in pack]
    packed = jnp.concatenate(pack, axis=0)
    nrow = packed.shape[0]
    nrow_p = -(-nrow // 8) * 8
    packed = jnp.pad(packed, ((0, nrow_p - nrow), (0, 0)))
    total = sum_devices(all_gather_small(packed, "gather_grads").reshape(8, nrow_p, e))
    parts, off = [], 0
    for n_ in sizes:
        parts.append(total[off:off + n_])
        off += n_
    t_gpre, t_gpost, t_gin, t_gout, t_pb, t_psc, t_lru, t_sq = parts
    loss = 0.5 * jnp.sum(t_sq) / d
    t_lru = t_lru.reshape(nmix, 8, e)
    t_lru_loc = lax.dynamic_slice_in_dim(t_lru, me * ec, ec, axis=2)

    later = {}

    def big_update_later(name, w, m, v):
        shp = big[name].shape
        riding = [_stage_scatter_ici([sums[0][name]], [gots[0][name]])] if row0[name] else []
        later[name], passed = adamw(w.reshape(shp), grads[name], row0[name], m.reshape(shp), v.reshape(shp), row0[name], shp[0],
                                    hosted=riding)
        if riding:
            gots[0][name] = passed[0]

    def big_update(name, w, m, v):
        shp = big[name].shape
        res4 = later[name]
        if row0[name]:
            res4, _ = adamw(w.reshape(shp), grads0[name], 0, m.reshape(shp), v.reshape(shp), 0, row0[name], prev=res4)
        return [a.reshape(w.shape) for a in res4]

    for name, w, m, v in (("w_in", w_in, m_w_in, v_w_in), ("w_out", w_out, m_w_out, v_w_out), ("gate", w_ple_gate, m_w_ple_gate, v_w_ple_gate),
                          ("pool", pool_w, m_pool_w, v_pool_w), ("ple", w_ple, m_w_ple, v_w_ple),
                          ("wa", lru_wa, m_lru_wa, v_lru_wa), ("wx", lru_wx, m_lru_wx, v_lru_wx)):
        big_update_later(name, w, m, v)
    n0 = list(gots[0])
    sum0 = [sum_chips(sums[0][n], gots[0][n], kc_idx, lax.empty((row0[n], 2) + sums[0][n].shape[2:], F32), 0) for n in n0]
    grads0 = {n: f.reshape((row0[n],) + big[n].shape[1:]) for n, f in zip(n0, share_halves(sum0, [(0, row0[n]) for n in n0]))}

    def small_update(w, g, m, v):
        shp = w.shape
        w2 = w.reshape(-1, shp[-1])
        dl, nm, nv = adamw_small(w2, g.reshape(w2.shape), m.reshape(w2.shape), v.reshape(w2.shape))
        return [g.reshape(shp), dl.reshape(shp), nm.reshape(shp), nv.reshape(shp)]

    res = {
        "w_in": big_update("w_in", w_in, m_w_in, v_w_in),
        "w_out": big_update("w_out", w_out, m_w_out, v_w_out),
        "g_pre": small_update(g_pre, t_gpre.reshape(depth, d), m_g_pre, v_g_pre),
        "g_post": small_update(g_post, t_gpost.reshape(depth, d), m_g_post, v_g_post),
        "pool_w": big_update("pool", pool_w, m_pool_w, v_pool_w),
        "pool_b": small_update(pool_b, t_pb, m_pool_b, v_pool_b),
        "pool_scale": small_update(pool_scale, t_psc, m_pool_scale, v_pool_scale),
        "conv_w": small_update(conv_w, t_lru_loc[:, :CONV_W], m_conv_w, v_conv_w),
        "conv_b": small_update(conv_b, t_lru_loc[:, 4], m_conv_b, v_conv_b),
        "lru_wa": big_update("wa", lru_wa, m_lru_wa, v_lru_wa),
        "lru_ba": small_update(lru_ba, t_lru_loc[:, 5], m_lru_ba, v_lru_ba),
        "lru_wx": big_update("wx", lru_wx, m_lru_wx, v_lru_wx),
        "lru_bx": small_update(lru_bx, t_lru_loc[:, 6], m_lru_bx, v_lru_bx),
        "lru_L": small_update(lru_L, t_lru_loc[:, 7], m_lru_L, v_lru_L),
        "w_ple": big_update("ple", w_ple, m_w_ple, v_w_ple),
        "w_ple_gate": big_update("gate", w_ple_gate, m_w_ple_gate, v_w_ple_gate),
        "g_ple_in": small_update(g_ple_in, t_gin.reshape(depth, d), m_g_ple_in, v_g_ple_in),
        "g_ple_out": small_update(g_ple_out, t_gout.reshape(depth, d), m_g_ple_out, v_g_ple_out),
    }
    order = ["w_in", "w_out", "g_pre", "g_post", "pool_w", "pool_b", "pool_scale", "conv_w", "conv_b", "lru_wa", "lru_ba",
             "lru_wx", "lru_bx", "lru_L", "w_ple", "w_ple_gate", "g_ple_in", "g_ple_out"]
    out = [loss, grad_x]
    for slot in range(4):
        out += [res[n][slot] for n in order]
    return tuple(out)
```

```python
import functools

import jax
import jax.numpy as jnp
from jax import lax
from jax.experimental import pallas as pl
from jax.experimental.pallas import tpu as pltpu

F32 = jnp.float32
MXU = jnp.bfloat16
WIRE = jnp.bfloat16
VMEM_LIMIT = 56 * 1024 * 1024
RMS_EPS = 1e-6
LRU_C = 8.0
POOL_WINDOWS = (2, 4, 8, 16)
MAXW = 16
CONV_W = 4
LRU_HEADS = 16
ADAM_LR, ADAM_B1, ADAM_B2, ADAM_EPS, ADAM_WD, ADAM_STEP = 0.001, 0.9, 0.999, 1e-08, 0.01, 10
MESHID = pl.DeviceIdType.MESH
ANY = pl.BlockSpec(memory_space=pl.ANY)


def _cp(sem=None):
    return pltpu.CompilerParams(dimension_semantics=sem, vmem_limit_bytes=VMEM_LIMIT)


def _sig(v):
    return 0.5 * jnp.tanh(0.5 * v) + 0.5


def _tile(n, pref):
    return pref if n % pref == 0 else n


_DN = {"nn": (((1,), (0,)), ((), ())), "nt": (((1,), (1,)), ((), ())), "tn": (((0,), (0,)), ((), ()))}


def _hosting_call(body, name, grid, in_specs, out_specs, out_shape, scratch_shapes, args, hosted, sem, aliases=None):
    n_in, n_out = len(args), len(out_shape)
    aliases = dict(aliases or {})
    if not hosted:
        res = pl.pallas_call(body, name=name, grid=grid, in_specs=in_specs, out_specs=out_specs, out_shape=out_shape,
                             scratch_shapes=scratch_shapes, input_output_aliases=aliases, compiler_params=_cp(sem))(*args)
        return list(res), []
    ro = [r for st in hosted for r in st["ro"]]
    bufs = [r for st in hosted for r in st["bufs"]]
    nro, nbuf = len(ro), len(bufs)
    nsem = sum(st["nsem"] for st in hosted)
    total = 1
    for g in grid:
        total *= g

    def wrapped(*refs):
        ins, ro_refs = refs[:n_in], refs[n_in:n_in + nro]
        outs = refs[n_in + nro + nbuf:n_in + nro + nbuf + n_out]
        buf_refs = refs[n_in + nro + nbuf + n_out:n_in + nro + 2 * nbuf + n_out]
        scr, ss, rs = refs[n_in + nro + 2 * nbuf + n_out:-2], refs[-2], refs[-1]
        step = 0
        for t, g in enumerate(grid):
            step = step * g + pl.program_id(t)

        def run_stages(what):
            r0 = b0 = s0 = 0
            for st in hosted:
                if what in st:
                    st[what](ro_refs[r0:r0 + len(st["ro"])], buf_refs[b0:b0 + len(st["bufs"])], ss, rs, s0)
                r0, b0, s0 = r0 + len(st["ro"]), b0 + len(st["bufs"]), s0 + st["nsem"]

        @pl.when(step == 0)
        def _():
            run_stages("start")

        if any("mid" in st for st in hosted):
            @pl.when(step == total // 2)
            def _():
                run_stages("mid")

        body(*ins, *outs, *scr)

        @pl.when(step == total - 1)
        def _():
            run_stages("finish")

    res = pl.pallas_call(
        wrapped, name=name, grid=grid,
        in_specs=list(in_specs) + [ANY] * (nro + nbuf), out_specs=list(out_specs) + [ANY] * nbuf,
        out_shape=list(out_shape) + [jax.ShapeDtypeStruct(x.shape, x.dtype) for x in bufs],
        scratch_shapes=list(scratch_shapes) + [pltpu.SemaphoreType.DMA((nsem,)), pltpu.SemaphoreType.DMA((nsem,))],
        input_output_aliases={**aliases, **{n_in + nro + t: n_out + t for t in range(nbuf)}},
        compiler_params=_cp(("arbitrary",) * len(grid)),
    )(*args, *ro, *bufs)
    return list(res[:n_out]), list(res[n_out:])


def _matmul(name, mode, grid, a, a_spec, b, b_spec, out_shape, out_spec, acc_shape, hosted=()):
    nk = grid[2]

    def body(a_ref, b_ref, o_ref, acc_ref):
        kk = pl.program_id(2)
        bv = b_ref[...]
        if bv.ndim == 3:
            bv = bv.reshape(bv.shape[0] * bv.shape[1], bv.shape[2])
        prod = lax.dot_general(a_ref[...].astype(MXU), bv.astype(MXU), _DN[mode], preferred_element_type=F32)
        if nk == 1:
            o_ref[...] = prod.astype(o_ref.dtype)
        else:
            @pl.when(kk == 0)
            def _():
                acc_ref[...] = prod

            @pl.when(kk > 0)
            def _():
                acc_ref[...] += prod

            @pl.when(kk == nk - 1)
            def _():
                o_ref[...] = acc_ref[...].astype(o_ref.dtype)

    outs, passed = _hosting_call(body, name, grid, [a_spec, b_spec], [out_spec], [out_shape], [pltpu.VMEM(acc_shape, F32)],
                                 [a, b], hosted, ("parallel", "parallel", "arbitrary"))
    return (outs[0], passed) if hosted else outs[0]


def _rows_call(name, body, ins, in_rows, outs, out_rows, n_rows, tr):
    def spec(shape, tiled):
        if tiled:
            return pl.BlockSpec((tr, shape[1]), lambda i: (i, 0))
        return pl.BlockSpec(shape, lambda i: (0, 0))

    return pl.pallas_call(
        body, name=name, grid=(n_rows // tr,),
        in_specs=[spec(a.shape, t) for a, t in zip(ins, in_rows)],
        out_specs=[spec(o.shape, t) for o, t in zip(outs, out_rows)],
        out_shape=outs, compiler_params=_cp(("arbitrary",)),
    )(*ins)


def _rstd(v):
    return lax.rsqrt(jnp.mean(v * v, axis=-1, keepdims=True) + RMS_EPS)


def _norm_bwd(v, g, dy):
    r = _rstd(v)
    n = v * r
    dn = dy * g
    dv = r * (dn - n * jnp.mean(dn * n, axis=-1, keepdims=True))
    return dv, jnp.sum(dy * n, axis=0, keepdims=True)


def _acc_rows(ref, val):
    @pl.when(pl.program_id(0) == 0)
    def _():
        ref[...] = val

    @pl.when(pl.program_id(0) > 0)
    def _():
        ref[...] += val


def rms_fwd(x, g, tr):
    def body(x_ref, g_ref, o_ref):
        v = x_ref[...]
        o_ref[...] = (v * _rstd(v) * g_ref[...]).astype(o_ref.dtype)

    return _rows_call("rms_fwd", body, [x, g], [True, False], [jax.ShapeDtypeStruct(x.shape, MXU)], [True], x.shape[0], tr)[0]


def res_rms_fwd(x, o, g, g_next, tr):
    def body(x_ref, o_ref, g_ref, gn_ref, y_ref, h_ref):
        v = o_ref[...].astype(F32)
        y = x_ref[...] + v * _rstd(v) * g_ref[...]
        y_ref[...] = y
        h_ref[...] = (y * _rstd(y) * gn_ref[...]).astype(h_ref.dtype)

    return _rows_call("res_rms_fwd", body, [x, o, g, g_next], [True, True, False, False],
                      [jax.ShapeDtypeStruct(x.shape, F32), jax.ShapeDtypeStruct(x.shape, MXU)], [True, True], x.shape[0], tr)


def ple_fwd(x1, gpre, e, g, g_next, tr):
    def body(x_ref, gp_ref, e_ref, g_ref, gn_ref, y_ref, h_ref):
        v = e_ref[...].astype(F32) * _sig(gp_ref[...].astype(F32))
        y = x_ref[...] + v * _rstd(v) * g_ref[...]
        y_ref[...] = y
        h_ref[...] = (y * _rstd(y) * gn_ref[...]).astype(h_ref.dtype)

    return _rows_call("ple_fwd", body, [x1, gpre, e, g, g_next], [True, True, True, False, False],
                      [jax.ShapeDtypeStruct(x1.shape, F32), jax.ShapeDtypeStruct(x1.shape, MXU)], [True, True], x1.shape[0], tr)


def ple_fwd_last(x1, gpre, e, g, tr):
    def body(x_ref, gp_ref, e_ref, g_ref, y_ref):
        v = e_ref[...].astype(F32) * _sig(gp_ref[...].astype(F32))
        y_ref[...] = x_ref[...] + v * _rstd(v) * g_ref[...]

    return _rows_call("ple_fwd_last", body, [x1, gpre, e, g], [True, True, True, False],
                      [jax.ShapeDtypeStruct(x1.shape, F32)], [True], x1.shape[0], tr)[0]


def loss_bwd(y, target, tr):
    d = y.shape[1]

    def body(y_ref, t_ref, dy_ref, sq_ref):
        diff = y_ref[...] - t_ref[...]
        dy_ref[...] = diff * (1.0 / d)
        _acc_rows(sq_ref, jnp.sum(diff * diff, axis=0, keepdims=True))

    return _rows_call("loss_bwd", body, [y, target], [True, True],
                      [jax.ShapeDtypeStruct(y.shape, F32), jax.ShapeDtypeStruct((1, d), F32)], [True, False], y.shape[0], tr)


def ple_bwd(dx2, gpre, e, g, tr):
    d = dx2.shape[1]

    def body(dx_ref, gp_ref, e_ref, g_ref, de_ref, dgp_ref, dg_ref):
        gate = _sig(gp_ref[...].astype(F32))
        ev = e_ref[...].astype(F32)
        dv, dg = _norm_bwd(ev * gate, g_ref[...], dx_ref[...])
        de_ref[...] = (dv * gate).astype(de_ref.dtype)
        dgp_ref[...] = (dv * ev * gate * (1.0 - gate)).astype(dgp_ref.dtype)
        _acc_rows(dg_ref, dg)

    return _rows_call("ple_bwd", body, [dx2, gpre, e, g], [True, True, True, False],
                      [jax.ShapeDtypeStruct(dx2.shape, MXU), jax.ShapeDtypeStruct(dx2.shape, MXU), jax.ShapeDtypeStruct((1, d), F32)],
                      [True, True, False], dx2.shape[0], tr)


def rms_bwd_res(dres, dh, x, g, tr):
    d = x.shape[1]

    def body(dr_ref, dh_ref, x_ref, g_ref, dx_ref, dg_ref):
        dv, dg = _norm_bwd(x_ref[...], g_ref[...], dh_ref[...].astype(F32))
        dx_ref[...] = dr_ref[...] + dv
        _acc_rows(dg_ref, dg)

    return _rows_call("rms_bwd_res", body, [dres, dh, x, g], [True, True, True, False],
                      [jax.ShapeDtypeStruct(x.shape, F32), jax.ShapeDtypeStruct((1, d), F32)], [True, False], x.shape[0], tr)


def post_bwd(dres, dh, x, g, o, g_o, tr):
    d = x.shape[1]

    def body(dr_ref, dh_ref, x_ref, g_ref, o_ref, go_ref, dx_ref, do_ref, dg_ref, dgo_ref):
        dv, dg = _norm_bwd(x_ref[...], g_ref[...], dh_ref[...].astype(F32))
        dxv = dr_ref[...] + dv
        dx_ref[...] = dxv
        dov, dgo = _norm_bwd(o_ref[...].astype(F32), go_ref[...], dxv)
        do_ref[...] = dov.astype(do_ref.dtype)
        _acc_rows(dg_ref, dg)
        _acc_rows(dgo_ref, dgo)

    return _rows_call("post_bwd", body, [dres, dh, x, g, o, g_o], [True, True, True, False, True, False],
                      [jax.ShapeDtypeStruct(x.shape, F32), jax.ShapeDtypeStruct(x.shape, MXU),
                       jax.ShapeDtypeStruct((1, d), F32), jax.ShapeDtypeStruct((1, d), F32)], [True, True, False, False], x.shape[0], tr)


def _trailing_sum(ext, w):
    s, k = ext, 1
    while k < w:
        s = s + pltpu.roll(s, k, 0)
        k *= 2
    return s


def _leading_sum(ext, w):
    n = ext.shape[0]
    s, k = ext, 1
    while k < w:
        s = s + pltpu.roll(s, n - k, 0)
        k *= 2
    return s


def _pool_inv_count(rb, tr, w, c):
    t = rb * tr + lax.broadcasted_iota(jnp.int32, (tr, c), 0)
    return 1.0 / jnp.minimum(t + 1, w).astype(F32)


def _pool_d(u_ref, up_ref, rb, tr, w):
    cur = u_ref[...].astype(F32)
    prev = jnp.where(rb > 0, up_ref[...].astype(F32), 0.0)
    ext = jnp.concatenate([prev, cur], axis=0)
    win = _trailing_sum(ext, w)[MAXW:]
    return win * _pool_inv_count(rb, tr, w, cur.shape[1]) - cur


def pool_fwd(uz, w_g, bias, scale, tr):
    _, s, e = uz.shape
    ng = len(POOL_WINDOWS)
    cg = e // ng
    nb = s // tr
    hb = tr // MAXW

    def body(u_ref, up_ref, z_ref, w_ref, b_ref, sc_ref, o_ref):
        g, rb = pl.program_id(0), pl.program_id(1)
        wmat = w_ref[...].reshape(cg, cg)
        for gg, win in enumerate(POOL_WINDOWS):
            @pl.when(g == gg)
            def _(win=win):
                d = _pool_d(u_ref, up_ref, rb, tr, win)
                y = (jnp.dot(d.astype(MXU), wmat, preferred_element_type=F32) + b_ref[...]) * sc_ref[...]
                z = z_ref[...].astype(F32)
                o_ref[...] = (y * z * _sig(z)).astype(o_ref.dtype)

    return pl.pallas_call(
        body, name="pool_fwd", grid=(ng, nb),
        in_specs=[
            pl.BlockSpec((None, tr, cg), lambda g, r: (0, r, g)),
            pl.BlockSpec((None, MAXW, cg), lambda g, r: (0, jnp.maximum(r * hb - 1, 0), g)),
            pl.BlockSpec((None, tr, cg), lambda g, r: (1, r, g)),
            pl.BlockSpec((4, None, cg // 4, cg), lambda g, r: (0, g, 0, 0)),
            pl.BlockSpec((1, cg), lambda g, r: (0, g)),
            pl.BlockSpec((1, cg), lambda g, r: (0, g)),
        ],
        out_specs=pl.BlockSpec((tr, cg), lambda g, r: (r, g)),
        out_shape=jax.ShapeDtypeStruct((s, e), MXU),
        compiler_params=_cp(("parallel", "arbitrary")),
    )(uz, uz, uz, w_g, bias, scale)


def pool_bwd(uz, dy2, w_g, bias, scale, tr, hosted=()):
    _, s, e = uz.shape
    ng = len(POOL_WINDOWS)
    cg = e // ng
    nb = s // tr
    hb = tr // MAXW

    def body(u_ref, up_ref, z_ref, dy_ref, w_ref, b_ref, sc_ref, duz_ref, gw_ref, db_ref, dsc_ref, acc_ref, carry_ref):
        g, step = pl.program_id(0), pl.program_id(1)
        rb = nb - 1 - step
        wmat = w_ref[...].reshape(cg, cg)
        for gg, win in enumerate(POOL_WINDOWS):
            @pl.when(g == gg)
            def _(win=win):
                d = _pool_d(u_ref, up_ref, rb, tr, win).astype(MXU)
                ypre = jnp.dot(d, wmat, preferred_element_type=F32) + b_ref[...]
                z = z_ref[...].astype(F32)
                sg = _sig(z)
                dy2v = dy_ref[...].astype(F32)
                dyv = dy2v * z * sg
                duz_ref[1] = (dy2v * ypre * sc_ref[...] * sg * (1.0 + z * (1.0 - sg))).astype(duz_ref.dtype)
                dypre = dyv * sc_ref[...]
                dsc = jnp.sum(dyv * ypre, axis=0, keepdims=True)
                dbv = jnp.sum(dypre, axis=0, keepdims=True)
                dypre_b = dypre.astype(MXU)
                dd = lax.dot_general(dypre_b, wmat, _DN["nt"], preferred_element_type=F32)
                gw = lax.dot_general(d, dypre_b, _DN["tn"], preferred_element_type=F32)
                q = dd * _pool_inv_count(rb, tr, win, cg)
                nxt = jnp.where(step > 0, carry_ref[...], 0.0)
                lead = _leading_sum(jnp.concatenate([q, nxt], axis=0), win)[:tr]
                duz_ref[0] = (lead - dd).astype(duz_ref.dtype)
                carry_ref[...] = q[:MAXW]

                @pl.when(step == 0)
                def _():
                    acc_ref[...] = gw
                    db_ref[...] = dbv
                    dsc_ref[...] = dsc

                @pl.when(step > 0)
                def _():
                    acc_ref[...] += gw
                    db_ref[...] += dbv
                    dsc_ref[...] += dsc

                @pl.when(step == nb - 1)
                def _():
                    gw_ref[...] = acc_ref[...].reshape(4, cg // 4, cg).astype(gw_ref.dtype)

    return _hosting_call(
        body, "pool_bwd", (ng, nb),
        [
            pl.BlockSpec((None, tr, cg), lambda g, r: (0, nb - 1 - r, g)),
            pl.BlockSpec((None, MAXW, cg), lambda g, r: (0, jnp.maximum((nb - 1 - r) * hb - 1, 0), g)),
            pl.BlockSpec((None, tr, cg), lambda g, r: (1, nb - 1 - r, g)),
            pl.BlockSpec((tr, cg), lambda g, r: (nb - 1 - r, g)),
            pl.BlockSpec((4, None, cg // 4, cg), lambda g, r: (0, g, 0, 0)),
            pl.BlockSpec((1, cg), lambda g, r: (0, g)),
            pl.BlockSpec((1, cg), lambda g, r: (0, g)),
        ],
        [
            pl.BlockSpec((2, tr, cg), lambda g, r: (0, nb - 1 - r, g)),
            pl.BlockSpec((4, None, cg // 4, cg), lambda g, r: (0, g, 0, 0)),
            pl.BlockSpec((1, cg), lambda g, r: (0, g)),
            pl.BlockSpec((1, cg), lambda g, r: (0, g)),
        ],
        [
            jax.ShapeDtypeStruct((2, s, e), MXU),
            jax.ShapeDtypeStruct(w_g.shape, WIRE),
            jax.ShapeDtypeStruct((1, e), F32),
            jax.ShapeDtypeStruct((1, e), F32),
        ],
        [pltpu.VMEM((cg, cg), F32), pltpu.VMEM((MAXW, cg), F32)],
        [uz, uz, uz, dy2, w_g, bias, scale], hosted, ("parallel", "arbitrary"))


HALO = 16


def _one_minus_sq(log_a, a):
    poly = (-2.0 * log_a) * (1.0 + log_a * (1.0 + log_a * (2.0 / 3.0)))
    return jnp.where(log_a > -0.01, poly, 1.0 - a * a)


def _softplus_neg(lam):
    t = jnp.exp(-jnp.abs(lam))
    log1p = jnp.where(t < 1e-3, t * (1.0 - t * (0.5 - t * (1.0 / 3.0))), jnp.log(1.0 + t))
    return jnp.maximum(-lam, 0.0) + log1p, _sig(-lam)


def _lru_gates(u_ref, up_ref, rb, sm_ref, wa, wx):
    cur = u_ref[...].astype(F32)
    prev = jnp.where(rb > 0, up_ref[...].astype(F32), 0.0)
    ext = jnp.concatenate([prev, cur], axis=0)
    taps = [cur] + [pltpu.roll(ext, k, 0)[HALO:] for k in range(1, CONV_W)]
    uc = sm_ref[CONV_W:CONV_W + 1, :]
    for k in range(CONV_W):
        uc = uc + taps[k] * sm_ref[CONV_W - 1 - k:CONV_W - k, :]
    ucb = uc.astype(MXU)
    r = _sig(jnp.dot(ucb, wa, preferred_element_type=F32) + sm_ref[5:6, :])
    ig = _sig(jnp.dot(ucb, wx, preferred_element_type=F32) + sm_ref[6:7, :])
    sp, sgn = _softplus_neg(sm_ref[7:8, :])
    log_a = r * (-LRU_C * sp)
    a = jnp.exp(log_a)
    mult = jnp.sqrt(jnp.maximum(_one_minus_sq(log_a, a), 0.0))
    return taps, uc, ucb, r, ig, sp, sgn, a, mult


LANES = 128


def _seg_scan(a, b, out_ref, scr, state, reverse):
    a_s, b_s, h_s, p_s = scr
    tr, c = a.shape
    seg = tr // 8
    nl = c // LANES
    for l in range(nl):
        a_s[l] = a[:, l * LANES:(l + 1) * LANES]
        b_s[l] = b[:, l * LANES:(l + 1) * LANES]
    h = [jnp.zeros((8, LANES), F32)] * nl
    pp = [jnp.ones((8, LANES), F32)] * nl
    for i in (range(seg - 1, -1, -1) if reverse else range(seg)):
        rows = pl.ds(i, 8, stride=seg)
        for l in range(nl):
            av = a_s[l, rows, :]
            h[l] = av * h[l] + b_s[l, rows, :]
            pp[l] = av * pp[l]
            h_s[l, pl.ds(8 * i, 8), :] = h[l]
            p_s[l, pl.ds(8 * i, 8), :] = pp[l]
    leaving = []
    for l in range(nl):
        lanes = slice(l * LANES, (l + 1) * LANES)
        st = state[:, lanes]
        for sgm in (range(7, -1, -1) if reverse else range(8)):
            for t0 in range(0, seg, 8):
                rows = pl.ds(8 * t0 + sgm, 8, stride=8)
                out_ref[pl.ds(sgm * seg + t0, 8), lanes] = h_s[l, rows, :] + p_s[l, rows, :] * st
            st = h[l][sgm:sgm + 1, :] + pp[l][sgm:sgm + 1, :] * st
        leaving.append(st)
    return jnp.concatenate(leaving, axis=1)


def lru_fwd(uz, wa_g, wx_g, small, tr):
    _, s, e = uz.shape
    cb = e // LRU_HEADS
    nb = s // tr
    hb = tr // HALO

    def body(u_ref, up_ref, z_ref, wa_ref, wx_ref, sm_ref, o_ref, h_ref, s0, s1, s2, s3, carry_ref):
        rb = pl.program_id(1)
        wa = wa_ref[...].reshape(cb, cb)
        wx = wx_ref[...].reshape(cb, cb)
        _, uc, _, _, ig, _, _, a, mult = _lru_gates(u_ref, up_ref, rb, sm_ref, wa, wx)
        start = jnp.where(rb > 0, carry_ref[0:1, :], 0.0)
        last = _seg_scan(a, mult * ig * uc, h_ref, (s0, s1, s2, s3), start, False)
        carry_ref[...] = jnp.broadcast_to(last, carry_ref.shape)
        z = z_ref[...].astype(F32)
        o_ref[...] = (h_ref[...] * z * _sig(z)).astype(o_ref.dtype)

    wspec = pl.BlockSpec((4, None, cb // 4, cb), lambda h, r: (0, h, 0, 0))
    return pl.pallas_call(
        body, name="lru_fwd", grid=(LRU_HEADS, nb),
        in_specs=[
            pl.BlockSpec((None, tr, cb), lambda h, r: (0, r, h)),
            pl.BlockSpec((None, HALO, cb), lambda h, r: (0, jnp.maximum(r * hb - 1, 0), h)),
            pl.BlockSpec((None, tr, cb), lambda h, r: (1, r, h)),
            wspec, wspec,
            pl.BlockSpec((8, cb), lambda h, r: (0, h)),
        ],
        out_specs=[pl.BlockSpec((tr, cb), lambda h, r: (r, h)), pl.BlockSpec((tr, cb), lambda h, r: (r, h))],
        out_shape=[jax.ShapeDtypeStruct((s, e), MXU), jax.ShapeDtypeStruct((s, e), F32)],
        scratch_shapes=[pltpu.VMEM((cb // LANES, tr, LANES), F32)] * 4 + [pltpu.VMEM((8, cb), F32)],
        compiler_params=_cp(("parallel", "arbitrary")),
    )(uz, uz, uz, wa_g, wx_g, small)


def lru_bwd(uz, hst, dy2, wa_g, wx_g, small, tr, hosted=()):
    _, s, e = uz.shape
    cb = e // LRU_HEADS
    nb = s // tr
    hb = tr // HALO

    def body(u_ref, up_ref, z_ref, h_ref, hp_ref, dy_ref, wa_ref, wx_ref, sm_ref,
             duz_ref, gwa_ref, gwx_ref, dsm_ref, s0, s1, s2, s3, g_s, acc_a, acc_x, gcar, acar, dcar):
        step = pl.program_id(1)
        rb = nb - 1 - step
        wa = wa_ref[...].reshape(cb, cb)
        wx = wx_ref[...].reshape(cb, cb)
        taps, uc, ucb, r, ig, sp, sgn, a, mult = _lru_gates(u_ref, up_ref, rb, sm_ref, wa, wx)
        row = lax.broadcasted_iota(jnp.int32, a.shape, 0)
        z = z_ref[...].astype(F32)
        sg = _sig(z)
        dy2v = dy_ref[...].astype(F32)
        hv = h_ref[...]
        duz_ref[1] = (dy2v * hv * sg * (1.0 + z * (1.0 - sg))).astype(duz_ref.dtype)
        a_next = jnp.where(row == tr - 1, jnp.where(step > 0, acar[0:1, :], 0.0), pltpu.roll(a, tr - 1, 0))
        g_first = _seg_scan(a_next, dy2v * z * sg, g_s, (s0, s1, s2, s3), jnp.where(step > 0, gcar[0:1, :], 0.0), True)
        gcar[...] = jnp.broadcast_to(g_first, gcar.shape)
        acar[...] = jnp.broadcast_to(a[0:1, :], acar.shape)
        gv = g_s[...]
        h_before = jnp.where(rb > 0, hp_ref[HALO - 1:HALO, :], 0.0)
        h_prev = jnp.where(row == 0, h_before, pltpu.roll(hv, 1, 0))
        da = gv * h_prev
        gu = gv * uc
        dmult = gu * ig
        dig = gu * mult
        dlog_a = da * a - dmult * jnp.where(mult > 0.0, a * a / mult, 0.0)
        dra = dlog_a * (-LRU_C) * sp * r * (1.0 - r)
        dix = dig * ig * (1.0 - ig)
        dl = jnp.sum(dlog_a * r, axis=0, keepdims=True) * (LRU_C * sgn)
        dra_b, dix_b = dra.astype(MXU), dix.astype(MXU)
        duc = (gv * mult * ig + lax.dot_general(dra_b, wa, _DN["nt"], preferred_element_type=F32)
               + lax.dot_general(dix_b, wx, _DN["nt"], preferred_element_type=F32))
        gwa = lax.dot_general(ucb, dra_b, _DN["tn"], preferred_element_type=F32)
        gwx = lax.dot_general(ucb, dix_b, _DN["tn"], preferred_element_type=F32)
        ext = jnp.concatenate([duc, jnp.where(step > 0, dcar[...], 0.0)], axis=0)
        n = ext.shape[0]
        du = duc * sm_ref[CONV_W - 1:CONV_W, :]
        for k in range(1, CONV_W):
            du = du + pltpu.roll(ext, n - k, 0)[:tr] * sm_ref[CONV_W - 1 - k:CONV_W - k, :]
        duz_ref[0] = du.astype(duz_ref.dtype)
        dcar[...] = duc[:HALO]
        rows = [jnp.sum(duc * taps[CONV_W - 1 - k], axis=0, keepdims=True) for k in range(CONV_W)]
        rows += [jnp.sum(duc, axis=0, keepdims=True), jnp.sum(dra, axis=0, keepdims=True),
                 jnp.sum(dix, axis=0, keepdims=True), dl]

        @pl.when(step == 0)
        def _():
            acc_a[...] = gwa
            acc_x[...] = gwx
            for k, rv in enumerate(rows):
                dsm_ref[k:k + 1, :] = rv

        @pl.when(step > 0)
        def _():
            acc_a[...] += gwa
            acc_x[...] += gwx
            for k, rv in enumerate(rows):
                dsm_ref[k:k + 1, :] += rv

        @pl.when(step == nb - 1)
        def _():
            gwa_ref[...] = acc_a[...].reshape(4, cb // 4, cb).astype(gwa_ref.dtype)
            gwx_ref[...] = acc_x[...].reshape(4, cb // 4, cb).astype(gwx_ref.dtype)

    wspec = pl.BlockSpec((4, None, cb // 4, cb), lambda h, r: (0, h, 0, 0))
    blk = pl.BlockSpec((tr, cb), lambda h, r: (nb - 1 - r, h))
    return _hosting_call(
        body, "lru_bwd", (LRU_HEADS, nb),
        [
            pl.BlockSpec((None, tr, cb), lambda h, r: (0, nb - 1 - r, h)),
            pl.BlockSpec((None, HALO, cb), lambda h, r: (0, jnp.maximum((nb - 1 - r) * hb - 1, 0), h)),
            pl.BlockSpec((None, tr, cb), lambda h, r: (1, nb - 1 - r, h)),
            blk,
            pl.BlockSpec((HALO, cb), lambda h, r: (jnp.maximum((nb - 1 - r) * hb - 1, 0), h)),
            blk,
            wspec, wspec,
            pl.BlockSpec((8, cb), lambda h, r: (0, h)),
        ],
        [
            pl.BlockSpec((2, tr, cb), lambda h, r: (0, nb - 1 - r, h)),
            wspec, wspec,
            pl.BlockSpec((8, cb), lambda h, r: (0, h)),
        ],
        [
            jax.ShapeDtypeStruct((2, s, e), MXU),
            jax.ShapeDtypeStruct(wa_g.shape, WIRE),
            jax.ShapeDtypeStruct(wx_g.shape, WIRE),
            jax.ShapeDtypeStruct((8, e), F32),
        ],
        ([pltpu.VMEM((cb // LANES, tr, LANES), F32)] * 4 + [pltpu.VMEM((tr, cb), F32)]
         + [pltpu.VMEM((cb, cb), F32)] * 2 + [pltpu.VMEM((8, cb), F32)] * 2 + [pltpu.VMEM((HALO, cb), F32)]),
        [uz, uz, uz, hst, hst, dy2, wa_g, wx_g, small], hosted, ("parallel", "arbitrary"))


def _place():
    x, y, c = lax.axis_index("x"), lax.axis_index("y"), lax.axis_index("c")
    chips = [(1 - x, y), (x, 1 - y), (1 - x, 1 - y)]
    return x, y, c, chips


def _rcopy(src, dst, send_sems, recv_sems, k, to):
    return pltpu.make_async_remote_copy(src_ref=src, dst_ref=dst, send_sem=send_sems.at[k], recv_sem=recv_sems.at[k],
                                        device_id=to, device_id_type=MESHID)


def _stage_gather_ici(bufs):
    n = len(bufs)

    def quarter(ref, chip, c, q):
        rq = ref.shape[3] // 2
        return ref.at[2 * chip[0] + chip[1], :, c, pl.ds(q * rq, rq)]

    def copies(refs, ss, rs, off, a, sending):
        x, y, c, _ = _place()
        me, xn, yn, dg = (x, y), (1 - x, y), (x, 1 - y), (1 - x, 1 - y)
        plan = [(0, me, xn, 0), (1, me, xn, 1), (3, me, yn, 1), (2, me, yn, 0),
                (4, xn, yn, 0), (5, yn, xn, 1)]
        if not sending:
            plan = [(0, xn, xn, 0), (1, xn, xn, 1), (3, yn, yn, 1), (2, yn, yn, 0), (4, dg, yn, 0), (5, dg, xn, 1)]
        out = []
        for k, owner, to, q in plan:
            blk = quarter(refs[a], owner, c, q)
            out.append(_rcopy(blk, blk, ss, rs, off + 6 * a + k, (*to, c)))
        return out

    def start(ro, refs, ss, rs, off):
        for a in range(n):
            for cp in copies(refs, ss, rs, off, a, True)[:4]:
                cp.start()

    def mid(ro, refs, ss, rs, off):
        for a in range(n):
            got, out = copies(refs, ss, rs, off, a, False), copies(refs, ss, rs, off, a, True)
            got[0].wait_recv()
            out[4].start()
            got[2].wait_recv()
            out[5].start()

    def finish(ro, refs, ss, rs, off):
        for a in range(n):
            got = copies(refs, ss, rs, off, a, False)
            for k in (1, 3, 4, 5):
                got[k].wait_recv()
            for cp in copies(refs, ss, rs, off, a, True):
                cp.wait_send()

    return dict(ro=[], bufs=list(bufs), nsem=6 * n, start=start, mid=mid, finish=finish)


def _stage_gather_d2d(bufs):
    n = len(bufs)

    def copies(refs, ss, rs, off, sending):
        x, y, c, chips = _place()
        out = []
        for a in range(n):
            for jj, ch in enumerate(chips):
                blk = refs[a].at[2 * ch[0] + ch[1], :, c if sending else 1 - c]
                out.append(_rcopy(blk, blk, ss, rs, off + 3 * a + jj, (x, y, 1 - c)))
        return out

    def start(ro, refs, ss, rs, off):
        for cp in copies(refs, ss, rs, off, True):
            cp.start()

    def finish(ro, refs, ss, rs, off):
        for cp in copies(refs, ss, rs, off, False):
            cp.wait_recv()
        for cp in copies(refs, ss, rs, off, True):
            cp.wait_send()

    return dict(ro=[], bufs=list(bufs), nsem=3 * n, start=start, finish=finish)


def _stage_scatter_ici(parts, gots):
    n = len(parts)

    def copies(ro, refs, ss, rs, off):
        x, y, c, chips = _place()
        return [_rcopy(ro[a].at[2 * ch[0] + ch[1]], refs[a].at[jj], ss, rs, off + 3 * a + jj, (*ch, c))
                for a in range(n) for jj, ch in enumerate(chips)]

    def start(ro, refs, ss, rs, off):
        for cp in copies(ro, refs, ss, rs, off):
            cp.start()

    def finish(ro, refs, ss, rs, off):
        for cp in copies(ro, refs, ss, rs, off):
            cp.wait()

    return dict(ro=list(parts), bufs=list(gots), nsem=3 * n, start=start, finish=finish)


def _stage_send_half(grads, lands):
    n = len(grads)

    def copies(ro, refs, ss, rs, off):
        x, y, c, _ = _place()
        return [_rcopy(ro[a].at[:, :, 1 - c], refs[a], ss, rs, off + a, (x, y, 1 - c)) for a in range(n)]

    def start(ro, refs, ss, rs, off):
        for cp in copies(ro, refs, ss, rs, off):
            cp.start()

    def finish(ro, refs, ss, rs, off):
        for cp in copies(ro, refs, ss, rs, off):
            cp.wait()

    return dict(ro=list(grads), bufs=list(lands), nsem=n, start=start, finish=finish)


def cast_into_slab(w, k_idx):
    l, r, c = w.shape
    rh = r // 2
    lb, tr = _block_lr(l, rh, c)
    nbh = rh // tr

    def body(k_ref, w_ref, o_ref):
        o_ref[...] = w_ref[...].astype(o_ref.dtype)

    return pl.pallas_call(
        body, name="cast_into_slab",
        grid_spec=pltpu.PrefetchScalarGridSpec(
            num_scalar_prefetch=1, grid=(l // lb, 2, nbh),
            in_specs=[pl.BlockSpec((lb, tr, c), lambda i, h, b, k_ref: (i, h * nbh + b, 0))],
            out_specs=pl.BlockSpec((None, lb, None, tr, c), lambda i, h, b, k_ref: (k_ref[0], i, h, b, 0))),
        out_shape=jax.ShapeDtypeStruct((4, l, 2, rh, c), WIRE),
        compiler_params=_cp(("parallel", "parallel", "parallel")),
    )(k_idx, w)


def gather_weights(bufs):
    n = len(bufs)
    ici = [_stage_gather_ici([b]) for b in bufs]
    d2d = [_stage_gather_d2d([b]) for b in bufs]
    per = ici[0]["nsem"] + d2d[0]["nsem"]

    def body(*refs):
        outs = refs[n:2 * n]
        ss, rs = refs[2 * n:]
        for what in ("start", "mid"):
            for a in range(n):
                ici[a][what]([], [outs[a]], ss, rs, per * a)
        for a in range(n):
            ici[a]["finish"]([], [outs[a]], ss, rs, per * a)
            d2d[a]["start"]([], [outs[a]], ss, rs, per * a + ici[a]["nsem"])
        for a in range(n):
            d2d[a]["finish"]([], [outs[a]], ss, rs, per * a + ici[a]["nsem"])

    return pl.pallas_call(
        body, name="gather_weights",
        in_specs=[ANY] * n, out_specs=[ANY] * n,
        out_shape=[jax.ShapeDtypeStruct(a.shape, a.dtype) for a in bufs],
        scratch_shapes=[pltpu.SemaphoreType.DMA((per * n,)), pltpu.SemaphoreType.DMA((per * n,))],
        input_output_aliases={a: a for a in range(n)},
        compiler_params=pltpu.CompilerParams(has_side_effects=True),
    )(*bufs)


def all_gather_small(v, name):
    m_per, n = v.shape

    def body(x_ref, out_ref, send_sems, recv_sems, local_sem):
        x, y, c, chips = _place()
        me, sibling = (x, y, c), (x, y, 1 - c)

        def rows(px, py, pc):
            return out_ref.at[pl.ds((4 * px + 2 * py + pc) * m_per, m_per), :]

        def copy(k, block, to, src=None):
            return _rcopy(rows(*block) if src is None else src, rows(*block), send_sems, recv_sems, k, to)

        mine = pltpu.make_async_copy(x_ref, rows(*me), local_sem)
        mine.start()
        first = [copy(0, me, sibling, src=x_ref)]
        first += [copy(1 + jj, me, (*chip, c), src=x_ref) for jj, chip in enumerate(chips)]
        for cp in first:
            cp.start()
        passed = [copy(4 + jj, (*chip, c), sibling) for jj, chip in enumerate(chips)]
        for jj, chip in enumerate(chips):
            copy(1 + jj, (*chip, c), me).wait_recv()
            passed[jj].start()
        copy(0, sibling, me).wait_recv()
        for jj, chip in enumerate(chips):
            copy(4 + jj, (*chip, 1 - c), me).wait_recv()
        for cp in first + passed:
            cp.wait_send()
        mine.wait()

    return pl.pallas_call(
        body, name=name,
        out_shape=jax.ShapeDtypeStruct((8 * m_per, n), v.dtype),
        in_specs=[pl.BlockSpec(memory_space=pltpu.VMEM)],
        out_specs=pl.BlockSpec(memory_space=pltpu.VMEM),
        scratch_shapes=[pltpu.SemaphoreType.DMA((7,)), pltpu.SemaphoreType.DMA((7,)), pltpu.SemaphoreType.DMA],
        compiler_params=pltpu.CompilerParams(vmem_limit_bytes=VMEM_LIMIT),
    )(v)


def sum_devices(g):
    def body(g_ref, o_ref):
        acc = g_ref[0]
        for d in range(1, 8):
            acc = acc + g_ref[d]
        o_ref[...] = acc

    return pl.pallas_call(body, name="sum_devices", out_shape=jax.ShapeDtypeStruct(g.shape[1:], g.dtype),
                          compiler_params=pltpu.CompilerParams(vmem_limit_bytes=VMEM_LIMIT))(g)


def send_other_half(grads):
    n = len(grads)

    def body(*refs):
        ins, outs = refs[:n], refs[n:2 * n]
        send_sems, recv_sems = refs[2 * n:]
        x, y, c, _ = _place()
        sib = (x, y, 1 - c)
        cps = [_rcopy(ins[a].at[:, :, 1 - c], outs[a], send_sems, recv_sems, a, sib) for a in range(n)]
        for cp in cps:
            cp.start()
        for cp in cps:
            cp.wait()

    return pl.pallas_call(
        body, name="send_other_half", in_specs=[ANY] * n, out_specs=[ANY] * n,
        out_shape=[jax.ShapeDtypeStruct(g.shape[:2] + g.shape[3:], g.dtype) for g in grads],
        scratch_shapes=[pltpu.SemaphoreType.DMA((n,)), pltpu.SemaphoreType.DMA((n,))],
        compiler_params=pltpu.CompilerParams(has_side_effects=True),
    )(*grads)


def scatter_to_chips(parts):
    n = len(parts)

    def body(*refs):
        ins, outs = refs[:n], refs[n:2 * n]
        send_sems, recv_sems = refs[2 * n:]
        x, y, c, chips = _place()
        cps = []
        for a in range(n):
            for jj, ch in enumerate(chips):
                cps.append(_rcopy(ins[a].at[2 * ch[0] + ch[1]], outs[a].at[jj], send_sems, recv_sems, 3 * a + jj, (*ch, c)))
        for cp in cps:
            cp.start()
        for cp in cps:
            cp.wait()

    return pl.pallas_call(
        body, name="scatter_to_chips", in_specs=[ANY] * n, out_specs=[ANY] * n,
        out_shape=[jax.ShapeDtypeStruct((3,) + p.shape[1:], p.dtype) for p in parts],
        scratch_shapes=[pltpu.SemaphoreType.DMA((3 * n,)), pltpu.SemaphoreType.DMA((3 * n,))],
        compiler_params=pltpu.CompilerParams(has_side_effects=True),
    )(*parts)


def share_halves(bufs, spans):
    n = len(bufs)

    def body(*refs):
        outs = refs[n:2 * n]
        send_sems, recv_sems = refs[2 * n:]
        x, y, c, _ = _place()
        sib = (x, y, 1 - c)

        def blk(a, half):
            return outs[a].at[pl.ds(spans[a][0], spans[a][1]), half]

        cps = [_rcopy(blk(a, c), blk(a, c), send_sems, recv_sems, a, sib) for a in range(n)]
        for cp in cps:
            cp.start()
        for a in range(n):
            _rcopy(blk(a, 1 - c), blk(a, 1 - c), send_sems, recv_sems, a, sib).wait_recv()
        for cp in cps:
            cp.wait_send()

    return pl.pallas_call(
        body, name="share_halves", in_specs=[ANY] * n, out_specs=[ANY] * n,
        out_shape=[jax.ShapeDtypeStruct(b.shape, b.dtype) for b in bufs],
        scratch_shapes=[pltpu.SemaphoreType.DMA((n,)), pltpu.SemaphoreType.DMA((n,))],
        input_output_aliases={a: a for a in range(n)},
        compiler_params=pltpu.CompilerParams(has_side_effects=True),
    )(*bufs)


def _block_rows(r, c, itemsize, budget=1 << 20):
    tr = r
    while tr * c * itemsize > budget and tr % 16 == 0:
        tr //= 2
    return tr


def _block_lr(l, r, c, budget=4 << 20):
    tr = _block_rows(r, c, 4, budget)
    lb = 1
    if tr == r:
        while l % (2 * lb) == 0 and 2 * lb * r * c * 4 <= budget:
            lb *= 2
    return lb, tr


def add_halves(g, got, c_idx):
    k4, l, _, rh, cc = g.shape
    lb, tr = _block_lr(l, rh, cc)

    def body(c_ref, g_ref, r_ref, o_ref):
        o_ref[...] = (g_ref[...].astype(F32) + r_ref[...].astype(F32)).astype(o_ref.dtype)

    return pl.pallas_call(
        body, name="add_halves",
        grid_spec=pltpu.PrefetchScalarGridSpec(
            num_scalar_prefetch=1, grid=(k4, l // lb, rh // tr),
            in_specs=[pl.BlockSpec((None, lb, None, tr, cc), lambda k, i, b, c_ref: (k, i, c_ref[0], b, 0)),
                      pl.BlockSpec((None, lb, tr, cc), lambda k, i, b, c_ref: (k, i, b, 0))],
            out_specs=pl.BlockSpec((None, lb, tr, cc), lambda k, i, b, c_ref: (k, i, b, 0))),
        out_shape=jax.ShapeDtypeStruct(got.shape, WIRE),
        compiler_params=_cp(("parallel", "parallel", "parallel")),
    )(c_idx, g, got)


def sum_chips(own, got, kc_idx, full, first):
    _, l, rh, cc = own.shape
    lb, tr = _block_lr(l, rh, cc, 2 << 20)
    assert first % lb == 0

    def body(k_ref, o_ref, r_ref, _full, s_ref):
        s_ref[...] = ((o_ref[...].astype(F32) + r_ref[0].astype(F32)) + r_ref[1].astype(F32)) + r_ref[2].astype(F32)

    return pl.pallas_call(
        body, name="sum_chips",
        grid_spec=pltpu.PrefetchScalarGridSpec(
            num_scalar_prefetch=1, grid=(l // lb, rh // tr),
            in_specs=[pl.BlockSpec((None, lb, tr, cc), lambda i, b, k_ref: (k_ref[0], i, b, 0)),
                      pl.BlockSpec((3, lb, tr, cc), lambda i, b, k_ref: (0, i, b, 0)),
                      ANY],
            out_specs=pl.BlockSpec((lb, None, tr, cc), lambda i, b, k_ref: (first // lb + i, k_ref[1], b, 0))),
        out_shape=jax.ShapeDtypeStruct(full.shape, F32),
        input_output_aliases={3: 0},
        compiler_params=_cp(("parallel", "parallel")),
    )(kc_idx, own, got, full)


def _adam_math(w, g, m, v):
    m = ADAM_B1 * m + (1.0 - ADAM_B1) * g
    v = ADAM_B2 * v + (1.0 - ADAM_B2) * (g * g)
    m_hat = m / (1.0 - ADAM_B1 ** ADAM_STEP)
    v_hat = v / (1.0 - ADAM_B2 ** ADAM_STEP)
    delta = -ADAM_LR * (m_hat / (jnp.sqrt(v_hat) + ADAM_EPS) + ADAM_WD * w)
    return delta, m, v


def adamw(w, g, g_first, m, v, lo, hi, prev=None, hosted=()):
    l, r, c = w.shape
    tr = _block_rows(r, c, 4, 2 << 20)
    prev = list(prev or [])

    def body(w_ref, g_ref, m_ref, v_ref, *rest):
        go_ref, d_ref, mo_ref, vo_ref = rest[len(prev):]
        gv = g_ref[...]
        go_ref[...] = gv
        d_ref[...], mo_ref[...], vo_ref[...] = _adam_math(w_ref[...], gv, m_ref[...], v_ref[...])

    spec = pl.BlockSpec((None, tr, c), lambda i, b: (lo + i, b, 0))
    gspec = pl.BlockSpec((None, tr, c), lambda i, b: (g_first + i, b, 0))
    return _hosting_call(
        body, "adamw", (hi - lo, r // tr), [spec, gspec, spec, spec] + [ANY] * len(prev), [spec] * 4,
        [jax.ShapeDtypeStruct(w.shape, F32)] * 4, [], [w, g, m, v] + prev, hosted, ("parallel", "parallel"),
        aliases={4 + t: t for t in range(len(prev))})


def adamw_small(w, g, m, v):
    def body(w_ref, g_ref, m_ref, v_ref, d_ref, mo_ref, vo_ref):
        d_ref[...], mo_ref[...], vo_ref[...] = _adam_math(w_ref[...], g_ref[...], m_ref[...], v_ref[...])

    return pl.pallas_call(body, name="adamw_small", out_shape=[jax.ShapeDtypeStruct(w.shape, F32)] * 3)(w, g, m, v)


def kernel(x, p, w_in, w_out, g_pre, g_post, pool_w, pool_b, pool_scale, conv_w, conv_b, lru_wa, lru_ba, lru_wx, lru_bx, lru_L, w_ple, w_ple_gate, g_ple_in, g_ple_out, loss_target, m_w_in, m_w_out, m_g_pre, m_g_post, m_pool_w, m_pool_b, m_pool_scale, m_conv_w, m_conv_b, m_lru_wa, m_lru_ba, m_lru_wx, m_lru_bx, m_lru_L, m_w_ple, m_w_ple_gate, m_g_ple_in, m_g_ple_out, v_w_in, v_w_out, v_g_pre, v_g_post, v_pool_w, v_pool_b, v_pool_scale, v_conv_w, v_conv_b, v_lru_wa, v_lru_ba, v_lru_wx, v_lru_bx, v_lru_L, v_w_ple, v_w_ple_gate, v_g_ple_in, v_g_ple_out):
    depth = w_in.shape[0]
    _, s, d = x.shape
    e = 2 * d
    kp = p.shape[-1]
    nmix = pool_w.shape[0]
    ngrp = pool_w.shape[1]
    cg = e // ngrp
    cb = e // LRU_HEADS
    xi, yi, ci = lax.axis_index("x"), lax.axis_index("y"), lax.axis_index("c")
    me = 2 * xi + yi
    c_idx = jnp.reshape(ci, (1,)).astype(jnp.int32)
    k_idx = jnp.reshape(me, (1,)).astype(jnp.int32)
    tr_row = _tile(s, 256)
    tr_mix = _tile(s, 512)
    tm = _tile(s, 1024)
    tm2 = _tile(s, 2048)

    def halves(a):
        return a.reshape(a.shape[0], 2, a.shape[1] // 2, a.shape[2])

    big = {
        "w_in": w_in, "w_out": w_out, "gate": w_ple_gate, "ple": w_ple,
        "pool": pool_w.reshape(nmix * ngrp, cg // 4, cg),
        "wa": lru_wa.reshape(nmix * LRU_HEADS, cb // 4, cb), "wx": lru_wx.reshape(nmix * LRU_HEADS, cb // 4, cb),
    }
    names = list(big)

    def layer_shards(i):
        sh = {"w_in": w_in[i][None], "w_out": w_out[i][None], "gate": w_ple_gate[i][None], "ple": w_ple[i][None]}
        if i % 2 == 0:
            sh["pool"] = pool_w[i // 2]
        else:
            sh["wa"], sh["wx"] = lru_wa[i // 2], lru_wx[i // 2]
        return sh

    def mixer_names(i):
        return ["pool"] if i % 2 == 0 else ["wa", "wx"]

    wbuf = [{n: cast_into_slab(w, k_idx) for n, w in layer_shards(i).items()} for i in range(depth)]
    first = list(wbuf[0])
    wbuf[0] = dict(zip(first, gather_weights([wbuf[0][n] for n in first])))

    def full_w(i, n):
        b = wbuf[i][n]
        return b.reshape(b.shape[0], b.shape[1], 2 * b.shape[3], b.shape[4])

    def run_mm(stages, *args):
        if not stages:
            return _matmul(*args)
        out, new = _matmul(*args, hosted=[mk([wbuf[l][n] for n in nms]) for mk, l, nms in stages])
        slots = [(l, n) for _, l, nms in stages for n in nms]
        for (l, n), b in zip(slots, new):
            wbuf[l][n] = b
        return out

    ec = e // 4
    small_loc = jnp.concatenate([conv_w, conv_b[:, None], lru_ba[:, None], lru_bx[:, None], lru_L[:, None]], axis=1)
    sm_all = all_gather_small(small_loc.reshape(nmix * 8, ec), "gather_small").reshape(4, 2, nmix, 8, ec)
    lru_small = jnp.transpose(sm_all[:, 0], (1, 2, 0, 3)).reshape(nmix, 8, e)

    xs = x[0]
    saved = []
    for i in range(depth):
        j = i // 2
        h = rms_fwd(xs, g_pre[i][None], tr_row) if i == 0 else h_next
        nj = (2 * e) // 1024 if (2 * e) % 1024 == 0 else 4
        tn = (2 * e) // nj
        per = e // tn
        perk = (e // 2) // tn
        nxt = i + 1 if i + 1 < depth else None
        stages = [(_stage_gather_d2d, i, ["gate", "ple"])] if i > 0 else []
        if nxt is not None:
            stages.append((_stage_gather_ici, nxt, ["w_in"]))
        uz = run_mm(
            stages, "mm_in", "nn", (s // tm2, nj, 1), h, pl.BlockSpec((tm2, d), lambda a, b, k: (a, 0)),
            full_w(i, "w_in"), pl.BlockSpec((None, None, d, tn), lambda a, b, k, perk=perk: (b // perk, 0, 0, b % perk)),
            jax.ShapeDtypeStruct((2, s, e), MXU), pl.BlockSpec((None, tm2, tn), lambda a, b, k, per=per: (b // per, a, b % per)),
            (8, 128))
        if i % 2 == 0:
            y2 = pool_fwd(uz, full_w(i, "pool"), pool_b[j][None], pool_scale[j][None], tr_mix)
            hst = None
        else:
            y2, hst = lru_fwd(uz, full_w(i, "wa"), full_w(i, "wx"), lru_small[j], tr_mix)
        tn_o = _tile(d, 1024)
        stages = [(_stage_gather_ici, nxt, ["w_out"] + mixer_names(nxt))] if nxt is not None else []
        o = run_mm(
            stages, "mm_out", "nn", (s // tm, d // tn_o, 1), y2, pl.BlockSpec((tm, e), lambda a, b, k: (a, 0)),
            full_w(i, "w_out"), pl.BlockSpec((4, None, e // 4, tn_o), lambda a, b, k: (0, 0, 0, b)),
            jax.ShapeDtypeStruct((s, d), MXU), pl.BlockSpec((tm, tn_o), lambda a, b, k: (a, b)), (8, 128))
        x1, hn = res_rms_fwd(xs, o, g_post[i][None], g_ple_in[i][None], tr_row)
        stages = []
        if nxt is not None:
            stages = [(_stage_gather_ici, nxt, ["gate", "ple"]), (_stage_gather_d2d, nxt, ["w_in", "w_out"] + mixer_names(nxt))]
        gpre = run_mm(
            stages, "mm_gate", "nn", (s // tm2, d // tn_o, 1), hn, pl.BlockSpec((tm2, d), lambda a, b, k: (a, 0)),
            full_w(i, "gate"), pl.BlockSpec((4, None, d // 4, tn_o), lambda a, b, k: (0, 0, 0, b)),
            jax.ShapeDtypeStruct((s, d), MXU), pl.BlockSpec((tm2, tn_o), lambda a, b, k: (a, b)), (8, 128))
        pe = p[i, 0]
        ev = _matmul(
            "mm_ple", "nn", (s // tm, 4, 1), pe, pl.BlockSpec((tm, kp), lambda a, b, k: (a, 0)),
            full_w(i, "ple"), pl.BlockSpec((None, None, kp, d // 4), lambda a, b, k: (b, 0, 0, 0)),
            jax.ShapeDtypeStruct((s, d), MXU), pl.BlockSpec((tm, d // 4), lambda a, b, k: (a, b)), (8, 128))
        if nxt is not None:
            x2, h_next = ple_fwd(x1, gpre, ev, g_ple_out[i][None], g_pre[nxt][None], tr_row)
        else:
            x2 = ple_fwd_last(x1, gpre, ev, g_ple_out[i][None], tr_row)
        saved.append((xs, h, uz, y2, hst, o, x1, hn, gpre, ev))
        xs = x2

    dx, sq = loss_bwd(xs, loss_target[0], tr_row)
    d_gpre, d_gpost, d_gin, d_gout = [None] * depth, [None] * depth, [None] * depth, [None] * depth
    d_pool_b, d_pool_sc, d_lru_small = [None] * nmix, [None] * nmix, [None] * nmix
    ts = _tile(s, 1024)
    kc_idx = jnp.stack([me, ci]).astype(jnp.int32)
    full = {n: lax.empty(halves(big[n]).shape, F32) for n in names}
    w_sum = w_got = None

    def first_row(i, n):
        return i if n in ("w_in", "w_out", "gate", "ple") else (i // 2) * (ngrp if n == "pool" else LRU_HEADS)

    for i in reversed(range(depth)):
        j = i // 2
        prev = i + 1 if i + 1 < depth else None
        x0, h, uz, y2, hst, o, x1, hn, gpre, ev = saved[i]
        pe = p[i, 0]
        gl = {}
        de, dgp, d_gout[i] = ple_bwd(dx, gpre, ev, g_ple_out[i][None], tr_row)
        gl["ple"] = _matmul(
            "mm_dple", "tn", (1, 4, s // ts), pe, pl.BlockSpec((ts, kp), lambda a, b, k: (k, 0)),
            de, pl.BlockSpec((ts, d // 4), lambda a, b, k: (k, b)),
            jax.ShapeDtypeStruct((4, 1, kp, d // 4), WIRE), pl.BlockSpec((None, None, kp, d // 4), lambda a, b, k: (b, 0, 0, 0)),
            (kp, d // 4))
        tn_o = _tile(d, 1024)
        gl["gate"] = _matmul(
            "mm_dgate", "tn", (4, d // tn_o, 1), hn, pl.BlockSpec((s, d // 4), lambda a, b, k: (0, a)),
            dgp, pl.BlockSpec((s, tn_o), lambda a, b, k: (0, b)),
            jax.ShapeDtypeStruct((4, 1, d // 4, d), WIRE), pl.BlockSpec((None, None, d // 4, tn_o), lambda a, b, k: (a, 0, 0, b)),
            (8, 128))
        dhn = _matmul(
            "mm_dhn", "nt", (s // tm2, 4, 1), dgp, pl.BlockSpec((tm2, d), lambda a, b, k: (a, 0)),
            full_w(i, "gate"), pl.BlockSpec((None, None, d // 4, d), lambda a, b, k: (b, 0, 0, 0)),
            jax.ShapeDtypeStruct((s, d), MXU), pl.BlockSpec((tm2, d // 4), lambda a, b, k: (a, b)), (8, 128))
        dx1, do, d_gin[i], d_gpost[i] = post_bwd(dx, dhn, x1, g_ple_in[i][None], o, g_post[i][None], tr_row)
        gl["w_out"] = _matmul(
            "mm_dwout", "tn", (4, d // tn_o, 1), y2, pl.BlockSpec((s, e // 4), lambda a, b, k: (0, a)),
            do, pl.BlockSpec((s, tn_o), lambda a, b, k: (0, b)),
            jax.ShapeDtypeStruct((4, 1, e // 4, d), WIRE), pl.BlockSpec((None, None, e // 4, tn_o), lambda a, b, k: (a, 0, 0, b)),
            (8, 128))
        dy2 = _matmul(
            "mm_dy2", "nt", (s // tm2, 4, 1), do, pl.BlockSpec((tm2, d), lambda a, b, k: (a, 0)),
            full_w(i, "w_out"), pl.BlockSpec((None, None, e // 4, d), lambda a, b, k: (b, 0, 0, 0)),
            jax.ShapeDtypeStruct((s, e), MXU), pl.BlockSpec((tm2, e // 4), lambda a, b, k: (a, b)), (8, 128))
        riding = [_stage_scatter_ici([w_sum], [w_got])] if prev is not None else []
        if i % 2 == 0:
            (duz, gl["pool"], d_pool_b[j], d_pool_sc[j]), passed = pool_bwd(
                uz, dy2, full_w(i, "pool"), pool_b[j][None], pool_scale[j][None], tr_mix, hosted=riding)
        else:
            (duz, gl["wa"], gl["wx"], d_lru_small[j]), passed = lru_bwd(
                uz, hst, dy2, full_w(i, "wa"), full_w(i, "wx"), lru_small[j], tr_mix, hosted=riding)
        if prev is not None:
            full["w_in"] = sum_chips(w_sum, passed[0], kc_idx, full["w_in"], first_row(prev, "w_in"))

        def in_halves(g):
            return g.reshape(4, g.shape[1], 2, g.shape[2] // 2, g.shape[3])

        early = list(gl)
        eparts = [in_halves(gl[n]) for n in early]
        tmi = _tile(d, 1024)
        tni = _tile(e // 2, 1024)
        nslab = (e // 2) // tni
        gl["w_in"], from_sib = _matmul(
            "mm_dwin", "tn", (d // tmi, 4 * nslab, 1), h, pl.BlockSpec((s, tmi), lambda a, b, k: (0, a)),
            duz, pl.BlockSpec((None, s, tni), lambda a, b, k, nslab=nslab: (b // (2 * nslab), 0, b % (2 * nslab))),
            jax.ShapeDtypeStruct((4, 1, d, e // 2), WIRE),
            pl.BlockSpec((None, None, tmi, tni), lambda a, b, k, nslab=nslab: (b // nslab, 0, a, b % nslab)),
            (8, 128), hosted=[_stage_send_half(eparts, [lax.empty(g.shape[:2] + g.shape[3:], WIRE) for g in eparts])])
        esums = [add_halves(g, r, c_idx) for g, r in zip(eparts, from_sib)]
        wpart = in_halves(gl["w_in"])
        tnd = _tile(d, 1024)
        dh, passed = _matmul(
            "mm_dh", "nt", (s // tm, d // tnd, 4), duz, pl.BlockSpec((None, tm, e // 2), lambda a, b, k: (k // 2, a, k % 2)),
            full_w(i, "w_in"), pl.BlockSpec((None, None, tnd, e // 2), lambda a, b, k: (k, 0, b, 0)),
            jax.ShapeDtypeStruct((s, d), MXU), pl.BlockSpec((tm, tnd), lambda a, b, k: (a, b)), (tm, tnd),
            hosted=[_stage_scatter_ici(esums, [lax.empty((3,) + q.shape[1:], WIRE) for q in esums]),
                    _stage_send_half([wpart], [lax.empty(wpart.shape[:2] + wpart.shape[3:], WIRE)])])
        for n, q, got in zip(early, esums, passed[:len(early)]):
            full[n] = sum_chips(q, got, kc_idx, full[n], first_row(i, n))
        w_sum = add_halves(wpart, passed[len(early)], c_idx)
        w_got = lax.empty((3,) + w_sum.shape[1:], WIRE)
        dx, d_gpre[i] = rms_bwd_res(dx1, dh, x0, g_pre[i][None], tr_row)
    grad_x = dx[None]

    full["w_in"] = sum_chips(w_sum, scatter_to_chips([w_sum])[0], kc_idx, full["w_in"], 0)
    shared = share_halves([full[n] for n in names], [(0, full[n].shape[0]) for n in names])
    grads = {n: f.reshape(big[n].shape) for n, f in zip(names, shared)}

    def rows_e(a):
        return jnp.stack(a).reshape(-1, e) if isinstance(a, list) else a.reshape(-1, e)

    pack = [rows_e([g[0] for g in d_gpre]), rows_e([g[0] for g in d_gpost]), rows_e([g[0] for g in d_gin]), rows_e([g[0] for g in d_gout]),
            jnp.concatenate(d_pool_b, axis=0), jnp.concatenate(d_pool_sc, axis=0), jnp.concatenate(d_lru_small, axis=0),
            jnp.pad(sq, ((0, 0), (0, e - d)))]
    sizes = [a.shape[0] for a in pack]
    packed = jnp.concatenate(pack, axis=0)
    nrow = packed.shape[0]
    nrow_p = -(-nrow // 8) * 8
    packed = jnp.pad(packed, ((0, nrow_p - nrow), (0, 0)))
    total = sum_devices(all_gather_small(packed, "gather_grads").reshape(8, nrow_p, e))
    parts, off = [], 0
    for n_ in sizes:
        parts.append(total[off:off + n_])
        off += n_
    t_gpre, t_gpost, t_gin, t_gout, t_pb, t_psc, t_lru, t_sq = parts
    loss = 0.5 * jnp.sum(t_sq) / d
    t_lru = t_lru.reshape(nmix, 8, e)
    t_lru_loc = lax.dynamic_slice_in_dim(t_lru, me * ec, ec, axis=2)

    def big_update(name, w, m, v):
        shp = big[name].shape
        res4, _ = adamw(w.reshape(shp), grads[name], 0, m.reshape(shp), v.reshape(shp), 0, shp[0])
        return [a.reshape(w.shape) for a in res4]

    def small_update(w, g, m, v):
        shp = w.shape
        w2 = w.reshape(-1, shp[-1])
        dl, nm, nv = adamw_small(w2, g.reshape(w2.shape), m.reshape(w2.shape), v.reshape(w2.shape))
        return [g.reshape(shp), dl.reshape(shp), nm.reshape(shp), nv.reshape(shp)]

    res = {
        "w_in": big_update("w_in", w_in, m_w_in, v_w_in),
        "w_out": big_update("w_out", w_out, m_w_out, v_w_out),
        "g_pre": small_update(g_pre, t_gpre.reshape(depth, d), m_g_pre, v_g_pre),
        "g_post": small_update(g_post, t_gpost.reshape(depth, d), m_g_post, v_g_post),
        "pool_w": big_update("pool", pool_w, m_pool_w, v_pool_w),
        "pool_b": small_update(pool_b, t_pb, m_pool_b, v_pool_b),
        "pool_scale": small_update(pool_scale, t_psc, m_pool_scale, v_pool_scale),
        "conv_w": small_update(conv_w, t_lru_loc[:, :CONV_W], m_conv_w, v_conv_w),
        "conv_b": small_update(conv_b, t_lru_loc[:, 4], m_conv_b, v_conv_b),
        "lru_wa": big_update("wa", lru_wa, m_lru_wa, v_lru_wa),
        "lru_ba": small_update(lru_ba, t_lru_loc[:, 5], m_lru_ba, v_lru_ba),
        "lru_wx": big_update("wx", lru_wx, m_lru_wx, v_lru_wx),
        "lru_bx": small_update(lru_bx, t_lru_loc[:, 6], m_lru_bx, v_lru_bx),
        "lru_L": small_update(lru_L, t_lru_loc[:, 7], m_lru_L, v_lru_L),
        "w_ple": big_update("ple", w_ple, m_w_ple, v_w_ple),
        "w_ple_gate": big_update("gate", w_ple_gate, m_w_ple_gate, v_w_ple_gate),
        "g_ple_in": small_update(g_ple_in, t_gin.reshape(depth, d), m_g_ple_in, v_g_ple_in),
        "g_ple_out": small_update(g_ple_out, t_gout.reshape(depth, d), m_g_ple_out, v_g_ple_out),
    }
    order = ["w_in", "w_out", "g_pre", "g_post", "pool_w", "pool_b", "pool_scale", "conv_w", "conv_b", "lru_wa", "lru_ba",
             "lru_wx", "lru_bx", "lru_L", "w_ple", "w_ple_gate", "g_ple_in", "g_ple_out"]
    out = [loss, grad_x]
    for slot in range(4):
        out += [res[n][slot] for n in order]
    return tuple(out)
```

```python
import functools

import jax
import jax.numpy as jnp
from jax import lax
from jax.experimental import pallas as pl
from jax.experimental.pallas import tpu as pltpu

F32 = jnp.float32
MXU = jnp.bfloat16
WIRE = jnp.bfloat16
VMEM_LIMIT = 56 * 1024 * 1024
RMS_EPS = 1e-6
LRU_C = 8.0
POOL_WINDOWS = (2, 4, 8, 16)
MAXW = 16
CONV_W = 4
LRU_HEADS = 16
ADAM_LR, ADAM_B1, ADAM_B2, ADAM_EPS, ADAM_WD, ADAM_STEP = 0.001, 0.9, 0.999, 1e-08, 0.01, 10
MESHID = pl.DeviceIdType.MESH
ANY = pl.BlockSpec(memory_space=pl.ANY)


def _cp(sem=None):
    return pltpu.CompilerParams(dimension_semantics=sem, vmem_limit_bytes=VMEM_LIMIT)


def _sig(v):
    return 0.5 * jnp.tanh(0.5 * v) + 0.5


def _tile(n, pref):
    return pref if n % pref == 0 else n


_DN = {"nn": (((1,), (0,)), ((), ())), "nt": (((1,), (1,)), ((), ())), "tn": (((0,), (0,)), ((), ()))}


def _hosting_call(body, name, grid, in_specs, out_specs, out_shape, scratch_shapes, args, hosted, sem, aliases=None):
    n_in, n_out = len(args), len(out_shape)
    aliases = dict(aliases or {})
    if not hosted:
        res = pl.pallas_call(body, name=name, grid=grid, in_specs=in_specs, out_specs=out_specs, out_shape=out_shape,
                             scratch_shapes=scratch_shapes, input_output_aliases=aliases, compiler_params=_cp(sem))(*args)
        return list(res), []
    ro = [r for st in hosted for r in st["ro"]]
    bufs = [r for st in hosted for r in st["bufs"]]
    nro, nbuf = len(ro), len(bufs)
    nsem = sum(st["nsem"] for st in hosted)
    total = 1
    for g in grid:
        total *= g

    def wrapped(*refs):
        ins, ro_refs = refs[:n_in], refs[n_in:n_in + nro]
        outs = refs[n_in + nro + nbuf:n_in + nro + nbuf + n_out]
        buf_refs = refs[n_in + nro + nbuf + n_out:n_in + nro + 2 * nbuf + n_out]
        scr, ss, rs = refs[n_in + nro + 2 * nbuf + n_out:-2], refs[-2], refs[-1]
        step = 0
        for t, g in enumerate(grid):
            step = step * g + pl.program_id(t)

        def run_stages(what):
            r0 = b0 = s0 = 0
            for st in hosted:
                if what in st:
                    st[what](ro_refs[r0:r0 + len(st["ro"])], buf_refs[b0:b0 + len(st["bufs"])], ss, rs, s0)
                r0, b0, s0 = r0 + len(st["ro"]), b0 + len(st["bufs"]), s0 + st["nsem"]

        @pl.when(step == 0)
        def _():
            run_stages("start")

        if any("mid" in st for st in hosted):
            @pl.when(step == total // 2)
            def _():
                run_stages("mid")

        body(*ins, *outs, *scr)

        @pl.when(step == total - 1)
        def _():
            run_stages("finish")

    res = pl.pallas_call(
        wrapped, name=name, grid=grid,
        in_specs=list(in_specs) + [ANY] * (nro + nbuf), out_specs=list(out_specs) + [ANY] * nbuf,
        out_shape=list(out_shape) + [jax.ShapeDtypeStruct(x.shape, x.dtype) for x in bufs],
        scratch_shapes=list(scratch_shapes) + [pltpu.SemaphoreType.DMA((nsem,)), pltpu.SemaphoreType.DMA((nsem,))],
        input_output_aliases={**aliases, **{n_in + nro + t: n_out + t for t in range(nbuf)}},
        compiler_params=_cp(("arbitrary",) * len(grid)),
    )(*args, *ro, *bufs)
    return list(res[:n_out]), list(res[n_out:])


def _matmul(name, mode, grid, a, a_spec, b, b_spec, out_shape, out_spec, acc_shape, hosted=()):
    nk = grid[2]

    def body(a_ref, b_ref, o_ref, acc_ref):
        kk = pl.program_id(2)
        bv = b_ref[...]
        if bv.ndim == 3:
            bv = bv.reshape(bv.shape[0] * bv.shape[1], bv.shape[2])
        prod = lax.dot_general(a_ref[...].astype(MXU), bv.astype(MXU), _DN[mode], preferred_element_type=F32)
        if nk == 1:
            o_ref[...] = prod.astype(o_ref.dtype)
        else:
            @pl.when(kk == 0)
            def _():
                acc_ref[...] = prod

            @pl.when(kk > 0)
            def _():
                acc_ref[...] += prod

            @pl.when(kk == nk - 1)
            def _():
                o_ref[...] = acc_ref[...].astype(o_ref.dtype)

    outs, passed = _hosting_call(body, name, grid, [a_spec, b_spec], [out_spec], [out_shape], [pltpu.VMEM(acc_shape, F32)],
                                 [a, b], hosted, ("parallel", "parallel", "arbitrary"))
    return (outs[0], passed) if hosted else outs[0]


def _rows_call(name, body, ins, in_rows, outs, out_rows, n_rows, tr):
    def spec(shape, tiled):
        if tiled:
            return pl.BlockSpec((tr, shape[1]), lambda i: (i, 0))
        return pl.BlockSpec(shape, lambda i: (0, 0))

    return pl.pallas_call(
        body, name=name, grid=(n_rows // tr,),
        in_specs=[spec(a.shape, t) for a, t in zip(ins, in_rows)],
        out_specs=[spec(o.shape, t) for o, t in zip(outs, out_rows)],
        out_shape=outs, compiler_params=_cp(("arbitrary",)),
    )(*ins)


def _rstd(v):
    return lax.rsqrt(jnp.mean(v * v, axis=-1, keepdims=True) + RMS_EPS)


def _norm_bwd(v, g, dy):
    r = _rstd(v)
    n = v * r
    dn = dy * g
    dv = r * (dn - n * jnp.mean(dn * n, axis=-1, keepdims=True))
    return dv, jnp.sum(dy * n, axis=0, keepdims=True)


def _acc_rows(ref, val):
    @pl.when(pl.program_id(0) == 0)
    def _():
        ref[...] = val

    @pl.when(pl.program_id(0) > 0)
    def _():
        ref[...] += val


def rms_fwd(x, g, tr):
    def body(x_ref, g_ref, o_ref):
        v = x_ref[...]
        o_ref[...] = (v * _rstd(v) * g_ref[...]).astype(o_ref.dtype)

    return _rows_call("rms_fwd", body, [x, g], [True, False], [jax.ShapeDtypeStruct(x.shape, MXU)], [True], x.shape[0], tr)[0]


def res_rms_fwd(x, o, g, g_next, tr):
    def body(x_ref, o_ref, g_ref, gn_ref, y_ref, h_ref):
        v = o_ref[...].astype(F32)
        y = x_ref[...] + v * _rstd(v) * g_ref[...]
        y_ref[...] = y
        h_ref[...] = (y * _rstd(y) * gn_ref[...]).astype(h_ref.dtype)

    return _rows_call("res_rms_fwd", body, [x, o, g, g_next], [True, True, False, False],
                      [jax.ShapeDtypeStruct(x.shape, F32), jax.ShapeDtypeStruct(x.shape, MXU)], [True, True], x.shape[0], tr)


def ple_fwd(x1, gpre, e, g, g_next, tr):
    def body(x_ref, gp_ref, e_ref, g_ref, gn_ref, y_ref, h_ref):
        v = e_ref[...].astype(F32) * _sig(gp_ref[...].astype(F32))
        y = x_ref[...] + v * _rstd(v) * g_ref[...]
        y_ref[...] = y
        h_ref[...] = (y * _rstd(y) * gn_ref[...]).astype(h_ref.dtype)

    return _rows_call("ple_fwd", body, [x1, gpre, e, g, g_next], [True, True, True, False, False],
                      [jax.ShapeDtypeStruct(x1.shape, F32), jax.ShapeDtypeStruct(x1.shape, MXU)], [True, True], x1.shape[0], tr)


def ple_fwd_last(x1, gpre, e, g, tr):
    def body(x_ref, gp_ref, e_ref, g_ref, y_ref):
        v = e_ref[...].astype(F32) * _sig(gp_ref[...].astype(F32))
        y_ref[...] = x_ref[...] + v * _rstd(v) * g_ref[...]

    return _rows_call("ple_fwd_last", body, [x1, gpre, e, g], [True, True, True, False],
                      [jax.ShapeDtypeStruct(x1.shape, F32)], [True], x1.shape[0], tr)[0]


def loss_bwd(y, target, tr):
    d = y.shape[1]

    def body(y_ref, t_ref, dy_ref, sq_ref):
        diff = y_ref[...] - t_ref[...]
        dy_ref[...] = diff * (1.0 / d)
        _acc_rows(sq_ref, jnp.sum(diff * diff, axis=0, keepdims=True))

    return _rows_call("loss_bwd", body, [y, target], [True, True],
                      [jax.ShapeDtypeStruct(y.shape, F32), jax.ShapeDtypeStruct((1, d), F32)], [True, False], y.shape[0], tr)


def ple_bwd(dx2, gpre, e, g, tr):
    d = dx2.shape[1]

    def body(dx_ref, gp_ref, e_ref, g_ref, de_ref, dgp_ref, dg_ref):
        gate = _sig(gp_ref[...].astype(F32))
        ev = e_ref[...].astype(F32)
        dv, dg = _norm_bwd(ev * gate, g_ref[...], dx_ref[...])
        de_ref[...] = (dv * gate).astype(de_ref.dtype)
        dgp_ref[...] = (dv * ev * gate * (1.0 - gate)).astype(dgp_ref.dtype)
        _acc_rows(dg_ref, dg)

    return _rows_call("ple_bwd", body, [dx2, gpre, e, g], [True, True, True, False],
                      [jax.ShapeDtypeStruct(dx2.shape, MXU), jax.ShapeDtypeStruct(dx2.shape, MXU), jax.ShapeDtypeStruct((1, d), F32)],
                      [True, True, False], dx2.shape[0], tr)


def rms_bwd_res(dres, dh, x, g, tr):
    d = x.shape[1]

    def body(dr_ref, dh_ref, x_ref, g_ref, dx_ref, dg_ref):
        dv, dg = _norm_bwd(x_ref[...], g_ref[...], dh_ref[...].astype(F32))
        dx_ref[...] = dr_ref[...] + dv
        _acc_rows(dg_ref, dg)

    return _rows_call("rms_bwd_res", body, [dres, dh, x, g], [True, True, True, False],
                      [jax.ShapeDtypeStruct(x.shape, F32), jax.ShapeDtypeStruct((1, d), F32)], [True, False], x.shape[0], tr)


def post_bwd(dres, dh, x, g, o, g_o, tr):
    d = x.shape[1]

    def body(dr_ref, dh_ref, x_ref, g_ref, o_ref, go_ref, dx_ref, do_ref, dg_ref, dgo_ref):
        dv, dg = _norm_bwd(x_ref[...], g_ref[...], dh_ref[...].astype(F32))
        dxv = dr_ref[...] + dv
        dx_ref[...] = dxv
        dov, dgo = _norm_bwd(o_ref[...].astype(F32), go_ref[...], dxv)
        do_ref[...] = dov.astype(do_ref.dtype)
        _acc_rows(dg_ref, dg)
        _acc_rows(dgo_ref, dgo)

    return _rows_call("post_bwd", body, [dres, dh, x, g, o, g_o], [True, True, True, False, True, False],
                      [jax.ShapeDtypeStruct(x.shape, F32), jax.ShapeDtypeStruct(x.shape, MXU),
                       jax.ShapeDtypeStruct((1, d), F32), jax.ShapeDtypeStruct((1, d), F32)], [True, True, False, False], x.shape[0], tr)


def _trailing_sum(ext, w):
    s, k = ext, 1
    while k < w:
        s = s + pltpu.roll(s, k, 0)
        k *= 2
    return s


def _leading_sum(ext, w):
    n = ext.shape[0]
    s, k = ext, 1
    while k < w:
        s = s + pltpu.roll(s, n - k, 0)
        k *= 2
    return s


def _pool_inv_count(rb, tr, w, c):
    t = rb * tr + lax.broadcasted_iota(jnp.int32, (tr, c), 0)
    return 1.0 / jnp.minimum(t + 1, w).astype(F32)


def _pool_d(u_ref, up_ref, rb, tr, w):
    cur = u_ref[...].astype(F32)
    prev = jnp.where(rb > 0, up_ref[...].astype(F32), 0.0)
    ext = jnp.concatenate([prev, cur], axis=0)
    win = _trailing_sum(ext, w)[MAXW:]
    return win * _pool_inv_count(rb, tr, w, cur.shape[1]) - cur


def pool_fwd(uz, w_g, bias, scale, tr, hosted=()):
    _, s, e = uz.shape
    ng = len(POOL_WINDOWS)
    cg = e // ng
    nb = s // tr
    hb = tr // MAXW

    def body(u_ref, up_ref, z_ref, w_ref, b_ref, sc_ref, o_ref):
        g, rb = pl.program_id(0), pl.program_id(1)
        wmat = w_ref[...].reshape(cg, cg)
        for gg, win in enumerate(POOL_WINDOWS):
            @pl.when(g == gg)
            def _(win=win):
                d = _pool_d(u_ref, up_ref, rb, tr, win)
                y = (jnp.dot(d.astype(MXU), wmat, preferred_element_type=F32) + b_ref[...]) * sc_ref[...]
                z = z_ref[...].astype(F32)
                o_ref[...] = (y * z * _sig(z)).astype(o_ref.dtype)

    return _hosting_call(
        body, "pool_fwd", (ng, nb),
        [
            pl.BlockSpec((None, tr, cg), lambda g, r: (0, r, g)),
            pl.BlockSpec((None, MAXW, cg), lambda g, r: (0, jnp.maximum(r * hb - 1, 0), g)),
            pl.BlockSpec((None, tr, cg), lambda g, r: (1, r, g)),
            pl.BlockSpec((4, None, cg // 4, cg), lambda g, r: (0, g, 0, 0)),
            pl.BlockSpec((1, cg), lambda g, r: (0, g)),
            pl.BlockSpec((1, cg), lambda g, r: (0, g)),
        ],
        [pl.BlockSpec((tr, cg), lambda g, r: (r, g))], [jax.ShapeDtypeStruct((s, e), MXU)], [],
        [uz, uz, uz, w_g, bias, scale], hosted, ("parallel", "arbitrary"))


def pool_bwd(uz, dy2, w_g, bias, scale, tr, hosted=()):
    _, s, e = uz.shape
    ng = len(POOL_WINDOWS)
    cg = e // ng
    nb = s // tr
    hb = tr // MAXW

    def body(u_ref, up_ref, z_ref, dy_ref, w_ref, b_ref, sc_ref, duz_ref, gw_ref, db_ref, dsc_ref, acc_ref, carry_ref):
        g, step = pl.program_id(0), pl.program_id(1)
        rb = nb - 1 - step
        wmat = w_ref[...].reshape(cg, cg)
        for gg, win in enumerate(POOL_WINDOWS):
            @pl.when(g == gg)
            def _(win=win):
                d = _pool_d(u_ref, up_ref, rb, tr, win).astype(MXU)
                ypre = jnp.dot(d, wmat, preferred_element_type=F32) + b_ref[...]
                z = z_ref[...].astype(F32)
                sg = _sig(z)
                dy2v = dy_ref[...].astype(F32)
                dyv = dy2v * z * sg
                duz_ref[1] = (dy2v * ypre * sc_ref[...] * sg * (1.0 + z * (1.0 - sg))).astype(duz_ref.dtype)
                dypre = dyv * sc_ref[...]
                dsc = jnp.sum(dyv * ypre, axis=0, keepdims=True)
                dbv = jnp.sum(dypre, axis=0, keepdims=True)
                dypre_b = dypre.astype(MXU)
                dd = lax.dot_general(dypre_b, wmat, _DN["nt"], preferred_element_type=F32)
                gw = lax.dot_general(d, dypre_b, _DN["tn"], preferred_element_type=F32)
                q = dd * _pool_inv_count(rb, tr, win, cg)
                nxt = jnp.where(step > 0, carry_ref[...], 0.0)
                lead = _leading_sum(jnp.concatenate([q, nxt], axis=0), win)[:tr]
                duz_ref[0] = (lead - dd).astype(duz_ref.dtype)
                carry_ref[...] = q[:MAXW]

                @pl.when(step == 0)
                def _():
                    acc_ref[...] = gw
                    db_ref[...] = dbv
                    dsc_ref[...] = dsc

                @pl.when(step > 0)
                def _():
                    acc_ref[...] += gw
                    db_ref[...] += dbv
                    dsc_ref[...] += dsc

                @pl.when(step == nb - 1)
                def _():
                    gw_ref[...] = acc_ref[...].reshape(4, cg // 4, cg).astype(gw_ref.dtype)

    return _hosting_call(
        body, "pool_bwd", (ng, nb),
        [
            pl.BlockSpec((None, tr, cg), lambda g, r: (0, nb - 1 - r, g)),
            pl.BlockSpec((None, MAXW, cg), lambda g, r: (0, jnp.maximum((nb - 1 - r) * hb - 1, 0), g)),
            pl.BlockSpec((None, tr, cg), lambda g, r: (1, nb - 1 - r, g)),
            pl.BlockSpec((tr, cg), lambda g, r: (nb - 1 - r, g)),
            pl.BlockSpec((4, None, cg // 4, cg), lambda g, r: (0, g, 0, 0)),
            pl.BlockSpec((1, cg), lambda g, r: (0, g)),
            pl.BlockSpec((1, cg), lambda g, r: (0, g)),
        ],
        [
            pl.BlockSpec((2, tr, cg), lambda g, r: (0, nb - 1 - r, g)),
            pl.BlockSpec((4, None, cg // 4, cg), lambda g, r: (0, g, 0, 0)),
            pl.BlockSpec((1, cg), lambda g, r: (0, g)),
            pl.BlockSpec((1, cg), lambda g, r: (0, g)),
        ],
        [
            jax.ShapeDtypeStruct((2, s, e), MXU),
            jax.ShapeDtypeStruct(w_g.shape, WIRE),
            jax.ShapeDtypeStruct((1, e), F32),
            jax.ShapeDtypeStruct((1, e), F32),
        ],
        [pltpu.VMEM((cg, cg), F32), pltpu.VMEM((MAXW, cg), F32)],
        [uz, uz, uz, dy2, w_g, bias, scale], hosted, ("parallel", "arbitrary"))


HALO = 16


def _one_minus_sq(log_a, a):
    poly = (-2.0 * log_a) * (1.0 + log_a * (1.0 + log_a * (2.0 / 3.0)))
    return jnp.where(log_a > -0.01, poly, 1.0 - a * a)


def _softplus_neg(lam):
    t = jnp.exp(-jnp.abs(lam))
    log1p = jnp.where(t < 1e-3, t * (1.0 - t * (0.5 - t * (1.0 / 3.0))), jnp.log(1.0 + t))
    return jnp.maximum(-lam, 0.0) + log1p, _sig(-lam)


def _lru_gates(u_ref, up_ref, rb, sm_ref, wa, wx):
    cur = u_ref[...].astype(F32)
    prev = jnp.where(rb > 0, up_ref[...].astype(F32), 0.0)
    ext = jnp.concatenate([prev, cur], axis=0)
    taps = [cur] + [pltpu.roll(ext, k, 0)[HALO:] for k in range(1, CONV_W)]
    uc = sm_ref[CONV_W:CONV_W + 1, :]
    for k in range(CONV_W):
        uc = uc + taps[k] * sm_ref[CONV_W - 1 - k:CONV_W - k, :]
    ucb = uc.astype(MXU)
    r = _sig(jnp.dot(ucb, wa, preferred_element_type=F32) + sm_ref[5:6, :])
    ig = _sig(jnp.dot(ucb, wx, preferred_element_type=F32) + sm_ref[6:7, :])
    sp, sgn = _softplus_neg(sm_ref[7:8, :])
    log_a = r * (-LRU_C * sp)
    a = jnp.exp(log_a)
    mult = jnp.sqrt(jnp.maximum(_one_minus_sq(log_a, a), 0.0))
    return taps, uc, ucb, r, ig, sp, sgn, a, mult


LANES = 128


def _seg_scan(a, b, out_ref, scr, state, reverse):
    a_s, b_s, h_s, p_s = scr
    tr, c = a.shape
    seg = tr // 8
    nl = c // LANES
    for l in range(nl):
        a_s[l] = a[:, l * LANES:(l + 1) * LANES]
        b_s[l] = b[:, l * LANES:(l + 1) * LANES]
    h = [jnp.zeros((8, LANES), F32)] * nl
    pp = [jnp.ones((8, LANES), F32)] * nl
    for i in (range(seg - 1, -1, -1) if reverse else range(seg)):
        rows = pl.ds(i, 8, stride=seg)
        for l in range(nl):
            av = a_s[l, rows, :]
            h[l] = av * h[l] + b_s[l, rows, :]
            pp[l] = av * pp[l]
            h_s[l, pl.ds(8 * i, 8), :] = h[l]
            p_s[l, pl.ds(8 * i, 8), :] = pp[l]
    leaving = []
    sub = lax.broadcasted_iota(jnp.int32, (8, LANES), 0)
    for l in range(nl):
        lanes = slice(l * LANES, (l + 1) * LANES)
        st = state[:, lanes]
        entering = jnp.zeros((8, LANES), F32)
        for sgm in (range(7, -1, -1) if reverse else range(8)):
            entering = jnp.where(sub == sgm, st, entering)
            st = h[l][sgm:sgm + 1, :] + pp[l][sgm:sgm + 1, :] * st
        leaving.append(st)
        for i in range(seg):
            rows = pl.ds(8 * i, 8)
            h_s[l, rows, :] = h_s[l, rows, :] + p_s[l, rows, :] * entering
        for sgm in range(8):
            for t0 in range(0, seg, 8):
                out_ref[pl.ds(sgm * seg + t0, 8), lanes] = h_s[l, pl.ds(8 * t0 + sgm, 8, stride=8), :]
    return jnp.concatenate(leaving, axis=1)


def lru_fwd(uz, wa_g, wx_g, small, tr, hosted=()):
    _, s, e = uz.shape
    cb = e // LRU_HEADS
    nb = s // tr
    hb = tr // HALO

    def body(u_ref, up_ref, z_ref, wa_ref, wx_ref, sm_ref, o_ref, h_ref, s0, s1, s2, s3, carry_ref):
        rb = pl.program_id(1)
        wa = wa_ref[...].reshape(cb, cb)
        wx = wx_ref[...].reshape(cb, cb)
        _, uc, _, _, ig, _, _, a, mult = _lru_gates(u_ref, up_ref, rb, sm_ref, wa, wx)
        start = jnp.where(rb > 0, carry_ref[0:1, :], 0.0)
        last = _seg_scan(a, mult * ig * uc, h_ref, (s0, s1, s2, s3), start, False)
        carry_ref[...] = jnp.broadcast_to(last, carry_ref.shape)
        z = z_ref[...].astype(F32)
        o_ref[...] = (h_ref[...] * z * _sig(z)).astype(o_ref.dtype)

    wspec = pl.BlockSpec((4, None, cb // 4, cb), lambda h, r: (0, h, 0, 0))
    return _hosting_call(
        body, "lru_fwd", (LRU_HEADS, nb),
        [
            pl.BlockSpec((None, tr, cb), lambda h, r: (0, r, h)),
            pl.BlockSpec((None, HALO, cb), lambda h, r: (0, jnp.maximum(r * hb - 1, 0), h)),
            pl.BlockSpec((None, tr, cb), lambda h, r: (1, r, h)),
            wspec, wspec,
            pl.BlockSpec((8, cb), lambda h, r: (0, h)),
        ],
        [pl.BlockSpec((tr, cb), lambda h, r: (r, h)), pl.BlockSpec((tr, cb), lambda h, r: (r, h))],
        [jax.ShapeDtypeStruct((s, e), MXU), jax.ShapeDtypeStruct((s, e), F32)],
        [pltpu.VMEM((cb // LANES, tr, LANES), F32)] * 4 + [pltpu.VMEM((8, cb), F32)],
        [uz, uz, uz, wa_g, wx_g, small], hosted, ("parallel", "arbitrary"))


def lru_bwd(uz, hst, dy2, wa_g, wx_g, small, tr, hosted=()):
    _, s, e = uz.shape
    cb = e // LRU_HEADS
    nb = s // tr
    hb = tr // HALO

    def body(u_ref, up_ref, z_ref, h_ref, hp_ref, dy_ref, wa_ref, wx_ref, sm_ref,
             duz_ref, gwa_ref, gwx_ref, dsm_ref, s0, s1, s2, s3, g_s, acc_a, acc_x, gcar, acar, dcar):
        step = pl.program_id(1)
        rb = nb - 1 - step
        wa = wa_ref[...].reshape(cb, cb)
        wx = wx_ref[...].reshape(cb, cb)
        taps, uc, ucb, r, ig, sp, sgn, a, mult = _lru_gates(u_ref, up_ref, rb, sm_ref, wa, wx)
        row = lax.broadcasted_iota(jnp.int32, a.shape, 0)
        z = z_ref[...].astype(F32)
        sg = _sig(z)
        dy2v = dy_ref[...].astype(F32)
        hv = h_ref[...]
        duz_ref[1] = (dy2v * hv * sg * (1.0 + z * (1.0 - sg))).astype(duz_ref.dtype)
        a_next = jnp.where(row == tr - 1, jnp.where(step > 0, acar[0:1, :], 0.0), pltpu.roll(a, tr - 1, 0))
        g_first = _seg_scan(a_next, dy2v * z * sg, g_s, (s0, s1, s2, s3), jnp.where(step > 0, gcar[0:1, :], 0.0), True)
        gcar[...] = jnp.broadcast_to(g_first, gcar.shape)
        acar[...] = jnp.broadcast_to(a[0:1, :], acar.shape)
        gv = g_s[...]
        h_before = jnp.where(rb > 0, hp_ref[HALO - 1:HALO, :], 0.0)
        h_prev = jnp.where(row == 0, h_before, pltpu.roll(hv, 1, 0))
        da = gv * h_prev
        gu = gv * uc
        dmult = gu * ig
        dig = gu * mult
        dlog_a = da * a - dmult * jnp.where(mult > 0.0, a * a / mult, 0.0)
        dra = dlog_a * (-LRU_C) * sp * r * (1.0 - r)
        dix = dig * ig * (1.0 - ig)
        dl = jnp.sum(dlog_a * r, axis=0, keepdims=True) * (LRU_C * sgn)
        dra_b, dix_b = dra.astype(MXU), dix.astype(MXU)
        duc = (gv * mult * ig + lax.dot_general(dra_b, wa, _DN["nt"], preferred_element_type=F32)
               + lax.dot_general(dix_b, wx, _DN["nt"], preferred_element_type=F32))
        gwa = lax.dot_general(ucb, dra_b, _DN["tn"], preferred_element_type=F32)
        gwx = lax.dot_general(ucb, dix_b, _DN["tn"], preferred_element_type=F32)
        ext = jnp.concatenate([duc, jnp.where(step > 0, dcar[...], 0.0)], axis=0)
        n = ext.shape[0]
        du = duc * sm_ref[CONV_W - 1:CONV_W, :]
        for k in range(1, CONV_W):
            du = du + pltpu.roll(ext, n - k, 0)[:tr] * sm_ref[CONV_W - 1 - k:CONV_W - k, :]
        duz_ref[0] = du.astype(duz_ref.dtype)
        dcar[...] = duc[:HALO]
        rows = [jnp.sum(duc * taps[CONV_W - 1 - k], axis=0, keepdims=True) for k in range(CONV_W)]
        rows += [jnp.sum(duc, axis=0, keepdims=True), jnp.sum(dra, axis=0, keepdims=True),
                 jnp.sum(dix, axis=0, keepdims=True), dl]

        @pl.when(step == 0)
        def _():
            acc_a[...] = gwa
            acc_x[...] = gwx
            for k, rv in enumerate(rows):
                dsm_ref[k:k + 1, :] = rv

        @pl.when(step > 0)
        def _():
            acc_a[...] += gwa
            acc_x[...] += gwx
            for k, rv in enumerate(rows):
                dsm_ref[k:k + 1, :] += rv

        @pl.when(step == nb - 1)
        def _():
            gwa_ref[...] = acc_a[...].reshape(4, cb // 4, cb).astype(gwa_ref.dtype)
            gwx_ref[...] = acc_x[...].reshape(4, cb // 4, cb).astype(gwx_ref.dtype)

    wspec = pl.BlockSpec((4, None, cb // 4, cb), lambda h, r: (0, h, 0, 0))
    blk = pl.BlockSpec((tr, cb), lambda h, r: (nb - 1 - r, h))
    return _hosting_call(
        body, "lru_bwd", (LRU_HEADS, nb),
        [
            pl.BlockSpec((None, tr, cb), lambda h, r: (0, nb - 1 - r, h)),
            pl.BlockSpec((None, HALO, cb), lambda h, r: (0, jnp.maximum((nb - 1 - r) * hb - 1, 0), h)),
            pl.BlockSpec((None, tr, cb), lambda h, r: (1, nb - 1 - r, h)),
            blk,
            pl.BlockSpec((HALO, cb), lambda h, r: (jnp.maximum((nb - 1 - r) * hb - 1, 0), h)),
            blk,
            wspec, wspec,
            pl.BlockSpec((8, cb), lambda h, r: (0, h)),
        ],
        [
            pl.BlockSpec((2, tr, cb), lambda h, r: (0, nb - 1 - r, h)),
            wspec, wspec,
            pl.BlockSpec((8, cb), lambda h, r: (0, h)),
        ],
        [
            jax.ShapeDtypeStruct((2, s, e), MXU),
            jax.ShapeDtypeStruct(wa_g.shape, WIRE),
            jax.ShapeDtypeStruct(wx_g.shape, WIRE),
            jax.ShapeDtypeStruct((8, e), F32),
        ],
        ([pltpu.VMEM((cb // LANES, tr, LANES), F32)] * 4 + [pltpu.VMEM((tr, cb), F32)]
         + [pltpu.VMEM((cb, cb), F32)] * 2 + [pltpu.VMEM((8, cb), F32)] * 2 + [pltpu.VMEM((HALO, cb), F32)]),
        [uz, uz, uz, hst, hst, dy2, wa_g, wx_g, small], hosted, ("parallel", "arbitrary"))


def _place():
    x, y, c = lax.axis_index("x"), lax.axis_index("y"), lax.axis_index("c")
    chips = [(1 - x, y), (x, 1 - y), (1 - x, 1 - y)]
    return x, y, c, chips


def _rcopy(src, dst, send_sems, recv_sems, k, to):
    return pltpu.make_async_remote_copy(src_ref=src, dst_ref=dst, send_sem=send_sems.at[k], recv_sem=recv_sems.at[k],
                                        device_id=to, device_id_type=MESHID)


def _stage_gather_ici(bufs):
    n = len(bufs)

    def quarter(ref, chip, c, q):
        rq = ref.shape[3] // 2
        return ref.at[2 * chip[0] + chip[1], :, c, pl.ds(q * rq, rq)]

    def copies(refs, ss, rs, off, a, sending):
        x, y, c, _ = _place()
        me, xn, yn, dg = (x, y), (1 - x, y), (x, 1 - y), (1 - x, 1 - y)
        plan = [(0, me, xn, 0), (1, me, xn, 1), (3, me, yn, 1), (2, me, yn, 0),
                (4, xn, yn, 0), (5, yn, xn, 1)]
        if not sending:
            plan = [(0, xn, xn, 0), (1, xn, xn, 1), (3, yn, yn, 1), (2, yn, yn, 0), (4, dg, yn, 0), (5, dg, xn, 1)]
        out = []
        for k, owner, to, q in plan:
            blk = quarter(refs[a], owner, c, q)
            out.append(_rcopy(blk, blk, ss, rs, off + 6 * a + k, (*to, c)))
        return out

    def start(ro, refs, ss, rs, off):
        for a in range(n):
            for cp in copies(refs, ss, rs, off, a, True)[:4]:
                cp.start()

    def mid(ro, refs, ss, rs, off):
        for a in range(n):
            got, out = copies(refs, ss, rs, off, a, False), copies(refs, ss, rs, off, a, True)
            got[0].wait_recv()
            out[4].start()
            got[2].wait_recv()
            out[5].start()

    def finish(ro, refs, ss, rs, off):
        for a in range(n):
            got = copies(refs, ss, rs, off, a, False)
            for k in (1, 3, 4, 5):
                got[k].wait_recv()
            for cp in copies(refs, ss, rs, off, a, True):
                cp.wait_send()

    return dict(ro=[], bufs=list(bufs), nsem=6 * n, start=start, mid=mid, finish=finish)


def _stage_gather_d2d(bufs):
    n = len(bufs)

    def copies(refs, ss, rs, off, sending):
        x, y, c, chips = _place()
        out = []
        for a in range(n):
            for jj, ch in enumerate(chips):
                blk = refs[a].at[2 * ch[0] + ch[1], :, c if sending else 1 - c]
                out.append(_rcopy(blk, blk, ss, rs, off + 3 * a + jj, (x, y, 1 - c)))
        return out

    def start(ro, refs, ss, rs, off):
        for cp in copies(refs, ss, rs, off, True):
            cp.start()

    def finish(ro, refs, ss, rs, off):
        for cp in copies(refs, ss, rs, off, False):
            cp.wait_recv()
        for cp in copies(refs, ss, rs, off, True):
            cp.wait_send()

    return dict(ro=[], bufs=list(bufs), nsem=3 * n, start=start, finish=finish)


def _stage_scatter_ici(parts, gots):
    n = len(parts)

    def copies(ro, refs, ss, rs, off):
        x, y, c, chips = _place()
        return [_rcopy(ro[a].at[2 * ch[0] + ch[1]], refs[a].at[jj], ss, rs, off + 3 * a + jj, (*ch, c))
                for a in range(n) for jj, ch in enumerate(chips)]

    def start(ro, refs, ss, rs, off):
        for cp in copies(ro, refs, ss, rs, off):
            cp.start()

    def finish(ro, refs, ss, rs, off):
        for cp in copies(ro, refs, ss, rs, off):
            cp.wait()

    return dict(ro=list(parts), bufs=list(gots), nsem=3 * n, start=start, finish=finish)


def _stage_send_half(grads, lands):
    n = len(grads)

    def copies(ro, refs, ss, rs, off):
        x, y, c, _ = _place()
        return [_rcopy(ro[a].at[:, :, 1 - c], refs[a], ss, rs, off + a, (x, y, 1 - c)) for a in range(n)]

    def start(ro, refs, ss, rs, off):
        for cp in copies(ro, refs, ss, rs, off):
            cp.start()

    def finish(ro, refs, ss, rs, off):
        for cp in copies(ro, refs, ss, rs, off):
            cp.wait()

    return dict(ro=list(grads), bufs=list(lands), nsem=n, start=start, finish=finish)


def cast_into_slab(w, first, l, k_idx):
    _, r, c = w.shape
    rh = r // 2
    lb, tr = _block_lr(l, rh, c)
    nbh = rh // tr
    assert first % lb == 0

    def body(k_ref, w_ref, o_ref):
        o_ref[...] = w_ref[...].astype(o_ref.dtype)

    return pl.pallas_call(
        body, name="cast_into_slab",
        grid_spec=pltpu.PrefetchScalarGridSpec(
            num_scalar_prefetch=1, grid=(l // lb, 2, nbh),
            in_specs=[pl.BlockSpec((lb, tr, c), lambda i, h, b, k_ref: (first // lb + i, h * nbh + b, 0))],
            out_specs=pl.BlockSpec((None, lb, None, tr, c), lambda i, h, b, k_ref: (k_ref[0], i, h, b, 0))),
        out_shape=jax.ShapeDtypeStruct((4, l, 2, rh, c), WIRE),
        compiler_params=_cp(("parallel", "parallel", "parallel")),
    )(k_idx, w)


def gather_weights(bufs):
    n = len(bufs)
    ici = [_stage_gather_ici([b]) for b in bufs]
    d2d = [_stage_gather_d2d([b]) for b in bufs]
    per = ici[0]["nsem"] + d2d[0]["nsem"]

    def body(*refs):
        outs = refs[n:2 * n]
        ss, rs = refs[2 * n:]
        for what in ("start", "mid"):
            for a in range(n):
                ici[a][what]([], [outs[a]], ss, rs, per * a)
        for a in range(n):
            ici[a]["finish"]([], [outs[a]], ss, rs, per * a)
            d2d[a]["start"]([], [outs[a]], ss, rs, per * a + ici[a]["nsem"])
        for a in range(n):
            d2d[a]["finish"]([], [outs[a]], ss, rs, per * a + ici[a]["nsem"])

    return pl.pallas_call(
        body, name="gather_weights",
        in_specs=[ANY] * n, out_specs=[ANY] * n,
        out_shape=[jax.ShapeDtypeStruct(a.shape, a.dtype) for a in bufs],
        scratch_shapes=[pltpu.SemaphoreType.DMA((per * n,)), pltpu.SemaphoreType.DMA((per * n,))],
        input_output_aliases={a: a for a in range(n)},
        compiler_params=pltpu.CompilerParams(has_side_effects=True),
    )(*bufs)


def all_gather_small(v, name):
    m_per, n = v.shape

    def body(x_ref, out_ref, send_sems, recv_sems, local_sem):
        x, y, c, chips = _place()
        me, sibling = (x, y, c), (x, y, 1 - c)

        def rows(px, py, pc):
            return out_ref.at[pl.ds((4 * px + 2 * py + pc) * m_per, m_per), :]

        def copy(k, block, to, src=None):
            return _rcopy(rows(*block) if src is None else src, rows(*block), send_sems, recv_sems, k, to)

        mine = pltpu.make_async_copy(x_ref, rows(*me), local_sem)
        mine.start()
        first = [copy(0, me, sibling, src=x_ref)]
        first += [copy(1 + jj, me, (*chip, c), src=x_ref) for jj, chip in enumerate(chips)]
        for cp in first:
            cp.start()
        passed = [copy(4 + jj, (*chip, c), sibling) for jj, chip in enumerate(chips)]
        for jj, chip in enumerate(chips):
            copy(1 + jj, (*chip, c), me).wait_recv()
            passed[jj].start()
        copy(0, sibling, me).wait_recv()
        for jj, chip in enumerate(chips):
            copy(4 + jj, (*chip, 1 - c), me).wait_recv()
        for cp in first + passed:
            cp.wait_send()
        mine.wait()

    return pl.pallas_call(
        body, name=name,
        out_shape=jax.ShapeDtypeStruct((8 * m_per, n), v.dtype),
        in_specs=[pl.BlockSpec(memory_space=pltpu.VMEM)],
        out_specs=pl.BlockSpec(memory_space=pltpu.VMEM),
        scratch_shapes=[pltpu.SemaphoreType.DMA((7,)), pltpu.SemaphoreType.DMA((7,)), pltpu.SemaphoreType.DMA],
        compiler_params=pltpu.CompilerParams(vmem_limit_bytes=VMEM_LIMIT),
    )(v)


def sum_devices(g):
    def body(g_ref, o_ref):
        acc = g_ref[0]
        for d in range(1, 8):
            acc = acc + g_ref[d]
        o_ref[...] = acc

    return pl.pallas_call(body, name="sum_devices", out_shape=jax.ShapeDtypeStruct(g.shape[1:], g.dtype),
                          compiler_params=pltpu.CompilerParams(vmem_limit_bytes=VMEM_LIMIT))(g)


def send_other_half(grads):
    n = len(grads)

    def body(*refs):
        ins, outs = refs[:n], refs[n:2 * n]
        send_sems, recv_sems = refs[2 * n:]
        x, y, c, _ = _place()
        sib = (x, y, 1 - c)
        cps = [_rcopy(ins[a].at[:, :, 1 - c], outs[a], send_sems, recv_sems, a, sib) for a in range(n)]
        for cp in cps:
            cp.start()
        for cp in cps:
            cp.wait()

    return pl.pallas_call(
        body, name="send_other_half", in_specs=[ANY] * n, out_specs=[ANY] * n,
        out_shape=[jax.ShapeDtypeStruct(g.shape[:2] + g.shape[3:], g.dtype) for g in grads],
        scratch_shapes=[pltpu.SemaphoreType.DMA((n,)), pltpu.SemaphoreType.DMA((n,))],
        compiler_params=pltpu.CompilerParams(has_side_effects=True),
    )(*grads)


def scatter_to_chips(parts):
    n = len(parts)

    def body(*refs):
        ins, outs = refs[:n], refs[n:2 * n]
        send_sems, recv_sems = refs[2 * n:]
        x, y, c, chips = _place()
        cps = []
        for a in range(n):
            for jj, ch in enumerate(chips):
                cps.append(_rcopy(ins[a].at[2 * ch[0] + ch[1]], outs[a].at[jj], send_sems, recv_sems, 3 * a + jj, (*ch, c)))
        for cp in cps:
            cp.start()
        for cp in cps:
            cp.wait()

    return pl.pallas_call(
        body, name="scatter_to_chips", in_specs=[ANY] * n, out_specs=[ANY] * n,
        out_shape=[jax.ShapeDtypeStruct((3,) + p.shape[1:], p.dtype) for p in parts],
        scratch_shapes=[pltpu.SemaphoreType.DMA((3 * n,)), pltpu.SemaphoreType.DMA((3 * n,))],
        compiler_params=pltpu.CompilerParams(has_side_effects=True),
    )(*parts)


def share_halves(bufs, spans):
    n = len(bufs)

    def body(*refs):
        outs = refs[n:2 * n]
        send_sems, recv_sems = refs[2 * n:]
        x, y, c, _ = _place()
        sib = (x, y, 1 - c)

        def blk(a, half):
            return outs[a].at[pl.ds(spans[a][0], spans[a][1]), half]

        cps = [_rcopy(blk(a, c), blk(a, c), send_sems, recv_sems, a, sib) for a in range(n)]
        for cp in cps:
            cp.start()
        for a in range(n):
            _rcopy(blk(a, 1 - c), blk(a, 1 - c), send_sems, recv_sems, a, sib).wait_recv()
        for cp in cps:
            cp.wait_send()

    return pl.pallas_call(
        body, name="share_halves", in_specs=[ANY] * n, out_specs=[ANY] * n,
        out_shape=[jax.ShapeDtypeStruct(b.shape, b.dtype) for b in bufs],
        scratch_shapes=[pltpu.SemaphoreType.DMA((n,)), pltpu.SemaphoreType.DMA((n,))],
        input_output_aliases={a: a for a in range(n)},
        compiler_params=pltpu.CompilerParams(has_side_effects=True),
    )(*bufs)


def _block_rows(r, c, itemsize, budget=1 << 20):
    tr = r
    while tr * c * itemsize > budget and tr % 16 == 0:
        tr //= 2
    return tr


def _block_lr(l, r, c, budget=4 << 20):
    tr = _block_rows(r, c, 4, budget)
    lb = 1
    if tr == r:
        while l % (2 * lb) == 0 and 2 * lb * r * c * 4 <= budget:
            lb *= 2
    return lb, tr


def add_halves(g, got, c_idx):
    k4, l, _, rh, cc = g.shape
    lb, tr = _block_lr(l, rh, cc)

    def body(c_ref, g_ref, r_ref, o_ref):
        o_ref[...] = (g_ref[...].astype(F32) + r_ref[...].astype(F32)).astype(o_ref.dtype)

    return pl.pallas_call(
        body, name="add_halves",
        grid_spec=pltpu.PrefetchScalarGridSpec(
            num_scalar_prefetch=1, grid=(k4, l // lb, rh // tr),
            in_specs=[pl.BlockSpec((None, lb, None, tr, cc), lambda k, i, b, c_ref: (k, i, c_ref[0], b, 0)),
                      pl.BlockSpec((None, lb, tr, cc), lambda k, i, b, c_ref: (k, i, b, 0))],
            out_specs=pl.BlockSpec((None, lb, tr, cc), lambda k, i, b, c_ref: (k, i, b, 0))),
        out_shape=jax.ShapeDtypeStruct(got.shape, WIRE),
        compiler_params=_cp(("parallel", "parallel", "parallel")),
    )(c_idx, g, got)


def sum_chips(own, got, kc_idx, full, first):
    _, l, rh, cc = own.shape
    lb, tr = _block_lr(l, rh, cc, 2 << 20)
    assert first % lb == 0

    def body(k_ref, o_ref, r_ref, _full, s_ref):
        s_ref[...] = ((o_ref[...].astype(F32) + r_ref[0].astype(F32)) + r_ref[1].astype(F32)) + r_ref[2].astype(F32)

    return pl.pallas_call(
        body, name="sum_chips",
        grid_spec=pltpu.PrefetchScalarGridSpec(
            num_scalar_prefetch=1, grid=(l // lb, rh // tr),
            in_specs=[pl.BlockSpec((None, lb, tr, cc), lambda i, b, k_ref: (k_ref[0], i, b, 0)),
                      pl.BlockSpec((3, lb, tr, cc), lambda i, b, k_ref: (0, i, b, 0)),
                      ANY],
            out_specs=pl.BlockSpec((lb, None, tr, cc), lambda i, b, k_ref: (first // lb + i, k_ref[1], b, 0))),
        out_shape=jax.ShapeDtypeStruct(full.shape, F32),
        input_output_aliases={3: 0},
        compiler_params=_cp(("parallel", "parallel")),
    )(kc_idx, own, got, full)


def _adam_math(w, g, m, v):
    m = ADAM_B1 * m + (1.0 - ADAM_B1) * g
    v = ADAM_B2 * v + (1.0 - ADAM_B2) * (g * g)
    m_hat = m / (1.0 - ADAM_B1 ** ADAM_STEP)
    v_hat = v / (1.0 - ADAM_B2 ** ADAM_STEP)
    delta = -ADAM_LR * (m_hat / (jnp.sqrt(v_hat) + ADAM_EPS) + ADAM_WD * w)
    return delta, m, v


def adamw(w, g, g_first, m, v, lo, hi, prev=None, hosted=()):
    l, r, c = w.shape
    tr = _block_rows(r, c, 4, 2 << 20)
    prev = list(prev or [])

    def body(w_ref, g_ref, m_ref, v_ref, *rest):
        go_ref, d_ref, mo_ref, vo_ref = rest[len(prev):]
        gv = g_ref[...]
        go_ref[...] = gv
        d_ref[...], mo_ref[...], vo_ref[...] = _adam_math(w_ref[...], gv, m_ref[...], v_ref[...])

    spec = pl.BlockSpec((None, tr, c), lambda i, b: (lo + i, b, 0))
    gspec = pl.BlockSpec((None, tr, c), lambda i, b: (g_first + i, b, 0))
    return _hosting_call(
        body, "adamw", (hi - lo, r // tr), [spec, gspec, spec, spec] + [ANY] * len(prev), [spec] * 4,
        [jax.ShapeDtypeStruct(w.shape, F32)] * 4, [], [w, g, m, v] + prev, hosted, ("parallel", "parallel"),
        aliases={4 + t: t for t in range(len(prev))})


def adamw_small(w, g, m, v):
    def body(w_ref, g_ref, m_ref, v_ref, d_ref, mo_ref, vo_ref):
        d_ref[...], mo_ref[...], vo_ref[...] = _adam_math(w_ref[...], g_ref[...], m_ref[...], v_ref[...])

    return pl.pallas_call(body, name="adamw_small", out_shape=[jax.ShapeDtypeStruct(w.shape, F32)] * 3)(w, g, m, v)


def kernel(x, p, w_in, w_out, g_pre, g_post, pool_w, pool_b, pool_scale, conv_w, conv_b, lru_wa, lru_ba, lru_wx, lru_bx, lru_L, w_ple, w_ple_gate, g_ple_in, g_ple_out, loss_target, m_w_in, m_w_out, m_g_pre, m_g_post, m_pool_w, m_pool_b, m_pool_scale, m_conv_w, m_conv_b, m_lru_wa, m_lru_ba, m_lru_wx, m_lru_bx, m_lru_L, m_w_ple, m_w_ple_gate, m_g_ple_in, m_g_ple_out, v_w_in, v_w_out, v_g_pre, v_g_post, v_pool_w, v_pool_b, v_pool_scale, v_conv_w, v_conv_b, v_lru_wa, v_lru_ba, v_lru_wx, v_lru_bx, v_lru_L, v_w_ple, v_w_ple_gate, v_g_ple_in, v_g_ple_out):
    depth = w_in.shape[0]
    _, s, d = x.shape
    e = 2 * d
    kp = p.shape[-1]
    nmix = pool_w.shape[0]
    ngrp = pool_w.shape[1]
    cg = e // ngrp
    cb = e // LRU_HEADS
    xi, yi, ci = lax.axis_index("x"), lax.axis_index("y"), lax.axis_index("c")
    me = 2 * xi + yi
    c_idx = jnp.reshape(ci, (1,)).astype(jnp.int32)
    k_idx = jnp.reshape(me, (1,)).astype(jnp.int32)
    tr_row = _tile(s, 256)
    tr_mix = _tile(s, 512)
    tm = _tile(s, 1024)
    tm2 = _tile(s, 2048)

    def halves(a):
        return a.reshape(a.shape[0], 2, a.shape[1] // 2, a.shape[2])

    big = {
        "w_in": w_in, "w_out": w_out, "gate": w_ple_gate, "ple": w_ple,
        "pool": pool_w.reshape(nmix * ngrp, cg // 4, cg),
        "wa": lru_wa.reshape(nmix * LRU_HEADS, cb // 4, cb), "wx": lru_wx.reshape(nmix * LRU_HEADS, cb // 4, cb),
    }
    names = list(big)

    def layer_shards(i):
        sh = {n: (big[n], i, 1) for n in ("w_in", "w_out", "gate", "ple")}
        if i % 2 == 0:
            sh["pool"] = (big["pool"], (i // 2) * ngrp, ngrp)
        else:
            sh["wa"], sh["wx"] = (big["wa"], (i // 2) * LRU_HEADS, LRU_HEADS), (big["wx"], (i // 2) * LRU_HEADS, LRU_HEADS)
        return sh

    def mixer_names(i):
        return ["pool"] if i % 2 == 0 else ["wa", "wx"]

    wbuf = [{n: cast_into_slab(*wfl, k_idx) for n, wfl in layer_shards(i).items()} for i in range(depth)]
    first = list(wbuf[0])
    wbuf[0] = dict(zip(first, gather_weights([wbuf[0][n] for n in first])))

    def full_w(i, n):
        b = wbuf[i][n]
        return b.reshape(b.shape[0], b.shape[1], 2 * b.shape[3], b.shape[4])

    def stages_of(specs):
        return [mk([wbuf[l][n] for n in nms]) for mk, l, nms in specs]

    def keep(specs, new):
        for (l, n), b in zip([(l, n) for _, l, nms in specs for n in nms], new):
            wbuf[l][n] = b

    def run_mm(specs, *args):
        if not specs:
            return _matmul(*args)
        out, new = _matmul(*args, hosted=stages_of(specs))
        keep(specs, new)
        return out

    ec = e // 4
    small_loc = jnp.concatenate([conv_w, conv_b[:, None], lru_ba[:, None], lru_bx[:, None], lru_L[:, None]], axis=1)
    sm_all = all_gather_small(small_loc.reshape(nmix * 8, ec), "gather_small").reshape(4, 2, nmix, 8, ec)
    lru_small = jnp.transpose(sm_all[:, 0], (1, 2, 0, 3)).reshape(nmix, 8, e)

    xs = x[0]
    saved = []
    for i in range(depth):
        j = i // 2
        h = rms_fwd(xs, g_pre[i][None], tr_row) if i == 0 else h_next
        nj = (2 * e) // 1024 if (2 * e) % 1024 == 0 else 4
        tn = (2 * e) // nj
        per = e // tn
        perk = (e // 2) // tn
        nxt = i + 1 if i + 1 < depth else None
        stages = [(_stage_gather_d2d, i, ["gate", "ple"])] if i > 0 else []
        if nxt is not None:
            stages.append((_stage_gather_ici, nxt, ["w_in"]))
        uz = run_mm(
            stages, "mm_in", "nn", (s // tm2, nj, 1), h, pl.BlockSpec((tm2, d), lambda a, b, k: (a, 0)),
            full_w(i, "w_in"), pl.BlockSpec((None, None, d, tn), lambda a, b, k, perk=perk: (b // perk, 0, 0, b % perk)),
            jax.ShapeDtypeStruct((2, s, e), MXU), pl.BlockSpec((None, tm2, tn), lambda a, b, k, per=per: (b // per, a, b % per)),
            (8, 128))
        specs = []
        if nxt is not None:
            specs = [(_stage_gather_ici, nxt, mixer_names(nxt) + (["w_out"] if i % 2 else []))]
        if i % 2 == 0:
            (y2,), new = pool_fwd(uz, full_w(i, "pool"), pool_b[j][None], pool_scale[j][None], tr_mix, hosted=stages_of(specs))
            hst = None
        else:
            (y2, hst), new = lru_fwd(uz, full_w(i, "wa"), full_w(i, "wx"), lru_small[j], tr_mix, hosted=stages_of(specs))
        keep(specs, new)
        tn_o = _tile(d, 1024)
        stages = [(_stage_gather_ici, nxt, ["w_out"])] if nxt is not None and i % 2 == 0 else []
        o = run_mm(
            stages, "mm_out", "nn", (s // tm, d // tn_o, 1), y2, pl.BlockSpec((tm, e), lambda a, b, k: (a, 0)),
            full_w(i, "w_out"), pl.BlockSpec((4, None, e // 4, tn_o), lambda a, b, k: (0, 0, 0, b)),
            jax.ShapeDtypeStruct((s, d), MXU), pl.BlockSpec((tm, tn_o), lambda a, b, k: (a, b)), (8, 128))
        x1, hn = res_rms_fwd(xs, o, g_post[i][None], g_ple_in[i][None], tr_row)
        stages = []
        if nxt is not None:
            stages = [(_stage_gather_ici, nxt, ["gate", "ple"]), (_stage_gather_d2d, nxt, ["w_out"] + mixer_names(nxt))]
        gpre = run_mm(
            stages, "mm_gate", "nn", (s // tm2, d // tn_o, 1), hn, pl.BlockSpec((tm2, d), lambda a, b, k: (a, 0)),
            full_w(i, "gate"), pl.BlockSpec((4, None, d // 4, tn_o), lambda a, b, k: (0, 0, 0, b)),
            jax.ShapeDtypeStruct((s, d), MXU), pl.BlockSpec((tm2, tn_o), lambda a, b, k: (a, b)), (8, 128))
        pe = p[i, 0]
        ev = run_mm(
            [(_stage_gather_d2d, nxt, ["w_in"])] if nxt is not None else [],
            "mm_ple", "nn", (s // tm, 4, 1), pe, pl.BlockSpec((tm, kp), lambda a, b, k: (a, 0)),
            full_w(i, "ple"), pl.BlockSpec((None, None, kp, d // 4), lambda a, b, k: (b, 0, 0, 0)),
            jax.ShapeDtypeStruct((s, d), MXU), pl.BlockSpec((tm, d // 4), lambda a, b, k: (a, b)), (8, 128))
        if nxt is not None:
            x2, h_next = ple_fwd(x1, gpre, ev, g_ple_out[i][None], g_pre[nxt][None], tr_row)
        else:
            x2 = ple_fwd_last(x1, gpre, ev, g_ple_out[i][None], tr_row)
        saved.append((xs, h, uz, y2, hst, o, x1, hn, gpre, ev))
        xs = x2

    dx, sq = loss_bwd(xs, loss_target[0], tr_row)
    d_gpre, d_gpost, d_gin, d_gout = [None] * depth, [None] * depth, [None] * depth, [None] * depth
    d_pool_b, d_pool_sc, d_lru_small = [None] * nmix, [None] * nmix, [None] * nmix
    ts = _tile(s, 1024)
    kc_idx = jnp.stack([me, ci]).astype(jnp.int32)
    full = {n: lax.empty(halves(big[n]).shape, F32) for n in names}
    w_sum = w_got = None

    def first_row(i, n):
        return i if n in ("w_in", "w_out", "gate", "ple") else (i // 2) * (ngrp if n == "pool" else LRU_HEADS)

    for i in reversed(range(depth)):
        j = i // 2
        prev = i + 1 if i + 1 < depth else None
        x0, h, uz, y2, hst, o, x1, hn, gpre, ev = saved[i]
        pe = p[i, 0]
        gl = {}
        de, dgp, d_gout[i] = ple_bwd(dx, gpre, ev, g_ple_out[i][None], tr_row)
        gl["ple"] = _matmul(
            "mm_dple", "tn", (1, 4, s // ts), pe, pl.BlockSpec((ts, kp), lambda a, b, k: (k, 0)),
            de, pl.BlockSpec((ts, d // 4), lambda a, b, k: (k, b)),
            jax.ShapeDtypeStruct((4, 1, kp, d // 4), WIRE), pl.BlockSpec((None, None, kp, d // 4), lambda a, b, k: (b, 0, 0, 0)),
            (kp, d // 4))
        tn_o = _tile(d, 1024)
        gl["gate"] = _matmul(
            "mm_dgate", "tn", (4, d // tn_o, 1), hn, pl.BlockSpec((s, d // 4), lambda a, b, k: (0, a)),
            dgp, pl.BlockSpec((s, tn_o), lambda a, b, k: (0, b)),
            jax.ShapeDtypeStruct((4, 1, d // 4, d), WIRE), pl.BlockSpec((None, None, d // 4, tn_o), lambda a, b, k: (a, 0, 0, b)),
            (8, 128))
        dhn = _matmul(
            "mm_dhn", "nt", (s // tm2, 4, 1), dgp, pl.BlockSpec((tm2, d), lambda a, b, k: (a, 0)),
            full_w(i, "gate"), pl.BlockSpec((None, None, d // 4, d), lambda a, b, k: (b, 0, 0, 0)),
            jax.ShapeDtypeStruct((s, d), MXU), pl.BlockSpec((tm2, d // 4), lambda a, b, k: (a, b)), (8, 128))
        dx1, do, d_gin[i], d_gpost[i] = post_bwd(dx, dhn, x1, g_ple_in[i][None], o, g_post[i][None], tr_row)
        gl["w_out"] = _matmul(
            "mm_dwout", "tn", (4, d // tn_o, 1), y2, pl.BlockSpec((s, e // 4), lambda a, b, k: (0, a)),
            do, pl.BlockSpec((s, tn_o), lambda a, b, k: (0, b)),
            jax.ShapeDtypeStruct((4, 1, e // 4, d), WIRE), pl.BlockSpec((None, None, e // 4, tn_o), lambda a, b, k: (a, 0, 0, b)),
            (8, 128))
        dy2 = _matmul(
            "mm_dy2", "nt", (s // tm2, 4, 1), do, pl.BlockSpec((tm2, d), lambda a, b, k: (a, 0)),
            full_w(i, "w_out"), pl.BlockSpec((None, None, e // 4, d), lambda a, b, k: (b, 0, 0, 0)),
            jax.ShapeDtypeStruct((s, e), MXU), pl.BlockSpec((tm2, e // 4), lambda a, b, k: (a, b)), (8, 128))
        riding = [_stage_scatter_ici([w_sum], [w_got])] if prev is not None else []
        if i % 2 == 0:
            (duz, gl["pool"], d_pool_b[j], d_pool_sc[j]), passed = pool_bwd(
                uz, dy2, full_w(i, "pool"), pool_b[j][None], pool_scale[j][None], tr_mix, hosted=riding)
        else:
            (duz, gl["wa"], gl["wx"], d_lru_small[j]), passed = lru_bwd(
                uz, hst, dy2, full_w(i, "wa"), full_w(i, "wx"), lru_small[j], tr_mix, hosted=riding)
        if prev is not None:
            full["w_in"] = sum_chips(w_sum, passed[0], kc_idx, full["w_in"], first_row(prev, "w_in"))

        def in_halves(g):
            return g.reshape(4, g.shape[1], 2, g.shape[2] // 2, g.shape[3])

        early = list(gl)
        eparts = [in_halves(gl[n]) for n in early]
        tmi = _tile(d, 1024)
        tni = _tile(e // 2, 1024)
        nslab = (e // 2) // tni
        gl["w_in"], from_sib = _matmul(
            "mm_dwin", "tn", (d // tmi, 4 * nslab, 1), h, pl.BlockSpec((s, tmi), lambda a, b, k: (0, a)),
            duz, pl.BlockSpec((None, s, tni), lambda a, b, k, nslab=nslab: (b // (2 * nslab), 0, b % (2 * nslab))),
            jax.ShapeDtypeStruct((4, 1, d, e // 2), WIRE),
            pl.BlockSpec((None, None, tmi, tni), lambda a, b, k, nslab=nslab: (b // nslab, 0, a, b % nslab)),
            (8, 128), hosted=[_stage_send_half(eparts, [lax.empty(g.shape[:2] + g.shape[3:], WIRE) for g in eparts])])
        esums = [add_halves(g, r, c_idx) for g, r in zip(eparts, from_sib)]
        wpart = in_halves(gl["w_in"])
        tnd = _tile(d, 1024)
        dh, passed = _matmul(
            "mm_dh", "nt", (s // tm, d // tnd, 4), duz, pl.BlockSpec((None, tm, e // 2), lambda a, b, k: (k // 2, a, k % 2)),
            full_w(i, "w_in"), pl.BlockSpec((None, None, tnd, e // 2), lambda a, b, k: (k, 0, b, 0)),
            jax.ShapeDtypeStruct((s, d), MXU), pl.BlockSpec((tm, tnd), lambda a, b, k: (a, b)), (tm, tnd),
            hosted=[_stage_scatter_ici(esums, [lax.empty((3,) + q.shape[1:], WIRE) for q in esums]),
                    _stage_send_half([wpart], [lax.empty(wpart.shape[:2] + wpart.shape[3:], WIRE)])])
        for n, q, got in zip(early, esums, passed[:len(early)]):
            full[n] = sum_chips(q, got, kc_idx, full[n], first_row(i, n))
        w_sum = add_halves(wpart, passed[len(early)], c_idx)
        w_got = lax.empty((3,) + w_sum.shape[1:], WIRE)
        dx, d_gpre[i] = rms_bwd_res(dx1, dh, x0, g_pre[i][None], tr_row)
    grad_x = dx[None]

    full["w_in"] = sum_chips(w_sum, scatter_to_chips([w_sum])[0], kc_idx, full["w_in"], 0)
    shared = share_halves([full[n] for n in names], [(0, full[n].shape[0]) for n in names])
    grads = {n: f.reshape(big[n].shape) for n, f in zip(names, shared)}

    def rows_e(a):
        return jnp.stack(a).reshape(-1, e) if isinstance(a, list) else a.reshape(-1, e)

    pack = [rows_e([g[0] for g in d_gpre]), rows_e([g[0] for g in d_gpost]), rows_e([g[0] for g in d_gin]), rows_e([g[0] for g in d_gout]),
            jnp.concatenate(d_pool_b, axis=0), jnp.concatenate(d_pool_sc, axis=0), jnp.concatenate(d_lru_small, axis=0),
            jnp.pad(sq, ((0, 0), (0, e - d)))]
    sizes = [a.shape[0] for a in pack]
    packed = jnp.concatenate(pack, axis=0)
    nrow = packed.shape[0]
    nrow_p = -(-nrow // 8) * 8
    packed = jnp.pad(packed, ((0, nrow_p - nrow), (0, 0)))
    total = sum_devices(all_gather_small(packed, "gather_grads").reshape(8, nrow_p, e))
    parts, off = [], 0
    for n_ in sizes:
        parts.append(total[off:off + n_])
        off += n_
    t_gpre, t_gpost, t_gin, t_gout, t_pb, t_psc, t_lru, t_sq = parts
    loss = 0.5 * jnp.sum(t_sq) / d
    t_lru = t_lru.reshape(nmix, 8, e)
    t_lru_loc = lax.dynamic_slice_in_dim(t_lru, me * ec, ec, axis=2)

    def big_update(name, w, m, v):
        shp = big[name].shape
        res4, _ = adamw(w.reshape(shp), grads[name], 0, m.reshape(shp), v.reshape(shp), 0, shp[0])
        return [a.reshape(w.shape) for a in res4]

    def small_update(w, g, m, v):
        shp = w.shape
        w2 = w.reshape(-1, shp[-1])
        dl, nm, nv = adamw_small(w2, g.reshape(w2.shape), m.reshape(w2.shape), v.reshape(w2.shape))
        return [g.reshape(shp), dl.reshape(shp), nm.reshape(shp), nv.reshape(shp)]

    res = {
        "w_in": big_update("w_in", w_in, m_w_in, v_w_in),
        "w_out": big_update("w_out", w_out, m_w_out, v_w_out),
        "g_pre": small_update(g_pre, t_gpre.reshape(depth, d), m_g_pre, v_g_pre),
        "g_post": small_update(g_post, t_gpost.reshape(depth, d), m_g_post, v_g_post),
        "pool_w": big_update("pool", pool_w, m_pool_w, v_pool_w),
        "pool_b": small_update(pool_b, t_pb, m_pool_b, v_pool_b),
        "pool_scale": small_update(pool_scale, t_psc, m_pool_scale, v_pool_scale),
        "conv_w": small_update(conv_w, t_lru_loc[:, :CONV_W], m_conv_w, v_conv_w),
        "conv_b": small_update(conv_b, t_lru_loc[:, 4], m_conv_b, v_conv_b),
        "lru_wa": big_update("wa", lru_wa, m_lru_wa, v_lru_wa),
        "lru_ba": small_update(lru_ba, t_lru_loc[:, 5], m_lru_ba, v_lru_ba),
        "lru_wx": big_update("wx", lru_wx, m_lru_wx, v_lru_wx),
        "lru_bx": small_update(lru_bx, t_lru_loc[:, 6], m_lru_bx, v_lru_bx),
        "lru_L": small_update(lru_L, t_lru_loc[:, 7], m_lru_L, v_lru_L),
        "w_ple": big_update("ple", w_ple, m_w_ple, v_w_ple),
        "w_ple_gate": big_update("gate", w_ple_gate, m_w_ple_gate, v_w_ple_gate),
        "g_ple_in": small_update(g_ple_in, t_gin.reshape(depth, d), m_g_ple_in, v_g_ple_in),
        "g_ple_out": small_update(g_ple_out, t_gout.reshape(depth, d), m_g_ple_out, v_g_ple_out),
    }
    order = ["w_in", "w_out", "g_pre", "g_post", "pool_w", "pool_b", "pool_scale", "conv_w", "conv_b", "lru_wa", "lru_ba",
             "lru_wx", "lru_bx", "lru_L", "w_ple", "w_ple_gate", "g_ple_in", "g_ple_out"]
    out = [loss, grad_x]
    for slot in range(4):
        out += [res[n][slot] for n in order]
    return tuple(out)
```

```python
import functools

import jax
import jax.numpy as jnp
from jax import lax
from jax.experimental import pallas as pl
from jax.experimental.pallas import tpu as pltpu

F32 = jnp.float32
MXU = jnp.bfloat16
WIRE = jnp.bfloat16
VMEM_LIMIT = 56 * 1024 * 1024
RMS_EPS = 1e-6
LRU_C = 8.0
POOL_WINDOWS = (2, 4, 8, 16)
MAXW = 16
CONV_W = 4
LRU_HEADS = 16
ADAM_LR, ADAM_B1, ADAM_B2, ADAM_EPS, ADAM_WD, ADAM_STEP = 0.001, 0.9, 0.999, 1e-08, 0.01, 10
MESHID = pl.DeviceIdType.MESH
ANY = pl.BlockSpec(memory_space=pl.ANY)


def _cp(sem=None):
    return pltpu.CompilerParams(dimension_semantics=sem, vmem_limit_bytes=VMEM_LIMIT)


def _sig(v):
    return 0.5 * jnp.tanh(0.5 * v) + 0.5


def _tile(n, pref):
    return pref if n % pref == 0 else n


_DN = {"nn": (((1,), (0,)), ((), ())), "nt": (((1,), (1,)), ((), ())), "tn": (((0,), (0,)), ((), ()))}


def _hosting_call(body, name, grid, in_specs, out_specs, out_shape, scratch_shapes, args, hosted, sem, aliases=None):
    n_in, n_out = len(args), len(out_shape)
    aliases = dict(aliases or {})
    if not hosted:
        res = pl.pallas_call(body, name=name, grid=grid, in_specs=in_specs, out_specs=out_specs, out_shape=out_shape,
                             scratch_shapes=scratch_shapes, input_output_aliases=aliases, compiler_params=_cp(sem))(*args)
        return list(res), []
    ro = [r for st in hosted for r in st["ro"]]
    bufs = [r for st in hosted for r in st["bufs"]]
    nro, nbuf = len(ro), len(bufs)
    nsem = sum(st["nsem"] for st in hosted)
    total = 1
    for g in grid:
        total *= g

    def wrapped(*refs):
        ins, ro_refs = refs[:n_in], refs[n_in:n_in + nro]
        outs = refs[n_in + nro + nbuf:n_in + nro + nbuf + n_out]
        buf_refs = refs[n_in + nro + nbuf + n_out:n_in + nro + 2 * nbuf + n_out]
        scr, ss, rs = refs[n_in + nro + 2 * nbuf + n_out:-2], refs[-2], refs[-1]
        step = 0
        for t, g in enumerate(grid):
            step = step * g + pl.program_id(t)

        def run_stages(what):
            r0 = b0 = s0 = 0
            for st in hosted:
                if what in st:
                    st[what](ro_refs[r0:r0 + len(st["ro"])], buf_refs[b0:b0 + len(st["bufs"])], ss, rs, s0)
                r0, b0, s0 = r0 + len(st["ro"]), b0 + len(st["bufs"]), s0 + st["nsem"]

        @pl.when(step == 0)
        def _():
            run_stages("start")

        if any("mid" in st for st in hosted):
            @pl.when(step == total // 2)
            def _():
                run_stages("mid")

        body(*ins, *outs, *scr)

        @pl.when(step == total - 1)
        def _():
            run_stages("finish")

    res = pl.pallas_call(
        wrapped, name=name, grid=grid,
        in_specs=list(in_specs) + [ANY] * (nro + nbuf), out_specs=list(out_specs) + [ANY] * nbuf,
        out_shape=list(out_shape) + [jax.ShapeDtypeStruct(x.shape, x.dtype) for x in bufs],
        scratch_shapes=list(scratch_shapes) + [pltpu.SemaphoreType.DMA((nsem,)), pltpu.SemaphoreType.DMA((nsem,))],
        input_output_aliases={**aliases, **{n_in + nro + t: n_out + t for t in range(nbuf)}},
        compiler_params=_cp(("arbitrary",) * len(grid)),
    )(*args, *ro, *bufs)
    return list(res[:n_out]), list(res[n_out:])


def _matmul(name, mode, grid, a, a_spec, b, b_spec, out_shape, out_spec, acc_shape, hosted=()):
    nk = grid[2]

    def body(a_ref, b_ref, o_ref, acc_ref):
        kk = pl.program_id(2)
        bv = b_ref[...]
        if bv.ndim == 3:
            bv = bv.reshape(bv.shape[0] * bv.shape[1], bv.shape[2])
        prod = lax.dot_general(a_ref[...].astype(MXU), bv.astype(MXU), _DN[mode], preferred_element_type=F32)
        if nk == 1:
            o_ref[...] = prod.astype(o_ref.dtype)
        else:
            @pl.when(kk == 0)
            def _():
                acc_ref[...] = prod

            @pl.when(kk > 0)
            def _():
                acc_ref[...] += prod

            @pl.when(kk == nk - 1)
            def _():
                o_ref[...] = acc_ref[...].astype(o_ref.dtype)

    outs, passed = _hosting_call(body, name, grid, [a_spec, b_spec], [out_spec], [out_shape], [pltpu.VMEM(acc_shape, F32)],
                                 [a, b], hosted, ("parallel", "parallel", "arbitrary"))
    return (outs[0], passed) if hosted else outs[0]


def _rows_call(name, body, ins, in_rows, outs, out_rows, n_rows, tr):
    def spec(shape, tiled):
        if tiled:
            return pl.BlockSpec((tr, shape[1]), lambda i: (i, 0))
        return pl.BlockSpec(shape, lambda i: (0, 0))

    return pl.pallas_call(
        body, name=name, grid=(n_rows // tr,),
        in_specs=[spec(a.shape, t) for a, t in zip(ins, in_rows)],
        out_specs=[spec(o.shape, t) for o, t in zip(outs, out_rows)],
        out_shape=outs, compiler_params=_cp(("arbitrary",)),
    )(*ins)


def _rstd(v):
    return lax.rsqrt(jnp.mean(v * v, axis=-1, keepdims=True) + RMS_EPS)


def _norm_bwd(v, g, dy):
    r = _rstd(v)
    n = v * r
    dn = dy * g
    dv = r * (dn - n * jnp.mean(dn * n, axis=-1, keepdims=True))
    return dv, jnp.sum(dy * n, axis=0, keepdims=True)


def _acc_rows(ref, val):
    @pl.when(pl.program_id(0) == 0)
    def _():
        ref[...] = val

    @pl.when(pl.program_id(0) > 0)
    def _():
        ref[...] += val


def rms_fwd(x, g, tr):
    def body(x_ref, g_ref, o_ref):
        v = x_ref[...]
        o_ref[...] = (v * _rstd(v) * g_ref[...]).astype(o_ref.dtype)

    return _rows_call("rms_fwd", body, [x, g], [True, False], [jax.ShapeDtypeStruct(x.shape, MXU)], [True], x.shape[0], tr)[0]


def res_rms_fwd(x, o, g, g_next, tr):
    def body(x_ref, o_ref, g_ref, gn_ref, y_ref, h_ref):
        v = o_ref[...].astype(F32)
        y = x_ref[...] + v * _rstd(v) * g_ref[...]
        y_ref[...] = y
        h_ref[...] = (y * _rstd(y) * gn_ref[...]).astype(h_ref.dtype)

    return _rows_call("res_rms_fwd", body, [x, o, g, g_next], [True, True, False, False],
                      [jax.ShapeDtypeStruct(x.shape, F32), jax.ShapeDtypeStruct(x.shape, MXU)], [True, True], x.shape[0], tr)


def ple_fwd(x1, gpre, e, g, g_next, tr):
    def body(x_ref, gp_ref, e_ref, g_ref, gn_ref, y_ref, h_ref):
        v = e_ref[...].astype(F32) * _sig(gp_ref[...].astype(F32))
        y = x_ref[...] + v * _rstd(v) * g_ref[...]
        y_ref[...] = y
        h_ref[...] = (y * _rstd(y) * gn_ref[...]).astype(h_ref.dtype)

    return _rows_call("ple_fwd", body, [x1, gpre, e, g, g_next], [True, True, True, False, False],
                      [jax.ShapeDtypeStruct(x1.shape, F32), jax.ShapeDtypeStruct(x1.shape, MXU)], [True, True], x1.shape[0], tr)


def ple_fwd_last(x1, gpre, e, g, tr):
    def body(x_ref, gp_ref, e_ref, g_ref, y_ref):
        v = e_ref[...].astype(F32) * _sig(gp_ref[...].astype(F32))
        y_ref[...] = x_ref[...] + v * _rstd(v) * g_ref[...]

    return _rows_call("ple_fwd_last", body, [x1, gpre, e, g], [True, True, True, False],
                      [jax.ShapeDtypeStruct(x1.shape, F32)], [True], x1.shape[0], tr)[0]


def loss_bwd(y, target, tr):
    d = y.shape[1]

    def body(y_ref, t_ref, dy_ref, sq_ref):
        diff = y_ref[...] - t_ref[...]
        dy_ref[...] = diff * (1.0 / d)
        _acc_rows(sq_ref, jnp.sum(diff * diff, axis=0, keepdims=True))

    return _rows_call("loss_bwd", body, [y, target], [True, True],
                      [jax.ShapeDtypeStruct(y.shape, F32), jax.ShapeDtypeStruct((1, d), F32)], [True, False], y.shape[0], tr)


def ple_bwd(dx2, gpre, e, g, tr):
    d = dx2.shape[1]

    def body(dx_ref, gp_ref, e_ref, g_ref, de_ref, dgp_ref, dg_ref):
        gate = _sig(gp_ref[...].astype(F32))
        ev = e_ref[...].astype(F32)
        dv, dg = _norm_bwd(ev * gate, g_ref[...], dx_ref[...])
        de_ref[...] = (dv * gate).astype(de_ref.dtype)
        dgp_ref[...] = (dv * ev * gate * (1.0 - gate)).astype(dgp_ref.dtype)
        _acc_rows(dg_ref, dg)

    return _rows_call("ple_bwd", body, [dx2, gpre, e, g], [True, True, True, False],
                      [jax.ShapeDtypeStruct(dx2.shape, MXU), jax.ShapeDtypeStruct(dx2.shape, MXU), jax.ShapeDtypeStruct((1, d), F32)],
                      [True, True, False], dx2.shape[0], tr)


def rms_bwd_res(dres, dh, x, g, tr):
    d = x.shape[1]

    def body(dr_ref, dh_ref, x_ref, g_ref, dx_ref, dg_ref):
        dv, dg = _norm_bwd(x_ref[...], g_ref[...], dh_ref[...].astype(F32))
        dx_ref[...] = dr_ref[...] + dv
        _acc_rows(dg_ref, dg)

    return _rows_call("rms_bwd_res", body, [dres, dh, x, g], [True, True, True, False],
                      [jax.ShapeDtypeStruct(x.shape, F32), jax.ShapeDtypeStruct((1, d), F32)], [True, False], x.shape[0], tr)


def post_bwd(dres, dh, x, g, o, g_o, tr):
    d = x.shape[1]

    def body(dr_ref, dh_ref, x_ref, g_ref, o_ref, go_ref, dx_ref, do_ref, dg_ref, dgo_ref):
        dv, dg = _norm_bwd(x_ref[...], g_ref[...], dh_ref[...].astype(F32))
        dxv = dr_ref[...] + dv
        dx_ref[...] = dxv
        dov, dgo = _norm_bwd(o_ref[...].astype(F32), go_ref[...], dxv)
        do_ref[...] = dov.astype(do_ref.dtype)
        _acc_rows(dg_ref, dg)
        _acc_rows(dgo_ref, dgo)

    return _rows_call("post_bwd", body, [dres, dh, x, g, o, g_o], [True, True, True, False, True, False],
                      [jax.ShapeDtypeStruct(x.shape, F32), jax.ShapeDtypeStruct(x.shape, MXU),
                       jax.ShapeDtypeStruct((1, d), F32), jax.ShapeDtypeStruct((1, d), F32)], [True, True, False, False], x.shape[0], tr)


def _trailing_sum(ext, w):
    s, k = ext, 1
    while k < w:
        s = s + pltpu.roll(s, k, 0)
        k *= 2
    return s


def _leading_sum(ext, w):
    n = ext.shape[0]
    s, k = ext, 1
    while k < w:
        s = s + pltpu.roll(s, n - k, 0)
        k *= 2
    return s


def _pool_inv_count(first, tr, w, c):
    if not first:
        return 1.0 / w
    t = lax.broadcasted_iota(jnp.int32, (tr, c), 0)
    return 1.0 / jnp.minimum(t + 1, w).astype(F32)


def _pool_d(u_ref, up_ref, first, tr, w):
    cur = u_ref[...].astype(F32)
    prev = jnp.zeros((MAXW, cur.shape[1]), F32) if first else up_ref[...].astype(F32)
    ext = jnp.concatenate([prev, cur], axis=0)
    win = _trailing_sum(ext, w)[MAXW:]
    return win * _pool_inv_count(first, tr, w, cur.shape[1]) - cur


def _first_or_not(rb, fn):
    @pl.when(rb == 0)
    def _():
        fn(True)

    @pl.when(rb > 0)
    def _():
        fn(False)


def pool_fwd(uz, w_g, bias, scale, tr, hosted=()):
    _, s, e = uz.shape
    ng = len(POOL_WINDOWS)
    cg = e // ng
    nb = s // tr
    hb = tr // MAXW

    def body(u_ref, up_ref, z_ref, w_ref, b_ref, sc_ref, o_ref):
        g, rb = pl.program_id(0), pl.program_id(1)
        wmat = w_ref[...].reshape(cg, cg)
        for gg, win in enumerate(POOL_WINDOWS):
            @pl.when(g == gg)
            def _(win=win):
                def compute(first):
                    d = _pool_d(u_ref, up_ref, first, tr, win)
                    y = (jnp.dot(d.astype(MXU), wmat, preferred_element_type=F32) + b_ref[...]) * sc_ref[...]
                    z = z_ref[...].astype(F32)
                    o_ref[...] = (y * z * _sig(z)).astype(o_ref.dtype)

                _first_or_not(rb, compute)

    return _hosting_call(
        body, "pool_fwd", (ng, nb),
        [
            pl.BlockSpec((None, tr, cg), lambda g, r: (0, r, g)),
            pl.BlockSpec((None, MAXW, cg), lambda g, r: (0, jnp.maximum(r * hb - 1, 0), g)),
            pl.BlockSpec((None, tr, cg), lambda g, r: (1, r, g)),
            pl.BlockSpec((4, None, cg // 4, cg), lambda g, r: (0, g, 0, 0)),
            pl.BlockSpec((1, cg), lambda g, r: (0, g)),
            pl.BlockSpec((1, cg), lambda g, r: (0, g)),
        ],
        [pl.BlockSpec((tr, cg), lambda g, r: (r, g))], [jax.ShapeDtypeStruct((s, e), MXU)], [],
        [uz, uz, uz, w_g, bias, scale], hosted, ("parallel", "arbitrary"))


def pool_bwd(uz, dy2, w_g, bias, scale, tr, hosted=()):
    _, s, e = uz.shape
    ng = len(POOL_WINDOWS)
    cg = e // ng
    nb = s // tr
    hb = tr // MAXW

    def body(u_ref, up_ref, z_ref, dy_ref, w_ref, b_ref, sc_ref, duz_ref, gw_ref, db_ref, dsc_ref, acc_ref, carry_ref):
        g, step = pl.program_id(0), pl.program_id(1)
        rb = nb - 1 - step
        wmat = w_ref[...].reshape(cg, cg)
        for gg, win in enumerate(POOL_WINDOWS):
            @pl.when(g == gg)
            def _(win=win):
                def compute(first):
                    d = _pool_d(u_ref, up_ref, first, tr, win).astype(MXU)
                    ypre = jnp.dot(d, wmat, preferred_element_type=F32) + b_ref[...]
                    z = z_ref[...].astype(F32)
                    sg = _sig(z)
                    dy2v = dy_ref[...].astype(F32)
                    dyv = dy2v * z * sg
                    duz_ref[1] = (dy2v * ypre * sc_ref[...] * sg * (1.0 + z * (1.0 - sg))).astype(duz_ref.dtype)
                    dypre = dyv * sc_ref[...]
                    dsc = jnp.sum(dyv * ypre, axis=0, keepdims=True)
                    dbv = jnp.sum(dypre, axis=0, keepdims=True)
                    dypre_b = dypre.astype(MXU)
                    dd = lax.dot_general(dypre_b, wmat, _DN["nt"], preferred_element_type=F32)
                    gw = lax.dot_general(d, dypre_b, _DN["tn"], preferred_element_type=F32)
                    q = dd * _pool_inv_count(first, tr, win, cg)
                    nxt = jnp.where(step > 0, carry_ref[...], 0.0)
                    lead = _leading_sum(jnp.concatenate([q, nxt], axis=0), win)[:tr]
                    duz_ref[0] = (lead - dd).astype(duz_ref.dtype)
                    carry_ref[...] = q[:MAXW]

                    @pl.when(step == 0)
                    def _():
                        acc_ref[...] = gw
                        db_ref[...] = dbv
                        dsc_ref[...] = dsc

                    @pl.when(step > 0)
                    def _():
                        acc_ref[...] += gw
                        db_ref[...] += dbv
                        dsc_ref[...] += dsc

                    @pl.when(step == nb - 1)
                    def _():
                        gw_ref[...] = acc_ref[...].reshape(4, cg // 4, cg).astype(gw_ref.dtype)

                _first_or_not(rb, compute)

    return _hosting_call(
        body, "pool_bwd", (ng, nb),
        [
            pl.BlockSpec((None, tr, cg), lambda g, r: (0, nb - 1 - r, g)),
            pl.BlockSpec((None, MAXW, cg), lambda g, r: (0, jnp.maximum((nb - 1 - r) * hb - 1, 0), g)),
            pl.BlockSpec((None, tr, cg), lambda g, r: (1, nb - 1 - r, g)),
            pl.BlockSpec((tr, cg), lambda g, r: (nb - 1 - r, g)),
            pl.BlockSpec((4, None, cg // 4, cg), lambda g, r: (0, g, 0, 0)),
            pl.BlockSpec((1, cg), lambda g, r: (0, g)),
            pl.BlockSpec((1, cg), lambda g, r: (0, g)),
        ],
        [
            pl.BlockSpec((2, tr, cg), lambda g, r: (0, nb - 1 - r, g)),
            pl.BlockSpec((4, None, cg // 4, cg), lambda g, r: (0, g, 0, 0)),
            pl.BlockSpec((1, cg), lambda g, r: (0, g)),
            pl.BlockSpec((1, cg), lambda g, r: (0, g)),
        ],
        [
            jax.ShapeDtypeStruct((2, s, e), MXU),
            jax.ShapeDtypeStruct(w_g.shape, WIRE),
            jax.ShapeDtypeStruct((1, e), F32),
            jax.ShapeDtypeStruct((1, e), F32),
        ],
        [pltpu.VMEM((cg, cg), F32), pltpu.VMEM((MAXW, cg), F32)],
        [uz, uz, uz, dy2, w_g, bias, scale], hosted, ("parallel", "arbitrary"))


HALO = 16


def _one_minus_sq(log_a, a):
    poly = (-2.0 * log_a) * (1.0 + log_a * (1.0 + log_a * (2.0 / 3.0)))
    return jnp.where(log_a > -0.01, poly, 1.0 - a * a)


def _softplus_neg(lam):
    t = jnp.exp(-jnp.abs(lam))
    log1p = jnp.where(t < 1e-3, t * (1.0 - t * (0.5 - t * (1.0 / 3.0))), jnp.log(1.0 + t))
    return jnp.maximum(-lam, 0.0) + log1p, _sig(-lam)


def _lru_gates(u_ref, up_ref, rb, sm_ref, wa, wx):
    cur = u_ref[...].astype(F32)
    prev = jnp.where(rb > 0, up_ref[...].astype(F32), 0.0)
    ext = jnp.concatenate([prev, cur], axis=0)
    taps = [cur] + [pltpu.roll(ext, k, 0)[HALO:] for k in range(1, CONV_W)]
    uc = sm_ref[CONV_W:CONV_W + 1, :]
    for k in range(CONV_W):
        uc = uc + taps[k] * sm_ref[CONV_W - 1 - k:CONV_W - k, :]
    ucb = uc.astype(MXU)
    r = _sig(jnp.dot(ucb, wa, preferred_element_type=F32) + sm_ref[5:6, :])
    ig = _sig(jnp.dot(ucb, wx, preferred_element_type=F32) + sm_ref[6:7, :])
    sp, sgn = _softplus_neg(sm_ref[7:8, :])
    log_a = r * (-LRU_C * sp)
    a = jnp.exp(log_a)
    mult = jnp.sqrt(jnp.maximum(_one_minus_sq(log_a, a), 0.0))
    return taps, uc, ucb, r, ig, sp, sgn, a, mult


LANES = 128


def _seg_scan(a, b, out_ref, scr, state, reverse):
    a_s, b_s, h_s, p_s = scr
    tr, c = a.shape
    seg = tr // 8
    nl = c // LANES
    for l in range(nl):
        a_s[l] = a[:, l * LANES:(l + 1) * LANES]
        b_s[l] = b[:, l * LANES:(l + 1) * LANES]
    h = [jnp.zeros((8, LANES), F32)] * nl
    pp = [jnp.ones((8, LANES), F32)] * nl
    for i in (range(seg - 1, -1, -1) if reverse else range(seg)):
        rows = pl.ds(i, 8, stride=seg)
        for l in range(nl):
            av = a_s[l, rows, :]
            h[l] = av * h[l] + b_s[l, rows, :]
            pp[l] = av * pp[l]
            h_s[l, pl.ds(8 * i, 8), :] = h[l]
            p_s[l, pl.ds(8 * i, 8), :] = pp[l]
    leaving = []
    sub = lax.broadcasted_iota(jnp.int32, (8, LANES), 0)
    for l in range(nl):
        lanes = slice(l * LANES, (l + 1) * LANES)
        st = state[:, lanes]
        entering = jnp.zeros((8, LANES), F32)
        for sgm in (range(7, -1, -1) if reverse else range(8)):
            entering = jnp.where(sub == sgm, st, entering)
            st = h[l][sgm:sgm + 1, :] + pp[l][sgm:sgm + 1, :] * st
        leaving.append(st)
        for i in range(seg):
            rows = pl.ds(8 * i, 8)
            h_s[l, rows, :] = h_s[l, rows, :] + p_s[l, rows, :] * entering
        for sgm in range(8):
            for t0 in range(0, seg, 8):
                out_ref[pl.ds(sgm * seg + t0, 8), lanes] = h_s[l, pl.ds(8 * t0 + sgm, 8, stride=8), :]
    return jnp.concatenate(leaving, axis=1)


def lru_fwd(uz, wa_g, wx_g, small, tr, hosted=()):
    _, s, e = uz.shape
    cb = e // LRU_HEADS
    nb = s // tr
    hb = tr // HALO

    def body(u_ref, up_ref, z_ref, wa_ref, wx_ref, sm_ref, o_ref, h_ref, s0, s1, s2, s3, carry_ref):
        rb = pl.program_id(1)
        wa = wa_ref[...].reshape(cb, cb)
        wx = wx_ref[...].reshape(cb, cb)
        _, uc, _, _, ig, _, _, a, mult = _lru_gates(u_ref, up_ref, rb, sm_ref, wa, wx)
        start = jnp.where(rb > 0, carry_ref[0:1, :], 0.0)
        last = _seg_scan(a, mult * ig * uc, h_ref, (s0, s1, s2, s3), start, False)
        carry_ref[...] = jnp.broadcast_to(last, carry_ref.shape)
        z = z_ref[...].astype(F32)
        o_ref[...] = (h_ref[...] * z * _sig(z)).astype(o_ref.dtype)

    wspec = pl.BlockSpec((4, None, cb // 4, cb), lambda h, r: (0, h, 0, 0))
    return _hosting_call(
        body, "lru_fwd", (LRU_HEADS, nb),
        [
            pl.BlockSpec((None, tr, cb), lambda h, r: (0, r, h)),
            pl.BlockSpec((None, HALO, cb), lambda h, r: (0, jnp.maximum(r * hb - 1, 0), h)),
            pl.BlockSpec((None, tr, cb), lambda h, r: (1, r, h)),
            wspec, wspec,
            pl.BlockSpec((8, cb), lambda h, r: (0, h)),
        ],
        [pl.BlockSpec((tr, cb), lambda h, r: (r, h)), pl.BlockSpec((tr, cb), lambda h, r: (r, h))],
        [jax.ShapeDtypeStruct((s, e), MXU), jax.ShapeDtypeStruct((s, e), F32)],
        [pltpu.VMEM((cb // LANES, tr, LANES), F32)] * 4 + [pltpu.VMEM((8, cb), F32)],
        [uz, uz, uz, wa_g, wx_g, small], hosted, ("parallel", "arbitrary"))


def lru_bwd(uz, hst, dy2, wa_g, wx_g, small, tr, hosted=()):
    _, s, e = uz.shape
    cb = e // LRU_HEADS
    nb = s // tr
    hb = tr // HALO

    def body(u_ref, up_ref, z_ref, h_ref, hp_ref, dy_ref, wa_ref, wx_ref, sm_ref,
             duz_ref, gwa_ref, gwx_ref, dsm_ref, s0, s1, s2, s3, g_s, acc_a, acc_x, gcar, acar, dcar):
        step = pl.program_id(1)
        rb = nb - 1 - step
        wa = wa_ref[...].reshape(cb, cb)
        wx = wx_ref[...].reshape(cb, cb)
        taps, uc, ucb, r, ig, sp, sgn, a, mult = _lru_gates(u_ref, up_ref, rb, sm_ref, wa, wx)
        row = lax.broadcasted_iota(jnp.int32, a.shape, 0)
        z = z_ref[...].astype(F32)
        sg = _sig(z)
        dy2v = dy_ref[...].astype(F32)
        hv = h_ref[...]
        duz_ref[1] = (dy2v * hv * sg * (1.0 + z * (1.0 - sg))).astype(duz_ref.dtype)
        a_next = jnp.where(row == tr - 1, jnp.where(step > 0, acar[0:1, :], 0.0), pltpu.roll(a, tr - 1, 0))
        g_first = _seg_scan(a_next, dy2v * z * sg, g_s, (s0, s1, s2, s3), jnp.where(step > 0, gcar[0:1, :], 0.0), True)
        gcar[...] = jnp.broadcast_to(g_first, gcar.shape)
        acar[...] = jnp.broadcast_to(a[0:1, :], acar.shape)
        gv = g_s[...]
        h_before = jnp.where(rb > 0, hp_ref[HALO - 1:HALO, :], 0.0)
        h_prev = jnp.where(row == 0, h_before, pltpu.roll(hv, 1, 0))
        da = gv * h_prev
        gu = gv * uc
        dmult = gu * ig
        dig = gu * mult
        dlog_a = da * a - dmult * jnp.where(mult > 0.0, a * a / mult, 0.0)
        dra = dlog_a * (-LRU_C) * sp * r * (1.0 - r)
        dix = dig * ig * (1.0 - ig)
        dl = jnp.sum(dlog_a * r, axis=0, keepdims=True) * (LRU_C * sgn)
        dra_b, dix_b = dra.astype(MXU), dix.astype(MXU)
        duc = (gv * mult * ig + lax.dot_general(dra_b, wa, _DN["nt"], preferred_element_type=F32)
               + lax.dot_general(dix_b, wx, _DN["nt"], preferred_element_type=F32))
        gwa = lax.dot_general(ucb, dra_b, _DN["tn"], preferred_element_type=F32)
        gwx = lax.dot_general(ucb, dix_b, _DN["tn"], preferred_element_type=F32)
        ext = jnp.concatenate([duc, jnp.where(step > 0, dcar[...], 0.0)], axis=0)
        n = ext.shape[0]
        du = duc * sm_ref[CONV_W - 1:CONV_W, :]
        for k in range(1, CONV_W):
            du = du + pltpu.roll(ext, n - k, 0)[:tr] * sm_ref[CONV_W - 1 - k:CONV_W - k, :]
        duz_ref[0] = du.astype(duz_ref.dtype)
        dcar[...] = duc[:HALO]
        rows = [jnp.sum(duc * taps[CONV_W - 1 - k], axis=0, keepdims=True) for k in range(CONV_W)]
        rows += [jnp.sum(duc, axis=0, keepdims=True), jnp.sum(dra, axis=0, keepdims=True),
                 jnp.sum(dix, axis=0, keepdims=True), dl]

        @pl.when(step == 0)
        def _():
            acc_a[...] = gwa
            acc_x[...] = gwx
            for k, rv in enumerate(rows):
                dsm_ref[k:k + 1, :] = rv

        @pl.when(step > 0)
        def _():
            acc_a[...] += gwa
            acc_x[...] += gwx
            for k, rv in enumerate(rows):
                dsm_ref[k:k + 1, :] += rv

        @pl.when(step == nb - 1)
        def _():
            gwa_ref[...] = acc_a[...].reshape(4, cb // 4, cb).astype(gwa_ref.dtype)
            gwx_ref[...] = acc_x[...].reshape(4, cb // 4, cb).astype(gwx_ref.dtype)

    wspec = pl.BlockSpec((4, None, cb // 4, cb), lambda h, r: (0, h, 0, 0))
    blk = pl.BlockSpec((tr, cb), lambda h, r: (nb - 1 - r, h))
    return _hosting_call(
        body, "lru_bwd", (LRU_HEADS, nb),
        [
            pl.BlockSpec((None, tr, cb), lambda h, r: (0, nb - 1 - r, h)),
            pl.BlockSpec((None, HALO, cb), lambda h, r: (0, jnp.maximum((nb - 1 - r) * hb - 1, 0), h)),
            pl.BlockSpec((None, tr, cb), lambda h, r: (1, nb - 1 - r, h)),
            blk,
            pl.BlockSpec((HALO, cb), lambda h, r: (jnp.maximum((nb - 1 - r) * hb - 1, 0), h)),
            blk,
            wspec, wspec,
            pl.BlockSpec((8, cb), lambda h, r: (0, h)),
        ],
        [
            pl.BlockSpec((2, tr, cb), lambda h, r: (0, nb - 1 - r, h)),
            wspec, wspec,
            pl.BlockSpec((8, cb), lambda h, r: (0, h)),
        ],
        [
            jax.ShapeDtypeStruct((2, s, e), MXU),
            jax.ShapeDtypeStruct(wa_g.shape, WIRE),
            jax.ShapeDtypeStruct(wx_g.shape, WIRE),
            jax.ShapeDtypeStruct((8, e), F32),
        ],
        ([pltpu.VMEM((cb // LANES, tr, LANES), F32)] * 4 + [pltpu.VMEM((tr, cb), F32)]
         + [pltpu.VMEM((cb, cb), F32)] * 2 + [pltpu.VMEM((8, cb), F32)] * 2 + [pltpu.VMEM((HALO, cb), F32)]),
        [uz, uz, uz, hst, hst, dy2, wa_g, wx_g, small], hosted, ("parallel", "arbitrary"))


def _place():
    x, y, c = lax.axis_index("x"), lax.axis_index("y"), lax.axis_index("c")
    chips = [(1 - x, y), (x, 1 - y), (1 - x, 1 - y)]
    return x, y, c, chips


def _rcopy(src, dst, send_sems, recv_sems, k, to):
    return pltpu.make_async_remote_copy(src_ref=src, dst_ref=dst, send_sem=send_sems.at[k], recv_sem=recv_sems.at[k],
                                        device_id=to, device_id_type=MESHID)


def _stage_gather_ici(bufs):
    n = len(bufs)

    def quarter(ref, chip, c, q):
        rq = ref.shape[3] // 2
        return ref.at[2 * chip[0] + chip[1], :, c, pl.ds(q * rq, rq)]

    def copies(refs, ss, rs, off, a, sending):
        x, y, c, _ = _place()
        me, xn, yn, dg = (x, y), (1 - x, y), (x, 1 - y), (1 - x, 1 - y)
        plan = [(0, me, xn, 0), (1, me, xn, 1), (3, me, yn, 1), (2, me, yn, 0),
                (4, xn, yn, 0), (5, yn, xn, 1)]
        if not sending:
            plan = [(0, xn, xn, 0), (1, xn, xn, 1), (3, yn, yn, 1), (2, yn, yn, 0), (4, dg, yn, 0), (5, dg, xn, 1)]
        out = []
        for k, owner, to, q in plan:
            blk = quarter(refs[a], owner, c, q)
            out.append(_rcopy(blk, blk, ss, rs, off + 6 * a + k, (*to, c)))
        return out

    def start(ro, refs, ss, rs, off):
        for a in range(n):
            for cp in copies(refs, ss, rs, off, a, True)[:4]:
                cp.start()

    def mid(ro, refs, ss, rs, off):
        for a in range(n):
            got, out = copies(refs, ss, rs, off, a, False), copies(refs, ss, rs, off, a, True)
            got[0].wait_recv()
            out[4].start()
            got[2].wait_recv()
            out[5].start()

    def finish(ro, refs, ss, rs, off):
        for a in range(n):
            got = copies(refs, ss, rs, off, a, False)
            for k in (1, 3, 4, 5):
                got[k].wait_recv()
            for cp in copies(refs, ss, rs, off, a, True):
                cp.wait_send()

    return dict(ro=[], bufs=list(bufs), nsem=6 * n, start=start, mid=mid, finish=finish)


def _stage_gather_d2d(bufs):
    n = len(bufs)

    def copies(refs, ss, rs, off, sending):
        x, y, c, chips = _place()
        out = []
        for a in range(n):
            for jj, ch in enumerate(chips):
                blk = refs[a].at[2 * ch[0] + ch[1], :, c if sending else 1 - c]
                out.append(_rcopy(blk, blk, ss, rs, off + 3 * a + jj, (x, y, 1 - c)))
        return out

    def start(ro, refs, ss, rs, off):
        for cp in copies(refs, ss, rs, off, True):
            cp.start()

    def finish(ro, refs, ss, rs, off):
        for cp in copies(refs, ss, rs, off, False):
            cp.wait_recv()
        for cp in copies(refs, ss, rs, off, True):
            cp.wait_send()

    return dict(ro=[], bufs=list(bufs), nsem=3 * n, start=start, finish=finish)


def _stage_scatter_ici(parts, gots):
    n = len(parts)

    def copies(ro, refs, ss, rs, off):
        x, y, c, chips = _place()
        return [_rcopy(ro[a].at[2 * ch[0] + ch[1]], refs[a].at[jj], ss, rs, off + 3 * a + jj, (*ch, c))
                for a in range(n) for jj, ch in enumerate(chips)]

    def start(ro, refs, ss, rs, off):
        for cp in copies(ro, refs, ss, rs, off):
            cp.start()

    def finish(ro, refs, ss, rs, off):
        for cp in copies(ro, refs, ss, rs, off):
            cp.wait()

    return dict(ro=list(parts), bufs=list(gots), nsem=3 * n, start=start, finish=finish)


def _stage_send_half(grads, lands):
    n = len(grads)

    def copies(ro, refs, ss, rs, off):
        x, y, c, _ = _place()
        return [_rcopy(ro[a].at[:, :, 1 - c], refs[a], ss, rs, off + a, (x, y, 1 - c)) for a in range(n)]

    def start(ro, refs, ss, rs, off):
        for cp in copies(ro, refs, ss, rs, off):
            cp.start()

    def finish(ro, refs, ss, rs, off):
        for cp in copies(ro, refs, ss, rs, off):
            cp.wait()

    return dict(ro=list(grads), bufs=list(lands), nsem=n, start=start, finish=finish)


def cast_into_slab(w, first, l, k_idx):
    _, r, c = w.shape
    rh = r // 2
    lb, tr = _block_lr(l, rh, c)
    nbh = rh // tr
    assert first % lb == 0

    def body(k_ref, w_ref, o_ref):
        o_ref[...] = w_ref[...].astype(o_ref.dtype)

    return pl.pallas_call(
        body, name="cast_into_slab",
        grid_spec=pltpu.PrefetchScalarGridSpec(
            num_scalar_prefetch=1, grid=(l // lb, 2, nbh),
            in_specs=[pl.BlockSpec((lb, tr, c), lambda i, h, b, k_ref: (first // lb + i, h * nbh + b, 0))],
            out_specs=pl.BlockSpec((None, lb, None, tr, c), lambda i, h, b, k_ref: (k_ref[0], i, h, b, 0))),
        out_shape=jax.ShapeDtypeStruct((4, l, 2, rh, c), WIRE),
        compiler_params=_cp(("parallel", "parallel", "parallel")),
    )(k_idx, w)


def gather_weights(bufs):
    n = len(bufs)
    ici = [_stage_gather_ici([b]) for b in bufs]
    d2d = [_stage_gather_d2d([b]) for b in bufs]
    per = ici[0]["nsem"] + d2d[0]["nsem"]

    def body(*refs):
        outs = refs[n:2 * n]
        ss, rs = refs[2 * n:]
        for what in ("start", "mid"):
            for a in range(n):
                ici[a][what]([], [outs[a]], ss, rs, per * a)
        for a in range(n):
            ici[a]["finish"]([], [outs[a]], ss, rs, per * a)
            d2d[a]["start"]([], [outs[a]], ss, rs, per * a + ici[a]["nsem"])
        for a in range(n):
            d2d[a]["finish"]([], [outs[a]], ss, rs, per * a + ici[a]["nsem"])

    return pl.pallas_call(
        body, name="gather_weights",
        in_specs=[ANY] * n, out_specs=[ANY] * n,
        out_shape=[jax.ShapeDtypeStruct(a.shape, a.dtype) for a in bufs],
        scratch_shapes=[pltpu.SemaphoreType.DMA((per * n,)), pltpu.SemaphoreType.DMA((per * n,))],
        input_output_aliases={a: a for a in range(n)},
        compiler_params=pltpu.CompilerParams(has_side_effects=True),
    )(*bufs)


def all_gather_small(v, name):
    m_per, n = v.shape

    def body(x_ref, out_ref, send_sems, recv_sems, local_sem):
        x, y, c, chips = _place()
        me, sibling = (x, y, c), (x, y, 1 - c)

        def rows(px, py, pc):
            return out_ref.at[pl.ds((4 * px + 2 * py + pc) * m_per, m_per), :]

        def copy(k, block, to, src=None):
            return _rcopy(rows(*block) if src is None else src, rows(*block), send_sems, recv_sems, k, to)

        mine = pltpu.make_async_copy(x_ref, rows(*me), local_sem)
        mine.start()
        first = [copy(0, me, sibling, src=x_ref)]
        first += [copy(1 + jj, me, (*chip, c), src=x_ref) for jj, chip in enumerate(chips)]
        for cp in first:
            cp.start()
        passed = [copy(4 + jj, (*chip, c), sibling) for jj, chip in enumerate(chips)]
        for jj, chip in enumerate(chips):
            copy(1 + jj, (*chip, c), me).wait_recv()
            passed[jj].start()
        copy(0, sibling, me).wait_recv()
        for jj, chip in enumerate(chips):
            copy(4 + jj, (*chip, 1 - c), me).wait_recv()
        for cp in first + passed:
            cp.wait_send()
        mine.wait()

    return pl.pallas_call(
        body, name=name,
        out_shape=jax.ShapeDtypeStruct((8 * m_per, n), v.dtype),
        in_specs=[pl.BlockSpec(memory_space=pltpu.VMEM)],
        out_specs=pl.BlockSpec(memory_space=pltpu.VMEM),
        scratch_shapes=[pltpu.SemaphoreType.DMA((7,)), pltpu.SemaphoreType.DMA((7,)), pltpu.SemaphoreType.DMA],
        compiler_params=pltpu.CompilerParams(vmem_limit_bytes=VMEM_LIMIT),
    )(v)


def sum_devices(g):
    def body(g_ref, o_ref):
        acc = g_ref[0]
        for d in range(1, 8):
            acc = acc + g_ref[d]
        o_ref[...] = acc

    return pl.pallas_call(body, name="sum_devices", out_shape=jax.ShapeDtypeStruct(g.shape[1:], g.dtype),
                          compiler_params=pltpu.CompilerParams(vmem_limit_bytes=VMEM_LIMIT))(g)


def send_other_half(grads):
    n = len(grads)

    def body(*refs):
        ins, outs = refs[:n], refs[n:2 * n]
        send_sems, recv_sems = refs[2 * n:]
        x, y, c, _ = _place()
        sib = (x, y, 1 - c)
        cps = [_rcopy(ins[a].at[:, :, 1 - c], outs[a], send_sems, recv_sems, a, sib) for a in range(n)]
        for cp in cps:
            cp.start()
        for cp in cps:
            cp.wait()

    return pl.pallas_call(
        body, name="send_other_half", in_specs=[ANY] * n, out_specs=[ANY] * n,
        out_shape=[jax.ShapeDtypeStruct(g.shape[:2] + g.shape[3:], g.dtype) for g in grads],
        scratch_shapes=[pltpu.SemaphoreType.DMA((n,)), pltpu.SemaphoreType.DMA((n,))],
        compiler_params=pltpu.CompilerParams(has_side_effects=True),
    )(*grads)


def scatter_to_chips(parts, fulls, spans):
    n, m = len(parts), len(fulls)

    def body(*refs):
        ins, outs, shared = refs[:n], refs[n + m:2 * n + m], refs[2 * n + m:2 * n + 2 * m]
        send_sems, recv_sems = refs[2 * n + 2 * m:]
        x, y, c, chips = _place()
        sib = (x, y, 1 - c)
        cps = []
        for a in range(n):
            for jj, ch in enumerate(chips):
                cps.append(_rcopy(ins[a].at[2 * ch[0] + ch[1]], outs[a].at[jj], send_sems, recv_sems, 3 * a + jj, (*ch, c)))

        def blk(a, half):
            return shared[a].at[pl.ds(spans[a][0], spans[a][1]), half]

        swaps = [_rcopy(blk(a, c), blk(a, c), send_sems, recv_sems, 3 * n + a, sib) for a in range(m)]
        for cp in cps + swaps:
            cp.start()
        for cp in cps:
            cp.wait()
        for a in range(m):
            _rcopy(blk(a, 1 - c), blk(a, 1 - c), send_sems, recv_sems, 3 * n + a, sib).wait_recv()
        for cp in swaps:
            cp.wait_send()

    res = pl.pallas_call(
        body, name="scatter_to_chips", in_specs=[ANY] * (n + m), out_specs=[ANY] * (n + m),
        out_shape=[jax.ShapeDtypeStruct((3,) + p.shape[1:], p.dtype) for p in parts]
        + [jax.ShapeDtypeStruct(f.shape, f.dtype) for f in fulls],
        scratch_shapes=[pltpu.SemaphoreType.DMA((3 * n + m,)), pltpu.SemaphoreType.DMA((3 * n + m,))],
        input_output_aliases={n + a: n + a for a in range(m)},
        compiler_params=pltpu.CompilerParams(has_side_effects=True),
    )(*parts, *fulls)
    return list(res[:n]), list(res[n:])


def share_halves(bufs, spans):
    n = len(bufs)

    def body(*refs):
        outs = refs[n:2 * n]
        send_sems, recv_sems = refs[2 * n:]
        x, y, c, _ = _place()
        sib = (x, y, 1 - c)

        def blk(a, half):
            return outs[a].at[pl.ds(spans[a][0], spans[a][1]), half]

        cps = [_rcopy(blk(a, c), blk(a, c), send_sems, recv_sems, a, sib) for a in range(n)]
        for cp in cps:
            cp.start()
        for a in range(n):
            _rcopy(blk(a, 1 - c), blk(a, 1 - c), send_sems, recv_sems, a, sib).wait_recv()
        for cp in cps:
            cp.wait_send()

    return pl.pallas_call(
        body, name="share_halves", in_specs=[ANY] * n, out_specs=[ANY] * n,
        out_shape=[jax.ShapeDtypeStruct(b.shape, b.dtype) for b in bufs],
        scratch_shapes=[pltpu.SemaphoreType.DMA((n,)), pltpu.SemaphoreType.DMA((n,))],
        input_output_aliases={a: a for a in range(n)},
        compiler_params=pltpu.CompilerParams(has_side_effects=True),
    )(*bufs)


def _block_rows(r, c, itemsize, budget=1 << 20):
    tr = r
    while tr * c * itemsize > budget and tr % 16 == 0:
        tr //= 2
    return tr


def _block_lr(l, r, c, budget=4 << 20):
    tr = _block_rows(r, c, 4, budget)
    lb = 1
    if tr == r:
        while l % (2 * lb) == 0 and 2 * lb * r * c * 4 <= budget:
            lb *= 2
    return lb, tr


def add_halves(g, got, c_idx):
    k4, l, _, rh, cc = g.shape
    lb, tr = _block_lr(l, rh, cc)

    def body(c_ref, g_ref, r_ref, o_ref):
        o_ref[...] = (g_ref[...].astype(F32) + r_ref[...].astype(F32)).astype(o_ref.dtype)

    return pl.pallas_call(
        body, name="add_halves",
        grid_spec=pltpu.PrefetchScalarGridSpec(
            num_scalar_prefetch=1, grid=(k4, l // lb, rh // tr),
            in_specs=[pl.BlockSpec((None, lb, None, tr, cc), lambda k, i, b, c_ref: (k, i, c_ref[0], b, 0)),
                      pl.BlockSpec((None, lb, tr, cc), lambda k, i, b, c_ref: (k, i, b, 0))],
            out_specs=pl.BlockSpec((None, lb, tr, cc), lambda k, i, b, c_ref: (k, i, b, 0))),
        out_shape=jax.ShapeDtypeStruct(got.shape, WIRE),
        compiler_params=_cp(("parallel", "parallel", "parallel")),
    )(c_idx, g, got)


def sum_chips(own, got, kc_idx, full, first):
    _, l, rh, cc = own.shape
    lb, tr = _block_lr(l, rh, cc, 2 << 20)
    assert first % lb == 0

    def body(k_ref, o_ref, r_ref, _full, s_ref):
        s_ref[...] = ((o_ref[...].astype(F32) + r_ref[0].astype(F32)) + r_ref[1].astype(F32)) + r_ref[2].astype(F32)

    return pl.pallas_call(
        body, name="sum_chips",
        grid_spec=pltpu.PrefetchScalarGridSpec(
            num_scalar_prefetch=1, grid=(l // lb, rh // tr),
            in_specs=[pl.BlockSpec((None, lb, tr, cc), lambda i, b, k_ref: (k_ref[0], i, b, 0)),
                      pl.BlockSpec((3, lb, tr, cc), lambda i, b, k_ref: (0, i, b, 0)),
                      ANY],
            out_specs=pl.BlockSpec((lb, None, tr, cc), lambda i, b, k_ref: (first // lb + i, k_ref[1], b, 0))),
        out_shape=jax.ShapeDtypeStruct(full.shape, F32),
        input_output_aliases={3: 0},
        compiler_params=_cp(("parallel", "parallel")),
    )(kc_idx, own, got, full)


def _adam_math(w, g, m, v):
    m = ADAM_B1 * m + (1.0 - ADAM_B1) * g
    v = ADAM_B2 * v + (1.0 - ADAM_B2) * (g * g)
    m_hat = m / (1.0 - ADAM_B1 ** ADAM_STEP)
    v_hat = v / (1.0 - ADAM_B2 ** ADAM_STEP)
    delta = -ADAM_LR * (m_hat / (jnp.sqrt(v_hat) + ADAM_EPS) + ADAM_WD * w)
    return delta, m, v


def adamw(w, g, g_first, m, v, lo, hi, prev=None, hosted=()):
    l, r, c = w.shape
    tr = _block_rows(r, c, 4, 2 << 20)
    prev = list(prev or [])

    def body(w_ref, g_ref, m_ref, v_ref, *rest):
        go_ref, d_ref, mo_ref, vo_ref = rest[len(prev):]
        gv = g_ref[...]
        go_ref[...] = gv
        d_ref[...], mo_ref[...], vo_ref[...] = _adam_math(w_ref[...], gv, m_ref[...], v_ref[...])

    spec = pl.BlockSpec((None, tr, c), lambda i, b: (lo + i, b, 0))
    gspec = pl.BlockSpec((None, tr, c), lambda i, b: (g_first + i, b, 0))
    return _hosting_call(
        body, "adamw", (hi - lo, r // tr), [spec, gspec, spec, spec] + [ANY] * len(prev), [spec] * 4,
        [jax.ShapeDtypeStruct(w.shape, F32)] * 4, [], [w, g, m, v] + prev, hosted, ("parallel", "parallel"),
        aliases={4 + t: t for t in range(len(prev))})


def adamw_small(w, g, m, v):
    def body(w_ref, g_ref, m_ref, v_ref, d_ref, mo_ref, vo_ref):
        d_ref[...], mo_ref[...], vo_ref[...] = _adam_math(w_ref[...], g_ref[...], m_ref[...], v_ref[...])

    return pl.pallas_call(body, name="adamw_small", out_shape=[jax.ShapeDtypeStruct(w.shape, F32)] * 3)(w, g, m, v)


def kernel(x, p, w_in, w_out, g_pre, g_post, pool_w, pool_b, pool_scale, conv_w, conv_b, lru_wa, lru_ba, lru_wx, lru_bx, lru_L, w_ple, w_ple_gate, g_ple_in, g_ple_out, loss_target, m_w_in, m_w_out, m_g_pre, m_g_post, m_pool_w, m_pool_b, m_pool_scale, m_conv_w, m_conv_b, m_lru_wa, m_lru_ba, m_lru_wx, m_lru_bx, m_lru_L, m_w_ple, m_w_ple_gate, m_g_ple_in, m_g_ple_out, v_w_in, v_w_out, v_g_pre, v_g_post, v_pool_w, v_pool_b, v_pool_scale, v_conv_w, v_conv_b, v_lru_wa, v_lru_ba, v_lru_wx, v_lru_bx, v_lru_L, v_w_ple, v_w_ple_gate, v_g_ple_in, v_g_ple_out):
    depth = w_in.shape[0]
    _, s, d = x.shape
    e = 2 * d
    kp = p.shape[-1]
    nmix = pool_w.shape[0]
    ngrp = pool_w.shape[1]
    cg = e // ngrp
    cb = e // LRU_HEADS
    xi, yi, ci = lax.axis_index("x"), lax.axis_index("y"), lax.axis_index("c")
    me = 2 * xi + yi
    c_idx = jnp.reshape(ci, (1,)).astype(jnp.int32)
    k_idx = jnp.reshape(me, (1,)).astype(jnp.int32)
    tr_row = _tile(s, 256)
    tr_mix = _tile(s, 512)
    tm = _tile(s, 1024)
    tm2 = _tile(s, 2048)

    def halves(a):
        return a.reshape(a.shape[0], 2, a.shape[1] // 2, a.shape[2])

    big = {
        "w_in": w_in, "w_out": w_out, "gate": w_ple_gate, "ple": w_ple,
        "pool": pool_w.reshape(nmix * ngrp, cg // 4, cg),
        "wa": lru_wa.reshape(nmix * LRU_HEADS, cb // 4, cb), "wx": lru_wx.reshape(nmix * LRU_HEADS, cb // 4, cb),
    }
    names = list(big)

    def layer_shards(i):
        sh = {n: (big[n], i, 1) for n in ("w_in", "w_out", "gate", "ple")}
        if i % 2 == 0:
            sh["pool"] = (big["pool"], (i // 2) * ngrp, ngrp)
        else:
            sh["wa"], sh["wx"] = (big["wa"], (i // 2) * LRU_HEADS, LRU_HEADS), (big["wx"], (i // 2) * LRU_HEADS, LRU_HEADS)
        return sh

    def mixer_names(i):
        return ["pool"] if i % 2 == 0 else ["wa", "wx"]

    wbuf = [{n: cast_into_slab(*wfl, k_idx) for n, wfl in layer_shards(i).items()} for i in range(depth)]
    first = list(wbuf[0])
    wbuf[0] = dict(zip(first, gather_weights([wbuf[0][n] for n in first])))

    def full_w(i, n):
        b = wbuf[i][n]
        return b.reshape(b.shape[0], b.shape[1], 2 * b.shape[3], b.shape[4])

    def stages_of(specs):
        return [mk([wbuf[l][n] for n in nms]) for mk, l, nms in specs]

    def keep(specs, new):
        for (l, n), b in zip([(l, n) for _, l, nms in specs for n in nms], new):
            wbuf[l][n] = b

    def run_mm(specs, *args):
        if not specs:
            return _matmul(*args)
        out, new = _matmul(*args, hosted=stages_of(specs))
        keep(specs, new)
        return out

    ec = e // 4
    small_loc = jnp.concatenate([conv_w, conv_b[:, None], lru_ba[:, None], lru_bx[:, None], lru_L[:, None]], axis=1)
    sm_all = all_gather_small(small_loc.reshape(nmix * 8, ec), "gather_small").reshape(4, 2, nmix, 8, ec)
    lru_small = jnp.transpose(sm_all[:, 0], (1, 2, 0, 3)).reshape(nmix, 8, e)

    xs = x[0]
    saved = []
    for i in range(depth):
        j = i // 2
        h = rms_fwd(xs, g_pre[i][None], tr_row) if i == 0 else h_next
        nj = (2 * e) // 1024 if (2 * e) % 1024 == 0 else 4
        tn = (2 * e) // nj
        per = e // tn
        perk = (e // 2) // tn
        nxt = i + 1 if i + 1 < depth else None
        stages = [(_stage_gather_d2d, i, ["gate", "ple"])] if i > 0 else []
        if nxt is not None:
            stages.append((_stage_gather_ici, nxt, ["w_in"]))
        uz = run_mm(
            stages, "mm_in", "nn", (s // tm2, nj, 1), h, pl.BlockSpec((tm2, d), lambda a, b, k: (a, 0)),
            full_w(i, "w_in"), pl.BlockSpec((None, None, d, tn), lambda a, b, k, perk=perk: (b // perk, 0, 0, b % perk)),
            jax.ShapeDtypeStruct((2, s, e), MXU), pl.BlockSpec((None, tm2, tn), lambda a, b, k, per=per: (b // per, a, b % per)),
            (8, 128))
        specs = []
        if nxt is not None:
            specs = [(_stage_gather_ici, nxt, mixer_names(nxt) + (["w_out"] if i % 2 else []))]
        if i % 2 == 0:
            (y2,), new = pool_fwd(uz, full_w(i, "pool"), pool_b[j][None], pool_scale[j][None], tr_mix, hosted=stages_of(specs))
            hst = None
        else:
            (y2, hst), new = lru_fwd(uz, full_w(i, "wa"), full_w(i, "wx"), lru_small[j], tr_mix, hosted=stages_of(specs))
        keep(specs, new)
        tn_o = _tile(d, 1024)
        stages = []
        if nxt is not None:
            stages = [(_stage_gather_ici, nxt, ["w_out"] if i % 2 == 0 else ["gate", "ple"])]
        o = run_mm(
            stages, "mm_out", "nn", (s // tm, d // tn_o, 1), y2, pl.BlockSpec((tm, e), lambda a, b, k: (a, 0)),
            full_w(i, "w_out"), pl.BlockSpec((4, None, e // 4, tn_o), lambda a, b, k: (0, 0, 0, b)),
            jax.ShapeDtypeStruct((s, d), MXU), pl.BlockSpec((tm, tn_o), lambda a, b, k: (a, b)), (8, 128))
        x1, hn = res_rms_fwd(xs, o, g_post[i][None], g_ple_in[i][None], tr_row)
        stages = []
        if nxt is not None:
            stages = [(_stage_gather_d2d, nxt, ["w_out"] + mixer_names(nxt))]
            if i % 2 == 0:
                stages.insert(0, (_stage_gather_ici, nxt, ["gate", "ple"]))
        gpre = run_mm(
            stages, "mm_gate", "nn", (s // tm2, d // tn_o, 1), hn, pl.BlockSpec((tm2, d), lambda a, b, k: (a, 0)),
            full_w(i, "gate"), pl.BlockSpec((4, None, d // 4, tn_o), lambda a, b, k: (0, 0, 0, b)),
            jax.ShapeDtypeStruct((s, d), MXU), pl.BlockSpec((tm2, tn_o), lambda a, b, k: (a, b)), (8, 128))
        pe = p[i, 0]
        ev = run_mm(
            [(_stage_gather_d2d, nxt, ["w_in"])] if nxt is not None else [],
            "mm_ple", "nn", (s // tm, 4, 1), pe, pl.BlockSpec((tm, kp), lambda a, b, k: (a, 0)),
            full_w(i, "ple"), pl.BlockSpec((None, None, kp, d // 4), lambda a, b, k: (b, 0, 0, 0)),
            jax.ShapeDtypeStruct((s, d), MXU), pl.BlockSpec((tm, d // 4), lambda a, b, k: (a, b)), (8, 128))
        if nxt is not None:
            x2, h_next = ple_fwd(x1, gpre, ev, g_ple_out[i][None], g_pre[nxt][None], tr_row)
        else:
            x2 = ple_fwd_last(x1, gpre, ev, g_ple_out[i][None], tr_row)
        saved.append((xs, h, uz, y2, hst, o, x1, hn, gpre, ev))
        xs = x2

    dx, sq = loss_bwd(xs, loss_target[0], tr_row)
    d_gpre, d_gpost, d_gin, d_gout = [None] * depth, [None] * depth, [None] * depth, [None] * depth
    d_pool_b, d_pool_sc, d_lru_small = [None] * nmix, [None] * nmix, [None] * nmix
    ts = _tile(s, 1024)
    kc_idx = jnp.stack([me, ci]).astype(jnp.int32)
    full = {n: lax.empty(halves(big[n]).shape, F32) for n in names}
    w_sum = w_got = None

    def first_row(i, n):
        return i if n in ("w_in", "w_out", "gate", "ple") else (i // 2) * (ngrp if n == "pool" else LRU_HEADS)

    for i in reversed(range(depth)):
        j = i // 2
        prev = i + 1 if i + 1 < depth else None
        x0, h, uz, y2, hst, o, x1, hn, gpre, ev = saved[i]
        pe = p[i, 0]
        gl = {}
        de, dgp, d_gout[i] = ple_bwd(dx, gpre, ev, g_ple_out[i][None], tr_row)
        gl["ple"] = _matmul(
            "mm_dple", "tn", (1, 4, s // ts), pe, pl.BlockSpec((ts, kp), lambda a, b, k: (k, 0)),
            de, pl.BlockSpec((ts, d // 4), lambda a, b, k: (k, b)),
            jax.ShapeDtypeStruct((4, 1, kp, d // 4), WIRE), pl.BlockSpec((None, None, kp, d // 4), lambda a, b, k: (b, 0, 0, 0)),
            (kp, d // 4))
        tn_o = _tile(d, 1024)
        gl["gate"] = _matmul(
            "mm_dgate", "tn", (4, d // tn_o, 1), hn, pl.BlockSpec((s, d // 4), lambda a, b, k: (0, a)),
            dgp, pl.BlockSpec((s, tn_o), lambda a, b, k: (0, b)),
            jax.ShapeDtypeStruct((4, 1, d // 4, d), WIRE), pl.BlockSpec((None, None, d // 4, tn_o), lambda a, b, k: (a, 0, 0, b)),
            (8, 128))
        dhn = _matmul(
            "mm_dhn", "nt", (s // tm2, 4, 1), dgp, pl.BlockSpec((tm2, d), lambda a, b, k: (a, 0)),
            full_w(i, "gate"), pl.BlockSpec((None, None, d // 4, d), lambda a, b, k: (b, 0, 0, 0)),
            jax.ShapeDtypeStruct((s, d), MXU), pl.BlockSpec((tm2, d // 4), lambda a, b, k: (a, b)), (8, 128))
        dx1, do, d_gin[i], d_gpost[i] = post_bwd(dx, dhn, x1, g_ple_in[i][None], o, g_post[i][None], tr_row)
        gl["w_out"] = _matmul(
            "mm_dwout", "tn", (4, d // tn_o, 1), y2, pl.BlockSpec((s, e // 4), lambda a, b, k: (0, a)),
            do, pl.BlockSpec((s, tn_o), lambda a, b, k: (0, b)),
            jax.ShapeDtypeStruct((4, 1, e // 4, d), WIRE), pl.BlockSpec((None, None, e // 4, tn_o), lambda a, b, k: (a, 0, 0, b)),
            (8, 128))
        dy2 = _matmul(
            "mm_dy2", "nt", (s // tm2, 4, 1), do, pl.BlockSpec((tm2, d), lambda a, b, k: (a, 0)),
            full_w(i, "w_out"), pl.BlockSpec((None, None, e // 4, d), lambda a, b, k: (b, 0, 0, 0)),
            jax.ShapeDtypeStruct((s, e), MXU), pl.BlockSpec((tm2, e // 4), lambda a, b, k: (a, b)), (8, 128))
        riding = [_stage_scatter_ici([w_sum], [w_got])] if prev is not None else []
        if i % 2 == 0:
            (duz, gl["pool"], d_pool_b[j], d_pool_sc[j]), passed = pool_bwd(
                uz, dy2, full_w(i, "pool"), pool_b[j][None], pool_scale[j][None], tr_mix, hosted=riding)
        else:
            (duz, gl["wa"], gl["wx"], d_lru_small[j]), passed = lru_bwd(
                uz, hst, dy2, full_w(i, "wa"), full_w(i, "wx"), lru_small[j], tr_mix, hosted=riding)
        if prev is not None:
            full["w_in"] = sum_chips(w_sum, passed[0], kc_idx, full["w_in"], first_row(prev, "w_in"))

        def in_halves(g):
            return g.reshape(4, g.shape[1], 2, g.shape[2] // 2, g.shape[3])

        early = list(gl)
        eparts = [in_halves(gl[n]) for n in early]
        tmi = _tile(d, 1024)
        tni = _tile(e // 2, 1024)
        nslab = (e // 2) // tni
        gl["w_in"], from_sib = _matmul(
            "mm_dwin", "tn", (d // tmi, 4 * nslab, 1), h, pl.BlockSpec((s, tmi), lambda a, b, k: (0, a)),
            duz, pl.BlockSpec((None, s, tni), lambda a, b, k, nslab=nslab: (b // (2 * nslab), 0, b % (2 * nslab))),
            jax.ShapeDtypeStruct((4, 1, d, e // 2), WIRE),
            pl.BlockSpec((None, None, tmi, tni), lambda a, b, k, nslab=nslab: (b // nslab, 0, a, b % nslab)),
            (8, 128), hosted=[_stage_send_half(eparts, [lax.empty(g.shape[:2] + g.shape[3:], WIRE) for g in eparts])])
        esums = [add_halves(g, r, c_idx) for g, r in zip(eparts, from_sib)]
        wpart = in_halves(gl["w_in"])
        tnd = _tile(d, 1024)
        dh, passed = _matmul(
            "mm_dh", "nt", (s // tm, d // tnd, 4), duz, pl.BlockSpec((None, tm, e // 2), lambda a, b, k: (k // 2, a, k % 2)),
            full_w(i, "w_in"), pl.BlockSpec((None, None, tnd, e // 2), lambda a, b, k: (k, 0, b, 0)),
            jax.ShapeDtypeStruct((s, d), MXU), pl.BlockSpec((tm, tnd), lambda a, b, k: (a, b)), (tm, tnd),
            hosted=[_stage_scatter_ici(esums, [lax.empty((3,) + q.shape[1:], WIRE) for q in esums]),
                    _stage_send_half([wpart], [lax.empty(wpart.shape[:2] + wpart.shape[3:], WIRE)])])
        for n, q, got in zip(early, esums, passed[:len(early)]):
            full[n] = sum_chips(q, got, kc_idx, full[n], first_row(i, n))
        w_sum = add_halves(wpart, passed[len(early)], c_idx)
        w_got = lax.empty((3,) + w_sum.shape[1:], WIRE)
        dx, d_gpre[i] = rms_bwd_res(dx1, dh, x0, g_pre[i][None], tr_row)
    grad_x = dx[None]

    got, swapped = scatter_to_chips([w_sum], [full[n] for n in names],
                                    [(1, depth - 1) if n == "w_in" else (0, full[n].shape[0]) for n in names])
    full = dict(zip(names, swapped))
    full["w_in"] = share_halves([sum_chips(w_sum, got[0], kc_idx, full["w_in"], 0)], [(0, 1)])[0]
    grads = {n: full[n].reshape(big[n].shape) for n in names}

    def rows_e(a):
        return jnp.stack(a).reshape(-1, e) if isinstance(a, list) else a.reshape(-1, e)

    pack = [rows_e([g[0] for g in d_gpre]), rows_e([g[0] for g in d_gpost]), rows_e([g[0] for g in d_gin]), rows_e([g[0] for g in d_gout]),
            jnp.concatenate(d_pool_b, axis=0), jnp.concatenate(d_pool_sc, axis=0), jnp.concatenate(d_lru_small, axis=0),
            jnp.pad(sq, ((0, 0), (0, e - d)))]
    sizes = [a.shape[0] for a in pack]
    packed = jnp.concatenate(pack, axis=0)
    nrow = packed.shape[0]
    nrow_p = -(-nrow // 8) * 8
    packed = jnp.pad(packed, ((0, nrow_p - nrow), (0, 0)))
    total = sum_devices(all_gather_small(packed, "gather_grads").reshape(8, nrow_p, e))
    parts, off = [], 0
    for n_ in sizes:
        parts.append(total[off:off + n_])
        off += n_
    t_gpre, t_gpost, t_gin, t_gout, t_pb, t_psc, t_lru, t_sq = parts
    loss = 0.5 * jnp.sum(t_sq) / d
    t_lru = t_lru.reshape(nmix, 8, e)
    t_lru_loc = lax.dynamic_slice_in_dim(t_lru, me * ec, ec, axis=2)

    def big_update(name, w, m, v):
        shp = big[name].shape
        res4, _ = adamw(w.reshape(shp), grads[name], 0, m.reshape(shp), v.reshape(shp), 0, shp[0])
        return [a.reshape(w.shape) for a in res4]

    def small_update(w, g, m, v):
        shp = w.shape
        w2 = w.reshape(-1, shp[-1])
        dl, nm, nv = adamw_small(w2, g.reshape(w2.shape), m.reshape(w2.shape), v.reshape(w2.shape))
        return [g.reshape(shp), dl.reshape(shp), nm.reshape(shp), nv.reshape(shp)]

    res = {
        "w_in": big_update("w_in", w_in, m_w_in, v_w_in),
        "w_out": big_update("w_out", w_out, m_w_out, v_w_out),
        "g_pre": small_update(g_pre, t_gpre.reshape(depth, d), m_g_pre, v_g_pre),
        "g_post": small_update(g_post, t_gpost.reshape(depth, d), m_g_post, v_g_post),
        "pool_w": big_update("pool", pool_w, m_pool_w, v_pool_w),
        "pool_b": small_update(pool_b, t_pb, m_pool_b, v_pool_b),
        "pool_scale": small_update(pool_scale, t_psc, m_pool_scale, v_pool_scale),
        "conv_w": small_update(conv_w, t_lru_loc[:, :CONV_W], m_conv_w, v_conv_w),
        "conv_b": small_update(conv_b, t_lru_loc[:, 4], m_conv_b, v_conv_b),
        "lru_wa": big_update("wa", lru_wa, m_lru_wa, v_lru_wa),
        "lru_ba": small_update(lru_ba, t_lru_loc[:, 5], m_lru_ba, v_lru_ba),
        "lru_wx": big_update("wx", lru_wx, m_lru_wx, v_lru_wx),
        "lru_bx": small_update(lru_bx, t_lru_loc[:, 6], m_lru_bx, v_lru_bx),
        "lru_L": small_update(lru_L, t_lru_loc[:, 7], m_lru_L, v_lru_L),
        "w_ple": big_update("ple", w_ple, m_w_ple, v_w_ple),
        "w_ple_gate": big_update("gate", w_ple_gate, m_w_ple_gate, v_w_ple_gate),
        "g_ple_in": small_update(g_ple_in, t_gin.reshape(depth, d), m_g_ple_in, v_g_ple_in),
        "g_ple_out": small_update(g_ple_out, t_gout.reshape(depth, d), m_g_ple_out, v_g_ple_out),
    }
    order = ["w_in", "w_out", "g_pre", "g_post", "pool_w", "pool_b", "pool_scale", "conv_w", "conv_b", "lru_wa", "lru_ba",
             "lru_wx", "lru_bx", "lru_L", "w_ple", "w_ple_gate", "g_ple_in", "g_ple_out"]
    out = [loss, grad_x]
    for slot in range(4):
        out += [res[n][slot] for n in order]
    return tuple(out)
```

```python
import functools

import jax
import jax.numpy as jnp
from jax import lax
from jax.experimental import pallas as pl
from jax.experimental.pallas import tpu as pltpu

F32 = jnp.float32
MXU = jnp.bfloat16
WIRE = jnp.bfloat16
VMEM_LIMIT = 56 * 1024 * 1024
RMS_EPS = 1e-6
LRU_C = 8.0
POOL_WINDOWS = (2, 4, 8, 16)
MAXW = 16
CONV_W = 4
LRU_HEADS = 16
ADAM_LR, ADAM_B1, ADAM_B2, ADAM_EPS, ADAM_WD, ADAM_STEP = 0.001, 0.9, 0.999, 1e-08, 0.01, 10
MESHID = pl.DeviceIdType.MESH
ANY = pl.BlockSpec(memory_space=pl.ANY)


def _cp(sem=None):
    return pltpu.CompilerParams(dimension_semantics=sem, vmem_limit_bytes=VMEM_LIMIT)


def _sig(v):
    return 0.5 * jnp.tanh(0.5 * v) + 0.5


def _tile(n, pref):
    return pref if n % pref == 0 else n


_DN = {"nn": (((1,), (0,)), ((), ())), "nt": (((1,), (1,)), ((), ())), "tn": (((0,), (0,)), ((), ()))}


def _hosting_call(body, name, grid, in_specs, out_specs, out_shape, scratch_shapes, args, hosted, sem, aliases=None):
    n_in, n_out = len(args), len(out_shape)
    aliases = dict(aliases or {})
    if not hosted:
        res = pl.pallas_call(body, name=name, grid=grid, in_specs=in_specs, out_specs=out_specs, out_shape=out_shape,
                             scratch_shapes=scratch_shapes, input_output_aliases=aliases, compiler_params=_cp(sem))(*args)
        return list(res), []
    ro = [r for st in hosted for r in st["ro"]]
    bufs = [r for st in hosted for r in st["bufs"]]
    nro, nbuf = len(ro), len(bufs)
    nsem = sum(st["nsem"] for st in hosted)
    total = 1
    for g in grid:
        total *= g

    def wrapped(*refs):
        ins, ro_refs = refs[:n_in], refs[n_in:n_in + nro]
        outs = refs[n_in + nro + nbuf:n_in + nro + nbuf + n_out]
        buf_refs = refs[n_in + nro + nbuf + n_out:n_in + nro + 2 * nbuf + n_out]
        scr, ss, rs = refs[n_in + nro + 2 * nbuf + n_out:-2], refs[-2], refs[-1]
        step = 0
        for t, g in enumerate(grid):
            step = step * g + pl.program_id(t)

        def run_stages(what):
            r0 = b0 = s0 = 0
            for st in hosted:
                if what in st:
                    st[what](ro_refs[r0:r0 + len(st["ro"])], buf_refs[b0:b0 + len(st["bufs"])], ss, rs, s0)
                r0, b0, s0 = r0 + len(st["ro"]), b0 + len(st["bufs"]), s0 + st["nsem"]

        @pl.when(step == 0)
        def _():
            run_stages("start")

        if any("mid" in st for st in hosted):
            @pl.when(step == total // 2)
            def _():
                run_stages("mid")

        body(*ins, *outs, *scr)

        @pl.when(step == total - 1)
        def _():
            run_stages("finish")

    res = pl.pallas_call(
        wrapped, name=name, grid=grid,
        in_specs=list(in_specs) + [ANY] * (nro + nbuf), out_specs=list(out_specs) + [ANY] * nbuf,
        out_shape=list(out_shape) + [jax.ShapeDtypeStruct(x.shape, x.dtype) for x in bufs],
        scratch_shapes=list(scratch_shapes) + [pltpu.SemaphoreType.DMA((nsem,)), pltpu.SemaphoreType.DMA((nsem,))],
        input_output_aliases={**aliases, **{n_in + nro + t: n_out + t for t in range(nbuf)}},
        compiler_params=_cp(("arbitrary",) * len(grid)),
    )(*args, *ro, *bufs)
    return list(res[:n_out]), list(res[n_out:])


def _matmul(name, mode, grid, a, a_spec, b, b_spec, out_shape, out_spec, acc_shape, hosted=()):
    nk = grid[2]

    def body(a_ref, b_ref, o_ref, acc_ref):
        kk = pl.program_id(2)
        bv = b_ref[...]
        if bv.ndim == 3:
            bv = bv.reshape(bv.shape[0] * bv.shape[1], bv.shape[2])
        prod = lax.dot_general(a_ref[...].astype(MXU), bv.astype(MXU), _DN[mode], preferred_element_type=F32)
        if nk == 1:
            o_ref[...] = prod.astype(o_ref.dtype)
        else:
            @pl.when(kk == 0)
            def _():
                acc_ref[...] = prod

            @pl.when(kk > 0)
            def _():
                acc_ref[...] += prod

            @pl.when(kk == nk - 1)
            def _():
                o_ref[...] = acc_ref[...].astype(o_ref.dtype)

    outs, passed = _hosting_call(body, name, grid, [a_spec, b_spec], [out_spec], [out_shape], [pltpu.VMEM(acc_shape, F32)],
                                 [a, b], hosted, ("parallel", "parallel", "arbitrary"))
    return (outs[0], passed) if hosted else outs[0]


def _rows_call(name, body, ins, in_rows, outs, out_rows, n_rows, tr):
    def spec(shape, tiled):
        if tiled:
            return pl.BlockSpec((tr, shape[1]), lambda i: (i, 0))
        return pl.BlockSpec(shape, lambda i: (0, 0))

    return pl.pallas_call(
        body, name=name, grid=(n_rows // tr,),
        in_specs=[spec(a.shape, t) for a, t in zip(ins, in_rows)],
        out_specs=[spec(o.shape, t) for o, t in zip(outs, out_rows)],
        out_shape=outs, compiler_params=_cp(("arbitrary",)),
    )(*ins)


def _rstd(v):
    return lax.rsqrt(jnp.mean(v * v, axis=-1, keepdims=True) + RMS_EPS)


def _norm_bwd(v, g, dy):
    r = _rstd(v)
    n = v * r
    dn = dy * g
    dv = r * (dn - n * jnp.mean(dn * n, axis=-1, keepdims=True))
    return dv, jnp.sum(dy * n, axis=0, keepdims=True)


def _acc_rows(ref, val):
    @pl.when(pl.program_id(0) == 0)
    def _():
        ref[...] = val

    @pl.when(pl.program_id(0) > 0)
    def _():
        ref[...] += val


def rms_fwd(x, g, tr):
    def body(x_ref, g_ref, o_ref):
        v = x_ref[...]
        o_ref[...] = (v * _rstd(v) * g_ref[...]).astype(o_ref.dtype)

    return _rows_call("rms_fwd", body, [x, g], [True, False], [jax.ShapeDtypeStruct(x.shape, MXU)], [True], x.shape[0], tr)[0]


def res_rms_fwd(x, o, g, g_next, tr):
    def body(x_ref, o_ref, g_ref, gn_ref, y_ref, h_ref):
        v = o_ref[...].astype(F32)
        y = x_ref[...] + v * _rstd(v) * g_ref[...]
        y_ref[...] = y
        h_ref[...] = (y * _rstd(y) * gn_ref[...]).astype(h_ref.dtype)

    return _rows_call("res_rms_fwd", body, [x, o, g, g_next], [True, True, False, False],
                      [jax.ShapeDtypeStruct(x.shape, F32), jax.ShapeDtypeStruct(x.shape, MXU)], [True, True], x.shape[0], tr)


def ple_fwd(x1, gpre, e, g, g_next, tr):
    def body(x_ref, gp_ref, e_ref, g_ref, gn_ref, y_ref, h_ref):
        v = e_ref[...].astype(F32) * _sig(gp_ref[...].astype(F32))
        y = x_ref[...] + v * _rstd(v) * g_ref[...]
        y_ref[...] = y
        h_ref[...] = (y * _rstd(y) * gn_ref[...]).astype(h_ref.dtype)

    return _rows_call("ple_fwd", body, [x1, gpre, e, g, g_next], [True, True, True, False, False],
                      [jax.ShapeDtypeStruct(x1.shape, F32), jax.ShapeDtypeStruct(x1.shape, MXU)], [True, True], x1.shape[0], tr)


def ple_fwd_last(x1, gpre, e, g, tr):
    def body(x_ref, gp_ref, e_ref, g_ref, y_ref):
        v = e_ref[...].astype(F32) * _sig(gp_ref[...].astype(F32))
        y_ref[...] = x_ref[...] + v * _rstd(v) * g_ref[...]

    return _rows_call("ple_fwd_last", body, [x1, gpre, e, g], [True, True, True, False],
                      [jax.ShapeDtypeStruct(x1.shape, F32)], [True], x1.shape[0], tr)[0]


def loss_bwd(y, target, tr):
    d = y.shape[1]

    def body(y_ref, t_ref, dy_ref, sq_ref):
        diff = y_ref[...] - t_ref[...]
        dy_ref[...] = diff * (1.0 / d)
        _acc_rows(sq_ref, jnp.sum(diff * diff, axis=0, keepdims=True))

    return _rows_call("loss_bwd", body, [y, target], [True, True],
                      [jax.ShapeDtypeStruct(y.shape, F32), jax.ShapeDtypeStruct((1, d), F32)], [True, False], y.shape[0], tr)


def ple_bwd(dx2, gpre, e, g, tr):
    d = dx2.shape[1]

    def body(dx_ref, gp_ref, e_ref, g_ref, de_ref, dgp_ref, dg_ref):
        gate = _sig(gp_ref[...].astype(F32))
        ev = e_ref[...].astype(F32)
        dv, dg = _norm_bwd(ev * gate, g_ref[...], dx_ref[...])
        de_ref[...] = (dv * gate).astype(de_ref.dtype)
        dgp_ref[...] = (dv * ev * gate * (1.0 - gate)).astype(dgp_ref.dtype)
        _acc_rows(dg_ref, dg)

    return _rows_call("ple_bwd", body, [dx2, gpre, e, g], [True, True, True, False],
                      [jax.ShapeDtypeStruct(dx2.shape, MXU), jax.ShapeDtypeStruct(dx2.shape, MXU), jax.ShapeDtypeStruct((1, d), F32)],
                      [True, True, False], dx2.shape[0], tr)


def rms_bwd_res(dres, dh, x, g, tr):
    d = x.shape[1]

    def body(dr_ref, dh_ref, x_ref, g_ref, dx_ref, dg_ref):
        dv, dg = _norm_bwd(x_ref[...], g_ref[...], dh_ref[...].astype(F32))
        dx_ref[...] = dr_ref[...] + dv
        _acc_rows(dg_ref, dg)

    return _rows_call("rms_bwd_res", body, [dres, dh, x, g], [True, True, True, False],
                      [jax.ShapeDtypeStruct(x.shape, F32), jax.ShapeDtypeStruct((1, d), F32)], [True, False], x.shape[0], tr)


def post_bwd(dres, dh, x, g, o, g_o, tr):
    d = x.shape[1]

    def body(dr_ref, dh_ref, x_ref, g_ref, o_ref, go_ref, dx_ref, do_ref, dg_ref, dgo_ref):
        dv, dg = _norm_bwd(x_ref[...], g_ref[...], dh_ref[...].astype(F32))
        dxv = dr_ref[...] + dv
        dx_ref[...] = dxv
        dov, dgo = _norm_bwd(o_ref[...].astype(F32), go_ref[...], dxv)
        do_ref[...] = dov.astype(do_ref.dtype)
        _acc_rows(dg_ref, dg)
        _acc_rows(dgo_ref, dgo)

    return _rows_call("post_bwd", body, [dres, dh, x, g, o, g_o], [True, True, True, False, True, False],
                      [jax.ShapeDtypeStruct(x.shape, F32), jax.ShapeDtypeStruct(x.shape, MXU),
                       jax.ShapeDtypeStruct((1, d), F32), jax.ShapeDtypeStruct((1, d), F32)], [True, True, False, False], x.shape[0], tr)


def _trailing_sum(ext, w):
    s, k = ext, 1
    while k < w:
        s = s + pltpu.roll(s, k, 0)
        k *= 2
    return s


def _leading_sum(ext, w):
    n = ext.shape[0]
    s, k = ext, 1
    while k < w:
        s = s + pltpu.roll(s, n - k, 0)
        k *= 2
    return s


def _pool_inv_count(rb, tr, w, c):
    t = rb * tr + lax.broadcasted_iota(jnp.int32, (tr, c), 0)
    return 1.0 / jnp.minimum(t + 1, w).astype(F32)


def _pool_d(u_ref, up_ref, rb, tr, w):
    cur = u_ref[...].astype(F32)
    prev = jnp.where(rb > 0, up_ref[...].astype(F32), 0.0)
    ext = jnp.concatenate([prev, cur], axis=0)
    win = _trailing_sum(ext, w)[MAXW:]
    return win * _pool_inv_count(rb, tr, w, cur.shape[1]) - cur


def pool_fwd(uz, w_g, bias, scale, tr, hosted=()):
    _, s, e = uz.shape
    ng = len(POOL_WINDOWS)
    cg = e // ng
    nb = s // tr
    hb = tr // MAXW

    def body(u_ref, up_ref, z_ref, w_ref, b_ref, sc_ref, o_ref):
        g, rb = pl.program_id(0), pl.program_id(1)
        wmat = w_ref[...].reshape(cg, cg)
        for gg, win in enumerate(POOL_WINDOWS):
            @pl.when(g == gg)
            def _(win=win):
                d = _pool_d(u_ref, up_ref, rb, tr, win)
                y = (jnp.dot(d.astype(MXU), wmat, preferred_element_type=F32) + b_ref[...]) * sc_ref[...]
                z = z_ref[...].astype(F32)
                o_ref[...] = (y * z * _sig(z)).astype(o_ref.dtype)

    return _hosting_call(
        body, "pool_fwd", (ng, nb),
        [
            pl.BlockSpec((None, tr, cg), lambda g, r: (0, r, g)),
            pl.BlockSpec((None, MAXW, cg), lambda g, r: (0, jnp.maximum(r * hb - 1, 0), g)),
            pl.BlockSpec((None, tr, cg), lambda g, r: (1, r, g)),
            pl.BlockSpec((4, None, cg // 4, cg), lambda g, r: (0, g, 0, 0)),
            pl.BlockSpec((1, cg), lambda g, r: (0, g)),
            pl.BlockSpec((1, cg), lambda g, r: (0, g)),
        ],
        [pl.BlockSpec((tr, cg), lambda g, r: (r, g))], [jax.ShapeDtypeStruct((s, e), MXU)], [],
        [uz, uz, uz, w_g, bias, scale], hosted, ("parallel", "arbitrary"))


def pool_bwd(uz, dy2, w_g, bias, scale, tr, hosted=()):
    _, s, e = uz.shape
    ng = len(POOL_WINDOWS)
    cg = e // ng
    nb = s // tr
    hb = tr // MAXW

    def body(u_ref, up_ref, z_ref, dy_ref, w_ref, b_ref, sc_ref, duz_ref, gw_ref, db_ref, dsc_ref, acc_ref, carry_ref):
        g, step = pl.program_id(0), pl.program_id(1)
        rb = nb - 1 - step
        wmat = w_ref[...].reshape(cg, cg)
        for gg, win in enumerate(POOL_WINDOWS):
            @pl.when(g == gg)
            def _(win=win):
                d = _pool_d(u_ref, up_ref, rb, tr, win).astype(MXU)
                ypre = jnp.dot(d, wmat, preferred_element_type=F32) + b_ref[...]
                z = z_ref[...].astype(F32)
                sg = _sig(z)
                dy2v = dy_ref[...].astype(F32)
                dyv = dy2v * z * sg
                duz_ref[1] = (dy2v * ypre * sc_ref[...] * sg * (1.0 + z * (1.0 - sg))).astype(duz_ref.dtype)
                dypre = dyv * sc_ref[...]
                dsc = jnp.sum(dyv * ypre, axis=0, keepdims=True)
                dbv = jnp.sum(dypre, axis=0, keepdims=True)
                dypre_b = dypre.astype(MXU)
                dd = lax.dot_general(dypre_b, wmat, _DN["nt"], preferred_element_type=F32)
                gw = lax.dot_general(d, dypre_b, _DN["tn"], preferred_element_type=F32)
                q = dd * _pool_inv_count(rb, tr, win, cg)
                nxt = jnp.where(step > 0, carry_ref[...], 0.0)
                lead = _leading_sum(jnp.concatenate([q, nxt], axis=0), win)[:tr]
                duz_ref[0] = (lead - dd).astype(duz_ref.dtype)
                carry_ref[...] = q[:MAXW]

                @pl.when(step == 0)
                def _():
                    acc_ref[...] = gw
                    db_ref[...] = dbv
                    dsc_ref[...] = dsc

                @pl.when(step > 0)
                def _():
                    acc_ref[...] += gw
                    db_ref[...] += dbv
                    dsc_ref[...] += dsc

                @pl.when(step == nb - 1)
                def _():
                    gw_ref[...] = acc_ref[...].reshape(4, cg // 4, cg).astype(gw_ref.dtype)

    return _hosting_call(
        body, "pool_bwd", (ng, nb),
        [
            pl.BlockSpec((None, tr, cg), lambda g, r: (0, nb - 1 - r, g)),
            pl.BlockSpec((None, MAXW, cg), lambda g, r: (0, jnp.maximum((nb - 1 - r) * hb - 1, 0), g)),
            pl.BlockSpec((None, tr, cg), lambda g, r: (1, nb - 1 - r, g)),
            pl.BlockSpec((tr, cg), lambda g, r: (nb - 1 - r, g)),
            pl.BlockSpec((4, None, cg // 4, cg), lambda g, r: (0, g, 0, 0)),
            pl.BlockSpec((1, cg), lambda g, r: (0, g)),
            pl.BlockSpec((1, cg), lambda g, r: (0, g)),
        ],
        [
            pl.BlockSpec((2, tr, cg), lambda g, r: (0, nb - 1 - r, g)),
            pl.BlockSpec((4, None, cg // 4, cg), lambda g, r: (0, g, 0, 0)),
            pl.BlockSpec((1, cg), lambda g, r: (0, g)),
            pl.BlockSpec((1, cg), lambda g, r: (0, g)),
        ],
        [
            jax.ShapeDtypeStruct((2, s, e), MXU),
            jax.ShapeDtypeStruct(w_g.shape, WIRE),
            jax.ShapeDtypeStruct((1, e), F32),
            jax.ShapeDtypeStruct((1, e), F32),
        ],
        [pltpu.VMEM((cg, cg), F32), pltpu.VMEM((MAXW, cg), F32)],
        [uz, uz, uz, dy2, w_g, bias, scale], hosted, ("parallel", "arbitrary"))


HALO = 16


def _one_minus_sq(log_a, a):
    poly = (-2.0 * log_a) * (1.0 + log_a * (1.0 + log_a * (2.0 / 3.0)))
    return jnp.where(log_a > -0.01, poly, 1.0 - a * a)


def _softplus_neg(lam):
    t = jnp.exp(-jnp.abs(lam))
    log1p = jnp.where(t < 1e-3, t * (1.0 - t * (0.5 - t * (1.0 / 3.0))), jnp.log(1.0 + t))
    return jnp.maximum(-lam, 0.0) + log1p, _sig(-lam)


def _lru_gates(u_ref, up_ref, rb, sm_ref, wa, wx):
    cur = u_ref[...].astype(F32)
    prev = jnp.where(rb > 0, up_ref[...].astype(F32), 0.0)
    ext = jnp.concatenate([prev, cur], axis=0)
    taps = [cur] + [pltpu.roll(ext, k, 0)[HALO:] for k in range(1, CONV_W)]
    uc = sm_ref[CONV_W:CONV_W + 1, :]
    for k in range(CONV_W):
        uc = uc + taps[k] * sm_ref[CONV_W - 1 - k:CONV_W - k, :]
    ucb = uc.astype(MXU)
    r = _sig(jnp.dot(ucb, wa, preferred_element_type=F32) + sm_ref[5:6, :])
    ig = _sig(jnp.dot(ucb, wx, preferred_element_type=F32) + sm_ref[6:7, :])
    sp, sgn = _softplus_neg(sm_ref[7:8, :])
    log_a = r * (-LRU_C * sp)
    a = jnp.exp(log_a)
    mult = jnp.sqrt(jnp.maximum(_one_minus_sq(log_a, a), 0.0))
    return taps, uc, ucb, r, ig, sp, sgn, a, mult


LANES = 128


def _seg_scan(a, b, out_ref, scr, state, reverse):
    a_s, b_s, h_s, p_s = scr
    tr, c = a.shape
    seg = tr // 8
    nl = c // LANES
    for l in range(nl):
        a_s[l] = a[:, l * LANES:(l + 1) * LANES]
        b_s[l] = b[:, l * LANES:(l + 1) * LANES]
    h = [jnp.zeros((8, LANES), F32)] * nl
    pp = [jnp.ones((8, LANES), F32)] * nl
    for i in (range(seg - 1, -1, -1) if reverse else range(seg)):
        rows = pl.ds(i, 8, stride=seg)
        for l in range(nl):
            av = a_s[l, rows, :]
            h[l] = av * h[l] + b_s[l, rows, :]
            pp[l] = av * pp[l]
            h_s[l, pl.ds(8 * i, 8), :] = h[l]
            p_s[l, pl.ds(8 * i, 8), :] = pp[l]
    leaving = []
    sub = lax.broadcasted_iota(jnp.int32, (8, LANES), 0)
    for l in range(nl):
        lanes = slice(l * LANES, (l + 1) * LANES)
        st = state[:, lanes]
        entering = jnp.zeros((8, LANES), F32)
        for sgm in (range(7, -1, -1) if reverse else range(8)):
            entering = jnp.where(sub == sgm, st, entering)
            st = h[l][sgm:sgm + 1, :] + pp[l][sgm:sgm + 1, :] * st
        leaving.append(st)
        for i in range(seg):
            rows = pl.ds(8 * i, 8)
            h_s[l, rows, :] = h_s[l, rows, :] + p_s[l, rows, :] * entering
        for sgm in range(8):
            for t0 in range(0, seg, 8):
                out_ref[pl.ds(sgm * seg + t0, 8), lanes] = h_s[l, pl.ds(8 * t0 + sgm, 8, stride=8), :]
    return jnp.concatenate(leaving, axis=1)


def lru_fwd(uz, wa_g, wx_g, small, tr, hosted=()):
    _, s, e = uz.shape
    cb = e // LRU_HEADS
    nb = s // tr
    hb = tr // HALO

    def body(u_ref, up_ref, z_ref, wa_ref, wx_ref, sm_ref, o_ref, h_ref, s0, s1, s2, s3, carry_ref):
        rb = pl.program_id(1)
        wa = wa_ref[...].reshape(cb, cb)
        wx = wx_ref[...].reshape(cb, cb)
        _, uc, _, _, ig, _, _, a, mult = _lru_gates(u_ref, up_ref, rb, sm_ref, wa, wx)
        start = jnp.where(rb > 0, carry_ref[0:1, :], 0.0)
        last = _seg_scan(a, mult * ig * uc, h_ref, (s0, s1, s2, s3), start, False)
        carry_ref[...] = jnp.broadcast_to(last, carry_ref.shape)
        z = z_ref[...].astype(F32)
        o_ref[...] = (h_ref[...] * z * _sig(z)).astype(o_ref.dtype)

    wspec = pl.BlockSpec((4, None, cb // 4, cb), lambda h, r: (0, h, 0, 0))
    return _hosting_call(
        body, "lru_fwd", (LRU_HEADS, nb),
        [
            pl.BlockSpec((None, tr, cb), lambda h, r: (0, r, h)),
            pl.BlockSpec((None, HALO, cb), lambda h, r: (0, jnp.maximum(r * hb - 1, 0), h)),
            pl.BlockSpec((None, tr, cb), lambda h, r: (1, r, h)),
            wspec, wspec,
            pl.BlockSpec((8, cb), lambda h, r: (0, h)),
        ],
        [pl.BlockSpec((tr, cb), lambda h, r: (r, h)), pl.BlockSpec((tr, cb), lambda h, r: (r, h))],
        [jax.ShapeDtypeStruct((s, e), MXU), jax.ShapeDtypeStruct((s, e), F32)],
        [pltpu.VMEM((cb // LANES, tr, LANES), F32)] * 4 + [pltpu.VMEM((8, cb), F32)],
        [uz, uz, uz, wa_g, wx_g, small], hosted, ("parallel", "arbitrary"))


def lru_bwd(uz, hst, dy2, wa_g, wx_g, small, tr, hosted=()):
    _, s, e = uz.shape
    cb = e // LRU_HEADS
    nb = s // tr
    hb = tr // HALO

    def body(u_ref, up_ref, z_ref, h_ref, hp_ref, dy_ref, wa_ref, wx_ref, sm_ref,
             duz_ref, gwa_ref, gwx_ref, dsm_ref, s0, s1, s2, s3, g_s, acc_a, acc_x, gcar, acar, dcar):
        step = pl.program_id(1)
        rb = nb - 1 - step
        wa = wa_ref[...].reshape(cb, cb)
        wx = wx_ref[...].reshape(cb, cb)
        taps, uc, ucb, r, ig, sp, sgn, a, mult = _lru_gates(u_ref, up_ref, rb, sm_ref, wa, wx)
        row = lax.broadcasted_iota(jnp.int32, a.shape, 0)
        z = z_ref[...].astype(F32)
        sg = _sig(z)
        dy2v = dy_ref[...].astype(F32)
        hv = h_ref[...]
        duz_ref[1] = (dy2v * hv * sg * (1.0 + z * (1.0 - sg))).astype(duz_ref.dtype)
        a_next = jnp.where(row == tr - 1, jnp.where(step > 0, acar[0:1, :], 0.0), pltpu.roll(a, tr - 1, 0))
        g_first = _seg_scan(a_next, dy2v * z * sg, g_s, (s0, s1, s2, s3), jnp.where(step > 0, gcar[0:1, :], 0.0), True)
        gcar[...] = jnp.broadcast_to(g_first, gcar.shape)
        acar[...] = jnp.broadcast_to(a[0:1, :], acar.shape)
        gv = g_s[...]
        h_before = jnp.where(rb > 0, hp_ref[HALO - 1:HALO, :], 0.0)
        h_prev = jnp.where(row == 0, h_before, pltpu.roll(hv, 1, 0))
        da = gv * h_prev
        gu = gv * uc
        dmult = gu * ig
        dig = gu * mult
        dlog_a = da * a - dmult * jnp.where(mult > 0.0, a * a / mult, 0.0)
        dra = dlog_a * (-LRU_C) * sp * r * (1.0 - r)
        dix = dig * ig * (1.0 - ig)
        dl = jnp.sum(dlog_a * r, axis=0, keepdims=True) * (LRU_C * sgn)
        dra_b, dix_b = dra.astype(MXU), dix.astype(MXU)
        duc = (gv * mult * ig + lax.dot_general(dra_b, wa, _DN["nt"], preferred_element_type=F32)
               + lax.dot_general(dix_b, wx, _DN["nt"], preferred_element_type=F32))
        gwa = lax.dot_general(ucb, dra_b, _DN["tn"], preferred_element_type=F32)
        gwx = lax.dot_general(ucb, dix_b, _DN["tn"], preferred_element_type=F32)
        ext = jnp.concatenate([duc, jnp.where(step > 0, dcar[...], 0.0)], axis=0)
        n = ext.shape[0]
        du = duc * sm_ref[CONV_W - 1:CONV_W, :]
        for k in range(1, CONV_W):
            du = du + pltpu.roll(ext, n - k, 0)[:tr] * sm_ref[CONV_W - 1 - k:CONV_W - k, :]
        duz_ref[0] = du.astype(duz_ref.dtype)
        dcar[...] = duc[:HALO]
        rows = [jnp.sum(duc * taps[CONV_W - 1 - k], axis=0, keepdims=True) for k in range(CONV_W)]
        rows += [jnp.sum(duc, axis=0, keepdims=True), jnp.sum(dra, axis=0, keepdims=True),
                 jnp.sum(dix, axis=0, keepdims=True), dl]

        @pl.when(step == 0)
        def _():
            acc_a[...] = gwa
            acc_x[...] = gwx
            for k, rv in enumerate(rows):
                dsm_ref[k:k + 1, :] = rv

        @pl.when(step > 0)
        def _():
            acc_a[...] += gwa
            acc_x[...] += gwx
            for k, rv in enumerate(rows):
                dsm_ref[k:k + 1, :] += rv

        @pl.when(step == nb - 1)
        def _():
            gwa_ref[...] = acc_a[...].reshape(4, cb // 4, cb).astype(gwa_ref.dtype)
            gwx_ref[...] = acc_x[...].reshape(4, cb // 4, cb).astype(gwx_ref.dtype)

    wspec = pl.BlockSpec((4, None, cb // 4, cb), lambda h, r: (0, h, 0, 0))
    blk = pl.BlockSpec((tr, cb), lambda h, r: (nb - 1 - r, h))
    return _hosting_call(
        body, "lru_bwd", (LRU_HEADS, nb),
        [
            pl.BlockSpec((None, tr, cb), lambda h, r: (0, nb - 1 - r, h)),
            pl.BlockSpec((None, HALO, cb), lambda h, r: (0, jnp.maximum((nb - 1 - r) * hb - 1, 0), h)),
            pl.BlockSpec((None, tr, cb), lambda h, r: (1, nb - 1 - r, h)),
            blk,
            pl.BlockSpec((HALO, cb), lambda h, r: (jnp.maximum((nb - 1 - r) * hb - 1, 0), h)),
            blk,
            wspec, wspec,
            pl.BlockSpec((8, cb), lambda h, r: (0, h)),
        ],
        [
            pl.BlockSpec((2, tr, cb), lambda h, r: (0, nb - 1 - r, h)),
            wspec, wspec,
            pl.BlockSpec((8, cb), lambda h, r: (0, h)),
        ],
        [
            jax.ShapeDtypeStruct((2, s, e), MXU),
            jax.ShapeDtypeStruct(wa_g.shape, WIRE),
            jax.ShapeDtypeStruct(wx_g.shape, WIRE),
            jax.ShapeDtypeStruct((8, e), F32),
        ],
        ([pltpu.VMEM((cb // LANES, tr, LANES), F32)] * 4 + [pltpu.VMEM((tr, cb), F32)]
         + [pltpu.VMEM((cb, cb), F32)] * 2 + [pltpu.VMEM((8, cb), F32)] * 2 + [pltpu.VMEM((HALO, cb), F32)]),
        [uz, uz, uz, hst, hst, dy2, wa_g, wx_g, small], hosted, ("parallel", "arbitrary"))


def _place():
    x, y, c = lax.axis_index("x"), lax.axis_index("y"), lax.axis_index("c")
    chips = [(1 - x, y), (x, 1 - y), (1 - x, 1 - y)]
    return x, y, c, chips


def _rcopy(src, dst, send_sems, recv_sems, k, to):
    return pltpu.make_async_remote_copy(src_ref=src, dst_ref=dst, send_sem=send_sems.at[k], recv_sem=recv_sems.at[k],
                                        device_id=to, device_id_type=MESHID)


def _stage_gather_ici(bufs):
    n = len(bufs)

    def quarter(ref, chip, c, q):
        rq = ref.shape[3] // 2
        return ref.at[2 * chip[0] + chip[1], :, c, pl.ds(q * rq, rq)]

    def copies(refs, ss, rs, off, a, sending):
        x, y, c, _ = _place()
        me, xn, yn, dg = (x, y), (1 - x, y), (x, 1 - y), (1 - x, 1 - y)
        plan = [(0, me, xn, 0), (1, me, xn, 1), (3, me, yn, 1), (2, me, yn, 0),
                (4, xn, yn, 0), (5, yn, xn, 1)]
        if not sending:
            plan = [(0, xn, xn, 0), (1, xn, xn, 1), (3, yn, yn, 1), (2, yn, yn, 0), (4, dg, yn, 0), (5, dg, xn, 1)]
        out = []
        for k, owner, to, q in plan:
            blk = quarter(refs[a], owner, c, q)
            out.append(_rcopy(blk, blk, ss, rs, off + 6 * a + k, (*to, c)))
        return out

    def start(ro, refs, ss, rs, off):
        for a in range(n):
            for cp in copies(refs, ss, rs, off, a, True)[:4]:
                cp.start()

    def mid(ro, refs, ss, rs, off):
        for a in range(n):
            got, out = copies(refs, ss, rs, off, a, False), copies(refs, ss, rs, off, a, True)
            got[0].wait_recv()
            out[4].start()
            got[2].wait_recv()
            out[5].start()

    def finish(ro, refs, ss, rs, off):
        for a in range(n):
            got = copies(refs, ss, rs, off, a, False)
            for k in (1, 3, 4, 5):
                got[k].wait_recv()
            for cp in copies(refs, ss, rs, off, a, True):
                cp.wait_send()

    return dict(ro=[], bufs=list(bufs), nsem=6 * n, start=start, mid=mid, finish=finish)


def _stage_gather_d2d(bufs):
    n = len(bufs)

    def copies(refs, ss, rs, off, sending):
        x, y, c, chips = _place()
        out = []
        for a in range(n):
            for jj, ch in enumerate(chips):
                blk = refs[a].at[2 * ch[0] + ch[1], :, c if sending else 1 - c]
                out.append(_rcopy(blk, blk, ss, rs, off + 3 * a + jj, (x, y, 1 - c)))
        return out

    def start(ro, refs, ss, rs, off):
        for cp in copies(refs, ss, rs, off, True):
            cp.start()

    def finish(ro, refs, ss, rs, off):
        for cp in copies(refs, ss, rs, off, False):
            cp.wait_recv()
        for cp in copies(refs, ss, rs, off, True):
            cp.wait_send()

    return dict(ro=[], bufs=list(bufs), nsem=3 * n, start=start, finish=finish)


def _stage_scatter_ici(parts, gots):
    n = len(parts)

    def copies(ro, refs, ss, rs, off):
        x, y, c, chips = _place()
        return [_rcopy(ro[a].at[2 * ch[0] + ch[1]], refs[a].at[jj], ss, rs, off + 3 * a + jj, (*ch, c))
                for a in range(n) for jj, ch in enumerate(chips)]

    def start(ro, refs, ss, rs, off):
        for cp in copies(ro, refs, ss, rs, off):
            cp.start()

    def finish(ro, refs, ss, rs, off):
        for cp in copies(ro, refs, ss, rs, off):
            cp.wait()

    return dict(ro=list(parts), bufs=list(gots), nsem=3 * n, start=start, finish=finish)


def _stage_send_half(grads, lands):
    n = len(grads)

    def copies(ro, refs, ss, rs, off):
        x, y, c, _ = _place()
        return [_rcopy(ro[a].at[:, :, 1 - c], refs[a], ss, rs, off + a, (x, y, 1 - c)) for a in range(n)]

    def start(ro, refs, ss, rs, off):
        for cp in copies(ro, refs, ss, rs, off):
            cp.start()

    def finish(ro, refs, ss, rs, off):
        for cp in copies(ro, refs, ss, rs, off):
            cp.wait()

    return dict(ro=list(grads), bufs=list(lands), nsem=n, start=start, finish=finish)


def cast_into_slab(w, first, l, k_idx):
    _, r, c = w.shape
    rh = r // 2
    lb, tr = _block_lr(l, rh, c)
    nbh = rh // tr
    assert first % lb == 0

    def body(k_ref, w_ref, o_ref):
        o_ref[...] = w_ref[...].astype(o_ref.dtype)

    return pl.pallas_call(
        body, name="cast_into_slab",
        grid_spec=pltpu.PrefetchScalarGridSpec(
            num_scalar_prefetch=1, grid=(l // lb, 2, nbh),
            in_specs=[pl.BlockSpec((lb, tr, c), lambda i, h, b, k_ref: (first // lb + i, h * nbh + b, 0))],
            out_specs=pl.BlockSpec((None, lb, None, tr, c), lambda i, h, b, k_ref: (k_ref[0], i, h, b, 0))),
        out_shape=jax.ShapeDtypeStruct((4, l, 2, rh, c), WIRE),
        compiler_params=_cp(("parallel", "parallel", "parallel")),
    )(k_idx, w)


def gather_weights(bufs):
    n = len(bufs)
    ici = [_stage_gather_ici([b]) for b in bufs]
    d2d = [_stage_gather_d2d([b]) for b in bufs]
    per = ici[0]["nsem"] + d2d[0]["nsem"]

    def body(*refs):
        outs = refs[n:2 * n]
        ss, rs = refs[2 * n:]
        for what in ("start", "mid"):
            for a in range(n):
                ici[a][what]([], [outs[a]], ss, rs, per * a)
        for a in range(n):
            ici[a]["finish"]([], [outs[a]], ss, rs, per * a)
            d2d[a]["start"]([], [outs[a]], ss, rs, per * a + ici[a]["nsem"])
        for a in range(n):
            d2d[a]["finish"]([], [outs[a]], ss, rs, per * a + ici[a]["nsem"])

    return pl.pallas_call(
        body, name="gather_weights",
        in_specs=[ANY] * n, out_specs=[ANY] * n,
        out_shape=[jax.ShapeDtypeStruct(a.shape, a.dtype) for a in bufs],
        scratch_shapes=[pltpu.SemaphoreType.DMA((per * n,)), pltpu.SemaphoreType.DMA((per * n,))],
        input_output_aliases={a: a for a in range(n)},
        compiler_params=pltpu.CompilerParams(has_side_effects=True),
    )(*bufs)


def all_gather_small(v, name):
    m_per, n = v.shape

    def body(x_ref, out_ref, send_sems, recv_sems, local_sem):
        x, y, c, chips = _place()
        me, sibling = (x, y, c), (x, y, 1 - c)

        def rows(px, py, pc):
            return out_ref.at[pl.ds((4 * px + 2 * py + pc) * m_per, m_per), :]

        def copy(k, block, to, src=None):
            return _rcopy(rows(*block) if src is None else src, rows(*block), send_sems, recv_sems, k, to)

        mine = pltpu.make_async_copy(x_ref, rows(*me), local_sem)
        mine.start()
        first = [copy(0, me, sibling, src=x_ref)]
        first += [copy(1 + jj, me, (*chip, c), src=x_ref) for jj, chip in enumerate(chips)]
        for cp in first:
            cp.start()
        passed = [copy(4 + jj, (*chip, c), sibling) for jj, chip in enumerate(chips)]
        for jj, chip in enumerate(chips):
            copy(1 + jj, (*chip, c), me).wait_recv()
            passed[jj].start()
        copy(0, sibling, me).wait_recv()
        for jj, chip in enumerate(chips):
            copy(4 + jj, (*chip, 1 - c), me).wait_recv()
        for cp in first + passed:
            cp.wait_send()
        mine.wait()

    return pl.pallas_call(
        body, name=name,
        out_shape=jax.ShapeDtypeStruct((8 * m_per, n), v.dtype),
        in_specs=[pl.BlockSpec(memory_space=pltpu.VMEM)],
        out_specs=pl.BlockSpec(memory_space=pltpu.VMEM),
        scratch_shapes=[pltpu.SemaphoreType.DMA((7,)), pltpu.SemaphoreType.DMA((7,)), pltpu.SemaphoreType.DMA],
        compiler_params=pltpu.CompilerParams(vmem_limit_bytes=VMEM_LIMIT),
    )(v)


def sum_devices(g):
    def body(g_ref, o_ref):
        acc = g_ref[0]
        for d in range(1, 8):
            acc = acc + g_ref[d]
        o_ref[...] = acc

    return pl.pallas_call(body, name="sum_devices", out_shape=jax.ShapeDtypeStruct(g.shape[1:], g.dtype),
                          compiler_params=pltpu.CompilerParams(vmem_limit_bytes=VMEM_LIMIT))(g)


def send_other_half(grads):
    n = len(grads)

    def body(*refs):
        ins, outs = refs[:n], refs[n:2 * n]
        send_sems, recv_sems = refs[2 * n:]
        x, y, c, _ = _place()
        sib = (x, y, 1 - c)
        cps = [_rcopy(ins[a].at[:, :, 1 - c], outs[a], send_sems, recv_sems, a, sib) for a in range(n)]
        for cp in cps:
            cp.start()
        for cp in cps:
            cp.wait()

    return pl.pallas_call(
        body, name="send_other_half", in_specs=[ANY] * n, out_specs=[ANY] * n,
        out_shape=[jax.ShapeDtypeStruct(g.shape[:2] + g.shape[3:], g.dtype) for g in grads],
        scratch_shapes=[pltpu.SemaphoreType.DMA((n,)), pltpu.SemaphoreType.DMA((n,))],
        compiler_params=pltpu.CompilerParams(has_side_effects=True),
    )(*grads)


def scatter_to_chips(parts, fulls, spans):
    n, m = len(parts), len(fulls)

    def body(*refs):
        ins, outs, shared = refs[:n], refs[n + m:2 * n + m], refs[2 * n + m:2 * n + 2 * m]
        send_sems, recv_sems = refs[2 * n + 2 * m:]
        x, y, c, chips = _place()
        sib = (x, y, 1 - c)
        cps = []
        for a in range(n):
            for jj, ch in enumerate(chips):
                cps.append(_rcopy(ins[a].at[2 * ch[0] + ch[1]], outs[a].at[jj], send_sems, recv_sems, 3 * a + jj, (*ch, c)))

        def blk(a, half):
            return shared[a].at[pl.ds(spans[a][0], spans[a][1]), half]

        swaps = [_rcopy(blk(a, c), blk(a, c), send_sems, recv_sems, 3 * n + a, sib) for a in range(m)]
        for cp in cps + swaps:
            cp.start()
        for cp in cps:
            cp.wait()
        for a in range(m):
            _rcopy(blk(a, 1 - c), blk(a, 1 - c), send_sems, recv_sems, 3 * n + a, sib).wait_recv()
        for cp in swaps:
            cp.wait_send()

    res = pl.pallas_call(
        body, name="scatter_to_chips", in_specs=[ANY] * (n + m), out_specs=[ANY] * (n + m),
        out_shape=[jax.ShapeDtypeStruct((3,) + p.shape[1:], p.dtype) for p in parts]
        + [jax.ShapeDtypeStruct(f.shape, f.dtype) for f in fulls],
        scratch_shapes=[pltpu.SemaphoreType.DMA((3 * n + m,)), pltpu.SemaphoreType.DMA((3 * n + m,))],
        input_output_aliases={n + a: n + a for a in range(m)},
        compiler_params=pltpu.CompilerParams(has_side_effects=True),
    )(*parts, *fulls)
    return list(res[:n]), list(res[n:])


def share_halves(bufs, spans):
    n = len(bufs)

    def body(*refs):
        outs = refs[n:2 * n]
        send_sems, recv_sems = refs[2 * n:]
        x, y, c, _ = _place()
        sib = (x, y, 1 - c)

        def blk(a, half):
            return outs[a].at[pl.ds(spans[a][0], spans[a][1]), half]

        cps = [_rcopy(blk(a, c), blk(a, c), send_sems, recv_sems, a, sib) for a in range(n)]
        for cp in cps:
            cp.start()
        for a in range(n):
            _rcopy(blk(a, 1 - c), blk(a, 1 - c), send_sems, recv_sems, a, sib).wait_recv()
        for cp in cps:
            cp.wait_send()

    return pl.pallas_call(
        body, name="share_halves", in_specs=[ANY] * n, out_specs=[ANY] * n,
        out_shape=[jax.ShapeDtypeStruct(b.shape, b.dtype) for b in bufs],
        scratch_shapes=[pltpu.SemaphoreType.DMA((n,)), pltpu.SemaphoreType.DMA((n,))],
        input_output_aliases={a: a for a in range(n)},
        compiler_params=pltpu.CompilerParams(has_side_effects=True),
    )(*bufs)


def _block_rows(r, c, itemsize, budget=1 << 20):
    tr = r
    while tr * c * itemsize > budget and tr % 16 == 0:
        tr //= 2
    return tr


def _block_lr(l, r, c, budget=4 << 20):
    tr = _block_rows(r, c, 4, budget)
    lb = 1
    if tr == r:
        while l % (2 * lb) == 0 and 2 * lb * r * c * 4 <= budget:
            lb *= 2
    return lb, tr


def add_halves(g, got, c_idx):
    k4, l, _, rh, cc = g.shape
    lb, tr = _block_lr(l, rh, cc)

    def body(c_ref, g_ref, r_ref, o_ref):
        o_ref[...] = (g_ref[...].astype(F32) + r_ref[...].astype(F32)).astype(o_ref.dtype)

    return pl.pallas_call(
        body, name="add_halves",
        grid_spec=pltpu.PrefetchScalarGridSpec(
            num_scalar_prefetch=1, grid=(k4, l // lb, rh // tr),
            in_specs=[pl.BlockSpec((None, lb, None, tr, cc), lambda k, i, b, c_ref: (k, i, c_ref[0], b, 0)),
                      pl.BlockSpec((None, lb, tr, cc), lambda k, i, b, c_ref: (k, i, b, 0))],
            out_specs=pl.BlockSpec((None, lb, tr, cc), lambda k, i, b, c_ref: (k, i, b, 0))),
        out_shape=jax.ShapeDtypeStruct(got.shape, WIRE),
        compiler_params=_cp(("parallel", "parallel", "parallel")),
    )(c_idx, g, got)


def sum_chips(own, got, kc_idx, full, first):
    _, l, rh, cc = own.shape
    lb, tr = _block_lr(l, rh, cc, 2 << 20)
    assert first % lb == 0

    def body(k_ref, o_ref, r_ref, _full, s_ref):
        s_ref[...] = ((o_ref[...].astype(F32) + r_ref[0].astype(F32)) + r_ref[1].astype(F32)) + r_ref[2].astype(F32)

    return pl.pallas_call(
        body, name="sum_chips",
        grid_spec=pltpu.PrefetchScalarGridSpec(
            num_scalar_prefetch=1, grid=(l // lb, rh // tr),
            in_specs=[pl.BlockSpec((None, lb, tr, cc), lambda i, b, k_ref: (k_ref[0], i, b, 0)),
                      pl.BlockSpec((3, lb, tr, cc), lambda i, b, k_ref: (0, i, b, 0)),
                      ANY],
            out_specs=pl.BlockSpec((lb, None, tr, cc), lambda i, b, k_ref: (first // lb + i, k_ref[1], b, 0))),
        out_shape=jax.ShapeDtypeStruct(full.shape, F32),
        input_output_aliases={3: 0},
        compiler_params=_cp(("parallel", "parallel")),
    )(kc_idx, own, got, full)


def _adam_math(w, g, m, v):
    m = ADAM_B1 * m + (1.0 - ADAM_B1) * g
    v = ADAM_B2 * v + (1.0 - ADAM_B2) * (g * g)
    m_hat = m / (1.0 - ADAM_B1 ** ADAM_STEP)
    v_hat = v / (1.0 - ADAM_B2 ** ADAM_STEP)
    delta = -ADAM_LR * (m_hat / (jnp.sqrt(v_hat) + ADAM_EPS) + ADAM_WD * w)
    return delta, m, v


def adamw(w, g, g_first, m, v, lo, hi, prev=None, hosted=()):
    l, r, c = w.shape
    tr = _block_rows(r, c, 4, 2 << 20)
    prev = list(prev or [])

    def body(w_ref, g_ref, m_ref, v_ref, *rest):
        go_ref, d_ref, mo_ref, vo_ref = rest[len(prev):]
        gv = g_ref[...]
        go_ref[...] = gv
        d_ref[...], mo_ref[...], vo_ref[...] = _adam_math(w_ref[...], gv, m_ref[...], v_ref[...])

    spec = pl.BlockSpec((None, tr, c), lambda i, b: (lo + i, b, 0))
    gspec = pl.BlockSpec((None, tr, c), lambda i, b: (g_first + i, b, 0))
    return _hosting_call(
        body, "adamw", (hi - lo, r // tr), [spec, gspec, spec, spec] + [ANY] * len(prev), [spec] * 4,
        [jax.ShapeDtypeStruct(w.shape, F32)] * 4, [], [w, g, m, v] + prev, hosted, ("parallel", "parallel"),
        aliases={4 + t: t for t in range(len(prev))})


def adamw_small(w, g, m, v):
    def body(w_ref, g_ref, m_ref, v_ref, d_ref, mo_ref, vo_ref):
        d_ref[...], mo_ref[...], vo_ref[...] = _adam_math(w_ref[...], g_ref[...], m_ref[...], v_ref[...])

    return pl.pallas_call(body, name="adamw_small", out_shape=[jax.ShapeDtypeStruct(w.shape, F32)] * 3)(w, g, m, v)


def kernel(x, p, w_in, w_out, g_pre, g_post, pool_w, pool_b, pool_scale, conv_w, conv_b, lru_wa, lru_ba, lru_wx, lru_bx, lru_L, w_ple, w_ple_gate, g_ple_in, g_ple_out, loss_target, m_w_in, m_w_out, m_g_pre, m_g_post, m_pool_w, m_pool_b, m_pool_scale, m_conv_w, m_conv_b, m_lru_wa, m_lru_ba, m_lru_wx, m_lru_bx, m_lru_L, m_w_ple, m_w_ple_gate, m_g_ple_in, m_g_ple_out, v_w_in, v_w_out, v_g_pre, v_g_post, v_pool_w, v_pool_b, v_pool_scale, v_conv_w, v_conv_b, v_lru_wa, v_lru_ba, v_lru_wx, v_lru_bx, v_lru_L, v_w_ple, v_w_ple_gate, v_g_ple_in, v_g_ple_out):
    depth = w_in.shape[0]
    _, s, d = x.shape
    e = 2 * d
    kp = p.shape[-1]
    nmix = pool_w.shape[0]
    ngrp = pool_w.shape[1]
    cg = e // ngrp
    cb = e // LRU_HEADS
    xi, yi, ci = lax.axis_index("x"), lax.axis_index("y"), lax.axis_index("c")
    me = 2 * xi + yi
    c_idx = jnp.reshape(ci, (1,)).astype(jnp.int32)
    k_idx = jnp.reshape(me, (1,)).astype(jnp.int32)
    tr_row = _tile(s, 256)
    tr_mix = _tile(s, 512)
    tm = _tile(s, 1024)
    tm2 = _tile(s, 2048)

    def halves(a):
        return a.reshape(a.shape[0], 2, a.shape[1] // 2, a.shape[2])

    big = {
        "w_in": w_in, "w_out": w_out, "gate": w_ple_gate, "ple": w_ple,
        "pool": pool_w.reshape(nmix * ngrp, cg // 4, cg),
        "wa": lru_wa.reshape(nmix * LRU_HEADS, cb // 4, cb), "wx": lru_wx.reshape(nmix * LRU_HEADS, cb // 4, cb),
    }
    names = list(big)

    def layer_shards(i):
        sh = {n: (big[n], i, 1) for n in ("w_in", "w_out", "gate", "ple")}
        if i % 2 == 0:
            sh["pool"] = (big["pool"], (i // 2) * ngrp, ngrp)
        else:
            sh["wa"], sh["wx"] = (big["wa"], (i // 2) * LRU_HEADS, LRU_HEADS), (big["wx"], (i // 2) * LRU_HEADS, LRU_HEADS)
        return sh

    def mixer_names(i):
        return ["pool"] if i % 2 == 0 else ["wa", "wx"]

    wbuf = [{n: cast_into_slab(*wfl, k_idx) for n, wfl in layer_shards(i).items()} for i in range(depth)]
    first = list(wbuf[0])
    wbuf[0] = dict(zip(first, gather_weights([wbuf[0][n] for n in first])))

    def full_w(i, n):
        b = wbuf[i][n]
        return b.reshape(b.shape[0], b.shape[1], 2 * b.shape[3], b.shape[4])

    def stages_of(specs):
        return [mk([wbuf[l][n] for n in nms]) for mk, l, nms in specs]

    def keep(specs, new):
        for (l, n), b in zip([(l, n) for _, l, nms in specs for n in nms], new):
            wbuf[l][n] = b

    def run_mm(specs, *args):
        if not specs:
            return _matmul(*args)
        out, new = _matmul(*args, hosted=stages_of(specs))
        keep(specs, new)
        return out

    ec = e // 4
    small_loc = jnp.concatenate([conv_w, conv_b[:, None], lru_ba[:, None], lru_bx[:, None], lru_L[:, None]], axis=1)
    sm_all = all_gather_small(small_loc.reshape(nmix * 8, ec), "gather_small").reshape(4, 2, nmix, 8, ec)
    lru_small = jnp.transpose(sm_all[:, 0], (1, 2, 0, 3)).reshape(nmix, 8, e)

    xs = x[0]
    saved = []
    for i in range(depth):
        j = i // 2
        h = rms_fwd(xs, g_pre[i][None], tr_row) if i == 0 else h_next
        nj = (2 * e) // 1024 if (2 * e) % 1024 == 0 else 4
        tn = (2 * e) // nj
        per = e // tn
        perk = (e // 2) // tn
        nxt = i + 1 if i + 1 < depth else None
        stages = [(_stage_gather_d2d, i, ["gate", "ple"])] if i > 0 else []
        if nxt is not None:
            stages.append((_stage_gather_ici, nxt, ["w_in"]))
        uz = run_mm(
            stages, "mm_in", "nn", (s // tm2, nj, 1), h, pl.BlockSpec((tm2, d), lambda a, b, k: (a, 0)),
            full_w(i, "w_in"), pl.BlockSpec((None, None, d, tn), lambda a, b, k, perk=perk: (b // perk, 0, 0, b % perk)),
            jax.ShapeDtypeStruct((2, s, e), MXU), pl.BlockSpec((None, tm2, tn), lambda a, b, k, per=per: (b // per, a, b % per)),
            (8, 128))
        specs = []
        if nxt is not None:
            specs = [(_stage_gather_ici, nxt, mixer_names(nxt) + (["w_out"] if i % 2 else []))]
        if i % 2 == 0:
            (y2,), new = pool_fwd(uz, full_w(i, "pool"), pool_b[j][None], pool_scale[j][None], tr_mix, hosted=stages_of(specs))
            hst = None
        else:
            (y2, hst), new = lru_fwd(uz, full_w(i, "wa"), full_w(i, "wx"), lru_small[j], tr_mix, hosted=stages_of(specs))
        keep(specs, new)
        tn_o = _tile(d, 1024)
        stages = []
        if nxt is not None:
            stages = [(_stage_gather_ici, nxt, ["w_out"] if i % 2 == 0 else ["gate", "ple"])]
        o = run_mm(
            stages, "mm_out", "nn", (s // tm, d // tn_o, 1), y2, pl.BlockSpec((tm, e), lambda a, b, k: (a, 0)),
            full_w(i, "w_out"), pl.BlockSpec((4, None, e // 4, tn_o), lambda a, b, k: (0, 0, 0, b)),
            jax.ShapeDtypeStruct((s, d), MXU), pl.BlockSpec((tm, tn_o), lambda a, b, k: (a, b)), (8, 128))
        x1, hn = res_rms_fwd(xs, o, g_post[i][None], g_ple_in[i][None], tr_row)
        stages = []
        if nxt is not None:
            stages = [(_stage_gather_d2d, nxt, ["w_out"] + mixer_names(nxt))]
            if i % 2 == 0:
                stages.insert(0, (_stage_gather_ici, nxt, ["gate", "ple"]))
        gpre = run_mm(
            stages, "mm_gate", "nn", (s // tm2, d // tn_o, 1), hn, pl.BlockSpec((tm2, d), lambda a, b, k: (a, 0)),
            full_w(i, "gate"), pl.BlockSpec((4, None, d // 4, tn_o), lambda a, b, k: (0, 0, 0, b)),
            jax.ShapeDtypeStruct((s, d), MXU), pl.BlockSpec((tm2, tn_o), lambda a, b, k: (a, b)), (8, 128))
        pe = p[i, 0]
        ev = run_mm(
            [(_stage_gather_d2d, nxt, ["w_in"])] if nxt is not None else [],
            "mm_ple", "nn", (s // tm, 4, 1), pe, pl.BlockSpec((tm, kp), lambda a, b, k: (a, 0)),
            full_w(i, "ple"), pl.BlockSpec((None, None, kp, d // 4), lambda a, b, k: (b, 0, 0, 0)),
            jax.ShapeDtypeStruct((s, d), MXU), pl.BlockSpec((tm, d // 4), lambda a, b, k: (a, b)), (8, 128))
        if nxt is not None:
            x2, h_next = ple_fwd(x1, gpre, ev, g_ple_out[i][None], g_pre[nxt][None], tr_row)
        else:
            x2 = ple_fwd_last(x1, gpre, ev, g_ple_out[i][None], tr_row)
        saved.append((xs, h, uz, y2, hst, o, x1, hn, gpre, ev))
        xs = x2

    dx, sq = loss_bwd(xs, loss_target[0], tr_row)
    d_gpre, d_gpost, d_gin, d_gout = [None] * depth, [None] * depth, [None] * depth, [None] * depth
    d_pool_b, d_pool_sc, d_lru_small = [None] * nmix, [None] * nmix, [None] * nmix
    ts = _tile(s, 1024)
    kc_idx = jnp.stack([me, ci]).astype(jnp.int32)
    full = {n: lax.empty(halves(big[n]).shape, F32) for n in names}
    w_sum = w_got = None

    def first_row(i, n):
        return i if n in ("w_in", "w_out", "gate", "ple") else (i // 2) * (ngrp if n == "pool" else LRU_HEADS)

    for i in reversed(range(depth)):
        j = i // 2
        prev = i + 1 if i + 1 < depth else None
        x0, h, uz, y2, hst, o, x1, hn, gpre, ev = saved[i]
        pe = p[i, 0]
        gl = {}
        de, dgp, d_gout[i] = ple_bwd(dx, gpre, ev, g_ple_out[i][None], tr_row)
        gl["ple"] = _matmul(
            "mm_dple", "tn", (1, 4, s // ts), pe, pl.BlockSpec((ts, kp), lambda a, b, k: (k, 0)),
            de, pl.BlockSpec((ts, d // 4), lambda a, b, k: (k, b)),
            jax.ShapeDtypeStruct((4, 1, kp, d // 4), WIRE), pl.BlockSpec((None, None, kp, d // 4), lambda a, b, k: (b, 0, 0, 0)),
            (kp, d // 4))
        tn_o = _tile(d, 1024)
        gl["gate"] = _matmul(
            "mm_dgate", "tn", (4, d // tn_o, 1), hn, pl.BlockSpec((s, d // 4), lambda a, b, k: (0, a)),
            dgp, pl.BlockSpec((s, tn_o), lambda a, b, k: (0, b)),
            jax.ShapeDtypeStruct((4, 1, d // 4, d), WIRE), pl.BlockSpec((None, None, d // 4, tn_o), lambda a, b, k: (a, 0, 0, b)),
            (8, 128))
        dhn = _matmul(
            "mm_dhn", "nt", (s // tm2, 4, 1), dgp, pl.BlockSpec((tm2, d), lambda a, b, k: (a, 0)),
            full_w(i, "gate"), pl.BlockSpec((None, None, d // 4, d), lambda a, b, k: (b, 0, 0, 0)),
            jax.ShapeDtypeStruct((s, d), MXU), pl.BlockSpec((tm2, d // 4), lambda a, b, k: (a, b)), (8, 128))
        dx1, do, d_gin[i], d_gpost[i] = post_bwd(dx, dhn, x1, g_ple_in[i][None], o, g_post[i][None], tr_row)
        gl["w_out"] = _matmul(
            "mm_dwout", "tn", (4, d // tn_o, 1), y2, pl.BlockSpec((s, e // 4), lambda a, b, k: (0, a)),
            do, pl.BlockSpec((s, tn_o), lambda a, b, k: (0, b)),
            jax.ShapeDtypeStruct((4, 1, e // 4, d), WIRE), pl.BlockSpec((None, None, e // 4, tn_o), lambda a, b, k: (a, 0, 0, b)),
            (8, 128))
        dy2 = _matmul(
            "mm_dy2", "nt", (s // tm2, 4, 1), do, pl.BlockSpec((tm2, d), lambda a, b, k: (a, 0)),
            full_w(i, "w_out"), pl.BlockSpec((None, None, e // 4, d), lambda a, b, k: (b, 0, 0, 0)),
            jax.ShapeDtypeStruct((s, e), MXU), pl.BlockSpec((tm2, e // 4), lambda a, b, k: (a, b)), (8, 128))
        riding = [_stage_scatter_ici([w_sum], [w_got])] if prev is not None else []
        if i % 2 == 0:
            (duz, gl["pool"], d_pool_b[j], d_pool_sc[j]), passed = pool_bwd(
                uz, dy2, full_w(i, "pool"), pool_b[j][None], pool_scale[j][None], tr_mix, hosted=riding)
        else:
            (duz, gl["wa"], gl["wx"], d_lru_small[j]), passed = lru_bwd(
                uz, hst, dy2, full_w(i, "wa"), full_w(i, "wx"), lru_small[j], tr_mix, hosted=riding)
        if prev is not None:
            full["w_in"] = sum_chips(w_sum, passed[0], kc_idx, full["w_in"], first_row(prev, "w_in"))

        def in_halves(g):
            return g.reshape(4, g.shape[1], 2, g.shape[2] // 2, g.shape[3])

        early = list(gl)
        eparts = [in_halves(gl[n]) for n in early]
        tmi = _tile(d, 1024)
        tni = _tile(e // 2, 1024)
        nslab = (e // 2) // tni
        gl["w_in"], from_sib = _matmul(
            "mm_dwin", "tn", (d // tmi, 4 * nslab, 1), h, pl.BlockSpec((s, tmi), lambda a, b, k: (0, a)),
            duz, pl.BlockSpec((None, s, tni), lambda a, b, k, nslab=nslab: (b // (2 * nslab), 0, b % (2 * nslab))),
            jax.ShapeDtypeStruct((4, 1, d, e // 2), WIRE),
            pl.BlockSpec((None, None, tmi, tni), lambda a, b, k, nslab=nslab: (b // nslab, 0, a, b % nslab)),
            (8, 128), hosted=[_stage_send_half(eparts, [lax.empty(g.shape[:2] + g.shape[3:], WIRE) for g in eparts])])
        esums = [add_halves(g, r, c_idx) for g, r in zip(eparts, from_sib)]
        wpart = in_halves(gl["w_in"])
        tnd = _tile(d, 1024)
        dh, passed = _matmul(
            "mm_dh", "nt", (s // tm, d // tnd, 4), duz, pl.BlockSpec((None, tm, e // 2), lambda a, b, k: (k // 2, a, k % 2)),
            full_w(i, "w_in"), pl.BlockSpec((None, None, tnd, e // 2), lambda a, b, k: (k, 0, b, 0)),
            jax.ShapeDtypeStruct((s, d), MXU), pl.BlockSpec((tm, tnd), lambda a, b, k: (a, b)), (tm, tnd),
            hosted=[_stage_scatter_ici(esums, [lax.empty((3,) + q.shape[1:], WIRE) for q in esums]),
                    _stage_send_half([wpart], [lax.empty(wpart.shape[:2] + wpart.shape[3:], WIRE)])])
        for n, q, got in zip(early, esums, passed[:len(early)]):
            full[n] = sum_chips(q, got, kc_idx, full[n], first_row(i, n))
        w_sum = add_halves(wpart, passed[len(early)], c_idx)
        w_got = lax.empty((3,) + w_sum.shape[1:], WIRE)
        dx, d_gpre[i] = rms_bwd_res(dx1, dh, x0, g_pre[i][None], tr_row)
    grad_x = dx[None]

    got, swapped = scatter_to_chips([w_sum], [full[n] for n in names],
                                    [(1, depth - 1) if n == "w_in" else (0, full[n].shape[0]) for n in names])
    full = dict(zip(names, swapped))
    full["w_in"] = share_halves([sum_chips(w_sum, got[0], kc_idx, full["w_in"], 0)], [(0, 1)])[0]
    grads = {n: full[n].reshape(big[n].shape) for n in names}

    def rows_e(a):
        return jnp.stack(a).reshape(-1, e) if isinstance(a, list) else a.reshape(-1, e)

    pack = [rows_e([g[0] for g in d_gpre]), rows_e([g[0] for g in d_gpost]), rows_e([g[0] for g in d_gin]), rows_e([g[0] for g in d_gout]),
            jnp.concatenate(d_pool_b, axis=0), jnp.concatenate(d_pool_sc, axis=0), jnp.concatenate(d_lru_small, axis=0),
            jnp.pad(sq, ((0, 0), (0, e - d)))]
    sizes = [a.shape[0] for a in pack]
    packed = jnp.concatenate(pack, axis=0)
    nrow = packed.shape[0]
    nrow_p = -(-nrow // 8) * 8
    packed = jnp.pad(packed, ((0, nrow_p - nrow), (0, 0)))
    total = sum_devices(all_gather_small(packed, "gather_grads").reshape(8, nrow_p, e))
    parts, off = [], 0
    for n_ in sizes:
        parts.append(total[off:off + n_])
        off += n_
    t_gpre, t_gpost, t_gin, t_gout, t_pb, t_psc, t_lru, t_sq = parts
    loss = 0.5 * jnp.sum(t_sq) / d
    t_lru = t_lru.reshape(nmix, 8, e)
    t_lru_loc = lax.dynamic_slice_in_dim(t_lru, me * ec, ec, axis=2)

    def big_update(name, w, m, v):
        shp = big[name].shape
        res4, _ = adamw(w.reshape(shp), grads[name], 0, m.reshape(shp), v.reshape(shp), 0, shp[0])
        return [a.reshape(w.shape) for a in res4]

    def small_update(w, g, m, v):
        shp = w.shape
        w2 = w.reshape(-1, shp[-1])
        dl, nm, nv = adamw_small(w2, g.reshape(w2.shape), m.reshape(w2.shape), v.reshape(w2.shape))
        return [g.reshape(shp), dl.reshape(shp), nm.reshape(shp), nv.reshape(shp)]

    res = {
        "w_in": big_update("w_in", w_in, m_w_in, v_w_in),
        "w_out": big_update("w_out", w_out, m_w_out, v_w_out),
        "g_pre": small_update(g_pre, t_gpre.reshape(depth, d), m_g_pre, v_g_pre),
        "g_post": small_update(g_post, t_gpost.reshape(depth, d), m_g_post, v_g_post),
        "pool_w": big_update("pool", pool_w, m_pool_w, v_pool_w),
        "pool_b": small_update(pool_b, t_pb, m_pool_b, v_pool_b),
        "pool_scale": small_update(pool_scale, t_psc, m_pool_scale, v_pool_scale),
        "conv_w": small_update(conv_w, t_lru_loc[:, :CONV_W], m_conv_w, v_conv_w),
        "conv_b": small_update(conv_b, t_lru_loc[:, 4], m_conv_b, v_conv_b),
        "lru_wa": big_update("wa", lru_wa, m_lru_wa, v_lru_wa),
        "lru_ba": small_update(lru_ba, t_lru_loc[:, 5], m_lru_ba, v_lru_ba),
        "lru_wx": big_update("wx", lru_wx, m_lru_wx, v_lru_wx),
        "lru_bx": small_update(lru_bx, t_lru_loc[:, 6], m_lru_bx, v_lru_bx),
        "lru_L": small_update(lru_L, t_lru_loc[:, 7], m_lru_L, v_lru_L),
        "w_ple": big_update("ple", w_ple, m_w_ple, v_w_ple),
        "w_ple_gate": big_update("gate", w_ple_gate, m_w_ple_gate, v_w_ple_gate),
        "g_ple_in": small_update(g_ple_in, t_gin.reshape(depth, d), m_g_ple_in, v_g_ple_in),
        "g_ple_out": small_update(g_ple_out, t_gout.reshape(depth, d), m_g_ple_out, v_g_ple_out),
    }
    order = ["w_in", "w_out", "g_pre", "g_post", "pool_w", "pool_b", "pool_scale", "conv_w", "conv_b", "lru_wa", "lru_ba",
             "lru_wx", "lru_bx", "lru_L", "w_ple", "w_ple_gate", "g_ple_in", "g_ple_out"]
    out = [loss, grad_x]
    for slot in range(4):
        out += [res[n][slot] for n in order]
    return tuple(out)
```

```python
import functools

import jax
import jax.numpy as jnp
from jax import lax
from jax.experimental import pallas as pl
from jax.experimental.pallas import tpu as pltpu

F32 = jnp.float32
MXU = jnp.bfloat16
WIRE = jnp.bfloat16
VMEM_LIMIT = 56 * 1024 * 1024
RMS_EPS = 1e-6
LRU_C = 8.0
POOL_WINDOWS = (2, 4, 8, 16)
MAXW = 16
CONV_W = 4
LRU_HEADS = 16
ADAM_LR, ADAM_B1, ADAM_B2, ADAM_EPS, ADAM_WD, ADAM_STEP = 0.001, 0.9, 0.999, 1e-08, 0.01, 10
MESHID = pl.DeviceIdType.MESH
ANY = pl.BlockSpec(memory_space=pl.ANY)


def _cp(sem=None):
    return pltpu.CompilerParams(dimension_semantics=sem, vmem_limit_bytes=VMEM_LIMIT)


def _sig(v):
    return 0.5 * jnp.tanh(0.5 * v) + 0.5


def _tile(n, pref):
    return pref if n % pref == 0 else n


_DN = {"nn": (((1,), (0,)), ((), ())), "nt": (((1,), (1,)), ((), ())), "tn": (((0,), (0,)), ((), ()))}


def _hosting_call(body, name, grid, in_specs, out_specs, out_shape, scratch_shapes, args, hosted, sem, aliases=None):
    n_in, n_out = len(args), len(out_shape)
    aliases = dict(aliases or {})
    if not hosted:
        res = pl.pallas_call(body, name=name, grid=grid, in_specs=in_specs, out_specs=out_specs, out_shape=out_shape,
                             scratch_shapes=scratch_shapes, input_output_aliases=aliases, compiler_params=_cp(sem))(*args)
        return list(res), []
    ro = [r for st in hosted for r in st["ro"]]
    bufs = [r for st in hosted for r in st["bufs"]]
    nro, nbuf = len(ro), len(bufs)
    nsem = sum(st["nsem"] for st in hosted)
    total = 1
    for g in grid:
        total *= g

    def wrapped(*refs):
        ins, ro_refs = refs[:n_in], refs[n_in:n_in + nro]
        outs = refs[n_in + nro + nbuf:n_in + nro + nbuf + n_out]
        buf_refs = refs[n_in + nro + nbuf + n_out:n_in + nro + 2 * nbuf + n_out]
        scr, ss, rs = refs[n_in + nro + 2 * nbuf + n_out:-2], refs[-2], refs[-1]
        step = 0
        for t, g in enumerate(grid):
            step = step * g + pl.program_id(t)

        def run_stages(what):
            r0 = b0 = s0 = 0
            for st in hosted:
                if what in st:
                    st[what](ro_refs[r0:r0 + len(st["ro"])], buf_refs[b0:b0 + len(st["bufs"])], ss, rs, s0)
                r0, b0, s0 = r0 + len(st["ro"]), b0 + len(st["bufs"]), s0 + st["nsem"]

        @pl.when(step == 0)
        def _():
            run_stages("start")

        if any("mid" in st for st in hosted):
            @pl.when(step == total // 2)
            def _():
                run_stages("mid")

        body(*ins, *outs, *scr)

        @pl.when(step == total - 1)
        def _():
            run_stages("finish")

    res = pl.pallas_call(
        wrapped, name=name, grid=grid,
        in_specs=list(in_specs) + [ANY] * (nro + nbuf), out_specs=list(out_specs) + [ANY] * nbuf,
        out_shape=list(out_shape) + [jax.ShapeDtypeStruct(x.shape, x.dtype) for x in bufs],
        scratch_shapes=list(scratch_shapes) + [pltpu.SemaphoreType.DMA((nsem,)), pltpu.SemaphoreType.DMA((nsem,))],
        input_output_aliases={**aliases, **{n_in + nro + t: n_out + t for t in range(nbuf)}},
        compiler_params=_cp(("arbitrary",) * len(grid)),
    )(*args, *ro, *bufs)
    return list(res[:n_out]), list(res[n_out:])


def _matmul(name, mode, grid, a, a_spec, b, b_spec, out_shape, out_spec, acc_shape, hosted=()):
    nk = grid[2]

    def body(a_ref, b_ref, o_ref, acc_ref):
        kk = pl.program_id(2)
        bv = b_ref[...]
        if bv.ndim == 3:
            bv = bv.reshape(bv.shape[0] * bv.shape[1], bv.shape[2])
        prod = lax.dot_general(a_ref[...].astype(MXU), bv.astype(MXU), _DN[mode], preferred_element_type=F32)
        if nk == 1:
            o_ref[...] = prod.astype(o_ref.dtype)
        else:
            @pl.when(kk == 0)
            def _():
                acc_ref[...] = prod

            @pl.when(kk > 0)
            def _():
                acc_ref[...] += prod

            @pl.when(kk == nk - 1)
            def _():
                o_ref[...] = acc_ref[...].astype(o_ref.dtype)

    outs, passed = _hosting_call(body, name, grid, [a_spec, b_spec], [out_spec], [out_shape], [pltpu.VMEM(acc_shape, F32)],
                                 [a, b], hosted, ("parallel", "parallel", "arbitrary"))
    return (outs[0], passed) if hosted else outs[0]


def _rows_call(name, body, ins, in_rows, outs, out_rows, n_rows, tr):
    def spec(shape, tiled):
        if tiled:
            return pl.BlockSpec((tr, shape[1]), lambda i: (i, 0))
        return pl.BlockSpec(shape, lambda i: (0, 0))

    return pl.pallas_call(
        body, name=name, grid=(n_rows // tr,),
        in_specs=[spec(a.shape, t) for a, t in zip(ins, in_rows)],
        out_specs=[spec(o.shape, t) for o, t in zip(outs, out_rows)],
        out_shape=outs, compiler_params=_cp(("arbitrary",)),
    )(*ins)


def _rstd(v):
    return lax.rsqrt(jnp.mean(v * v, axis=-1, keepdims=True) + RMS_EPS)


def _norm_bwd(v, g, dy):
    r = _rstd(v)
    n = v * r
    dn = dy * g
    dv = r * (dn - n * jnp.mean(dn * n, axis=-1, keepdims=True))
    return dv, jnp.sum(dy * n, axis=0, keepdims=True)


def _acc_rows(ref, val):
    @pl.when(pl.program_id(0) == 0)
    def _():
        ref[...] = val

    @pl.when(pl.program_id(0) > 0)
    def _():
        ref[...] += val


def rms_fwd(x, g, tr):
    def body(x_ref, g_ref, o_ref):
        v = x_ref[...]
        o_ref[...] = (v * _rstd(v) * g_ref[...]).astype(o_ref.dtype)

    return _rows_call("rms_fwd", body, [x, g], [True, False], [jax.ShapeDtypeStruct(x.shape, MXU)], [True], x.shape[0], tr)[0]


def res_rms_fwd(x, o, g, g_next, tr):
    def body(x_ref, o_ref, g_ref, gn_ref, y_ref, h_ref):
        v = o_ref[...].astype(F32)
        y = x_ref[...] + v * _rstd(v) * g_ref[...]
        y_ref[...] = y
        h_ref[...] = (y * _rstd(y) * gn_ref[...]).astype(h_ref.dtype)

    return _rows_call("res_rms_fwd", body, [x, o, g, g_next], [True, True, False, False],
                      [jax.ShapeDtypeStruct(x.shape, F32), jax.ShapeDtypeStruct(x.shape, MXU)], [True, True], x.shape[0], tr)


def ple_fwd(x1, gpre, e, g, g_next, tr):
    def body(x_ref, gp_ref, e_ref, g_ref, gn_ref, y_ref, h_ref):
        v = e_ref[...].astype(F32) * _sig(gp_ref[...].astype(F32))
        y = x_ref[...] + v * _rstd(v) * g_ref[...]
        y_ref[...] = y
        h_ref[...] = (y * _rstd(y) * gn_ref[...]).astype(h_ref.dtype)

    return _rows_call("ple_fwd", body, [x1, gpre, e, g, g_next], [True, True, True, False, False],
                      [jax.ShapeDtypeStruct(x1.shape, F32), jax.ShapeDtypeStruct(x1.shape, MXU)], [True, True], x1.shape[0], tr)


def ple_fwd_last(x1, gpre, e, g, tr):
    def body(x_ref, gp_ref, e_ref, g_ref, y_ref):
        v = e_ref[...].astype(F32) * _sig(gp_ref[...].astype(F32))
        y_ref[...] = x_ref[...] + v * _rstd(v) * g_ref[...]

    return _rows_call("ple_fwd_last", body, [x1, gpre, e, g], [True, True, True, False],
                      [jax.ShapeDtypeStruct(x1.shape, F32)], [True], x1.shape[0], tr)[0]


def loss_bwd(y, target, tr):
    d = y.shape[1]

    def body(y_ref, t_ref, dy_ref, sq_ref):
        diff = y_ref[...] - t_ref[...]
        dy_ref[...] = diff * (1.0 / d)
        _acc_rows(sq_ref, jnp.sum(diff * diff, axis=0, keepdims=True))

    return _rows_call("loss_bwd", body, [y, target], [True, True],
                      [jax.ShapeDtypeStruct(y.shape, F32), jax.ShapeDtypeStruct((1, d), F32)], [True, False], y.shape[0], tr)


def ple_bwd(dx2, gpre, e, g, tr):
    d = dx2.shape[1]

    def body(dx_ref, gp_ref, e_ref, g_ref, de_ref, dgp_ref, dg_ref):
        gate = _sig(gp_ref[...].astype(F32))
        ev = e_ref[...].astype(F32)
        dv, dg = _norm_bwd(ev * gate, g_ref[...], dx_ref[...])
        de_ref[...] = (dv * gate).astype(de_ref.dtype)
        dgp_ref[...] = (dv * ev * gate * (1.0 - gate)).astype(dgp_ref.dtype)
        _acc_rows(dg_ref, dg)

    return _rows_call("ple_bwd", body, [dx2, gpre, e, g], [True, True, True, False],
                      [jax.ShapeDtypeStruct(dx2.shape, MXU), jax.ShapeDtypeStruct(dx2.shape, MXU), jax.ShapeDtypeStruct((1, d), F32)],
                      [True, True, False], dx2.shape[0], tr)


def rms_bwd_res(dres, dh, x, g, tr):
    d = x.shape[1]

    def body(dr_ref, dh_ref, x_ref, g_ref, dx_ref, dg_ref):
        dv, dg = _norm_bwd(x_ref[...], g_ref[...], dh_ref[...].astype(F32))
        dx_ref[...] = dr_ref[...] + dv
        _acc_rows(dg_ref, dg)

    return _rows_call("rms_bwd_res", body, [dres, dh, x, g], [True, True, True, False],
                      [jax.ShapeDtypeStruct(x.shape, F32), jax.ShapeDtypeStruct((1, d), F32)], [True, False], x.shape[0], tr)


def post_bwd(dres, dh, x, g, o, g_o, tr):
    d = x.shape[1]

    def body(dr_ref, dh_ref, x_ref, g_ref, o_ref, go_ref, dx_ref, do_ref, dg_ref, dgo_ref):
        dv, dg = _norm_bwd(x_ref[...], g_ref[...], dh_ref[...].astype(F32))
        dxv = dr_ref[...] + dv
        dx_ref[...] = dxv
        dov, dgo = _norm_bwd(o_ref[...].astype(F32), go_ref[...], dxv)
        do_ref[...] = dov.astype(do_ref.dtype)
        _acc_rows(dg_ref, dg)
        _acc_rows(dgo_ref, dgo)

    return _rows_call("post_bwd", body, [dres, dh, x, g, o, g_o], [True, True, True, False, True, False],
                      [jax.ShapeDtypeStruct(x.shape, F32), jax.ShapeDtypeStruct(x.shape, MXU),
                       jax.ShapeDtypeStruct((1, d), F32), jax.ShapeDtypeStruct((1, d), F32)], [True, True, False, False], x.shape[0], tr)


def _trailing_sum(ext, w):
    s, k = ext, 1
    while k < w:
        s = s + pltpu.roll(s, k, 0)
        k *= 2
    return s


def _leading_sum(ext, w):
    n = ext.shape[0]
    s, k = ext, 1
    while k < w:
        s = s + pltpu.roll(s, n - k, 0)
        k *= 2
    return s


def _pool_inv_count(rb, tr, w, c):
    t = rb * tr + lax.broadcasted_iota(jnp.int32, (tr, c), 0)
    return 1.0 / jnp.minimum(t + 1, w).astype(F32)


def _pool_d(u_ref, up_ref, rb, tr, w):
    cur = u_ref[...].astype(F32)
    prev = jnp.where(rb > 0, up_ref[...].astype(F32), 0.0)
    ext = jnp.concatenate([prev, cur], axis=0)
    win = _trailing_sum(ext, w)[MAXW:]
    return win * _pool_inv_count(rb, tr, w, cur.shape[1]) - cur


def pool_fwd(uz, w_g, bias, scale, tr, hosted=()):
    _, s, e = uz.shape
    ng = len(POOL_WINDOWS)
    cg = e // ng
    nb = s // tr
    hb = tr // MAXW

    def body(u_ref, up_ref, z_ref, w_ref, b_ref, sc_ref, o_ref):
        g, rb = pl.program_id(0), pl.program_id(1)
        wmat = w_ref[...].reshape(cg, cg)
        for gg, win in enumerate(POOL_WINDOWS):
            @pl.when(g == gg)
            def _(win=win):
                d = _pool_d(u_ref, up_ref, rb, tr, win)
                y = (jnp.dot(d.astype(MXU), wmat, preferred_element_type=F32) + b_ref[...]) * sc_ref[...]
                z = z_ref[...].astype(F32)
                o_ref[...] = (y * z * _sig(z)).astype(o_ref.dtype)

    return _hosting_call(
        body, "pool_fwd", (ng, nb),
        [
            pl.BlockSpec((None, tr, cg), lambda g, r: (0, r, g)),
            pl.BlockSpec((None, MAXW, cg), lambda g, r: (0, jnp.maximum(r * hb - 1, 0), g)),
            pl.BlockSpec((None, tr, cg), lambda g, r: (1, r, g)),
            pl.BlockSpec((4, None, cg // 4, cg), lambda g, r: (0, g, 0, 0)),
            pl.BlockSpec((1, cg), lambda g, r: (0, g)),
            pl.BlockSpec((1, cg), lambda g, r: (0, g)),
        ],
        [pl.BlockSpec((tr, cg), lambda g, r: (r, g))], [jax.ShapeDtypeStruct((s, e), MXU)], [],
        [uz, uz, uz, w_g, bias, scale], hosted, ("parallel", "arbitrary"))


def pool_bwd(uz, dy2, w_g, bias, scale, tr, hosted=()):
    _, s, e = uz.shape
    ng = len(POOL_WINDOWS)
    cg = e // ng
    nb = s // tr
    hb = tr // MAXW

    def body(u_ref, up_ref, z_ref, dy_ref, w_ref, b_ref, sc_ref, duz_ref, gw_ref, db_ref, dsc_ref, acc_ref, carry_ref):
        g, step = pl.program_id(0), pl.program_id(1)
        rb = nb - 1 - step
        wmat = w_ref[...].reshape(cg, cg)
        for gg, win in enumerate(POOL_WINDOWS):
            @pl.when(g == gg)
            def _(win=win):
                d = _pool_d(u_ref, up_ref, rb, tr, win).astype(MXU)
                ypre = jnp.dot(d, wmat, preferred_element_type=F32) + b_ref[...]
                z = z_ref[...].astype(F32)
                sg = _sig(z)
                dy2v = dy_ref[...].astype(F32)
                dyv = dy2v * z * sg
                duz_ref[1] = (dy2v * ypre * sc_ref[...] * sg * (1.0 + z * (1.0 - sg))).astype(duz_ref.dtype)
                dypre = dyv * sc_ref[...]
                dsc = jnp.sum(dyv * ypre, axis=0, keepdims=True)
                dbv = jnp.sum(dypre, axis=0, keepdims=True)
                dypre_b = dypre.astype(MXU)
                dd = lax.dot_general(dypre_b, wmat, _DN["nt"], preferred_element_type=F32)
                gw = lax.dot_general(d, dypre_b, _DN["tn"], preferred_element_type=F32)
                q = dd * _pool_inv_count(rb, tr, win, cg)
                nxt = jnp.where(step > 0, carry_ref[...], 0.0)
                lead = _leading_sum(jnp.concatenate([q, nxt], axis=0), win)[:tr]
                duz_ref[0] = (lead - dd).astype(duz_ref.dtype)
                carry_ref[...] = q[:MAXW]

                @pl.when(step == 0)
                def _():
                    acc_ref[...] = gw
                    db_ref[...] = dbv
                    dsc_ref[...] = dsc

                @pl.when(step > 0)
                def _():
                    acc_ref[...] += gw
                    db_ref[...] += dbv
                    dsc_ref[...] += dsc

                @pl.when(step == nb - 1)
                def _():
                    gw_ref[...] = acc_ref[...].reshape(4, cg // 4, cg).astype(gw_ref.dtype)

    return _hosting_call(
        body, "pool_bwd", (ng, nb),
        [
            pl.BlockSpec((None, tr, cg), lambda g, r: (0, nb - 1 - r, g)),
            pl.BlockSpec((None, MAXW, cg), lambda g, r: (0, jnp.maximum((nb - 1 - r) * hb - 1, 0), g)),
            pl.BlockSpec((None, tr, cg), lambda g, r: (1, nb - 1 - r, g)),
            pl.BlockSpec((tr, cg), lambda g, r: (nb - 1 - r, g)),
            pl.BlockSpec((4, None, cg // 4, cg), lambda g, r: (0, g, 0, 0)),
            pl.BlockSpec((1, cg), lambda g, r: (0, g)),
            pl.BlockSpec((1, cg), lambda g, r: (0, g)),
        ],
        [
            pl.BlockSpec((2, tr, cg), lambda g, r: (0, nb - 1 - r, g)),
            pl.BlockSpec((4, None, cg // 4, cg), lambda g, r: (0, g, 0, 0)),
            pl.BlockSpec((1, cg), lambda g, r: (0, g)),
            pl.BlockSpec((1, cg), lambda g, r: (0, g)),
        ],
        [
            jax.ShapeDtypeStruct((2, s, e), MXU),
            jax.ShapeDtypeStruct(w_g.shape, WIRE),
            jax.ShapeDtypeStruct((1, e), F32),
            jax.ShapeDtypeStruct((1, e), F32),
        ],
        [pltpu.VMEM((cg, cg), F32), pltpu.VMEM((MAXW, cg), F32)],
        [uz, uz, uz, dy2, w_g, bias, scale], hosted, ("parallel", "arbitrary"))


HALO = 16


def _one_minus_sq(log_a, a):
    poly = (-2.0 * log_a) * (1.0 + log_a * (1.0 + log_a * (2.0 / 3.0)))
    return jnp.where(log_a > -0.01, poly, 1.0 - a * a)


def _softplus_neg(lam):
    t = jnp.exp(-jnp.abs(lam))
    log1p = jnp.where(t < 1e-3, t * (1.0 - t * (0.5 - t * (1.0 / 3.0))), jnp.log(1.0 + t))
    return jnp.maximum(-lam, 0.0) + log1p, _sig(-lam)


def _lru_gates(u_ref, up_ref, rb, sm_ref, wa, wx):
    cur = u_ref[...].astype(F32)
    prev = jnp.where(rb > 0, up_ref[...].astype(F32), 0.0)
    ext = jnp.concatenate([prev, cur], axis=0)
    taps = [cur] + [pltpu.roll(ext, k, 0)[HALO:] for k in range(1, CONV_W)]
    uc = sm_ref[CONV_W:CONV_W + 1, :]
    for k in range(CONV_W):
        uc = uc + taps[k] * sm_ref[CONV_W - 1 - k:CONV_W - k, :]
    ucb = uc.astype(MXU)
    r = _sig(jnp.dot(ucb, wa, preferred_element_type=F32) + sm_ref[5:6, :])
    ig = _sig(jnp.dot(ucb, wx, preferred_element_type=F32) + sm_ref[6:7, :])
    sp, sgn = _softplus_neg(sm_ref[7:8, :])
    log_a = r * (-LRU_C * sp)
    a = jnp.exp(log_a)
    mult = jnp.sqrt(jnp.maximum(_one_minus_sq(log_a, a), 0.0))
    return taps, uc, ucb, r, ig, sp, sgn, a, mult


LANES = 128


def _seg_scan(a, b, out_ref, scr, state, reverse):
    a_s, b_s, h_s, p_s = scr
    tr, c = a.shape
    seg = tr // 8
    nl = c // LANES
    for l in range(nl):
        a_s[l] = a[:, l * LANES:(l + 1) * LANES]
        b_s[l] = b[:, l * LANES:(l + 1) * LANES]
    h = [jnp.zeros((8, LANES), F32)] * nl
    pp = [jnp.ones((8, LANES), F32)] * nl
    for i in (range(seg - 1, -1, -1) if reverse else range(seg)):
        rows = pl.ds(i, 8, stride=seg)
        for l in range(nl):
            av = a_s[l, rows, :]
            h[l] = av * h[l] + b_s[l, rows, :]
            pp[l] = av * pp[l]
            h_s[l, pl.ds(8 * i, 8), :] = h[l]
            p_s[l, pl.ds(8 * i, 8), :] = pp[l]
    leaving = []
    sub = lax.broadcasted_iota(jnp.int32, (8, LANES), 0)
    for l in range(nl):
        lanes = slice(l * LANES, (l + 1) * LANES)
        st = state[:, lanes]
        entering = jnp.zeros((8, LANES), F32)
        for sgm in (range(7, -1, -1) if reverse else range(8)):
            entering = jnp.where(sub == sgm, st, entering)
            st = h[l][sgm:sgm + 1, :] + pp[l][sgm:sgm + 1, :] * st
        leaving.append(st)
        for i in range(seg):
            rows = pl.ds(8 * i, 8)
            h_s[l, rows, :] = h_s[l, rows, :] + p_s[l, rows, :] * entering
        for sgm in range(8):
            for t0 in range(0, seg, 8):
                out_ref[pl.ds(sgm * seg + t0, 8), lanes] = h_s[l, pl.ds(8 * t0 + sgm, 8, stride=8), :]
    return jnp.concatenate(leaving, axis=1)


def lru_fwd(uz, wa_g, wx_g, small, tr, hosted=()):
    _, s, e = uz.shape
    cb = e // LRU_HEADS
    nb = s // tr
    hb = tr // HALO

    def body(u_ref, up_ref, z_ref, wa_ref, wx_ref, sm_ref, o_ref, h_ref, s0, s1, s2, s3, carry_ref):
        rb = pl.program_id(1)
        wa = wa_ref[...].reshape(cb, cb)
        wx = wx_ref[...].reshape(cb, cb)
        _, uc, _, _, ig, _, _, a, mult = _lru_gates(u_ref, up_ref, rb, sm_ref, wa, wx)
        start = jnp.where(rb > 0, carry_ref[0:1, :], 0.0)
        last = _seg_scan(a, mult * ig * uc, h_ref, (s0, s1, s2, s3), start, False)
        carry_ref[...] = jnp.broadcast_to(last, carry_ref.shape)
        z = z_ref[...].astype(F32)
        o_ref[...] = (h_ref[...] * z * _sig(z)).astype(o_ref.dtype)

    wspec = pl.BlockSpec((4, None, cb // 4, cb), lambda h, r: (0, h, 0, 0))
    return _hosting_call(
        body, "lru_fwd", (LRU_HEADS, nb),
        [
            pl.BlockSpec((None, tr, cb), lambda h, r: (0, r, h)),
            pl.BlockSpec((None, HALO, cb), lambda h, r: (0, jnp.maximum(r * hb - 1, 0), h)),
            pl.BlockSpec((None, tr, cb), lambda h, r: (1, r, h)),
            wspec, wspec,
            pl.BlockSpec((8, cb), lambda h, r: (0, h)),
        ],
        [pl.BlockSpec((tr, cb), lambda h, r: (r, h)), pl.BlockSpec((tr, cb), lambda h, r: (r, h))],
        [jax.ShapeDtypeStruct((s, e), MXU), jax.ShapeDtypeStruct((s, e), F32)],
        [pltpu.VMEM((cb // LANES, tr, LANES), F32)] * 4 + [pltpu.VMEM((8, cb), F32)],
        [uz, uz, uz, wa_g, wx_g, small], hosted, ("parallel", "arbitrary"))


def lru_bwd(uz, hst, dy2, wa_g, wx_g, small, tr, hosted=()):
    _, s, e = uz.shape
    cb = e // LRU_HEADS
    nb = s // tr
    hb = tr // HALO

    def body(u_ref, up_ref, z_ref, h_ref, hp_ref, dy_ref, wa_ref, wx_ref, sm_ref,
             duz_ref, gwa_ref, gwx_ref, dsm_ref, s0, s1, s2, s3, g_s, acc_a, acc_x, gcar, acar, dcar):
        step = pl.program_id(1)
        rb = nb - 1 - step
        wa = wa_ref[...].reshape(cb, cb)
        wx = wx_ref[...].reshape(cb, cb)
        taps, uc, ucb, r, ig, sp, sgn, a, mult = _lru_gates(u_ref, up_ref, rb, sm_ref, wa, wx)
        row = lax.broadcasted_iota(jnp.int32, a.shape, 0)
        z = z_ref[...].astype(F32)
        sg = _sig(z)
        dy2v = dy_ref[...].astype(F32)
        hv = h_ref[...]
        duz_ref[1] = (dy2v * hv * sg * (1.0 + z * (1.0 - sg))).astype(duz_ref.dtype)
        a_next = jnp.where(row == tr - 1, jnp.where(step > 0, acar[0:1, :], 0.0), pltpu.roll(a, tr - 1, 0))
        g_first = _seg_scan(a_next, dy2v * z * sg, g_s, (s0, s1, s2, s3), jnp.where(step > 0, gcar[0:1, :], 0.0), True)
        gcar[...] = jnp.broadcast_to(g_first, gcar.shape)
        acar[...] = jnp.broadcast_to(a[0:1, :], acar.shape)
        gv = g_s[...]
        h_before = jnp.where(rb > 0, hp_ref[HALO - 1:HALO, :], 0.0)
        h_prev = jnp.where(row == 0, h_before, pltpu.roll(hv, 1, 0))
        da = gv * h_prev
        gu = gv * uc
        dmult = gu * ig
        dig = gu * mult
        dlog_a = da * a - dmult * jnp.where(mult > 0.0, a * a / mult, 0.0)
        dra = dlog_a * (-LRU_C) * sp * r * (1.0 - r)
        dix = dig * ig * (1.0 - ig)
        dl = jnp.sum(dlog_a * r, axis=0, keepdims=True) * (LRU_C * sgn)
        dra_b, dix_b = dra.astype(MXU), dix.astype(MXU)
        duc = (gv * mult * ig + lax.dot_general(dra_b, wa, _DN["nt"], preferred_element_type=F32)
               + lax.dot_general(dix_b, wx, _DN["nt"], preferred_element_type=F32))
        gwa = lax.dot_general(ucb, dra_b, _DN["tn"], preferred_element_type=F32)
        gwx = lax.dot_general(ucb, dix_b, _DN["tn"], preferred_element_type=F32)
        ext = jnp.concatenate([duc, jnp.where(step > 0, dcar[...], 0.0)], axis=0)
        n = ext.shape[0]
        du = duc * sm_ref[CONV_W - 1:CONV_W, :]
        for k in range(1, CONV_W):
            du = du + pltpu.roll(ext, n - k, 0)[:tr] * sm_ref[CONV_W - 1 - k:CONV_W - k, :]
        duz_ref[0] = du.astype(duz_ref.dtype)
        dcar[...] = duc[:HALO]
        rows = [jnp.sum(duc * taps[CONV_W - 1 - k], axis=0, keepdims=True) for k in range(CONV_W)]
        rows += [jnp.sum(duc, axis=0, keepdims=True), jnp.sum(dra, axis=0, keepdims=True),
                 jnp.sum(dix, axis=0, keepdims=True), dl]

        @pl.when(step == 0)
        def _():
            acc_a[...] = gwa
            acc_x[...] = gwx
            for k, rv in enumerate(rows):
                dsm_ref[k:k + 1, :] = rv

        @pl.when(step > 0)
        def _():
            acc_a[...] += gwa
            acc_x[...] += gwx
            for k, rv in enumerate(rows):
                dsm_ref[k:k + 1, :] += rv

        @pl.when(step == nb - 1)
        def _():
            gwa_ref[...] = acc_a[...].reshape(4, cb // 4, cb).astype(gwa_ref.dtype)
            gwx_ref[...] = acc_x[...].reshape(4, cb // 4, cb).astype(gwx_ref.dtype)

    wspec = pl.BlockSpec((4, None, cb // 4, cb), lambda h, r: (0, h, 0, 0))
    blk = pl.BlockSpec((tr, cb), lambda h, r: (nb - 1 - r, h))
    return _hosting_call(
        body, "lru_bwd", (LRU_HEADS, nb),
        [
            pl.BlockSpec((None, tr, cb), lambda h, r: (0, nb - 1 - r, h)),
            pl.BlockSpec((None, HALO, cb), lambda h, r: (0, jnp.maximum((nb - 1 - r) * hb - 1, 0), h)),
            pl.BlockSpec((None, tr, cb), lambda h, r: (1, nb - 1 - r, h)),
            blk,
            pl.BlockSpec((HALO, cb), lambda h, r: (jnp.maximum((nb - 1 - r) * hb - 1, 0), h)),
            blk,
            wspec, wspec,
            pl.BlockSpec((8, cb), lambda h, r: (0, h)),
        ],
        [
            pl.BlockSpec((2, tr, cb), lambda h, r: (0, nb - 1 - r, h)),
            wspec, wspec,
            pl.BlockSpec((8, cb), lambda h, r: (0, h)),
        ],
        [
            jax.ShapeDtypeStruct((2, s, e), MXU),
            jax.ShapeDtypeStruct(wa_g.shape, WIRE),
            jax.ShapeDtypeStruct(wx_g.shape, WIRE),
            jax.ShapeDtypeStruct((8, e), F32),
        ],
        ([pltpu.VMEM((cb // LANES, tr, LANES), F32)] * 4 + [pltpu.VMEM((tr, cb), F32)]
         + [pltpu.VMEM((cb, cb), F32)] * 2 + [pltpu.VMEM((8, cb), F32)] * 2 + [pltpu.VMEM((HALO, cb), F32)]),
        [uz, uz, uz, hst, hst, dy2, wa_g, wx_g, small], hosted, ("parallel", "arbitrary"))


def _place():
    x, y, c = lax.axis_index("x"), lax.axis_index("y"), lax.axis_index("c")
    chips = [(1 - x, y), (x, 1 - y), (1 - x, 1 - y)]
    return x, y, c, chips


def _rcopy(src, dst, send_sems, recv_sems, k, to):
    return pltpu.make_async_remote_copy(src_ref=src, dst_ref=dst, send_sem=send_sems.at[k], recv_sem=recv_sems.at[k],
                                        device_id=to, device_id_type=MESHID)


def _stage_gather_ici(bufs):
    n = len(bufs)

    def quarter(ref, chip, c, q):
        rq = ref.shape[3] // 2
        return ref.at[2 * chip[0] + chip[1], :, c, pl.ds(q * rq, rq)]

    def copies(refs, ss, rs, off, a, sending):
        x, y, c, _ = _place()
        me, xn, yn, dg = (x, y), (1 - x, y), (x, 1 - y), (1 - x, 1 - y)
        plan = [(0, me, xn, 0), (1, me, xn, 1), (3, me, yn, 1), (2, me, yn, 0),
                (4, xn, yn, 0), (5, yn, xn, 1)]
        if not sending:
            plan = [(0, xn, xn, 0), (1, xn, xn, 1), (3, yn, yn, 1), (2, yn, yn, 0), (4, dg, yn, 0), (5, dg, xn, 1)]
        out = []
        for k, owner, to, q in plan:
            blk = quarter(refs[a], owner, c, q)
            out.append(_rcopy(blk, blk, ss, rs, off + 6 * a + k, (*to, c)))
        return out

    def start(ro, refs, ss, rs, off):
        for a in range(n):
            for cp in copies(refs, ss, rs, off, a, True)[:4]:
                cp.start()

    def mid(ro, refs, ss, rs, off):
        for a in range(n):
            got, out = copies(refs, ss, rs, off, a, False), copies(refs, ss, rs, off, a, True)
            got[0].wait_recv()
            out[4].start()
            got[2].wait_recv()
            out[5].start()

    def finish(ro, refs, ss, rs, off):
        for a in range(n):
            got = copies(refs, ss, rs, off, a, False)
            for k in (1, 3, 4, 5):
                got[k].wait_recv()
            for cp in copies(refs, ss, rs, off, a, True):
                cp.wait_send()

    return dict(ro=[], bufs=list(bufs), nsem=6 * n, start=start, mid=mid, finish=finish)


def _stage_gather_d2d(bufs):
    n = len(bufs)

    def copies(refs, ss, rs, off, sending):
        x, y, c, chips = _place()
        out = []
        for a in range(n):
            for jj, ch in enumerate(chips):
                blk = refs[a].at[2 * ch[0] + ch[1], :, c if sending else 1 - c]
                out.append(_rcopy(blk, blk, ss, rs, off + 3 * a + jj, (x, y, 1 - c)))
        return out

    def start(ro, refs, ss, rs, off):
        for cp in copies(refs, ss, rs, off, True):
            cp.start()

    def finish(ro, refs, ss, rs, off):
        for cp in copies(refs, ss, rs, off, False):
            cp.wait_recv()
        for cp in copies(refs, ss, rs, off, True):
            cp.wait_send()

    return dict(ro=[], bufs=list(bufs), nsem=3 * n, start=start, finish=finish)


def _stage_scatter_ici(parts, gots):
    n = len(parts)

    def copies(ro, refs, ss, rs, off):
        x, y, c, chips = _place()
        return [_rcopy(ro[a].at[2 * ch[0] + ch[1]], refs[a].at[jj], ss, rs, off + 3 * a + jj, (*ch, c))
                for a in range(n) for jj, ch in enumerate(chips)]

    def start(ro, refs, ss, rs, off):
        for cp in copies(ro, refs, ss, rs, off):
            cp.start()

    def finish(ro, refs, ss, rs, off):
        for cp in copies(ro, refs, ss, rs, off):
            cp.wait()

    return dict(ro=list(parts), bufs=list(gots), nsem=3 * n, start=start, finish=finish)


def _stage_send_half(grads, lands):
    n = len(grads)

    def copies(ro, refs, ss, rs, off):
        x, y, c, _ = _place()
        return [_rcopy(ro[a].at[:, :, 1 - c], refs[a], ss, rs, off + a, (x, y, 1 - c)) for a in range(n)]

    def start(ro, refs, ss, rs, off):
        for cp in copies(ro, refs, ss, rs, off):
            cp.start()

    def finish(ro, refs, ss, rs, off):
        for cp in copies(ro, refs, ss, rs, off):
            cp.wait()

    return dict(ro=list(grads), bufs=list(lands), nsem=n, start=start, finish=finish)


def cast_into_slab(w, first, l, k_idx):
    _, r, c = w.shape
    rh = r // 2
    lb, tr = _block_lr(l, rh, c)
    nbh = rh // tr
    assert first % lb == 0

    def body(k_ref, w_ref, o_ref):
        o_ref[...] = w_ref[...].astype(o_ref.dtype)

    return pl.pallas_call(
        body, name="cast_into_slab",
        grid_spec=pltpu.PrefetchScalarGridSpec(
            num_scalar_prefetch=1, grid=(l // lb, 2, nbh),
            in_specs=[pl.BlockSpec((lb, tr, c), lambda i, h, b, k_ref: (first // lb + i, h * nbh + b, 0))],
            out_specs=pl.BlockSpec((None, lb, None, tr, c), lambda i, h, b, k_ref: (k_ref[0], i, h, b, 0))),
        out_shape=jax.ShapeDtypeStruct((4, l, 2, rh, c), WIRE),
        compiler_params=_cp(("parallel", "parallel", "parallel")),
    )(k_idx, w)


def gather_weights(bufs):
    n = len(bufs)
    ici = [_stage_gather_ici([b]) for b in bufs]
    d2d = [_stage_gather_d2d([b]) for b in bufs]
    per = ici[0]["nsem"] + d2d[0]["nsem"]

    def body(*refs):
        outs = refs[n:2 * n]
        ss, rs = refs[2 * n:]
        for what in ("start", "mid"):
            for a in range(n):
                ici[a][what]([], [outs[a]], ss, rs, per * a)
        for a in range(n):
            ici[a]["finish"]([], [outs[a]], ss, rs, per * a)
            d2d[a]["start"]([], [outs[a]], ss, rs, per * a + ici[a]["nsem"])
        for a in range(n):
            d2d[a]["finish"]([], [outs[a]], ss, rs, per * a + ici[a]["nsem"])

    return pl.pallas_call(
        body, name="gather_weights",
        in_specs=[ANY] * n, out_specs=[ANY] * n,
        out_shape=[jax.ShapeDtypeStruct(a.shape, a.dtype) for a in bufs],
        scratch_shapes=[pltpu.SemaphoreType.DMA((per * n,)), pltpu.SemaphoreType.DMA((per * n,))],
        input_output_aliases={a: a for a in range(n)},
        compiler_params=pltpu.CompilerParams(has_side_effects=True),
    )(*bufs)


def all_gather_small(v, name):
    m_per, n = v.shape

    def body(x_ref, out_ref, send_sems, recv_sems, local_sem):
        x, y, c, chips = _place()
        me, sibling = (x, y, c), (x, y, 1 - c)

        def rows(px, py, pc):
            return out_ref.at[pl.ds((4 * px + 2 * py + pc) * m_per, m_per), :]

        def copy(k, block, to, src=None):
            return _rcopy(rows(*block) if src is None else src, rows(*block), send_sems, recv_sems, k, to)

        mine = pltpu.make_async_copy(x_ref, rows(*me), local_sem)
        mine.start()
        first = [copy(0, me, sibling, src=x_ref)]
        first += [copy(1 + jj, me, (*chip, c), src=x_ref) for jj, chip in enumerate(chips)]
        for cp in first:
            cp.start()
        passed = [copy(4 + jj, (*chip, c), sibling) for jj, chip in enumerate(chips)]
        for jj, chip in enumerate(chips):
            copy(1 + jj, (*chip, c), me).wait_recv()
            passed[jj].start()
        copy(0, sibling, me).wait_recv()
        for jj, chip in enumerate(chips):
            copy(4 + jj, (*chip, 1 - c), me).wait_recv()
        for cp in first + passed:
            cp.wait_send()
        mine.wait()

    return pl.pallas_call(
        body, name=name,
        out_shape=jax.ShapeDtypeStruct((8 * m_per, n), v.dtype),
        in_specs=[pl.BlockSpec(memory_space=pltpu.VMEM)],
        out_specs=pl.BlockSpec(memory_space=pltpu.VMEM),
        scratch_shapes=[pltpu.SemaphoreType.DMA((7,)), pltpu.SemaphoreType.DMA((7,)), pltpu.SemaphoreType.DMA],
        compiler_params=pltpu.CompilerParams(vmem_limit_bytes=VMEM_LIMIT),
    )(v)


def sum_devices(g):
    def body(g_ref, o_ref):
        acc = g_ref[0]
        for d in range(1, 8):
            acc = acc + g_ref[d]
        o_ref[...] = acc

    return pl.pallas_call(body, name="sum_devices", out_shape=jax.ShapeDtypeStruct(g.shape[1:], g.dtype),
                          compiler_params=pltpu.CompilerParams(vmem_limit_bytes=VMEM_LIMIT))(g)


def send_other_half(grads):
    n = len(grads)

    def body(*refs):
        ins, outs = refs[:n], refs[n:2 * n]
        send_sems, recv_sems = refs[2 * n:]
        x, y, c, _ = _place()
        sib = (x, y, 1 - c)
        cps = [_rcopy(ins[a].at[:, :, 1 - c], outs[a], send_sems, recv_sems, a, sib) for a in range(n)]
        for cp in cps:
            cp.start()
        for cp in cps:
            cp.wait()

    return pl.pallas_call(
        body, name="send_other_half", in_specs=[ANY] * n, out_specs=[ANY] * n,
        out_shape=[jax.ShapeDtypeStruct(g.shape[:2] + g.shape[3:], g.dtype) for g in grads],
        scratch_shapes=[pltpu.SemaphoreType.DMA((n,)), pltpu.SemaphoreType.DMA((n,))],
        compiler_params=pltpu.CompilerParams(has_side_effects=True),
    )(*grads)


def scatter_to_chips(parts, fulls, spans):
    n, m = len(parts), len(fulls)

    def body(*refs):
        ins, outs, shared = refs[:n], refs[n + m:2 * n + m], refs[2 * n + m:2 * n + 2 * m]
        send_sems, recv_sems = refs[2 * n + 2 * m:]
        x, y, c, chips = _place()
        sib = (x, y, 1 - c)
        cps = []
        for a in range(n):
            for jj, ch in enumerate(chips):
                cps.append(_rcopy(ins[a].at[2 * ch[0] + ch[1]], outs[a].at[jj], send_sems, recv_sems, 3 * a + jj, (*ch, c)))

        def blk(a, half):
            return shared[a].at[pl.ds(spans[a][0], spans[a][1]), half]

        swaps = [_rcopy(blk(a, c), blk(a, c), send_sems, recv_sems, 3 * n + a, sib) for a in range(m)]
        for cp in cps + swaps:
            cp.start()
        for cp in cps:
            cp.wait()
        for a in range(m):
            _rcopy(blk(a, 1 - c), blk(a, 1 - c), send_sems, recv_sems, 3 * n + a, sib).wait_recv()
        for cp in swaps:
            cp.wait_send()

    res = pl.pallas_call(
        body, name="scatter_to_chips", in_specs=[ANY] * (n + m), out_specs=[ANY] * (n + m),
        out_shape=[jax.ShapeDtypeStruct((3,) + p.shape[1:], p.dtype) for p in parts]
        + [jax.ShapeDtypeStruct(f.shape, f.dtype) for f in fulls],
        scratch_shapes=[pltpu.SemaphoreType.DMA((3 * n + m,)), pltpu.SemaphoreType.DMA((3 * n + m,))],
        input_output_aliases={n + a: n + a for a in range(m)},
        compiler_params=pltpu.CompilerParams(has_side_effects=True),
    )(*parts, *fulls)
    return list(res[:n]), list(res[n:])


def share_halves(bufs, spans):
    n = len(bufs)

    def body(*refs):
        outs = refs[n:2 * n]
        send_sems, recv_sems = refs[2 * n:]
        x, y, c, _ = _place()
        sib = (x, y, 1 - c)

        def blk(a, half):
            return outs[a].at[pl.ds(spans[a][0], spans[a][1]), half]

        cps = [_rcopy(blk(a, c), blk(a, c), send_sems, recv_sems, a, sib) for a in range(n)]
        for cp in cps:
            cp.start()
        for a in range(n):
            _rcopy(blk(a, 1 - c), blk(a, 1 - c), send_sems, recv_sems, a, sib).wait_recv()
        for cp in cps:
            cp.wait_send()

    return pl.pallas_call(
        body, name="share_halves", in_specs=[ANY] * n, out_specs=[ANY] * n,
        out_shape=[jax.ShapeDtypeStruct(b.shape, b.dtype) for b in bufs],
        scratch_shapes=[pltpu.SemaphoreType.DMA((n,)), pltpu.SemaphoreType.DMA((n,))],
        input_output_aliases={a: a for a in range(n)},
        compiler_params=pltpu.CompilerParams(has_side_effects=True),
    )(*bufs)


def _block_rows(r, c, itemsize, budget=1 << 20):
    tr = r
    while tr * c * itemsize > budget and tr % 16 == 0:
        tr //= 2
    return tr


def _block_lr(l, r, c, budget=4 << 20):
    tr = _block_rows(r, c, 4, budget)
    lb = 1
    if tr == r:
        while l % (2 * lb) == 0 and 2 * lb * r * c * 4 <= budget:
            lb *= 2
    return lb, tr


def add_halves(g, got, c_idx):
    k4, l, _, rh, cc = g.shape
    lb, tr = _block_lr(l, rh, cc)

    def body(c_ref, g_ref, r_ref, o_ref):
        o_ref[...] = (g_ref[...].astype(F32) + r_ref[...].astype(F32)).astype(o_ref.dtype)

    return pl.pallas_call(
        body, name="add_halves",
        grid_spec=pltpu.PrefetchScalarGridSpec(
            num_scalar_prefetch=1, grid=(k4, l // lb, rh // tr),
            in_specs=[pl.BlockSpec((None, lb, None, tr, cc), lambda k, i, b, c_ref: (k, i, c_ref[0], b, 0)),
                      pl.BlockSpec((None, lb, tr, cc), lambda k, i, b, c_ref: (k, i, b, 0))],
            out_specs=pl.BlockSpec((None, lb, tr, cc), lambda k, i, b, c_ref: (k, i, b, 0))),
        out_shape=jax.ShapeDtypeStruct(got.shape, WIRE),
        compiler_params=_cp(("parallel", "parallel", "parallel")),
    )(c_idx, g, got)


def sum_chips(own, got, kc_idx, full, first):
    _, l, rh, cc = own.shape
    lb, tr = _block_lr(l, rh, cc, 2 << 20)
    assert first % lb == 0

    def body(k_ref, o_ref, r_ref, _full, s_ref):
        s_ref[...] = ((o_ref[...].astype(F32) + r_ref[0].astype(F32)) + r_ref[1].astype(F32)) + r_ref[2].astype(F32)

    return pl.pallas_call(
        body, name="sum_chips",
        grid_spec=pltpu.PrefetchScalarGridSpec(
            num_scalar_prefetch=1, grid=(l // lb, rh // tr),
            in_specs=[pl.BlockSpec((None, lb, tr, cc), lambda i, b, k_ref: (k_ref[0], i, b, 0)),
                      pl.BlockSpec((3, lb, tr, cc), lambda i, b, k_ref: (0, i, b, 0)),
                      ANY],
            out_specs=pl.BlockSpec((lb, None, tr, cc), lambda i, b, k_ref: (first // lb + i, k_ref[1], b, 0))),
        out_shape=jax.ShapeDtypeStruct(full.shape, F32),
        input_output_aliases={3: 0},
        compiler_params=_cp(("parallel", "parallel")),
    )(kc_idx, own, got, full)


def _adam_math(w, g, m, v):
    m = ADAM_B1 * m + (1.0 - ADAM_B1) * g
    v = ADAM_B2 * v + (1.0 - ADAM_B2) * (g * g)
    m_hat = m / (1.0 - ADAM_B1 ** ADAM_STEP)
    v_hat = v / (1.0 - ADAM_B2 ** ADAM_STEP)
    delta = -ADAM_LR * (m_hat / (jnp.sqrt(v_hat) + ADAM_EPS) + ADAM_WD * w)
    return delta, m, v


def adamw(w, g, g_first, m, v, lo, hi, prev=None, hosted=()):
    l, r, c = w.shape
    tr = _block_rows(r, c, 4, 2 << 20)
    prev = list(prev or [])

    def body(w_ref, g_ref, m_ref, v_ref, *rest):
        go_ref, d_ref, mo_ref, vo_ref = rest[len(prev):]
        gv = g_ref[...]
        go_ref[...] = gv
        d_ref[...], mo_ref[...], vo_ref[...] = _adam_math(w_ref[...], gv, m_ref[...], v_ref[...])

    spec = pl.BlockSpec((None, tr, c), lambda i, b: (lo + i, b, 0))
    gspec = pl.BlockSpec((None, tr, c), lambda i, b: (g_first + i, b, 0))
    return _hosting_call(
        body, "adamw", (hi - lo, r // tr), [spec, gspec, spec, spec] + [ANY] * len(prev), [spec] * 4,
        [jax.ShapeDtypeStruct(w.shape, F32)] * 4, [], [w, g, m, v] + prev, hosted, ("parallel", "parallel"),
        aliases={4 + t: t for t in range(len(prev))})


def adamw_small(w, g, m, v):
    def body(w_ref, g_ref, m_ref, v_ref, d_ref, mo_ref, vo_ref):
        d_ref[...], mo_ref[...], vo_ref[...] = _adam_math(w_ref[...], g_ref[...], m_ref[...], v_ref[...])

    return pl.pallas_call(body, name="adamw_small", out_shape=[jax.ShapeDtypeStruct(w.shape, F32)] * 3)(w, g, m, v)


def kernel(x, p, w_in, w_out, g_pre, g_post, pool_w, pool_b, pool_scale, conv_w, conv_b, lru_wa, lru_ba, lru_wx, lru_bx, lru_L, w_ple, w_ple_gate, g_ple_in, g_ple_out, loss_target, m_w_in, m_w_out, m_g_pre, m_g_post, m_pool_w, m_pool_b, m_pool_scale, m_conv_w, m_conv_b, m_lru_wa, m_lru_ba, m_lru_wx, m_lru_bx, m_lru_L, m_w_ple, m_w_ple_gate, m_g_ple_in, m_g_ple_out, v_w_in, v_w_out, v_g_pre, v_g_post, v_pool_w, v_pool_b, v_pool_scale, v_conv_w, v_conv_b, v_lru_wa, v_lru_ba, v_lru_wx, v_lru_bx, v_lru_L, v_w_ple, v_w_ple_gate, v_g_ple_in, v_g_ple_out):
    depth = w_in.shape[0]
    _, s, d = x.shape
    e = 2 * d
    kp = p.shape[-1]
    nmix = pool_w.shape[0]
    ngrp = pool_w.shape[1]
    cg = e // ngrp
    cb = e // LRU_HEADS
    xi, yi, ci = lax.axis_index("x"), lax.axis_index("y"), lax.axis_index("c")
    me = 2 * xi + yi
    c_idx = jnp.reshape(ci, (1,)).astype(jnp.int32)
    k_idx = jnp.reshape(me, (1,)).astype(jnp.int32)
    tr_row = _tile(s, 512)
    tr_mix = _tile(s, 512)
    tr_lru = _tile(s, 1024)
    tm = _tile(s, 1024)
    tm2 = _tile(s, 2048)

    def halves(a):
        return a.reshape(a.shape[0], 2, a.shape[1] // 2, a.shape[2])

    big = {
        "w_in": w_in, "w_out": w_out, "gate": w_ple_gate, "ple": w_ple,
        "pool": pool_w.reshape(nmix * ngrp, cg // 4, cg),
        "wa": lru_wa.reshape(nmix * LRU_HEADS, cb // 4, cb), "wx": lru_wx.reshape(nmix * LRU_HEADS, cb // 4, cb),
    }
    names = list(big)

    def layer_shards(i):
        sh = {n: (big[n], i, 1) for n in ("w_in", "w_out", "gate", "ple")}
        if i % 2 == 0:
            sh["pool"] = (big["pool"], (i // 2) * ngrp, ngrp)
        else:
            sh["wa"], sh["wx"] = (big["wa"], (i // 2) * LRU_HEADS, LRU_HEADS), (big["wx"], (i // 2) * LRU_HEADS, LRU_HEADS)
        return sh

    def mixer_names(i):
        return ["pool"] if i % 2 == 0 else ["wa", "wx"]

    wbuf = [{n: cast_into_slab(*wfl, k_idx) for n, wfl in layer_shards(i).items()} for i in range(depth)]
    first = list(wbuf[0])
    wbuf[0] = dict(zip(first, gather_weights([wbuf[0][n] for n in first])))

    def full_w(i, n):
        b = wbuf[i][n]
        return b.reshape(b.shape[0], b.shape[1], 2 * b.shape[3], b.shape[4])

    def stages_of(specs):
        return [mk([wbuf[l][n] for n in nms]) for mk, l, nms in specs]

    def keep(specs, new):
        for (l, n), b in zip([(l, n) for _, l, nms in specs for n in nms], new):
            wbuf[l][n] = b

    def run_mm(specs, *args):
        if not specs:
            return _matmul(*args)
        out, new = _matmul(*args, hosted=stages_of(specs))
        keep(specs, new)
        return out

    ec = e // 4
    small_loc = jnp.concatenate([conv_w, conv_b[:, None], lru_ba[:, None], lru_bx[:, None], lru_L[:, None]], axis=1)
    sm_all = all_gather_small(small_loc.reshape(nmix * 8, ec), "gather_small").reshape(4, 2, nmix, 8, ec)
    lru_small = jnp.transpose(sm_all[:, 0], (1, 2, 0, 3)).reshape(nmix, 8, e)

    xs = x[0]
    saved = []
    for i in range(depth):
        j = i // 2
        h = rms_fwd(xs, g_pre[i][None], tr_row) if i == 0 else h_next
        nj = (2 * e) // 1024 if (2 * e) % 1024 == 0 else 4
        tn = (2 * e) // nj
        per = e // tn
        perk = (e // 2) // tn
        nxt = i + 1 if i + 1 < depth else None
        stages = [(_stage_gather_d2d, i, ["gate", "ple"])] if i > 0 else []
        if nxt is not None:
            stages.append((_stage_gather_ici, nxt, ["w_in"]))
        uz = run_mm(
            stages, "mm_in", "nn", (s // tm2, nj, 1), h, pl.BlockSpec((tm2, d), lambda a, b, k: (a, 0)),
            full_w(i, "w_in"), pl.BlockSpec((None, None, d, tn), lambda a, b, k, perk=perk: (b // perk, 0, 0, b % perk)),
            jax.ShapeDtypeStruct((2, s, e), MXU), pl.BlockSpec((None, tm2, tn), lambda a, b, k, per=per: (b // per, a, b % per)),
            (8, 128))
        specs = []
        if nxt is not None:
            specs = [(_stage_gather_ici, nxt, mixer_names(nxt) + (["w_out"] if i % 2 else []))]
        if i % 2 == 0:
            (y2,), new = pool_fwd(uz, full_w(i, "pool"), pool_b[j][None], pool_scale[j][None], tr_mix, hosted=stages_of(specs))
            hst = None
        else:
            (y2, hst), new = lru_fwd(uz, full_w(i, "wa"), full_w(i, "wx"), lru_small[j], tr_lru, hosted=stages_of(specs))
        keep(specs, new)
        tn_o = _tile(d, 1024)
        stages = []
        if nxt is not None:
            stages = [(_stage_gather_ici, nxt, ["w_out"] if i % 2 == 0 else ["gate", "ple"])]
        o = run_mm(
            stages, "mm_out", "nn", (s // tm, d // tn_o, 1), y2, pl.BlockSpec((tm, e), lambda a, b, k: (a, 0)),
            full_w(i, "w_out"), pl.BlockSpec((4, None, e // 4, tn_o), lambda a, b, k: (0, 0, 0, b)),
            jax.ShapeDtypeStruct((s, d), MXU), pl.BlockSpec((tm, tn_o), lambda a, b, k: (a, b)), (8, 128))
        x1, hn = res_rms_fwd(xs, o, g_post[i][None], g_ple_in[i][None], tr_row)
        stages = []
        if nxt is not None:
            stages = [(_stage_gather_d2d, nxt, ["w_out"] + mixer_names(nxt))]
            if i % 2 == 0:
                stages.insert(0, (_stage_gather_ici, nxt, ["gate", "ple"]))
        gpre = run_mm(
            stages, "mm_gate", "nn", (s // tm2, d // tn_o, 1), hn, pl.BlockSpec((tm2, d), lambda a, b, k: (a, 0)),
            full_w(i, "gate"), pl.BlockSpec((4, None, d // 4, tn_o), lambda a, b, k: (0, 0, 0, b)),
            jax.ShapeDtypeStruct((s, d), MXU), pl.BlockSpec((tm2, tn_o), lambda a, b, k: (a, b)), (8, 128))
        pe = p[i, 0]
        ev = run_mm(
            [(_stage_gather_d2d, nxt, ["w_in"])] if nxt is not None else [],
            "mm_ple", "nn", (s // tm, 4, 1), pe, pl.BlockSpec((tm, kp), lambda a, b, k: (a, 0)),
            full_w(i, "ple"), pl.BlockSpec((None, None, kp, d // 4), lambda a, b, k: (b, 0, 0, 0)),
            jax.ShapeDtypeStruct((s, d), MXU), pl.BlockSpec((tm, d // 4), lambda a, b, k: (a, b)), (8, 128))
        if nxt is not None:
            x2, h_next = ple_fwd(x1, gpre, ev, g_ple_out[i][None], g_pre[nxt][None], tr_row)
        else:
            x2 = ple_fwd_last(x1, gpre, ev, g_ple_out[i][None], tr_row)
        saved.append((xs, h, uz, y2, hst, o, x1, hn, gpre, ev))
        xs = x2

    dx, sq = loss_bwd(xs, loss_target[0], tr_row)
    d_gpre, d_gpost, d_gin, d_gout = [None] * depth, [None] * depth, [None] * depth, [None] * depth
    d_pool_b, d_pool_sc, d_lru_small = [None] * nmix, [None] * nmix, [None] * nmix
    ts = _tile(s, 1024)
    kc_idx = jnp.stack([me, ci]).astype(jnp.int32)
    full = {n: lax.empty(halves(big[n]).shape, F32) for n in names}
    w_sum = w_got = None

    def first_row(i, n):
        return i if n in ("w_in", "w_out", "gate", "ple") else (i // 2) * (ngrp if n == "pool" else LRU_HEADS)

    for i in reversed(range(depth)):
        j = i // 2
        prev = i + 1 if i + 1 < depth else None
        x0, h, uz, y2, hst, o, x1, hn, gpre, ev = saved[i]
        pe = p[i, 0]
        gl = {}
        de, dgp, d_gout[i] = ple_bwd(dx, gpre, ev, g_ple_out[i][None], tr_row)
        gl["ple"] = _matmul(
            "mm_dple", "tn", (1, 4, s // ts), pe, pl.BlockSpec((ts, kp), lambda a, b, k: (k, 0)),
            de, pl.BlockSpec((ts, d // 4), lambda a, b, k: (k, b)),
            jax.ShapeDtypeStruct((4, 1, kp, d // 4), WIRE), pl.BlockSpec((None, None, kp, d // 4), lambda a, b, k: (b, 0, 0, 0)),
            (kp, d // 4))
        tn_o = _tile(d, 1024)
        gl["gate"] = _matmul(
            "mm_dgate", "tn", (4, d // tn_o, 1), hn, pl.BlockSpec((s, d // 4), lambda a, b, k: (0, a)),
            dgp, pl.BlockSpec((s, tn_o), lambda a, b, k: (0, b)),
            jax.ShapeDtypeStruct((4, 1, d // 4, d), WIRE), pl.BlockSpec((None, None, d // 4, tn_o), lambda a, b, k: (a, 0, 0, b)),
            (8, 128))
        dhn = _matmul(
            "mm_dhn", "nt", (s // tm2, 4, 1), dgp, pl.BlockSpec((tm2, d), lambda a, b, k: (a, 0)),
            full_w(i, "gate"), pl.BlockSpec((None, None, d // 4, d), lambda a, b, k: (b, 0, 0, 0)),
            jax.ShapeDtypeStruct((s, d), MXU), pl.BlockSpec((tm2, d // 4), lambda a, b, k: (a, b)), (8, 128))
        dx1, do, d_gin[i], d_gpost[i] = post_bwd(dx, dhn, x1, g_ple_in[i][None], o, g_post[i][None], tr_row)
        gl["w_out"] = _matmul(
            "mm_dwout", "tn", (4, d // tn_o, 1), y2, pl.BlockSpec((s, e // 4), lambda a, b, k: (0, a)),
            do, pl.BlockSpec((s, tn_o), lambda a, b, k: (0, b)),
            jax.ShapeDtypeStruct((4, 1, e // 4, d), WIRE), pl.BlockSpec((None, None, e // 4, tn_o), lambda a, b, k: (a, 0, 0, b)),
            (8, 128))
        dy2 = _matmul(
            "mm_dy2", "nt", (s // tm2, 4, 1), do, pl.BlockSpec((tm2, d), lambda a, b, k: (a, 0)),
            full_w(i, "w_out"), pl.BlockSpec((None, None, e // 4, d), lambda a, b, k: (b, 0, 0, 0)),
            jax.ShapeDtypeStruct((s, e), MXU), pl.BlockSpec((tm2, e // 4), lambda a, b, k: (a, b)), (8, 128))
        riding = [_stage_scatter_ici([w_sum], [w_got])] if prev is not None else []
        if i % 2 == 0:
            (duz, gl["pool"], d_pool_b[j], d_pool_sc[j]), passed = pool_bwd(
                uz, dy2, full_w(i, "pool"), pool_b[j][None], pool_scale[j][None], tr_mix, hosted=riding)
        else:
            (duz, gl["wa"], gl["wx"], d_lru_small[j]), passed = lru_bwd(
                uz, hst, dy2, full_w(i, "wa"), full_w(i, "wx"), lru_small[j], tr_lru, hosted=riding)
        if prev is not None:
            full["w_in"] = sum_chips(w_sum, passed[0], kc_idx, full["w_in"], first_row(prev, "w_in"))

        def in_halves(g):
            return g.reshape(4, g.shape[1], 2, g.shape[2] // 2, g.shape[3])

        early = list(gl)
        eparts = [in_halves(gl[n]) for n in early]
        tmi = _tile(d, 1024)
        tni = _tile(e // 2, 1024)
        nslab = (e // 2) // tni
        gl["w_in"], from_sib = _matmul(
            "mm_dwin", "tn", (d // tmi, 4 * nslab, 1), h, pl.BlockSpec((s, tmi), lambda a, b, k: (0, a)),
            duz, pl.BlockSpec((None, s, tni), lambda a, b, k, nslab=nslab: (b // (2 * nslab), 0, b % (2 * nslab))),
            jax.ShapeDtypeStruct((4, 1, d, e // 2), WIRE),
            pl.BlockSpec((None, None, tmi, tni), lambda a, b, k, nslab=nslab: (b // nslab, 0, a, b % nslab)),
            (8, 128), hosted=[_stage_send_half(eparts, [lax.empty(g.shape[:2] + g.shape[3:], WIRE) for g in eparts])])
        esums = [add_halves(g, r, c_idx) for g, r in zip(eparts, from_sib)]
        wpart = in_halves(gl["w_in"])
        tnd = _tile(d, 1024)
        dh, passed = _matmul(
            "mm_dh", "nt", (s // tm, d // tnd, 4), duz, pl.BlockSpec((None, tm, e // 2), lambda a, b, k: (k // 2, a, k % 2)),
            full_w(i, "w_in"), pl.BlockSpec((None, None, tnd, e // 2), lambda a, b, k: (k, 0, b, 0)),
            jax.ShapeDtypeStruct((s, d), MXU), pl.BlockSpec((tm, tnd), lambda a, b, k: (a, b)), (tm, tnd),
            hosted=[_stage_scatter_ici(esums, [lax.empty((3,) + q.shape[1:], WIRE) for q in esums]),
                    _stage_send_half([wpart], [lax.empty(wpart.shape[:2] + wpart.shape[3:], WIRE)])])
        for n, q, got in zip(early, esums, passed[:len(early)]):
            full[n] = sum_chips(q, got, kc_idx, full[n], first_row(i, n))
        w_sum = add_halves(wpart, passed[len(early)], c_idx)
        w_got = lax.empty((3,) + w_sum.shape[1:], WIRE)
        dx, d_gpre[i] = rms_bwd_res(dx1, dh, x0, g_pre[i][None], tr_row)
    grad_x = dx[None]

    got, swapped = scatter_to_chips([w_sum], [full[n] for n in names],
                                    [(1, depth - 1) if n == "w_in" else (0, full[n].shape[0]) for n in names])
    full = dict(zip(names, swapped))
    full["w_in"] = share_halves([sum_chips(w_sum, got[0], kc_idx, full["w_in"], 0)], [(0, 1)])[0]
    grads = {n: full[n].reshape(big[n].shape) for n in names}

    def rows_e(a):
        return jnp.stack(a).reshape(-1, e) if isinstance(a, list) else a.reshape(-1, e)

    pack = [rows_e([g[0] for g in d_gpre]), rows_e([g[0] for g in d_gpost]), rows_e([g[0] for g in d_gin]), rows_e([g[0] for g in d_gout]),
            jnp.concatenate(d_pool_b, axis=0), jnp.concatenate(d_pool_sc, axis=0), jnp.concatenate(d_lru_small, axis=0),
            jnp.pad(sq, ((0, 0), (0, e - d)))]
    sizes = [a.shape[0] for a in pack]
    packed = jnp.concatenate(pack, axis=0)
    nrow = packed.shape[0]
    nrow_p = -(-nrow // 8) * 8
    packed = jnp.pad(packed, ((0, nrow_p - nrow), (0, 0)))
    total = sum_devices(all_gather_small(packed, "gather_grads").reshape(8, nrow_p, e))
    parts, off = [], 0
    for n_ in sizes:
        parts.append(total[off:off + n_])
        off += n_
    t_gpre, t_gpost, t_gin, t_gout, t_pb, t_psc, t_lru, t_sq = parts
    loss = 0.5 * jnp.sum(t_sq) / d
    t_lru = t_lru.reshape(nmix, 8, e)
    t_lru_loc = lax.dynamic_slice_in_dim(t_lru, me * ec, ec, axis=2)

    def big_update(name, w, m, v):
        shp = big[name].shape
        res4, _ = adamw(w.reshape(shp), grads[name], 0, m.reshape(shp), v.reshape(shp), 0, shp[0])
        return [a.reshape(w.shape) for a in res4]

    def small_update(w, g, m, v):
        shp = w.shape
        w2 = w.reshape(-1, shp[-1])
        dl, nm, nv = adamw_small(w2, g.reshape(w2.shape), m.reshape(w2.shape), v.reshape(w2.shape))
        return [g.reshape(shp), dl.reshape(shp), nm.reshape(shp), nv.reshape(shp)]

    res = {
        "w_in": big_update("w_in", w_in, m_w_in, v_w_in),
        "w_out": big_update("w_out", w_out, m_w_out, v_w_out),
        "g_pre": small_update(g_pre, t_gpre.reshape(depth, d), m_g_pre, v_g_pre),
        "g_post": small_update(g_post, t_gpost.reshape(depth, d), m_g_post, v_g_post),
        "pool_w": big_update("pool", pool_w, m_pool_w, v_pool_w),
        "pool_b": small_update(pool_b, t_pb, m_pool_b, v_pool_b),
        "pool_scale": small_update(pool_scale, t_psc, m_pool_scale, v_pool_scale),
        "conv_w": small_update(conv_w, t_lru_loc[:, :CONV_W], m_conv_w, v_conv_w),
        "conv_b": small_update(conv_b, t_lru_loc[:, 4], m_conv_b, v_conv_b),
        "lru_wa": big_update("wa", lru_wa, m_lru_wa, v_lru_wa),
        "lru_ba": small_update(lru_ba, t_lru_loc[:, 5], m_lru_ba, v_lru_ba),
        "lru_wx": big_update("wx", lru_wx, m_lru_wx, v_lru_wx),
        "lru_bx": small_update(lru_bx, t_lru_loc[:, 6], m_lru_bx, v_lru_bx),
        "lru_L": small_update(lru_L, t_lru_loc[:, 7], m_lru_L, v_lru_L),
        "w_ple": big_update("ple", w_ple, m_w_ple, v_w_ple),
        "w_ple_gate": big_update("gate", w_ple_gate, m_w_ple_gate, v_w_ple_gate),
        "g_ple_in": small_update(g_ple_in, t_gin.reshape(depth, d), m_g_ple_in, v_g_ple_in),
        "g_ple_out": small_update(g_ple_out, t_gout.reshape(depth, d), m_g_ple_out, v_g_ple_out),
    }
    order = ["w_in", "w_out", "g_pre", "g_post", "pool_w", "pool_b", "pool_scale", "conv_w", "conv_b", "lru_wa", "lru_ba",
             "lru_wx", "lru_bx", "lru_L", "w_ple", "w_ple_gate", "g_ple_in", "g_ple_out"]
    out = [loss, grad_x]
    for slot in range(4):
        out += [res[n][slot] for n in order]
    return tuple(out)
```

```python
import functools

import jax
import jax.numpy as jnp
from jax import lax
from jax.experimental import pallas as pl
from jax.experimental.pallas import tpu as pltpu

F32 = jnp.float32
MXU = jnp.bfloat16
WIRE = jnp.bfloat16
VMEM_LIMIT = 56 * 1024 * 1024
RMS_EPS = 1e-6
LRU_C = 8.0
POOL_WINDOWS = (2, 4, 8, 16)
MAXW = 16
CONV_W = 4
LRU_HEADS = 16
ADAM_LR, ADAM_B1, ADAM_B2, ADAM_EPS, ADAM_WD, ADAM_STEP = 0.001, 0.9, 0.999, 1e-08, 0.01, 10
MESHID = pl.DeviceIdType.MESH
ANY = pl.BlockSpec(memory_space=pl.ANY)


def _cp(sem=None):
    return pltpu.CompilerParams(dimension_semantics=sem, vmem_limit_bytes=VMEM_LIMIT)


def _sig(v):
    return 0.5 * jnp.tanh(0.5 * v) + 0.5


def _tile(n, pref):
    return pref if n % pref == 0 else n


_DN = {"nn": (((1,), (0,)), ((), ())), "nt": (((1,), (1,)), ((), ())), "tn": (((0,), (0,)), ((), ()))}


def _hosting_call(body, name, grid, in_specs, out_specs, out_shape, scratch_shapes, args, hosted, sem, aliases=None):
    n_in, n_out = len(args), len(out_shape)
    aliases = dict(aliases or {})
    if not hosted:
        res = pl.pallas_call(body, name=name, grid=grid, in_specs=in_specs, out_specs=out_specs, out_shape=out_shape,
                             scratch_shapes=scratch_shapes, input_output_aliases=aliases, compiler_params=_cp(sem))(*args)
        return list(res), []
    ro = [r for st in hosted for r in st["ro"]]
    bufs = [r for st in hosted for r in st["bufs"]]
    nro, nbuf = len(ro), len(bufs)
    nsem = sum(st["nsem"] for st in hosted)
    total = 1
    for g in grid:
        total *= g

    def wrapped(*refs):
        ins, ro_refs = refs[:n_in], refs[n_in:n_in + nro]
        outs = refs[n_in + nro + nbuf:n_in + nro + nbuf + n_out]
        buf_refs = refs[n_in + nro + nbuf + n_out:n_in + nro + 2 * nbuf + n_out]
        scr, ss, rs = refs[n_in + nro + 2 * nbuf + n_out:-2], refs[-2], refs[-1]
        step = 0
        for t, g in enumerate(grid):
            step = step * g + pl.program_id(t)

        def run_stages(what):
            r0 = b0 = s0 = 0
            for st in hosted:
                if what in st:
                    st[what](ro_refs[r0:r0 + len(st["ro"])], buf_refs[b0:b0 + len(st["bufs"])], ss, rs, s0)
                r0, b0, s0 = r0 + len(st["ro"]), b0 + len(st["bufs"]), s0 + st["nsem"]

        @pl.when(step == 0)
        def _():
            run_stages("start")

        if any("mid" in st for st in hosted):
            @pl.when(step == total // 2)
            def _():
                run_stages("mid")

        body(*ins, *outs, *scr)

        @pl.when(step == total - 1)
        def _():
            run_stages("finish")

    res = pl.pallas_call(
        wrapped, name=name, grid=grid,
        in_specs=list(in_specs) + [ANY] * (nro + nbuf), out_specs=list(out_specs) + [ANY] * nbuf,
        out_shape=list(out_shape) + [jax.ShapeDtypeStruct(x.shape, x.dtype) for x in bufs],
        scratch_shapes=list(scratch_shapes) + [pltpu.SemaphoreType.DMA((nsem,)), pltpu.SemaphoreType.DMA((nsem,))],
        input_output_aliases={**aliases, **{n_in + nro + t: n_out + t for t in range(nbuf)}},
        compiler_params=_cp(("arbitrary",) * len(grid)),
    )(*args, *ro, *bufs)
    return list(res[:n_out]), list(res[n_out:])


def _matmul(name, mode, grid, a, a_spec, b, b_spec, out_shape, out_spec, acc_shape, hosted=()):
    nk = grid[2]

    def body(a_ref, b_ref, o_ref, acc_ref):
        kk = pl.program_id(2)
        bv = b_ref[...]
        if bv.ndim == 3:
            bv = bv.reshape(bv.shape[0] * bv.shape[1], bv.shape[2])
        prod = lax.dot_general(a_ref[...].astype(MXU), bv.astype(MXU), _DN[mode], preferred_element_type=F32)
        if nk == 1:
            o_ref[...] = prod.astype(o_ref.dtype)
        else:
            @pl.when(kk == 0)
            def _():
                acc_ref[...] = prod

            @pl.when(kk > 0)
            def _():
                acc_ref[...] += prod

            @pl.when(kk == nk - 1)
            def _():
                o_ref[...] = acc_ref[...].astype(o_ref.dtype)

    outs, passed = _hosting_call(body, name, grid, [a_spec, b_spec], [out_spec], [out_shape], [pltpu.VMEM(acc_shape, F32)],
                                 [a, b], hosted, ("parallel", "parallel", "arbitrary"))
    return (outs[0], passed) if hosted else outs[0]


def _rows_call(name, body, ins, in_rows, outs, out_rows, n_rows, tr):
    def spec(shape, tiled):
        if tiled:
            return pl.BlockSpec((tr, shape[1]), lambda i: (i, 0))
        return pl.BlockSpec(shape, lambda i: (0, 0))

    return pl.pallas_call(
        body, name=name, grid=(n_rows // tr,),
        in_specs=[spec(a.shape, t) for a, t in zip(ins, in_rows)],
        out_specs=[spec(o.shape, t) for o, t in zip(outs, out_rows)],
        out_shape=outs, compiler_params=_cp(("arbitrary",)),
    )(*ins)


def _rstd(v):
    return lax.rsqrt(jnp.mean(v * v, axis=-1, keepdims=True) + RMS_EPS)


def _norm_bwd(v, g, dy):
    r = _rstd(v)
    n = v * r
    dn = dy * g
    dv = r * (dn - n * jnp.mean(dn * n, axis=-1, keepdims=True))
    return dv, jnp.sum(dy * n, axis=0, keepdims=True)


def _acc_rows(ref, val):
    @pl.when(pl.program_id(0) == 0)
    def _():
        ref[...] = val

    @pl.when(pl.program_id(0) > 0)
    def _():
        ref[...] += val


def rms_fwd(x, g, tr):
    def body(x_ref, g_ref, o_ref):
        v = x_ref[...]
        o_ref[...] = (v * _rstd(v) * g_ref[...]).astype(o_ref.dtype)

    return _rows_call("rms_fwd", body, [x, g], [True, False], [jax.ShapeDtypeStruct(x.shape, MXU)], [True], x.shape[0], tr)[0]


def res_rms_fwd(x, o, g, g_next, tr):
    def body(x_ref, o_ref, g_ref, gn_ref, y_ref, h_ref):
        v = o_ref[...].astype(F32)
        y = x_ref[...] + v * _rstd(v) * g_ref[...]
        y_ref[...] = y
        h_ref[...] = (y * _rstd(y) * gn_ref[...]).astype(h_ref.dtype)

    return _rows_call("res_rms_fwd", body, [x, o, g, g_next], [True, True, False, False],
                      [jax.ShapeDtypeStruct(x.shape, F32), jax.ShapeDtypeStruct(x.shape, MXU)], [True, True], x.shape[0], tr)


def ple_fwd(x1, gpre, e, g, g_next, tr):
    def body(x_ref, gp_ref, e_ref, g_ref, gn_ref, y_ref, h_ref):
        v = e_ref[...].astype(F32) * _sig(gp_ref[...].astype(F32))
        y = x_ref[...] + v * _rstd(v) * g_ref[...]
        y_ref[...] = y
        h_ref[...] = (y * _rstd(y) * gn_ref[...]).astype(h_ref.dtype)

    return _rows_call("ple_fwd", body, [x1, gpre, e, g, g_next], [True, True, True, False, False],
                      [jax.ShapeDtypeStruct(x1.shape, F32), jax.ShapeDtypeStruct(x1.shape, MXU)], [True, True], x1.shape[0], tr)


def ple_fwd_loss(x1, gpre, e, g, target, tr):
    d = x1.shape[1]

    def body(x_ref, gp_ref, e_ref, g_ref, t_ref, dy_ref, sq_ref):
        v = e_ref[...].astype(F32) * _sig(gp_ref[...].astype(F32))
        diff = x_ref[...] + v * _rstd(v) * g_ref[...] - t_ref[...]
        dy_ref[...] = diff * (1.0 / d)
        _acc_rows(sq_ref, jnp.sum(diff * diff, axis=0, keepdims=True))

    return _rows_call("ple_fwd_loss", body, [x1, gpre, e, g, target], [True, True, True, False, True],
                      [jax.ShapeDtypeStruct(x1.shape, F32), jax.ShapeDtypeStruct((1, d), F32)], [True, False], x1.shape[0], tr)


def loss_bwd(y, target, tr):
    d = y.shape[1]

    def body(y_ref, t_ref, dy_ref, sq_ref):
        diff = y_ref[...] - t_ref[...]
        dy_ref[...] = diff * (1.0 / d)
        _acc_rows(sq_ref, jnp.sum(diff * diff, axis=0, keepdims=True))

    return _rows_call("loss_bwd", body, [y, target], [True, True],
                      [jax.ShapeDtypeStruct(y.shape, F32), jax.ShapeDtypeStruct((1, d), F32)], [True, False], y.shape[0], tr)


def ple_bwd(dx2, gpre, e, g, tr):
    d = dx2.shape[1]

    def body(dx_ref, gp_ref, e_ref, g_ref, de_ref, dgp_ref, dg_ref):
        gate = _sig(gp_ref[...].astype(F32))
        ev = e_ref[...].astype(F32)
        dv, dg = _norm_bwd(ev * gate, g_ref[...], dx_ref[...])
        de_ref[...] = (dv * gate).astype(de_ref.dtype)
        dgp_ref[...] = (dv * ev * gate * (1.0 - gate)).astype(dgp_ref.dtype)
        _acc_rows(dg_ref, dg)

    return _rows_call("ple_bwd", body, [dx2, gpre, e, g], [True, True, True, False],
                      [jax.ShapeDtypeStruct(dx2.shape, MXU), jax.ShapeDtypeStruct(dx2.shape, MXU), jax.ShapeDtypeStruct((1, d), F32)],
                      [True, True, False], dx2.shape[0], tr)


def rms_bwd_res(dres, dh, x, g, tr):
    d = x.shape[1]

    def body(dr_ref, dh_ref, x_ref, g_ref, dx_ref, dg_ref):
        dv, dg = _norm_bwd(x_ref[...], g_ref[...], dh_ref[...].astype(F32))
        dx_ref[...] = dr_ref[...] + dv
        _acc_rows(dg_ref, dg)

    return _rows_call("rms_bwd_res", body, [dres, dh, x, g], [True, True, True, False],
                      [jax.ShapeDtypeStruct(x.shape, F32), jax.ShapeDtypeStruct((1, d), F32)], [True, False], x.shape[0], tr)


def post_bwd(dres, dh, x, g, o, g_o, tr):
    d = x.shape[1]

    def body(dr_ref, dh_ref, x_ref, g_ref, o_ref, go_ref, dx_ref, do_ref, dg_ref, dgo_ref):
        dv, dg = _norm_bwd(x_ref[...], g_ref[...], dh_ref[...].astype(F32))
        dxv = dr_ref[...] + dv
        dx_ref[...] = dxv
        dov, dgo = _norm_bwd(o_ref[...].astype(F32), go_ref[...], dxv)
        do_ref[...] = dov.astype(do_ref.dtype)
        _acc_rows(dg_ref, dg)
        _acc_rows(dgo_ref, dgo)

    return _rows_call("post_bwd", body, [dres, dh, x, g, o, g_o], [True, True, True, False, True, False],
                      [jax.ShapeDtypeStruct(x.shape, F32), jax.ShapeDtypeStruct(x.shape, MXU),
                       jax.ShapeDtypeStruct((1, d), F32), jax.ShapeDtypeStruct((1, d), F32)], [True, True, False, False], x.shape[0], tr)


def _trailing_sum(ext, w):
    s, k = ext, 1
    while k < w:
        s = s + pltpu.roll(s, k, 0)
        k *= 2
    return s


def _leading_sum(ext, w):
    n = ext.shape[0]
    s, k = ext, 1
    while k < w:
        s = s + pltpu.roll(s, n - k, 0)
        k *= 2
    return s


def _pool_inv_count(rb, tr, w, c):
    t = rb * tr + lax.broadcasted_iota(jnp.int32, (tr, c), 0)
    return 1.0 / jnp.minimum(t + 1, w).astype(F32)


def _pool_d(u_ref, up_ref, rb, tr, w):
    cur = u_ref[...].astype(F32)
    prev = jnp.where(rb > 0, up_ref[...].astype(F32), 0.0)
    ext = jnp.concatenate([prev, cur], axis=0)
    win = _trailing_sum(ext, w)[MAXW:]
    return win * _pool_inv_count(rb, tr, w, cur.shape[1]) - cur


def pool_fwd(uz, w_g, bias, scale, tr, hosted=()):
    _, s, e = uz.shape
    ng = len(POOL_WINDOWS)
    cg = e // ng
    nb = s // tr
    hb = tr // MAXW

    def body(u_ref, up_ref, z_ref, w_ref, b_ref, sc_ref, o_ref):
        g, rb = pl.program_id(0), pl.program_id(1)
        wmat = w_ref[...].reshape(cg, cg)
        for gg, win in enumerate(POOL_WINDOWS):
            @pl.when(g == gg)
            def _(win=win):
                d = _pool_d(u_ref, up_ref, rb, tr, win)
                y = (jnp.dot(d.astype(MXU), wmat, preferred_element_type=F32) + b_ref[...]) * sc_ref[...]
                z = z_ref[...].astype(F32)
                o_ref[...] = (y * z * _sig(z)).astype(o_ref.dtype)

    return _hosting_call(
        body, "pool_fwd", (ng, nb),
        [
            pl.BlockSpec((None, tr, cg), lambda g, r: (0, r, g)),
            pl.BlockSpec((None, MAXW, cg), lambda g, r: (0, jnp.maximum(r * hb - 1, 0), g)),
            pl.BlockSpec((None, tr, cg), lambda g, r: (1, r, g)),
            pl.BlockSpec((4, None, cg // 4, cg), lambda g, r: (0, g, 0, 0)),
            pl.BlockSpec((1, cg), lambda g, r: (0, g)),
            pl.BlockSpec((1, cg), lambda g, r: (0, g)),
        ],
        [pl.BlockSpec((tr, cg), lambda g, r: (r, g))], [jax.ShapeDtypeStruct((s, e), MXU)], [],
        [uz, uz, uz, w_g, bias, scale], hosted, ("parallel", "arbitrary"))


def pool_bwd(uz, dy2, w_g, bias, scale, tr, hosted=()):
    _, s, e = uz.shape
    ng = len(POOL_WINDOWS)
    cg = e // ng
    nb = s // tr
    hb = tr // MAXW

    def body(u_ref, up_ref, z_ref, dy_ref, w_ref, b_ref, sc_ref, duz_ref, gw_ref, db_ref, dsc_ref, acc_ref, carry_ref):
        g, step = pl.program_id(0), pl.program_id(1)
        rb = nb - 1 - step
        wmat = w_ref[...].reshape(cg, cg)
        for gg, win in enumerate(POOL_WINDOWS):
            @pl.when(g == gg)
            def _(win=win):
                d = _pool_d(u_ref, up_ref, rb, tr, win).astype(MXU)
                ypre = jnp.dot(d, wmat, preferred_element_type=F32) + b_ref[...]
                z = z_ref[...].astype(F32)
                sg = _sig(z)
                dy2v = dy_ref[...].astype(F32)
                dyv = dy2v * z * sg
                duz_ref[1] = (dy2v * ypre * sc_ref[...] * sg * (1.0 + z * (1.0 - sg))).astype(duz_ref.dtype)
                dypre = dyv * sc_ref[...]
                dsc = jnp.sum(dyv * ypre, axis=0, keepdims=True)
                dbv = jnp.sum(dypre, axis=0, keepdims=True)
                dypre_b = dypre.astype(MXU)
                dd = lax.dot_general(dypre_b, wmat, _DN["nt"], preferred_element_type=F32)
                gw = lax.dot_general(d, dypre_b, _DN["tn"], preferred_element_type=F32)
                q = dd * _pool_inv_count(rb, tr, win, cg)
                nxt = jnp.where(step > 0, carry_ref[...], 0.0)
                lead = _leading_sum(jnp.concatenate([q, nxt], axis=0), win)[:tr]
                duz_ref[0] = (lead - dd).astype(duz_ref.dtype)
                carry_ref[...] = q[:MAXW]

                @pl.when(step == 0)
                def _():
                    acc_ref[...] = gw
                    db_ref[...] = dbv
                    dsc_ref[...] = dsc

                @pl.when(step > 0)
                def _():
                    acc_ref[...] += gw
                    db_ref[...] += dbv
                    dsc_ref[...] += dsc

                @pl.when(step == nb - 1)
                def _():
                    gw_ref[...] = acc_ref[...].reshape(4, cg // 4, cg).astype(gw_ref.dtype)

    return _hosting_call(
        body, "pool_bwd", (ng, nb),
        [
            pl.BlockSpec((None, tr, cg), lambda g, r: (0, nb - 1 - r, g)),
            pl.BlockSpec((None, MAXW, cg), lambda g, r: (0, jnp.maximum((nb - 1 - r) * hb - 1, 0), g)),
            pl.BlockSpec((None, tr, cg), lambda g, r: (1, nb - 1 - r, g)),
            pl.BlockSpec((tr, cg), lambda g, r: (nb - 1 - r, g)),
            pl.BlockSpec((4, None, cg // 4, cg), lambda g, r: (0, g, 0, 0)),
            pl.BlockSpec((1, cg), lambda g, r: (0, g)),
            pl.BlockSpec((1, cg), lambda g, r: (0, g)),
        ],
        [
            pl.BlockSpec((2, tr, cg), lambda g, r: (0, nb - 1 - r, g)),
            pl.BlockSpec((4, None, cg // 4, cg), lambda g, r: (0, g, 0, 0)),
            pl.BlockSpec((1, cg), lambda g, r: (0, g)),
            pl.BlockSpec((1, cg), lambda g, r: (0, g)),
        ],
        [
            jax.ShapeDtypeStruct((2, s, e), MXU),
            jax.ShapeDtypeStruct(w_g.shape, WIRE),
            jax.ShapeDtypeStruct((1, e), F32),
            jax.ShapeDtypeStruct((1, e), F32),
        ],
        [pltpu.VMEM((cg, cg), F32), pltpu.VMEM((MAXW, cg), F32)],
        [uz, uz, uz, dy2, w_g, bias, scale], hosted, ("parallel", "arbitrary"))


HALO = 16


def _one_minus_sq(log_a, a):
    poly = (-2.0 * log_a) * (1.0 + log_a * (1.0 + log_a * (2.0 / 3.0)))
    return jnp.where(log_a > -0.01, poly, 1.0 - a * a)


def _softplus_neg(lam):
    t = jnp.exp(-jnp.abs(lam))
    log1p = jnp.where(t < 1e-3, t * (1.0 - t * (0.5 - t * (1.0 / 3.0))), jnp.log(1.0 + t))
    return jnp.maximum(-lam, 0.0) + log1p, _sig(-lam)


def _lru_gates(u_ref, up_ref, rb, sm_ref, wa, wx):
    cur = u_ref[...].astype(F32)
    prev = jnp.where(rb > 0, up_ref[...].astype(F32), 0.0)
    ext = jnp.concatenate([prev, cur], axis=0)
    taps = [cur] + [pltpu.roll(ext, k, 0)[HALO:] for k in range(1, CONV_W)]
    uc = sm_ref[CONV_W:CONV_W + 1, :]
    for k in range(CONV_W):
        uc = uc + taps[k] * sm_ref[CONV_W - 1 - k:CONV_W - k, :]
    ucb = uc.astype(MXU)
    r = _sig(jnp.dot(ucb, wa, preferred_element_type=F32) + sm_ref[5:6, :])
    ig = _sig(jnp.dot(ucb, wx, preferred_element_type=F32) + sm_ref[6:7, :])
    sp, sgn = _softplus_neg(sm_ref[7:8, :])
    log_a = r * (-LRU_C * sp)
    a = jnp.exp(log_a)
    mult = jnp.sqrt(jnp.maximum(_one_minus_sq(log_a, a), 0.0))
    return taps, uc, ucb, r, ig, sp, sgn, a, mult


LANES = 128


def _seg_scan(a, b, out_ref, scr, state, reverse):
    a_s, b_s, h_s, p_s = scr
    tr, c = a.shape
    seg = tr // 8
    nl = c // LANES
    for l in range(nl):
        a_s[l] = a[:, l * LANES:(l + 1) * LANES]
        b_s[l] = b[:, l * LANES:(l + 1) * LANES]
    h = [jnp.zeros((8, LANES), F32)] * nl
    pp = [jnp.ones((8, LANES), F32)] * nl
    for i in (range(seg - 1, -1, -1) if reverse else range(seg)):
        rows = pl.ds(i, 8, stride=seg)
        for l in range(nl):
            av = a_s[l, rows, :]
            h[l] = av * h[l] + b_s[l, rows, :]
            pp[l] = av * pp[l]
            h_s[l, pl.ds(8 * i, 8), :] = h[l]
            p_s[l, pl.ds(8 * i, 8), :] = pp[l]
    leaving = []
    sub = lax.broadcasted_iota(jnp.int32, (8, LANES), 0)
    for l in range(nl):
        lanes = slice(l * LANES, (l + 1) * LANES)
        st = state[:, lanes]
        entering = jnp.zeros((8, LANES), F32)
        for sgm in (range(7, -1, -1) if reverse else range(8)):
            entering = jnp.where(sub == sgm, st, entering)
            st = h[l][sgm:sgm + 1, :] + pp[l][sgm:sgm + 1, :] * st
        leaving.append(st)
        for i in range(seg):
            rows = pl.ds(8 * i, 8)
            h_s[l, rows, :] = h_s[l, rows, :] + p_s[l, rows, :] * entering
        for sgm in range(8):
            for t0 in range(0, seg, 8):
                out_ref[pl.ds(sgm * seg + t0, 8), lanes] = h_s[l, pl.ds(8 * t0 + sgm, 8, stride=8), :]
    return jnp.concatenate(leaving, axis=1)


def lru_fwd(uz, wa_g, wx_g, small, tr, hosted=()):
    _, s, e = uz.shape
    cb = e // LRU_HEADS
    nb = s // tr
    hb = tr // HALO

    def body(u_ref, up_ref, z_ref, wa_ref, wx_ref, sm_ref, o_ref, h_ref, s0, s1, s2, s3, carry_ref):
        rb = pl.program_id(1)
        wa = wa_ref[...].reshape(cb, cb)
        wx = wx_ref[...].reshape(cb, cb)
        _, uc, _, _, ig, _, _, a, mult = _lru_gates(u_ref, up_ref, rb, sm_ref, wa, wx)
        start = jnp.where(rb > 0, carry_ref[0:1, :], 0.0)
        last = _seg_scan(a, mult * ig * uc, h_ref, (s0, s1, s2, s3), start, False)
        carry_ref[...] = jnp.broadcast_to(last, carry_ref.shape)
        z = z_ref[...].astype(F32)
        o_ref[...] = (h_ref[...] * z * _sig(z)).astype(o_ref.dtype)

    wspec = pl.BlockSpec((4, None, cb // 4, cb), lambda h, r: (0, h, 0, 0))
    return _hosting_call(
        body, "lru_fwd", (LRU_HEADS, nb),
        [
            pl.BlockSpec((None, tr, cb), lambda h, r: (0, r, h)),
            pl.BlockSpec((None, HALO, cb), lambda h, r: (0, jnp.maximum(r * hb - 1, 0), h)),
            pl.BlockSpec((None, tr, cb), lambda h, r: (1, r, h)),
            wspec, wspec,
            pl.BlockSpec((8, cb), lambda h, r: (0, h)),
        ],
        [pl.BlockSpec((tr, cb), lambda h, r: (r, h)), pl.BlockSpec((tr, cb), lambda h, r: (r, h))],
        [jax.ShapeDtypeStruct((s, e), MXU), jax.ShapeDtypeStruct((s, e), F32)],
        [pltpu.VMEM((cb // LANES, tr, LANES), F32)] * 4 + [pltpu.VMEM((8, cb), F32)],
        [uz, uz, uz, wa_g, wx_g, small], hosted, ("parallel", "arbitrary"))


def lru_bwd(uz, hst, dy2, wa_g, wx_g, small, tr, hosted=()):
    _, s, e = uz.shape
    cb = e // LRU_HEADS
    nb = s // tr
    hb = tr // HALO

    def body(u_ref, up_ref, z_ref, h_ref, hp_ref, dy_ref, wa_ref, wx_ref, sm_ref,
             duz_ref, gwa_ref, gwx_ref, dsm_ref, s0, s1, s2, s3, g_s, acc_a, acc_x, gcar, acar, dcar):
        step = pl.program_id(1)
        rb = nb - 1 - step
        wa = wa_ref[...].reshape(cb, cb)
        wx = wx_ref[...].reshape(cb, cb)
        taps, uc, ucb, r, ig, sp, sgn, a, mult = _lru_gates(u_ref, up_ref, rb, sm_ref, wa, wx)
        row = lax.broadcasted_iota(jnp.int32, a.shape, 0)
        z = z_ref[...].astype(F32)
        sg = _sig(z)
        dy2v = dy_ref[...].astype(F32)
        hv = h_ref[...]
        duz_ref[1] = (dy2v * hv * sg * (1.0 + z * (1.0 - sg))).astype(duz_ref.dtype)
        a_next = jnp.where(row == tr - 1, jnp.where(step > 0, acar[0:1, :], 0.0), pltpu.roll(a, tr - 1, 0))
        g_first = _seg_scan(a_next, dy2v * z * sg, g_s, (s0, s1, s2, s3), jnp.where(step > 0, gcar[0:1, :], 0.0), True)
        gcar[...] = jnp.broadcast_to(g_first, gcar.shape)
        acar[...] = jnp.broadcast_to(a[0:1, :], acar.shape)
        gv = g_s[...]
        h_before = jnp.where(rb > 0, hp_ref[HALO - 1:HALO, :], 0.0)
        h_prev = jnp.where(row == 0, h_before, pltpu.roll(hv, 1, 0))
        da = gv * h_prev
        gu = gv * uc
        dmult = gu * ig
        dig = gu * mult
        dlog_a = da * a - dmult * jnp.where(mult > 0.0, a * a / mult, 0.0)
        dra = dlog_a * (-LRU_C) * sp * r * (1.0 - r)
        dix = dig * ig * (1.0 - ig)
        dl = jnp.sum(dlog_a * r, axis=0, keepdims=True) * (LRU_C * sgn)
        dra_b, dix_b = dra.astype(MXU), dix.astype(MXU)
        duc = (gv * mult * ig + lax.dot_general(dra_b, wa, _DN["nt"], preferred_element_type=F32)
               + lax.dot_general(dix_b, wx, _DN["nt"], preferred_element_type=F32))
        gwa = lax.dot_general(ucb, dra_b, _DN["tn"], preferred_element_type=F32)
        gwx = lax.dot_general(ucb, dix_b, _DN["tn"], preferred_element_type=F32)
        ext = jnp.concatenate([duc, jnp.where(step > 0, dcar[...], 0.0)], axis=0)
        n = ext.shape[0]
        du = duc * sm_ref[CONV_W - 1:CONV_W, :]
        for k in range(1, CONV_W):
            du = du + pltpu.roll(ext, n - k, 0)[:tr] * sm_ref[CONV_W - 1 - k:CONV_W - k, :]
        duz_ref[0] = du.astype(duz_ref.dtype)
        dcar[...] = duc[:HALO]
        rows = [jnp.sum(duc * taps[CONV_W - 1 - k], axis=0, keepdims=True) for k in range(CONV_W)]
        rows += [jnp.sum(duc, axis=0, keepdims=True), jnp.sum(dra, axis=0, keepdims=True),
                 jnp.sum(dix, axis=0, keepdims=True), dl]

        @pl.when(step == 0)
        def _():
            acc_a[...] = gwa
            acc_x[...] = gwx
            for k, rv in enumerate(rows):
                dsm_ref[k:k + 1, :] = rv

        @pl.when(step > 0)
        def _():
            acc_a[...] += gwa
            acc_x[...] += gwx
            for k, rv in enumerate(rows):
                dsm_ref[k:k + 1, :] += rv

        @pl.when(step == nb - 1)
        def _():
            gwa_ref[...] = acc_a[...].reshape(4, cb // 4, cb).astype(gwa_ref.dtype)
            gwx_ref[...] = acc_x[...].reshape(4, cb // 4, cb).astype(gwx_ref.dtype)

    wspec = pl.BlockSpec((4, None, cb // 4, cb), lambda h, r: (0, h, 0, 0))
    blk = pl.BlockSpec((tr, cb), lambda h, r: (nb - 1 - r, h))
    return _hosting_call(
        body, "lru_bwd", (LRU_HEADS, nb),
        [
            pl.BlockSpec((None, tr, cb), lambda h, r: (0, nb - 1 - r, h)),
            pl.BlockSpec((None, HALO, cb), lambda h, r: (0, jnp.maximum((nb - 1 - r) * hb - 1, 0), h)),
            pl.BlockSpec((None, tr, cb), lambda h, r: (1, nb - 1 - r, h)),
            blk,
            pl.BlockSpec((HALO, cb), lambda h, r: (jnp.maximum((nb - 1 - r) * hb - 1, 0), h)),
            blk,
            wspec, wspec,
            pl.BlockSpec((8, cb), lambda h, r: (0, h)),
        ],
        [
            pl.BlockSpec((2, tr, cb), lambda h, r: (0, nb - 1 - r, h)),
            wspec, wspec,
            pl.BlockSpec((8, cb), lambda h, r: (0, h)),
        ],
        [
            jax.ShapeDtypeStruct((2, s, e), MXU),
            jax.ShapeDtypeStruct(wa_g.shape, WIRE),
            jax.ShapeDtypeStruct(wx_g.shape, WIRE),
            jax.ShapeDtypeStruct((8, e), F32),
        ],
        ([pltpu.VMEM((cb // LANES, tr, LANES), F32)] * 4 + [pltpu.VMEM((tr, cb), F32)]
         + [pltpu.VMEM((cb, cb), F32)] * 2 + [pltpu.VMEM((8, cb), F32)] * 2 + [pltpu.VMEM((HALO, cb), F32)]),
        [uz, uz, uz, hst, hst, dy2, wa_g, wx_g, small], hosted, ("parallel", "arbitrary"))


def _place():
    x, y, c = lax.axis_index("x"), lax.axis_index("y"), lax.axis_index("c")
    chips = [(1 - x, y), (x, 1 - y), (1 - x, 1 - y)]
    return x, y, c, chips


def _rcopy(src, dst, send_sems, recv_sems, k, to):
    return pltpu.make_async_remote_copy(src_ref=src, dst_ref=dst, send_sem=send_sems.at[k], recv_sem=recv_sems.at[k],
                                        device_id=to, device_id_type=MESHID)


def _stage_gather_ici(bufs):
    n = len(bufs)

    def quarter(ref, chip, c, q):
        rq = ref.shape[3] // 2
        return ref.at[2 * chip[0] + chip[1], :, c, pl.ds(q * rq, rq)]

    def copies(refs, ss, rs, off, a, sending):
        x, y, c, _ = _place()
        me, xn, yn, dg = (x, y), (1 - x, y), (x, 1 - y), (1 - x, 1 - y)
        plan = [(0, me, xn, 0), (1, me, xn, 1), (3, me, yn, 1), (2, me, yn, 0),
                (4, xn, yn, 0), (5, yn, xn, 1)]
        if not sending:
            plan = [(0, xn, xn, 0), (1, xn, xn, 1), (3, yn, yn, 1), (2, yn, yn, 0), (4, dg, yn, 0), (5, dg, xn, 1)]
        out = []
        for k, owner, to, q in plan:
            blk = quarter(refs[a], owner, c, q)
            out.append(_rcopy(blk, blk, ss, rs, off + 6 * a + k, (*to, c)))
        return out

    def start(ro, refs, ss, rs, off):
        for a in range(n):
            for cp in copies(refs, ss, rs, off, a, True)[:4]:
                cp.start()

    def mid(ro, refs, ss, rs, off):
        for a in range(n):
            got, out = copies(refs, ss, rs, off, a, False), copies(refs, ss, rs, off, a, True)
            got[0].wait_recv()
            out[4].start()
            got[2].wait_recv()
            out[5].start()

    def finish(ro, refs, ss, rs, off):
        for a in range(n):
            got = copies(refs, ss, rs, off, a, False)
            for k in (1, 3, 4, 5):
                got[k].wait_recv()
            for cp in copies(refs, ss, rs, off, a, True):
                cp.wait_send()

    return dict(ro=[], bufs=list(bufs), nsem=6 * n, start=start, mid=mid, finish=finish)


def _stage_gather_d2d(bufs):
    n = len(bufs)

    def copies(refs, ss, rs, off, sending):
        x, y, c, chips = _place()
        out = []
        for a in range(n):
            for jj, ch in enumerate(chips):
                blk = refs[a].at[2 * ch[0] + ch[1], :, c if sending else 1 - c]
                out.append(_rcopy(blk, blk, ss, rs, off + 3 * a + jj, (x, y, 1 - c)))
        return out

    def start(ro, refs, ss, rs, off):
        for cp in copies(refs, ss, rs, off, True):
            cp.start()

    def finish(ro, refs, ss, rs, off):
        for cp in copies(refs, ss, rs, off, False):
            cp.wait_recv()
        for cp in copies(refs, ss, rs, off, True):
            cp.wait_send()

    return dict(ro=[], bufs=list(bufs), nsem=3 * n, start=start, finish=finish)


def _stage_scatter_ici(parts, gots):
    n = len(parts)

    def copies(ro, refs, ss, rs, off):
        x, y, c, chips = _place()
        return [_rcopy(ro[a].at[2 * ch[0] + ch[1]], refs[a].at[jj], ss, rs, off + 3 * a + jj, (*ch, c))
                for a in range(n) for jj, ch in enumerate(chips)]

    def start(ro, refs, ss, rs, off):
        for cp in copies(ro, refs, ss, rs, off):
            cp.start()

    def finish(ro, refs, ss, rs, off):
        for cp in copies(ro, refs, ss, rs, off):
            cp.wait()

    return dict(ro=list(parts), bufs=list(gots), nsem=3 * n, start=start, finish=finish)


def _stage_send_half(grads, lands):
    n = len(grads)

    def copies(ro, refs, ss, rs, off):
        x, y, c, _ = _place()
        return [_rcopy(ro[a].at[:, :, 1 - c], refs[a], ss, rs, off + a, (x, y, 1 - c)) for a in range(n)]

    def start(ro, refs, ss, rs, off):
        for cp in copies(ro, refs, ss, rs, off):
            cp.start()

    def finish(ro, refs, ss, rs, off):
        for cp in copies(ro, refs, ss, rs, off):
            cp.wait()

    return dict(ro=list(grads), bufs=list(lands), nsem=n, start=start, finish=finish)


def cast_into_slab(w, first, l, k_idx):
    _, r, c = w.shape
    rh = r // 2
    lb, tr = _block_lr(l, rh, c)
    nbh = rh // tr
    assert first % lb == 0

    def body(k_ref, w_ref, o_ref):
        o_ref[...] = w_ref[...].astype(o_ref.dtype)

    return pl.pallas_call(
        body, name="cast_into_slab",
        grid_spec=pltpu.PrefetchScalarGridSpec(
            num_scalar_prefetch=1, grid=(l // lb, 2, nbh),
            in_specs=[pl.BlockSpec((lb, tr, c), lambda i, h, b, k_ref: (first // lb + i, h * nbh + b, 0))],
            out_specs=pl.BlockSpec((None, lb, None, tr, c), lambda i, h, b, k_ref: (k_ref[0], i, h, b, 0))),
        out_shape=jax.ShapeDtypeStruct((4, l, 2, rh, c), WIRE),
        compiler_params=_cp(("parallel", "parallel", "parallel")),
    )(k_idx, w)


def gather_weights(bufs):
    n = len(bufs)
    ici = [_stage_gather_ici([b]) for b in bufs]
    d2d = [_stage_gather_d2d([b]) for b in bufs]
    per = ici[0]["nsem"] + d2d[0]["nsem"]

    def body(*refs):
        outs = refs[n:2 * n]
        ss, rs = refs[2 * n:]
        for what in ("start", "mid"):
            for a in range(n):
                ici[a][what]([], [outs[a]], ss, rs, per * a)
        for a in range(n):
            ici[a]["finish"]([], [outs[a]], ss, rs, per * a)
            d2d[a]["start"]([], [outs[a]], ss, rs, per * a + ici[a]["nsem"])
        for a in range(n):
            d2d[a]["finish"]([], [outs[a]], ss, rs, per * a + ici[a]["nsem"])

    return pl.pallas_call(
        body, name="gather_weights",
        in_specs=[ANY] * n, out_specs=[ANY] * n,
        out_shape=[jax.ShapeDtypeStruct(a.shape, a.dtype) for a in bufs],
        scratch_shapes=[pltpu.SemaphoreType.DMA((per * n,)), pltpu.SemaphoreType.DMA((per * n,))],
        input_output_aliases={a: a for a in range(n)},
        compiler_params=pltpu.CompilerParams(has_side_effects=True),
    )(*bufs)


def all_gather_small(v, name):
    m_per, n = v.shape

    def body(x_ref, out_ref, send_sems, recv_sems, local_sem):
        x, y, c, chips = _place()
        me, sibling = (x, y, c), (x, y, 1 - c)

        def rows(px, py, pc):
            return out_ref.at[pl.ds((4 * px + 2 * py + pc) * m_per, m_per), :]

        def copy(k, block, to, src=None):
            return _rcopy(rows(*block) if src is None else src, rows(*block), send_sems, recv_sems, k, to)

        mine = pltpu.make_async_copy(x_ref, rows(*me), local_sem)
        mine.start()
        first = [copy(0, me, sibling, src=x_ref)]
        first += [copy(1 + jj, me, (*chip, c), src=x_ref) for jj, chip in enumerate(chips)]
        for cp in first:
            cp.start()
        passed = [copy(4 + jj, (*chip, c), sibling) for jj, chip in enumerate(chips)]
        for jj, chip in enumerate(chips):
            copy(1 + jj, (*chip, c), me).wait_recv()
            passed[jj].start()
        copy(0, sibling, me).wait_recv()
        for jj, chip in enumerate(chips):
            copy(4 + jj, (*chip, 1 - c), me).wait_recv()
        for cp in first + passed:
            cp.wait_send()
        mine.wait()

    return pl.pallas_call(
        body, name=name,
        out_shape=jax.ShapeDtypeStruct((8 * m_per, n), v.dtype),
        in_specs=[pl.BlockSpec(memory_space=pltpu.VMEM)],
        out_specs=pl.BlockSpec(memory_space=pltpu.VMEM),
        scratch_shapes=[pltpu.SemaphoreType.DMA((7,)), pltpu.SemaphoreType.DMA((7,)), pltpu.SemaphoreType.DMA],
        compiler_params=pltpu.CompilerParams(vmem_limit_bytes=VMEM_LIMIT),
    )(v)


def sum_devices(g):
    def body(g_ref, o_ref):
        acc = g_ref[0]
        for d in range(1, 8):
            acc = acc + g_ref[d]
        o_ref[...] = acc

    return pl.pallas_call(body, name="sum_devices", out_shape=jax.ShapeDtypeStruct(g.shape[1:], g.dtype),
                          compiler_params=pltpu.CompilerParams(vmem_limit_bytes=VMEM_LIMIT))(g)


def send_other_half(grads):
    n = len(grads)

    def body(*refs):
        ins, outs = refs[:n], refs[n:2 * n]
        send_sems, recv_sems = refs[2 * n:]
        x, y, c, _ = _place()
        sib = (x, y, 1 - c)
        cps = [_rcopy(ins[a].at[:, :, 1 - c], outs[a], send_sems, recv_sems, a, sib) for a in range(n)]
        for cp in cps:
            cp.start()
        for cp in cps:
            cp.wait()

    return pl.pallas_call(
        body, name="send_other_half", in_specs=[ANY] * n, out_specs=[ANY] * n,
        out_shape=[jax.ShapeDtypeStruct(g.shape[:2] + g.shape[3:], g.dtype) for g in grads],
        scratch_shapes=[pltpu.SemaphoreType.DMA((n,)), pltpu.SemaphoreType.DMA((n,))],
        compiler_params=pltpu.CompilerParams(has_side_effects=True),
    )(*grads)


def scatter_to_chips(parts, fulls, spans):
    n, m = len(parts), len(fulls)

    def body(*refs):
        ins, outs, shared = refs[:n], refs[n + m:2 * n + m], refs[2 * n + m:2 * n + 2 * m]
        send_sems, recv_sems = refs[2 * n + 2 * m:]
        x, y, c, chips = _place()
        sib = (x, y, 1 - c)
        cps = []
        for a in range(n):
            for jj, ch in enumerate(chips):
                cps.append(_rcopy(ins[a].at[2 * ch[0] + ch[1]], outs[a].at[jj], send_sems, recv_sems, 3 * a + jj, (*ch, c)))

        def blk(a, half):
            return shared[a].at[pl.ds(spans[a][0], spans[a][1]), half]

        swaps = [_rcopy(blk(a, c), blk(a, c), send_sems, recv_sems, 3 * n + a, sib) for a in range(m)]
        for cp in cps + swaps:
            cp.start()
        for cp in cps:
            cp.wait()
        for a in range(m):
            _rcopy(blk(a, 1 - c), blk(a, 1 - c), send_sems, recv_sems, 3 * n + a, sib).wait_recv()
        for cp in swaps:
            cp.wait_send()

    res = pl.pallas_call(
        body, name="scatter_to_chips", in_specs=[ANY] * (n + m), out_specs=[ANY] * (n + m),
        out_shape=[jax.ShapeDtypeStruct((3,) + p.shape[1:], p.dtype) for p in parts]
        + [jax.ShapeDtypeStruct(f.shape, f.dtype) for f in fulls],
        scratch_shapes=[pltpu.SemaphoreType.DMA((3 * n + m,)), pltpu.SemaphoreType.DMA((3 * n + m,))],
        input_output_aliases={n + a: n + a for a in range(m)},
        compiler_params=pltpu.CompilerParams(has_side_effects=True),
    )(*parts, *fulls)
    return list(res[:n]), list(res[n:])


def share_halves(bufs, spans):
    n = len(bufs)

    def body(*refs):
        outs = refs[n:2 * n]
        send_sems, recv_sems = refs[2 * n:]
        x, y, c, _ = _place()
        sib = (x, y, 1 - c)

        def blk(a, half):
            return outs[a].at[pl.ds(spans[a][0], spans[a][1]), half]

        cps = [_rcopy(blk(a, c), blk(a, c), send_sems, recv_sems, a, sib) for a in range(n)]
        for cp in cps:
            cp.start()
        for a in range(n):
            _rcopy(blk(a, 1 - c), blk(a, 1 - c), send_sems, recv_sems, a, sib).wait_recv()
        for cp in cps:
            cp.wait_send()

    return pl.pallas_call(
        body, name="share_halves", in_specs=[ANY] * n, out_specs=[ANY] * n,
        out_shape=[jax.ShapeDtypeStruct(b.shape, b.dtype) for b in bufs],
        scratch_shapes=[pltpu.SemaphoreType.DMA((n,)), pltpu.SemaphoreType.DMA((n,))],
        input_output_aliases={a: a for a in range(n)},
        compiler_params=pltpu.CompilerParams(has_side_effects=True),
    )(*bufs)


def _block_rows(r, c, itemsize, budget=1 << 20):
    tr = r
    while tr * c * itemsize > budget and tr % 16 == 0:
        tr //= 2
    return tr


def _block_lr(l, r, c, budget=4 << 20):
    tr = _block_rows(r, c, 4, budget)
    lb = 1
    if tr == r:
        while l % (2 * lb) == 0 and 2 * lb * r * c * 4 <= budget:
            lb *= 2
    return lb, tr


def add_halves(g, got, c_idx):
    k4, l, _, rh, cc = g.shape
    lb, tr = _block_lr(l, rh, cc)

    def body(c_ref, g_ref, r_ref, o_ref):
        o_ref[...] = (g_ref[...].astype(F32) + r_ref[...].astype(F32)).astype(o_ref.dtype)

    return pl.pallas_call(
        body, name="add_halves",
        grid_spec=pltpu.PrefetchScalarGridSpec(
            num_scalar_prefetch=1, grid=(k4, l // lb, rh // tr),
            in_specs=[pl.BlockSpec((None, lb, None, tr, cc), lambda k, i, b, c_ref: (k, i, c_ref[0], b, 0)),
                      pl.BlockSpec((None, lb, tr, cc), lambda k, i, b, c_ref: (k, i, b, 0))],
            out_specs=pl.BlockSpec((None, lb, tr, cc), lambda k, i, b, c_ref: (k, i, b, 0))),
        out_shape=jax.ShapeDtypeStruct(got.shape, WIRE),
        compiler_params=_cp(("parallel", "parallel", "parallel")),
    )(c_idx, g, got)


def sum_chips(own, got, kc_idx, full, first):
    _, l, rh, cc = own.shape
    lb, tr = _block_lr(l, rh, cc, 2 << 20)
    assert first % lb == 0

    def body(k_ref, o_ref, r_ref, _full, s_ref):
        s_ref[...] = ((o_ref[...].astype(F32) + r_ref[0].astype(F32)) + r_ref[1].astype(F32)) + r_ref[2].astype(F32)

    return pl.pallas_call(
        body, name="sum_chips",
        grid_spec=pltpu.PrefetchScalarGridSpec(
            num_scalar_prefetch=1, grid=(l // lb, rh // tr),
            in_specs=[pl.BlockSpec((None, lb, tr, cc), lambda i, b, k_ref: (k_ref[0], i, b, 0)),
                      pl.BlockSpec((3, lb, tr, cc), lambda i, b, k_ref: (0, i, b, 0)),
                      ANY],
            out_specs=pl.BlockSpec((lb, None, tr, cc), lambda i, b, k_ref: (first // lb + i, k_ref[1], b, 0))),
        out_shape=jax.ShapeDtypeStruct(full.shape, F32),
        input_output_aliases={3: 0},
        compiler_params=_cp(("parallel", "parallel")),
    )(kc_idx, own, got, full)


def _adam_math(w, g, m, v):
    m = ADAM_B1 * m + (1.0 - ADAM_B1) * g
    v = ADAM_B2 * v + (1.0 - ADAM_B2) * (g * g)
    m_hat = m / (1.0 - ADAM_B1 ** ADAM_STEP)
    v_hat = v / (1.0 - ADAM_B2 ** ADAM_STEP)
    delta = -ADAM_LR * (m_hat / (jnp.sqrt(v_hat) + ADAM_EPS) + ADAM_WD * w)
    return delta, m, v


def adamw(w, g, g_first, m, v, lo, hi, prev=None, hosted=()):
    l, r, c = w.shape
    tr = _block_rows(r, c, 4, 2 << 20)
    prev = list(prev or [])

    def body(w_ref, g_ref, m_ref, v_ref, *rest):
        go_ref, d_ref, mo_ref, vo_ref = rest[len(prev):]
        gv = g_ref[...]
        go_ref[...] = gv
        d_ref[...], mo_ref[...], vo_ref[...] = _adam_math(w_ref[...], gv, m_ref[...], v_ref[...])

    spec = pl.BlockSpec((None, tr, c), lambda i, b: (lo + i, b, 0))
    gspec = pl.BlockSpec((None, tr, c), lambda i, b: (g_first + i, b, 0))
    return _hosting_call(
        body, "adamw", (hi - lo, r // tr), [spec, gspec, spec, spec] + [ANY] * len(prev), [spec] * 4,
        [jax.ShapeDtypeStruct(w.shape, F32)] * 4, [], [w, g, m, v] + prev, hosted, ("parallel", "parallel"),
        aliases={4 + t: t for t in range(len(prev))})


def adamw_small(w, g, m, v):
    def body(w_ref, g_ref, m_ref, v_ref, d_ref, mo_ref, vo_ref):
        d_ref[...], mo_ref[...], vo_ref[...] = _adam_math(w_ref[...], g_ref[...], m_ref[...], v_ref[...])

    return pl.pallas_call(body, name="adamw_small", out_shape=[jax.ShapeDtypeStruct(w.shape, F32)] * 3)(w, g, m, v)


def kernel(x, p, w_in, w_out, g_pre, g_post, pool_w, pool_b, pool_scale, conv_w, conv_b, lru_wa, lru_ba, lru_wx, lru_bx, lru_L, w_ple, w_ple_gate, g_ple_in, g_ple_out, loss_target, m_w_in, m_w_out, m_g_pre, m_g_post, m_pool_w, m_pool_b, m_pool_scale, m_conv_w, m_conv_b, m_lru_wa, m_lru_ba, m_lru_wx, m_lru_bx, m_lru_L, m_w_ple, m_w_ple_gate, m_g_ple_in, m_g_ple_out, v_w_in, v_w_out, v_g_pre, v_g_post, v_pool_w, v_pool_b, v_pool_scale, v_conv_w, v_conv_b, v_lru_wa, v_lru_ba, v_lru_wx, v_lru_bx, v_lru_L, v_w_ple, v_w_ple_gate, v_g_ple_in, v_g_ple_out):
    depth = w_in.shape[0]
    _, s, d = x.shape
    e = 2 * d
    kp = p.shape[-1]
    nmix = pool_w.shape[0]
    ngrp = pool_w.shape[1]
    cg = e // ngrp
    cb = e // LRU_HEADS
    xi, yi, ci = lax.axis_index("x"), lax.axis_index("y"), lax.axis_index("c")
    me = 2 * xi + yi
    c_idx = jnp.reshape(ci, (1,)).astype(jnp.int32)
    k_idx = jnp.reshape(me, (1,)).astype(jnp.int32)
    tr_row = _tile(s, 512)
    tr_mix = _tile(s, 512)
    tr_lru = _tile(s, 1024)
    tm = _tile(s, 1024)
    tm2 = _tile(s, 2048)

    def halves(a):
        return a.reshape(a.shape[0], 2, a.shape[1] // 2, a.shape[2])

    big = {
        "w_in": w_in, "w_out": w_out, "gate": w_ple_gate, "ple": w_ple,
        "pool": pool_w.reshape(nmix * ngrp, cg // 4, cg),
        "wa": lru_wa.reshape(nmix * LRU_HEADS, cb // 4, cb), "wx": lru_wx.reshape(nmix * LRU_HEADS, cb // 4, cb),
    }
    names = list(big)

    def layer_shards(i):
        sh = {n: (big[n], i, 1) for n in ("w_in", "w_out", "gate", "ple")}
        if i % 2 == 0:
            sh["pool"] = (big["pool"], (i // 2) * ngrp, ngrp)
        else:
            sh["wa"], sh["wx"] = (big["wa"], (i // 2) * LRU_HEADS, LRU_HEADS), (big["wx"], (i // 2) * LRU_HEADS, LRU_HEADS)
        return sh

    def mixer_names(i):
        return ["pool"] if i % 2 == 0 else ["wa", "wx"]

    wbuf = [{n: cast_into_slab(*wfl, k_idx) for n, wfl in layer_shards(i).items()} for i in range(depth)]
    first = list(wbuf[0])
    wbuf[0] = dict(zip(first, gather_weights([wbuf[0][n] for n in first])))

    def full_w(i, n):
        b = wbuf[i][n]
        return b.reshape(b.shape[0], b.shape[1], 2 * b.shape[3], b.shape[4])

    def stages_of(specs):
        return [mk([wbuf[l][n] for n in nms]) for mk, l, nms in specs]

    def keep(specs, new):
        for (l, n), b in zip([(l, n) for _, l, nms in specs for n in nms], new):
            wbuf[l][n] = b

    def run_mm(specs, *args):
        if not specs:
            return _matmul(*args)
        out, new = _matmul(*args, hosted=stages_of(specs))
        keep(specs, new)
        return out

    ec = e // 4
    small_loc = jnp.concatenate([conv_w, conv_b[:, None], lru_ba[:, None], lru_bx[:, None], lru_L[:, None]], axis=1)
    sm_all = all_gather_small(small_loc.reshape(nmix * 8, ec), "gather_small").reshape(4, 2, nmix, 8, ec)
    lru_small = jnp.transpose(sm_all[:, 0], (1, 2, 0, 3)).reshape(nmix, 8, e)

    xs = x[0]
    saved = []
    for i in range(depth):
        j = i // 2
        h = rms_fwd(xs, g_pre[i][None], tr_row) if i == 0 else h_next
        nj = (2 * e) // 1024 if (2 * e) % 1024 == 0 else 4
        tn = (2 * e) // nj
        per = e // tn
        perk = (e // 2) // tn
        nxt = i + 1 if i + 1 < depth else None
        stages = [(_stage_gather_d2d, i, ["gate", "ple"])] if i > 0 else []
        if nxt is not None:
            stages.append((_stage_gather_ici, nxt, ["w_in"]))
        uz = run_mm(
            stages, "mm_in", "nn", (s // tm2, nj, 1), h, pl.BlockSpec((tm2, d), lambda a, b, k: (a, 0)),
            full_w(i, "w_in"), pl.BlockSpec((None, None, d, tn), lambda a, b, k, perk=perk: (b // perk, 0, 0, b % perk)),
            jax.ShapeDtypeStruct((2, s, e), MXU), pl.BlockSpec((None, tm2, tn), lambda a, b, k, per=per: (b // per, a, b % per)),
            (8, 128))
        specs = []
        if nxt is not None:
            specs = [(_stage_gather_ici, nxt, mixer_names(nxt) + (["w_out"] if i % 2 else []))]
        if i % 2 == 0:
            (y2,), new = pool_fwd(uz, full_w(i, "pool"), pool_b[j][None], pool_scale[j][None], tr_mix, hosted=stages_of(specs))
            hst = None
        else:
            (y2, hst), new = lru_fwd(uz, full_w(i, "wa"), full_w(i, "wx"), lru_small[j], tr_lru, hosted=stages_of(specs))
        keep(specs, new)
        tn_o = _tile(d, 1024)
        stages = []
        if nxt is not None:
            stages = [(_stage_gather_ici, nxt, ["w_out"] if i % 2 == 0 else ["gate", "ple"])]
        o = run_mm(
            stages, "mm_out", "nn", (s // tm, d // tn_o, 1), y2, pl.BlockSpec((tm, e), lambda a, b, k: (a, 0)),
            full_w(i, "w_out"), pl.BlockSpec((4, None, e // 4, tn_o), lambda a, b, k: (0, 0, 0, b)),
            jax.ShapeDtypeStruct((s, d), MXU), pl.BlockSpec((tm, tn_o), lambda a, b, k: (a, b)), (8, 128))
        x1, hn = res_rms_fwd(xs, o, g_post[i][None], g_ple_in[i][None], tr_row)
        stages = []
        if nxt is not None:
            stages = [(_stage_gather_d2d, nxt, ["w_out"] + mixer_names(nxt))]
            if i % 2 == 0:
                stages.insert(0, (_stage_gather_ici, nxt, ["gate", "ple"]))
        gpre = run_mm(
            stages, "mm_gate", "nn", (s // tm2, d // tn_o, 1), hn, pl.BlockSpec((tm2, d), lambda a, b, k: (a, 0)),
            full_w(i, "gate"), pl.BlockSpec((4, None, d // 4, tn_o), lambda a, b, k: (0, 0, 0, b)),
            jax.ShapeDtypeStruct((s, d), MXU), pl.BlockSpec((tm2, tn_o), lambda a, b, k: (a, b)), (8, 128))
        pe = p[i, 0]
        ev = run_mm(
            [(_stage_gather_d2d, nxt, ["w_in"])] if nxt is not None else [],
            "mm_ple", "nn", (s // tm, 4, 1), pe, pl.BlockSpec((tm, kp), lambda a, b, k: (a, 0)),
            full_w(i, "ple"), pl.BlockSpec((None, None, kp, d // 4), lambda a, b, k: (b, 0, 0, 0)),
            jax.ShapeDtypeStruct((s, d), MXU), pl.BlockSpec((tm, d // 4), lambda a, b, k: (a, b)), (8, 128))
        if nxt is not None:
            x2, h_next = ple_fwd(x1, gpre, ev, g_ple_out[i][None], g_pre[nxt][None], tr_row)
        else:
            x2 = None
            dx, sq = ple_fwd_loss(x1, gpre, ev, g_ple_out[i][None], loss_target[0], tr_row)
        saved.append((xs, h, uz, y2, hst, o, x1, hn, gpre, ev))
        xs = x2

    d_gpre, d_gpost, d_gin, d_gout = [None] * depth, [None] * depth, [None] * depth, [None] * depth
    d_pool_b, d_pool_sc, d_lru_small = [None] * nmix, [None] * nmix, [None] * nmix
    ts = _tile(s, 1024)
    kc_idx = jnp.stack([me, ci]).astype(jnp.int32)
    full = {n: lax.empty(halves(big[n]).shape, F32) for n in names}
    w_sum = w_got = None

    def first_row(i, n):
        return i if n in ("w_in", "w_out", "gate", "ple") else (i // 2) * (ngrp if n == "pool" else LRU_HEADS)

    for i in reversed(range(depth)):
        j = i // 2
        prev = i + 1 if i + 1 < depth else None
        x0, h, uz, y2, hst, o, x1, hn, gpre, ev = saved[i]
        pe = p[i, 0]
        gl = {}
        de, dgp, d_gout[i] = ple_bwd(dx, gpre, ev, g_ple_out[i][None], tr_row)
        gl["ple"] = _matmul(
            "mm_dple", "tn", (1, 4, s // ts), pe, pl.BlockSpec((ts, kp), lambda a, b, k: (k, 0)),
            de, pl.BlockSpec((ts, d // 4), lambda a, b, k: (k, b)),
            jax.ShapeDtypeStruct((4, 1, kp, d // 4), WIRE), pl.BlockSpec((None, None, kp, d // 4), lambda a, b, k: (b, 0, 0, 0)),
            (kp, d // 4))
        tn_o = _tile(d, 1024)
        gl["gate"] = _matmul(
            "mm_dgate", "tn", (4, d // tn_o, 1), hn, pl.BlockSpec((s, d // 4), lambda a, b, k: (0, a)),
            dgp, pl.BlockSpec((s, tn_o), lambda a, b, k: (0, b)),
            jax.ShapeDtypeStruct((4, 1, d // 4, d), WIRE), pl.BlockSpec((None, None, d // 4, tn_o), lambda a, b, k: (a, 0, 0, b)),
            (8, 128))
        dhn = _matmul(
            "mm_dhn", "nt", (s // tm2, 4, 1), dgp, pl.BlockSpec((tm2, d), lambda a, b, k: (a, 0)),
            full_w(i, "gate"), pl.BlockSpec((None, None, d // 4, d), lambda a, b, k: (b, 0, 0, 0)),
            jax.ShapeDtypeStruct((s, d), MXU), pl.BlockSpec((tm2, d // 4), lambda a, b, k: (a, b)), (8, 128))
        dx1, do, d_gin[i], d_gpost[i] = post_bwd(dx, dhn, x1, g_ple_in[i][None], o, g_post[i][None], tr_row)
        gl["w_out"] = _matmul(
            "mm_dwout", "tn", (4, d // tn_o, 1), y2, pl.BlockSpec((s, e // 4), lambda a, b, k: (0, a)),
            do, pl.BlockSpec((s, tn_o), lambda a, b, k: (0, b)),
            jax.ShapeDtypeStruct((4, 1, e // 4, d), WIRE), pl.BlockSpec((None, None, e // 4, tn_o), lambda a, b, k: (a, 0, 0, b)),
            (8, 128))
        dy2 = _matmul(
            "mm_dy2", "nt", (s // tm2, 4, 1), do, pl.BlockSpec((tm2, d), lambda a, b, k: (a, 0)),
            full_w(i, "w_out"), pl.BlockSpec((None, None, e // 4, d), lambda a, b, k: (b, 0, 0, 0)),
            jax.ShapeDtypeStruct((s, e), MXU), pl.BlockSpec((tm2, e // 4), lambda a, b, k: (a, b)), (8, 128))
        riding = [_stage_scatter_ici([w_sum], [w_got])] if prev is not None else []
        if i % 2 == 0:
            (duz, gl["pool"], d_pool_b[j], d_pool_sc[j]), passed = pool_bwd(
                uz, dy2, full_w(i, "pool"), pool_b[j][None], pool_scale[j][None], tr_mix, hosted=riding)
        else:
            (duz, gl["wa"], gl["wx"], d_lru_small[j]), passed = lru_bwd(
                uz, hst, dy2, full_w(i, "wa"), full_w(i, "wx"), lru_small[j], tr_lru, hosted=riding)
        if prev is not None:
            full["w_in"] = sum_chips(w_sum, passed[0], kc_idx, full["w_in"], first_row(prev, "w_in"))

        def in_halves(g):
            return g.reshape(4, g.shape[1], 2, g.shape[2] // 2, g.shape[3])

        early = list(gl)
        eparts = [in_halves(gl[n]) for n in early]
        tmi = _tile(d, 1024)
        tni = _tile(e // 2, 1024)
        nslab = (e // 2) // tni
        gl["w_in"], from_sib = _matmul(
            "mm_dwin", "tn", (d // tmi, 4 * nslab, 1), h, pl.BlockSpec((s, tmi), lambda a, b, k: (0, a)),
            duz, pl.BlockSpec((None, s, tni), lambda a, b, k, nslab=nslab: (b // (2 * nslab), 0, b % (2 * nslab))),
            jax.ShapeDtypeStruct((4, 1, d, e // 2), WIRE),
            pl.BlockSpec((None, None, tmi, tni), lambda a, b, k, nslab=nslab: (b // nslab, 0, a, b % nslab)),
            (8, 128), hosted=[_stage_send_half(eparts, [lax.empty(g.shape[:2] + g.shape[3:], WIRE) for g in eparts])])
        esums = [add_halves(g, r, c_idx) for g, r in zip(eparts, from_sib)]
        wpart = in_halves(gl["w_in"])
        tnd = _tile(d, 1024)
        dh, passed = _matmul(
            "mm_dh", "nt", (s // tm2, d // tnd, 4), duz, pl.BlockSpec((None, tm2, e // 2), lambda a, b, k: (k // 2, a, k % 2)),
            full_w(i, "w_in"), pl.BlockSpec((None, None, tnd, e // 2), lambda a, b, k: (k, 0, b, 0)),
            jax.ShapeDtypeStruct((s, d), MXU), pl.BlockSpec((tm2, tnd), lambda a, b, k: (a, b)), (tm2, tnd),
            hosted=[_stage_scatter_ici(esums, [lax.empty((3,) + q.shape[1:], WIRE) for q in esums]),
                    _stage_send_half([wpart], [lax.empty(wpart.shape[:2] + wpart.shape[3:], WIRE)])])
        for n, q, got in zip(early, esums, passed[:len(early)]):
            full[n] = sum_chips(q, got, kc_idx, full[n], first_row(i, n))
        w_sum = add_halves(wpart, passed[len(early)], c_idx)
        w_got = lax.empty((3,) + w_sum.shape[1:], WIRE)
        dx, d_gpre[i] = rms_bwd_res(dx1, dh, x0, g_pre[i][None], tr_row)
    grad_x = dx[None]

    got, swapped = scatter_to_chips([w_sum], [full[n] for n in names],
                                    [(1, depth - 1) if n == "w_in" else (0, full[n].shape[0]) for n in names])
    full = dict(zip(names, swapped))
    full["w_in"] = share_halves([sum_chips(w_sum, got[0], kc_idx, full["w_in"], 0)], [(0, 1)])[0]
    grads = {n: full[n].reshape(big[n].shape) for n in names}

    def rows_e(a):
        return jnp.stack(a).reshape(-1, e) if isinstance(a, list) else a.reshape(-1, e)

    pack = [rows_e([g[0] for g in d_gpre]), rows_e([g[0] for g in d_gpost]), rows_e([g[0] for g in d_gin]), rows_e([g[0] for g in d_gout]),
            jnp.concatenate(d_pool_b, axis=0), jnp.concatenate(d_pool_sc, axis=0), jnp.concatenate(d_lru_small, axis=0),
            jnp.pad(sq, ((0, 0), (0, e - d)))]
    sizes = [a.shape[0] for a in pack]
    packed = jnp.concatenate(pack, axis=0)
    nrow = packed.shape[0]
    nrow_p = -(-nrow // 8) * 8
    packed = jnp.pad(packed, ((0, nrow_p - nrow), (0, 0)))
    total = sum_devices(all_gather_small(packed, "gather_grads").reshape(8, nrow_p, e))
    parts, off = [], 0
    for n_ in sizes:
        parts.append(total[off:off + n_])
        off += n_
    t_gpre, t_gpost, t_gin, t_gout, t_pb, t_psc, t_lru, t_sq = parts
    loss = 0.5 * jnp.sum(t_sq) / d
    t_lru = t_lru.reshape(nmix, 8, e)
    t_lru_loc = lax.dynamic_slice_in_dim(t_lru, me * ec, ec, axis=2)

    def big_update(name, w, m, v):
        shp = big[name].shape
        res4, _ = adamw(w.reshape(shp), grads[name], 0, m.reshape(shp), v.reshape(shp), 0, shp[0])
        return [a.reshape(w.shape) for a in res4]

    def small_update(w, g, m, v):
        shp = w.shape
        w2 = w.reshape(-1, shp[-1])
        dl, nm, nv = adamw_small(w2, g.reshape(w2.shape), m.reshape(w2.shape), v.reshape(w2.shape))
        return [g.reshape(shp), dl.reshape(shp), nm.reshape(shp), nv.reshape(shp)]

    res = {
        "w_in": big_update("w_in", w_in, m_w_in, v_w_in),
        "w_out": big_update("w_out", w_out, m_w_out, v_w_out),
        "g_pre": small_update(g_pre, t_gpre.reshape(depth, d), m_g_pre, v_g_pre),
        "g_post": small_update(g_post, t_gpost.reshape(depth, d), m_g_post, v_g_post),
        "pool_w": big_update("pool", pool_w, m_pool_w, v_pool_w),
        "pool_b": small_update(pool_b, t_pb, m_pool_b, v_pool_b),
        "pool_scale": small_update(pool_scale, t_psc, m_pool_scale, v_pool_scale),
        "conv_w": small_update(conv_w, t_lru_loc[:, :CONV_W], m_conv_w, v_conv_w),
        "conv_b": small_update(conv_b, t_lru_loc[:, 4], m_conv_b, v_conv_b),
        "lru_wa": big_update("wa", lru_wa, m_lru_wa, v_lru_wa),
        "lru_ba": small_update(lru_ba, t_lru_loc[:, 5], m_lru_ba, v_lru_ba),
        "lru_wx": big_update("wx", lru_wx, m_lru_wx, v_lru_wx),
        "lru_bx": small_update(lru_bx, t_lru_loc[:, 6], m_lru_bx, v_lru_bx),
        "lru_L": small_update(lru_L, t_lru_loc[:, 7], m_lru_L, v_lru_L),
        "w_ple": big_update("ple", w_ple, m_w_ple, v_w_ple),
        "w_ple_gate": big_update("gate", w_ple_gate, m_w_ple_gate, v_w_ple_gate),
        "g_ple_in": small_update(g_ple_in, t_gin.reshape(depth, d), m_g_ple_in, v_g_ple_in),
        "g_ple_out": small_update(g_ple_out, t_gout.reshape(depth, d), m_g_ple_out, v_g_ple_out),
    }
    order = ["w_in", "w_out", "g_pre", "g_post", "pool_w", "pool_b", "pool_scale", "conv_w", "conv_b", "lru_wa", "lru_ba",
             "lru_wx", "lru_bx", "lru_L", "w_ple", "w_ple_gate", "g_ple_in", "g_ple_out"]
    out = [loss, grad_x]
    for slot in range(4):
        out += [res[n][slot] for n in order]
    return tuple(out)
```

```python
import functools

import jax
import jax.numpy as jnp
from jax import lax
from jax.experimental import pallas as pl
from jax.experimental.pallas import tpu as pltpu

F32 = jnp.float32
MXU = jnp.bfloat16
WIRE = jnp.bfloat16
VMEM_LIMIT = 56 * 1024 * 1024
RMS_EPS = 1e-6
LRU_C = 8.0
POOL_WINDOWS = (2, 4, 8, 16)
MAXW = 16
CONV_W = 4
LRU_HEADS = 16
ADAM_LR, ADAM_B1, ADAM_B2, ADAM_EPS, ADAM_WD, ADAM_STEP = 0.001, 0.9, 0.999, 1e-08, 0.01, 10
MESHID = pl.DeviceIdType.MESH
ANY = pl.BlockSpec(memory_space=pl.ANY)


def _cp(sem=None):
    return pltpu.CompilerParams(dimension_semantics=sem, vmem_limit_bytes=VMEM_LIMIT)


def _sig(v):
    return 0.5 * jnp.tanh(0.5 * v) + 0.5


def _tile(n, pref):
    return pref if n % pref == 0 else n


_DN = {"nn": (((1,), (0,)), ((), ())), "nt": (((1,), (1,)), ((), ())), "tn": (((0,), (0,)), ((), ()))}


def _hosting_call(body, name, grid, in_specs, out_specs, out_shape, scratch_shapes, args, hosted, sem, aliases=None):
    n_in, n_out = len(args), len(out_shape)
    aliases = dict(aliases or {})
    if not hosted:
        res = pl.pallas_call(body, name=name, grid=grid, in_specs=in_specs, out_specs=out_specs, out_shape=out_shape,
                             scratch_shapes=scratch_shapes, input_output_aliases=aliases, compiler_params=_cp(sem))(*args)
        return list(res), []
    ro = [r for st in hosted for r in st["ro"]]
    bufs = [r for st in hosted for r in st["bufs"]]
    nro, nbuf = len(ro), len(bufs)
    nsem = sum(st["nsem"] for st in hosted)
    total = 1
    for g in grid:
        total *= g

    def wrapped(*refs):
        ins, ro_refs = refs[:n_in], refs[n_in:n_in + nro]
        outs = refs[n_in + nro + nbuf:n_in + nro + nbuf + n_out]
        buf_refs = refs[n_in + nro + nbuf + n_out:n_in + nro + 2 * nbuf + n_out]
        scr, ss, rs = refs[n_in + nro + 2 * nbuf + n_out:-2], refs[-2], refs[-1]
        step = 0
        for t, g in enumerate(grid):
            step = step * g + pl.program_id(t)

        def run_stages(what):
            r0 = b0 = s0 = 0
            for st in hosted:
                if what in st:
                    st[what](ro_refs[r0:r0 + len(st["ro"])], buf_refs[b0:b0 + len(st["bufs"])], ss, rs, s0)
                r0, b0, s0 = r0 + len(st["ro"]), b0 + len(st["bufs"]), s0 + st["nsem"]

        @pl.when(step == 0)
        def _():
            run_stages("start")

        if any("mid" in st for st in hosted):
            @pl.when(step == total // 2)
            def _():
                run_stages("mid")

        body(*ins, *outs, *scr)

        @pl.when(step == total - 1)
        def _():
            run_stages("finish")

    res = pl.pallas_call(
        wrapped, name=name, grid=grid,
        in_specs=list(in_specs) + [ANY] * (nro + nbuf), out_specs=list(out_specs) + [ANY] * nbuf,
        out_shape=list(out_shape) + [jax.ShapeDtypeStruct(x.shape, x.dtype) for x in bufs],
        scratch_shapes=list(scratch_shapes) + [pltpu.SemaphoreType.DMA((nsem,)), pltpu.SemaphoreType.DMA((nsem,))],
        input_output_aliases={**aliases, **{n_in + nro + t: n_out + t for t in range(nbuf)}},
        compiler_params=_cp(("arbitrary",) * len(grid)),
    )(*args, *ro, *bufs)
    return list(res[:n_out]), list(res[n_out:])


def _matmul(name, mode, grid, a, a_spec, b, b_spec, out_shape, out_spec, acc_shape, hosted=()):
    nk = grid[2]

    def body(a_ref, b_ref, o_ref, acc_ref):
        kk = pl.program_id(2)
        bv = b_ref[...]
        if bv.ndim == 3:
            bv = bv.reshape(bv.shape[0] * bv.shape[1], bv.shape[2])
        prod = lax.dot_general(a_ref[...].astype(MXU), bv.astype(MXU), _DN[mode], preferred_element_type=F32)
        if nk == 1:
            o_ref[...] = prod.astype(o_ref.dtype)
        else:
            @pl.when(kk == 0)
            def _():
                acc_ref[...] = prod

            @pl.when(kk > 0)
            def _():
                acc_ref[...] += prod

            @pl.when(kk == nk - 1)
            def _():
                o_ref[...] = acc_ref[...].astype(o_ref.dtype)

    outs, passed = _hosting_call(body, name, grid, [a_spec, b_spec], [out_spec], [out_shape], [pltpu.VMEM(acc_shape, F32)],
                                 [a, b], hosted, ("parallel", "parallel", "arbitrary"))
    return (outs[0], passed) if hosted else outs[0]


def _rows_call(name, body, ins, in_rows, outs, out_rows, n_rows, tr):
    def spec(shape, tiled):
        if tiled:
            return pl.BlockSpec((tr, shape[1]), lambda i: (i, 0))
        return pl.BlockSpec(shape, lambda i: (0, 0))

    return pl.pallas_call(
        body, name=name, grid=(n_rows // tr,),
        in_specs=[spec(a.shape, t) for a, t in zip(ins, in_rows)],
        out_specs=[spec(o.shape, t) for o, t in zip(outs, out_rows)],
        out_shape=outs, compiler_params=_cp(("arbitrary",)),
    )(*ins)


def _rstd(v):
    return lax.rsqrt(jnp.mean(v * v, axis=-1, keepdims=True) + RMS_EPS)


def _norm_bwd(v, g, dy):
    r = _rstd(v)
    n = v * r
    dn = dy * g
    dv = r * (dn - n * jnp.mean(dn * n, axis=-1, keepdims=True))
    return dv, jnp.sum(dy * n, axis=0, keepdims=True)


def _acc_rows(ref, val):
    @pl.when(pl.program_id(0) == 0)
    def _():
        ref[...] = val

    @pl.when(pl.program_id(0) > 0)
    def _():
        ref[...] += val


def rms_fwd(x, g, tr):
    def body(x_ref, g_ref, o_ref):
        v = x_ref[...]
        o_ref[...] = (v * _rstd(v) * g_ref[...]).astype(o_ref.dtype)

    return _rows_call("rms_fwd", body, [x, g], [True, False], [jax.ShapeDtypeStruct(x.shape, MXU)], [True], x.shape[0], tr)[0]


def res_rms_fwd(x, o, g, g_next, tr):
    def body(x_ref, o_ref, g_ref, gn_ref, y_ref, h_ref):
        v = o_ref[...].astype(F32)
        y = x_ref[...] + v * _rstd(v) * g_ref[...]
        y_ref[...] = y
        h_ref[...] = (y * _rstd(y) * gn_ref[...]).astype(h_ref.dtype)

    return _rows_call("res_rms_fwd", body, [x, o, g, g_next], [True, True, False, False],
                      [jax.ShapeDtypeStruct(x.shape, F32), jax.ShapeDtypeStruct(x.shape, MXU)], [True, True], x.shape[0], tr)


def ple_fwd(x1, gpre, e, g, g_next, tr):
    def body(x_ref, gp_ref, e_ref, g_ref, gn_ref, y_ref, h_ref):
        v = e_ref[...].astype(F32) * _sig(gp_ref[...].astype(F32))
        y = x_ref[...] + v * _rstd(v) * g_ref[...]
        y_ref[...] = y
        h_ref[...] = (y * _rstd(y) * gn_ref[...]).astype(h_ref.dtype)

    return _rows_call("ple_fwd", body, [x1, gpre, e, g, g_next], [True, True, True, False, False],
                      [jax.ShapeDtypeStruct(x1.shape, F32), jax.ShapeDtypeStruct(x1.shape, MXU)], [True, True], x1.shape[0], tr)


def ple_fwd_loss(x1, gpre, e, g, target, tr):
    d = x1.shape[1]

    def body(x_ref, gp_ref, e_ref, g_ref, t_ref, dy_ref, sq_ref):
        v = e_ref[...].astype(F32) * _sig(gp_ref[...].astype(F32))
        diff = x_ref[...] + v * _rstd(v) * g_ref[...] - t_ref[...]
        dy_ref[...] = diff * (1.0 / d)
        _acc_rows(sq_ref, jnp.sum(diff * diff, axis=0, keepdims=True))

    return _rows_call("ple_fwd_loss", body, [x1, gpre, e, g, target], [True, True, True, False, True],
                      [jax.ShapeDtypeStruct(x1.shape, F32), jax.ShapeDtypeStruct((1, d), F32)], [True, False], x1.shape[0], tr)


def ple_bwd(dx2, gpre, e, g, tr):
    d = dx2.shape[1]

    def body(dx_ref, gp_ref, e_ref, g_ref, de_ref, dgp_ref, dg_ref):
        gate = _sig(gp_ref[...].astype(F32))
        ev = e_ref[...].astype(F32)
        dv, dg = _norm_bwd(ev * gate, g_ref[...], dx_ref[...])
        de_ref[...] = (dv * gate).astype(de_ref.dtype)
        dgp_ref[...] = (dv * ev * gate * (1.0 - gate)).astype(dgp_ref.dtype)
        _acc_rows(dg_ref, dg)

    return _rows_call("ple_bwd", body, [dx2, gpre, e, g], [True, True, True, False],
                      [jax.ShapeDtypeStruct(dx2.shape, MXU), jax.ShapeDtypeStruct(dx2.shape, MXU), jax.ShapeDtypeStruct((1, d), F32)],
                      [True, True, False], dx2.shape[0], tr)


def rms_bwd_res(dres, dh, x, g, tr):
    d = x.shape[1]

    def body(dr_ref, dh_ref, x_ref, g_ref, dx_ref, dg_ref):
        dv, dg = _norm_bwd(x_ref[...], g_ref[...], dh_ref[...].astype(F32))
        dx_ref[...] = dr_ref[...] + dv
        _acc_rows(dg_ref, dg)

    return _rows_call("rms_bwd_res", body, [dres, dh, x, g], [True, True, True, False],
                      [jax.ShapeDtypeStruct(x.shape, F32), jax.ShapeDtypeStruct((1, d), F32)], [True, False], x.shape[0], tr)


def post_bwd(dres, dh, x, g, o, g_o, tr):
    d = x.shape[1]

    def body(dr_ref, dh_ref, x_ref, g_ref, o_ref, go_ref, dx_ref, do_ref, dg_ref, dgo_ref):
        dv, dg = _norm_bwd(x_ref[...], g_ref[...], dh_ref[...].astype(F32))
        dxv = dr_ref[...] + dv
        dx_ref[...] = dxv
        dov, dgo = _norm_bwd(o_ref[...].astype(F32), go_ref[...], dxv)
        do_ref[...] = dov.astype(do_ref.dtype)
        _acc_rows(dg_ref, dg)
        _acc_rows(dgo_ref, dgo)

    return _rows_call("post_bwd", body, [dres, dh, x, g, o, g_o], [True, True, True, False, True, False],
                      [jax.ShapeDtypeStruct(x.shape, F32), jax.ShapeDtypeStruct(x.shape, MXU),
                       jax.ShapeDtypeStruct((1, d), F32), jax.ShapeDtypeStruct((1, d), F32)], [True, True, False, False], x.shape[0], tr)


def _trailing_sum(ext, w):
    s, k = ext, 1
    while k < w:
        s = s + pltpu.roll(s, k, 0)
        k *= 2
    return s


def _leading_sum(ext, w):
    n = ext.shape[0]
    s, k = ext, 1
    while k < w:
        s = s + pltpu.roll(s, n - k, 0)
        k *= 2
    return s


def _pool_inv_count(rb, tr, w, c):
    t = rb * tr + lax.broadcasted_iota(jnp.int32, (tr, c), 0)
    return 1.0 / jnp.minimum(t + 1, w).astype(F32)


def _pool_d(u_ref, up_ref, rb, tr, w):
    cur = u_ref[...].astype(F32)
    prev = jnp.where(rb > 0, up_ref[...].astype(F32), 0.0)
    ext = jnp.concatenate([prev, cur], axis=0)
    win = _trailing_sum(ext, w)[MAXW:]
    return win * _pool_inv_count(rb, tr, w, cur.shape[1]) - cur


def pool_fwd(uz, w_g, bias, scale, tr, hosted=()):
    _, s, e = uz.shape
    ng = len(POOL_WINDOWS)
    cg = e // ng
    nb = s // tr
    hb = tr // MAXW

    def body(u_ref, up_ref, z_ref, w_ref, b_ref, sc_ref, o_ref):
        g, rb = pl.program_id(0), pl.program_id(1)
        wmat = w_ref[...].reshape(cg, cg)
        for gg, win in enumerate(POOL_WINDOWS):
            @pl.when(g == gg)
            def _(win=win):
                d = _pool_d(u_ref, up_ref, rb, tr, win)
                y = (jnp.dot(d.astype(MXU), wmat, preferred_element_type=F32) + b_ref[...]) * sc_ref[...]
                z = z_ref[...].astype(F32)
                o_ref[...] = (y * z * _sig(z)).astype(o_ref.dtype)

    return _hosting_call(
        body, "pool_fwd", (ng, nb),
        [
            pl.BlockSpec((None, tr, cg), lambda g, r: (0, r, g)),
            pl.BlockSpec((None, MAXW, cg), lambda g, r: (0, jnp.maximum(r * hb - 1, 0), g)),
            pl.BlockSpec((None, tr, cg), lambda g, r: (1, r, g)),
            pl.BlockSpec((4, None, cg // 4, cg), lambda g, r: (0, g, 0, 0)),
            pl.BlockSpec((1, cg), lambda g, r: (0, g)),
            pl.BlockSpec((1, cg), lambda g, r: (0, g)),
        ],
        [pl.BlockSpec((tr, cg), lambda g, r: (r, g))], [jax.ShapeDtypeStruct((s, e), MXU)], [],
        [uz, uz, uz, w_g, bias, scale], hosted, ("parallel", "arbitrary"))


def pool_bwd(uz, dy2, w_g, bias, scale, tr, hosted=()):
    _, s, e = uz.shape
    ng = len(POOL_WINDOWS)
    cg = e // ng
    nb = s // tr
    hb = tr // MAXW

    def body(u_ref, up_ref, z_ref, dy_ref, w_ref, b_ref, sc_ref, duz_ref, gw_ref, db_ref, dsc_ref, acc_ref, carry_ref):
        g, step = pl.program_id(0), pl.program_id(1)
        rb = nb - 1 - step
        wmat = w_ref[...].reshape(cg, cg)
        for gg, win in enumerate(POOL_WINDOWS):
            @pl.when(g == gg)
            def _(win=win):
                d = _pool_d(u_ref, up_ref, rb, tr, win).astype(MXU)
                ypre = jnp.dot(d, wmat, preferred_element_type=F32) + b_ref[...]
                z = z_ref[...].astype(F32)
                sg = _sig(z)
                dy2v = dy_ref[...].astype(F32)
                dyv = dy2v * z * sg
                duz_ref[1] = (dy2v * ypre * sc_ref[...] * sg * (1.0 + z * (1.0 - sg))).astype(duz_ref.dtype)
                dypre = dyv * sc_ref[...]
                dsc = jnp.sum(dyv * ypre, axis=0, keepdims=True)
                dbv = jnp.sum(dypre, axis=0, keepdims=True)
                dypre_b = dypre.astype(MXU)
                dd = lax.dot_general(dypre_b, wmat, _DN["nt"], preferred_element_type=F32)
                gw = lax.dot_general(d, dypre_b, _DN["tn"], preferred_element_type=F32)
                q = dd * _pool_inv_count(rb, tr, win, cg)
                nxt = jnp.where(step > 0, carry_ref[...], 0.0)
                lead = _leading_sum(jnp.concatenate([q, nxt], axis=0), win)[:tr]
                duz_ref[0] = (lead - dd).astype(duz_ref.dtype)
                carry_ref[...] = q[:MAXW]

                @pl.when(step == 0)
                def _():
                    acc_ref[...] = gw
                    db_ref[...] = dbv
                    dsc_ref[...] = dsc

                @pl.when(step > 0)
                def _():
                    acc_ref[...] += gw
                    db_ref[...] += dbv
                    dsc_ref[...] += dsc

                @pl.when(step == nb - 1)
                def _():
                    gw_ref[...] = acc_ref[...].reshape(4, cg // 4, cg).astype(gw_ref.dtype)

    return _hosting_call(
        body, "pool_bwd", (ng, nb),
        [
            pl.BlockSpec((None, tr, cg), lambda g, r: (0, nb - 1 - r, g)),
            pl.BlockSpec((None, MAXW, cg), lambda g, r: (0, jnp.maximum((nb - 1 - r) * hb - 1, 0), g)),
            pl.BlockSpec((None, tr, cg), lambda g, r: (1, nb - 1 - r, g)),
            pl.BlockSpec((tr, cg), lambda g, r: (nb - 1 - r, g)),
            pl.BlockSpec((4, None, cg // 4, cg), lambda g, r: (0, g, 0, 0)),
            pl.BlockSpec((1, cg), lambda g, r: (0, g)),
            pl.BlockSpec((1, cg), lambda g, r: (0, g)),
        ],
        [
            pl.BlockSpec((2, tr, cg), lambda g, r: (0, nb - 1 - r, g)),
            pl.BlockSpec((4, None, cg // 4, cg), lambda g, r: (0, g, 0, 0)),
            pl.BlockSpec((1, cg), lambda g, r: (0, g)),
            pl.BlockSpec((1, cg), lambda g, r: (0, g)),
        ],
        [
            jax.ShapeDtypeStruct((2, s, e), MXU),
            jax.ShapeDtypeStruct(w_g.shape, WIRE),
            jax.ShapeDtypeStruct((1, e), F32),
            jax.ShapeDtypeStruct((1, e), F32),
        ],
        [pltpu.VMEM((cg, cg), F32), pltpu.VMEM((MAXW, cg), F32)],
        [uz, uz, uz, dy2, w_g, bias, scale], hosted, ("parallel", "arbitrary"))


HALO = 16


def _one_minus_sq(log_a, a):
    poly = (-2.0 * log_a) * (1.0 + log_a * (1.0 + log_a * (2.0 / 3.0)))
    return jnp.where(log_a > -0.01, poly, 1.0 - a * a)


def _softplus_neg(lam):
    t = jnp.exp(-jnp.abs(lam))
    log1p = jnp.where(t < 1e-3, t * (1.0 - t * (0.5 - t * (1.0 / 3.0))), jnp.log(1.0 + t))
    return jnp.maximum(-lam, 0.0) + log1p, _sig(-lam)


def _lru_gates(u_ref, up_ref, rb, sm_ref, wa, wx):
    cur = u_ref[...].astype(F32)
    prev = jnp.where(rb > 0, up_ref[...].astype(F32), 0.0)
    ext = jnp.concatenate([prev, cur], axis=0)
    taps = [cur] + [pltpu.roll(ext, k, 0)[HALO:] for k in range(1, CONV_W)]
    uc = sm_ref[CONV_W:CONV_W + 1, :]
    for k in range(CONV_W):
        uc = uc + taps[k] * sm_ref[CONV_W - 1 - k:CONV_W - k, :]
    ucb = uc.astype(MXU)
    r = _sig(jnp.dot(ucb, wa, preferred_element_type=F32) + sm_ref[5:6, :])
    ig = _sig(jnp.dot(ucb, wx, preferred_element_type=F32) + sm_ref[6:7, :])
    sp, sgn = _softplus_neg(sm_ref[7:8, :])
    log_a = r * (-LRU_C * sp)
    a = jnp.exp(log_a)
    mult = jnp.sqrt(jnp.maximum(_one_minus_sq(log_a, a), 0.0))
    return taps, uc, ucb, r, ig, sp, sgn, a, mult


LANES = 128


def _seg_scan(a, b, out_ref, scr, state, reverse):
    a_s, b_s, h_s, p_s = scr
    tr, c = a.shape
    seg = tr // 8
    nl = c // LANES
    for l in range(nl):
        a_s[l] = a[:, l * LANES:(l + 1) * LANES]
        b_s[l] = b[:, l * LANES:(l + 1) * LANES]
    h = [jnp.zeros((8, LANES), F32)] * nl
    pp = [jnp.ones((8, LANES), F32)] * nl
    for i in (range(seg - 1, -1, -1) if reverse else range(seg)):
        rows = pl.ds(i, 8, stride=seg)
        for l in range(nl):
            av = a_s[l, rows, :]
            h[l] = av * h[l] + b_s[l, rows, :]
            pp[l] = av * pp[l]
            h_s[l, pl.ds(8 * i, 8), :] = h[l]
            p_s[l, pl.ds(8 * i, 8), :] = pp[l]
    leaving = []
    sub = lax.broadcasted_iota(jnp.int32, (8, LANES), 0)
    for l in range(nl):
        lanes = slice(l * LANES, (l + 1) * LANES)
        st = state[:, lanes]
        entering = jnp.zeros((8, LANES), F32)
        for sgm in (range(7, -1, -1) if reverse else range(8)):
            entering = jnp.where(sub == sgm, st, entering)
            st = h[l][sgm:sgm + 1, :] + pp[l][sgm:sgm + 1, :] * st
        leaving.append(st)
        for i in range(seg):
            rows = pl.ds(8 * i, 8)
            h_s[l, rows, :] = h_s[l, rows, :] + p_s[l, rows, :] * entering
        for sgm in range(8):
            for t0 in range(0, seg, 8):
                out_ref[pl.ds(sgm * seg + t0, 8), lanes] = h_s[l, pl.ds(8 * t0 + sgm, 8, stride=8), :]
    return jnp.concatenate(leaving, axis=1)


def lru_fwd(uz, wa_g, wx_g, small, tr, hosted=()):
    _, s, e = uz.shape
    cb = e // LRU_HEADS
    nb = s // tr
    hb = tr // HALO

    def body(u_ref, up_ref, z_ref, wa_ref, wx_ref, sm_ref, o_ref, h_ref, s0, s1, s2, s3, carry_ref):
        rb = pl.program_id(1)
        wa = wa_ref[...].reshape(cb, cb)
        wx = wx_ref[...].reshape(cb, cb)
        _, uc, _, _, ig, _, _, a, mult = _lru_gates(u_ref, up_ref, rb, sm_ref, wa, wx)
        start = jnp.where(rb > 0, carry_ref[0:1, :], 0.0)
        last = _seg_scan(a, mult * ig * uc, h_ref, (s0, s1, s2, s3), start, False)
        carry_ref[...] = jnp.broadcast_to(last, carry_ref.shape)
        z = z_ref[...].astype(F32)
        o_ref[...] = (h_ref[...] * z * _sig(z)).astype(o_ref.dtype)

    wspec = pl.BlockSpec((4, None, cb // 4, cb), lambda h, r: (0, h, 0, 0))
    return _hosting_call(
        body, "lru_fwd", (LRU_HEADS, nb),
        [
            pl.BlockSpec((None, tr, cb), lambda h, r: (0, r, h)),
            pl.BlockSpec((None, HALO, cb), lambda h, r: (0, jnp.maximum(r * hb - 1, 0), h)),
            pl.BlockSpec((None, tr, cb), lambda h, r: (1, r, h)),
            wspec, wspec,
            pl.BlockSpec((8, cb), lambda h, r: (0, h)),
        ],
        [pl.BlockSpec((tr, cb), lambda h, r: (r, h)), pl.BlockSpec((tr, cb), lambda h, r: (r, h))],
        [jax.ShapeDtypeStruct((s, e), MXU), jax.ShapeDtypeStruct((s, e), F32)],
        [pltpu.VMEM((cb // LANES, tr, LANES), F32)] * 4 + [pltpu.VMEM((8, cb), F32)],
        [uz, uz, uz, wa_g, wx_g, small], hosted, ("parallel", "arbitrary"))


def lru_bwd(uz, hst, dy2, wa_g, wx_g, small, tr, hosted=()):
    _, s, e = uz.shape
    cb = e // LRU_HEADS
    nb = s // tr
    hb = tr // HALO

    def body(u_ref, up_ref, z_ref, h_ref, hp_ref, dy_ref, wa_ref, wx_ref, sm_ref,
             duz_ref, gwa_ref, gwx_ref, dsm_ref, s0, s1, s2, s3, g_s, acc_a, acc_x, gcar, acar, dcar):
        step = pl.program_id(1)
        rb = nb - 1 - step
        wa = wa_ref[...].reshape(cb, cb)
        wx = wx_ref[...].reshape(cb, cb)
        taps, uc, ucb, r, ig, sp, sgn, a, mult = _lru_gates(u_ref, up_ref, rb, sm_ref, wa, wx)
        row = lax.broadcasted_iota(jnp.int32, a.shape, 0)
        z = z_ref[...].astype(F32)
        sg = _sig(z)
        dy2v = dy_ref[...].astype(F32)
        hv = h_ref[...]
        duz_ref[1] = (dy2v * hv * sg * (1.0 + z * (1.0 - sg))).astype(duz_ref.dtype)
        a_next = jnp.where(row == tr - 1, jnp.where(step > 0, acar[0:1, :], 0.0), pltpu.roll(a, tr - 1, 0))
        g_first = _seg_scan(a_next, dy2v * z * sg, g_s, (s0, s1, s2, s3), jnp.where(step > 0, gcar[0:1, :], 0.0), True)
        gcar[...] = jnp.broadcast_to(g_first, gcar.shape)
        acar[...] = jnp.broadcast_to(a[0:1, :], acar.shape)
        gv = g_s[...]
        h_before = jnp.where(rb > 0, hp_ref[HALO - 1:HALO, :], 0.0)
        h_prev = jnp.where(row == 0, h_before, pltpu.roll(hv, 1, 0))
        da = gv * h_prev
        gu = gv * uc
        dmult = gu * ig
        dig = gu * mult
        dlog_a = da * a - dmult * jnp.where(mult > 0.0, a * a / mult, 0.0)
        dra = dlog_a * (-LRU_C) * sp * r * (1.0 - r)
        dix = dig * ig * (1.0 - ig)
        dl = jnp.sum(dlog_a * r, axis=0, keepdims=True) * (LRU_C * sgn)
        dra_b, dix_b = dra.astype(MXU), dix.astype(MXU)
        duc = (gv * mult * ig + lax.dot_general(dra_b, wa, _DN["nt"], preferred_element_type=F32)
               + lax.dot_general(dix_b, wx, _DN["nt"], preferred_element_type=F32))
        gwa = lax.dot_general(ucb, dra_b, _DN["tn"], preferred_element_type=F32)
        gwx = lax.dot_general(ucb, dix_b, _DN["tn"], preferred_element_type=F32)
        ext = jnp.concatenate([duc, jnp.where(step > 0, dcar[...], 0.0)], axis=0)
        n = ext.shape[0]
        du = duc * sm_ref[CONV_W - 1:CONV_W, :]
        for k in range(1, CONV_W):
            du = du + pltpu.roll(ext, n - k, 0)[:tr] * sm_ref[CONV_W - 1 - k:CONV_W - k, :]
        duz_ref[0] = du.astype(duz_ref.dtype)
        dcar[...] = duc[:HALO]
        rows = [jnp.sum(duc * taps[CONV_W - 1 - k], axis=0, keepdims=True) for k in range(CONV_W)]
        rows += [jnp.sum(duc, axis=0, keepdims=True), jnp.sum(dra, axis=0, keepdims=True),
                 jnp.sum(dix, axis=0, keepdims=True), dl]

        @pl.when(step == 0)
        def _():
            acc_a[...] = gwa
            acc_x[...] = gwx
            for k, rv in enumerate(rows):
                dsm_ref[k:k + 1, :] = rv

        @pl.when(step > 0)
        def _():
            acc_a[...] += gwa
            acc_x[...] += gwx
            for k, rv in enumerate(rows):
                dsm_ref[k:k + 1, :] += rv

        @pl.when(step == nb - 1)
        def _():
            gwa_ref[...] = acc_a[...].reshape(4, cb // 4, cb).astype(gwa_ref.dtype)
            gwx_ref[...] = acc_x[...].reshape(4, cb // 4, cb).astype(gwx_ref.dtype)

    wspec = pl.BlockSpec((4, None, cb // 4, cb), lambda h, r: (0, h, 0, 0))
    blk = pl.BlockSpec((tr, cb), lambda h, r: (nb - 1 - r, h))
    return _hosting_call(
        body, "lru_bwd", (LRU_HEADS, nb),
        [
            pl.BlockSpec((None, tr, cb), lambda h, r: (0, nb - 1 - r, h)),
            pl.BlockSpec((None, HALO, cb), lambda h, r: (0, jnp.maximum((nb - 1 - r) * hb - 1, 0), h)),
            pl.BlockSpec((None, tr, cb), lambda h, r: (1, nb - 1 - r, h)),
            blk,
            pl.BlockSpec((HALO, cb), lambda h, r: (jnp.maximum((nb - 1 - r) * hb - 1, 0), h)),
            blk,
            wspec, wspec,
            pl.BlockSpec((8, cb), lambda h, r: (0, h)),
        ],
        [
            pl.BlockSpec((2, tr, cb), lambda h, r: (0, nb - 1 - r, h)),
            wspec, wspec,
            pl.BlockSpec((8, cb), lambda h, r: (0, h)),
        ],
        [
            jax.ShapeDtypeStruct((2, s, e), MXU),
            jax.ShapeDtypeStruct(wa_g.shape, WIRE),
            jax.ShapeDtypeStruct(wx_g.shape, WIRE),
            jax.ShapeDtypeStruct((8, e), F32),
        ],
        ([pltpu.VMEM((cb // LANES, tr, LANES), F32)] * 4 + [pltpu.VMEM((tr, cb), F32)]
         + [pltpu.VMEM((cb, cb), F32)] * 2 + [pltpu.VMEM((8, cb), F32)] * 2 + [pltpu.VMEM((HALO, cb), F32)]),
        [uz, uz, uz, hst, hst, dy2, wa_g, wx_g, small], hosted, ("parallel", "arbitrary"))


def _place():
    x, y, c = lax.axis_index("x"), lax.axis_index("y"), lax.axis_index("c")
    chips = [(1 - x, y), (x, 1 - y), (1 - x, 1 - y)]
    return x, y, c, chips


def _rcopy(src, dst, send_sems, recv_sems, k, to):
    return pltpu.make_async_remote_copy(src_ref=src, dst_ref=dst, send_sem=send_sems.at[k], recv_sem=recv_sems.at[k],
                                        device_id=to, device_id_type=MESHID)


def _stage_gather_ici(bufs):
    n = len(bufs)

    def quarter(ref, chip, c, q):
        rq = ref.shape[3] // 2
        return ref.at[2 * chip[0] + chip[1], :, c, pl.ds(q * rq, rq)]

    def copies(refs, ss, rs, off, a, sending):
        x, y, c, _ = _place()
        me, xn, yn, dg = (x, y), (1 - x, y), (x, 1 - y), (1 - x, 1 - y)
        plan = [(0, me, xn, 0), (1, me, xn, 1), (3, me, yn, 1), (2, me, yn, 0),
                (4, xn, yn, 0), (5, yn, xn, 1)]
        if not sending:
            plan = [(0, xn, xn, 0), (1, xn, xn, 1), (3, yn, yn, 1), (2, yn, yn, 0), (4, dg, yn, 0), (5, dg, xn, 1)]
        out = []
        for k, owner, to, q in plan:
            blk = quarter(refs[a], owner, c, q)
            out.append(_rcopy(blk, blk, ss, rs, off + 6 * a + k, (*to, c)))
        return out

    def start(ro, refs, ss, rs, off):
        for a in range(n):
            for cp in copies(refs, ss, rs, off, a, True)[:4]:
                cp.start()

    def mid(ro, refs, ss, rs, off):
        for a in range(n):
            got, out = copies(refs, ss, rs, off, a, False), copies(refs, ss, rs, off, a, True)
            got[0].wait_recv()
            out[4].start()
            got[2].wait_recv()
            out[5].start()

    def finish(ro, refs, ss, rs, off):
        for a in range(n):
            got = copies(refs, ss, rs, off, a, False)
            for k in (1, 3, 4, 5):
                got[k].wait_recv()
            for cp in copies(refs, ss, rs, off, a, True):
                cp.wait_send()

    return dict(ro=[], bufs=list(bufs), nsem=6 * n, start=start, mid=mid, finish=finish)


def _stage_gather_d2d(bufs):
    n = len(bufs)

    def copies(refs, ss, rs, off, sending):
        x, y, c, chips = _place()
        out = []
        for a in range(n):
            for jj, ch in enumerate(chips):
                blk = refs[a].at[2 * ch[0] + ch[1], :, c if sending else 1 - c]
                out.append(_rcopy(blk, blk, ss, rs, off + 3 * a + jj, (x, y, 1 - c)))
        return out

    def start(ro, refs, ss, rs, off):
        for cp in copies(refs, ss, rs, off, True):
            cp.start()

    def finish(ro, refs, ss, rs, off):
        for cp in copies(refs, ss, rs, off, False):
            cp.wait_recv()
        for cp in copies(refs, ss, rs, off, True):
            cp.wait_send()

    return dict(ro=[], bufs=list(bufs), nsem=3 * n, start=start, finish=finish)


def _stage_scatter_ici(parts, gots):
    n = len(parts)

    def copies(ro, refs, ss, rs, off):
        x, y, c, chips = _place()
        return [_rcopy(ro[a].at[2 * ch[0] + ch[1]], refs[a].at[jj], ss, rs, off + 3 * a + jj, (*ch, c))
                for a in range(n) for jj, ch in enumerate(chips)]

    def start(ro, refs, ss, rs, off):
        for cp in copies(ro, refs, ss, rs, off):
            cp.start()

    def finish(ro, refs, ss, rs, off):
        for cp in copies(ro, refs, ss, rs, off):
            cp.wait()

    return dict(ro=list(parts), bufs=list(gots), nsem=3 * n, start=start, finish=finish)


def _stage_send_half(grads, lands):
    n = len(grads)

    def copies(ro, refs, ss, rs, off):
        x, y, c, _ = _place()
        return [_rcopy(ro[a].at[:, :, 1 - c], refs[a], ss, rs, off + a, (x, y, 1 - c)) for a in range(n)]

    def start(ro, refs, ss, rs, off):
        for cp in copies(ro, refs, ss, rs, off):
            cp.start()

    def finish(ro, refs, ss, rs, off):
        for cp in copies(ro, refs, ss, rs, off):
            cp.wait()

    return dict(ro=list(grads), bufs=list(lands), nsem=n, start=start, finish=finish)


def cast_into_slab(w, first, l, k_idx):
    _, r, c = w.shape
    rh = r // 2
    lb, tr = _block_lr(l, rh, c)
    nbh = rh // tr
    assert first % lb == 0

    def body(k_ref, w_ref, o_ref):
        o_ref[...] = w_ref[...].astype(o_ref.dtype)

    return pl.pallas_call(
        body, name="cast_into_slab",
        grid_spec=pltpu.PrefetchScalarGridSpec(
            num_scalar_prefetch=1, grid=(l // lb, 2, nbh),
            in_specs=[pl.BlockSpec((lb, tr, c), lambda i, h, b, k_ref: (first // lb + i, h * nbh + b, 0))],
            out_specs=pl.BlockSpec((None, lb, None, tr, c), lambda i, h, b, k_ref: (k_ref[0], i, h, b, 0))),
        out_shape=jax.ShapeDtypeStruct((4, l, 2, rh, c), WIRE),
        compiler_params=_cp(("parallel", "parallel", "parallel")),
    )(k_idx, w)


def gather_weights(bufs):
    n = len(bufs)
    ici = [_stage_gather_ici([b]) for b in bufs]
    d2d = [_stage_gather_d2d([b]) for b in bufs]
    per = ici[0]["nsem"] + d2d[0]["nsem"]

    def body(*refs):
        outs = refs[n:2 * n]
        ss, rs = refs[2 * n:]
        for what in ("start", "mid"):
            for a in range(n):
                ici[a][what]([], [outs[a]], ss, rs, per * a)
        for a in range(n):
            ici[a]["finish"]([], [outs[a]], ss, rs, per * a)
            d2d[a]["start"]([], [outs[a]], ss, rs, per * a + ici[a]["nsem"])
        for a in range(n):
            d2d[a]["finish"]([], [outs[a]], ss, rs, per * a + ici[a]["nsem"])

    return pl.pallas_call(
        body, name="gather_weights",
        in_specs=[ANY] * n, out_specs=[ANY] * n,
        out_shape=[jax.ShapeDtypeStruct(a.shape, a.dtype) for a in bufs],
        scratch_shapes=[pltpu.SemaphoreType.DMA((per * n,)), pltpu.SemaphoreType.DMA((per * n,))],
        input_output_aliases={a: a for a in range(n)},
        compiler_params=pltpu.CompilerParams(has_side_effects=True),
    )(*bufs)


def all_gather_small(v, name):
    m_per, n = v.shape

    def body(x_ref, out_ref, send_sems, recv_sems, local_sem):
        x, y, c, chips = _place()
        me, sibling = (x, y, c), (x, y, 1 - c)

        def rows(px, py, pc):
            return out_ref.at[pl.ds((4 * px + 2 * py + pc) * m_per, m_per), :]

        def copy(k, block, to, src=None):
            return _rcopy(rows(*block) if src is None else src, rows(*block), send_sems, recv_sems, k, to)

        mine = pltpu.make_async_copy(x_ref, rows(*me), local_sem)
        mine.start()
        first = [copy(0, me, sibling, src=x_ref)]
        first += [copy(1 + jj, me, (*chip, c), src=x_ref) for jj, chip in enumerate(chips)]
        for cp in first:
            cp.start()
        passed = [copy(4 + jj, (*chip, c), sibling) for jj, chip in enumerate(chips)]
        for jj, chip in enumerate(chips):
            copy(1 + jj, (*chip, c), me).wait_recv()
            passed[jj].start()
        copy(0, sibling, me).wait_recv()
        for jj, chip in enumerate(chips):
            copy(4 + jj, (*chip, 1 - c), me).wait_recv()
        for cp in first + passed:
            cp.wait_send()
        mine.wait()

    return pl.pallas_call(
        body, name=name,
        out_shape=jax.ShapeDtypeStruct((8 * m_per, n), v.dtype),
        in_specs=[pl.BlockSpec(memory_space=pltpu.VMEM)],
        out_specs=pl.BlockSpec(memory_space=pltpu.VMEM),
        scratch_shapes=[pltpu.SemaphoreType.DMA((7,)), pltpu.SemaphoreType.DMA((7,)), pltpu.SemaphoreType.DMA],
        compiler_params=pltpu.CompilerParams(vmem_limit_bytes=VMEM_LIMIT),
    )(v)


def sum_devices(g):
    def body(g_ref, o_ref):
        acc = g_ref[0]
        for d in range(1, 8):
            acc = acc + g_ref[d]
        o_ref[...] = acc

    return pl.pallas_call(body, name="sum_devices", out_shape=jax.ShapeDtypeStruct(g.shape[1:], g.dtype),
                          compiler_params=pltpu.CompilerParams(vmem_limit_bytes=VMEM_LIMIT))(g)


def scatter_to_chips(parts, fulls, spans):
    n, m = len(parts), len(fulls)

    def body(*refs):
        ins, outs, shared = refs[:n], refs[n + m:2 * n + m], refs[2 * n + m:2 * n + 2 * m]
        send_sems, recv_sems = refs[2 * n + 2 * m:]
        x, y, c, chips = _place()
        sib = (x, y, 1 - c)
        cps = []
        for a in range(n):
            for jj, ch in enumerate(chips):
                cps.append(_rcopy(ins[a].at[2 * ch[0] + ch[1]], outs[a].at[jj], send_sems, recv_sems, 3 * a + jj, (*ch, c)))

        def blk(a, half):
            return shared[a].at[pl.ds(spans[a][0], spans[a][1]), half]

        swaps = [_rcopy(blk(a, c), blk(a, c), send_sems, recv_sems, 3 * n + a, sib) for a in range(m)]
        for cp in cps + swaps:
            cp.start()
        for cp in cps:
            cp.wait()
        for a in range(m):
            _rcopy(blk(a, 1 - c), blk(a, 1 - c), send_sems, recv_sems, 3 * n + a, sib).wait_recv()
        for cp in swaps:
            cp.wait_send()

    res = pl.pallas_call(
        body, name="scatter_to_chips", in_specs=[ANY] * (n + m), out_specs=[ANY] * (n + m),
        out_shape=[jax.ShapeDtypeStruct((3,) + p.shape[1:], p.dtype) for p in parts]
        + [jax.ShapeDtypeStruct(f.shape, f.dtype) for f in fulls],
        scratch_shapes=[pltpu.SemaphoreType.DMA((3 * n + m,)), pltpu.SemaphoreType.DMA((3 * n + m,))],
        input_output_aliases={n + a: n + a for a in range(m)},
        compiler_params=pltpu.CompilerParams(has_side_effects=True),
    )(*parts, *fulls)
    return list(res[:n]), list(res[n:])


def share_halves(bufs, spans):
    n = len(bufs)

    def body(*refs):
        outs = refs[n:2 * n]
        send_sems, recv_sems = refs[2 * n:]
        x, y, c, _ = _place()
        sib = (x, y, 1 - c)

        def blk(a, half):
            return outs[a].at[pl.ds(spans[a][0], spans[a][1]), half]

        cps = [_rcopy(blk(a, c), blk(a, c), send_sems, recv_sems, a, sib) for a in range(n)]
        for cp in cps:
            cp.start()
        for a in range(n):
            _rcopy(blk(a, 1 - c), blk(a, 1 - c), send_sems, recv_sems, a, sib).wait_recv()
        for cp in cps:
            cp.wait_send()

    return pl.pallas_call(
        body, name="share_halves", in_specs=[ANY] * n, out_specs=[ANY] * n,
        out_shape=[jax.ShapeDtypeStruct(b.shape, b.dtype) for b in bufs],
        scratch_shapes=[pltpu.SemaphoreType.DMA((n,)), pltpu.SemaphoreType.DMA((n,))],
        input_output_aliases={a: a for a in range(n)},
        compiler_params=pltpu.CompilerParams(has_side_effects=True),
    )(*bufs)


def _block_rows(r, c, itemsize, budget=1 << 20):
    tr = r
    while tr * c * itemsize > budget and tr % 16 == 0:
        tr //= 2
    return tr


def _block_lr(l, r, c, budget=4 << 20):
    tr = _block_rows(r, c, 4, budget)
    lb = 1
    if tr == r:
        while l % (2 * lb) == 0 and 2 * lb * r * c * 4 <= budget:
            lb *= 2
    return lb, tr


def add_halves(g, got, c_idx):
    k4, l, _, rh, cc = g.shape
    lb, tr = _block_lr(l, rh, cc)

    def body(c_ref, g_ref, r_ref, o_ref):
        o_ref[...] = (g_ref[...].astype(F32) + r_ref[...].astype(F32)).astype(o_ref.dtype)

    return pl.pallas_call(
        body, name="add_halves",
        grid_spec=pltpu.PrefetchScalarGridSpec(
            num_scalar_prefetch=1, grid=(k4, l // lb, rh // tr),
            in_specs=[pl.BlockSpec((None, lb, None, tr, cc), lambda k, i, b, c_ref: (k, i, c_ref[0], b, 0)),
                      pl.BlockSpec((None, lb, tr, cc), lambda k, i, b, c_ref: (k, i, b, 0))],
            out_specs=pl.BlockSpec((None, lb, tr, cc), lambda k, i, b, c_ref: (k, i, b, 0))),
        out_shape=jax.ShapeDtypeStruct(got.shape, WIRE),
        compiler_params=_cp(("parallel", "parallel", "parallel")),
    )(c_idx, g, got)


def sum_chips(own, got, kc_idx, full, first):
    _, l, rh, cc = own.shape
    lb, tr = _block_lr(l, rh, cc, 2 << 20)
    assert first % lb == 0

    def body(k_ref, o_ref, r_ref, _full, s_ref):
        s_ref[...] = ((o_ref[...].astype(F32) + r_ref[0].astype(F32)) + r_ref[1].astype(F32)) + r_ref[2].astype(F32)

    return pl.pallas_call(
        body, name="sum_chips",
        grid_spec=pltpu.PrefetchScalarGridSpec(
            num_scalar_prefetch=1, grid=(l // lb, rh // tr),
            in_specs=[pl.BlockSpec((None, lb, tr, cc), lambda i, b, k_ref: (k_ref[0], i, b, 0)),
                      pl.BlockSpec((3, lb, tr, cc), lambda i, b, k_ref: (0, i, b, 0)),
                      ANY],
            out_specs=pl.BlockSpec((lb, None, tr, cc), lambda i, b, k_ref: (first // lb + i, k_ref[1], b, 0))),
        out_shape=jax.ShapeDtypeStruct(full.shape, F32),
        input_output_aliases={3: 0},
        compiler_params=_cp(("parallel", "parallel")),
    )(kc_idx, own, got, full)


def _adam_math(w, g, m, v):
    m = ADAM_B1 * m + (1.0 - ADAM_B1) * g
    v = ADAM_B2 * v + (1.0 - ADAM_B2) * (g * g)
    m_hat = m / (1.0 - ADAM_B1 ** ADAM_STEP)
    v_hat = v / (1.0 - ADAM_B2 ** ADAM_STEP)
    delta = -ADAM_LR * (m_hat / (jnp.sqrt(v_hat) + ADAM_EPS) + ADAM_WD * w)
    return delta, m, v


def adamw(w, g, g_first, m, v, lo, hi, prev=None, hosted=()):
    l, r, c = w.shape
    tr = _block_rows(r, c, 4, 2 << 20)
    prev = list(prev or [])

    def body(w_ref, g_ref, m_ref, v_ref, *rest):
        go_ref, d_ref, mo_ref, vo_ref = rest[len(prev):]
        gv = g_ref[...]
        go_ref[...] = gv
        d_ref[...], mo_ref[...], vo_ref[...] = _adam_math(w_ref[...], gv, m_ref[...], v_ref[...])

    spec = pl.BlockSpec((None, tr, c), lambda i, b: (lo + i, b, 0))
    gspec = pl.BlockSpec((None, tr, c), lambda i, b: (g_first + i, b, 0))
    return _hosting_call(
        body, "adamw", (hi - lo, r // tr), [spec, gspec, spec, spec] + [ANY] * len(prev), [spec] * 4,
        [jax.ShapeDtypeStruct(w.shape, F32)] * 4, [], [w, g, m, v] + prev, hosted, ("parallel", "parallel"),
        aliases={4 + t: t for t in range(len(prev))})


def adamw_small(w, g, m, v):
    def body(w_ref, g_ref, m_ref, v_ref, d_ref, mo_ref, vo_ref):
        d_ref[...], mo_ref[...], vo_ref[...] = _adam_math(w_ref[...], g_ref[...], m_ref[...], v_ref[...])

    return pl.pallas_call(body, name="adamw_small", out_shape=[jax.ShapeDtypeStruct(w.shape, F32)] * 3)(w, g, m, v)


def kernel(x, p, w_in, w_out, g_pre, g_post, pool_w, pool_b, pool_scale, conv_w, conv_b, lru_wa, lru_ba, lru_wx, lru_bx, lru_L, w_ple, w_ple_gate, g_ple_in, g_ple_out, loss_target, m_w_in, m_w_out, m_g_pre, m_g_post, m_pool_w, m_pool_b, m_pool_scale, m_conv_w, m_conv_b, m_lru_wa, m_lru_ba, m_lru_wx, m_lru_bx, m_lru_L, m_w_ple, m_w_ple_gate, m_g_ple_in, m_g_ple_out, v_w_in, v_w_out, v_g_pre, v_g_post, v_pool_w, v_pool_b, v_pool_scale, v_conv_w, v_conv_b, v_lru_wa, v_lru_ba, v_lru_wx, v_lru_bx, v_lru_L, v_w_ple, v_w_ple_gate, v_g_ple_in, v_g_ple_out):
    depth = w_in.shape[0]
    _, s, d = x.shape
    e = 2 * d
    kp = p.shape[-1]
    nmix = pool_w.shape[0]
    ngrp = pool_w.shape[1]
    cg = e // ngrp
    cb = e // LRU_HEADS
    xi, yi, ci = lax.axis_index("x"), lax.axis_index("y"), lax.axis_index("c")
    me = 2 * xi + yi
    c_idx = jnp.reshape(ci, (1,)).astype(jnp.int32)
    k_idx = jnp.reshape(me, (1,)).astype(jnp.int32)
    tr_row = _tile(s, 512)
    tr_mix = _tile(s, 512)
    tr_lru = _tile(s, 1024)
    tm = _tile(s, 1024)
    tm2 = _tile(s, 2048)

    def halves(a):
        return a.reshape(a.shape[0], 2, a.shape[1] // 2, a.shape[2])

    big = {
        "w_in": w_in, "w_out": w_out, "gate": w_ple_gate, "ple": w_ple,
        "pool": pool_w.reshape(nmix * ngrp, cg // 4, cg),
        "wa": lru_wa.reshape(nmix * LRU_HEADS, cb // 4, cb), "wx": lru_wx.reshape(nmix * LRU_HEADS, cb // 4, cb),
    }
    names = list(big)

    def layer_shards(i):
        sh = {n: (big[n], i, 1) for n in ("w_in", "w_out", "gate", "ple")}
        if i % 2 == 0:
            sh["pool"] = (big["pool"], (i // 2) * ngrp, ngrp)
        else:
            sh["wa"], sh["wx"] = (big["wa"], (i // 2) * LRU_HEADS, LRU_HEADS), (big["wx"], (i // 2) * LRU_HEADS, LRU_HEADS)
        return sh

    def mixer_names(i):
        return ["pool"] if i % 2 == 0 else ["wa", "wx"]

    wbuf = [{n: cast_into_slab(*wfl, k_idx) for n, wfl in layer_shards(i).items()} for i in range(depth)]
    first = list(wbuf[0])
    wbuf[0] = dict(zip(first, gather_weights([wbuf[0][n] for n in first])))

    def full_w(i, n):
        b = wbuf[i][n]
        return b.reshape(b.shape[0], b.shape[1], 2 * b.shape[3], b.shape[4])

    def stages_of(specs):
        return [mk([wbuf[l][n] for n in nms]) for mk, l, nms in specs]

    def keep(specs, new):
        for (l, n), b in zip([(l, n) for _, l, nms in specs for n in nms], new):
            wbuf[l][n] = b

    def run_mm(specs, *args):
        if not specs:
            return _matmul(*args)
        out, new = _matmul(*args, hosted=stages_of(specs))
        keep(specs, new)
        return out

    ec = e // 4
    small_loc = jnp.concatenate([conv_w, conv_b[:, None], lru_ba[:, None], lru_bx[:, None], lru_L[:, None]], axis=1)
    sm_all = all_gather_small(small_loc.reshape(nmix * 8, ec), "gather_small").reshape(4, 2, nmix, 8, ec)
    lru_small = jnp.transpose(sm_all[:, 0], (1, 2, 0, 3)).reshape(nmix, 8, e)

    xs = x[0]
    saved = []
    for i in range(depth):
        j = i // 2
        h = rms_fwd(xs, g_pre[i][None], tr_row) if i == 0 else h_next
        nj = (2 * e) // 1024 if (2 * e) % 1024 == 0 else 4
        tn = (2 * e) // nj
        per = e // tn
        perk = (e // 2) // tn
        nxt = i + 1 if i + 1 < depth else None
        stages = [(_stage_gather_d2d, i, ["gate", "ple", "w_out"] + mixer_names(i))] if i > 0 else []
        if nxt is not None:
            stages.append((_stage_gather_ici, nxt, ["w_in"]))
        uz = run_mm(
            stages, "mm_in", "nn", (s // tm2, nj, 1), h, pl.BlockSpec((tm2, d), lambda a, b, k: (a, 0)),
            full_w(i, "w_in"), pl.BlockSpec((None, None, d, tn), lambda a, b, k, perk=perk: (b // perk, 0, 0, b % perk)),
            jax.ShapeDtypeStruct((2, s, e), MXU), pl.BlockSpec((None, tm2, tn), lambda a, b, k, per=per: (b // per, a, b % per)),
            (8, 128))
        specs = []
        if nxt is not None:
            specs = [(_stage_gather_ici, nxt, mixer_names(nxt) + (["w_out"] if i % 2 else []))]
        if i % 2 == 0:
            (y2,), new = pool_fwd(uz, full_w(i, "pool"), pool_b[j][None], pool_scale[j][None], tr_mix, hosted=stages_of(specs))
            hst = None
        else:
            (y2, hst), new = lru_fwd(uz, full_w(i, "wa"), full_w(i, "wx"), lru_small[j], tr_lru, hosted=stages_of(specs))
        keep(specs, new)
        tn_o = _tile(d, 1024)
        stages = []
        if nxt is not None:
            stages = [(_stage_gather_ici, nxt, ["w_out"] if i % 2 == 0 else ["gate", "ple"])]
        o = run_mm(
            stages, "mm_out", "nn", (s // tm, d // tn_o, 1), y2, pl.BlockSpec((tm, e), lambda a, b, k: (a, 0)),
            full_w(i, "w_out"), pl.BlockSpec((4, None, e // 4, tn_o), lambda a, b, k: (0, 0, 0, b)),
            jax.ShapeDtypeStruct((s, d), MXU), pl.BlockSpec((tm, tn_o), lambda a, b, k: (a, b)), (8, 128))
        x1, hn = res_rms_fwd(xs, o, g_post[i][None], g_ple_in[i][None], tr_row)
        stages = []
        if nxt is not None:
            if i % 2 == 0:
                stages = [(_stage_gather_ici, nxt, ["gate", "ple"])]
        gpre = run_mm(
            stages, "mm_gate", "nn", (s // tm2, d // tn_o, 1), hn, pl.BlockSpec((tm2, d), lambda a, b, k: (a, 0)),
            full_w(i, "gate"), pl.BlockSpec((4, None, d // 4, tn_o), lambda a, b, k: (0, 0, 0, b)),
            jax.ShapeDtypeStruct((s, d), MXU), pl.BlockSpec((tm2, tn_o), lambda a, b, k: (a, b)), (8, 128))
        pe = p[i, 0]
        ev = run_mm(
            [(_stage_gather_d2d, nxt, ["w_in"])] if nxt is not None else [],
            "mm_ple", "nn", (s // tm, 4, 1), pe, pl.BlockSpec((tm, kp), lambda a, b, k: (a, 0)),
            full_w(i, "ple"), pl.BlockSpec((None, None, kp, d // 4), lambda a, b, k: (b, 0, 0, 0)),
            jax.ShapeDtypeStruct((s, d), MXU), pl.BlockSpec((tm, d // 4), lambda a, b, k: (a, b)), (8, 128))
        if nxt is not None:
            x2, h_next = ple_fwd(x1, gpre, ev, g_ple_out[i][None], g_pre[nxt][None], tr_row)
        else:
            x2 = None
            dx, sq = ple_fwd_loss(x1, gpre, ev, g_ple_out[i][None], loss_target[0], tr_row)
        saved.append((xs, h, uz, y2, hst, o, x1, hn, gpre, ev))
        xs = x2

    d_gpre, d_gpost, d_gin, d_gout = [None] * depth, [None] * depth, [None] * depth, [None] * depth
    d_pool_b, d_pool_sc, d_lru_small = [None] * nmix, [None] * nmix, [None] * nmix
    ts = _tile(s, 1024)
    kc_idx = jnp.stack([me, ci]).astype(jnp.int32)
    full = {n: lax.empty(halves(big[n]).shape, F32) for n in names}
    w_sum = w_got = None

    def first_row(i, n):
        return i if n in ("w_in", "w_out", "gate", "ple") else (i // 2) * (ngrp if n == "pool" else LRU_HEADS)

    for i in reversed(range(depth)):
        j = i // 2
        prev = i + 1 if i + 1 < depth else None
        x0, h, uz, y2, hst, o, x1, hn, gpre, ev = saved[i]
        pe = p[i, 0]
        gl = {}
        de, dgp, d_gout[i] = ple_bwd(dx, gpre, ev, g_ple_out[i][None], tr_row)
        gl["ple"] = _matmul(
            "mm_dple", "tn", (1, 4, s // ts), pe, pl.BlockSpec((ts, kp), lambda a, b, k: (k, 0)),
            de, pl.BlockSpec((ts, d // 4), lambda a, b, k: (k, b)),
            jax.ShapeDtypeStruct((4, 1, kp, d // 4), WIRE), pl.BlockSpec((None, None, kp, d // 4), lambda a, b, k: (b, 0, 0, 0)),
            (kp, d // 4))
        tn_o = _tile(d, 1024)
        gl["gate"] = _matmul(
            "mm_dgate", "tn", (4, d // tn_o, 1), hn, pl.BlockSpec((s, d // 4), lambda a, b, k: (0, a)),
            dgp, pl.BlockSpec((s, tn_o), lambda a, b, k: (0, b)),
            jax.ShapeDtypeStruct((4, 1, d // 4, d), WIRE), pl.BlockSpec((None, None, d // 4, tn_o), lambda a, b, k: (a, 0, 0, b)),
            (8, 128))
        dhn = _matmul(
            "mm_dhn", "nt", (s // tm2, 4, 1), dgp, pl.BlockSpec((tm2, d), lambda a, b, k: (a, 0)),
            full_w(i, "gate"), pl.BlockSpec((None, None, d // 4, d), lambda a, b, k: (b, 0, 0, 0)),
            jax.ShapeDtypeStruct((s, d), MXU), pl.BlockSpec((tm2, d // 4), lambda a, b, k: (a, b)), (8, 128))
        dx1, do, d_gin[i], d_gpost[i] = post_bwd(dx, dhn, x1, g_ple_in[i][None], o, g_post[i][None], tr_row)
        gl["w_out"] = _matmul(
            "mm_dwout", "tn", (4, d // tn_o, 1), y2, pl.BlockSpec((s, e // 4), lambda a, b, k: (0, a)),
            do, pl.BlockSpec((s, tn_o), lambda a, b, k: (0, b)),
            jax.ShapeDtypeStruct((4, 1, e // 4, d), WIRE), pl.BlockSpec((None, None, e // 4, tn_o), lambda a, b, k: (a, 0, 0, b)),
            (8, 128))
        dy2 = _matmul(
            "mm_dy2", "nt", (s // tm2, 4, 1), do, pl.BlockSpec((tm2, d), lambda a, b, k: (a, 0)),
            full_w(i, "w_out"), pl.BlockSpec((None, None, e // 4, d), lambda a, b, k: (b, 0, 0, 0)),
            jax.ShapeDtypeStruct((s, e), MXU), pl.BlockSpec((tm2, e // 4), lambda a, b, k: (a, b)), (8, 128))
        riding = [_stage_scatter_ici([w_sum], [w_got])] if prev is not None else []
        if i % 2 == 0:
            (duz, gl["pool"], d_pool_b[j], d_pool_sc[j]), passed = pool_bwd(
                uz, dy2, full_w(i, "pool"), pool_b[j][None], pool_scale[j][None], tr_mix, hosted=riding)
        else:
            (duz, gl["wa"], gl["wx"], d_lru_small[j]), passed = lru_bwd(
                uz, hst, dy2, full_w(i, "wa"), full_w(i, "wx"), lru_small[j], tr_lru, hosted=riding)
        if prev is not None:
            full["w_in"] = sum_chips(w_sum, passed[0], kc_idx, full["w_in"], first_row(prev, "w_in"))

        def in_halves(g):
            return g.reshape(4, g.shape[1], 2, g.shape[2] // 2, g.shape[3])

        early = list(gl)
        eparts = [in_halves(gl[n]) for n in early]
        tmi = _tile(d, 1024)
        tni = _tile(e // 2, 1024)
        nslab = (e // 2) // tni
        gl["w_in"], from_sib = _matmul(
            "mm_dwin", "tn", (d // tmi, 4 * nslab, 1), h, pl.BlockSpec((s, tmi), lambda a, b, k: (0, a)),
            duz, pl.BlockSpec((None, s, tni), lambda a, b, k, nslab=nslab: (b // (2 * nslab), 0, b % (2 * nslab))),
            jax.ShapeDtypeStruct((4, 1, d, e // 2), WIRE),
            pl.BlockSpec((None, None, tmi, tni), lambda a, b, k, nslab=nslab: (b // nslab, 0, a, b % nslab)),
            (8, 128), hosted=[_stage_send_half(eparts, [lax.empty(g.shape[:2] + g.shape[3:], WIRE) for g in eparts])])
        esums = [add_halves(g, r, c_idx) for g, r in zip(eparts, from_sib)]
        wpart = in_halves(gl["w_in"])
        tnd = _tile(d, 1024)
        dh, passed = _matmul(
            "mm_dh", "nt", (s // tm2, d // tnd, 4), duz, pl.BlockSpec((None, tm2, e // 2), lambda a, b, k: (k // 2, a, k % 2)),
            full_w(i, "w_in"), pl.BlockSpec((None, None, tnd, e // 2), lambda a, b, k: (k, 0, b, 0)),
            jax.ShapeDtypeStruct((s, d), MXU), pl.BlockSpec((tm2, tnd), lambda a, b, k: (a, b)), (tm2, tnd),
            hosted=[_stage_scatter_ici(esums, [lax.empty((3,) + q.shape[1:], WIRE) for q in esums]),
                    _stage_send_half([wpart], [lax.empty(wpart.shape[:2] + wpart.shape[3:], WIRE)])])
        for n, q, got in zip(early, esums, passed[:len(early)]):
            full[n] = sum_chips(q, got, kc_idx, full[n], first_row(i, n))
        w_sum = add_halves(wpart, passed[len(early)], c_idx)
        w_got = lax.empty((3,) + w_sum.shape[1:], WIRE)
        dx, d_gpre[i] = rms_bwd_res(dx1, dh, x0, g_pre[i][None], tr_row)
    grad_x = dx[None]

    got, swapped = scatter_to_chips([w_sum], [full[n] for n in names],
                                    [(1, depth - 1) if n == "w_in" else (0, full[n].shape[0]) for n in names])
    full = dict(zip(names, swapped))
    full["w_in"] = share_halves([sum_chips(w_sum, got[0], kc_idx, full["w_in"], 0)], [(0, 1)])[0]
    grads = {n: full[n].reshape(big[n].shape) for n in names}

    def rows_e(a):
        return jnp.stack(a).reshape(-1, e) if isinstance(a, list) else a.reshape(-1, e)

    pack = [rows_e([g[0] for g in d_gpre]), rows_e([g[0] for g in d_gpost]), rows_e([g[0] for g in d_gin]), rows_e([g[0] for g in d_gout]),
            jnp.concatenate(d_pool_b, axis=0), jnp.concatenate(d_pool_sc, axis=0), jnp.concatenate(d_lru_small, axis=0),
            jnp.pad(sq, ((0, 0), (0, e - d)))]
    sizes = [a.shape[0] for a in pack]
    packed = jnp.concatenate(pack, axis=0)
    nrow = packed.shape[0]
    nrow_p = -(-nrow // 8) * 8
    packed = jnp.pad(packed, ((0, nrow_p - nrow), (0, 0)))
    total = sum_devices(all_gather_small(packed, "gather_grads").reshape(8, nrow_p, e))
    parts, off = [], 0
    for n_ in sizes:
        parts.append(total[off:off + n_])
        off += n_
    t_gpre, t_gpost, t_gin, t_gout, t_pb, t_psc, t_lru, t_sq = parts
    loss = 0.5 * jnp.sum(t_sq) / d
    t_lru = t_lru.reshape(nmix, 8, e)
    t_lru_loc = lax.dynamic_slice_in_dim(t_lru, me * ec, ec, axis=2)

    def big_update(name, w, m, v):
        shp = big[name].shape
        res4, _ = adamw(w.reshape(shp), grads[name], 0, m.reshape(shp), v.reshape(shp), 0, shp[0])
        return [a.reshape(w.shape) for a in res4]

    def small_update(w, g, m, v):
        shp = w.shape
        w2 = w.reshape(-1, shp[-1])
        dl, nm, nv = adamw_small(w2, g.reshape(w2.shape), m.reshape(w2.shape), v.reshape(w2.shape))
        return [g.reshape(shp), dl.reshape(shp), nm.reshape(shp), nv.reshape(shp)]

    res = {
        "w_in": big_update("w_in", w_in, m_w_in, v_w_in),
        "w_out": big_update("w_out", w_out, m_w_out, v_w_out),
        "g_pre": small_update(g_pre, t_gpre.reshape(depth, d), m_g_pre, v_g_pre),
        "g_post": small_update(g_post, t_gpost.reshape(depth, d), m_g_post, v_g_post),
        "pool_w": big_update("pool", pool_w, m_pool_w, v_pool_w),
        "pool_b": small_update(pool_b, t_pb, m_pool_b, v_pool_b),
        "pool_scale": small_update(pool_scale, t_psc, m_pool_scale, v_pool_scale),
        "conv_w": small_update(conv_w, t_lru_loc[:, :CONV_W], m_conv_w, v_conv_w),
        "conv_b": small_update(conv_b, t_lru_loc[:, 4], m_conv_b, v_conv_b),
        "lru_wa": big_update("wa", lru_wa, m_lru_wa, v_lru_wa),
        "lru_ba": small_update(lru_ba, t_lru_loc[:, 5], m_lru_ba, v_lru_ba),
        "lru_wx": big_update("wx", lru_wx, m_lru_wx, v_lru_wx),
        "lru_bx": small_update(lru_bx, t_lru_loc[:, 6], m_lru_bx, v_lru_bx),
        "lru_L": small_update(lru_L, t_lru_loc[:, 7], m_lru_L, v_lru_L),
        "w_ple": big_update("ple", w_ple, m_w_ple, v_w_ple),
        "w_ple_gate": big_update("gate", w_ple_gate, m_w_ple_gate, v_w_ple_gate),
        "g_ple_in": small_update(g_ple_in, t_gin.reshape(depth, d), m_g_ple_in, v_g_ple_in),
        "g_ple_out": small_update(g_ple_out, t_gout.reshape(depth, d), m_g_ple_out, v_g_ple_out),
    }
    order = ["w_in", "w_out", "g_pre", "g_post", "pool_w", "pool_b", "pool_scale", "conv_w", "conv_b", "lru_wa", "lru_ba",
             "lru_wx", "lru_bx", "lru_L", "w_ple", "w_ple_gate", "g_ple_in", "g_ple_out"]
    out = [loss, grad_x]
    for slot in range(4):
        out += [res[n][slot] for n in order]
    return tuple(out)
```

```python
import functools

import jax
import jax.numpy as jnp
from jax import lax
from jax.experimental import pallas as pl
from jax.experimental.pallas import tpu as pltpu

F32 = jnp.float32
MXU = jnp.bfloat16
WIRE = jnp.bfloat16
VMEM_LIMIT = 56 * 1024 * 1024
RMS_EPS = 1e-6
LRU_C = 8.0
POOL_WINDOWS = (2, 4, 8, 16)
MAXW = 16
CONV_W = 4
LRU_HEADS = 16
ADAM_LR, ADAM_B1, ADAM_B2, ADAM_EPS, ADAM_WD, ADAM_STEP = 0.001, 0.9, 0.999, 1e-08, 0.01, 10
MESHID = pl.DeviceIdType.MESH
ANY = pl.BlockSpec(memory_space=pl.ANY)


def _cp(sem=None):
    return pltpu.CompilerParams(dimension_semantics=sem, vmem_limit_bytes=VMEM_LIMIT)


def _sig(v):
    return 0.5 * jnp.tanh(0.5 * v) + 0.5


def _tile(n, pref):
    return pref if n % pref == 0 else n


_DN = {"nn": (((1,), (0,)), ((), ())), "nt": (((1,), (1,)), ((), ())), "tn": (((0,), (0,)), ((), ()))}


def _hosting_call(body, name, grid, in_specs, out_specs, out_shape, scratch_shapes, args, hosted, sem, aliases=None):
    n_in, n_out = len(args), len(out_shape)
    aliases = dict(aliases or {})
    if not hosted:
        res = pl.pallas_call(body, name=name, grid=grid, in_specs=in_specs, out_specs=out_specs, out_shape=out_shape,
                             scratch_shapes=scratch_shapes, input_output_aliases=aliases, compiler_params=_cp(sem))(*args)
        return list(res), []
    ro = [r for st in hosted for r in st["ro"]]
    bufs = [r for st in hosted for r in st["bufs"]]
    nro, nbuf = len(ro), len(bufs)
    nsem = sum(st["nsem"] for st in hosted)
    total = 1
    for g in grid:
        total *= g

    def wrapped(*refs):
        ins, ro_refs = refs[:n_in], refs[n_in:n_in + nro]
        outs = refs[n_in + nro + nbuf:n_in + nro + nbuf + n_out]
        buf_refs = refs[n_in + nro + nbuf + n_out:n_in + nro + 2 * nbuf + n_out]
        scr, ss, rs = refs[n_in + nro + 2 * nbuf + n_out:-2], refs[-2], refs[-1]
        step = 0
        for t, g in enumerate(grid):
            step = step * g + pl.program_id(t)

        def run_stages(what):
            r0 = b0 = s0 = 0
            for st in hosted:
                if what in st:
                    st[what](ro_refs[r0:r0 + len(st["ro"])], buf_refs[b0:b0 + len(st["bufs"])], ss, rs, s0)
                r0, b0, s0 = r0 + len(st["ro"]), b0 + len(st["bufs"]), s0 + st["nsem"]

        @pl.when(step == 0)
        def _():
            run_stages("start")

        if any("mid" in st for st in hosted):
            @pl.when(step == total // 2)
            def _():
                run_stages("mid")

        body(*ins, *outs, *scr)

        @pl.when(step == total - 1)
        def _():
            run_stages("finish")

    res = pl.pallas_call(
        wrapped, name=name, grid=grid,
        in_specs=list(in_specs) + [ANY] * (nro + nbuf), out_specs=list(out_specs) + [ANY] * nbuf,
        out_shape=list(out_shape) + [jax.ShapeDtypeStruct(x.shape, x.dtype) for x in bufs],
        scratch_shapes=list(scratch_shapes) + [pltpu.SemaphoreType.DMA((nsem,)), pltpu.SemaphoreType.DMA((nsem,))],
        input_output_aliases={**aliases, **{n_in + nro + t: n_out + t for t in range(nbuf)}},
        compiler_params=_cp(("arbitrary",) * len(grid)),
    )(*args, *ro, *bufs)
    return list(res[:n_out]), list(res[n_out:])


def _matmul(name, mode, grid, a, a_spec, b, b_spec, out_shape, out_spec, acc_shape, hosted=()):
    nk = grid[2]

    def body(a_ref, b_ref, o_ref, acc_ref):
        kk = pl.program_id(2)
        bv = b_ref[...]
        if bv.ndim == 3:
            bv = bv.reshape(bv.shape[0] * bv.shape[1], bv.shape[2])
        prod = lax.dot_general(a_ref[...].astype(MXU), bv.astype(MXU), _DN[mode], preferred_element_type=F32)
        if nk == 1:
            o_ref[...] = prod.astype(o_ref.dtype)
        else:
            @pl.when(kk == 0)
            def _():
                acc_ref[...] = prod

            @pl.when(kk > 0)
            def _():
                acc_ref[...] += prod

            @pl.when(kk == nk - 1)
            def _():
                o_ref[...] = acc_ref[...].astype(o_ref.dtype)

    outs, passed = _hosting_call(body, name, grid, [a_spec, b_spec], [out_spec], [out_shape], [pltpu.VMEM(acc_shape, F32)],
                                 [a, b], hosted, ("parallel", "parallel", "arbitrary"))
    return (outs[0], passed) if hosted else outs[0]


def _rows_call(name, body, ins, in_rows, outs, out_rows, n_rows, tr):
    def spec(shape, tiled):
        if tiled:
            return pl.BlockSpec((tr, shape[1]), lambda i: (i, 0))
        return pl.BlockSpec(shape, lambda i: (0, 0))

    return pl.pallas_call(
        body, name=name, grid=(n_rows // tr,),
        in_specs=[spec(a.shape, t) for a, t in zip(ins, in_rows)],
        out_specs=[spec(o.shape, t) for o, t in zip(outs, out_rows)],
        out_shape=outs, compiler_params=_cp(("arbitrary",)),
    )(*ins)


def _rstd(v):
    return lax.rsqrt(jnp.mean(v * v, axis=-1, keepdims=True) + RMS_EPS)


def _norm_bwd(v, g, dy):
    r = _rstd(v)
    n = v * r
    dn = dy * g
    dv = r * (dn - n * jnp.mean(dn * n, axis=-1, keepdims=True))
    return dv, jnp.sum(dy * n, axis=0, keepdims=True)


def _acc_rows(ref, val):
    @pl.when(pl.program_id(0) == 0)
    def _():
        ref[...] = val

    @pl.when(pl.program_id(0) > 0)
    def _():
        ref[...] += val


def rms_fwd(x, g, tr):
    def body(x_ref, g_ref, o_ref):
        v = x_ref[...]
        o_ref[...] = (v * _rstd(v) * g_ref[...]).astype(o_ref.dtype)

    return _rows_call("rms_fwd", body, [x, g], [True, False], [jax.ShapeDtypeStruct(x.shape, MXU)], [True], x.shape[0], tr)[0]


def res_rms_fwd(x, o, g, g_next, tr):
    def body(x_ref, o_ref, g_ref, gn_ref, y_ref, h_ref):
        v = o_ref[...].astype(F32)
        y = x_ref[...] + v * _rstd(v) * g_ref[...]
        y_ref[...] = y
        h_ref[...] = (y * _rstd(y) * gn_ref[...]).astype(h_ref.dtype)

    return _rows_call("res_rms_fwd", body, [x, o, g, g_next], [True, True, False, False],
                      [jax.ShapeDtypeStruct(x.shape, F32), jax.ShapeDtypeStruct(x.shape, MXU)], [True, True], x.shape[0], tr)


def ple_fwd(x1, gpre, e, g, g_next, tr):
    def body(x_ref, gp_ref, e_ref, g_ref, gn_ref, y_ref, h_ref):
        v = e_ref[...].astype(F32) * _sig(gp_ref[...].astype(F32))
        y = x_ref[...] + v * _rstd(v) * g_ref[...]
        y_ref[...] = y
        h_ref[...] = (y * _rstd(y) * gn_ref[...]).astype(h_ref.dtype)

    return _rows_call("ple_fwd", body, [x1, gpre, e, g, g_next], [True, True, True, False, False],
                      [jax.ShapeDtypeStruct(x1.shape, F32), jax.ShapeDtypeStruct(x1.shape, MXU)], [True, True], x1.shape[0], tr)


def ple_fwd_loss(x1, gpre, e, g, target, tr):
    d = x1.shape[1]

    def body(x_ref, gp_ref, e_ref, g_ref, t_ref, dy_ref, sq_ref):
        v = e_ref[...].astype(F32) * _sig(gp_ref[...].astype(F32))
        diff = x_ref[...] + v * _rstd(v) * g_ref[...] - t_ref[...]
        dy_ref[...] = diff * (1.0 / d)
        _acc_rows(sq_ref, jnp.sum(diff * diff, axis=0, keepdims=True))

    return _rows_call("ple_fwd_loss", body, [x1, gpre, e, g, target], [True, True, True, False, True],
                      [jax.ShapeDtypeStruct(x1.shape, F32), jax.ShapeDtypeStruct((1, d), F32)], [True, False], x1.shape[0], tr)


def ple_bwd(dx2, gpre, e, g, tr):
    d = dx2.shape[1]

    def body(dx_ref, gp_ref, e_ref, g_ref, de_ref, dgp_ref, dg_ref):
        gate = _sig(gp_ref[...].astype(F32))
        ev = e_ref[...].astype(F32)
        dv, dg = _norm_bwd(ev * gate, g_ref[...], dx_ref[...])
        de_ref[...] = (dv * gate).astype(de_ref.dtype)
        dgp_ref[...] = (dv * ev * gate * (1.0 - gate)).astype(dgp_ref.dtype)
        _acc_rows(dg_ref, dg)

    return _rows_call("ple_bwd", body, [dx2, gpre, e, g], [True, True, True, False],
                      [jax.ShapeDtypeStruct(dx2.shape, MXU), jax.ShapeDtypeStruct(dx2.shape, MXU), jax.ShapeDtypeStruct((1, d), F32)],
                      [True, True, False], dx2.shape[0], tr)


def rms_bwd_res(dres, dh, x, g, tr):
    d = x.shape[1]

    def body(dr_ref, dh_ref, x_ref, g_ref, dx_ref, dg_ref):
        dv, dg = _norm_bwd(x_ref[...], g_ref[...], dh_ref[...].astype(F32))
        dx_ref[...] = dr_ref[...] + dv
        _acc_rows(dg_ref, dg)

    return _rows_call("rms_bwd_res", body, [dres, dh, x, g], [True, True, True, False],
                      [jax.ShapeDtypeStruct(x.shape, F32), jax.ShapeDtypeStruct((1, d), F32)], [True, False], x.shape[0], tr)


def post_bwd(dres, dh, x, g, o, g_o, tr):
    d = x.shape[1]

    def body(dr_ref, dh_ref, x_ref, g_ref, o_ref, go_ref, dx_ref, do_ref, dg_ref, dgo_ref):
        dv, dg = _norm_bwd(x_ref[...], g_ref[...], dh_ref[...].astype(F32))
        dxv = dr_ref[...] + dv
        dx_ref[...] = dxv
        dov, dgo = _norm_bwd(o_ref[...].astype(F32), go_ref[...], dxv)
        do_ref[...] = dov.astype(do_ref.dtype)
        _acc_rows(dg_ref, dg)
        _acc_rows(dgo_ref, dgo)

    return _rows_call("post_bwd", body, [dres, dh, x, g, o, g_o], [True, True, True, False, True, False],
                      [jax.ShapeDtypeStruct(x.shape, F32), jax.ShapeDtypeStruct(x.shape, MXU),
                       jax.ShapeDtypeStruct((1, d), F32), jax.ShapeDtypeStruct((1, d), F32)], [True, True, False, False], x.shape[0], tr)


def _trailing_sum(ext, w):
    s, k = ext, 1
    while k < w:
        s = s + pltpu.roll(s, k, 0)
        k *= 2
    return s


def _leading_sum(ext, w):
    n = ext.shape[0]
    s, k = ext, 1
    while k < w:
        s = s + pltpu.roll(s, n - k, 0)
        k *= 2
    return s


def _pool_inv_count(rb, tr, w, c):
    t = rb * tr + lax.broadcasted_iota(jnp.int32, (tr, c), 0)
    return 1.0 / jnp.minimum(t + 1, w).astype(F32)


def _pool_d(u_ref, up_ref, rb, tr, w):
    cur = u_ref[...].astype(F32)
    prev = jnp.where(rb > 0, up_ref[...].astype(F32), 0.0)
    ext = jnp.concatenate([prev, cur], axis=0)
    win = _trailing_sum(ext, w)[MAXW:]
    return win * _pool_inv_count(rb, tr, w, cur.shape[1]) - cur


def pool_fwd(uz, w_g, bias, scale, tr, hosted=()):
    _, s, e = uz.shape
    ng = len(POOL_WINDOWS)
    cg = e // ng
    nb = s // tr
    hb = tr // MAXW

    def body(u_ref, up_ref, z_ref, w_ref, b_ref, sc_ref, o_ref):
        g, rb = pl.program_id(0), pl.program_id(1)
        wmat = w_ref[...].reshape(cg, cg)
        for gg, win in enumerate(POOL_WINDOWS):
            @pl.when(g == gg)
            def _(win=win):
                d = _pool_d(u_ref, up_ref, rb, tr, win)
                y = (jnp.dot(d.astype(MXU), wmat, preferred_element_type=F32) + b_ref[...]) * sc_ref[...]
                z = z_ref[...].astype(F32)
                o_ref[...] = (y * z * _sig(z)).astype(o_ref.dtype)

    return _hosting_call(
        body, "pool_fwd", (ng, nb),
        [
            pl.BlockSpec((None, tr, cg), lambda g, r: (0, r, g)),
            pl.BlockSpec((None, MAXW, cg), lambda g, r: (0, jnp.maximum(r * hb - 1, 0), g)),
            pl.BlockSpec((None, tr, cg), lambda g, r: (1, r, g)),
            pl.BlockSpec((4, None, cg // 4, cg), lambda g, r: (0, g, 0, 0)),
            pl.BlockSpec((1, cg), lambda g, r: (0, g)),
            pl.BlockSpec((1, cg), lambda g, r: (0, g)),
        ],
        [pl.BlockSpec((tr, cg), lambda g, r: (r, g))], [jax.ShapeDtypeStruct((s, e), MXU)], [],
        [uz, uz, uz, w_g, bias, scale], hosted, ("parallel", "arbitrary"))


def pool_bwd(uz, dy2, w_g, bias, scale, tr, hosted=()):
    _, s, e = uz.shape
    ng = len(POOL_WINDOWS)
    cg = e // ng
    nb = s // tr
    hb = tr // MAXW

    def body(u_ref, up_ref, z_ref, dy_ref, w_ref, b_ref, sc_ref, duz_ref, gw_ref, db_ref, dsc_ref, acc_ref, carry_ref):
        g, step = pl.program_id(0), pl.program_id(1)
        rb = nb - 1 - step
        wmat = w_ref[...].reshape(cg, cg)
        for gg, win in enumerate(POOL_WINDOWS):
            @pl.when(g == gg)
            def _(win=win):
                d = _pool_d(u_ref, up_ref, rb, tr, win).astype(MXU)
                ypre = jnp.dot(d, wmat, preferred_element_type=F32) + b_ref[...]
                z = z_ref[...].astype(F32)
                sg = _sig(z)
                dy2v = dy_ref[...].astype(F32)
                dyv = dy2v * z * sg
                duz_ref[1] = (dy2v * ypre * sc_ref[...] * sg * (1.0 + z * (1.0 - sg))).astype(duz_ref.dtype)
                dypre = dyv * sc_ref[...]
                dsc = jnp.sum(dyv * ypre, axis=0, keepdims=True)
                dbv = jnp.sum(dypre, axis=0, keepdims=True)
                dypre_b = dypre.astype(MXU)
                dd = lax.dot_general(dypre_b, wmat, _DN["nt"], preferred_element_type=F32)
                gw = lax.dot_general(d, dypre_b, _DN["tn"], preferred_element_type=F32)
                q = dd * _pool_inv_count(rb, tr, win, cg)
                nxt = jnp.where(step > 0, carry_ref[...], 0.0)
                lead = _leading_sum(jnp.concatenate([q, nxt], axis=0), win)[:tr]
                duz_ref[0] = (lead - dd).astype(duz_ref.dtype)
                carry_ref[...] = q[:MAXW]

                @pl.when(step == 0)
                def _():
                    acc_ref[...] = gw
                    db_ref[...] = dbv
                    dsc_ref[...] = dsc

                @pl.when(step > 0)
                def _():
                    acc_ref[...] += gw
                    db_ref[...] += dbv
                    dsc_ref[...] += dsc

                @pl.when(step == nb - 1)
                def _():
                    gw_ref[...] = acc_ref[...].reshape(4, cg // 4, cg).astype(gw_ref.dtype)

    return _hosting_call(
        body, "pool_bwd", (ng, nb),
        [
            pl.BlockSpec((None, tr, cg), lambda g, r: (0, nb - 1 - r, g)),
            pl.BlockSpec((None, MAXW, cg), lambda g, r: (0, jnp.maximum((nb - 1 - r) * hb - 1, 0), g)),
            pl.BlockSpec((None, tr, cg), lambda g, r: (1, nb - 1 - r, g)),
            pl.BlockSpec((tr, cg), lambda g, r: (nb - 1 - r, g)),
            pl.BlockSpec((4, None, cg // 4, cg), lambda g, r: (0, g, 0, 0)),
            pl.BlockSpec((1, cg), lambda g, r: (0, g)),
            pl.BlockSpec((1, cg), lambda g, r: (0, g)),
        ],
        [
            pl.BlockSpec((2, tr, cg), lambda g, r: (0, nb - 1 - r, g)),
            pl.BlockSpec((4, None, cg // 4, cg), lambda g, r: (0, g, 0, 0)),
            pl.BlockSpec((1, cg), lambda g, r: (0, g)),
            pl.BlockSpec((1, cg), lambda g, r: (0, g)),
        ],
        [
            jax.ShapeDtypeStruct((2, s, e), MXU),
            jax.ShapeDtypeStruct(w_g.shape, WIRE),
            jax.ShapeDtypeStruct((1, e), F32),
            jax.ShapeDtypeStruct((1, e), F32),
        ],
        [pltpu.VMEM((cg, cg), F32), pltpu.VMEM((MAXW, cg), F32)],
        [uz, uz, uz, dy2, w_g, bias, scale], hosted, ("parallel", "arbitrary"))


HALO = 16


def _one_minus_sq(log_a, a):
    poly = (-2.0 * log_a) * (1.0 + log_a * (1.0 + log_a * (2.0 / 3.0)))
    return jnp.where(log_a > -0.01, poly, 1.0 - a * a)


def _softplus_neg(lam):
    t = jnp.exp(-jnp.abs(lam))
    log1p = jnp.where(t < 1e-3, t * (1.0 - t * (0.5 - t * (1.0 / 3.0))), jnp.log(1.0 + t))
    return jnp.maximum(-lam, 0.0) + log1p, _sig(-lam)


def _lru_gates(u_ref, up_ref, rb, sm_ref, wa, wx):
    cur = u_ref[...].astype(F32)
    prev = jnp.where(rb > 0, up_ref[...].astype(F32), 0.0)
    ext = jnp.concatenate([prev, cur], axis=0)
    taps = [cur] + [pltpu.roll(ext, k, 0)[HALO:] for k in range(1, CONV_W)]
    uc = sm_ref[CONV_W:CONV_W + 1, :]
    for k in range(CONV_W):
        uc = uc + taps[k] * sm_ref[CONV_W - 1 - k:CONV_W - k, :]
    ucb = uc.astype(MXU)
    r = _sig(jnp.dot(ucb, wa, preferred_element_type=F32) + sm_ref[5:6, :])
    ig = _sig(jnp.dot(ucb, wx, preferred_element_type=F32) + sm_ref[6:7, :])
    sp, sgn = _softplus_neg(sm_ref[7:8, :])
    log_a = r * (-LRU_C * sp)
    a = jnp.exp(log_a)
    m2 = jnp.maximum(_one_minus_sq(log_a, a), 0.0)
    inv_mult = lax.rsqrt(jnp.maximum(m2, 1e-30))
    mult = m2 * inv_mult
    return taps, uc, ucb, r, ig, sp, sgn, a, mult, inv_mult


LANES = 128


def _seg_scan(a, b, out_ref, scr, state, reverse):
    a_s, b_s, h_s, p_s = scr
    tr, c = a.shape
    seg = tr // 8
    nl = c // LANES
    for l in range(nl):
        a_s[l] = a[:, l * LANES:(l + 1) * LANES]
        b_s[l] = b[:, l * LANES:(l + 1) * LANES]
    h = [jnp.zeros((8, LANES), F32)] * nl
    pp = [jnp.ones((8, LANES), F32)] * nl
    for i in (range(seg - 1, -1, -1) if reverse else range(seg)):
        rows = pl.ds(i, 8, stride=seg)
        for l in range(nl):
            av = a_s[l, rows, :]
            h[l] = av * h[l] + b_s[l, rows, :]
            pp[l] = av * pp[l]
            h_s[l, pl.ds(8 * i, 8), :] = h[l]
            p_s[l, pl.ds(8 * i, 8), :] = pp[l]
    leaving = []
    sub = lax.broadcasted_iota(jnp.int32, (8, LANES), 0)
    for l in range(nl):
        lanes = slice(l * LANES, (l + 1) * LANES)
        st = state[:, lanes]
        entering = jnp.zeros((8, LANES), F32)
        for sgm in (range(7, -1, -1) if reverse else range(8)):
            entering = jnp.where(sub == sgm, st, entering)
            st = h[l][sgm:sgm + 1, :] + pp[l][sgm:sgm + 1, :] * st
        leaving.append(st)
        for i in range(seg):
            rows = pl.ds(8 * i, 8)
            h_s[l, rows, :] = h_s[l, rows, :] + p_s[l, rows, :] * entering
        for sgm in range(8):
            for t0 in range(0, seg, 8):
                out_ref[pl.ds(sgm * seg + t0, 8), lanes] = h_s[l, pl.ds(8 * t0 + sgm, 8, stride=8), :]
    return jnp.concatenate(leaving, axis=1)


def lru_fwd(uz, wa_g, wx_g, small, tr, hosted=()):
    _, s, e = uz.shape
    cb = e // LRU_HEADS
    nb = s // tr
    hb = tr // HALO

    def body(u_ref, up_ref, z_ref, wa_ref, wx_ref, sm_ref, o_ref, h_ref, s0, s1, s2, s3, carry_ref):
        rb = pl.program_id(1)
        wa = wa_ref[...].reshape(cb, cb)
        wx = wx_ref[...].reshape(cb, cb)
        _, uc, _, _, ig, _, _, a, mult, _ = _lru_gates(u_ref, up_ref, rb, sm_ref, wa, wx)
        start = jnp.where(rb > 0, carry_ref[0:1, :], 0.0)
        last = _seg_scan(a, mult * ig * uc, h_ref, (s0, s1, s2, s3), start, False)
        carry_ref[...] = jnp.broadcast_to(last, carry_ref.shape)
        z = z_ref[...].astype(F32)
        o_ref[...] = (h_ref[...] * z * _sig(z)).astype(o_ref.dtype)

    wspec = pl.BlockSpec((4, None, cb // 4, cb), lambda h, r: (0, h, 0, 0))
    return _hosting_call(
        body, "lru_fwd", (LRU_HEADS, nb),
        [
            pl.BlockSpec((None, tr, cb), lambda h, r: (0, r, h)),
            pl.BlockSpec((None, HALO, cb), lambda h, r: (0, jnp.maximum(r * hb - 1, 0), h)),
            pl.BlockSpec((None, tr, cb), lambda h, r: (1, r, h)),
            wspec, wspec,
            pl.BlockSpec((8, cb), lambda h, r: (0, h)),
        ],
        [pl.BlockSpec((tr, cb), lambda h, r: (r, h)), pl.BlockSpec((tr, cb), lambda h, r: (r, h))],
        [jax.ShapeDtypeStruct((s, e), MXU), jax.ShapeDtypeStruct((s, e), F32)],
        [pltpu.VMEM((cb // LANES, tr, LANES), F32)] * 4 + [pltpu.VMEM((8, cb), F32)],
        [uz, uz, uz, wa_g, wx_g, small], hosted, ("parallel", "arbitrary"))


def lru_bwd(uz, hst, dy2, wa_g, wx_g, small, tr, hosted=()):
    _, s, e = uz.shape
    cb = e // LRU_HEADS
    nb = s // tr
    hb = tr // HALO

    def body(u_ref, up_ref, z_ref, h_ref, hp_ref, dy_ref, wa_ref, wx_ref, sm_ref,
             duz_ref, gwa_ref, gwx_ref, dsm_ref, s0, s1, s2, s3, g_s, acc_a, acc_x, gcar, acar, dcar):
        step = pl.program_id(1)
        rb = nb - 1 - step
        wa = wa_ref[...].reshape(cb, cb)
        wx = wx_ref[...].reshape(cb, cb)
        taps, uc, ucb, r, ig, sp, sgn, a, mult, inv_mult = _lru_gates(u_ref, up_ref, rb, sm_ref, wa, wx)
        row = lax.broadcasted_iota(jnp.int32, a.shape, 0)
        z = z_ref[...].astype(F32)
        sg = _sig(z)
        dy2v = dy_ref[...].astype(F32)
        hv = h_ref[...]
        duz_ref[1] = (dy2v * hv * sg * (1.0 + z * (1.0 - sg))).astype(duz_ref.dtype)
        a_next = jnp.where(row == tr - 1, jnp.where(step > 0, acar[0:1, :], 0.0), pltpu.roll(a, tr - 1, 0))
        g_first = _seg_scan(a_next, dy2v * z * sg, g_s, (s0, s1, s2, s3), jnp.where(step > 0, gcar[0:1, :], 0.0), True)
        gcar[...] = jnp.broadcast_to(g_first, gcar.shape)
        acar[...] = jnp.broadcast_to(a[0:1, :], acar.shape)
        gv = g_s[...]
        h_before = jnp.where(rb > 0, hp_ref[HALO - 1:HALO, :], 0.0)
        h_prev = jnp.where(row == 0, h_before, pltpu.roll(hv, 1, 0))
        da = gv * h_prev
        gu = gv * uc
        dmult = gu * ig
        dig = gu * mult
        dlog_a = da * a - dmult * jnp.where(mult > 0.0, a * a * inv_mult, 0.0)
        dra = dlog_a * (-LRU_C) * sp * r * (1.0 - r)
        dix = dig * ig * (1.0 - ig)
        dl = jnp.sum(dlog_a * r, axis=0, keepdims=True) * (LRU_C * sgn)
        dra_b, dix_b = dra.astype(MXU), dix.astype(MXU)
        duc = (gv * mult * ig + lax.dot_general(dra_b, wa, _DN["nt"], preferred_element_type=F32)
               + lax.dot_general(dix_b, wx, _DN["nt"], preferred_element_type=F32))
        gwa = lax.dot_general(ucb, dra_b, _DN["tn"], preferred_element_type=F32)
        gwx = lax.dot_general(ucb, dix_b, _DN["tn"], preferred_element_type=F32)
        ext = jnp.concatenate([duc, jnp.where(step > 0, dcar[...], 0.0)], axis=0)
        n = ext.shape[0]
        du = duc * sm_ref[CONV_W - 1:CONV_W, :]
        for k in range(1, CONV_W):
            du = du + pltpu.roll(ext, n - k, 0)[:tr] * sm_ref[CONV_W - 1 - k:CONV_W - k, :]
        duz_ref[0] = du.astype(duz_ref.dtype)
        dcar[...] = duc[:HALO]
        rows = [jnp.sum(duc * taps[CONV_W - 1 - k], axis=0, keepdims=True) for k in range(CONV_W)]
        rows += [jnp.sum(duc, axis=0, keepdims=True), jnp.sum(dra, axis=0, keepdims=True),
                 jnp.sum(dix, axis=0, keepdims=True), dl]

        @pl.when(step == 0)
        def _():
            acc_a[...] = gwa
            acc_x[...] = gwx
            for k, rv in enumerate(rows):
                dsm_ref[k:k + 1, :] = rv

        @pl.when(step > 0)
        def _():
            acc_a[...] += gwa
            acc_x[...] += gwx
            for k, rv in enumerate(rows):
                dsm_ref[k:k + 1, :] += rv

        @pl.when(step == nb - 1)
        def _():
            gwa_ref[...] = acc_a[...].reshape(4, cb // 4, cb).astype(gwa_ref.dtype)
            gwx_ref[...] = acc_x[...].reshape(4, cb // 4, cb).astype(gwx_ref.dtype)

    wspec = pl.BlockSpec((4, None, cb // 4, cb), lambda h, r: (0, h, 0, 0))
    blk = pl.BlockSpec((tr, cb), lambda h, r: (nb - 1 - r, h))
    return _hosting_call(
        body, "lru_bwd", (LRU_HEADS, nb),
        [
            pl.BlockSpec((None, tr, cb), lambda h, r: (0, nb - 1 - r, h)),
            pl.BlockSpec((None, HALO, cb), lambda h, r: (0, jnp.maximum((nb - 1 - r) * hb - 1, 0), h)),
            pl.BlockSpec((None, tr, cb), lambda h, r: (1, nb - 1 - r, h)),
            blk,
            pl.BlockSpec((HALO, cb), lambda h, r: (jnp.maximum((nb - 1 - r) * hb - 1, 0), h)),
            blk,
            wspec, wspec,
            pl.BlockSpec((8, cb), lambda h, r: (0, h)),
        ],
        [
            pl.BlockSpec((2, tr, cb), lambda h, r: (0, nb - 1 - r, h)),
            wspec, wspec,
            pl.BlockSpec((8, cb), lambda h, r: (0, h)),
        ],
        [
            jax.ShapeDtypeStruct((2, s, e), MXU),
            jax.ShapeDtypeStruct(wa_g.shape, WIRE),
            jax.ShapeDtypeStruct(wx_g.shape, WIRE),
            jax.ShapeDtypeStruct((8, e), F32),
        ],
        ([pltpu.VMEM((cb // LANES, tr, LANES), F32)] * 4 + [pltpu.VMEM((tr, cb), F32)]
         + [pltpu.VMEM((cb, cb), F32)] * 2 + [pltpu.VMEM((8, cb), F32)] * 2 + [pltpu.VMEM((HALO, cb), F32)]),
        [uz, uz, uz, hst, hst, dy2, wa_g, wx_g, small], hosted, ("parallel", "arbitrary"))


def _place():
    x, y, c = lax.axis_index("x"), lax.axis_index("y"), lax.axis_index("c")
    chips = [(1 - x, y), (x, 1 - y), (1 - x, 1 - y)]
    return x, y, c, chips


def _rcopy(src, dst, send_sems, recv_sems, k, to):
    return pltpu.make_async_remote_copy(src_ref=src, dst_ref=dst, send_sem=send_sems.at[k], recv_sem=recv_sems.at[k],
                                        device_id=to, device_id_type=MESHID)


def _stage_gather_ici(bufs):
    n = len(bufs)

    def quarter(ref, chip, c, q):
        rq = ref.shape[3] // 2
        return ref.at[2 * chip[0] + chip[1], :, c, pl.ds(q * rq, rq)]

    def copies(refs, ss, rs, off, a, sending):
        x, y, c, _ = _place()
        me, xn, yn, dg = (x, y), (1 - x, y), (x, 1 - y), (1 - x, 1 - y)
        plan = [(0, me, xn, 0), (1, me, xn, 1), (3, me, yn, 1), (2, me, yn, 0),
                (4, xn, yn, 0), (5, yn, xn, 1)]
        if not sending:
            plan = [(0, xn, xn, 0), (1, xn, xn, 1), (3, yn, yn, 1), (2, yn, yn, 0), (4, dg, yn, 0), (5, dg, xn, 1)]
        out = []
        for k, owner, to, q in plan:
            blk = quarter(refs[a], owner, c, q)
            out.append(_rcopy(blk, blk, ss, rs, off + 6 * a + k, (*to, c)))
        return out

    def start(ro, refs, ss, rs, off):
        for a in range(n):
            for cp in copies(refs, ss, rs, off, a, True)[:4]:
                cp.start()

    def mid(ro, refs, ss, rs, off):
        for a in range(n):
            got, out = copies(refs, ss, rs, off, a, False), copies(refs, ss, rs, off, a, True)
            got[0].wait_recv()
            out[4].start()
            got[2].wait_recv()
            out[5].start()

    def finish(ro, refs, ss, rs, off):
        for a in range(n):
            got = copies(refs, ss, rs, off, a, False)
            for k in (1, 3, 4, 5):
                got[k].wait_recv()
            for cp in copies(refs, ss, rs, off, a, True):
                cp.wait_send()

    return dict(ro=[], bufs=list(bufs), nsem=6 * n, start=start, mid=mid, finish=finish)


def _stage_gather_d2d(bufs):
    n = len(bufs)

    def copies(refs, ss, rs, off, sending):
        x, y, c, chips = _place()
        out = []
        for a in range(n):
            for jj, ch in enumerate(chips):
                blk = refs[a].at[2 * ch[0] + ch[1], :, c if sending else 1 - c]
                out.append(_rcopy(blk, blk, ss, rs, off + 3 * a + jj, (x, y, 1 - c)))
        return out

    def start(ro, refs, ss, rs, off):
        for cp in copies(refs, ss, rs, off, True):
            cp.start()

    def finish(ro, refs, ss, rs, off):
        for cp in copies(refs, ss, rs, off, False):
            cp.wait_recv()
        for cp in copies(refs, ss, rs, off, True):
            cp.wait_send()

    return dict(ro=[], bufs=list(bufs), nsem=3 * n, start=start, finish=finish)


def _stage_scatter_ici(parts, gots):
    n = len(parts)

    def copies(ro, refs, ss, rs, off):
        x, y, c, chips = _place()
        return [_rcopy(ro[a].at[2 * ch[0] + ch[1]], refs[a].at[jj], ss, rs, off + 3 * a + jj, (*ch, c))
                for a in range(n) for jj, ch in enumerate(chips)]

    def start(ro, refs, ss, rs, off):
        for cp in copies(ro, refs, ss, rs, off):
            cp.start()

    def finish(ro, refs, ss, rs, off):
        for cp in copies(ro, refs, ss, rs, off):
            cp.wait()

    return dict(ro=list(parts), bufs=list(gots), nsem=3 * n, start=start, finish=finish)


def _stage_send_half(grads, lands):
    n = len(grads)

    def copies(ro, refs, ss, rs, off):
        x, y, c, _ = _place()
        return [_rcopy(ro[a].at[:, :, 1 - c], refs[a], ss, rs, off + a, (x, y, 1 - c)) for a in range(n)]

    def start(ro, refs, ss, rs, off):
        for cp in copies(ro, refs, ss, rs, off):
            cp.start()

    def finish(ro, refs, ss, rs, off):
        for cp in copies(ro, refs, ss, rs, off):
            cp.wait()

    return dict(ro=list(grads), bufs=list(lands), nsem=n, start=start, finish=finish)


def cast_into_slab(w, first, l, k_idx):
    _, r, c = w.shape
    rh = r // 2
    lb, tr = _block_lr(l, rh, c)
    nbh = rh // tr
    assert first % lb == 0

    def body(k_ref, w_ref, o_ref):
        o_ref[...] = w_ref[...].astype(o_ref.dtype)

    return pl.pallas_call(
        body, name="cast_into_slab",
        grid_spec=pltpu.PrefetchScalarGridSpec(
            num_scalar_prefetch=1, grid=(l // lb, 2, nbh),
            in_specs=[pl.BlockSpec((lb, tr, c), lambda i, h, b, k_ref: (first // lb + i, h * nbh + b, 0))],
            out_specs=pl.BlockSpec((None, lb, None, tr, c), lambda i, h, b, k_ref: (k_ref[0], i, h, b, 0))),
        out_shape=jax.ShapeDtypeStruct((4, l, 2, rh, c), WIRE),
        compiler_params=_cp(("parallel", "parallel", "parallel")),
    )(k_idx, w)


def gather_weights(bufs):
    n = len(bufs)
    ici = [_stage_gather_ici([b]) for b in bufs]
    d2d = [_stage_gather_d2d([b]) for b in bufs]
    per = ici[0]["nsem"] + d2d[0]["nsem"]

    def body(*refs):
        outs = refs[n:2 * n]
        ss, rs = refs[2 * n:]
        for what in ("start", "mid"):
            for a in range(n):
                ici[a][what]([], [outs[a]], ss, rs, per * a)
        for a in range(n):
            ici[a]["finish"]([], [outs[a]], ss, rs, per * a)
            d2d[a]["start"]([], [outs[a]], ss, rs, per * a + ici[a]["nsem"])
        for a in range(n):
            d2d[a]["finish"]([], [outs[a]], ss, rs, per * a + ici[a]["nsem"])

    return pl.pallas_call(
        body, name="gather_weights",
        in_specs=[ANY] * n, out_specs=[ANY] * n,
        out_shape=[jax.ShapeDtypeStruct(a.shape, a.dtype) for a in bufs],
        scratch_shapes=[pltpu.SemaphoreType.DMA((per * n,)), pltpu.SemaphoreType.DMA((per * n,))],
        input_output_aliases={a: a for a in range(n)},
        compiler_params=pltpu.CompilerParams(has_side_effects=True),
    )(*bufs)


def all_gather_small(v, name):
    m_per, n = v.shape

    def body(x_ref, out_ref, send_sems, recv_sems, local_sem):
        x, y, c, chips = _place()
        me, sibling = (x, y, c), (x, y, 1 - c)

        def rows(px, py, pc):
            return out_ref.at[pl.ds((4 * px + 2 * py + pc) * m_per, m_per), :]

        def copy(k, block, to, src=None):
            return _rcopy(rows(*block) if src is None else src, rows(*block), send_sems, recv_sems, k, to)

        mine = pltpu.make_async_copy(x_ref, rows(*me), local_sem)
        mine.start()
        first = [copy(0, me, sibling, src=x_ref)]
        first += [copy(1 + jj, me, (*chip, c), src=x_ref) for jj, chip in enumerate(chips)]
        for cp in first:
            cp.start()
        passed = [copy(4 + jj, (*chip, c), sibling) for jj, chip in enumerate(chips)]
        for jj, chip in enumerate(chips):
            copy(1 + jj, (*chip, c), me).wait_recv()
            passed[jj].start()
        copy(0, sibling, me).wait_recv()
        for jj, chip in enumerate(chips):
            copy(4 + jj, (*chip, 1 - c), me).wait_recv()
        for cp in first + passed:
            cp.wait_send()
        mine.wait()

    return pl.pallas_call(
        body, name=name,
        out_shape=jax.ShapeDtypeStruct((8 * m_per, n), v.dtype),
        in_specs=[pl.BlockSpec(memory_space=pltpu.VMEM)],
        out_specs=pl.BlockSpec(memory_space=pltpu.VMEM),
        scratch_shapes=[pltpu.SemaphoreType.DMA((7,)), pltpu.SemaphoreType.DMA((7,)), pltpu.SemaphoreType.DMA],
        compiler_params=pltpu.CompilerParams(vmem_limit_bytes=VMEM_LIMIT),
    )(v)


def sum_devices(g):
    def body(g_ref, o_ref):
        acc = g_ref[0]
        for d in range(1, 8):
            acc = acc + g_ref[d]
        o_ref[...] = acc

    return pl.pallas_call(body, name="sum_devices", out_shape=jax.ShapeDtypeStruct(g.shape[1:], g.dtype),
                          compiler_params=pltpu.CompilerParams(vmem_limit_bytes=VMEM_LIMIT))(g)


def scatter_to_chips(parts, fulls, spans):
    n, m = len(parts), len(fulls)

    def body(*refs):
        ins, outs, shared = refs[:n], refs[n + m:2 * n + m], refs[2 * n + m:2 * n + 2 * m]
        send_sems, recv_sems = refs[2 * n + 2 * m:]
        x, y, c, chips = _place()
        sib = (x, y, 1 - c)
        cps = []
        for a in range(n):
            for jj, ch in enumerate(chips):
                cps.append(_rcopy(ins[a].at[2 * ch[0] + ch[1]], outs[a].at[jj], send_sems, recv_sems, 3 * a + jj, (*ch, c)))

        def blk(a, half):
            return shared[a].at[pl.ds(spans[a][0], spans[a][1]), half]

        swaps = [_rcopy(blk(a, c), blk(a, c), send_sems, recv_sems, 3 * n + a, sib) for a in range(m)]
        for cp in cps + swaps:
            cp.start()
        for cp in cps:
            cp.wait()
        for a in range(m):
            _rcopy(blk(a, 1 - c), blk(a, 1 - c), send_sems, recv_sems, 3 * n + a, sib).wait_recv()
        for cp in swaps:
            cp.wait_send()

    res = pl.pallas_call(
        body, name="scatter_to_chips", in_specs=[ANY] * (n + m), out_specs=[ANY] * (n + m),
        out_shape=[jax.ShapeDtypeStruct((3,) + p.shape[1:], p.dtype) for p in parts]
        + [jax.ShapeDtypeStruct(f.shape, f.dtype) for f in fulls],
        scratch_shapes=[pltpu.SemaphoreType.DMA((3 * n + m,)), pltpu.SemaphoreType.DMA((3 * n + m,))],
        input_output_aliases={n + a: n + a for a in range(m)},
        compiler_params=pltpu.CompilerParams(has_side_effects=True),
    )(*parts, *fulls)
    return list(res[:n]), list(res[n:])


def share_halves(bufs, spans):
    n = len(bufs)

    def body(*refs):
        outs = refs[n:2 * n]
        send_sems, recv_sems = refs[2 * n:]
        x, y, c, _ = _place()
        sib = (x, y, 1 - c)

        def blk(a, half):
            return outs[a].at[pl.ds(spans[a][0], spans[a][1]), half]

        cps = [_rcopy(blk(a, c), blk(a, c), send_sems, recv_sems, a, sib) for a in range(n)]
        for cp in cps:
            cp.start()
        for a in range(n):
            _rcopy(blk(a, 1 - c), blk(a, 1 - c), send_sems, recv_sems, a, sib).wait_recv()
        for cp in cps:
            cp.wait_send()

    return pl.pallas_call(
        body, name="share_halves", in_specs=[ANY] * n, out_specs=[ANY] * n,
        out_shape=[jax.ShapeDtypeStruct(b.shape, b.dtype) for b in bufs],
        scratch_shapes=[pltpu.SemaphoreType.DMA((n,)), pltpu.SemaphoreType.DMA((n,))],
        input_output_aliases={a: a for a in range(n)},
        compiler_params=pltpu.CompilerParams(has_side_effects=True),
    )(*bufs)


def _block_rows(r, c, itemsize, budget=1 << 20):
    tr = r
    while tr * c * itemsize > budget and tr % 16 == 0:
        tr //= 2
    return tr


def _block_lr(l, r, c, budget=4 << 20):
    tr = _block_rows(r, c, 4, budget)
    lb = 1
    if tr == r:
        while l % (2 * lb) == 0 and 2 * lb * r * c * 4 <= budget:
            lb *= 2
    return lb, tr


def add_halves(g, got, c_idx):
    k4, l, _, rh, cc = g.shape
    lb, tr = _block_lr(l, rh, cc)

    def body(c_ref, g_ref, r_ref, o_ref):
        o_ref[...] = (g_ref[...].astype(F32) + r_ref[...].astype(F32)).astype(o_ref.dtype)

    return pl.pallas_call(
        body, name="add_halves",
        grid_spec=pltpu.PrefetchScalarGridSpec(
            num_scalar_prefetch=1, grid=(k4, l // lb, rh // tr),
            in_specs=[pl.BlockSpec((None, lb, None, tr, cc), lambda k, i, b, c_ref: (k, i, c_ref[0], b, 0)),
                      pl.BlockSpec((None, lb, tr, cc), lambda k, i, b, c_ref: (k, i, b, 0))],
            out_specs=pl.BlockSpec((None, lb, tr, cc), lambda k, i, b, c_ref: (k, i, b, 0))),
        out_shape=jax.ShapeDtypeStruct(got.shape, WIRE),
        compiler_params=_cp(("parallel", "parallel", "parallel")),
    )(c_idx, g, got)


def sum_chips(own, got, kc_idx, full, first):
    _, l, rh, cc = own.shape
    lb, tr = _block_lr(l, rh, cc, 2 << 20)
    assert first % lb == 0

    def body(k_ref, o_ref, r_ref, _full, s_ref):
        s_ref[...] = ((o_ref[...].astype(F32) + r_ref[0].astype(F32)) + r_ref[1].astype(F32)) + r_ref[2].astype(F32)

    return pl.pallas_call(
        body, name="sum_chips",
        grid_spec=pltpu.PrefetchScalarGridSpec(
            num_scalar_prefetch=1, grid=(l // lb, rh // tr),
            in_specs=[pl.BlockSpec((None, lb, tr, cc), lambda i, b, k_ref: (k_ref[0], i, b, 0)),
                      pl.BlockSpec((3, lb, tr, cc), lambda i, b, k_ref: (0, i, b, 0)),
                      ANY],
            out_specs=pl.BlockSpec((lb, None, tr, cc), lambda i, b, k_ref: (first // lb + i, k_ref[1], b, 0))),
        out_shape=jax.ShapeDtypeStruct(full.shape, F32),
        input_output_aliases={3: 0},
        compiler_params=_cp(("parallel", "parallel")),
    )(kc_idx, own, got, full)


def _adam_math(w, g, m, v):
    m = ADAM_B1 * m + (1.0 - ADAM_B1) * g
    v = ADAM_B2 * v + (1.0 - ADAM_B2) * (g * g)
    m_hat = m / (1.0 - ADAM_B1 ** ADAM_STEP)
    v_hat = v / (1.0 - ADAM_B2 ** ADAM_STEP)
    delta = -ADAM_LR * (m_hat / (jnp.sqrt(v_hat) + ADAM_EPS) + ADAM_WD * w)
    return delta, m, v


def adamw(w, g, g_first, m, v, lo, hi, prev=None, hosted=()):
    l, r, c = w.shape
    tr = _block_rows(r, c, 4, 2 << 20)
    prev = list(prev or [])

    def body(w_ref, g_ref, m_ref, v_ref, *rest):
        go_ref, d_ref, mo_ref, vo_ref = rest[len(prev):]
        gv = g_ref[...]
        go_ref[...] = gv
        d_ref[...], mo_ref[...], vo_ref[...] = _adam_math(w_ref[...], gv, m_ref[...], v_ref[...])

    spec = pl.BlockSpec((None, tr, c), lambda i, b: (lo + i, b, 0))
    gspec = pl.BlockSpec((None, tr, c), lambda i, b: (g_first + i, b, 0))
    return _hosting_call(
        body, "adamw", (hi - lo, r // tr), [spec, gspec, spec, spec] + [ANY] * len(prev), [spec] * 4,
        [jax.ShapeDtypeStruct(w.shape, F32)] * 4, [], [w, g, m, v] + prev, hosted, ("parallel", "parallel"),
        aliases={4 + t: t for t in range(len(prev))})


def adamw_small(w, g, m, v):
    def body(w_ref, g_ref, m_ref, v_ref, d_ref, mo_ref, vo_ref):
        d_ref[...], mo_ref[...], vo_ref[...] = _adam_math(w_ref[...], g_ref[...], m_ref[...], v_ref[...])

    return pl.pallas_call(body, name="adamw_small", out_shape=[jax.ShapeDtypeStruct(w.shape, F32)] * 3)(w, g, m, v)


def kernel(x, p, w_in, w_out, g_pre, g_post, pool_w, pool_b, pool_scale, conv_w, conv_b, lru_wa, lru_ba, lru_wx, lru_bx, lru_L, w_ple, w_ple_gate, g_ple_in, g_ple_out, loss_target, m_w_in, m_w_out, m_g_pre, m_g_post, m_pool_w, m_pool_b, m_pool_scale, m_conv_w, m_conv_b, m_lru_wa, m_lru_ba, m_lru_wx, m_lru_bx, m_lru_L, m_w_ple, m_w_ple_gate, m_g_ple_in, m_g_ple_out, v_w_in, v_w_out, v_g_pre, v_g_post, v_pool_w, v_pool_b, v_pool_scale, v_conv_w, v_conv_b, v_lru_wa, v_lru_ba, v_lru_wx, v_lru_bx, v_lru_L, v_w_ple, v_w_ple_gate, v_g_ple_in, v_g_ple_out):
    depth = w_in.shape[0]
    _, s, d = x.shape
    e = 2 * d
    kp = p.shape[-1]
    nmix = pool_w.shape[0]
    ngrp = pool_w.shape[1]
    cg = e // ngrp
    cb = e // LRU_HEADS
    xi, yi, ci = lax.axis_index("x"), lax.axis_index("y"), lax.axis_index("c")
    me = 2 * xi + yi
    c_idx = jnp.reshape(ci, (1,)).astype(jnp.int32)
    k_idx = jnp.reshape(me, (1,)).astype(jnp.int32)
    tr_row = _tile(s, 512)
    tr_mix = _tile(s, 512)
    tr_lru = _tile(s, 1024)
    tm = _tile(s, 1024)
    tm2 = _tile(s, 2048)

    def halves(a):
        return a.reshape(a.shape[0], 2, a.shape[1] // 2, a.shape[2])

    big = {
        "w_in": w_in, "w_out": w_out, "gate": w_ple_gate, "ple": w_ple,
        "pool": pool_w.reshape(nmix * ngrp, cg // 4, cg),
        "wa": lru_wa.reshape(nmix * LRU_HEADS, cb // 4, cb), "wx": lru_wx.reshape(nmix * LRU_HEADS, cb // 4, cb),
    }
    names = list(big)

    def layer_shards(i):
        sh = {n: (big[n], i, 1) for n in ("w_in", "w_out", "gate", "ple")}
        if i % 2 == 0:
            sh["pool"] = (big["pool"], (i // 2) * ngrp, ngrp)
        else:
            sh["wa"], sh["wx"] = (big["wa"], (i // 2) * LRU_HEADS, LRU_HEADS), (big["wx"], (i // 2) * LRU_HEADS, LRU_HEADS)
        return sh

    def mixer_names(i):
        return ["pool"] if i % 2 == 0 else ["wa", "wx"]

    wbuf = [{n: cast_into_slab(*wfl, k_idx) for n, wfl in layer_shards(i).items()} for i in range(depth)]
    first = list(wbuf[0])
    wbuf[0] = dict(zip(first, gather_weights([wbuf[0][n] for n in first])))

    def full_w(i, n):
        b = wbuf[i][n]
        return b.reshape(b.shape[0], b.shape[1], 2 * b.shape[3], b.shape[4])

    def stages_of(specs):
        return [mk([wbuf[l][n] for n in nms]) for mk, l, nms in specs]

    def keep(specs, new):
        for (l, n), b in zip([(l, n) for _, l, nms in specs for n in nms], new):
            wbuf[l][n] = b

    def run_mm(specs, *args):
        if not specs:
            return _matmul(*args)
        out, new = _matmul(*args, hosted=stages_of(specs))
        keep(specs, new)
        return out

    ec = e // 4
    small_loc = jnp.concatenate([conv_w, conv_b[:, None], lru_ba[:, None], lru_bx[:, None], lru_L[:, None]], axis=1)
    sm_all = all_gather_small(small_loc.reshape(nmix * 8, ec), "gather_small").reshape(4, 2, nmix, 8, ec)
    lru_small = jnp.transpose(sm_all[:, 0], (1, 2, 0, 3)).reshape(nmix, 8, e)

    xs = x[0]
    saved = []
    for i in range(depth):
        j = i // 2
        h = rms_fwd(xs, g_pre[i][None], tr_row) if i == 0 else h_next
        nj = (2 * e) // 1024 if (2 * e) % 1024 == 0 else 4
        tn = (2 * e) // nj
        per = e // tn
        perk = (e // 2) // tn
        nxt = i + 1 if i + 1 < depth else None
        stages = [(_stage_gather_d2d, i, ["gate", "ple"])] if i > 0 else []
        if nxt is not None:
            stages.append((_stage_gather_ici, nxt, ["w_in"]))
        uz = run_mm(
            stages, "mm_in", "nn", (s // tm2, nj, 1), h, pl.BlockSpec((tm2, d), lambda a, b, k: (a, 0)),
            full_w(i, "w_in"), pl.BlockSpec((None, None, d, tn), lambda a, b, k, perk=perk: (b // perk, 0, 0, b % perk)),
            jax.ShapeDtypeStruct((2, s, e), MXU), pl.BlockSpec((None, tm2, tn), lambda a, b, k, per=per: (b // per, a, b % per)),
            (8, 128))
        specs = []
        if nxt is not None:
            specs = [(_stage_gather_ici, nxt, mixer_names(nxt) + (["w_out"] if i % 2 else []))]
        if i % 2 == 0:
            (y2,), new = pool_fwd(uz, full_w(i, "pool"), pool_b[j][None], pool_scale[j][None], tr_mix, hosted=stages_of(specs))
            hst = None
        else:
            (y2, hst), new = lru_fwd(uz, full_w(i, "wa"), full_w(i, "wx"), lru_small[j], tr_lru, hosted=stages_of(specs))
        keep(specs, new)
        tn_o = _tile(d, 1024)
        stages = []
        if nxt is not None:
            stages = [(_stage_gather_ici, nxt, ["w_out"] if i % 2 == 0 else ["gate", "ple"])]
        o = run_mm(
            stages, "mm_out", "nn", (s // tm, d // tn_o, 1), y2, pl.BlockSpec((tm, e), lambda a, b, k: (a, 0)),
            full_w(i, "w_out"), pl.BlockSpec((4, None, e // 4, tn_o), lambda a, b, k: (0, 0, 0, b)),
            jax.ShapeDtypeStruct((s, d), MXU), pl.BlockSpec((tm, tn_o), lambda a, b, k: (a, b)), (8, 128))
        x1, hn = res_rms_fwd(xs, o, g_post[i][None], g_ple_in[i][None], tr_row)
        stages = []
        if nxt is not None:
            stages = [(_stage_gather_d2d, nxt, ["w_out"] + mixer_names(nxt))]
            if i % 2 == 0:
                stages.insert(0, (_stage_gather_ici, nxt, ["gate", "ple"]))
        gpre = run_mm(
            stages, "mm_gate", "nn", (s // tm2, d // tn_o, 1), hn, pl.BlockSpec((tm2, d), lambda a, b, k: (a, 0)),
            full_w(i, "gate"), pl.BlockSpec((4, None, d // 4, tn_o), lambda a, b, k: (0, 0, 0, b)),
            jax.ShapeDtypeStruct((s, d), MXU), pl.BlockSpec((tm2, tn_o), lambda a, b, k: (a, b)), (8, 128))
        pe = p[i, 0]
        ev = run_mm(
            [(_stage_gather_d2d, nxt, ["w_in"])] if nxt is not None else [],
            "mm_ple", "nn", (s // tm, 4, 1), pe, pl.BlockSpec((tm, kp), lambda a, b, k: (a, 0)),
            full_w(i, "ple"), pl.BlockSpec((None, None, kp, d // 4), lambda a, b, k: (b, 0, 0, 0)),
            jax.ShapeDtypeStruct((s, d), MXU), pl.BlockSpec((tm, d // 4), lambda a, b, k: (a, b)), (8, 128))
        if nxt is not None:
            x2, h_next = ple_fwd(x1, gpre, ev, g_ple_out[i][None], g_pre[nxt][None], tr_row)
        else:
            x2 = None
            dx, sq = ple_fwd_loss(x1, gpre, ev, g_ple_out[i][None], loss_target[0], tr_row)
        saved.append((xs, h, uz, y2, hst, o, x1, hn, gpre, ev))
        xs = x2

    d_gpre, d_gpost, d_gin, d_gout = [None] * depth, [None] * depth, [None] * depth, [None] * depth
    d_pool_b, d_pool_sc, d_lru_small = [None] * nmix, [None] * nmix, [None] * nmix
    ts = _tile(s, 1024)
    kc_idx = jnp.stack([me, ci]).astype(jnp.int32)
    full = {n: lax.empty(halves(big[n]).shape, F32) for n in names}
    w_sum = w_got = None

    def first_row(i, n):
        return i if n in ("w_in", "w_out", "gate", "ple") else (i // 2) * (ngrp if n == "pool" else LRU_HEADS)

    for i in reversed(range(depth)):
        j = i // 2
        prev = i + 1 if i + 1 < depth else None
        x0, h, uz, y2, hst, o, x1, hn, gpre, ev = saved[i]
        pe = p[i, 0]
        gl = {}
        de, dgp, d_gout[i] = ple_bwd(dx, gpre, ev, g_ple_out[i][None], tr_row)
        gl["ple"] = _matmul(
            "mm_dple", "tn", (1, 4, s // ts), pe, pl.BlockSpec((ts, kp), lambda a, b, k: (k, 0)),
            de, pl.BlockSpec((ts, d // 4), lambda a, b, k: (k, b)),
            jax.ShapeDtypeStruct((4, 1, kp, d // 4), WIRE), pl.BlockSpec((None, None, kp, d // 4), lambda a, b, k: (b, 0, 0, 0)),
            (kp, d // 4))
        tn_o = _tile(d, 1024)
        gl["gate"] = _matmul(
            "mm_dgate", "tn", (4, d // tn_o, 1), hn, pl.BlockSpec((s, d // 4), lambda a, b, k: (0, a)),
            dgp, pl.BlockSpec((s, tn_o), lambda a, b, k: (0, b)),
            jax.ShapeDtypeStruct((4, 1, d // 4, d), WIRE), pl.BlockSpec((None, None, d // 4, tn_o), lambda a, b, k: (a, 0, 0, b)),
            (8, 128))
        dhn = _matmul(
            "mm_dhn", "nt", (s // tm2, 4, 1), dgp, pl.BlockSpec((tm2, d), lambda a, b, k: (a, 0)),
            full_w(i, "gate"), pl.BlockSpec((None, None, d // 4, d), lambda a, b, k: (b, 0, 0, 0)),
            jax.ShapeDtypeStruct((s, d), MXU), pl.BlockSpec((tm2, d // 4), lambda a, b, k: (a, b)), (8, 128))
        dx1, do, d_gin[i], d_gpost[i] = post_bwd(dx, dhn, x1, g_ple_in[i][None], o, g_post[i][None], tr_row)
        gl["w_out"] = _matmul(
            "mm_dwout", "tn", (4, d // tn_o, 1), y2, pl.BlockSpec((s, e // 4), lambda a, b, k: (0, a)),
            do, pl.BlockSpec((s, tn_o), lambda a, b, k: (0, b)),
            jax.ShapeDtypeStruct((4, 1, e // 4, d), WIRE), pl.BlockSpec((None, None, e // 4, tn_o), lambda a, b, k: (a, 0, 0, b)),
            (8, 128))
        dy2 = _matmul(
            "mm_dy2", "nt", (s // tm2, 4, 1), do, pl.BlockSpec((tm2, d), lambda a, b, k: (a, 0)),
            full_w(i, "w_out"), pl.BlockSpec((None, None, e // 4, d), lambda a, b, k: (b, 0, 0, 0)),
            jax.ShapeDtypeStruct((s, e), MXU), pl.BlockSpec((tm2, e // 4), lambda a, b, k: (a, b)), (8, 128))
        riding = [_stage_scatter_ici([w_sum], [w_got])] if prev is not None else []
        if i % 2 == 0:
            (duz, gl["pool"], d_pool_b[j], d_pool_sc[j]), passed = pool_bwd(
                uz, dy2, full_w(i, "pool"), pool_b[j][None], pool_scale[j][None], tr_mix, hosted=riding)
        else:
            (duz, gl["wa"], gl["wx"], d_lru_small[j]), passed = lru_bwd(
                uz, hst, dy2, full_w(i, "wa"), full_w(i, "wx"), lru_small[j], tr_lru, hosted=riding)
        if prev is not None:
            full["w_in"] = sum_chips(w_sum, passed[0], kc_idx, full["w_in"], first_row(prev, "w_in"))

        def in_halves(g):
            return g.reshape(4, g.shape[1], 2, g.shape[2] // 2, g.shape[3])

        early = list(gl)
        eparts = [in_halves(gl[n]) for n in early]
        tmi = _tile(d, 1024)
        tni = _tile(e // 2, 1024)
        nslab = (e // 2) // tni
        gl["w_in"], from_sib = _matmul(
            "mm_dwin", "tn", (d // tmi, 4 * nslab, 1), h, pl.BlockSpec((s, tmi), lambda a, b, k: (0, a)),
            duz, pl.BlockSpec((None, s, tni), lambda a, b, k, nslab=nslab: (b // (2 * nslab), 0, b % (2 * nslab))),
            jax.ShapeDtypeStruct((4, 1, d, e // 2), WIRE),
            pl.BlockSpec((None, None, tmi, tni), lambda a, b, k, nslab=nslab: (b // nslab, 0, a, b % nslab)),
            (8, 128), hosted=[_stage_send_half(eparts, [lax.empty(g.shape[:2] + g.shape[3:], WIRE) for g in eparts])])
        esums = [add_halves(g, r, c_idx) for g, r in zip(eparts, from_sib)]
        wpart = in_halves(gl["w_in"])
        tnd = _tile(d, 1024)
        dh, passed = _matmul(
            "mm_dh", "nt", (s // tm2, d // tnd, 4), duz, pl.BlockSpec((None, tm2, e // 2), lambda a, b, k: (k // 2, a, k % 2)),
            full_w(i, "w_in"), pl.BlockSpec((None, None, tnd, e // 2), lambda a, b, k: (k, 0, b, 0)),
            jax.ShapeDtypeStruct((s, d), MXU), pl.BlockSpec((tm2, tnd), lambda a, b, k: (a, b)), (tm2, tnd),
            hosted=[_stage_scatter_ici(esums, [lax.empty((3,) + q.shape[1:], WIRE) for q in esums]),
                    _stage_send_half([wpart], [lax.empty(wpart.shape[:2] + wpart.shape[3:], WIRE)])])
        for n, q, got in zip(early, esums, passed[:len(early)]):
            full[n] = sum_chips(q, got, kc_idx, full[n], first_row(i, n))
        w_sum = add_halves(wpart, passed[len(early)], c_idx)
        w_got = lax.empty((3,) + w_sum.shape[1:], WIRE)
        dx, d_gpre[i] = rms_bwd_res(dx1, dh, x0, g_pre[i][None], tr_row)
    grad_x = dx[None]

    got, swapped = scatter_to_chips([w_sum], [full[n] for n in names],
                                    [(1, depth - 1) if n == "w_in" else (0, full[n].shape[0]) for n in names])
    full = dict(zip(names, swapped))
    full["w_in"] = share_halves([sum_chips(w_sum, got[0], kc_idx, full["w_in"], 0)], [(0, 1)])[0]
    grads = {n: full[n].reshape(big[n].shape) for n in names}

    def rows_e(a):
        return jnp.stack(a).reshape(-1, e) if isinstance(a, list) else a.reshape(-1, e)

    pack = [rows_e([g[0] for g in d_gpre]), rows_e([g[0] for g in d_gpost]), rows_e([g[0] for g in d_gin]), rows_e([g[0] for g in d_gout]),
            jnp.concatenate(d_pool_b, axis=0), jnp.concatenate(d_pool_sc, axis=0), jnp.concatenate(d_lru_small, axis=0),
            jnp.pad(sq, ((0, 0), (0, e - d)))]
    sizes = [a.shape[0] for a in pack]
    packed = jnp.concatenate(pack, axis=0)
    nrow = packed.shape[0]
    nrow_p = -(-nrow // 8) * 8
    packed = jnp.pad(packed, ((0, nrow_p - nrow), (0, 0)))
    total = sum_devices(all_gather_small(packed, "gather_grads").reshape(8, nrow_p, e))
    parts, off = [], 0
    for n_ in sizes:
        parts.append(total[off:off + n_])
        off += n_
    t_gpre, t_gpost, t_gin, t_gout, t_pb, t_psc, t_lru, t_sq = parts
    loss = 0.5 * jnp.sum(t_sq) / d
    t_lru = t_lru.reshape(nmix, 8, e)
    t_lru_loc = lax.dynamic_slice_in_dim(t_lru, me * ec, ec, axis=2)

    def big_update(name, w, m, v):
        shp = big[name].shape
        res4, _ = adamw(w.reshape(shp), grads[name], 0, m.reshape(shp), v.reshape(shp), 0, shp[0])
        return [a.reshape(w.shape) for a in res4]

    def small_update(w, g, m, v):
        shp = w.shape
        w2 = w.reshape(-1, shp[-1])
        dl, nm, nv = adamw_small(w2, g.reshape(w2.shape), m.reshape(w2.shape), v.reshape(w2.shape))
        return [g.reshape(shp), dl.reshape(shp), nm.reshape(shp), nv.reshape(shp)]

    res = {
        "w_in": big_update("w_in", w_in, m_w_in, v_w_in),
        "w_out": big_update("w_out", w_out, m_w_out, v_w_out),
        "g_pre": small_update(g_pre, t_gpre.reshape(depth, d), m_g_pre, v_g_pre),
        "g_post": small_update(g_post, t_gpost.reshape(depth, d), m_g_post, v_g_post),
        "pool_w": big_update("pool", pool_w, m_pool_w, v_pool_w),
        "pool_b": small_update(pool_b, t_pb, m_pool_b, v_pool_b),
        "pool_scale": small_update(pool_scale, t_psc, m_pool_scale, v_pool_scale),
        "conv_w": small_update(conv_w, t_lru_loc[:, :CONV_W], m_conv_w, v_conv_w),
        "conv_b": small_update(conv_b, t_lru_loc[:, 4], m_conv_b, v_conv_b),
        "lru_wa": big_update("wa", lru_wa, m_lru_wa, v_lru_wa),
        "lru_ba": small_update(lru_ba, t_lru_loc[:, 5], m_lru_ba, v_lru_ba),
        "lru_wx": big_update("wx", lru_wx, m_lru_wx, v_lru_wx),
        "lru_bx": small_update(lru_bx, t_lru_loc[:, 6], m_lru_bx, v_lru_bx),
        "lru_L": small_update(lru_L, t_lru_loc[:, 7], m_lru_L, v_lru_L),
        "w_ple": big_update("ple", w_ple, m_w_ple, v_w_ple),
        "w_ple_gate": big_update("gate", w_ple_gate, m_w_ple_gate, v_w_ple_gate),
        "g_ple_in": small_update(g_ple_in, t_gin.reshape(depth, d), m_g_ple_in, v_g_ple_in),
        "g_ple_out": small_update(g_ple_out, t_gout.reshape(depth, d), m_g_ple_out, v_g_ple_out),
    }
    order = ["w_in", "w_out", "g_pre", "g_post", "pool_w", "pool_b", "pool_scale", "conv_w", "conv_b", "lru_wa", "lru_ba",
             "lru_wx", "lru_bx", "lru_L", "w_ple", "w_ple_gate", "g_ple_in", "g_ple_out"]
    out = [loss, grad_x]
    for slot in range(4):
        out += [res[n][slot] for n in order]
    return tuple(out)
```
